```python
import jax, jax.numpy as jnp
from jax import lax
import numpy as np

D_MODEL = 1024
BATCH = 8
SEQ = 8192
DEPTH = 1

D_FF = 2816
D_MIX = D_MODEL
N_HEADS = 8
HEAD_DIM = 64
N_KV_HEADS = 2
KV_GROUP = N_HEADS // N_KV_HEADS
D_ATTN = N_HEADS * HEAD_DIM
D_KV = N_KV_HEADS * HEAD_DIM
Q_BLOCK = 128
ROPE_THETA = 10000.0
AXIS_DIM = HEAD_DIM // 2
GRID_W = 64
N_SGU_GROUPS = 8
SGU_GROUP_DIM = 64
D_SGU = N_SGU_GROUPS * SGU_GROUP_DIM
CHUNK = 128
D_IN = D_ATTN + 2 * D_KV + 2 * D_SGU
EPS = 1e-6

kernel_name = "hybrid_macaron_gmlp_gqa_axial_encoder_block"


def rms_norm(x, g):
    xf = x.astype(jnp.float32)
    y = xf * lax.rsqrt(jnp.mean(xf * xf, axis=-1, keepdims=True) + EPS)
    return (y * g.astype(jnp.float32)).astype(x.dtype)


def swiglu(h, w_gate, w_up, w_down):
    return (jax.nn.silu(h @ w_gate) * (h @ w_up)) @ w_down


def axial_rope_tables(rows):
    row_idx = jnp.repeat(jnp.arange(rows, dtype=jnp.float32), GRID_W)
    col_idx = jnp.tile(jnp.arange(GRID_W, dtype=jnp.float32), rows)
    inv = 1.0 / (ROPE_THETA ** (jnp.arange(0, AXIS_DIM, 2, dtype=jnp.float32) / AXIS_DIM))
    ang = jnp.concatenate([row_idx[:, None] * inv, col_idx[:, None] * inv], axis=-1)
    return jnp.cos(ang), jnp.sin(ang)


def apply_rope(x, cos, sin):
    b, s, h, d = x.shape
    xf = x.astype(jnp.float32).reshape(b, s, h, d // 2, 2)
    x1, x2 = xf[..., 0], xf[..., 1]
    c = cos[None, :, None, :]
    sn = sin[None, :, None, :]
    out = jnp.stack([x1 * c - x2 * sn, x1 * sn + x2 * c], axis=-1)
    return out.reshape(b, s, h, d).astype(x.dtype)


def gqa_attention(q, k, v):
    b, s, _, d = q.shape
    nblk = s // Q_BLOCK
    scale = HEAD_DIM ** -0.5
    qb = q.reshape(b, nblk, Q_BLOCK, N_KV_HEADS, KV_GROUP, d).transpose(1, 0, 2, 3, 4, 5)

    def one_block(qi):
        sc = jnp.einsum('bqkgd,bskd->bkgqs', qi, k, preferred_element_type=jnp.float32) * scale
        p = jax.nn.softmax(sc, axis=-1)
        return jnp.einsum('bkgqs,bskd->bqkgd', p.astype(v.dtype), v)

    o = lax.map(one_block, qb)
    return o.transpose(1, 0, 2, 3, 4, 5).reshape(b, s, N_HEADS * d)


def spatial_gating(z, g_sgu, w_s, b_s):
    b, s, _ = z.shape
    u, vv = jnp.split(z, 2, axis=-1)
    vv = rms_norm(vv, g_sgu)
    vv = vv.reshape(b, s // CHUNK, CHUNK, N_SGU_GROUPS, SGU_GROUP_DIM)
    f = jnp.einsum('gpq,bnqgd->bnpgd', w_s, vv) + b_s.T[None, None, :, :, None]
    return u * f.reshape(b, s, D_SGU)


def _fwd_setup_inputs(seed: int = 0) -> dict:
    key = jax.random.key(seed)
    ks = jax.random.split(key, 24)
    L = DEPTH
    nrm = lambda k, shape, fan_in: jax.random.normal(k, shape, jnp.float32) * fan_in ** -0.5
    gain = lambda k, shape: 1.0 + 0.02 * jax.random.normal(k, shape, jnp.float32)
    return {
        "x": jax.random.normal(ks[0], (BATCH, SEQ, D_MODEL), jnp.float32),
        "g_ffn1": gain(ks[1], (L, D_MODEL)),
        "w1_gate": nrm(ks[2], (L, D_MODEL, D_FF), D_MODEL),
        "w1_up": nrm(ks[3], (L, D_MODEL, D_FF), D_MODEL),
        "w1_down": nrm(ks[4], (L, D_FF, D_MODEL), D_FF),
        "g_mix": gain(ks[5], (L, D_MODEL)),
        "w_in": nrm(ks[6], (L, D_MODEL, D_IN), D_MODEL),
        "g_q": gain(ks[7], (L, HEAD_DIM)),
        "g_k": gain(ks[8], (L, HEAD_DIM)),
        "g_sgu": gain(ks[9], (L, D_SGU)),
        "w_s": nrm(ks[10], (L, N_SGU_GROUPS, CHUNK, CHUNK), CHUNK),
        "b_s": 1.0 + 0.02 * jax.random.normal(ks[11], (L, N_SGU_GROUPS, CHUNK), jnp.float32),
        "g_attn_out": gain(ks[12], (L, D_ATTN)),
        "g_sgu_out": gain(ks[13], (L, D_SGU)),
        "w_out": nrm(ks[14], (L, D_MIX, D_MODEL), D_MIX),
        "g_ffn2": gain(ks[15], (L, D_MODEL)),
        "w2_gate": nrm(ks[16], (L, D_MODEL, D_FF), D_MODEL),
        "w2_up": nrm(ks[17], (L, D_MODEL, D_FF), D_MODEL),
        "w2_down": nrm(ks[18], (L, D_FF, D_MODEL), D_FF),
        "g_final": gain(ks[19], (L, D_MODEL)),
    }


def _fwd_reference(x, g_ffn1, w1_gate, w1_up, w1_down, g_mix, w_in, g_q, g_k, g_sgu, w_s, b_s,
              g_attn_out, g_sgu_out, w_out, g_ffn2, w2_gate, w2_up, w2_down, g_final):
    b, s, _ = x.shape
    rows = s // GRID_W
    cos, sin = axial_rope_tables(rows)
    for l in range(DEPTH):
        x = x + 0.5 * swiglu(rms_norm(x, g_ffn1[l]), w1_gate[l], w1_up[l], w1_down[l])

        h = rms_norm(x, g_mix[l])
        proj = h @ w_in[l]
        q, k, v, z = jnp.split(proj, [D_ATTN, D_ATTN + D_KV, D_ATTN + 2 * D_KV], axis=-1)

        q = rms_norm(q.reshape(b, s, N_HEADS, HEAD_DIM), g_q[l])
        k = rms_norm(k.reshape(b, s, N_KV_HEADS, HEAD_DIM), g_k[l])
        v = v.reshape(b, s, N_KV_HEADS, HEAD_DIM)
        q = apply_rope(q, cos, sin)
        k = apply_rope(k, cos, sin)
        attn = gqa_attention(q, k, v)

        sgu = spatial_gating(jax.nn.gelu(z), g_sgu[l], w_s[l], b_s[l])

        mixed = jnp.concatenate([rms_norm(attn, g_attn_out[l]), rms_norm(sgu, g_sgu_out[l])], axis=-1)
        x = x + mixed @ w_out[l]

        x = x + 0.5 * swiglu(rms_norm(x, g_ffn2[l]), w2_gate[l], w2_up[l], w2_down[l])
        x = rms_norm(x, g_final[l])
    return x


import jax as _jax
import jax.numpy as _jnp

TWIN_FORMAT = 'train_step'
FWD_PARAMS = ['x', 'g_ffn1', 'w1_gate', 'w1_up', 'w1_down', 'g_mix', 'w_in', 'g_q', 'g_k', 'g_sgu', 'w_s', 'b_s', 'g_attn_out', 'g_sgu_out', 'w_out', 'g_ffn2', 'w2_gate', 'w2_up', 'w2_down', 'g_final']
TWIN_WEIGHTS = ['g_ffn1', 'w1_gate', 'w1_up', 'w1_down', 'g_mix', 'w_in', 'g_q', 'g_k', 'g_sgu', 'w_s', 'b_s', 'g_attn_out', 'g_sgu_out', 'w_out', 'g_ffn2', 'w2_gate', 'w2_up', 'w2_down', 'g_final']
TWIN_DIFF_INPUT = 'x'
TWIN_INPUTS = ['x', 'g_ffn1', 'w1_gate', 'w1_up', 'w1_down', 'g_mix', 'w_in', 'g_q', 'g_k', 'g_sgu', 'w_s', 'b_s', 'g_attn_out', 'g_sgu_out', 'w_out', 'g_ffn2', 'w2_gate', 'w2_up', 'w2_down', 'g_final', 'loss_target', 'm_g_ffn1', 'm_w1_gate', 'm_w1_up', 'm_w1_down', 'm_g_mix', 'm_w_in', 'm_g_q', 'm_g_k', 'm_g_sgu', 'm_w_s', 'm_b_s', 'm_g_attn_out', 'm_g_sgu_out', 'm_w_out', 'm_g_ffn2', 'm_w2_gate', 'm_w2_up', 'm_w2_down', 'm_g_final', 'v_g_ffn1', 'v_w1_gate', 'v_w1_up', 'v_w1_down', 'v_g_mix', 'v_w_in', 'v_g_q', 'v_g_k', 'v_g_sgu', 'v_w_s', 'v_b_s', 'v_g_attn_out', 'v_g_sgu_out', 'v_w_out', 'v_g_ffn2', 'v_w2_gate', 'v_w2_up', 'v_w2_down', 'v_g_final']
TWIN_OUTPUTS = ['loss', 'grad_x', 'grad_g_ffn1', 'grad_w1_gate', 'grad_w1_up', 'grad_w1_down', 'grad_g_mix', 'grad_w_in', 'grad_g_q', 'grad_g_k', 'grad_g_sgu', 'grad_w_s', 'grad_b_s', 'grad_g_attn_out', 'grad_g_sgu_out', 'grad_w_out', 'grad_g_ffn2', 'grad_w2_gate', 'grad_w2_up', 'grad_w2_down', 'grad_g_final', 'delta_g_ffn1', 'delta_w1_gate', 'delta_w1_up', 'delta_w1_down', 'delta_g_mix', 'delta_w_in', 'delta_g_q', 'delta_g_k', 'delta_g_sgu', 'delta_w_s', 'delta_b_s', 'delta_g_attn_out', 'delta_g_sgu_out', 'delta_w_out', 'delta_g_ffn2', 'delta_w2_gate', 'delta_w2_up', 'delta_w2_down', 'delta_g_final', 'new_m_g_ffn1', 'new_m_w1_gate', 'new_m_w1_up', 'new_m_w1_down', 'new_m_g_mix', 'new_m_w_in', 'new_m_g_q', 'new_m_g_k', 'new_m_g_sgu', 'new_m_w_s', 'new_m_b_s', 'new_m_g_attn_out', 'new_m_g_sgu_out', 'new_m_w_out', 'new_m_g_ffn2', 'new_m_w2_gate', 'new_m_w2_up', 'new_m_w2_down', 'new_m_g_final', 'new_v_g_ffn1', 'new_v_w1_gate', 'new_v_w1_up', 'new_v_w1_down', 'new_v_g_mix', 'new_v_w_in', 'new_v_g_q', 'new_v_g_k', 'new_v_g_sgu', 'new_v_w_s', 'new_v_b_s', 'new_v_g_attn_out', 'new_v_g_sgu_out', 'new_v_w_out', 'new_v_g_ffn2', 'new_v_w2_gate', 'new_v_w2_up', 'new_v_w2_down', 'new_v_g_final']
TWIN_LEAF_KINDS = {'loss': 'loss', 'grad_x': 'grad_x', 'grad_g_ffn1': 'grad_w', 'grad_w1_gate': 'grad_w', 'grad_w1_up': 'grad_w', 'grad_w1_down': 'grad_w', 'grad_g_mix': 'grad_w', 'grad_w_in': 'grad_w', 'grad_g_q': 'grad_w', 'grad_g_k': 'grad_w', 'grad_g_sgu': 'grad_w', 'grad_w_s': 'grad_w', 'grad_b_s': 'grad_w', 'grad_g_attn_out': 'grad_w', 'grad_g_sgu_out': 'grad_w', 'grad_w_out': 'grad_w', 'grad_g_ffn2': 'grad_w', 'grad_w2_gate': 'grad_w', 'grad_w2_up': 'grad_w', 'grad_w2_down': 'grad_w', 'grad_g_final': 'grad_w', 'delta_g_ffn1': 'delta_w', 'delta_w1_gate': 'delta_w', 'delta_w1_up': 'delta_w', 'delta_w1_down': 'delta_w', 'delta_g_mix': 'delta_w', 'delta_w_in': 'delta_w', 'delta_g_q': 'delta_w', 'delta_g_k': 'delta_w', 'delta_g_sgu': 'delta_w', 'delta_w_s': 'delta_w', 'delta_b_s': 'delta_w', 'delta_g_attn_out': 'delta_w', 'delta_g_sgu_out': 'delta_w', 'delta_w_out': 'delta_w', 'delta_g_ffn2': 'delta_w', 'delta_w2_gate': 'delta_w', 'delta_w2_up': 'delta_w', 'delta_w2_down': 'delta_w', 'delta_g_final': 'delta_w', 'new_m_g_ffn1': 'new_m', 'new_m_w1_gate': 'new_m', 'new_m_w1_up': 'new_m', 'new_m_w1_down': 'new_m', 'new_m_g_mix': 'new_m', 'new_m_w_in': 'new_m', 'new_m_g_q': 'new_m', 'new_m_g_k': 'new_m', 'new_m_g_sgu': 'new_m', 'new_m_w_s': 'new_m', 'new_m_b_s': 'new_m', 'new_m_g_attn_out': 'new_m', 'new_m_g_sgu_out': 'new_m', 'new_m_w_out': 'new_m', 'new_m_g_ffn2': 'new_m', 'new_m_w2_gate': 'new_m', 'new_m_w2_up': 'new_m', 'new_m_w2_down': 'new_m', 'new_m_g_final': 'new_m', 'new_v_g_ffn1': 'new_v', 'new_v_w1_gate': 'new_v', 'new_v_w1_up': 'new_v', 'new_v_w1_down': 'new_v', 'new_v_g_mix': 'new_v', 'new_v_w_in': 'new_v', 'new_v_g_q': 'new_v', 'new_v_g_k': 'new_v', 'new_v_g_sgu': 'new_v', 'new_v_w_s': 'new_v', 'new_v_b_s': 'new_v', 'new_v_g_attn_out': 'new_v', 'new_v_g_sgu_out': 'new_v', 'new_v_w_out': 'new_v', 'new_v_g_ffn2': 'new_v', 'new_v_w2_gate': 'new_v', 'new_v_w2_up': 'new_v', 'new_v_w2_down': 'new_v', 'new_v_g_final': 'new_v'}


def _forward(args):
    return _fwd_reference(*[args[k] for k in FWD_PARAMS])


def _output_shape():
    out = _jax.eval_shape(lambda: _forward(_fwd_setup_inputs(0)))
    return out.shape, out.dtype

N_MICROBATCH = 1
ADAM_LR = 0.001
ADAM_B1 = 0.9
ADAM_B2 = 0.999
ADAM_EPS = 1e-08
ADAM_WD = 0.01
ADAM_STEP = 10
PER_EXAMPLE_BATCH_AXIS = {'x': 0, 'loss_target': 0}
SHARED_INPUTS = []
_WEIGHT_DTYPES = {'g_ffn1': _jnp.float32, 'w1_gate': _jnp.float32, 'w1_up': _jnp.float32, 'w1_down': _jnp.float32, 'g_mix': _jnp.float32, 'w_in': _jnp.float32, 'g_q': _jnp.float32, 'g_k': _jnp.float32, 'g_sgu': _jnp.float32, 'w_s': _jnp.float32, 'b_s': _jnp.float32, 'g_attn_out': _jnp.float32, 'g_sgu_out': _jnp.float32, 'w_out': _jnp.float32, 'g_ffn2': _jnp.float32, 'w2_gate': _jnp.float32, 'w2_up': _jnp.float32, 'w2_down': _jnp.float32, 'g_final': _jnp.float32}
MOMENT_SCALE = {'g_ffn1': 1.672024e-01, 'w1_gate': 5.863722e-02, 'w1_up': 5.741918e-02, 'w1_down': 9.496110e-02, 'g_mix': 2.843262e-01, 'w_in': 2.116333e-01, 'g_q': 5.535946e-01, 'g_k': 5.295835e-01, 'g_sgu': 1.360658e-01, 'w_s': 9.067870e-02, 'b_s': 9.694061e-02, 'g_attn_out': 2.165473e-01, 'g_sgu_out': 1.852447e-01, 'w_out': 1.849037e-01, 'g_ffn2': 7.356336e-02, 'w2_gate': 3.158054e-02, 'w2_up': 3.061336e-02, 'w2_down': 5.068126e-02, 'g_final': 6.391005e+01}


def _to_microbatches(a, axis):
    t = _jnp.moveaxis(a, axis, 0)
    t = t.reshape((N_MICROBATCH, t.shape[0] // N_MICROBATCH) + t.shape[1:])
    return _jnp.moveaxis(t, 1, axis + 1)


def setup_inputs(seed: int = 0) -> dict:
    inp = _fwd_setup_inputs(seed)
    key = _jax.random.fold_in(_jax.random.key(seed), 7919)
    shape, _ = _output_shape()
    out = dict(inp)
    out["loss_target"] = _jax.random.normal(_jax.random.fold_in(key, 0), shape, _jnp.float32)
    for i, name in enumerate(TWIN_WEIGHTS):
        w = inp[name].astype(_jnp.float32)
        if MOMENT_SCALE is None:
            s = _jnp.sqrt(_jnp.mean(_jnp.square(w)) + 1e-30)
        else:
            s = MOMENT_SCALE[name]
        km, kv = _jax.random.split(_jax.random.fold_in(key, i + 1))
        out[name] = w
        out["m_" + name] = s * _jax.random.normal(km, w.shape, _jnp.float32)
        out["v_" + name] = (s * s) * _jax.random.uniform(kv, w.shape, _jnp.float32, 0.5, 1.5)
    if N_MICROBATCH > 1:
        for name, axis in PER_EXAMPLE_BATCH_AXIS.items():
            out[name] = _to_microbatches(out[name], axis)
    return {'x': out['x'], 'g_ffn1': out['g_ffn1'], 'w1_gate': out['w1_gate'], 'w1_up': out['w1_up'], 'w1_down': out['w1_down'], 'g_mix': out['g_mix'], 'w_in': out['w_in'], 'g_q': out['g_q'], 'g_k': out['g_k'], 'g_sgu': out['g_sgu'], 'w_s': out['w_s'], 'b_s': out['b_s'], 'g_attn_out': out['g_attn_out'], 'g_sgu_out': out['g_sgu_out'], 'w_out': out['w_out'], 'g_ffn2': out['g_ffn2'], 'w2_gate': out['w2_gate'], 'w2_up': out['w2_up'], 'w2_down': out['w2_down'], 'g_final': out['g_final'], 'loss_target': out['loss_target'], 'm_g_ffn1': out['m_g_ffn1'], 'm_w1_gate': out['m_w1_gate'], 'm_w1_up': out['m_w1_up'], 'm_w1_down': out['m_w1_down'], 'm_g_mix': out['m_g_mix'], 'm_w_in': out['m_w_in'], 'm_g_q': out['m_g_q'], 'm_g_k': out['m_g_k'], 'm_g_sgu': out['m_g_sgu'], 'm_w_s': out['m_w_s'], 'm_b_s': out['m_b_s'], 'm_g_attn_out': out['m_g_attn_out'], 'm_g_sgu_out': out['m_g_sgu_out'], 'm_w_out': out['m_w_out'], 'm_g_ffn2': out['m_g_ffn2'], 'm_w2_gate': out['m_w2_gate'], 'm_w2_up': out['m_w2_up'], 'm_w2_down': out['m_w2_down'], 'm_g_final': out['m_g_final'], 'v_g_ffn1': out['v_g_ffn1'], 'v_w1_gate': out['v_w1_gate'], 'v_w1_up': out['v_w1_up'], 'v_w1_down': out['v_w1_down'], 'v_g_mix': out['v_g_mix'], 'v_w_in': out['v_w_in'], 'v_g_q': out['v_g_q'], 'v_g_k': out['v_g_k'], 'v_g_sgu': out['v_g_sgu'], 'v_w_s': out['v_w_s'], 'v_b_s': out['v_b_s'], 'v_g_attn_out': out['v_g_attn_out'], 'v_g_sgu_out': out['v_g_sgu_out'], 'v_w_out': out['v_w_out'], 'v_g_ffn2': out['v_g_ffn2'], 'v_w2_gate': out['v_w2_gate'], 'v_w2_up': out['v_w2_up'], 'v_w2_down': out['v_w2_down'], 'v_g_final': out['v_g_final']}


def _loss(weights, diff, rest, loss_target):
    with _jax.named_scope("forward"):
        args = {**rest, TWIN_DIFF_INPUT: diff, **{k: w.astype(_WEIGHT_DTYPES[k]) for k, w in weights.items()}}
        y = _forward(args)
    with _jax.named_scope("loss_head"):
        err = _jnp.square(y.astype(_jnp.float32) - loss_target)
        return 0.5 * _jnp.sum(_jnp.mean(err, axis=-1)) if err.ndim else 0.5 * err


def _adamw(w, g, m, v):
    m = ADAM_B1 * m + (1.0 - ADAM_B1) * g
    v = ADAM_B2 * v + (1.0 - ADAM_B2) * _jnp.square(g)
    m_hat = m / (1.0 - ADAM_B1 ** ADAM_STEP)
    v_hat = v / (1.0 - ADAM_B2 ** ADAM_STEP)
    delta = -ADAM_LR * (m_hat / (_jnp.sqrt(v_hat) + ADAM_EPS) + ADAM_WD * w)
    return delta, m, v


def reference(x, g_ffn1, w1_gate, w1_up, w1_down, g_mix, w_in, g_q, g_k, g_sgu, w_s, b_s, g_attn_out, g_sgu_out, w_out, g_ffn2, w2_gate, w2_up, w2_down, g_final, loss_target, m_g_ffn1, m_w1_gate, m_w1_up, m_w1_down, m_g_mix, m_w_in, m_g_q, m_g_k, m_g_sgu, m_w_s, m_b_s, m_g_attn_out, m_g_sgu_out, m_w_out, m_g_ffn2, m_w2_gate, m_w2_up, m_w2_down, m_g_final, v_g_ffn1, v_w1_gate, v_w1_up, v_w1_down, v_g_mix, v_w_in, v_g_q, v_g_k, v_g_sgu, v_w_s, v_b_s, v_g_attn_out, v_g_sgu_out, v_w_out, v_g_ffn2, v_w2_gate, v_w2_up, v_w2_down, v_g_final):
    given = dict(x=x, g_ffn1=g_ffn1, w1_gate=w1_gate, w1_up=w1_up, w1_down=w1_down, g_mix=g_mix, w_in=w_in, g_q=g_q, g_k=g_k, g_sgu=g_sgu, w_s=w_s, b_s=b_s, g_attn_out=g_attn_out, g_sgu_out=g_sgu_out, w_out=w_out, g_ffn2=g_ffn2, w2_gate=w2_gate, w2_up=w2_up, w2_down=w2_down, g_final=g_final, loss_target=loss_target, m_g_ffn1=m_g_ffn1, m_w1_gate=m_w1_gate, m_w1_up=m_w1_up, m_w1_down=m_w1_down, m_g_mix=m_g_mix, m_w_in=m_w_in, m_g_q=m_g_q, m_g_k=m_g_k, m_g_sgu=m_g_sgu, m_w_s=m_w_s, m_b_s=m_b_s, m_g_attn_out=m_g_attn_out, m_g_sgu_out=m_g_sgu_out, m_w_out=m_w_out, m_g_ffn2=m_g_ffn2, m_w2_gate=m_w2_gate, m_w2_up=m_w2_up, m_w2_down=m_w2_down, m_g_final=m_g_final, v_g_ffn1=v_g_ffn1, v_w1_gate=v_w1_gate, v_w1_up=v_w1_up, v_w1_down=v_w1_down, v_g_mix=v_g_mix, v_w_in=v_w_in, v_g_q=v_g_q, v_g_k=v_g_k, v_g_sgu=v_g_sgu, v_w_s=v_w_s, v_b_s=v_b_s, v_g_attn_out=v_g_attn_out, v_g_sgu_out=v_g_sgu_out, v_w_out=v_w_out, v_g_ffn2=v_g_ffn2, v_w2_gate=v_w2_gate, v_w2_up=v_w2_up, v_w2_down=v_w2_down, v_g_final=v_g_final)
    weights = {n: given[n] for n in TWIN_WEIGHTS}
    shared = {n: given[n] for n in SHARED_INPUTS}
    per_example = {n: given[n] for n in ['x']}
    grad_fn = _jax.value_and_grad(_loss, argnums=(0, 1))

    def one_microbatch(ex, loss_target):
        ex = dict(ex)
        diff = ex.pop(TWIN_DIFF_INPUT)
        return grad_fn(weights, diff, {**shared, **ex}, loss_target)

    if N_MICROBATCH == 1:
        loss, (grad_w, grad_x) = one_microbatch(per_example, given["loss_target"])
    else:
        def body(carry, xs):
            loss_sum, grad_sum = carry
            l_k, (gw_k, gx_k) = one_microbatch(xs[0], xs[1])
            with _jax.named_scope("update"):
                return (loss_sum + l_k, _jax.tree.map(_jnp.add, grad_sum, gw_k)), gx_k

        init = (_jnp.zeros((), _jnp.float32), _jax.tree.map(_jnp.zeros_like, weights))
        (loss, grad_w), grad_x = _jax.lax.scan(body, init, (per_example, given["loss_target"]))
    with _jax.named_scope("update"):
        delta_w, new_m, new_v = {}, {}, {}
        for n in TWIN_WEIGHTS:
            delta_w[n], new_m[n], new_v[n] = _adamw(weights[n], grad_w[n], given["m_" + n], given["v_" + n])
    return (loss, grad_x, *[grad_w[n] for n in TWIN_WEIGHTS], *[delta_w[n] for n in TWIN_WEIGHTS],
            *[new_m[n] for n in TWIN_WEIGHTS], *[new_v[n] for n in TWIN_WEIGHTS])
```

```python
import functools
import math

import jax
import jax.numpy as jnp
from jax import lax
from jax.experimental import pallas as pl
from jax.experimental.pallas import tpu as pltpu

F32 = jnp.float32
BF16 = jnp.bfloat16

D_MODEL = 1024
D_FF = 2816
N_HEADS = 8
HEAD_DIM = 64
N_KV_HEADS = 2
KV_GROUP = N_HEADS // N_KV_HEADS
D_ATTN = N_HEADS * HEAD_DIM
D_KV = N_KV_HEADS * HEAD_DIM
D_QKV = D_ATTN + 2 * D_KV
N_SGU_GROUPS = 8
SGU_GROUP_DIM = 64
D_SGU = N_SGU_GROUPS * SGU_GROUP_DIM
CHUNK = 128
GRID_W = 64
ROPE_THETA = 10000.0
EPS = 1e-6
N_DEV = 8
LANES = 128

ADAM_LR = 0.001
ADAM_B1 = 0.9
ADAM_B2 = 0.999
ADAM_EPS = 1e-08
ADAM_WD = 0.01
ADAM_STEP = 10

MESH_AXES = ("x", "y", "c")
MESH_IDS = pl.DeviceIdType.MESH

VMEM_LIMIT = 56 * 1024 * 1024

SHARD_ROWS = (("w1_gate", D_FF // N_DEV), ("w1_up", D_FF // N_DEV), ("w1_down", D_FF // N_DEV),
              ("w_in", (D_QKV + 2 * D_SGU) // N_DEV), ("w_out", D_MODEL // N_DEV),
              ("w2_gate", D_FF // N_DEV), ("w2_up", D_FF // N_DEV), ("w2_down", D_FF // N_DEV))
PACK_ROWS = sum(r for _, r in SHARD_ROWS)
TRANSPOSED = ("w1_gate", "w1_up", "w_in", "w2_gate", "w2_up")


def _params(n_grid):
    return pltpu.CompilerParams(dimension_semantics=("arbitrary",) * n_grid, vmem_limit_bytes=VMEM_LIMIT)


def _dot(a, b):
    return jnp.dot(a, b, preferred_element_type=F32)


def _dot_nt(a, b):
    return lax.dot_general(a, b, (((1,), (1,)), ((), ())), preferred_element_type=F32)


def _dot_tn(a, b):
    return lax.dot_general(a, b, (((0,), (0,)), ((), ())), preferred_element_type=F32)


def _dot_f32(a, b):
    return jnp.dot(a, b, preferred_element_type=F32, precision=lax.Precision.HIGHEST)


def _rstd(x):
    return lax.rsqrt(jnp.mean(x * x, axis=-1, keepdims=True) + EPS)


def _rms_bwd(dy, n, r, g):
    dn = dy * g
    return r * (dn - n * jnp.mean(dn * n, axis=-1, keepdims=True)), dy * n


def _colsum(a):
    return jnp.sum(a, axis=0, keepdims=True)


_GELU_C = math.sqrt(2.0 / math.pi)


def _gelu(x):
    t = jnp.tanh(_GELU_C * (x + 0.044715 * (x * x * x)))
    return x * (0.5 * (1.0 + t)), t


def _gelu_grad(x, t):
    return 0.5 * (1.0 + t) + 0.5 * x * (1.0 - t * t) * (_GELU_C * (1.0 + 3 * 0.044715 * x * x))


def _pair_swap(a):
    w = a.shape[-1]
    lane = lax.broadcasted_iota(jnp.int32, a.shape, a.ndim - 1)
    return jnp.where(lane % 2 == 0, pltpu.roll(a, w - 1, a.ndim - 1), pltpu.roll(a, 1, a.ndim - 1))


def _tile_lanes(a, reps):
    return jnp.concatenate([a] * reps, axis=-1) if reps > 1 else a


def _full(shape):
    nd = len(shape)
    return pl.BlockSpec(shape, lambda *_: (0,) * nd)


def _mesh_pos():
    return lax.axis_index("x"), lax.axis_index("y"), lax.axis_index("c")


def _peer(pos, d):
    x, y, c = pos
    px = 1 - x if d & 4 else x
    py = 1 - y if d & 2 else y
    pc = 1 - c if d & 1 else c
    return (px, py, pc), 4 * px + 2 * py + pc


def all_gather_weights(packed):
    offs = []
    o = 0
    for _, r in SHARD_ROWS:
        offs.append(o)
        o += r

    def body(src, *refs):
        outs = refs[:len(SHARD_ROWS)]
        send_sems, recv_sems, local_sem = refs[len(SHARD_ROWS):]
        pos = _mesh_pos()
        me = 4 * pos[0] + 2 * pos[1] + pos[2]
        local = [pltpu.make_async_copy(src.at[pl.ds(offs[w], r), :], outs[w].at[me], local_sem.at[w])
                 for w, (_, r) in enumerate(SHARD_ROWS)]
        for cp in local:
            cp.start()
        for d in range(1, N_DEV):
            peer, _ = _peer(pos, d)
            for w, (_, r) in enumerate(SHARD_ROWS):
                pltpu.make_async_remote_copy(
                    src_ref=src.at[pl.ds(offs[w], r), :], dst_ref=outs[w].at[me],
                    send_sem=send_sems.at[d - 1], recv_sem=recv_sems.at[d - 1],
                    device_id=peer, device_id_type=MESH_IDS).start()
        for d in range(1, N_DEV):
            peer, _ = _peer(pos, d)
            everything = pltpu.make_async_remote_copy(
                src_ref=src, dst_ref=src, send_sem=send_sems.at[d - 1], recv_sem=recv_sems.at[d - 1],
                device_id=peer, device_id_type=MESH_IDS)
            everything.wait_send()
            everything.wait_recv()
        for cp in local:
            cp.wait()

    any_spec = pl.BlockSpec(memory_space=pl.ANY)
    return pl.pallas_call(
        functools.partial(body),
        name="all_gather_weights",
        out_shape=[jax.ShapeDtypeStruct((N_DEV, r, D_MODEL), BF16) for _, r in SHARD_ROWS],
        in_specs=[any_spec],
        out_specs=[any_spec] * len(SHARD_ROWS),
        scratch_shapes=[pltpu.SemaphoreType.DMA((N_DEV - 1,)), pltpu.SemaphoreType.DMA((N_DEV - 1,)),
                        pltpu.SemaphoreType.DMA((len(SHARD_ROWS),))],
        compiler_params=pltpu.CompilerParams(has_side_effects=True),
    )(packed)


def exchange_grads(grads, small):
    offs = []
    o = 0
    for _, r in SHARD_ROWS:
        offs.append(o)
        o += r
    n_w = len(SHARD_ROWS)

    def body(*refs):
        gs = refs[:n_w]
        small_ref, parts, small_parts, send_sems, recv_sems, small_send, small_recv, local_sem = refs[n_w:]
        pos = _mesh_pos()
        me = 4 * pos[0] + 2 * pos[1] + pos[2]
        local = [pltpu.make_async_copy(gs[w].at[pl.ds(pl.multiple_of(me * r, 16), r), :],
                                       parts.at[me, pl.ds(offs[w], r), :], local_sem.at[w])
                 for w, (_, r) in enumerate(SHARD_ROWS)]
        local.append(pltpu.make_async_copy(small_ref, small_parts.at[me], local_sem.at[n_w]))
        for cp in local:
            cp.start()
        small_copies = []
        for d in range(1, N_DEV):
            peer, peer_lin = _peer(pos, d)
            for w, (_, r) in enumerate(SHARD_ROWS):
                pltpu.make_async_remote_copy(
                    src_ref=gs[w].at[pl.ds(pl.multiple_of(peer_lin * r, 16), r), :],
                    dst_ref=parts.at[me, pl.ds(offs[w], r), :],
                    send_sem=send_sems.at[d - 1], recv_sem=recv_sems.at[d - 1],
                    device_id=peer, device_id_type=MESH_IDS).start()
            cp = pltpu.make_async_remote_copy(
                src_ref=small_ref, dst_ref=small_parts.at[me],
                send_sem=small_send.at[d - 1], recv_sem=small_recv.at[d - 1],
                device_id=peer, device_id_type=MESH_IDS)
            cp.start()
            small_copies.append(cp)
        for d in range(1, N_DEV):
            peer, peer_lin = _peer(pos, d)
            everything = pltpu.make_async_remote_copy(
                src_ref=parts.at[peer_lin], dst_ref=parts.at[peer_lin],
                send_sem=send_sems.at[d - 1], recv_sem=recv_sems.at[d - 1],
                device_id=peer, device_id_type=MESH_IDS)
            everything.wait_send()
            everything.wait_recv()
            small_copies[d - 1].wait_send()
            small_copies[d - 1].wait_recv()
        for cp in local:
            cp.wait()

    any_spec = pl.BlockSpec(memory_space=pl.ANY)
    return pl.pallas_call(
        functools.partial(body),
        name="exchange_grads",
        out_shape=[jax.ShapeDtypeStruct((N_DEV, PACK_ROWS, D_MODEL), BF16),
                   jax.ShapeDtypeStruct((N_DEV,) + small.shape, F32)],
        in_specs=[any_spec] * (n_w + 1),
        out_specs=[any_spec, any_spec],
        scratch_shapes=[pltpu.SemaphoreType.DMA((N_DEV - 1,)), pltpu.SemaphoreType.DMA((N_DEV - 1,)),
                        pltpu.SemaphoreType.DMA((N_DEV - 1,)), pltpu.SemaphoreType.DMA((N_DEV - 1,)),
                        pltpu.SemaphoreType.DMA((n_w + 1,))],
        compiler_params=pltpu.CompilerParams(has_side_effects=True),
    )(*grads, small)


def sum_parts(parts, block_rows):
    n, rows, cols = parts.shape

    def body(p_ref, o_ref):
        acc = p_ref[0].astype(F32)
        for s in range(1, n):
            acc = acc + p_ref[s].astype(F32)
        o_ref[...] = acc

    return pl.pallas_call(
        functools.partial(body), name="sum_parts",
        grid=(rows // block_rows,),
        in_specs=[pl.BlockSpec((n, block_rows, cols), lambda i: (0, i, 0))],
        out_specs=pl.BlockSpec((block_rows, cols), lambda i: (i, 0)),
        out_shape=jax.ShapeDtypeStruct((rows, cols), F32),
        compiler_params=_params(1),
    )(parts)


def adamw(w, g, m, v):
    def body(w_ref, g_ref, m_ref, v_ref, d_ref, m_out, v_out):
        gg = g_ref[...]
        m2 = ADAM_B1 * m_ref[...] + (1.0 - ADAM_B1) * gg
        v2 = ADAM_B2 * v_ref[...] + (1.0 - ADAM_B2) * (gg * gg)
        m_hat = m2 / (1.0 - ADAM_B1 ** ADAM_STEP)
        v_hat = v2 / (1.0 - ADAM_B2 ** ADAM_STEP)
        d_ref[...] = -ADAM_LR * (m_hat / (jnp.sqrt(v_hat) + ADAM_EPS) + ADAM_WD * w_ref[...])
        m_out[...] = m2
        v_out[...] = v2

    spec = _full(w.shape)
    shape = jax.ShapeDtypeStruct(w.shape, F32)
    return pl.pallas_call(
        functools.partial(body), name="adamw",
        in_specs=[spec] * 4, out_specs=[spec] * 3, out_shape=[shape] * 3,
        compiler_params=pltpu.CompilerParams(vmem_limit_bytes=VMEM_LIMIT),
    )(w, g, m, v)


def ffn_up(x, g, wg_t, wu_t, tm, tn):
    t = x.shape[0]

    def body(x_ref, g_ref, wg_ref, wu_ref, h_ref, a_ref, b_ref, act_ref):
        @pl.when(pl.program_id(1) == 0)
        def _():
            xx = x_ref[...]
            h_ref[...] = ((xx * _rstd(xx)) * g_ref[...]).astype(BF16)

        h = h_ref[...]
        a = _dot_nt(h, wg_ref[...])
        b = _dot_nt(h, wu_ref[...])
        a_ref[...] = a.astype(BF16)
        b_ref[...] = b.astype(BF16)
        act_ref[...] = (a * jax.nn.sigmoid(a) * b).astype(BF16)

    wide = jax.ShapeDtypeStruct((t, D_FF), BF16)
    tile = pl.BlockSpec((tm, tn), lambda i, j: (i, j))
    return pl.pallas_call(
        functools.partial(body), name="ffn_up",
        grid=(t // tm, D_FF // tn),
        in_specs=[pl.BlockSpec((tm, D_MODEL), lambda i, j: (i, 0)), _full((1, D_MODEL)),
                  pl.BlockSpec((tn, D_MODEL), lambda i, j: (j, 0)), pl.BlockSpec((tn, D_MODEL), lambda i, j: (j, 0))],
        out_specs=[pl.BlockSpec((tm, D_MODEL), lambda i, j: (i, 0)), tile, tile, tile],
        out_shape=[jax.ShapeDtypeStruct((t, D_MODEL), BF16), wide, wide, wide],
        compiler_params=_params(2),
    )(x, g, wg_t, wu_t)


def ffn_down(act, wd, x, tm):
    t = x.shape[0]

    def body(act_ref, wd_ref, x_ref, o_ref):
        o_ref[...] = x_ref[...] + 0.5 * _dot(act_ref[...], wd_ref[...])

    return pl.pallas_call(
        functools.partial(body), name="ffn_down",
        grid=(t // tm,),
        in_specs=[pl.BlockSpec((tm, D_FF), lambda i: (i, 0)), _full((D_FF, D_MODEL)),
                  pl.BlockSpec((tm, D_MODEL), lambda i: (i, 0))],
        out_specs=pl.BlockSpec((tm, D_MODEL), lambda i: (i, 0)),
        out_shape=jax.ShapeDtypeStruct((t, D_MODEL), F32),
        compiler_params=_params(1),
    )(act, wd, x)


def ffn_bwd_act(dx, wd, a, b, tm, tn):
    t = dx.shape[0]

    def body(dx_ref, wd_ref, a_ref, b_ref, da_ref, db_ref, dxb_ref):
        @pl.when(pl.program_id(1) == 0)
        def _():
            dxb_ref[...] = dx_ref[...].astype(BF16)

        dact = 0.5 * _dot_nt(dxb_ref[...], wd_ref[...])
        aa = a_ref[...].astype(F32)
        sig = jax.nn.sigmoid(aa)
        da_ref[...] = (dact * b_ref[...].astype(F32) * (sig * (1.0 + aa * (1.0 - sig)))).astype(BF16)
        db_ref[...] = (dact * (aa * sig)).astype(BF16)

    wide = jax.ShapeDtypeStruct((t, D_FF), BF16)
    tile = pl.BlockSpec((tm, tn), lambda i, j: (i, j))
    return pl.pallas_call(
        functools.partial(body), name="ffn_bwd_act",
        grid=(t // tm, D_FF // tn),
        in_specs=[pl.BlockSpec((tm, D_MODEL), lambda i, j: (i, 0)),
                  pl.BlockSpec((tn, D_MODEL), lambda i, j: (j, 0)), tile, tile],
        out_specs=[tile, tile],
        out_shape=[wide, wide],
        scratch_shapes=[pltpu.VMEM((tm, D_MODEL), BF16)],
        compiler_params=_params(2),
    )(dx, wd, a, b)


def norm_bwd_matmul(a1, w1, a2, w2, x, g, dx_in, tm):
    t = x.shape[0]
    k1, k2 = a1.shape[1], a2.shape[1]

    def body(a1_ref, w1_ref, a2_ref, w2_ref, x_ref, g_ref, dxin_ref, dx_ref, dg_ref):
        dh = _dot(a1_ref[...], w1_ref[...]) + _dot(a2_ref[...], w2_ref[...])
        xx = x_ref[...]
        r = _rstd(xx)
        dx, dg_rows = _rms_bwd(dh, xx * r, r, g_ref[...])
        dx_ref[...] = dxin_ref[...] + dx

        @pl.when(pl.program_id(0) == 0)
        def _():
            dg_ref[...] = jnp.zeros_like(dg_ref)

        dg_ref[...] += _colsum(dg_rows)

    row = pl.BlockSpec((tm, D_MODEL), lambda i: (i, 0))
    return pl.pallas_call(
        functools.partial(body), name="norm_bwd_matmul",
        grid=(t // tm,),
        in_specs=[pl.BlockSpec((tm, k1), lambda i: (i, 0)), _full((k1, D_MODEL)),
                  pl.BlockSpec((tm, k2), lambda i: (i, 0)), _full((k2, D_MODEL)),
                  row, _full((1, D_MODEL)), row],
        out_specs=[row, _full((1, D_MODEL))],
        out_shape=[jax.ShapeDtypeStruct((t, D_MODEL), F32), jax.ShapeDtypeStruct((1, D_MODEL), F32)],
        compiler_params=_params(1),
    )(a1, w1, a2, w2, x, g, dx_in)


def matmul_tn(a, b, scale, tmm, tk):
    t, m = a.shape
    n = b.shape[1]
    nk = t // tk

    def body(a_ref, b_ref, o_ref, acc_ref):
        k = pl.program_id(1)

        @pl.when(k == 0)
        def _():
            acc_ref[...] = jnp.zeros_like(acc_ref)

        acc_ref[...] += _dot_tn(a_ref[...].astype(BF16), b_ref[...].astype(BF16))

        @pl.when(k == nk - 1)
        def _():
            o_ref[...] = (scale * acc_ref[...]).astype(BF16)

    return pl.pallas_call(
        functools.partial(body), name="matmul_tn",
        grid=(m // tmm, nk),
        in_specs=[pl.BlockSpec((tk, tmm), lambda i, k: (k, i)), pl.BlockSpec((tk, n), lambda i, k: (k, 0))],
        out_specs=pl.BlockSpec((tmm, n), lambda i, k: (i, 0)),
        out_shape=jax.ShapeDtypeStruct((m, n), BF16),
        scratch_shapes=[pltpu.VMEM((tmm, n), F32)],
        compiler_params=_params(2),
    )(a, b)


def norm_matmul(x, g, w_t, tm, tn, with_h):
    t = x.shape[0]
    n = w_t.shape[0]

    def body(x_ref, g_ref, w_ref, o_ref, h_ref):
        @pl.when(pl.program_id(1) == 0)
        def _():
            xx = x_ref[...]
            h_ref[...] = ((xx * _rstd(xx)) * g_ref[...]).astype(BF16)

        o_ref[...] = _dot_nt(h_ref[...], w_ref[...])

    in_specs = [pl.BlockSpec((tm, D_MODEL), lambda i, j: (i, 0)), _full((1, D_MODEL)),
                pl.BlockSpec((tn, D_MODEL), lambda i, j: (j, 0))]
    o_spec = pl.BlockSpec((tm, tn), lambda i, j: (i, j))
    o_shape = jax.ShapeDtypeStruct((t, n), F32)
    h_spec = pl.BlockSpec((tm, D_MODEL), lambda i, j: (i, 0))
    if with_h:
        return pl.pallas_call(
            functools.partial(body), name="norm_matmul_h", grid=(t // tm, n // tn),
            in_specs=in_specs, out_specs=[o_spec, h_spec],
            out_shape=[o_shape, jax.ShapeDtypeStruct((t, D_MODEL), BF16)],
            compiler_params=_params(2))(x, g, w_t)
    return pl.pallas_call(
        functools.partial(body), name="norm_matmul", grid=(t // tm, n // tn),
        in_specs=in_specs, out_specs=o_spec, out_shape=o_shape,
        scratch_shapes=[pltpu.VMEM((tm, D_MODEL), BF16)],
        compiler_params=_params(2))(x, g, w_t)


def _head_mean_matrix(width):
    head = jnp.arange(width) // HEAD_DIM
    return (head[:, None] == head[None, :]).astype(F32) / HEAD_DIM


def qk_prep(qkv, gq_w, gk_w, cos_w, sin_w, mean_q, mean_k, tm):
    t = qkv.shape[0]

    def body(p_ref, gq_ref, gk_ref, cos_ref, sin_ref, mq_ref, mk_ref, q_ref, k_ref, v_ref):
        cos2, sin2 = cos_ref[...], sin_ref[...]
        q = p_ref[:, :D_ATTN]
        k = p_ref[:, D_ATTN:D_ATTN + D_KV]
        qn = q * lax.rsqrt(_dot_f32(q * q, mq_ref[...]) + EPS) * gq_ref[...]
        kn = k * lax.rsqrt(_dot_f32(k * k, mk_ref[...]) + EPS) * gk_ref[...]
        cos8, sin8 = _tile_lanes(cos2, D_ATTN // LANES), _tile_lanes(sin2, D_ATTN // LANES)
        q_ref[...] = ((qn * cos8 + _pair_swap(qn) * sin8) * (HEAD_DIM ** -0.5)).astype(BF16)
        k_ref[...] = (kn * cos2 + _pair_swap(kn) * sin2).astype(BF16)
        v_ref[...] = p_ref[:, D_ATTN + D_KV:].astype(BF16)

    return pl.pallas_call(
        functools.partial(body), name="qk_prep", grid=(t // tm,),
        in_specs=[pl.BlockSpec((tm, D_QKV), lambda i: (i, 0)), _full((1, D_ATTN)), _full((1, D_KV)),
                  pl.BlockSpec((tm, LANES), lambda i: (i, 0)), pl.BlockSpec((tm, LANES), lambda i: (i, 0)),
                  _full((D_ATTN, D_ATTN)), _full((D_KV, D_KV))],
        out_specs=[pl.BlockSpec((tm, D_ATTN), lambda i: (i, 0)), pl.BlockSpec((tm, D_KV), lambda i: (i, 0)),
                   pl.BlockSpec((tm, D_KV), lambda i: (i, 0))],
        out_shape=[jax.ShapeDtypeStruct((t, D_ATTN), BF16), jax.ShapeDtypeStruct((t, D_KV), BF16),
                   jax.ShapeDtypeStruct((t, D_KV), BF16)],
        compiler_params=_params(1),
    )(qkv, gq_w, gk_w, cos_w, sin_w, mean_q, mean_k)


def qk_bwd(dq_rot, dk_rot, dv, qkv, gq_w, gk_w, cos_w, sin_w, mean_q, mean_k, tm):
    t = qkv.shape[0]

    def branch(raw, d_rot, gain, mean_mat, cos, sin, scale):
        r = lax.rsqrt(_dot_f32(raw * raw, mean_mat) + EPS)
        n = raw * r
        dy = (d_rot * cos - _pair_swap(d_rot) * sin) * scale
        dn = dy * gain
        return r * (dn - n * _dot_f32(dn * n, mean_mat)), dy * n

    def body(dq_ref, dk_ref, dv_ref, p_ref, gq_ref, gk_ref, cos_ref, sin_ref, mq_ref, mk_ref,
             dp_ref, dgq_ref, dgk_ref):
        cos2, sin2 = cos_ref[...], sin_ref[...]
        cos8, sin8 = _tile_lanes(cos2, D_ATTN // LANES), _tile_lanes(sin2, D_ATTN // LANES)
        dq, dgq = branch(p_ref[:, :D_ATTN], dq_ref[...], gq_ref[...], mq_ref[...], cos8, sin8, HEAD_DIM ** -0.5)
        dk, dgk = branch(p_ref[:, D_ATTN:D_ATTN + D_KV], dk_ref[...], gk_ref[...], mk_ref[...], cos2, sin2, 1.0)
        dp_ref[...] = jnp.concatenate([dq, dk, dv_ref[...]], axis=-1).astype(BF16)

        @pl.when(pl.program_id(0) == 0)
        def _():
            dgq_ref[...] = jnp.zeros_like(dgq_ref)
            dgk_ref[...] = jnp.zeros_like(dgk_ref)

        dgq_ref[...] += _colsum(dgq)
        dgk_ref[...] += _colsum(dgk)

    return pl.pallas_call(
        functools.partial(body), name="qk_bwd", grid=(t // tm,),
        in_specs=[pl.BlockSpec((tm, D_ATTN), lambda i: (i, 0)), pl.BlockSpec((tm, D_KV), lambda i: (i, 0)),
                  pl.BlockSpec((tm, D_KV), lambda i: (i, 0)), pl.BlockSpec((tm, D_QKV), lambda i: (i, 0)),
                  _full((1, D_ATTN)), _full((1, D_KV)),
                  pl.BlockSpec((tm, LANES), lambda i: (i, 0)), pl.BlockSpec((tm, LANES), lambda i: (i, 0)),
                  _full((D_ATTN, D_ATTN)), _full((D_KV, D_KV))],
        out_specs=[pl.BlockSpec((tm, D_QKV), lambda i: (i, 0)), _full((1, D_ATTN)), _full((1, D_KV))],
        out_shape=[jax.ShapeDtypeStruct((t, D_QKV), BF16), jax.ShapeDtypeStruct((1, D_ATTN), F32),
                   jax.ShapeDtypeStruct((1, D_KV), F32)],
        compiler_params=_params(1),
    )(dq_rot, dk_rot, dv, qkv, gq_w, gk_w, cos_w, sin_w, mean_q, mean_k)


def attention_fwd(q_t, k, v_t):
    _, nq, _, tq = q_t.shape
    _, nk, tk, _ = k.shape

    def body(q_ref, k_ref, v_ref, o_ref, lse_ref):
        q = q_ref[...]

        def step(j, carry):
            m, l, acc = carry
            s = _dot(k_ref[j], q)
            m_new = jnp.maximum(m, jnp.max(s, axis=0, keepdims=True))
            p = jnp.exp(s - m_new)
            alpha = jnp.exp(m - m_new)
            l = alpha * l + jnp.sum(p, axis=0, keepdims=True)
            acc = alpha * acc + _dot(v_ref[j], p.astype(BF16))
            return m_new, l, acc

        m, l, acc = lax.fori_loop(
            0, nk, step,
            (jnp.full((1, tq), -1e30, F32), jnp.zeros((1, tq), F32), jnp.zeros((HEAD_DIM, tq), F32)))
        o_ref[...] = acc / l
        lse_ref[...] = m + jnp.log(l)

    return pl.pallas_call(
        functools.partial(body), name="attention_fwd", grid=(N_HEADS, nq),
        in_specs=[pl.BlockSpec((None, None, HEAD_DIM, tq), lambda h, i: (h, i, 0, 0)),
                  pl.BlockSpec((None, nk, tk, HEAD_DIM), lambda h, i: (h // KV_GROUP, 0, 0, 0)),
                  pl.BlockSpec((None, nk, HEAD_DIM, tk), lambda h, i: (h // KV_GROUP, 0, 0, 0))],
        out_specs=[pl.BlockSpec((None, None, HEAD_DIM, tq), lambda h, i: (h, i, 0, 0)),
                   pl.BlockSpec((None, None, 1, tq), lambda h, i: (h, i, 0, 0))],
        out_shape=[jax.ShapeDtypeStruct((N_HEADS, nq, HEAD_DIM, tq), F32),
                   jax.ShapeDtypeStruct((N_HEADS, nq, 1, tq), F32)],
        compiler_params=_params(2),
    )(q_t, k, v_t)


def attention_delta(do_t, o_t):
    _, nq, _, tq = o_t.shape

    def body(do_ref, o_ref, d_ref):
        d_ref[...] = jnp.sum(do_ref[...] * o_ref[...], axis=0, keepdims=True)

    spec = pl.BlockSpec((None, None, HEAD_DIM, tq), lambda h, i: (h, i, 0, 0))
    return pl.pallas_call(
        functools.partial(body), name="attention_delta", grid=(N_HEADS, nq),
        in_specs=[spec, spec],
        out_specs=pl.BlockSpec((None, None, 1, tq), lambda h, i: (h, i, 0, 0)),
        out_shape=jax.ShapeDtypeStruct((N_HEADS, nq, 1, tq), F32),
        compiler_params=_params(2),
    )(do_t, o_t)


def attention_bwd(q_t, do_t, lse, delta, k, k_t, v):
    _, nq, _, tq = q_t.shape
    _, nk, tk, _ = k.shape

    def body(q_ref, do_ref, lse_ref, delta_ref, k_ref, kt_ref, v_ref, dq_ref, dk_ref, dv_ref):
        @pl.when(pl.program_id(1) == 0)
        def _():
            dq_ref[...] = jnp.zeros_like(dq_ref)

        kk, kt, vv = k_ref[...], kt_ref[...], v_ref[...]

        def step(it, carry):
            dk, dv = carry
            h = it // nq
            i = it % nq
            q = q_ref[h, i]
            do = do_ref[h, i]
            p = jnp.exp(_dot(kk, q) - lse_ref[h, i])
            ds = (p * (_dot(vv, do) - delta_ref[h, i])).astype(BF16)
            dv = dv + _dot_nt(do, p.astype(BF16))
            dk = dk + _dot_nt(q, ds)
            dq_ref[h, i] += _dot(kt, ds)
            return dk, dv

        zero = jnp.zeros((HEAD_DIM, tk), F32)
        dk, dv = lax.fori_loop(0, KV_GROUP * nq, step, (zero, zero))
        dk_ref[...] = dk
        dv_ref[...] = dv

    group = lambda g, j: (g, 0, 0, 0)
    tile = lambda g, j: (g, j, 0, 0)
    return pl.pallas_call(
        functools.partial(body), name="attention_bwd", grid=(N_KV_HEADS, nk),
        in_specs=[pl.BlockSpec((KV_GROUP, nq, HEAD_DIM, tq), group),
                  pl.BlockSpec((KV_GROUP, nq, HEAD_DIM, tq), group),
                  pl.BlockSpec((KV_GROUP, nq, 1, tq), group),
                  pl.BlockSpec((KV_GROUP, nq, 1, tq), group),
                  pl.BlockSpec((None, None, tk, HEAD_DIM), tile),
                  pl.BlockSpec((None, None, HEAD_DIM, tk), tile),
                  pl.BlockSpec((None, None, tk, HEAD_DIM), tile)],
        out_specs=[pl.BlockSpec((KV_GROUP, nq, HEAD_DIM, tq), group),
                   pl.BlockSpec((None, None, HEAD_DIM, tk), tile),
                   pl.BlockSpec((None, None, HEAD_DIM, tk), tile)],
        out_shape=[jax.ShapeDtypeStruct((N_HEADS, nq, HEAD_DIM, tq), F32),
                   jax.ShapeDtypeStruct((N_KV_HEADS, nk, HEAD_DIM, tk), F32),
                   jax.ShapeDtypeStruct((N_KV_HEADS, nk, HEAD_DIM, tk), F32)],
        compiler_params=_params(2),
    )(q_t, do_t, lse, delta, k, k_t, v)


def _group_select(parts):
    lane_group = lax.broadcasted_iota(jnp.int32, parts[0].shape, 1) // SGU_GROUP_DIM
    out = parts[0]
    for g in range(1, N_SGU_GROUPS):
        out = jnp.where(lane_group == g, parts[g], out)
    return out


def _gate_forward(z, g_sgu, ws_ref, bias):
    gz, th = _gelu(z)
    u, vv = gz[:, :D_SGU], gz[:, D_SGU:]
    rv = _rstd(vv)
    nv = vv * rv
    vn = (nv * g_sgu).astype(BF16)
    fs = []
    for c in range(z.shape[0] // CHUNK):
        vc = vn[c * CHUNK:(c + 1) * CHUNK]
        fs.append(_group_select([_dot(ws_ref[g], vc) for g in range(N_SGU_GROUPS)]) + bias)
    f = jnp.concatenate(fs, axis=0) if len(fs) > 1 else fs[0]
    return th, u, rv, nv, vn, f


def mix_out(z, o, x, g_sgu, g_ao, g_so, ws, bias, w_out, tm):
    t = x.shape[0]

    def body(z_ref, o_ref, x_ref, gs_ref, gao_ref, gso_ref, ws_ref, bias_ref, wout_ref, x2_ref, mixed_ref):
        _, u, _, _, _, f = _gate_forward(z_ref[...], gs_ref[...], ws_ref, bias_ref[...])
        sgu = u * f
        oo = o_ref[...]
        mixed = jnp.concatenate([oo * _rstd(oo) * gao_ref[...], sgu * _rstd(sgu) * gso_ref[...]], axis=-1).astype(BF16)
        mixed_ref[...] = mixed
        x2_ref[...] = x_ref[...] + _dot(mixed, wout_ref[...])

    row = lambda n: pl.BlockSpec((tm, n), lambda i: (i, 0))
    return pl.pallas_call(
        functools.partial(body), name="mix_out", grid=(t // tm,),
        in_specs=[row(2 * D_SGU), row(D_ATTN), row(D_MODEL), _full((1, D_SGU)), _full((1, D_ATTN)), _full((1, D_SGU)),
                  _full((N_SGU_GROUPS, CHUNK, CHUNK)), _full((CHUNK, D_SGU)), _full((D_MODEL, D_MODEL))],
        out_specs=[row(D_MODEL), row(D_MODEL)],
        out_shape=[jax.ShapeDtypeStruct((t, D_MODEL), F32), jax.ShapeDtypeStruct((t, D_MODEL), BF16)],
        compiler_params=_params(1),
    )(z, o, x, g_sgu, g_ao, g_so, ws, bias, w_out)


def mix_bwd(dx2, z, o, g_sgu, g_ao, g_so, ws, ws_t, bias, w_out, group_ind, tm):
    t = dx2.shape[0]
    n_tiles = t // tm

    def body(dx_ref, z_ref, o_ref, gs_ref, gao_ref, gso_ref, ws_ref, wst_ref, bias_ref, wout_ref, ind_ref,
             do_ref, dz_ref, dg_ref, dws_ref, dbs_ref, df_sum):
        step = pl.program_id(0)

        @pl.when(step == 0)
        def _():
            dg_ref[...] = jnp.zeros_like(dg_ref)
            dws_ref[...] = jnp.zeros_like(dws_ref)
            df_sum[...] = jnp.zeros_like(df_sum)

        z = z_ref[...]
        th, u, rv, nv, vn, f = _gate_forward(z, gs_ref[...], ws_ref, bias_ref[...])
        dmixed = _dot_nt(dx_ref[...].astype(BF16), wout_ref[...])
        oo = o_ref[...]
        ro = _rstd(oo)
        d_o, dgao = _rms_bwd(dmixed[:, :D_ATTN], oo * ro, ro, gao_ref[...])
        do_ref[...] = d_o
        sgu = u * f
        rs = _rstd(sgu)
        dsgu, dgso = _rms_bwd(dmixed[:, D_ATTN:], sgu * rs, rs, gso_ref[...])
        du = dsgu * f
        df = dsgu * u
        lane_group = lax.broadcasted_iota(jnp.int32, (CHUNK, D_SGU), 1) // SGU_GROUP_DIM
        dvns = []
        df_acc = jnp.zeros((CHUNK, D_SGU), F32)
        for c in range(tm // CHUNK):
            dfc32 = df[c * CHUNK:(c + 1) * CHUNK]
            dfc = dfc32.astype(BF16)
            vc = vn[c * CHUNK:(c + 1) * CHUNK]
            dvns.append(_group_select([_dot(wst_ref[g], dfc) for g in range(N_SGU_GROUPS)]))
            for g in range(N_SGU_GROUPS):
                dws_ref[g] += _dot_nt(jnp.where(lane_group == g, dfc, jnp.zeros_like(dfc)), vc)
            df_acc = df_acc + dfc32
        df_sum[...] += df_acc
        dvn = jnp.concatenate(dvns, axis=0) if len(dvns) > 1 else dvns[0]
        dvv, dgs = _rms_bwd(dvn, nv, rv, gs_ref[...])
        dz_ref[...] = (jnp.concatenate([du, dvv], axis=-1) * _gelu_grad(z, th)).astype(BF16)
        dg_ref[0:1, :] += _colsum(dgao)
        dg_ref[1:2, :] += _colsum(dgso)
        dg_ref[2:3, :] += _colsum(dgs)

        @pl.when(step == n_tiles - 1)
        def _():
            dbs_ref[...] = _dot_f32(df_sum[...], ind_ref[...])

    row = lambda n: pl.BlockSpec((tm, n), lambda i: (i, 0))
    return pl.pallas_call(
        functools.partial(body), name="mix_bwd", grid=(n_tiles,),
        in_specs=[row(D_MODEL), row(2 * D_SGU), row(D_ATTN), _full((1, D_SGU)), _full((1, D_ATTN)), _full((1, D_SGU)),
                  _full((N_SGU_GROUPS, CHUNK, CHUNK)), _full((N_SGU_GROUPS, CHUNK, CHUNK)), _full((CHUNK, D_SGU)),
                  _full((D_MODEL, D_MODEL)), _full((D_SGU, LANES))],
        out_specs=[row(D_ATTN), row(2 * D_SGU), _full((8, D_SGU)), _full((N_SGU_GROUPS, CHUNK, CHUNK)),
                   _full((CHUNK, LANES))],
        out_shape=[jax.ShapeDtypeStruct((t, D_ATTN), F32), jax.ShapeDtypeStruct((t, 2 * D_SGU), BF16),
                   jax.ShapeDtypeStruct((8, D_SGU), F32),
                   jax.ShapeDtypeStruct((N_SGU_GROUPS, CHUNK, CHUNK), F32),
                   jax.ShapeDtypeStruct((CHUNK, LANES), F32)],
        scratch_shapes=[pltpu.VMEM((CHUNK, D_SGU), F32)],
        compiler_params=_params(1),
    )(dx2, z, o, g_sgu, g_ao, g_so, ws, ws_t, bias, w_out, group_ind)


def loss_bwd(x, g, target, tm):
    t = x.shape[0]

    def body(x_ref, g_ref, t_ref, loss_ref, dx_ref, dg_ref):
        @pl.when(pl.program_id(0) == 0)
        def _():
            loss_ref[...] = jnp.zeros_like(loss_ref)
            dg_ref[...] = jnp.zeros_like(dg_ref)

        xx = x_ref[...]
        r = _rstd(xx)
        n = xx * r
        err = n * g_ref[...] - t_ref[...]
        per_token = jnp.mean(err * err, axis=-1, keepdims=True)
        loss_ref[...] += 0.5 * jnp.sum(per_token, axis=0, keepdims=True)
        dx, dg_rows = _rms_bwd(err * (1.0 / D_MODEL), n, r, g_ref[...])
        dx_ref[...] = dx
        dg_ref[...] += _colsum(dg_rows)

    row = pl.BlockSpec((tm, D_MODEL), lambda i: (i, 0))
    return pl.pallas_call(
        functools.partial(body), name="loss_bwd", grid=(t // tm,),
        in_specs=[row, _full((1, D_MODEL)), row],
        out_specs=[_full((1, LANES)), row, _full((1, D_MODEL))],
        out_shape=[jax.ShapeDtypeStruct((1, LANES), F32), jax.ShapeDtypeStruct((t, D_MODEL), F32),
                   jax.ShapeDtypeStruct((1, D_MODEL), F32)],
        compiler_params=_params(1),
    )(x, g, target)


def _rope_tables(t):
    rows = t // GRID_W
    row_idx = jnp.repeat(jnp.arange(rows, dtype=F32), GRID_W)
    col_idx = jnp.tile(jnp.arange(GRID_W, dtype=F32), rows)
    axis_dim = HEAD_DIM // 2
    inv = 1.0 / (ROPE_THETA ** (jnp.arange(0, axis_dim, 2, dtype=F32) / axis_dim))
    ang = jnp.concatenate([row_idx[:, None] * inv, col_idx[:, None] * inv], axis=-1)
    cos = jnp.repeat(jnp.cos(ang), 2, axis=-1)
    sin = jnp.repeat(jnp.sin(ang), 2, axis=-1) * jnp.tile(jnp.array([-1.0, 1.0], F32), HEAD_DIM // 2)
    return jnp.tile(cos, (1, LANES // HEAD_DIM)), jnp.tile(sin, (1, LANES // HEAD_DIM))


def _heads_to_tiles_t(a, n_heads, tile):
    t = a.shape[0]
    return a.reshape(t // tile, tile, n_heads, HEAD_DIM).transpose(2, 0, 3, 1)


def _heads_to_tiles(a, n_heads, tile):
    t = a.shape[0]
    return a.reshape(t // tile, tile, n_heads, HEAD_DIM).transpose(2, 0, 1, 3)


def _tiles_t_to_heads(a):
    h, n, _, tile = a.shape
    return a.transpose(1, 3, 0, 2).reshape(n * tile, h * HEAD_DIM)


def kernel(x, g_ffn1, w1_gate, w1_up, w1_down, g_mix, w_in, g_q, g_k, g_sgu, w_s, b_s, g_attn_out, g_sgu_out, w_out, g_ffn2, w2_gate, w2_up, w2_down, g_final, loss_target, m_g_ffn1, m_w1_gate, m_w1_up, m_w1_down, m_g_mix, m_w_in, m_g_q, m_g_k, m_g_sgu, m_w_s, m_b_s, m_g_attn_out, m_g_sgu_out, m_w_out, m_g_ffn2, m_w2_gate, m_w2_up, m_w2_down, m_g_final, v_g_ffn1, v_w1_gate, v_w1_up, v_w1_down, v_g_mix, v_w_in, v_g_q, v_g_k, v_g_sgu, v_w_s, v_b_s, v_g_attn_out, v_g_sgu_out, v_w_out, v_g_ffn2, v_w2_gate, v_w2_up, v_w2_down, v_g_final):
    weights = dict(g_ffn1=g_ffn1, w1_gate=w1_gate, w1_up=w1_up, w1_down=w1_down, g_mix=g_mix, w_in=w_in, g_q=g_q,
                   g_k=g_k, g_sgu=g_sgu, w_s=w_s, b_s=b_s, g_attn_out=g_attn_out, g_sgu_out=g_sgu_out, w_out=w_out,
                   g_ffn2=g_ffn2, w2_gate=w2_gate, w2_up=w2_up, w2_down=w2_down, g_final=g_final)
    m_in = dict(g_ffn1=m_g_ffn1, w1_gate=m_w1_gate, w1_up=m_w1_up, w1_down=m_w1_down, g_mix=m_g_mix, w_in=m_w_in,
                g_q=m_g_q, g_k=m_g_k, g_sgu=m_g_sgu, w_s=m_w_s, b_s=m_b_s, g_attn_out=m_g_attn_out,
                g_sgu_out=m_g_sgu_out, w_out=m_w_out, g_ffn2=m_g_ffn2, w2_gate=m_w2_gate, w2_up=m_w2_up,
                w2_down=m_w2_down, g_final=m_g_final)
    v_in = dict(g_ffn1=v_g_ffn1, w1_gate=v_w1_gate, w1_up=v_w1_up, w1_down=v_w1_down, g_mix=v_g_mix, w_in=v_w_in,
                g_q=v_g_q, g_k=v_g_k, g_sgu=v_g_sgu, w_s=v_w_s, b_s=v_b_s, g_attn_out=v_g_attn_out,
                g_sgu_out=v_g_sgu_out, w_out=v_w_out, g_ffn2=v_g_ffn2, w2_gate=v_w2_gate, w2_up=v_w2_up,
                w2_down=v_w2_down, g_final=v_g_final)
    names = list(weights)

    t = x.shape[1]
    x0 = x[0]
    target = loss_target[0]
    tm = min(256, t)
    tm_ff = min(512, t)
    tn_ff = 256
    tq = min(512, t)
    tk = min(256, t)
    tk_w = min(512, t)

    def shard_rows(name):
        w = weights[name][0]
        return (w.T if name in TRANSPOSED else w).astype(BF16)

    gathered = all_gather_weights(jnp.concatenate([shard_rows(n) for n, _ in SHARD_ROWS], axis=0))
    full = {n: g.reshape(N_DEV * r, D_MODEL) for (n, r), g in zip(SHARD_ROWS, gathered)}
    w_in_t = full["w_in"]
    w_qkv_t, w_z_t = w_in_t[:D_QKV], w_in_t[D_QKV:]

    h1, a1, b1, act1 = ffn_up(x0, g_ffn1, full["w1_gate"], full["w1_up"], tm_ff, tn_ff)
    x1 = ffn_down(act1, full["w1_down"], x0, tm)

    qkv, h2 = norm_matmul(x1, g_mix, w_qkv_t, tm_ff, 256, True)
    z = norm_matmul(x1, g_mix, w_z_t, tm_ff, 256, False)
    cos_w, sin_w = _rope_tables(t)
    gq_w = jnp.tile(g_q, (1, N_HEADS))
    gk_w = jnp.tile(g_k, (1, N_KV_HEADS))
    mean_q, mean_k = _head_mean_matrix(D_ATTN), _head_mean_matrix(D_KV)
    q_rot, k_rot, v_b = qk_prep(qkv, gq_w, gk_w, cos_w, sin_w, mean_q, mean_k, tm)
    q_t = _heads_to_tiles_t(q_rot, N_HEADS, tq)
    k_tiles = _heads_to_tiles(k_rot, N_KV_HEADS, tk)
    kt_tiles = _heads_to_tiles_t(k_rot, N_KV_HEADS, tk)
    v_tiles = _heads_to_tiles(v_b, N_KV_HEADS, tk)
    vt_tiles = _heads_to_tiles_t(v_b, N_KV_HEADS, tk)
    o_t, lse = attention_fwd(q_t, k_tiles, vt_tiles)
    o = _tiles_t_to_heads(o_t)

    ws_b = w_s[0].astype(BF16)
    ws_tb = jnp.swapaxes(w_s[0], 1, 2).astype(BF16)
    bias = jnp.repeat(b_s[0].T, SGU_GROUP_DIM, axis=1)
    x2, mixed = mix_out(z, o, x1, g_sgu, g_attn_out, g_sgu_out, ws_b, bias, full["w_out"], tm)

    h3, a2, b2, act2 = ffn_up(x2, g_ffn2, full["w2_gate"], full["w2_up"], tm_ff, tn_ff)
    x3 = ffn_down(act2, full["w2_down"], x2, tm)

    loss_part, dx3, dg_final = loss_bwd(x3, g_final, target, tm)
    loss = lax.psum(loss_part[0, 0], MESH_AXES)

    def ffn_backward(dx_out, h, a, b, act, x_in, g, wg_t, wu_t, wd):
        da, db = ffn_bwd_act(dx_out, wd, a, b, tm_ff, tn_ff)
        dx_in, dg = norm_bwd_matmul(da, wg_t, db, wu_t, x_in, g, dx_out, tm)
        tmm = D_FF // 2
        return dx_in, dg, matmul_tn(da, h, 1.0, tmm, tk_w), matmul_tn(db, h, 1.0, tmm, tk_w), \
            matmul_tn(act, dx_out, 0.5, tmm, tk_w)

    dx2, dg_ffn2, dwg2, dwu2, dwd2 = ffn_backward(dx3, h3, a2, b2, act2, x2, g_ffn2, full["w2_gate"], full["w2_up"],
                                                  full["w2_down"])

    group_ind = (jnp.arange(D_SGU)[:, None] // SGU_GROUP_DIM == jnp.arange(LANES)[None, :]).astype(F32)
    d_o, dz, dg_mixrow, dws, dbs = mix_bwd(dx2, z, o, g_sgu, g_attn_out, g_sgu_out, ws_b, ws_tb, bias, full["w_out"],
                                           group_ind, tm)
    dw_out = matmul_tn(mixed, dx2, 1.0, D_MODEL // 2, tk_w)

    do_t = _heads_to_tiles_t(d_o, N_HEADS, tq)
    delta = attention_delta(do_t, o_t)
    dq_t, dk_t, dv_t = attention_bwd(q_t, do_t.astype(BF16), lse, delta, k_tiles, kt_tiles, v_tiles)
    dqkv, dgq_w, dgk_w = qk_bwd(_tiles_t_to_heads(dq_t), _tiles_t_to_heads(dk_t), _tiles_t_to_heads(dv_t), qkv,
                                gq_w, gk_w, cos_w, sin_w, mean_q, mean_k, tm)
    dx1, dg_mix = norm_bwd_matmul(dqkv, w_qkv_t, dz, w_z_t, x1, g_mix, dx2, tm)
    dw_in = jnp.concatenate([matmul_tn(dqkv, h2, 1.0, D_QKV // 2, tk_w), matmul_tn(dz, h2, 1.0, D_SGU, tk_w)], axis=0)

    dx0, dg_ffn1, dwg1, dwu1, dwd1 = ffn_backward(dx1, h1, a1, b1, act1, x0, g_ffn1, full["w1_gate"], full["w1_up"],
                                                  full["w1_down"])

    small_grads = dict(
        g_ffn1=dg_ffn1, g_mix=dg_mix, g_ffn2=dg_ffn2, g_final=dg_final,
        g_q=dgq_w.reshape(N_HEADS, HEAD_DIM).sum(0)[None], g_k=dgk_w.reshape(N_KV_HEADS, HEAD_DIM).sum(0)[None],
        g_attn_out=dg_mixrow[0:1], g_sgu_out=dg_mixrow[1:2], g_sgu=dg_mixrow[2:3],
        w_s=dws[None], b_s=dbs[:, :N_SGU_GROUPS].T[None])
    small_names = [n for n in names if n not in dict(SHARD_ROWS)]
    pieces = []
    for n in small_names:
        flat = small_grads[n].reshape(-1)
        flat = jnp.pad(flat, (0, (-flat.shape[0]) % LANES))
        pieces.append(flat.reshape(-1, LANES))
    small_rows = [p.shape[0] for p in pieces]
    pad_rows = (-sum(small_rows)) % 8
    small_pack = jnp.concatenate(pieces + [jnp.zeros((pad_rows, LANES), F32)], axis=0)

    big = dict(w1_gate=dwg1, w1_up=dwu1, w1_down=dwd1, w_in=dw_in, w_out=dw_out, w2_gate=dwg2, w2_up=dwu2, w2_down=dwd2)
    parts, small_parts = exchange_grads([big[n] for n, _ in SHARD_ROWS], small_pack)
    summed = sum_parts(parts, 32)
    small_sum = sum_parts(small_parts, small_pack.shape[0])

    grads = {}
    off = 0
    for n, r in SHARD_ROWS:
        gsh = summed[off:off + r]
        grads[n] = (gsh.T if n in TRANSPOSED else gsh)[None]
        off += r
    off = 0
    for n, r in zip(small_names, small_rows):
        grads[n] = small_sum[off:off + r].reshape(-1)[:weights[n].size].reshape(weights[n].shape)
        off += r

    delta_w, new_m, new_v = {}, {}, {}
    for n in names:
        shape = weights[n].shape
        as2d = (lambda a: a.reshape(-1, shape[-1]))
        d, m2, v2 = adamw(as2d(weights[n]), as2d(grads[n]), as2d(m_in[n]), as2d(v_in[n]))
        delta_w[n], new_m[n], new_v[n] = d.reshape(shape), m2.reshape(shape), v2.reshape(shape)

    return (loss, dx0[None], *[grads[n] for n in names], *[delta_w[n] for n in names],
            *[new_m[n] for n in names], *[new_v[n] for n in names])
```

```python
import functools
import math

import jax
import jax.numpy as jnp
from jax import lax
from jax.experimental import pallas as pl
from jax.experimental.pallas import tpu as pltpu

F32 = jnp.float32
BF16 = jnp.bfloat16

D_MODEL = 1024
D_FF = 2816
N_HEADS = 8
HEAD_DIM = 64
N_KV_HEADS = 2
KV_GROUP = N_HEADS // N_KV_HEADS
D_ATTN = N_HEADS * HEAD_DIM
D_KV = N_KV_HEADS * HEAD_DIM
D_QKV = D_ATTN + 2 * D_KV
N_SGU_GROUPS = 8
SGU_GROUP_DIM = 64
D_SGU = N_SGU_GROUPS * SGU_GROUP_DIM
CHUNK = 128
GRID_W = 64
ROPE_THETA = 10000.0
EPS = 1e-6
N_DEV = 8
LANES = 128

LOG2_E = math.log2(math.e)
Q_SCALE = HEAD_DIM ** -0.5 * LOG2_E

ADAM_LR = 0.001
ADAM_B1 = 0.9
ADAM_B2 = 0.999
ADAM_EPS = 1e-08
ADAM_WD = 0.01
ADAM_STEP = 10

MESH_AXES = ("x", "y", "c")
MESH_IDS = pl.DeviceIdType.MESH

VMEM_LIMIT = 56 * 1024 * 1024

SHARD_ROWS = (("w1_gate", D_FF // N_DEV), ("w1_up", D_FF // N_DEV), ("w1_down", D_FF // N_DEV),
              ("w_in", (D_QKV + 2 * D_SGU) // N_DEV), ("w_out", D_MODEL // N_DEV),
              ("w2_gate", D_FF // N_DEV), ("w2_up", D_FF // N_DEV), ("w2_down", D_FF // N_DEV))
PACK_ROWS = sum(r for _, r in SHARD_ROWS)
TRANSPOSED = ("w1_gate", "w1_up", "w_in", "w2_gate", "w2_up")


def _params(n_grid):
    return pltpu.CompilerParams(dimension_semantics=("arbitrary",) * n_grid, vmem_limit_bytes=VMEM_LIMIT)


def _dot(a, b):
    return jnp.dot(a, b, preferred_element_type=F32)


def _dot_nt(a, b):
    return lax.dot_general(a, b, (((1,), (1,)), ((), ())), preferred_element_type=F32)


def _dot_tn(a, b):
    return lax.dot_general(a, b, (((0,), (0,)), ((), ())), preferred_element_type=F32)


def _dot_f32(a, b):
    return jnp.dot(a, b, preferred_element_type=F32, precision=lax.Precision.HIGHEST)


def _rstd(x):
    return lax.rsqrt(jnp.mean(x * x, axis=-1, keepdims=True) + EPS)


def _rms_bwd(dy, n, r, g):
    dn = dy * g
    return r * (dn - n * jnp.mean(dn * n, axis=-1, keepdims=True)), dy * n


def _colsum(a):
    return jnp.sum(a, axis=0, keepdims=True)


_GELU_C = math.sqrt(2.0 / math.pi)


def _gelu(x):
    t = jnp.tanh(_GELU_C * (x + 0.044715 * (x * x * x)))
    return x * (0.5 * (1.0 + t)), t


def _gelu_grad(x, t):
    return 0.5 * (1.0 + t) + 0.5 * x * (1.0 - t * t) * (_GELU_C * (1.0 + 3 * 0.044715 * x * x))


def _pair_swap(a):
    w = a.shape[-1]
    lane = lax.broadcasted_iota(jnp.int32, a.shape, a.ndim - 1)
    return jnp.where(lane % 2 == 0, pltpu.roll(a, w - 1, a.ndim - 1), pltpu.roll(a, 1, a.ndim - 1))


def _tile_lanes(a, reps):
    return jnp.concatenate([a] * reps, axis=-1) if reps > 1 else a


def _loop_pairs(n, step, carry):
    assert n % 2 == 0, n

    def pair(jj, c):
        return step(2 * jj + 1, 1, step(2 * jj, 0, c))

    return lax.fori_loop(0, n // 2, pair, carry)


def _full(shape):
    nd = len(shape)
    return pl.BlockSpec(shape, lambda *_: (0,) * nd)


def _mesh_pos():
    return lax.axis_index("x"), lax.axis_index("y"), lax.axis_index("c")


def _peer(pos, d):
    x, y, c = pos
    px = 1 - x if d & 4 else x
    py = 1 - y if d & 2 else y
    pc = 1 - c if d & 1 else c
    return (px, py, pc), 4 * px + 2 * py + pc


def all_gather_weights(packed):
    offs = []
    o = 0
    for _, r in SHARD_ROWS:
        offs.append(o)
        o += r

    def body(src, *refs):
        outs = refs[:len(SHARD_ROWS)]
        send_sems, recv_sems, local_sem = refs[len(SHARD_ROWS):]
        pos = _mesh_pos()
        me = 4 * pos[0] + 2 * pos[1] + pos[2]
        local = [pltpu.make_async_copy(src.at[pl.ds(offs[w], r), :], outs[w].at[me], local_sem.at[w])
                 for w, (_, r) in enumerate(SHARD_ROWS)]
        for cp in local:
            cp.start()
        for d in range(1, N_DEV):
            peer, _ = _peer(pos, d)
            for w, (_, r) in enumerate(SHARD_ROWS):
                pltpu.make_async_remote_copy(
                    src_ref=src.at[pl.ds(offs[w], r), :], dst_ref=outs[w].at[me],
                    send_sem=send_sems.at[d - 1], recv_sem=recv_sems.at[d - 1],
                    device_id=peer, device_id_type=MESH_IDS).start()
        for d in range(1, N_DEV):
            peer, _ = _peer(pos, d)
            everything = pltpu.make_async_remote_copy(
                src_ref=src, dst_ref=src, send_sem=send_sems.at[d - 1], recv_sem=recv_sems.at[d - 1],
                device_id=peer, device_id_type=MESH_IDS)
            everything.wait_send()
            everything.wait_recv()
        for cp in local:
            cp.wait()

    any_spec = pl.BlockSpec(memory_space=pl.ANY)
    return pl.pallas_call(
        functools.partial(body),
        name="all_gather_weights",
        out_shape=[jax.ShapeDtypeStruct((N_DEV, r, D_MODEL), BF16) for _, r in SHARD_ROWS],
        in_specs=[any_spec],
        out_specs=[any_spec] * len(SHARD_ROWS),
        scratch_shapes=[pltpu.SemaphoreType.DMA((N_DEV - 1,)), pltpu.SemaphoreType.DMA((N_DEV - 1,)),
                        pltpu.SemaphoreType.DMA((len(SHARD_ROWS),))],
        compiler_params=pltpu.CompilerParams(has_side_effects=True),
    )(packed)


def exchange_grads(grads, small):
    offs = []
    o = 0
    for _, r in SHARD_ROWS:
        offs.append(o)
        o += r
    n_w = len(SHARD_ROWS)

    def body(*refs):
        gs = refs[:n_w]
        small_ref, parts, small_parts, send_sems, recv_sems, small_send, small_recv, local_sem = refs[n_w:]
        pos = _mesh_pos()
        me = 4 * pos[0] + 2 * pos[1] + pos[2]
        local = [pltpu.make_async_copy(gs[w].at[pl.ds(pl.multiple_of(me * r, 16), r), :],
                                       parts.at[me, pl.ds(offs[w], r), :], local_sem.at[w])
                 for w, (_, r) in enumerate(SHARD_ROWS)]
        local.append(pltpu.make_async_copy(small_ref, small_parts.at[me], local_sem.at[n_w]))
        for cp in local:
            cp.start()
        small_copies = []
        for d in range(1, N_DEV):
            peer, peer_lin = _peer(pos, d)
            for w, (_, r) in enumerate(SHARD_ROWS):
                pltpu.make_async_remote_copy(
                    src_ref=gs[w].at[pl.ds(pl.multiple_of(peer_lin * r, 16), r), :],
                    dst_ref=parts.at[me, pl.ds(offs[w], r), :],
                    send_sem=send_sems.at[d - 1], recv_sem=recv_sems.at[d - 1],
                    device_id=peer, device_id_type=MESH_IDS).start()
            cp = pltpu.make_async_remote_copy(
                src_ref=small_ref, dst_ref=small_parts.at[me],
                send_sem=small_send.at[d - 1], recv_sem=small_recv.at[d - 1],
                device_id=peer, device_id_type=MESH_IDS)
            cp.start()
            small_copies.append(cp)
        for d in range(1, N_DEV):
            peer, peer_lin = _peer(pos, d)
            everything = pltpu.make_async_remote_copy(
                src_ref=parts.at[peer_lin], dst_ref=parts.at[peer_lin],
                send_sem=send_sems.at[d - 1], recv_sem=recv_sems.at[d - 1],
                device_id=peer, device_id_type=MESH_IDS)
            everything.wait_send()
            everything.wait_recv()
            small_copies[d - 1].wait_send()
            small_copies[d - 1].wait_recv()
        for cp in local:
            cp.wait()

    any_spec = pl.BlockSpec(memory_space=pl.ANY)
    return pl.pallas_call(
        functools.partial(body),
        name="exchange_grads",
        out_shape=[jax.ShapeDtypeStruct((N_DEV, PACK_ROWS, D_MODEL), BF16),
                   jax.ShapeDtypeStruct((N_DEV,) + small.shape, F32)],
        in_specs=[any_spec] * (n_w + 1),
        out_specs=[any_spec, any_spec],
        scratch_shapes=[pltpu.SemaphoreType.DMA((N_DEV - 1,)), pltpu.SemaphoreType.DMA((N_DEV - 1,)),
                        pltpu.SemaphoreType.DMA((N_DEV - 1,)), pltpu.SemaphoreType.DMA((N_DEV - 1,)),
                        pltpu.SemaphoreType.DMA((n_w + 1,))],
        compiler_params=pltpu.CompilerParams(has_side_effects=True),
    )(*grads, small)


def sum_parts(parts, block_rows):
    n, rows, cols = parts.shape

    def body(p_ref, o_ref):
        acc = p_ref[0].astype(F32)
        for s in range(1, n):
            acc = acc + p_ref[s].astype(F32)
        o_ref[...] = acc

    return pl.pallas_call(
        functools.partial(body), name="sum_parts",
        grid=(rows // block_rows,),
        in_specs=[pl.BlockSpec((n, block_rows, cols), lambda i: (0, i, 0))],
        out_specs=pl.BlockSpec((block_rows, cols), lambda i: (i, 0)),
        out_shape=jax.ShapeDtypeStruct((rows, cols), F32),
        compiler_params=_params(1),
    )(parts)


def adamw(w, g, m, v):
    def body(w_ref, g_ref, m_ref, v_ref, d_ref, m_out, v_out):
        gg = g_ref[...]
        m2 = ADAM_B1 * m_ref[...] + (1.0 - ADAM_B1) * gg
        v2 = ADAM_B2 * v_ref[...] + (1.0 - ADAM_B2) * (gg * gg)
        m_hat = m2 / (1.0 - ADAM_B1 ** ADAM_STEP)
        v_hat = v2 / (1.0 - ADAM_B2 ** ADAM_STEP)
        d_ref[...] = -ADAM_LR * (m_hat / (jnp.sqrt(v_hat) + ADAM_EPS) + ADAM_WD * w_ref[...])
        m_out[...] = m2
        v_out[...] = v2

    spec = _full(w.shape)
    shape = jax.ShapeDtypeStruct(w.shape, F32)
    return pl.pallas_call(
        functools.partial(body), name="adamw",
        in_specs=[spec] * 4, out_specs=[spec] * 3, out_shape=[shape] * 3,
        compiler_params=pltpu.CompilerParams(vmem_limit_bytes=VMEM_LIMIT),
    )(w, g, m, v)


def ffn_up(x, g, wg_t, wu_t, tm, tn):
    t = x.shape[0]

    def body(x_ref, g_ref, wg_ref, wu_ref, h_ref, a_ref, b_ref, act_ref):
        @pl.when(pl.program_id(1) == 0)
        def _():
            xx = x_ref[...]
            h_ref[...] = ((xx * _rstd(xx)) * g_ref[...]).astype(BF16)

        h = h_ref[...]
        a = _dot_nt(h, wg_ref[...])
        b = _dot_nt(h, wu_ref[...])
        a_ref[...] = a.astype(BF16)
        b_ref[...] = b.astype(BF16)
        act_ref[...] = (a * jax.nn.sigmoid(a) * b).astype(BF16)

    wide = jax.ShapeDtypeStruct((t, D_FF), BF16)
    tile = pl.BlockSpec((tm, tn), lambda i, j: (i, j))
    return pl.pallas_call(
        functools.partial(body), name="ffn_up",
        grid=(t // tm, D_FF // tn),
        in_specs=[pl.BlockSpec((tm, D_MODEL), lambda i, j: (i, 0)), _full((1, D_MODEL)),
                  pl.BlockSpec((tn, D_MODEL), lambda i, j: (j, 0)), pl.BlockSpec((tn, D_MODEL), lambda i, j: (j, 0))],
        out_specs=[pl.BlockSpec((tm, D_MODEL), lambda i, j: (i, 0)), tile, tile, tile],
        out_shape=[jax.ShapeDtypeStruct((t, D_MODEL), BF16), wide, wide, wide],
        compiler_params=_params(2),
    )(x, g, wg_t, wu_t)


def ffn_down(act, wd, x, tm):
    t = x.shape[0]

    def body(act_ref, wd_ref, x_ref, o_ref):
        o_ref[...] = x_ref[...] + 0.5 * _dot(act_ref[...], wd_ref[...])

    return pl.pallas_call(
        functools.partial(body), name="ffn_down",
        grid=(t // tm,),
        in_specs=[pl.BlockSpec((tm, D_FF), lambda i: (i, 0)), _full((D_FF, D_MODEL)),
                  pl.BlockSpec((tm, D_MODEL), lambda i: (i, 0))],
        out_specs=pl.BlockSpec((tm, D_MODEL), lambda i: (i, 0)),
        out_shape=jax.ShapeDtypeStruct((t, D_MODEL), F32),
        compiler_params=_params(1),
    )(act, wd, x)


def ffn_bwd_act(dx, wd, a, b, tm, tn):
    t = dx.shape[0]

    def body(dx_ref, wd_ref, a_ref, b_ref, da_ref, db_ref, dxb_ref):
        @pl.when(pl.program_id(1) == 0)
        def _():
            dxb_ref[...] = dx_ref[...].astype(BF16)

        dact = 0.5 * _dot_nt(dxb_ref[...], wd_ref[...])
        aa = a_ref[...].astype(F32)
        sig = jax.nn.sigmoid(aa)
        da_ref[...] = (dact * b_ref[...].astype(F32) * (sig * (1.0 + aa * (1.0 - sig)))).astype(BF16)
        db_ref[...] = (dact * (aa * sig)).astype(BF16)

    wide = jax.ShapeDtypeStruct((t, D_FF), BF16)
    tile = pl.BlockSpec((tm, tn), lambda i, j: (i, j))
    return pl.pallas_call(
        functools.partial(body), name="ffn_bwd_act",
        grid=(t // tm, D_FF // tn),
        in_specs=[pl.BlockSpec((tm, D_MODEL), lambda i, j: (i, 0)),
                  pl.BlockSpec((tn, D_MODEL), lambda i, j: (j, 0)), tile, tile],
        out_specs=[tile, tile],
        out_shape=[wide, wide],
        scratch_shapes=[pltpu.VMEM((tm, D_MODEL), BF16)],
        compiler_params=_params(2),
    )(dx, wd, a, b)


def norm_bwd_matmul(a1, w1, a2, w2, x, g, dx_in, tm):
    t = x.shape[0]
    k1, k2 = a1.shape[1], a2.shape[1]

    def body(a1_ref, w1_ref, a2_ref, w2_ref, x_ref, g_ref, dxin_ref, dx_ref, dg_ref):
        dh = _dot(a1_ref[...], w1_ref[...]) + _dot(a2_ref[...], w2_ref[...])
        xx = x_ref[...]
        r = _rstd(xx)
        dx, dg_rows = _rms_bwd(dh, xx * r, r, g_ref[...])
        dx_ref[...] = dxin_ref[...] + dx

        @pl.when(pl.program_id(0) == 0)
        def _():
            dg_ref[...] = jnp.zeros_like(dg_ref)

        dg_ref[...] += _colsum(dg_rows)

    row = pl.BlockSpec((tm, D_MODEL), lambda i: (i, 0))
    return pl.pallas_call(
        functools.partial(body), name="norm_bwd_matmul",
        grid=(t // tm,),
        in_specs=[pl.BlockSpec((tm, k1), lambda i: (i, 0)), _full((k1, D_MODEL)),
                  pl.BlockSpec((tm, k2), lambda i: (i, 0)), _full((k2, D_MODEL)),
                  row, _full((1, D_MODEL)), row],
        out_specs=[row, _full((1, D_MODEL))],
        out_shape=[jax.ShapeDtypeStruct((t, D_MODEL), F32), jax.ShapeDtypeStruct((1, D_MODEL), F32)],
        compiler_params=_params(1),
    )(a1, w1, a2, w2, x, g, dx_in)


def matmul_tn(a, b, scale, tmm, tk):
    t, m = a.shape
    n = b.shape[1]
    nk = t // tk

    def body(a_ref, b_ref, o_ref, acc_ref):
        k = pl.program_id(1)

        @pl.when(k == 0)
        def _():
            acc_ref[...] = jnp.zeros_like(acc_ref)

        acc_ref[...] += _dot_tn(a_ref[...].astype(BF16), b_ref[...].astype(BF16))

        @pl.when(k == nk - 1)
        def _():
            o_ref[...] = (scale * acc_ref[...]).astype(BF16)

    return pl.pallas_call(
        functools.partial(body), name="matmul_tn",
        grid=(m // tmm, nk),
        in_specs=[pl.BlockSpec((tk, tmm), lambda i, k: (k, i)), pl.BlockSpec((tk, n), lambda i, k: (k, 0))],
        out_specs=pl.BlockSpec((tmm, n), lambda i, k: (i, 0)),
        out_shape=jax.ShapeDtypeStruct((m, n), BF16),
        scratch_shapes=[pltpu.VMEM((tmm, n), F32)],
        compiler_params=_params(2),
    )(a, b)


def norm_matmul(x, g, w_t, tm, tn, with_h):
    t = x.shape[0]
    n = w_t.shape[0]

    def body(x_ref, g_ref, w_ref, o_ref, h_ref):
        @pl.when(pl.program_id(1) == 0)
        def _():
            xx = x_ref[...]
            h_ref[...] = ((xx * _rstd(xx)) * g_ref[...]).astype(BF16)

        o_ref[...] = _dot_nt(h_ref[...], w_ref[...])

    in_specs = [pl.BlockSpec((tm, D_MODEL), lambda i, j: (i, 0)), _full((1, D_MODEL)),
                pl.BlockSpec((tn, D_MODEL), lambda i, j: (j, 0))]
    o_spec = pl.BlockSpec((tm, tn), lambda i, j: (i, j))
    o_shape = jax.ShapeDtypeStruct((t, n), F32)
    h_spec = pl.BlockSpec((tm, D_MODEL), lambda i, j: (i, 0))
    if with_h:
        return pl.pallas_call(
            functools.partial(body), name="norm_matmul_h", grid=(t // tm, n // tn),
            in_specs=in_specs, out_specs=[o_spec, h_spec],
            out_shape=[o_shape, jax.ShapeDtypeStruct((t, D_MODEL), BF16)],
            compiler_params=_params(2))(x, g, w_t)
    return pl.pallas_call(
        functools.partial(body), name="norm_matmul", grid=(t // tm, n // tn),
        in_specs=in_specs, out_specs=o_spec, out_shape=o_shape,
        scratch_shapes=[pltpu.VMEM((tm, D_MODEL), BF16)],
        compiler_params=_params(2))(x, g, w_t)


def _head_mean_matrix(width):
    head = jnp.arange(width) // HEAD_DIM
    return (head[:, None] == head[None, :]).astype(F32) / HEAD_DIM


def qk_prep(qkv, gq_w, gk_w, cos_w, sin_w, mean_q, mean_k, tm):
    t = qkv.shape[0]

    def body(p_ref, gq_ref, gk_ref, cos_ref, sin_ref, mq_ref, mk_ref, q_ref, k_ref, v_ref):
        cos2, sin2 = cos_ref[...], sin_ref[...]
        q = p_ref[:, :D_ATTN]
        k = p_ref[:, D_ATTN:D_ATTN + D_KV]
        qn = q * lax.rsqrt(_dot_f32(q * q, mq_ref[...]) + EPS) * gq_ref[...]
        kn = k * lax.rsqrt(_dot_f32(k * k, mk_ref[...]) + EPS) * gk_ref[...]
        cos8, sin8 = _tile_lanes(cos2, D_ATTN // LANES), _tile_lanes(sin2, D_ATTN // LANES)
        q_ref[...] = ((qn * cos8 + _pair_swap(qn) * sin8) * Q_SCALE).astype(BF16)
        k_ref[...] = (kn * cos2 + _pair_swap(kn) * sin2).astype(BF16)
        v_ref[...] = p_ref[:, D_ATTN + D_KV:].astype(BF16)

    return pl.pallas_call(
        functools.partial(body), name="qk_prep", grid=(t // tm,),
        in_specs=[pl.BlockSpec((tm, D_QKV), lambda i: (i, 0)), _full((1, D_ATTN)), _full((1, D_KV)),
                  pl.BlockSpec((tm, LANES), lambda i: (i, 0)), pl.BlockSpec((tm, LANES), lambda i: (i, 0)),
                  _full((D_ATTN, D_ATTN)), _full((D_KV, D_KV))],
        out_specs=[pl.BlockSpec((tm, D_ATTN), lambda i: (i, 0)), pl.BlockSpec((tm, D_KV), lambda i: (i, 0)),
                   pl.BlockSpec((tm, D_KV), lambda i: (i, 0))],
        out_shape=[jax.ShapeDtypeStruct((t, D_ATTN), BF16), jax.ShapeDtypeStruct((t, D_KV), BF16),
                   jax.ShapeDtypeStruct((t, D_KV), BF16)],
        compiler_params=_params(1),
    )(qkv, gq_w, gk_w, cos_w, sin_w, mean_q, mean_k)


def qk_bwd(dq_rot, dk_rot, dv, qkv, gq_w, gk_w, cos_w, sin_w, mean_q, mean_k, tm):
    t = qkv.shape[0]

    def branch(raw, d_rot, gain, mean_mat, cos, sin, scale):
        r = lax.rsqrt(_dot_f32(raw * raw, mean_mat) + EPS)
        n = raw * r
        dy = (d_rot * cos - _pair_swap(d_rot) * sin) * scale
        dn = dy * gain
        return r * (dn - n * _dot_f32(dn * n, mean_mat)), dy * n

    def body(dq_ref, dk_ref, dv_ref, p_ref, gq_ref, gk_ref, cos_ref, sin_ref, mq_ref, mk_ref,
             dp_ref, dgq_ref, dgk_ref):
        cos2, sin2 = cos_ref[...], sin_ref[...]
        cos8, sin8 = _tile_lanes(cos2, D_ATTN // LANES), _tile_lanes(sin2, D_ATTN // LANES)
        dq, dgq = branch(p_ref[:, :D_ATTN], dq_ref[...], gq_ref[...], mq_ref[...], cos8, sin8, HEAD_DIM ** -0.5)
        dk, dgk = branch(p_ref[:, D_ATTN:D_ATTN + D_KV], dk_ref[...], gk_ref[...], mk_ref[...], cos2, sin2, 1.0)
        dp_ref[...] = jnp.concatenate([dq, dk, dv_ref[...]], axis=-1).astype(BF16)

        @pl.when(pl.program_id(0) == 0)
        def _():
            dgq_ref[...] = jnp.zeros_like(dgq_ref)
            dgk_ref[...] = jnp.zeros_like(dgk_ref)

        dgq_ref[...] += _colsum(dgq)
        dgk_ref[...] += _colsum(dgk)

    return pl.pallas_call(
        functools.partial(body), name="qk_bwd", grid=(t // tm,),
        in_specs=[pl.BlockSpec((tm, D_ATTN), lambda i: (i, 0)), pl.BlockSpec((tm, D_KV), lambda i: (i, 0)),
                  pl.BlockSpec((tm, D_KV), lambda i: (i, 0)), pl.BlockSpec((tm, D_QKV), lambda i: (i, 0)),
                  _full((1, D_ATTN)), _full((1, D_KV)),
                  pl.BlockSpec((tm, LANES), lambda i: (i, 0)), pl.BlockSpec((tm, LANES), lambda i: (i, 0)),
                  _full((D_ATTN, D_ATTN)), _full((D_KV, D_KV))],
        out_specs=[pl.BlockSpec((tm, D_QKV), lambda i: (i, 0)), _full((1, D_ATTN)), _full((1, D_KV))],
        out_shape=[jax.ShapeDtypeStruct((t, D_QKV), BF16), jax.ShapeDtypeStruct((1, D_ATTN), F32),
                   jax.ShapeDtypeStruct((1, D_KV), F32)],
        compiler_params=_params(1),
    )(dq_rot, dk_rot, dv, qkv, gq_w, gk_w, cos_w, sin_w, mean_q, mean_k)


def attention_fwd(q_t, k, v_t):
    _, nq, _, tq = q_t.shape
    _, nk, tk, _ = k.shape

    def body(q_ref, k_ref, v_ref, o_ref, lse_ref, s_scr, p_scr):
        q = q_ref[...]
        s_scr[0] = _dot(k_ref[0], q)
        p_scr[1] = jnp.zeros((tk, tq), BF16)

        def step(j, slot, carry):
            m, l, acc = carry
            s = s_scr[slot]
            pv = _dot(v_ref[jnp.maximum(j - 1, 0)], p_scr[1 - slot])
            s_scr[1 - slot] = _dot(k_ref[jnp.minimum(j + 1, nk - 1)], q)
            m_new = jnp.maximum(m, jnp.max(s, axis=0, keepdims=True))
            p = jnp.exp2(s - m_new)
            alpha = jnp.exp2(m - m_new)
            p_scr[slot] = p.astype(BF16)
            return m_new, alpha * l + jnp.sum(p, axis=0, keepdims=True), alpha * (acc + pv)

        m, l, acc = _loop_pairs(
            nk, step,
            (jnp.full((1, tq), -1e30, F32), jnp.zeros((1, tq), F32), jnp.zeros((HEAD_DIM, tq), F32)))
        acc = acc + _dot(v_ref[nk - 1], p_scr[(nk - 1) % 2])
        o_ref[...] = acc / l
        lse_ref[...] = m + jnp.log2(l)

    return pl.pallas_call(
        functools.partial(body), name="attention_fwd", grid=(N_HEADS, nq),
        in_specs=[pl.BlockSpec((None, None, HEAD_DIM, tq), lambda h, i: (h, i, 0, 0)),
                  pl.BlockSpec((None, nk, tk, HEAD_DIM), lambda h, i: (h // KV_GROUP, 0, 0, 0)),
                  pl.BlockSpec((None, nk, HEAD_DIM, tk), lambda h, i: (h // KV_GROUP, 0, 0, 0))],
        out_specs=[pl.BlockSpec((None, None, HEAD_DIM, tq), lambda h, i: (h, i, 0, 0)),
                   pl.BlockSpec((None, None, 1, tq), lambda h, i: (h, i, 0, 0))],
        out_shape=[jax.ShapeDtypeStruct((N_HEADS, nq, HEAD_DIM, tq), F32),
                   jax.ShapeDtypeStruct((N_HEADS, nq, 1, tq), F32)],
        scratch_shapes=[pltpu.VMEM((2, tk, tq), F32), pltpu.VMEM((2, tk, tq), BF16)],
        compiler_params=_params(2),
    )(q_t, k, v_t)


def attention_delta(do_t, o_t):
    _, nq, _, tq = o_t.shape

    def body(do_ref, o_ref, d_ref):
        d_ref[...] = jnp.sum(do_ref[...] * o_ref[...], axis=0, keepdims=True)

    spec = pl.BlockSpec((None, None, HEAD_DIM, tq), lambda h, i: (h, i, 0, 0))
    return pl.pallas_call(
        functools.partial(body), name="attention_delta", grid=(N_HEADS, nq),
        in_specs=[spec, spec],
        out_specs=pl.BlockSpec((None, None, 1, tq), lambda h, i: (h, i, 0, 0)),
        out_shape=jax.ShapeDtypeStruct((N_HEADS, nq, 1, tq), F32),
        compiler_params=_params(2),
    )(do_t, o_t)


def attention_bwd(q_t, do_t, lse, delta, k, k_t, v):
    _, nq, _, tq = q_t.shape
    _, nk, tk, _ = k.shape

    def body(q_ref, do_ref, lse_ref, delta_ref, k_ref, kt_ref, v_ref, dq_ref, dk_ref, dv_ref,
             s_scr, dp_scr, p_scr, ds_scr):
        @pl.when(pl.program_id(1) == 0)
        def _():
            dq_ref[...] = jnp.zeros_like(dq_ref)

        kk, kt, vv = k_ref[...], kt_ref[...], v_ref[...]
        n = KV_GROUP * nq
        s_scr[0] = _dot(kk, q_ref[0, 0])
        dp_scr[0] = _dot(vv, do_ref[0, 0])
        p_scr[1] = jnp.zeros((tk, tq), BF16)
        ds_scr[1] = jnp.zeros((tk, tq), BF16)

        def products(t, slot, dk, dv):
            h, i = t // nq, t % nq
            ds = ds_scr[slot]
            dq_ref[h, i] += _dot(kt, ds)
            return dk + _dot_nt(q_ref[h, i], ds), dv + _dot_nt(do_ref[h, i], p_scr[slot])

        def step(t, slot, carry):
            s, dp = s_scr[slot], dp_scr[slot]
            dk, dv = products(jnp.maximum(t - 1, 0), 1 - slot, *carry)
            nxt = jnp.minimum(t + 1, n - 1)
            s_scr[1 - slot] = _dot(kk, q_ref[nxt // nq, nxt % nq])
            dp_scr[1 - slot] = _dot(vv, do_ref[nxt // nq, nxt % nq])
            h, i = t // nq, t % nq
            p = jnp.exp2(s - lse_ref[h, i])
            p_scr[slot] = p.astype(BF16)
            ds_scr[slot] = (p * (dp - delta_ref[h, i])).astype(BF16)
            return dk, dv

        zero = jnp.zeros((HEAD_DIM, tk), F32)
        dk, dv = products(n - 1, (n - 1) % 2, *_loop_pairs(n, step, (zero, zero)))
        dk_ref[...] = dk * (1.0 / LOG2_E)
        dv_ref[...] = dv

    group = lambda g, j: (g, 0, 0, 0)
    tile = lambda g, j: (g, j, 0, 0)
    return pl.pallas_call(
        functools.partial(body), name="attention_bwd", grid=(N_KV_HEADS, nk),
        in_specs=[pl.BlockSpec((KV_GROUP, nq, HEAD_DIM, tq), group),
                  pl.BlockSpec((KV_GROUP, nq, HEAD_DIM, tq), group),
                  pl.BlockSpec((KV_GROUP, nq, 1, tq), group),
                  pl.BlockSpec((KV_GROUP, nq, 1, tq), group),
                  pl.BlockSpec((None, None, tk, HEAD_DIM), tile),
                  pl.BlockSpec((None, None, HEAD_DIM, tk), tile),
                  pl.BlockSpec((None, None, tk, HEAD_DIM), tile)],
        out_specs=[pl.BlockSpec((KV_GROUP, nq, HEAD_DIM, tq), group),
                   pl.BlockSpec((None, None, HEAD_DIM, tk), tile),
                   pl.BlockSpec((None, None, HEAD_DIM, tk), tile)],
        out_shape=[jax.ShapeDtypeStruct((N_HEADS, nq, HEAD_DIM, tq), F32),
                   jax.ShapeDtypeStruct((N_KV_HEADS, nk, HEAD_DIM, tk), F32),
                   jax.ShapeDtypeStruct((N_KV_HEADS, nk, HEAD_DIM, tk), F32)],
        scratch_shapes=[pltpu.VMEM((2, tk, tq), F32), pltpu.VMEM((2, tk, tq), F32),
                        pltpu.VMEM((2, tk, tq), BF16), pltpu.VMEM((2, tk, tq), BF16)],
        compiler_params=_params(2),
    )(q_t, do_t, lse, delta, k, k_t, v)


def _group_select(parts):
    lane_group = lax.broadcasted_iota(jnp.int32, parts[0].shape, 1) // SGU_GROUP_DIM
    out = parts[0]
    for g in range(1, N_SGU_GROUPS):
        out = jnp.where(lane_group == g, parts[g], out)
    return out


def _gate_forward(z, g_sgu, ws_ref, bias):
    gz, th = _gelu(z)
    u, vv = gz[:, :D_SGU], gz[:, D_SGU:]
    rv = _rstd(vv)
    nv = vv * rv
    vn = (nv * g_sgu).astype(BF16)
    fs = []
    for c in range(z.shape[0] // CHUNK):
        vc = vn[c * CHUNK:(c + 1) * CHUNK]
        fs.append(_group_select([_dot(ws_ref[g], vc) for g in range(N_SGU_GROUPS)]) + bias)
    f = jnp.concatenate(fs, axis=0) if len(fs) > 1 else fs[0]
    return th, u, rv, nv, vn, f


def mix_out(z, o, x, g_sgu, g_ao, g_so, ws, bias, w_out, tm):
    t = x.shape[0]

    def body(z_ref, o_ref, x_ref, gs_ref, gao_ref, gso_ref, ws_ref, bias_ref, wout_ref, x2_ref, mixed_ref):
        _, u, _, _, _, f = _gate_forward(z_ref[...], gs_ref[...], ws_ref, bias_ref[...])
        sgu = u * f
        oo = o_ref[...]
        mixed = jnp.concatenate([oo * _rstd(oo) * gao_ref[...], sgu * _rstd(sgu) * gso_ref[...]], axis=-1).astype(BF16)
        mixed_ref[...] = mixed
        x2_ref[...] = x_ref[...] + _dot(mixed, wout_ref[...])

    row = lambda n: pl.BlockSpec((tm, n), lambda i: (i, 0))
    return pl.pallas_call(
        functools.partial(body), name="mix_out", grid=(t // tm,),
        in_specs=[row(2 * D_SGU), row(D_ATTN), row(D_MODEL), _full((1, D_SGU)), _full((1, D_ATTN)), _full((1, D_SGU)),
                  _full((N_SGU_GROUPS, CHUNK, CHUNK)), _full((CHUNK, D_SGU)), _full((D_MODEL, D_MODEL))],
        out_specs=[row(D_MODEL), row(D_MODEL)],
        out_shape=[jax.ShapeDtypeStruct((t, D_MODEL), F32), jax.ShapeDtypeStruct((t, D_MODEL), BF16)],
        compiler_params=_params(1),
    )(z, o, x, g_sgu, g_ao, g_so, ws, bias, w_out)


def mix_bwd(dx2, z, o, g_sgu, g_ao, g_so, ws, ws_t, bias, w_out, group_ind, tm):
    t = dx2.shape[0]
    n_tiles = t // tm

    def body(dx_ref, z_ref, o_ref, gs_ref, gao_ref, gso_ref, ws_ref, wst_ref, bias_ref, wout_ref, ind_ref,
             do_ref, dz_ref, dg_ref, dws_ref, dbs_ref, df_sum):
        step = pl.program_id(0)

        @pl.when(step == 0)
        def _():
            dg_ref[...] = jnp.zeros_like(dg_ref)
            dws_ref[...] = jnp.zeros_like(dws_ref)
            df_sum[...] = jnp.zeros_like(df_sum)

        z = z_ref[...]
        th, u, rv, nv, vn, f = _gate_forward(z, gs_ref[...], ws_ref, bias_ref[...])
        dmixed = _dot_nt(dx_ref[...].astype(BF16), wout_ref[...])
        oo = o_ref[...]
        ro = _rstd(oo)
        d_o, dgao = _rms_bwd(dmixed[:, :D_ATTN], oo * ro, ro, gao_ref[...])
        do_ref[...] = d_o
        sgu = u * f
        rs = _rstd(sgu)
        dsgu, dgso = _rms_bwd(dmixed[:, D_ATTN:], sgu * rs, rs, gso_ref[...])
        du = dsgu * f
        df = dsgu * u
        lane_group = lax.broadcasted_iota(jnp.int32, (CHUNK, D_SGU), 1) // SGU_GROUP_DIM
        dvns = []
        df_acc = jnp.zeros((CHUNK, D_SGU), F32)
        for c in range(tm // CHUNK):
            dfc32 = df[c * CHUNK:(c + 1) * CHUNK]
            dfc = dfc32.astype(BF16)
            vc = vn[c * CHUNK:(c + 1) * CHUNK]
            dvns.append(_group_select([_dot(wst_ref[g], dfc) for g in range(N_SGU_GROUPS)]))
            for g in range(N_SGU_GROUPS):
                dws_ref[g] += _dot_nt(jnp.where(lane_group == g, dfc, jnp.zeros_like(dfc)), vc)
            df_acc = df_acc + dfc32
        df_sum[...] += df_acc
        dvn = jnp.concatenate(dvns, axis=0) if len(dvns) > 1 else dvns[0]
        dvv, dgs = _rms_bwd(dvn, nv, rv, gs_ref[...])
        dz_ref[...] = (jnp.concatenate([du, dvv], axis=-1) * _gelu_grad(z, th)).astype(BF16)
        dg_ref[0:1, :] += _colsum(dgao)
        dg_ref[1:2, :] += _colsum(dgso)
        dg_ref[2:3, :] += _colsum(dgs)

        @pl.when(step == n_tiles - 1)
        def _():
            dbs_ref[...] = _dot_f32(df_sum[...], ind_ref[...])

    row = lambda n: pl.BlockSpec((tm, n), lambda i: (i, 0))
    return pl.pallas_call(
        functools.partial(body), name="mix_bwd", grid=(n_tiles,),
        in_specs=[row(D_MODEL), row(2 * D_SGU), row(D_ATTN), _full((1, D_SGU)), _full((1, D_ATTN)), _full((1, D_SGU)),
                  _full((N_SGU_GROUPS, CHUNK, CHUNK)), _full((N_SGU_GROUPS, CHUNK, CHUNK)), _full((CHUNK, D_SGU)),
                  _full((D_MODEL, D_MODEL)), _full((D_SGU, LANES))],
        out_specs=[row(D_ATTN), row(2 * D_SGU), _full((8, D_SGU)), _full((N_SGU_GROUPS, CHUNK, CHUNK)),
                   _full((CHUNK, LANES))],
        out_shape=[jax.ShapeDtypeStruct((t, D_ATTN), F32), jax.ShapeDtypeStruct((t, 2 * D_SGU), BF16),
                   jax.ShapeDtypeStruct((8, D_SGU), F32),
                   jax.ShapeDtypeStruct((N_SGU_GROUPS, CHUNK, CHUNK), F32),
                   jax.ShapeDtypeStruct((CHUNK, LANES), F32)],
        scratch_shapes=[pltpu.VMEM((CHUNK, D_SGU), F32)],
        compiler_params=_params(1),
    )(dx2, z, o, g_sgu, g_ao, g_so, ws, ws_t, bias, w_out, group_ind)


def loss_bwd(x, g, target, tm):
    t = x.shape[0]

    def body(x_ref, g_ref, t_ref, loss_ref, dx_ref, dg_ref):
        @pl.when(pl.program_id(0) == 0)
        def _():
            loss_ref[...] = jnp.zeros_like(loss_ref)
            dg_ref[...] = jnp.zeros_like(dg_ref)

        xx = x_ref[...]
        r = _rstd(xx)
        n = xx * r
        err = n * g_ref[...] - t_ref[...]
        per_token = jnp.mean(err * err, axis=-1, keepdims=True)
        loss_ref[...] += 0.5 * jnp.sum(per_token, axis=0, keepdims=True)
        dx, dg_rows = _rms_bwd(err * (1.0 / D_MODEL), n, r, g_ref[...])
        dx_ref[...] = dx
        dg_ref[...] += _colsum(dg_rows)

    row = pl.BlockSpec((tm, D_MODEL), lambda i: (i, 0))
    return pl.pallas_call(
        functools.partial(body), name="loss_bwd", grid=(t // tm,),
        in_specs=[row, _full((1, D_MODEL)), row],
        out_specs=[_full((1, LANES)), row, _full((1, D_MODEL))],
        out_shape=[jax.ShapeDtypeStruct((1, LANES), F32), jax.ShapeDtypeStruct((t, D_MODEL), F32),
                   jax.ShapeDtypeStruct((1, D_MODEL), F32)],
        compiler_params=_params(1),
    )(x, g, target)


def _rope_tables(t):
    rows = t // GRID_W
    row_idx = jnp.repeat(jnp.arange(rows, dtype=F32), GRID_W)
    col_idx = jnp.tile(jnp.arange(GRID_W, dtype=F32), rows)
    axis_dim = HEAD_DIM // 2
    inv = 1.0 / (ROPE_THETA ** (jnp.arange(0, axis_dim, 2, dtype=F32) / axis_dim))
    ang = jnp.concatenate([row_idx[:, None] * inv, col_idx[:, None] * inv], axis=-1)
    cos = jnp.repeat(jnp.cos(ang), 2, axis=-1)
    sin = jnp.repeat(jnp.sin(ang), 2, axis=-1) * jnp.tile(jnp.array([-1.0, 1.0], F32), HEAD_DIM // 2)
    return jnp.tile(cos, (1, LANES // HEAD_DIM)), jnp.tile(sin, (1, LANES // HEAD_DIM))


def _heads_to_tiles_t(a, n_heads, tile):
    t = a.shape[0]
    return a.reshape(t // tile, tile, n_heads, HEAD_DIM).transpose(2, 0, 3, 1)


def _heads_to_tiles(a, n_heads, tile):
    t = a.shape[0]
    return a.reshape(t // tile, tile, n_heads, HEAD_DIM).transpose(2, 0, 1, 3)


def _tiles_t_to_heads(a):
    h, n, _, tile = a.shape
    return a.transpose(1, 3, 0, 2).reshape(n * tile, h * HEAD_DIM)


def kernel(x, g_ffn1, w1_gate, w1_up, w1_down, g_mix, w_in, g_q, g_k, g_sgu, w_s, b_s, g_attn_out, g_sgu_out, w_out, g_ffn2, w2_gate, w2_up, w2_down, g_final, loss_target, m_g_ffn1, m_w1_gate, m_w1_up, m_w1_down, m_g_mix, m_w_in, m_g_q, m_g_k, m_g_sgu, m_w_s, m_b_s, m_g_attn_out, m_g_sgu_out, m_w_out, m_g_ffn2, m_w2_gate, m_w2_up, m_w2_down, m_g_final, v_g_ffn1, v_w1_gate, v_w1_up, v_w1_down, v_g_mix, v_w_in, v_g_q, v_g_k, v_g_sgu, v_w_s, v_b_s, v_g_attn_out, v_g_sgu_out, v_w_out, v_g_ffn2, v_w2_gate, v_w2_up, v_w2_down, v_g_final):
    weights = dict(g_ffn1=g_ffn1, w1_gate=w1_gate, w1_up=w1_up, w1_down=w1_down, g_mix=g_mix, w_in=w_in, g_q=g_q,
                   g_k=g_k, g_sgu=g_sgu, w_s=w_s, b_s=b_s, g_attn_out=g_attn_out, g_sgu_out=g_sgu_out, w_out=w_out,
                   g_ffn2=g_ffn2, w2_gate=w2_gate, w2_up=w2_up, w2_down=w2_down, g_final=g_final)
    m_in = dict(g_ffn1=m_g_ffn1, w1_gate=m_w1_gate, w1_up=m_w1_up, w1_down=m_w1_down, g_mix=m_g_mix, w_in=m_w_in,
                g_q=m_g_q, g_k=m_g_k, g_sgu=m_g_sgu, w_s=m_w_s, b_s=m_b_s, g_attn_out=m_g_attn_out,
                g_sgu_out=m_g_sgu_out, w_out=m_w_out, g_ffn2=m_g_ffn2, w2_gate=m_w2_gate, w2_up=m_w2_up,
                w2_down=m_w2_down, g_final=m_g_final)
    v_in = dict(g_ffn1=v_g_ffn1, w1_gate=v_w1_gate, w1_up=v_w1_up, w1_down=v_w1_down, g_mix=v_g_mix, w_in=v_w_in,
                g_q=v_g_q, g_k=v_g_k, g_sgu=v_g_sgu, w_s=v_w_s, b_s=v_b_s, g_attn_out=v_g_attn_out,
                g_sgu_out=v_g_sgu_out, w_out=v_w_out, g_ffn2=v_g_ffn2, w2_gate=v_w2_gate, w2_up=v_w2_up,
                w2_down=v_w2_down, g_final=v_g_final)
    names = list(weights)

    t = x.shape[1]
    x0 = x[0]
    target = loss_target[0]
    tm = min(256, t)
    tm_ff = min(512, t)
    tn_ff = 256
    tq = min(512, t)
    tk = min(256, t)
    tk_w = min(512, t)

    def shard_rows(name):
        w = weights[name][0]
        return (w.T if name in TRANSPOSED else w).astype(BF16)

    gathered = all_gather_weights(jnp.concatenate([shard_rows(n) for n, _ in SHARD_ROWS], axis=0))
    full = {n: g.reshape(N_DEV * r, D_MODEL) for (n, r), g in zip(SHARD_ROWS, gathered)}
    w_in_t = full["w_in"]
    w_qkv_t, w_z_t = w_in_t[:D_QKV], w_in_t[D_QKV:]

    h1, a1, b1, act1 = ffn_up(x0, g_ffn1, full["w1_gate"], full["w1_up"], tm_ff, tn_ff)
    x1 = ffn_down(act1, full["w1_down"], x0, tm)

    qkv, h2 = norm_matmul(x1, g_mix, w_qkv_t, tm_ff, 256, True)
    z = norm_matmul(x1, g_mix, w_z_t, tm_ff, 256, False)
    cos_w, sin_w = _rope_tables(t)
    gq_w = jnp.tile(g_q, (1, N_HEADS))
    gk_w = jnp.tile(g_k, (1, N_KV_HEADS))
    mean_q, mean_k = _head_mean_matrix(D_ATTN), _head_mean_matrix(D_KV)
    q_rot, k_rot, v_b = qk_prep(qkv, gq_w, gk_w, cos_w, sin_w, mean_q, mean_k, tm)
    q_t = _heads_to_tiles_t(q_rot, N_HEADS, tq)
    k_tiles = _heads_to_tiles(k_rot, N_KV_HEADS, tk)
    kt_tiles = _heads_to_tiles_t(k_rot, N_KV_HEADS, tk)
    v_tiles = _heads_to_tiles(v_b, N_KV_HEADS, tk)
    vt_tiles = _heads_to_tiles_t(v_b, N_KV_HEADS, tk)
    o_t, lse = attention_fwd(q_t, k_tiles, vt_tiles)
    o = _tiles_t_to_heads(o_t)

    ws_b = w_s[0].astype(BF16)
    ws_tb = jnp.swapaxes(w_s[0], 1, 2).astype(BF16)
    bias = jnp.repeat(b_s[0].T, SGU_GROUP_DIM, axis=1)
    x2, mixed = mix_out(z, o, x1, g_sgu, g_attn_out, g_sgu_out, ws_b, bias, full["w_out"], tm)

    h3, a2, b2, act2 = ffn_up(x2, g_ffn2, full["w2_gate"], full["w2_up"], tm_ff, tn_ff)
    x3 = ffn_down(act2, full["w2_down"], x2, tm)

    loss_part, dx3, dg_final = loss_bwd(x3, g_final, target, tm)
    loss = lax.psum(loss_part[0, 0], MESH_AXES)

    def ffn_backward(dx_out, h, a, b, act, x_in, g, wg_t, wu_t, wd):
        da, db = ffn_bwd_act(dx_out, wd, a, b, tm_ff, tn_ff)
        dx_in, dg = norm_bwd_matmul(da, wg_t, db, wu_t, x_in, g, dx_out, tm)
        tmm = D_FF // 2
        return dx_in, dg, matmul_tn(da, h, 1.0, tmm, tk_w), matmul_tn(db, h, 1.0, tmm, tk_w), \
            matmul_tn(act, dx_out, 0.5, tmm, tk_w)

    dx2, dg_ffn2, dwg2, dwu2, dwd2 = ffn_backward(dx3, h3, a2, b2, act2, x2, g_ffn2, full["w2_gate"], full["w2_up"],
                                                  full["w2_down"])

    group_ind = (jnp.arange(D_SGU)[:, None] // SGU_GROUP_DIM == jnp.arange(LANES)[None, :]).astype(F32)
    d_o, dz, dg_mixrow, dws, dbs = mix_bwd(dx2, z, o, g_sgu, g_attn_out, g_sgu_out, ws_b, ws_tb, bias, full["w_out"],
                                           group_ind, tm)
    dw_out = matmul_tn(mixed, dx2, 1.0, D_MODEL // 2, tk_w)

    do_t = _heads_to_tiles_t(d_o, N_HEADS, tq)
    delta = attention_delta(do_t, o_t)
    dq_t, dk_t, dv_t = attention_bwd(q_t, do_t.astype(BF16), lse, delta, k_tiles, kt_tiles, v_tiles)
    dqkv, dgq_w, dgk_w = qk_bwd(_tiles_t_to_heads(dq_t), _tiles_t_to_heads(dk_t), _tiles_t_to_heads(dv_t), qkv,
                                gq_w, gk_w, cos_w, sin_w, mean_q, mean_k, tm)
    dx1, dg_mix = norm_bwd_matmul(dqkv, w_qkv_t, dz, w_z_t, x1, g_mix, dx2, tm)
    dw_in = jnp.concatenate([matmul_tn(dqkv, h2, 1.0, D_QKV // 2, tk_w), matmul_tn(dz, h2, 1.0, D_SGU, tk_w)], axis=0)

    dx0, dg_ffn1, dwg1, dwu1, dwd1 = ffn_backward(dx1, h1, a1, b1, act1, x0, g_ffn1, full["w1_gate"], full["w1_up"],
                                                  full["w1_down"])

    small_grads = dict(
        g_ffn1=dg_ffn1, g_mix=dg_mix, g_ffn2=dg_ffn2, g_final=dg_final,
        g_q=dgq_w.reshape(N_HEADS, HEAD_DIM).sum(0)[None], g_k=dgk_w.reshape(N_KV_HEADS, HEAD_DIM).sum(0)[None],
        g_attn_out=dg_mixrow[0:1], g_sgu_out=dg_mixrow[1:2], g_sgu=dg_mixrow[2:3],
        w_s=dws[None], b_s=dbs[:, :N_SGU_GROUPS].T[None])
    small_names = [n for n in names if n not in dict(SHARD_ROWS)]
    pieces = []
    for n in small_names:
        flat = small_grads[n].reshape(-1)
        flat = jnp.pad(flat, (0, (-flat.shape[0]) % LANES))
        pieces.append(flat.reshape(-1, LANES))
    small_rows = [p.shape[0] for p in pieces]
    pad_rows = (-sum(small_rows)) % 8
    small_pack = jnp.concatenate(pieces + [jnp.zeros((pad_rows, LANES), F32)], axis=0)

    big = dict(w1_gate=dwg1, w1_up=dwu1, w1_down=dwd1, w_in=dw_in, w_out=dw_out, w2_gate=dwg2, w2_up=dwu2, w2_down=dwd2)
    parts, small_parts = exchange_grads([big[n] for n, _ in SHARD_ROWS], small_pack)
    summed = sum_parts(parts, 32)
    small_sum = sum_parts(small_parts, small_pack.shape[0])

    grads = {}
    off = 0
    for n, r in SHARD_ROWS:
        gsh = summed[off:off + r]
        grads[n] = (gsh.T if n in TRANSPOSED else gsh)[None]
        off += r
    off = 0
    for n, r in zip(small_names, small_rows):
        grads[n] = small_sum[off:off + r].reshape(-1)[:weights[n].size].reshape(weights[n].shape)
        off += r

    delta_w, new_m, new_v = {}, {}, {}
    for n in names:
        shape = weights[n].shape
        as2d = (lambda a: a.reshape(-1, shape[-1]))
        d, m2, v2 = adamw(as2d(weights[n]), as2d(grads[n]), as2d(m_in[n]), as2d(v_in[n]))
        delta_w[n], new_m[n], new_v[n] = d.reshape(shape), m2.reshape(shape), v2.reshape(shape)

    return (loss, dx0[None], *[grads[n] for n in names], *[delta_w[n] for n in names],
            *[new_m[n] for n in names], *[new_v[n] for n in names])
```

```python
import functools
import math

import jax
import jax.numpy as jnp
from jax import lax
from jax.experimental import pallas as pl
from jax.experimental.pallas import tpu as pltpu

F32 = jnp.float32
BF16 = jnp.bfloat16

D_MODEL = 1024
D_FF = 2816
N_HEADS = 8
HEAD_DIM = 64
N_KV_HEADS = 2
KV_GROUP = N_HEADS // N_KV_HEADS
D_ATTN = N_HEADS * HEAD_DIM
D_KV = N_KV_HEADS * HEAD_DIM
D_QKV = D_ATTN + 2 * D_KV
N_SGU_GROUPS = 8
SGU_GROUP_DIM = 64
D_SGU = N_SGU_GROUPS * SGU_GROUP_DIM
CHUNK = 128
GRID_W = 64
ROPE_THETA = 10000.0
EPS = 1e-6
N_DEV = 8
LANES = 128

LOG2_E = math.log2(math.e)
Q_SCALE = HEAD_DIM ** -0.5 * LOG2_E

ADAM_LR = 0.001
ADAM_B1 = 0.9
ADAM_B2 = 0.999
ADAM_EPS = 1e-08
ADAM_WD = 0.01
ADAM_STEP = 10

MESH_AXES = ("x", "y", "c")
MESH_IDS = pl.DeviceIdType.MESH

VMEM_LIMIT = 56 * 1024 * 1024

SHARD_ROWS = (("w1_gate", D_FF // N_DEV), ("w1_up", D_FF // N_DEV), ("w1_down", D_FF // N_DEV),
              ("w_in", (D_QKV + 2 * D_SGU) // N_DEV), ("w_out", D_MODEL // N_DEV),
              ("w2_gate", D_FF // N_DEV), ("w2_up", D_FF // N_DEV), ("w2_down", D_FF // N_DEV))
PACK_ROWS = sum(r for _, r in SHARD_ROWS)
TRANSPOSED = ("w1_gate", "w1_up", "w_in", "w2_gate", "w2_up")


def _params(n_grid):
    return pltpu.CompilerParams(dimension_semantics=("arbitrary",) * n_grid, vmem_limit_bytes=VMEM_LIMIT)


def _dot(a, b):
    return jnp.dot(a, b, preferred_element_type=F32)


def _dot_nt(a, b):
    return lax.dot_general(a, b, (((1,), (1,)), ((), ())), preferred_element_type=F32)


def _dot_tn(a, b):
    return lax.dot_general(a, b, (((0,), (0,)), ((), ())), preferred_element_type=F32)


def _dot_f32(a, b):
    return jnp.dot(a, b, preferred_element_type=F32, precision=lax.Precision.HIGHEST)


def _rstd(x):
    return lax.rsqrt(jnp.mean(x * x, axis=-1, keepdims=True) + EPS)


def _rms_bwd(dy, n, r, g):
    dn = dy * g
    return r * (dn - n * jnp.mean(dn * n, axis=-1, keepdims=True)), dy * n


def _colsum(a):
    return jnp.sum(a, axis=0, keepdims=True)


_GELU_C = math.sqrt(2.0 / math.pi)


def _gelu(x):
    t = jnp.tanh(_GELU_C * (x + 0.044715 * (x * x * x)))
    return x * (0.5 * (1.0 + t)), t


def _gelu_grad(x, t):
    return 0.5 * (1.0 + t) + 0.5 * x * (1.0 - t * t) * (_GELU_C * (1.0 + 3 * 0.044715 * x * x))


def _pair_swap(a):
    w = a.shape[-1]
    lane = lax.broadcasted_iota(jnp.int32, a.shape, a.ndim - 1)
    return jnp.where(lane % 2 == 0, pltpu.roll(a, w - 1, a.ndim - 1), pltpu.roll(a, 1, a.ndim - 1))


def _tile_lanes(a, reps):
    return jnp.concatenate([a] * reps, axis=-1) if reps > 1 else a


def _loop_pairs(n, step, carry):
    assert n % 2 == 0, n

    def pair(jj, c):
        return step(2 * jj + 1, 1, step(2 * jj, 0, c))

    return lax.fori_loop(0, n // 2, pair, carry)


def _full(shape):
    nd = len(shape)
    return pl.BlockSpec(shape, lambda *_: (0,) * nd)


def _mesh_pos():
    return lax.axis_index("x"), lax.axis_index("y"), lax.axis_index("c")


def _peer(pos, d):
    x, y, c = pos
    px = 1 - x if d & 4 else x
    py = 1 - y if d & 2 else y
    pc = 1 - c if d & 1 else c
    return (px, py, pc), 4 * px + 2 * py + pc


class _Exchange:
    def __init__(self, operands, out_shape, n_local, plan):
        self.operands = list(operands)
        self.out_shape = list(out_shape)
        self.sem_shapes = [pltpu.SemaphoreType.DMA((N_DEV - 1,)), pltpu.SemaphoreType.DMA((N_DEV - 1,)),
                           pltpu.SemaphoreType.DMA((n_local,))]
        self._plan = plan

    def _copies(self, in_refs, out_refs):
        pos = _mesh_pos()
        return pos, self._plan(4 * pos[0] + 2 * pos[1] + pos[2], in_refs, out_refs)

    def start(self, in_refs, out_refs, sems):
        send_sems, recv_sems, local_sems = sems
        pos, (local, remote, _) = self._copies(in_refs, out_refs)
        for k, (src, dst) in enumerate(local):
            pltpu.make_async_copy(src, dst, local_sems.at[k]).start()
        for d in range(1, N_DEV):
            peer, peer_lin = _peer(pos, d)
            for src, dst in remote(peer_lin):
                pltpu.make_async_remote_copy(src_ref=src, dst_ref=dst, send_sem=send_sems.at[d - 1],
                                             recv_sem=recv_sems.at[d - 1], device_id=peer,
                                             device_id_type=MESH_IDS).start()

    def wait(self, in_refs, out_refs, sems):
        send_sems, recv_sems, local_sems = sems
        pos, (local, _, whole) = self._copies(in_refs, out_refs)
        for d in range(1, N_DEV):
            peer, peer_lin = _peer(pos, d)
            ref = whole(peer_lin)
            everything = pltpu.make_async_remote_copy(src_ref=ref, dst_ref=ref, send_sem=send_sems.at[d - 1],
                                                      recv_sem=recv_sems.at[d - 1], device_id=peer,
                                                      device_id_type=MESH_IDS)
            everything.wait_send()
            everything.wait_recv()
        for k, (src, dst) in enumerate(local):
            pltpu.make_async_copy(src, dst, local_sems.at[k]).wait()


def _offsets(rows):
    offs, o = [], 0
    for r in rows:
        offs.append(o)
        o += r
    return offs


def gather_exchange(src, rows):
    offs = _offsets(rows)

    def plan(me, in_refs, out_refs):
        pieces = [(in_refs[0].at[pl.ds(o, r)], out.at[me]) for o, r, out in zip(offs, rows, out_refs)]
        return pieces, (lambda peer_lin: pieces), (lambda peer_lin: in_refs[0])

    return _Exchange([src], [jax.ShapeDtypeStruct((N_DEV, r) + src.shape[1:], src.dtype) for r in rows],
                     len(rows), plan)


def scatter_exchange(grads):
    rows = [g.shape[0] // N_DEV for g in grads]
    offs = _offsets(rows)

    def plan(me, in_refs, out_refs):
        parts = out_refs[0]

        def slabs(owner):
            return [(g.at[pl.ds(pl.multiple_of(owner * r, 16), r)], parts.at[me, pl.ds(o, r)])
                    for g, o, r in zip(in_refs, offs, rows)]

        return slabs(me), slabs, (lambda peer_lin: parts.at[peer_lin])

    shape = jax.ShapeDtypeStruct((N_DEV, sum(rows)) + grads[0].shape[1:], grads[0].dtype)
    return _Exchange(grads, [shape], len(rows), plan)


def run_exchange(ex, name):
    n_in, n_out = len(ex.operands), len(ex.out_shape)

    def body(*refs):
        parts = refs[:n_in], refs[n_in:n_in + n_out], refs[n_in + n_out:]
        ex.start(*parts)
        ex.wait(*parts)

    any_spec = pl.BlockSpec(memory_space=pl.ANY)
    return pl.pallas_call(
        functools.partial(body), name=name, out_shape=ex.out_shape,
        in_specs=[any_spec] * n_in, out_specs=[any_spec] * n_out, scratch_shapes=ex.sem_shapes,
        compiler_params=pltpu.CompilerParams(has_side_effects=True),
    )(*ex.operands)


def _pallas(comm, body, *, name, grid, in_specs, out_specs, out_shape, args, scratch_shapes=()):
    params = _params(len(grid))
    if comm is None:
        res = pl.pallas_call(functools.partial(body), name=name, grid=grid, in_specs=list(in_specs),
                             out_specs=list(out_specs), out_shape=list(out_shape),
                             scratch_shapes=list(scratch_shapes), compiler_params=params)(*args)
        return list(res), []
    n_in, n_out, n_scr = len(in_specs), len(out_specs), len(scratch_shapes)
    c_in, c_out = len(comm.operands), len(comm.out_shape)

    def edge(last):
        conds = [pl.program_id(a) == (g - 1 if last else 0) for a, g in enumerate(grid)]
        return functools.reduce(jnp.logical_and, conds)

    def wrapped(*refs):
        refs = list(refs)
        ins, refs = refs[:n_in], refs[n_in:]
        cins, refs = refs[:c_in], refs[c_in:]
        outs, refs = refs[:n_out], refs[n_out:]
        couts, refs = refs[:c_out], refs[c_out:]
        scr, sems = refs[:n_scr], refs[n_scr:]

        @pl.when(edge(False))
        def _():
            comm.start(cins, couts, sems)

        body(*ins, *outs, *scr)

        @pl.when(edge(True))
        def _():
            comm.wait(cins, couts, sems)

    any_spec = pl.BlockSpec(memory_space=pl.ANY)
    res = pl.pallas_call(
        wrapped, name=name, grid=grid,
        in_specs=list(in_specs) + [any_spec] * c_in, out_specs=list(out_specs) + [any_spec] * c_out,
        out_shape=list(out_shape) + comm.out_shape, scratch_shapes=list(scratch_shapes) + comm.sem_shapes,
        compiler_params=pltpu.CompilerParams(dimension_semantics=("arbitrary",) * len(grid),
                                             vmem_limit_bytes=VMEM_LIMIT, has_side_effects=True),
    )(*args, *comm.operands)
    return res[:n_out], res[n_out:]


def sum_parts(parts, block_rows):
    n, rows, cols = parts.shape

    def body(p_ref, o_ref):
        acc = p_ref[0].astype(F32)
        for s in range(1, n):
            acc = acc + p_ref[s].astype(F32)
        o_ref[...] = acc

    return pl.pallas_call(
        functools.partial(body), name="sum_parts",
        grid=(rows // block_rows,),
        in_specs=[pl.BlockSpec((n, block_rows, cols), lambda i: (0, i, 0))],
        out_specs=pl.BlockSpec((block_rows, cols), lambda i: (i, 0)),
        out_shape=jax.ShapeDtypeStruct((rows, cols), F32),
        compiler_params=_params(1),
    )(parts)


def adamw(w, g, m, v):
    def body(w_ref, g_ref, m_ref, v_ref, d_ref, m_out, v_out):
        gg = g_ref[...]
        m2 = ADAM_B1 * m_ref[...] + (1.0 - ADAM_B1) * gg
        v2 = ADAM_B2 * v_ref[...] + (1.0 - ADAM_B2) * (gg * gg)
        m_hat = m2 / (1.0 - ADAM_B1 ** ADAM_STEP)
        v_hat = v2 / (1.0 - ADAM_B2 ** ADAM_STEP)
        d_ref[...] = -ADAM_LR * (m_hat / (jnp.sqrt(v_hat) + ADAM_EPS) + ADAM_WD * w_ref[...])
        m_out[...] = m2
        v_out[...] = v2

    spec = _full(w.shape)
    shape = jax.ShapeDtypeStruct(w.shape, F32)
    return pl.pallas_call(
        functools.partial(body), name="adamw",
        in_specs=[spec] * 4, out_specs=[spec] * 3, out_shape=[shape] * 3,
        compiler_params=pltpu.CompilerParams(vmem_limit_bytes=VMEM_LIMIT),
    )(w, g, m, v)


def ffn_up(x, g, wg_t, wu_t, tm, tn, comm=None):
    t = x.shape[0]

    def body(x_ref, g_ref, wg_ref, wu_ref, h_ref, a_ref, b_ref, act_ref):
        @pl.when(pl.program_id(1) == 0)
        def _():
            xx = x_ref[...]
            h_ref[...] = ((xx * _rstd(xx)) * g_ref[...]).astype(BF16)

        h = h_ref[...]
        a = _dot_nt(h, wg_ref[...])
        b = _dot_nt(h, wu_ref[...])
        a_ref[...] = a.astype(BF16)
        b_ref[...] = b.astype(BF16)
        act_ref[...] = (a * jax.nn.sigmoid(a) * b).astype(BF16)

    wide = jax.ShapeDtypeStruct((t, D_FF), BF16)
    tile = pl.BlockSpec((tm, tn), lambda i, j: (i, j))
    return _pallas(
        comm, body, name="ffn_up",
        grid=(t // tm, D_FF // tn),
        in_specs=[pl.BlockSpec((tm, D_MODEL), lambda i, j: (i, 0)), _full((1, D_MODEL)),
                  pl.BlockSpec((tn, D_MODEL), lambda i, j: (j, 0)), pl.BlockSpec((tn, D_MODEL), lambda i, j: (j, 0))],
        out_specs=[pl.BlockSpec((tm, D_MODEL), lambda i, j: (i, 0)), tile, tile, tile],
        out_shape=[jax.ShapeDtypeStruct((t, D_MODEL), BF16), wide, wide, wide],
        args=(x, g, wg_t, wu_t))


def ffn_down(act, wd, x, tm):
    t = x.shape[0]

    def body(act_ref, wd_ref, x_ref, o_ref):
        o_ref[...] = x_ref[...] + 0.5 * _dot(act_ref[...], wd_ref[...])

    return pl.pallas_call(
        functools.partial(body), name="ffn_down",
        grid=(t // tm,),
        in_specs=[pl.BlockSpec((tm, D_FF), lambda i: (i, 0)), _full((D_FF, D_MODEL)),
                  pl.BlockSpec((tm, D_MODEL), lambda i: (i, 0))],
        out_specs=pl.BlockSpec((tm, D_MODEL), lambda i: (i, 0)),
        out_shape=jax.ShapeDtypeStruct((t, D_MODEL), F32),
        compiler_params=_params(1),
    )(act, wd, x)


def ffn_bwd_act(dx, wd, a, b, tm, tn, comm=None):
    t = dx.shape[0]

    def body(dx_ref, wd_ref, a_ref, b_ref, da_ref, db_ref, dxb_ref):
        @pl.when(pl.program_id(1) == 0)
        def _():
            dxb_ref[...] = dx_ref[...].astype(BF16)

        dact = 0.5 * _dot_nt(dxb_ref[...], wd_ref[...])
        aa = a_ref[...].astype(F32)
        sig = jax.nn.sigmoid(aa)
        da_ref[...] = (dact * b_ref[...].astype(F32) * (sig * (1.0 + aa * (1.0 - sig)))).astype(BF16)
        db_ref[...] = (dact * (aa * sig)).astype(BF16)

    wide = jax.ShapeDtypeStruct((t, D_FF), BF16)
    tile = pl.BlockSpec((tm, tn), lambda i, j: (i, j))
    return _pallas(
        comm, body, name="ffn_bwd_act",
        grid=(t // tm, D_FF // tn),
        in_specs=[pl.BlockSpec((tm, D_MODEL), lambda i, j: (i, 0)),
                  pl.BlockSpec((tn, D_MODEL), lambda i, j: (j, 0)), tile, tile],
        out_specs=[tile, tile],
        out_shape=[wide, wide],
        scratch_shapes=[pltpu.VMEM((tm, D_MODEL), BF16)],
        args=(dx, wd, a, b))


def norm_bwd_matmul(a1, w1, a2, w2, x, g, dx_in, tm, comm=None):
    t = x.shape[0]
    k1, k2 = a1.shape[1], a2.shape[1]

    def body(a1_ref, w1_ref, a2_ref, w2_ref, x_ref, g_ref, dxin_ref, dx_ref, dg_ref):
        dh = _dot(a1_ref[...], w1_ref[...]) + _dot(a2_ref[...], w2_ref[...])
        xx = x_ref[...]
        r = _rstd(xx)
        dx, dg_rows = _rms_bwd(dh, xx * r, r, g_ref[...])
        dx_ref[...] = dxin_ref[...] + dx

        @pl.when(pl.program_id(0) == 0)
        def _():
            dg_ref[...] = jnp.zeros_like(dg_ref)

        dg_ref[...] += _colsum(dg_rows)

    row = pl.BlockSpec((tm, D_MODEL), lambda i: (i, 0))
    return _pallas(
        comm, body, name="norm_bwd_matmul",
        grid=(t // tm,),
        in_specs=[pl.BlockSpec((tm, k1), lambda i: (i, 0)), _full((k1, D_MODEL)),
                  pl.BlockSpec((tm, k2), lambda i: (i, 0)), _full((k2, D_MODEL)),
                  row, _full((1, D_MODEL)), row],
        out_specs=[row, _full((1, D_MODEL))],
        out_shape=[jax.ShapeDtypeStruct((t, D_MODEL), F32), jax.ShapeDtypeStruct((1, D_MODEL), F32)],
        args=(a1, w1, a2, w2, x, g, dx_in))


def matmul_tn(a, b, scale, tmm, tk):
    t, m = a.shape
    n = b.shape[1]
    nk = t // tk

    def body(a_ref, b_ref, o_ref, acc_ref):
        k = pl.program_id(1)

        @pl.when(k == 0)
        def _():
            acc_ref[...] = jnp.zeros_like(acc_ref)

        acc_ref[...] += _dot_tn(a_ref[...].astype(BF16), b_ref[...].astype(BF16))

        @pl.when(k == nk - 1)
        def _():
            o_ref[...] = (scale * acc_ref[...]).astype(BF16)

    return pl.pallas_call(
        functools.partial(body), name="matmul_tn",
        grid=(m // tmm, nk),
        in_specs=[pl.BlockSpec((tk, tmm), lambda i, k: (k, i)), pl.BlockSpec((tk, n), lambda i, k: (k, 0))],
        out_specs=pl.BlockSpec((tmm, n), lambda i, k: (i, 0)),
        out_shape=jax.ShapeDtypeStruct((m, n), BF16),
        scratch_shapes=[pltpu.VMEM((tmm, n), F32)],
        compiler_params=_params(2),
    )(a, b)


def norm_matmul(x, g, w_t, tm, tn, with_h):
    t = x.shape[0]
    n = w_t.shape[0]

    def body(x_ref, g_ref, w_ref, o_ref, h_ref):
        @pl.when(pl.program_id(1) == 0)
        def _():
            xx = x_ref[...]
            h_ref[...] = ((xx * _rstd(xx)) * g_ref[...]).astype(BF16)

        o_ref[...] = _dot_nt(h_ref[...], w_ref[...])

    in_specs = [pl.BlockSpec((tm, D_MODEL), lambda i, j: (i, 0)), _full((1, D_MODEL)),
                pl.BlockSpec((tn, D_MODEL), lambda i, j: (j, 0))]
    o_spec = pl.BlockSpec((tm, tn), lambda i, j: (i, j))
    o_shape = jax.ShapeDtypeStruct((t, n), F32)
    h_spec = pl.BlockSpec((tm, D_MODEL), lambda i, j: (i, 0))
    if with_h:
        return pl.pallas_call(
            functools.partial(body), name="norm_matmul_h", grid=(t // tm, n // tn),
            in_specs=in_specs, out_specs=[o_spec, h_spec],
            out_shape=[o_shape, jax.ShapeDtypeStruct((t, D_MODEL), BF16)],
            compiler_params=_params(2))(x, g, w_t)
    return pl.pallas_call(
        functools.partial(body), name="norm_matmul", grid=(t // tm, n // tn),
        in_specs=in_specs, out_specs=o_spec, out_shape=o_shape,
        scratch_shapes=[pltpu.VMEM((tm, D_MODEL), BF16)],
        compiler_params=_params(2))(x, g, w_t)


def _head_mean_matrix(width):
    head = jnp.arange(width) // HEAD_DIM
    return (head[:, None] == head[None, :]).astype(F32) / HEAD_DIM


def qk_prep(qkv, gq_w, gk_w, cos_w, sin_w, mean_q, mean_k, tm):
    t = qkv.shape[0]

    def body(p_ref, gq_ref, gk_ref, cos_ref, sin_ref, mq_ref, mk_ref, q_ref, k_ref, v_ref):
        cos2, sin2 = cos_ref[...], sin_ref[...]
        q = p_ref[:, :D_ATTN]
        k = p_ref[:, D_ATTN:D_ATTN + D_KV]
        qn = q * lax.rsqrt(_dot_f32(q * q, mq_ref[...]) + EPS) * gq_ref[...]
        kn = k * lax.rsqrt(_dot_f32(k * k, mk_ref[...]) + EPS) * gk_ref[...]
        cos8, sin8 = _tile_lanes(cos2, D_ATTN // LANES), _tile_lanes(sin2, D_ATTN // LANES)
        q_ref[...] = ((qn * cos8 + _pair_swap(qn) * sin8) * Q_SCALE).astype(BF16)
        k_ref[...] = (kn * cos2 + _pair_swap(kn) * sin2).astype(BF16)
        v_ref[...] = p_ref[:, D_ATTN + D_KV:].astype(BF16)

    return pl.pallas_call(
        functools.partial(body), name="qk_prep", grid=(t // tm,),
        in_specs=[pl.BlockSpec((tm, D_QKV), lambda i: (i, 0)), _full((1, D_ATTN)), _full((1, D_KV)),
                  pl.BlockSpec((tm, LANES), lambda i: (i, 0)), pl.BlockSpec((tm, LANES), lambda i: (i, 0)),
                  _full((D_ATTN, D_ATTN)), _full((D_KV, D_KV))],
        out_specs=[pl.BlockSpec((tm, D_ATTN), lambda i: (i, 0)), pl.BlockSpec((tm, D_KV), lambda i: (i, 0)),
                   pl.BlockSpec((tm, D_KV), lambda i: (i, 0))],
        out_shape=[jax.ShapeDtypeStruct((t, D_ATTN), BF16), jax.ShapeDtypeStruct((t, D_KV), BF16),
                   jax.ShapeDtypeStruct((t, D_KV), BF16)],
        compiler_params=_params(1),
    )(qkv, gq_w, gk_w, cos_w, sin_w, mean_q, mean_k)


def qk_bwd(dq_rot, dk_rot, dv, qkv, gq_w, gk_w, cos_w, sin_w, mean_q, mean_k, tm):
    t = qkv.shape[0]

    def branch(raw, d_rot, gain, mean_mat, cos, sin, scale):
        r = lax.rsqrt(_dot_f32(raw * raw, mean_mat) + EPS)
        n = raw * r
        dy = (d_rot * cos - _pair_swap(d_rot) * sin) * scale
        dn = dy * gain
        return r * (dn - n * _dot_f32(dn * n, mean_mat)), dy * n

    def body(dq_ref, dk_ref, dv_ref, p_ref, gq_ref, gk_ref, cos_ref, sin_ref, mq_ref, mk_ref,
             dp_ref, dgq_ref, dgk_ref):
        cos2, sin2 = cos_ref[...], sin_ref[...]
        cos8, sin8 = _tile_lanes(cos2, D_ATTN // LANES), _tile_lanes(sin2, D_ATTN // LANES)
        dq, dgq = branch(p_ref[:, :D_ATTN], dq_ref[...], gq_ref[...], mq_ref[...], cos8, sin8, HEAD_DIM ** -0.5)
        dk, dgk = branch(p_ref[:, D_ATTN:D_ATTN + D_KV], dk_ref[...], gk_ref[...], mk_ref[...], cos2, sin2, 1.0)
        dp_ref[...] = jnp.concatenate([dq, dk, dv_ref[...]], axis=-1).astype(BF16)

        @pl.when(pl.program_id(0) == 0)
        def _():
            dgq_ref[...] = jnp.zeros_like(dgq_ref)
            dgk_ref[...] = jnp.zeros_like(dgk_ref)

        dgq_ref[...] += _colsum(dgq)
        dgk_ref[...] += _colsum(dgk)

    return pl.pallas_call(
        functools.partial(body), name="qk_bwd", grid=(t // tm,),
        in_specs=[pl.BlockSpec((tm, D_ATTN), lambda i: (i, 0)), pl.BlockSpec((tm, D_KV), lambda i: (i, 0)),
                  pl.BlockSpec((tm, D_KV), lambda i: (i, 0)), pl.BlockSpec((tm, D_QKV), lambda i: (i, 0)),
                  _full((1, D_ATTN)), _full((1, D_KV)),
                  pl.BlockSpec((tm, LANES), lambda i: (i, 0)), pl.BlockSpec((tm, LANES), lambda i: (i, 0)),
                  _full((D_ATTN, D_ATTN)), _full((D_KV, D_KV))],
        out_specs=[pl.BlockSpec((tm, D_QKV), lambda i: (i, 0)), _full((1, D_ATTN)), _full((1, D_KV))],
        out_shape=[jax.ShapeDtypeStruct((t, D_QKV), BF16), jax.ShapeDtypeStruct((1, D_ATTN), F32),
                   jax.ShapeDtypeStruct((1, D_KV), F32)],
        compiler_params=_params(1),
    )(dq_rot, dk_rot, dv, qkv, gq_w, gk_w, cos_w, sin_w, mean_q, mean_k)


def attention_fwd(q_t, k, v_t, comm=None):
    _, nq, _, tq = q_t.shape
    _, nk, tk, _ = k.shape

    def body(q_ref, k_ref, v_ref, o_ref, lse_ref, s_scr, p_scr):
        q = q_ref[...]
        s_scr[0] = _dot(k_ref[0], q)
        p_scr[1] = jnp.zeros((tk, tq), BF16)

        def step(j, slot, carry):
            m, l, acc = carry
            s = s_scr[slot]
            pv = _dot(v_ref[jnp.maximum(j - 1, 0)], p_scr[1 - slot])
            s_scr[1 - slot] = _dot(k_ref[jnp.minimum(j + 1, nk - 1)], q)
            m_new = jnp.maximum(m, jnp.max(s, axis=0, keepdims=True))
            p = jnp.exp2(s - m_new)
            alpha = jnp.exp2(m - m_new)
            p_scr[slot] = p.astype(BF16)
            return m_new, alpha * l + jnp.sum(p, axis=0, keepdims=True), alpha * (acc + pv)

        m, l, acc = _loop_pairs(
            nk, step,
            (jnp.full((1, tq), -1e30, F32), jnp.zeros((1, tq), F32), jnp.zeros((HEAD_DIM, tq), F32)))
        acc = acc + _dot(v_ref[nk - 1], p_scr[(nk - 1) % 2])
        o_ref[...] = acc / l
        lse_ref[...] = m + jnp.log2(l)

    return _pallas(
        comm, body, name="attention_fwd", grid=(N_HEADS, nq),
        in_specs=[pl.BlockSpec((None, None, HEAD_DIM, tq), lambda h, i: (h, i, 0, 0)),
                  pl.BlockSpec((None, nk, tk, HEAD_DIM), lambda h, i: (h // KV_GROUP, 0, 0, 0)),
                  pl.BlockSpec((None, nk, HEAD_DIM, tk), lambda h, i: (h // KV_GROUP, 0, 0, 0))],
        out_specs=[pl.BlockSpec((None, None, HEAD_DIM, tq), lambda h, i: (h, i, 0, 0)),
                   pl.BlockSpec((None, None, 1, tq), lambda h, i: (h, i, 0, 0))],
        out_shape=[jax.ShapeDtypeStruct((N_HEADS, nq, HEAD_DIM, tq), F32),
                   jax.ShapeDtypeStruct((N_HEADS, nq, 1, tq), F32)],
        scratch_shapes=[pltpu.VMEM((2, tk, tq), F32), pltpu.VMEM((2, tk, tq), BF16)],
        args=(q_t, k, v_t))


def attention_delta(do_t, o_t):
    _, nq, _, tq = o_t.shape

    def body(do_ref, o_ref, d_ref):
        d_ref[...] = jnp.sum(do_ref[...] * o_ref[...], axis=0, keepdims=True)

    spec = pl.BlockSpec((None, None, HEAD_DIM, tq), lambda h, i: (h, i, 0, 0))
    return pl.pallas_call(
        functools.partial(body), name="attention_delta", grid=(N_HEADS, nq),
        in_specs=[spec, spec],
        out_specs=pl.BlockSpec((None, None, 1, tq), lambda h, i: (h, i, 0, 0)),
        out_shape=jax.ShapeDtypeStruct((N_HEADS, nq, 1, tq), F32),
        compiler_params=_params(2),
    )(do_t, o_t)


def attention_bwd(q_t, do_t, lse, delta, k, k_t, v, comm=None):
    _, nq, _, tq = q_t.shape
    _, nk, tk, _ = k.shape

    def body(q_ref, do_ref, lse_ref, delta_ref, k_ref, kt_ref, v_ref, dq_ref, dk_ref, dv_ref,
             s_scr, dp_scr, p_scr, ds_scr):
        @pl.when(pl.program_id(1) == 0)
        def _():
            dq_ref[...] = jnp.zeros_like(dq_ref)

        kk, kt, vv = k_ref[...], kt_ref[...], v_ref[...]
        n = KV_GROUP * nq
        s_scr[0] = _dot(kk, q_ref[0, 0])
        dp_scr[0] = _dot(vv, do_ref[0, 0])
        p_scr[1] = jnp.zeros((tk, tq), BF16)
        ds_scr[1] = jnp.zeros((tk, tq), BF16)

        def products(t, slot, dk, dv):
            h, i = t // nq, t % nq
            ds = ds_scr[slot]
            dq_ref[h, i] += _dot(kt, ds)
            return dk + _dot_nt(q_ref[h, i], ds), dv + _dot_nt(do_ref[h, i], p_scr[slot])

        def step(t, slot, carry):
            s, dp = s_scr[slot], dp_scr[slot]
            dk, dv = products(jnp.maximum(t - 1, 0), 1 - slot, *carry)
            nxt = jnp.minimum(t + 1, n - 1)
            s_scr[1 - slot] = _dot(kk, q_ref[nxt // nq, nxt % nq])
            dp_scr[1 - slot] = _dot(vv, do_ref[nxt // nq, nxt % nq])
            h, i = t // nq, t % nq
            p = jnp.exp2(s - lse_ref[h, i])
            p_scr[slot] = p.astype(BF16)
            ds_scr[slot] = (p * (dp - delta_ref[h, i])).astype(BF16)
            return dk, dv

        zero = jnp.zeros((HEAD_DIM, tk), F32)
        dk, dv = products(n - 1, (n - 1) % 2, *_loop_pairs(n, step, (zero, zero)))
        dk_ref[...] = dk * (1.0 / LOG2_E)
        dv_ref[...] = dv

    group = lambda g, j: (g, 0, 0, 0)
    tile = lambda g, j: (g, j, 0, 0)
    return _pallas(
        comm, body, name="attention_bwd", grid=(N_KV_HEADS, nk),
        in_specs=[pl.BlockSpec((KV_GROUP, nq, HEAD_DIM, tq), group),
                  pl.BlockSpec((KV_GROUP, nq, HEAD_DIM, tq), group),
                  pl.BlockSpec((KV_GROUP, nq, 1, tq), group),
                  pl.BlockSpec((KV_GROUP, nq, 1, tq), group),
                  pl.BlockSpec((None, None, tk, HEAD_DIM), tile),
                  pl.BlockSpec((None, None, HEAD_DIM, tk), tile),
                  pl.BlockSpec((None, None, tk, HEAD_DIM), tile)],
        out_specs=[pl.BlockSpec((KV_GROUP, nq, HEAD_DIM, tq), group),
                   pl.BlockSpec((None, None, HEAD_DIM, tk), tile),
                   pl.BlockSpec((None, None, HEAD_DIM, tk), tile)],
        out_shape=[jax.ShapeDtypeStruct((N_HEADS, nq, HEAD_DIM, tq), F32),
                   jax.ShapeDtypeStruct((N_KV_HEADS, nk, HEAD_DIM, tk), F32),
                   jax.ShapeDtypeStruct((N_KV_HEADS, nk, HEAD_DIM, tk), F32)],
        scratch_shapes=[pltpu.VMEM((2, tk, tq), F32), pltpu.VMEM((2, tk, tq), F32),
                        pltpu.VMEM((2, tk, tq), BF16), pltpu.VMEM((2, tk, tq), BF16)],
        args=(q_t, do_t, lse, delta, k, k_t, v))


def _group_select(parts):
    lane_group = lax.broadcasted_iota(jnp.int32, parts[0].shape, 1) // SGU_GROUP_DIM
    out = parts[0]
    for g in range(1, N_SGU_GROUPS):
        out = jnp.where(lane_group == g, parts[g], out)
    return out


def _gate_forward(z, g_sgu, ws_ref, bias):
    gz, th = _gelu(z)
    u, vv = gz[:, :D_SGU], gz[:, D_SGU:]
    rv = _rstd(vv)
    nv = vv * rv
    vn = (nv * g_sgu).astype(BF16)
    fs = []
    for c in range(z.shape[0] // CHUNK):
        vc = vn[c * CHUNK:(c + 1) * CHUNK]
        fs.append(_group_select([_dot(ws_ref[g], vc) for g in range(N_SGU_GROUPS)]) + bias)
    f = jnp.concatenate(fs, axis=0) if len(fs) > 1 else fs[0]
    return th, u, rv, nv, vn, f


def mix_out(z, o, x, g_sgu, g_ao, g_so, ws, bias, w_out, tm):
    t = x.shape[0]

    def body(z_ref, o_ref, x_ref, gs_ref, gao_ref, gso_ref, ws_ref, bias_ref, wout_ref, x2_ref, mixed_ref):
        _, u, _, _, _, f = _gate_forward(z_ref[...], gs_ref[...], ws_ref, bias_ref[...])
        sgu = u * f
        oo = o_ref[...]
        mixed = jnp.concatenate([oo * _rstd(oo) * gao_ref[...], sgu * _rstd(sgu) * gso_ref[...]], axis=-1).astype(BF16)
        mixed_ref[...] = mixed
        x2_ref[...] = x_ref[...] + _dot(mixed, wout_ref[...])

    row = lambda n: pl.BlockSpec((tm, n), lambda i: (i, 0))
    return pl.pallas_call(
        functools.partial(body), name="mix_out", grid=(t // tm,),
        in_specs=[row(2 * D_SGU), row(D_ATTN), row(D_MODEL), _full((1, D_SGU)), _full((1, D_ATTN)), _full((1, D_SGU)),
                  _full((N_SGU_GROUPS, CHUNK, CHUNK)), _full((CHUNK, D_SGU)), _full((D_MODEL, D_MODEL))],
        out_specs=[row(D_MODEL), row(D_MODEL)],
        out_shape=[jax.ShapeDtypeStruct((t, D_MODEL), F32), jax.ShapeDtypeStruct((t, D_MODEL), BF16)],
        compiler_params=_params(1),
    )(z, o, x, g_sgu, g_ao, g_so, ws, bias, w_out)


def mix_bwd(dx2, z, o, g_sgu, g_ao, g_so, ws, ws_t, bias, w_out, group_ind, tm):
    t = dx2.shape[0]
    n_tiles = t // tm

    def body(dx_ref, z_ref, o_ref, gs_ref, gao_ref, gso_ref, ws_ref, wst_ref, bias_ref, wout_ref, ind_ref,
             do_ref, dz_ref, dg_ref, dws_ref, dbs_ref, df_sum):
        step = pl.program_id(0)

        @pl.when(step == 0)
        def _():
            dg_ref[...] = jnp.zeros_like(dg_ref)
            dws_ref[...] = jnp.zeros_like(dws_ref)
            df_sum[...] = jnp.zeros_like(df_sum)

        z = z_ref[...]
        th, u, rv, nv, vn, f = _gate_forward(z, gs_ref[...], ws_ref, bias_ref[...])
        dmixed = _dot_nt(dx_ref[...].astype(BF16), wout_ref[...])
        oo = o_ref[...]
        ro = _rstd(oo)
        d_o, dgao = _rms_bwd(dmixed[:, :D_ATTN], oo * ro, ro, gao_ref[...])
        do_ref[...] = d_o
        sgu = u * f
        rs = _rstd(sgu)
        dsgu, dgso = _rms_bwd(dmixed[:, D_ATTN:], sgu * rs, rs, gso_ref[...])
        du = dsgu * f
        df = dsgu * u
        lane_group = lax.broadcasted_iota(jnp.int32, (CHUNK, D_SGU), 1) // SGU_GROUP_DIM
        dvns = []
        df_acc = jnp.zeros((CHUNK, D_SGU), F32)
        for c in range(tm // CHUNK):
            dfc32 = df[c * CHUNK:(c + 1) * CHUNK]
            dfc = dfc32.astype(BF16)
            vc = vn[c * CHUNK:(c + 1) * CHUNK]
            dvns.append(_group_select([_dot(wst_ref[g], dfc) for g in range(N_SGU_GROUPS)]))
            for g in range(N_SGU_GROUPS):
                dws_ref[g] += _dot_nt(jnp.where(lane_group == g, dfc, jnp.zeros_like(dfc)), vc)
            df_acc = df_acc + dfc32
        df_sum[...] += df_acc
        dvn = jnp.concatenate(dvns, axis=0) if len(dvns) > 1 else dvns[0]
        dvv, dgs = _rms_bwd(dvn, nv, rv, gs_ref[...])
        dz_ref[...] = (jnp.concatenate([du, dvv], axis=-1) * _gelu_grad(z, th)).astype(BF16)
        dg_ref[0:1, :] += _colsum(dgao)
        dg_ref[1:2, :] += _colsum(dgso)
        dg_ref[2:3, :] += _colsum(dgs)

        @pl.when(step == n_tiles - 1)
        def _():
            dbs_ref[...] = _dot_f32(df_sum[...], ind_ref[...])

    row = lambda n: pl.BlockSpec((tm, n), lambda i: (i, 0))
    return pl.pallas_call(
        functools.partial(body), name="mix_bwd", grid=(n_tiles,),
        in_specs=[row(D_MODEL), row(2 * D_SGU), row(D_ATTN), _full((1, D_SGU)), _full((1, D_ATTN)), _full((1, D_SGU)),
                  _full((N_SGU_GROUPS, CHUNK, CHUNK)), _full((N_SGU_GROUPS, CHUNK, CHUNK)), _full((CHUNK, D_SGU)),
                  _full((D_MODEL, D_MODEL)), _full((D_SGU, LANES))],
        out_specs=[row(D_ATTN), row(2 * D_SGU), _full((8, D_SGU)), _full((N_SGU_GROUPS, CHUNK, CHUNK)),
                   _full((CHUNK, LANES))],
        out_shape=[jax.ShapeDtypeStruct((t, D_ATTN), F32), jax.ShapeDtypeStruct((t, 2 * D_SGU), BF16),
                   jax.ShapeDtypeStruct((8, D_SGU), F32),
                   jax.ShapeDtypeStruct((N_SGU_GROUPS, CHUNK, CHUNK), F32),
                   jax.ShapeDtypeStruct((CHUNK, LANES), F32)],
        scratch_shapes=[pltpu.VMEM((CHUNK, D_SGU), F32)],
        compiler_params=_params(1),
    )(dx2, z, o, g_sgu, g_ao, g_so, ws, ws_t, bias, w_out, group_ind)


def loss_bwd(x, g, target, tm):
    t = x.shape[0]

    def body(x_ref, g_ref, t_ref, loss_ref, dx_ref, dg_ref):
        @pl.when(pl.program_id(0) == 0)
        def _():
            loss_ref[...] = jnp.zeros_like(loss_ref)
            dg_ref[...] = jnp.zeros_like(dg_ref)

        xx = x_ref[...]
        r = _rstd(xx)
        n = xx * r
        err = n * g_ref[...] - t_ref[...]
        per_token = jnp.mean(err * err, axis=-1, keepdims=True)
        loss_ref[...] += 0.5 * jnp.sum(per_token, axis=0, keepdims=True)
        dx, dg_rows = _rms_bwd(err * (1.0 / D_MODEL), n, r, g_ref[...])
        dx_ref[...] = dx
        dg_ref[...] += _colsum(dg_rows)

    row = pl.BlockSpec((tm, D_MODEL), lambda i: (i, 0))
    return pl.pallas_call(
        functools.partial(body), name="loss_bwd", grid=(t // tm,),
        in_specs=[row, _full((1, D_MODEL)), row],
        out_specs=[_full((1, LANES)), row, _full((1, D_MODEL))],
        out_shape=[jax.ShapeDtypeStruct((1, LANES), F32), jax.ShapeDtypeStruct((t, D_MODEL), F32),
                   jax.ShapeDtypeStruct((1, D_MODEL), F32)],
        compiler_params=_params(1),
    )(x, g, target)


def _rope_tables(t):
    rows = t // GRID_W
    row_idx = jnp.repeat(jnp.arange(rows, dtype=F32), GRID_W)
    col_idx = jnp.tile(jnp.arange(GRID_W, dtype=F32), rows)
    axis_dim = HEAD_DIM // 2
    inv = 1.0 / (ROPE_THETA ** (jnp.arange(0, axis_dim, 2, dtype=F32) / axis_dim))
    ang = jnp.concatenate([row_idx[:, None] * inv, col_idx[:, None] * inv], axis=-1)
    cos = jnp.repeat(jnp.cos(ang), 2, axis=-1)
    sin = jnp.repeat(jnp.sin(ang), 2, axis=-1) * jnp.tile(jnp.array([-1.0, 1.0], F32), HEAD_DIM // 2)
    return jnp.tile(cos, (1, LANES // HEAD_DIM)), jnp.tile(sin, (1, LANES // HEAD_DIM))


def _heads_to_tiles_t(a, n_heads, tile):
    t = a.shape[0]
    return a.reshape(t // tile, tile, n_heads, HEAD_DIM).transpose(2, 0, 3, 1)


def _heads_to_tiles(a, n_heads, tile):
    t = a.shape[0]
    return a.reshape(t // tile, tile, n_heads, HEAD_DIM).transpose(2, 0, 1, 3)


def _tiles_t_to_heads(a):
    h, n, _, tile = a.shape
    return a.transpose(1, 3, 0, 2).reshape(n * tile, h * HEAD_DIM)


def kernel(x, g_ffn1, w1_gate, w1_up, w1_down, g_mix, w_in, g_q, g_k, g_sgu, w_s, b_s, g_attn_out, g_sgu_out, w_out, g_ffn2, w2_gate, w2_up, w2_down, g_final, loss_target, m_g_ffn1, m_w1_gate, m_w1_up, m_w1_down, m_g_mix, m_w_in, m_g_q, m_g_k, m_g_sgu, m_w_s, m_b_s, m_g_attn_out, m_g_sgu_out, m_w_out, m_g_ffn2, m_w2_gate, m_w2_up, m_w2_down, m_g_final, v_g_ffn1, v_w1_gate, v_w1_up, v_w1_down, v_g_mix, v_w_in, v_g_q, v_g_k, v_g_sgu, v_w_s, v_b_s, v_g_attn_out, v_g_sgu_out, v_w_out, v_g_ffn2, v_w2_gate, v_w2_up, v_w2_down, v_g_final):
    weights = dict(g_ffn1=g_ffn1, w1_gate=w1_gate, w1_up=w1_up, w1_down=w1_down, g_mix=g_mix, w_in=w_in, g_q=g_q,
                   g_k=g_k, g_sgu=g_sgu, w_s=w_s, b_s=b_s, g_attn_out=g_attn_out, g_sgu_out=g_sgu_out, w_out=w_out,
                   g_ffn2=g_ffn2, w2_gate=w2_gate, w2_up=w2_up, w2_down=w2_down, g_final=g_final)
    m_in = dict(g_ffn1=m_g_ffn1, w1_gate=m_w1_gate, w1_up=m_w1_up, w1_down=m_w1_down, g_mix=m_g_mix, w_in=m_w_in,
                g_q=m_g_q, g_k=m_g_k, g_sgu=m_g_sgu, w_s=m_w_s, b_s=m_b_s, g_attn_out=m_g_attn_out,
                g_sgu_out=m_g_sgu_out, w_out=m_w_out, g_ffn2=m_g_ffn2, w2_gate=m_w2_gate, w2_up=m_w2_up,
                w2_down=m_w2_down, g_final=m_g_final)
    v_in = dict(g_ffn1=v_g_ffn1, w1_gate=v_w1_gate, w1_up=v_w1_up, w1_down=v_w1_down, g_mix=v_g_mix, w_in=v_w_in,
                g_q=v_g_q, g_k=v_g_k, g_sgu=v_g_sgu, w_s=v_w_s, b_s=v_b_s, g_attn_out=v_g_attn_out,
                g_sgu_out=v_g_sgu_out, w_out=v_w_out, g_ffn2=v_g_ffn2, w2_gate=v_w2_gate, w2_up=v_w2_up,
                w2_down=v_w2_down, g_final=v_g_final)
    names = list(weights)

    t = x.shape[1]
    x0 = x[0]
    target = loss_target[0]
    tm = min(256, t)
    tm_ff = min(512, t)
    tn_ff = 256
    tq = min(512, t)
    tk = min(256, t)
    tk_w = min(512, t)

    def shard_rows(name):
        w = weights[name][0]
        return (w.T if name in TRANSPOSED else w).astype(BF16)

    rows_of = dict(SHARD_ROWS)
    full = {}

    def gather_of(group):
        return gather_exchange(jnp.concatenate([shard_rows(n) for n in group], axis=0), [rows_of[n] for n in group])

    def take(group, gathered):
        for n, g in zip(group, gathered):
            full[n] = g.reshape(N_DEV * rows_of[n], D_MODEL)

    first, second, third = ("w1_gate", "w1_up"), ("w1_down", "w_in", "w_out"), ("w2_gate", "w2_up", "w2_down")
    take(first, run_exchange(gather_of(first), "gather_first"))

    (h1, a1, b1, act1), gathered = ffn_up(x0, g_ffn1, full["w1_gate"], full["w1_up"], tm_ff, tn_ff, gather_of(second))
    take(second, gathered)
    w_in_t = full["w_in"]
    w_qkv_t, w_z_t = w_in_t[:D_QKV], w_in_t[D_QKV:]
    x1 = ffn_down(act1, full["w1_down"], x0, tm)

    qkv, h2 = norm_matmul(x1, g_mix, w_qkv_t, tm_ff, 256, True)
    z = norm_matmul(x1, g_mix, w_z_t, tm_ff, 256, False)
    cos_w, sin_w = _rope_tables(t)
    gq_w = jnp.tile(g_q, (1, N_HEADS))
    gk_w = jnp.tile(g_k, (1, N_KV_HEADS))
    mean_q, mean_k = _head_mean_matrix(D_ATTN), _head_mean_matrix(D_KV)
    q_rot, k_rot, v_b = qk_prep(qkv, gq_w, gk_w, cos_w, sin_w, mean_q, mean_k, tm)
    q_t = _heads_to_tiles_t(q_rot, N_HEADS, tq)
    k_tiles = _heads_to_tiles(k_rot, N_KV_HEADS, tk)
    kt_tiles = _heads_to_tiles_t(k_rot, N_KV_HEADS, tk)
    v_tiles = _heads_to_tiles(v_b, N_KV_HEADS, tk)
    vt_tiles = _heads_to_tiles_t(v_b, N_KV_HEADS, tk)
    (o_t, lse), gathered = attention_fwd(q_t, k_tiles, vt_tiles, gather_of(third))
    take(third, gathered)
    o = _tiles_t_to_heads(o_t)

    ws_b = w_s[0].astype(BF16)
    ws_tb = jnp.swapaxes(w_s[0], 1, 2).astype(BF16)
    bias = jnp.repeat(b_s[0].T, SGU_GROUP_DIM, axis=1)
    x2, mixed = mix_out(z, o, x1, g_sgu, g_attn_out, g_sgu_out, ws_b, bias, full["w_out"], tm)

    (h3, a2, b2, act2), _ = ffn_up(x2, g_ffn2, full["w2_gate"], full["w2_up"], tm_ff, tn_ff)
    x3 = ffn_down(act2, full["w2_down"], x2, tm)

    loss_part, dx3, dg_final = loss_bwd(x3, g_final, target, tm)
    loss = lax.psum(loss_part[0, 0], MESH_AXES)

    tmm = D_FF // 2
    (da2, db2), _ = ffn_bwd_act(dx3, full["w2_down"], a2, b2, tm_ff, tn_ff)
    (dx2, dg_ffn2), _ = norm_bwd_matmul(da2, full["w2_gate"], db2, full["w2_up"], x2, g_ffn2, dx3, tm)
    dwg2, dwu2 = matmul_tn(da2, h3, 1.0, tmm, tk_w), matmul_tn(db2, h3, 1.0, tmm, tk_w)
    dwd2 = matmul_tn(act2, dx3, 0.5, tmm, tk_w)

    group_ind = (jnp.arange(D_SGU)[:, None] // SGU_GROUP_DIM == jnp.arange(LANES)[None, :]).astype(F32)
    d_o, dz, dg_mixrow, dws, dbs = mix_bwd(dx2, z, o, g_sgu, g_attn_out, g_sgu_out, ws_b, ws_tb, bias, full["w_out"],
                                           group_ind, tm)
    dw_out = matmul_tn(mixed, dx2, 1.0, D_MODEL // 2, tk_w)

    do_t = _heads_to_tiles_t(d_o, N_HEADS, tq)
    delta = attention_delta(do_t, o_t)
    group_a = ("w2_gate", "w2_up", "w2_down", "w_out")
    (dq_t, dk_t, dv_t), (parts_a,) = attention_bwd(q_t, do_t.astype(BF16), lse, delta, k_tiles, kt_tiles, v_tiles,
                                                   scatter_exchange([dwg2, dwu2, dwd2, dw_out]))
    dqkv, dgq_w, dgk_w = qk_bwd(_tiles_t_to_heads(dq_t), _tiles_t_to_heads(dk_t), _tiles_t_to_heads(dv_t), qkv,
                                gq_w, gk_w, cos_w, sin_w, mean_q, mean_k, tm)
    (dx1, dg_mix), _ = norm_bwd_matmul(dqkv, w_qkv_t, dz, w_z_t, x1, g_mix, dx2, tm)
    dw_in = jnp.concatenate([matmul_tn(dqkv, h2, 1.0, D_QKV // 2, tk_w), matmul_tn(dz, h2, 1.0, D_SGU, tk_w)], axis=0)

    dwd1 = matmul_tn(act1, dx1, 0.5, tmm, tk_w)
    group_b = ("w_in", "w1_down")
    (da1, db1), (parts_b,) = ffn_bwd_act(dx1, full["w1_down"], a1, b1, tm_ff, tn_ff, scatter_exchange([dw_in, dwd1]))
    dwg1, dwu1 = matmul_tn(da1, h1, 1.0, tmm, tk_w), matmul_tn(db1, h1, 1.0, tmm, tk_w)
    group_c = ("w1_gate", "w1_up")
    (dx0, dg_ffn1), (parts_c,) = norm_bwd_matmul(da1, full["w1_gate"], db1, full["w1_up"], x0, g_ffn1, dx1, tm,
                                                 scatter_exchange([dwg1, dwu1]))

    small_grads = dict(
        g_ffn1=dg_ffn1, g_mix=dg_mix, g_ffn2=dg_ffn2, g_final=dg_final,
        g_q=dgq_w.reshape(N_HEADS, HEAD_DIM).sum(0)[None], g_k=dgk_w.reshape(N_KV_HEADS, HEAD_DIM).sum(0)[None],
        g_attn_out=dg_mixrow[0:1], g_sgu_out=dg_mixrow[1:2], g_sgu=dg_mixrow[2:3],
        w_s=dws[None], b_s=dbs[:, :N_SGU_GROUPS].T[None])
    small_names = [n for n in names if n not in rows_of]
    pieces = []
    for n in small_names:
        flat = small_grads[n].reshape(-1)
        flat = jnp.pad(flat, (0, (-flat.shape[0]) % (8 * LANES)))
        pieces.append(flat.reshape(-1, LANES))
    small_rows = [p.shape[0] for p in pieces]
    small_pack = jnp.concatenate(pieces, axis=0)
    (small_parts,) = run_exchange(gather_exchange(small_pack, [small_pack.shape[0]]), "gather_small_grads")
    small_sum = sum_parts(small_parts, small_pack.shape[0])

    grads = {}
    for group, parts in ((group_a, parts_a), (group_b, parts_b), (group_c, parts_c)):
        summed = sum_parts(parts, 32)
        off = 0
        for n in group:
            gsh = summed[off:off + rows_of[n]]
            grads[n] = (gsh.T if n in TRANSPOSED else gsh)[None]
            off += rows_of[n]
    off = 0
    for n, r in zip(small_names, small_rows):
        grads[n] = small_sum[off:off + r].reshape(-1)[:weights[n].size].reshape(weights[n].shape)
        off += r

    delta_w, new_m, new_v = {}, {}, {}
    for n in names:
        shape = weights[n].shape
        as2d = (lambda a: a.reshape(-1, shape[-1]))
        d, m2, v2 = adamw(as2d(weights[n]), as2d(grads[n]), as2d(m_in[n]), as2d(v_in[n]))
        delta_w[n], new_m[n], new_v[n] = d.reshape(shape), m2.reshape(shape), v2.reshape(shape)

    return (loss, dx0[None], *[grads[n] for n in names], *[delta_w[n] for n in names],
            *[new_m[n] for n in names], *[new_v[n] for n in names])
```

```python
import functools
import math

import jax
import jax.numpy as jnp
from jax import lax
from jax.experimental import pallas as pl
from jax.experimental.pallas import tpu as pltpu

F32 = jnp.float32
BF16 = jnp.bfloat16

D_MODEL = 1024
D_FF = 2816
N_HEADS = 8
HEAD_DIM = 64
N_KV_HEADS = 2
KV_GROUP = N_HEADS // N_KV_HEADS
D_ATTN = N_HEADS * HEAD_DIM
D_KV = N_KV_HEADS * HEAD_DIM
D_QKV = D_ATTN + 2 * D_KV
N_SGU_GROUPS = 8
SGU_GROUP_DIM = 64
D_SGU = N_SGU_GROUPS * SGU_GROUP_DIM
CHUNK = 128
GRID_W = 64
ROPE_THETA = 10000.0
EPS = 1e-6
N_DEV = 8
LANES = 128

LOG2_E = math.log2(math.e)
Q_SCALE = HEAD_DIM ** -0.5 * LOG2_E

ADAM_LR = 0.001
ADAM_B1 = 0.9
ADAM_B2 = 0.999
ADAM_EPS = 1e-08
ADAM_WD = 0.01
ADAM_STEP = 10

MESH_AXES = ("x", "y", "c")
MESH_IDS = pl.DeviceIdType.MESH

VMEM_LIMIT = 56 * 1024 * 1024

SHARD_ROWS = (("w1_gate", D_FF // N_DEV), ("w1_up", D_FF // N_DEV), ("w1_down", D_FF // N_DEV),
              ("w_in", (D_QKV + 2 * D_SGU) // N_DEV), ("w_out", D_MODEL // N_DEV),
              ("w2_gate", D_FF // N_DEV), ("w2_up", D_FF // N_DEV), ("w2_down", D_FF // N_DEV))
PACK_ROWS = sum(r for _, r in SHARD_ROWS)
TRANSPOSED = ("w1_gate", "w1_up", "w_in", "w2_gate", "w2_up")


def _params(n_grid):
    return pltpu.CompilerParams(dimension_semantics=("arbitrary",) * n_grid, vmem_limit_bytes=VMEM_LIMIT)


def _dot(a, b):
    return jnp.dot(a, b, preferred_element_type=F32)


def _dot_nt(a, b):
    return lax.dot_general(a, b, (((1,), (1,)), ((), ())), preferred_element_type=F32)


def _dot_tn(a, b):
    return lax.dot_general(a, b, (((0,), (0,)), ((), ())), preferred_element_type=F32)


def _dot_f32(a, b):
    return jnp.dot(a, b, preferred_element_type=F32, precision=lax.Precision.HIGHEST)


def _rstd(x):
    return lax.rsqrt(jnp.mean(x * x, axis=-1, keepdims=True) + EPS)


def _rms_bwd(dy, n, r, g):
    dn = dy * g
    return r * (dn - n * jnp.mean(dn * n, axis=-1, keepdims=True)), dy * n


def _colsum(a):
    return jnp.sum(a, axis=0, keepdims=True)


_GELU_C = math.sqrt(2.0 / math.pi)


def _gelu(x):
    t = jnp.tanh(_GELU_C * (x + 0.044715 * (x * x * x)))
    return x * (0.5 * (1.0 + t)), t


def _gelu_grad(x, t):
    return 0.5 * (1.0 + t) + 0.5 * x * (1.0 - t * t) * (_GELU_C * (1.0 + 3 * 0.044715 * x * x))


def _pair_swap(a):
    w = a.shape[-1]
    lane = lax.broadcasted_iota(jnp.int32, a.shape, a.ndim - 1)
    return jnp.where(lane % 2 == 0, pltpu.roll(a, w - 1, a.ndim - 1), pltpu.roll(a, 1, a.ndim - 1))


def _tile_lanes(a, reps):
    return jnp.concatenate([a] * reps, axis=-1) if reps > 1 else a


def _loop_pairs(n, step, carry):
    assert n % 2 == 0, n

    def pair(jj, c):
        return step(2 * jj + 1, 1, step(2 * jj, 0, c))

    return lax.fori_loop(0, n // 2, pair, carry)


def _full(shape):
    nd = len(shape)
    return pl.BlockSpec(shape, lambda *_: (0,) * nd)


def _mesh_pos():
    return lax.axis_index("x"), lax.axis_index("y"), lax.axis_index("c")


def _peer(pos, d):
    x, y, c = pos
    px = 1 - x if d & 4 else x
    py = 1 - y if d & 2 else y
    pc = 1 - c if d & 1 else c
    return (px, py, pc), 4 * px + 2 * py + pc


class _Exchange:
    def __init__(self, operands, out_shape, n_local, plan):
        self.operands = list(operands)
        self.out_shape = list(out_shape)
        self.sem_shapes = [pltpu.SemaphoreType.DMA((N_DEV - 1,)), pltpu.SemaphoreType.DMA((N_DEV - 1,)),
                           pltpu.SemaphoreType.DMA((n_local,))]
        self._plan = plan

    def _copies(self, in_refs, out_refs):
        pos = _mesh_pos()
        return pos, self._plan(4 * pos[0] + 2 * pos[1] + pos[2], in_refs, out_refs)

    def start(self, in_refs, out_refs, sems):
        send_sems, recv_sems, local_sems = sems
        pos, (local, remote, _) = self._copies(in_refs, out_refs)
        for k, (src, dst) in enumerate(local):
            pltpu.make_async_copy(src, dst, local_sems.at[k]).start()
        for d in range(1, N_DEV):
            peer, peer_lin = _peer(pos, d)
            for src, dst in remote(peer_lin):
                pltpu.make_async_remote_copy(src_ref=src, dst_ref=dst, send_sem=send_sems.at[d - 1],
                                             recv_sem=recv_sems.at[d - 1], device_id=peer,
                                             device_id_type=MESH_IDS).start()

    def wait(self, in_refs, out_refs, sems):
        send_sems, recv_sems, local_sems = sems
        pos, (local, _, whole) = self._copies(in_refs, out_refs)
        for d in range(1, N_DEV):
            peer, peer_lin = _peer(pos, d)
            ref = whole(peer_lin)
            everything = pltpu.make_async_remote_copy(src_ref=ref, dst_ref=ref, send_sem=send_sems.at[d - 1],
                                                      recv_sem=recv_sems.at[d - 1], device_id=peer,
                                                      device_id_type=MESH_IDS)
            everything.wait_send()
            everything.wait_recv()
        for k, (src, dst) in enumerate(local):
            pltpu.make_async_copy(src, dst, local_sems.at[k]).wait()


def _offsets(rows):
    offs, o = [], 0
    for r in rows:
        offs.append(o)
        o += r
    return offs


def gather_exchange(src, rows):
    offs = _offsets(rows)

    def plan(me, in_refs, out_refs):
        pieces = [(in_refs[0].at[pl.ds(o, r)], out.at[me]) for o, r, out in zip(offs, rows, out_refs)]
        return pieces, (lambda peer_lin: pieces), (lambda peer_lin: in_refs[0])

    return _Exchange([src], [jax.ShapeDtypeStruct((N_DEV, r) + src.shape[1:], src.dtype) for r in rows],
                     len(rows), plan)


def scatter_exchange(grads):
    rows = [g.shape[0] // N_DEV for g in grads]
    offs = _offsets(rows)

    def plan(me, in_refs, out_refs):
        parts = out_refs[0]

        def slabs(owner):
            return [(g.at[pl.ds(pl.multiple_of(owner * r, 16), r)], parts.at[me, pl.ds(o, r)])
                    for g, o, r in zip(in_refs, offs, rows)]

        return slabs(me), slabs, (lambda peer_lin: parts.at[peer_lin])

    shape = jax.ShapeDtypeStruct((N_DEV, sum(rows)) + grads[0].shape[1:], grads[0].dtype)
    return _Exchange(grads, [shape], len(rows), plan)


def run_exchange(ex, name):
    n_in, n_out = len(ex.operands), len(ex.out_shape)

    def body(*refs):
        parts = refs[:n_in], refs[n_in:n_in + n_out], refs[n_in + n_out:]
        ex.start(*parts)
        ex.wait(*parts)

    any_spec = pl.BlockSpec(memory_space=pl.ANY)
    return pl.pallas_call(
        functools.partial(body), name=name, out_shape=ex.out_shape,
        in_specs=[any_spec] * n_in, out_specs=[any_spec] * n_out, scratch_shapes=ex.sem_shapes,
        compiler_params=pltpu.CompilerParams(has_side_effects=True),
    )(*ex.operands)


def _pallas(comm, body, *, name, grid, in_specs, out_specs, out_shape, args, scratch_shapes=()):
    params = _params(len(grid))
    if comm is None:
        res = pl.pallas_call(functools.partial(body), name=name, grid=grid, in_specs=list(in_specs),
                             out_specs=list(out_specs), out_shape=list(out_shape),
                             scratch_shapes=list(scratch_shapes), compiler_params=params)(*args)
        return list(res), []
    n_in, n_out, n_scr = len(in_specs), len(out_specs), len(scratch_shapes)
    c_in, c_out = len(comm.operands), len(comm.out_shape)

    def edge(last):
        conds = [pl.program_id(a) == (g - 1 if last else 0) for a, g in enumerate(grid)]
        return functools.reduce(jnp.logical_and, conds)

    def wrapped(*refs):
        refs = list(refs)
        ins, refs = refs[:n_in], refs[n_in:]
        cins, refs = refs[:c_in], refs[c_in:]
        outs, refs = refs[:n_out], refs[n_out:]
        couts, refs = refs[:c_out], refs[c_out:]
        scr, sems = refs[:n_scr], refs[n_scr:]

        @pl.when(edge(False))
        def _():
            comm.start(cins, couts, sems)

        body(*ins, *outs, *scr)

        @pl.when(edge(True))
        def _():
            comm.wait(cins, couts, sems)

    any_spec = pl.BlockSpec(memory_space=pl.ANY)
    res = pl.pallas_call(
        wrapped, name=name, grid=grid,
        in_specs=list(in_specs) + [any_spec] * c_in, out_specs=list(out_specs) + [any_spec] * c_out,
        out_shape=list(out_shape) + comm.out_shape, scratch_shapes=list(scratch_shapes) + comm.sem_shapes,
        compiler_params=pltpu.CompilerParams(dimension_semantics=("arbitrary",) * len(grid),
                                             vmem_limit_bytes=VMEM_LIMIT, has_side_effects=True),
    )(*args, *comm.operands)
    return res[:n_out], res[n_out:]


def sum_parts(parts, block_rows):
    n, rows, cols = parts.shape

    def body(p_ref, o_ref):
        acc = p_ref[0].astype(F32)
        for s in range(1, n):
            acc = acc + p_ref[s].astype(F32)
        o_ref[...] = acc

    return pl.pallas_call(
        functools.partial(body), name="sum_parts",
        grid=(rows // block_rows,),
        in_specs=[pl.BlockSpec((n, block_rows, cols), lambda i: (0, i, 0))],
        out_specs=pl.BlockSpec((block_rows, cols), lambda i: (i, 0)),
        out_shape=jax.ShapeDtypeStruct((rows, cols), F32),
        compiler_params=_params(1),
    )(parts)


def adamw(w, g, m, v):
    def body(w_ref, g_ref, m_ref, v_ref, d_ref, m_out, v_out):
        gg = g_ref[...]
        m2 = ADAM_B1 * m_ref[...] + (1.0 - ADAM_B1) * gg
        v2 = ADAM_B2 * v_ref[...] + (1.0 - ADAM_B2) * (gg * gg)
        m_hat = m2 / (1.0 - ADAM_B1 ** ADAM_STEP)
        v_hat = v2 / (1.0 - ADAM_B2 ** ADAM_STEP)
        d_ref[...] = -ADAM_LR * (m_hat / (jnp.sqrt(v_hat) + ADAM_EPS) + ADAM_WD * w_ref[...])
        m_out[...] = m2
        v_out[...] = v2

    spec = _full(w.shape)
    shape = jax.ShapeDtypeStruct(w.shape, F32)
    return pl.pallas_call(
        functools.partial(body), name="adamw",
        in_specs=[spec] * 4, out_specs=[spec] * 3, out_shape=[shape] * 3,
        compiler_params=pltpu.CompilerParams(vmem_limit_bytes=VMEM_LIMIT),
    )(w, g, m, v)


def ffn_up(x, g, wg_t, wu_t, tm, tn, comm=None):
    t = x.shape[0]

    def body(x_ref, g_ref, wg_ref, wu_ref, h_ref, a_ref, b_ref, act_ref):
        xx = x_ref[...]
        h = ((xx * _rstd(xx)) * g_ref[...]).astype(BF16)
        h_ref[...] = h
        for c in range(D_FF // tn):
            cols = slice(c * tn, (c + 1) * tn)
            a = _dot_nt(h, wg_ref[cols, :])
            b = _dot_nt(h, wu_ref[cols, :])
            a_ref[:, cols] = a.astype(BF16)
            b_ref[:, cols] = b.astype(BF16)
            act_ref[:, cols] = (a * jax.nn.sigmoid(a) * b).astype(BF16)

    wide = jax.ShapeDtypeStruct((t, D_FF), BF16)
    row = lambda n: pl.BlockSpec((tm, n), lambda i: (i, 0))
    return _pallas(
        comm, body, name="ffn_up",
        grid=(t // tm,),
        in_specs=[row(D_MODEL), _full((1, D_MODEL)), _full((D_FF, D_MODEL)), _full((D_FF, D_MODEL))],
        out_specs=[row(D_MODEL), row(D_FF), row(D_FF), row(D_FF)],
        out_shape=[jax.ShapeDtypeStruct((t, D_MODEL), BF16), wide, wide, wide],
        args=(x, g, wg_t, wu_t))


def ffn_down(act, wd, x, tm):
    t = x.shape[0]

    def body(act_ref, wd_ref, x_ref, o_ref):
        o_ref[...] = x_ref[...] + 0.5 * _dot(act_ref[...], wd_ref[...])

    return pl.pallas_call(
        functools.partial(body), name="ffn_down",
        grid=(t // tm,),
        in_specs=[pl.BlockSpec((tm, D_FF), lambda i: (i, 0)), _full((D_FF, D_MODEL)),
                  pl.BlockSpec((tm, D_MODEL), lambda i: (i, 0))],
        out_specs=pl.BlockSpec((tm, D_MODEL), lambda i: (i, 0)),
        out_shape=jax.ShapeDtypeStruct((t, D_MODEL), F32),
        compiler_params=_params(1),
    )(act, wd, x)


def ffn_bwd_act(dx, wd, a, b, tm, tn, comm=None):
    t = dx.shape[0]

    def body(dx_ref, wd_ref, a_ref, b_ref, da_ref, db_ref):
        dxb = (0.5 * dx_ref[...]).astype(BF16)
        for c in range(D_FF // tn):
            cols = slice(c * tn, (c + 1) * tn)
            dact = _dot_nt(dxb, wd_ref[cols, :])
            aa = a_ref[:, cols].astype(F32)
            sig = 0.5 * jnp.tanh(0.5 * aa) + 0.5
            silu = aa * sig
            da_ref[:, cols] = (dact * b_ref[:, cols].astype(F32) * (sig + silu * (1.0 - sig))).astype(BF16)
            db_ref[:, cols] = (dact * silu).astype(BF16)

    wide = jax.ShapeDtypeStruct((t, D_FF), BF16)
    row = lambda n: pl.BlockSpec((tm, n), lambda i: (i, 0))
    return _pallas(
        comm, body, name="ffn_bwd_act",
        grid=(t // tm,),
        in_specs=[row(D_MODEL), _full((D_FF, D_MODEL)), row(D_FF), row(D_FF)],
        out_specs=[row(D_FF), row(D_FF)],
        out_shape=[wide, wide],
        args=(dx, wd, a, b))


def norm_bwd_matmul(a1, w1, a2, w2, x, g, dx_in, tm, comm=None):
    t = x.shape[0]
    k1, k2 = a1.shape[1], a2.shape[1]

    def body(a1_ref, w1_ref, a2_ref, w2_ref, x_ref, g_ref, dxin_ref, dx_ref, dg_ref):
        dh = _dot(a1_ref[...], w1_ref[...]) + _dot(a2_ref[...], w2_ref[...])
        xx = x_ref[...]
        r = _rstd(xx)
        dx, dg_rows = _rms_bwd(dh, xx * r, r, g_ref[...])
        dx_ref[...] = dxin_ref[...] + dx

        @pl.when(pl.program_id(0) == 0)
        def _():
            dg_ref[...] = jnp.zeros_like(dg_ref)

        dg_ref[...] += _colsum(dg_rows)

    row = pl.BlockSpec((tm, D_MODEL), lambda i: (i, 0))
    return _pallas(
        comm, body, name="norm_bwd_matmul",
        grid=(t // tm,),
        in_specs=[pl.BlockSpec((tm, k1), lambda i: (i, 0)), _full((k1, D_MODEL)),
                  pl.BlockSpec((tm, k2), lambda i: (i, 0)), _full((k2, D_MODEL)),
                  row, _full((1, D_MODEL)), row],
        out_specs=[row, _full((1, D_MODEL))],
        out_shape=[jax.ShapeDtypeStruct((t, D_MODEL), F32), jax.ShapeDtypeStruct((1, D_MODEL), F32)],
        args=(a1, w1, a2, w2, x, g, dx_in))


def matmul_tn(a, b, scale, tmm, tk):
    t, m = a.shape
    n = b.shape[1]
    nk = t // tk

    def body(a_ref, b_ref, o_ref, acc_ref):
        k = pl.program_id(1)

        @pl.when(k == 0)
        def _():
            acc_ref[...] = jnp.zeros_like(acc_ref)

        acc_ref[...] += _dot_tn(a_ref[...].astype(BF16), b_ref[...].astype(BF16))

        @pl.when(k == nk - 1)
        def _():
            o_ref[...] = (scale * acc_ref[...]).astype(BF16)

    return pl.pallas_call(
        functools.partial(body), name="matmul_tn",
        grid=(m // tmm, nk),
        in_specs=[pl.BlockSpec((tk, tmm), lambda i, k: (k, i)), pl.BlockSpec((tk, n), lambda i, k: (k, 0))],
        out_specs=pl.BlockSpec((tmm, n), lambda i, k: (i, 0)),
        out_shape=jax.ShapeDtypeStruct((m, n), BF16),
        scratch_shapes=[pltpu.VMEM((tmm, n), F32)],
        compiler_params=_params(2),
    )(a, b)


def input_projection(x, g, w_qkv_t, w_z_t, tm):
    t = x.shape[0]

    def body(x_ref, g_ref, wq_ref, wz_ref, qkv_ref, z_ref, h_ref):
        xx = x_ref[...]
        h = ((xx * _rstd(xx)) * g_ref[...]).astype(BF16)
        h_ref[...] = h
        qkv_ref[...] = _dot_nt(h, wq_ref[...])
        z_ref[...] = _dot_nt(h, wz_ref[...])

    row = lambda n: pl.BlockSpec((tm, n), lambda i: (i, 0))
    return pl.pallas_call(
        functools.partial(body), name="input_projection", grid=(t // tm,),
        in_specs=[row(D_MODEL), _full((1, D_MODEL)), _full((D_QKV, D_MODEL)), _full((2 * D_SGU, D_MODEL))],
        out_specs=[row(D_QKV), row(2 * D_SGU), row(D_MODEL)],
        out_shape=[jax.ShapeDtypeStruct((t, D_QKV), F32), jax.ShapeDtypeStruct((t, 2 * D_SGU), F32),
                   jax.ShapeDtypeStruct((t, D_MODEL), BF16)],
        compiler_params=_params(1))(x, g, w_qkv_t, w_z_t)


def _head_tile_spec(tm, rows):
    return pl.BlockSpec((N_HEADS, None, rows, tm), lambda i: (0, i, 0, 0))


def _to_head_tiles(a):
    return a.T.reshape(N_HEADS, HEAD_DIM, a.shape[0])


def _from_head_tiles(a):
    return a.reshape(D_ATTN, a.shape[-1]).T


def _head_mean_matrix(width):
    head = jnp.arange(width) // HEAD_DIM
    return (head[:, None] == head[None, :]).astype(F32) / HEAD_DIM


def qk_prep(qkv, gq_w, gk_w, cos_w, sin_w, mean_q, mean_k, tm):
    t = qkv.shape[0]

    def body(p_ref, gq_ref, gk_ref, cos_ref, sin_ref, mq_ref, mk_ref, q_ref, k_ref, v_ref):
        cos2, sin2 = cos_ref[...], sin_ref[...]
        q = p_ref[:, :D_ATTN]
        k = p_ref[:, D_ATTN:D_ATTN + D_KV]
        qn = q * lax.rsqrt(_dot_f32(q * q, mq_ref[...]) + EPS) * gq_ref[...]
        kn = k * lax.rsqrt(_dot_f32(k * k, mk_ref[...]) + EPS) * gk_ref[...]
        cos8, sin8 = _tile_lanes(cos2, D_ATTN // LANES), _tile_lanes(sin2, D_ATTN // LANES)
        q_rot = (qn * cos8 + _pair_swap(qn) * sin8) * Q_SCALE
        q_ref[...] = _to_head_tiles(q_rot).astype(BF16)
        k_ref[...] = (kn * cos2 + _pair_swap(kn) * sin2).astype(BF16)
        v_ref[...] = p_ref[:, D_ATTN + D_KV:].astype(BF16)

    return pl.pallas_call(
        functools.partial(body), name="qk_prep", grid=(t // tm,),
        in_specs=[pl.BlockSpec((tm, D_QKV), lambda i: (i, 0)), _full((1, D_ATTN)), _full((1, D_KV)),
                  pl.BlockSpec((tm, LANES), lambda i: (i, 0)), pl.BlockSpec((tm, LANES), lambda i: (i, 0)),
                  _full((D_ATTN, D_ATTN)), _full((D_KV, D_KV))],
        out_specs=[_head_tile_spec(tm, HEAD_DIM), pl.BlockSpec((tm, D_KV), lambda i: (i, 0)),
                   pl.BlockSpec((tm, D_KV), lambda i: (i, 0))],
        out_shape=[jax.ShapeDtypeStruct((N_HEADS, t // tm, HEAD_DIM, tm), BF16), jax.ShapeDtypeStruct((t, D_KV), BF16),
                   jax.ShapeDtypeStruct((t, D_KV), BF16)],
        compiler_params=_params(1),
    )(qkv, gq_w, gk_w, cos_w, sin_w, mean_q, mean_k)


def qk_bwd(dq_rot, dk_rot, dv, qkv, gq_w, gk_w, cos_w, sin_w, mean_q, mean_k, tm):
    t = qkv.shape[0]

    def branch(raw, d_rot, gain, mean_mat, cos, sin, scale):
        r = lax.rsqrt(_dot_f32(raw * raw, mean_mat) + EPS)
        n = raw * r
        dy = (d_rot * cos - _pair_swap(d_rot) * sin) * scale
        dn = dy * gain
        return r * (dn - n * _dot_f32(dn * n, mean_mat)), dy * n

    def body(dq_ref, dk_ref, dv_ref, p_ref, gq_ref, gk_ref, cos_ref, sin_ref, mq_ref, mk_ref,
             dp_ref, dgq_ref, dgk_ref):
        cos2, sin2 = cos_ref[...], sin_ref[...]
        cos8, sin8 = _tile_lanes(cos2, D_ATTN // LANES), _tile_lanes(sin2, D_ATTN // LANES)
        dq, dgq = branch(p_ref[:, :D_ATTN], _from_head_tiles(dq_ref[...]), gq_ref[...], mq_ref[...], cos8, sin8,
                         HEAD_DIM ** -0.5)
        dk, dgk = branch(p_ref[:, D_ATTN:D_ATTN + D_KV], dk_ref[...], gk_ref[...], mk_ref[...], cos2, sin2, 1.0)
        dp_ref[...] = jnp.concatenate([dq, dk, dv_ref[...]], axis=-1).astype(BF16)

        @pl.when(pl.program_id(0) == 0)
        def _():
            dgq_ref[...] = jnp.zeros_like(dgq_ref)
            dgk_ref[...] = jnp.zeros_like(dgk_ref)

        dgq_ref[...] += _colsum(dgq)
        dgk_ref[...] += _colsum(dgk)

    return pl.pallas_call(
        functools.partial(body), name="qk_bwd", grid=(t // tm,),
        in_specs=[_head_tile_spec(tm, HEAD_DIM), pl.BlockSpec((tm, D_KV), lambda i: (i, 0)),
                  pl.BlockSpec((tm, D_KV), lambda i: (i, 0)), pl.BlockSpec((tm, D_QKV), lambda i: (i, 0)),
                  _full((1, D_ATTN)), _full((1, D_KV)),
                  pl.BlockSpec((tm, LANES), lambda i: (i, 0)), pl.BlockSpec((tm, LANES), lambda i: (i, 0)),
                  _full((D_ATTN, D_ATTN)), _full((D_KV, D_KV))],
        out_specs=[pl.BlockSpec((tm, D_QKV), lambda i: (i, 0)), _full((1, D_ATTN)), _full((1, D_KV))],
        out_shape=[jax.ShapeDtypeStruct((t, D_QKV), BF16), jax.ShapeDtypeStruct((1, D_ATTN), F32),
                   jax.ShapeDtypeStruct((1, D_KV), F32)],
        compiler_params=_params(1),
    )(dq_rot, dk_rot, dv, qkv, gq_w, gk_w, cos_w, sin_w, mean_q, mean_k)


def attention_fwd(q_t, k, v_t, comm=None):
    _, nq, _, tq = q_t.shape
    _, nk, tk, _ = k.shape

    def body(q_ref, k_ref, v_ref, o_ref, lse_ref, s_scr, p_scr):
        q = q_ref[...]
        s_scr[0] = _dot(k_ref[0], q)
        p_scr[1] = jnp.zeros((tk, tq), BF16)

        def step(j, slot, carry):
            m, l, acc = carry
            s = s_scr[slot]
            pv = _dot(v_ref[jnp.maximum(j - 1, 0)], p_scr[1 - slot])
            s_scr[1 - slot] = _dot(k_ref[jnp.minimum(j + 1, nk - 1)], q)
            m_new = jnp.maximum(m, jnp.max(s, axis=0, keepdims=True))
            p = jnp.exp2(s - m_new)
            alpha = jnp.exp2(m - m_new)
            p_scr[slot] = p.astype(BF16)
            return m_new, alpha * l + jnp.sum(p, axis=0, keepdims=True), alpha * (acc + pv)

        m, l, acc = _loop_pairs(
            nk, step,
            (jnp.full((1, tq), -1e30, F32), jnp.zeros((1, tq), F32), jnp.zeros((HEAD_DIM, tq), F32)))
        acc = acc + _dot(v_ref[nk - 1], p_scr[(nk - 1) % 2])
        o_ref[...] = acc / l
        lse_ref[...] = m + jnp.log2(l)

    return _pallas(
        comm, body, name="attention_fwd", grid=(N_HEADS, nq),
        in_specs=[pl.BlockSpec((None, None, HEAD_DIM, tq), lambda h, i: (h, i, 0, 0)),
                  pl.BlockSpec((None, nk, tk, HEAD_DIM), lambda h, i: (h // KV_GROUP, 0, 0, 0)),
                  pl.BlockSpec((None, nk, HEAD_DIM, tk), lambda h, i: (h // KV_GROUP, 0, 0, 0))],
        out_specs=[pl.BlockSpec((None, None, HEAD_DIM, tq), lambda h, i: (h, i, 0, 0)),
                   pl.BlockSpec((None, None, 1, tq), lambda h, i: (h, i, 0, 0))],
        out_shape=[jax.ShapeDtypeStruct((N_HEADS, nq, HEAD_DIM, tq), F32),
                   jax.ShapeDtypeStruct((N_HEADS, nq, 1, tq), F32)],
        scratch_shapes=[pltpu.VMEM((2, tk, tq), F32), pltpu.VMEM((2, tk, tq), BF16)],
        args=(q_t, k, v_t))


def attention_bwd(q_t, do_t, lse, delta, k, k_t, v, comm=None):
    _, nq, _, tq = q_t.shape
    _, nk, tk, _ = k.shape

    def body(q_ref, do_ref, lse_ref, delta_ref, k_ref, kt_ref, v_ref, dq_ref, dk_ref, dv_ref,
             s_scr, dp_scr, p_scr, ds_scr):
        @pl.when(pl.program_id(1) == 0)
        def _():
            dq_ref[...] = jnp.zeros_like(dq_ref)

        kk, kt, vv = k_ref[...], kt_ref[...], v_ref[...]
        n = KV_GROUP * nq
        s_scr[0] = _dot(kk, q_ref[0, 0])
        dp_scr[0] = _dot(vv, do_ref[0, 0])
        p_scr[1] = jnp.zeros((tk, tq), BF16)
        ds_scr[1] = jnp.zeros((tk, tq), BF16)

        def products(t, slot, dk, dv):
            h, i = t // nq, t % nq
            ds = ds_scr[slot]
            dq_ref[h, i] += _dot(kt, ds)
            return dk + _dot_nt(q_ref[h, i], ds), dv + _dot_nt(do_ref[h, i], p_scr[slot])

        def step(t, slot, carry):
            s, dp = s_scr[slot], dp_scr[slot]
            dk, dv = products(jnp.maximum(t - 1, 0), 1 - slot, *carry)
            nxt = jnp.minimum(t + 1, n - 1)
            s_scr[1 - slot] = _dot(kk, q_ref[nxt // nq, nxt % nq])
            dp_scr[1 - slot] = _dot(vv, do_ref[nxt // nq, nxt % nq])
            h, i = t // nq, t % nq
            p = jnp.exp2(s - lse_ref[h, i])
            p_scr[slot] = p.astype(BF16)
            ds_scr[slot] = (p * (dp - delta_ref[h, i])).astype(BF16)
            return dk, dv

        zero = jnp.zeros((HEAD_DIM, tk), F32)
        dk, dv = products(n - 1, (n - 1) % 2, *_loop_pairs(n, step, (zero, zero)))
        dk_ref[...] = dk * (1.0 / LOG2_E)
        dv_ref[...] = dv

    group = lambda g, j: (g, 0, 0, 0)
    tile = lambda g, j: (g, j, 0, 0)
    return _pallas(
        comm, body, name="attention_bwd", grid=(N_KV_HEADS, nk),
        in_specs=[pl.BlockSpec((KV_GROUP, nq, HEAD_DIM, tq), group),
                  pl.BlockSpec((KV_GROUP, nq, HEAD_DIM, tq), group),
                  pl.BlockSpec((KV_GROUP, nq, 1, tq), group),
                  pl.BlockSpec((KV_GROUP, nq, 1, tq), group),
                  pl.BlockSpec((None, None, tk, HEAD_DIM), tile),
                  pl.BlockSpec((None, None, HEAD_DIM, tk), tile),
                  pl.BlockSpec((None, None, tk, HEAD_DIM), tile)],
        out_specs=[pl.BlockSpec((KV_GROUP, nq, HEAD_DIM, tq), group),
                   pl.BlockSpec((None, None, HEAD_DIM, tk), tile),
                   pl.BlockSpec((None, None, HEAD_DIM, tk), tile)],
        out_shape=[jax.ShapeDtypeStruct((N_HEADS, nq, HEAD_DIM, tq), F32),
                   jax.ShapeDtypeStruct((N_KV_HEADS, nk, HEAD_DIM, tk), F32),
                   jax.ShapeDtypeStruct((N_KV_HEADS, nk, HEAD_DIM, tk), F32)],
        scratch_shapes=[pltpu.VMEM((2, tk, tq), F32), pltpu.VMEM((2, tk, tq), F32),
                        pltpu.VMEM((2, tk, tq), BF16), pltpu.VMEM((2, tk, tq), BF16)],
        args=(q_t, do_t, lse, delta, k, k_t, v))


def _group_select(parts):
    lane_group = lax.broadcasted_iota(jnp.int32, parts[0].shape, 1) // SGU_GROUP_DIM
    out = parts[0]
    for g in range(1, N_SGU_GROUPS):
        out = jnp.where(lane_group == g, parts[g], out)
    return out


def _gate_forward(z, g_sgu, ws_ref, bias):
    gz, th = _gelu(z)
    u, vv = gz[:, :D_SGU], gz[:, D_SGU:]
    rv = _rstd(vv)
    nv = vv * rv
    vn = (nv * g_sgu).astype(BF16)
    fs = []
    for c in range(z.shape[0] // CHUNK):
        vc = vn[c * CHUNK:(c + 1) * CHUNK]
        fs.append(_group_select([_dot(ws_ref[g], vc) for g in range(N_SGU_GROUPS)]) + bias)
    f = jnp.concatenate(fs, axis=0) if len(fs) > 1 else fs[0]
    return th, u, rv, nv, vn, f


def mix_out(z, o, x, g_sgu, g_ao, g_so, ws, bias, w_out, tm):
    t = x.shape[0]

    def body(z_ref, o_ref, x_ref, gs_ref, gao_ref, gso_ref, ws_ref, bias_ref, wout_ref, x2_ref, mixed_ref):
        _, u, _, _, _, f = _gate_forward(z_ref[...], gs_ref[...], ws_ref, bias_ref[...])
        sgu = u * f
        oo = _from_head_tiles(o_ref[...])
        mixed = jnp.concatenate([oo * _rstd(oo) * gao_ref[...], sgu * _rstd(sgu) * gso_ref[...]], axis=-1).astype(BF16)
        mixed_ref[...] = mixed
        x2_ref[...] = x_ref[...] + _dot(mixed, wout_ref[...])

    row = lambda n: pl.BlockSpec((tm, n), lambda i: (i, 0))
    return pl.pallas_call(
        functools.partial(body), name="mix_out", grid=(t // tm,),
        in_specs=[row(2 * D_SGU), _head_tile_spec(tm, HEAD_DIM), row(D_MODEL), _full((1, D_SGU)), _full((1, D_ATTN)),
                  _full((1, D_SGU)),
                  _full((N_SGU_GROUPS, CHUNK, CHUNK)), _full((CHUNK, D_SGU)), _full((D_MODEL, D_MODEL))],
        out_specs=[row(D_MODEL), row(D_MODEL)],
        out_shape=[jax.ShapeDtypeStruct((t, D_MODEL), F32), jax.ShapeDtypeStruct((t, D_MODEL), BF16)],
        compiler_params=_params(1),
    )(z, o, x, g_sgu, g_ao, g_so, ws, bias, w_out)


def mix_bwd(dx2, z, o, g_sgu, g_ao, g_so, ws, ws_t, bias, w_out, group_ind, tm):
    t = dx2.shape[0]
    n_tiles = t // tm

    def body(dx_ref, z_ref, o_ref, gs_ref, gao_ref, gso_ref, ws_ref, wst_ref, bias_ref, wout_ref, ind_ref,
             do_ref, delta_ref, dz_ref, dg_ref, dws_ref, dbs_ref, df_sum):
        step = pl.program_id(0)

        @pl.when(step == 0)
        def _():
            dg_ref[...] = jnp.zeros_like(dg_ref)
            dws_ref[...] = jnp.zeros_like(dws_ref)
            df_sum[...] = jnp.zeros_like(df_sum)

        z = z_ref[...]
        th, u, rv, nv, vn, f = _gate_forward(z, gs_ref[...], ws_ref, bias_ref[...])
        dmixed = _dot_nt(dx_ref[...].astype(BF16), wout_ref[...])
        o_tiles = o_ref[...]
        oo = _from_head_tiles(o_tiles)
        ro = _rstd(oo)
        d_o, dgao = _rms_bwd(dmixed[:, :D_ATTN], oo * ro, ro, gao_ref[...])
        do_tiles = _to_head_tiles(d_o)
        do_ref[...] = do_tiles.astype(BF16)
        delta_ref[...] = jnp.sum(do_tiles * o_tiles, axis=1, keepdims=True)
        sgu = u * f
        rs = _rstd(sgu)
        dsgu, dgso = _rms_bwd(dmixed[:, D_ATTN:], sgu * rs, rs, gso_ref[...])
        du = dsgu * f
        df = dsgu * u
        lane_group = lax.broadcasted_iota(jnp.int32, (CHUNK, D_SGU), 1) // SGU_GROUP_DIM
        dvns = []
        df_acc = jnp.zeros((CHUNK, D_SGU), F32)
        for c in range(tm // CHUNK):
            dfc32 = df[c * CHUNK:(c + 1) * CHUNK]
            dfc = dfc32.astype(BF16)
            vc = vn[c * CHUNK:(c + 1) * CHUNK]
            dvns.append(_group_select([_dot(wst_ref[g], dfc) for g in range(N_SGU_GROUPS)]))
            for g in range(N_SGU_GROUPS):
                dws_ref[g] += _dot_nt(jnp.where(lane_group == g, dfc, jnp.zeros_like(dfc)), vc)
            df_acc = df_acc + dfc32
        df_sum[...] += df_acc
        dvn = jnp.concatenate(dvns, axis=0) if len(dvns) > 1 else dvns[0]
        dvv, dgs = _rms_bwd(dvn, nv, rv, gs_ref[...])
        dz_ref[...] = (jnp.concatenate([du, dvv], axis=-1) * _gelu_grad(z, th)).astype(BF16)
        dg_ref[0:1, :] += _colsum(dgao)
        dg_ref[1:2, :] += _colsum(dgso)
        dg_ref[2:3, :] += _colsum(dgs)

        @pl.when(step == n_tiles - 1)
        def _():
            dbs_ref[...] = _dot_f32(df_sum[...], ind_ref[...])

    row = lambda n: pl.BlockSpec((tm, n), lambda i: (i, 0))
    return pl.pallas_call(
        functools.partial(body), name="mix_bwd", grid=(n_tiles,),
        in_specs=[row(D_MODEL), row(2 * D_SGU), _head_tile_spec(tm, HEAD_DIM), _full((1, D_SGU)), _full((1, D_ATTN)),
                  _full((1, D_SGU)),
                  _full((N_SGU_GROUPS, CHUNK, CHUNK)), _full((N_SGU_GROUPS, CHUNK, CHUNK)), _full((CHUNK, D_SGU)),
                  _full((D_MODEL, D_MODEL)), _full((D_SGU, LANES))],
        out_specs=[_head_tile_spec(tm, HEAD_DIM), _head_tile_spec(tm, 1), row(2 * D_SGU), _full((8, D_SGU)),
                   _full((N_SGU_GROUPS, CHUNK, CHUNK)), _full((CHUNK, LANES))],
        out_shape=[jax.ShapeDtypeStruct((N_HEADS, n_tiles, HEAD_DIM, tm), BF16),
                   jax.ShapeDtypeStruct((N_HEADS, n_tiles, 1, tm), F32), jax.ShapeDtypeStruct((t, 2 * D_SGU), BF16),
                   jax.ShapeDtypeStruct((8, D_SGU), F32),
                   jax.ShapeDtypeStruct((N_SGU_GROUPS, CHUNK, CHUNK), F32),
                   jax.ShapeDtypeStruct((CHUNK, LANES), F32)],
        scratch_shapes=[pltpu.VMEM((CHUNK, D_SGU), F32)],
        compiler_params=_params(1),
    )(dx2, z, o, g_sgu, g_ao, g_so, ws, ws_t, bias, w_out, group_ind)


def loss_bwd(x, g, target, tm):
    t = x.shape[0]

    def body(x_ref, g_ref, t_ref, loss_ref, dx_ref, dg_ref):
        @pl.when(pl.program_id(0) == 0)
        def _():
            loss_ref[...] = jnp.zeros_like(loss_ref)
            dg_ref[...] = jnp.zeros_like(dg_ref)

        xx = x_ref[...]
        r = _rstd(xx)
        n = xx * r
        err = n * g_ref[...] - t_ref[...]
        per_token = jnp.mean(err * err, axis=-1, keepdims=True)
        loss_ref[...] += 0.5 * jnp.sum(per_token, axis=0, keepdims=True)
        dx, dg_rows = _rms_bwd(err * (1.0 / D_MODEL), n, r, g_ref[...])
        dx_ref[...] = dx
        dg_ref[...] += _colsum(dg_rows)

    row = pl.BlockSpec((tm, D_MODEL), lambda i: (i, 0))
    return pl.pallas_call(
        functools.partial(body), name="loss_bwd", grid=(t // tm,),
        in_specs=[row, _full((1, D_MODEL)), row],
        out_specs=[_full((1, LANES)), row, _full((1, D_MODEL))],
        out_shape=[jax.ShapeDtypeStruct((1, LANES), F32), jax.ShapeDtypeStruct((t, D_MODEL), F32),
                   jax.ShapeDtypeStruct((1, D_MODEL), F32)],
        compiler_params=_params(1),
    )(x, g, target)


def _rope_tables(t):
    rows = t // GRID_W
    row_idx = jnp.repeat(jnp.arange(rows, dtype=F32), GRID_W)
    col_idx = jnp.tile(jnp.arange(GRID_W, dtype=F32), rows)
    axis_dim = HEAD_DIM // 2
    inv = 1.0 / (ROPE_THETA ** (jnp.arange(0, axis_dim, 2, dtype=F32) / axis_dim))
    ang = jnp.concatenate([row_idx[:, None] * inv, col_idx[:, None] * inv], axis=-1)
    cos = jnp.repeat(jnp.cos(ang), 2, axis=-1)
    sin = jnp.repeat(jnp.sin(ang), 2, axis=-1) * jnp.tile(jnp.array([-1.0, 1.0], F32), HEAD_DIM // 2)
    return jnp.tile(cos, (1, LANES // HEAD_DIM)), jnp.tile(sin, (1, LANES // HEAD_DIM))


def _heads_to_tiles_t(a, n_heads, tile):
    t = a.shape[0]
    return a.reshape(t // tile, tile, n_heads, HEAD_DIM).transpose(2, 0, 3, 1)


def _heads_to_tiles(a, n_heads, tile):
    t = a.shape[0]
    return a.reshape(t // tile, tile, n_heads, HEAD_DIM).transpose(2, 0, 1, 3)


def _tiles_t_to_heads(a):
    h, n, _, tile = a.shape
    return a.transpose(1, 3, 0, 2).reshape(n * tile, h * HEAD_DIM)


def kernel(x, g_ffn1, w1_gate, w1_up, w1_down, g_mix, w_in, g_q, g_k, g_sgu, w_s, b_s, g_attn_out, g_sgu_out, w_out, g_ffn2, w2_gate, w2_up, w2_down, g_final, loss_target, m_g_ffn1, m_w1_gate, m_w1_up, m_w1_down, m_g_mix, m_w_in, m_g_q, m_g_k, m_g_sgu, m_w_s, m_b_s, m_g_attn_out, m_g_sgu_out, m_w_out, m_g_ffn2, m_w2_gate, m_w2_up, m_w2_down, m_g_final, v_g_ffn1, v_w1_gate, v_w1_up, v_w1_down, v_g_mix, v_w_in, v_g_q, v_g_k, v_g_sgu, v_w_s, v_b_s, v_g_attn_out, v_g_sgu_out, v_w_out, v_g_ffn2, v_w2_gate, v_w2_up, v_w2_down, v_g_final):
    weights = dict(g_ffn1=g_ffn1, w1_gate=w1_gate, w1_up=w1_up, w1_down=w1_down, g_mix=g_mix, w_in=w_in, g_q=g_q,
                   g_k=g_k, g_sgu=g_sgu, w_s=w_s, b_s=b_s, g_attn_out=g_attn_out, g_sgu_out=g_sgu_out, w_out=w_out,
                   g_ffn2=g_ffn2, w2_gate=w2_gate, w2_up=w2_up, w2_down=w2_down, g_final=g_final)
    m_in = dict(g_ffn1=m_g_ffn1, w1_gate=m_w1_gate, w1_up=m_w1_up, w1_down=m_w1_down, g_mix=m_g_mix, w_in=m_w_in,
                g_q=m_g_q, g_k=m_g_k, g_sgu=m_g_sgu, w_s=m_w_s, b_s=m_b_s, g_attn_out=m_g_attn_out,
                g_sgu_out=m_g_sgu_out, w_out=m_w_out, g_ffn2=m_g_ffn2, w2_gate=m_w2_gate, w2_up=m_w2_up,
                w2_down=m_w2_down, g_final=m_g_final)
    v_in = dict(g_ffn1=v_g_ffn1, w1_gate=v_w1_gate, w1_up=v_w1_up, w1_down=v_w1_down, g_mix=v_g_mix, w_in=v_w_in,
                g_q=v_g_q, g_k=v_g_k, g_sgu=v_g_sgu, w_s=v_w_s, b_s=v_b_s, g_attn_out=v_g_attn_out,
                g_sgu_out=v_g_sgu_out, w_out=v_w_out, g_ffn2=v_g_ffn2, w2_gate=v_w2_gate, w2_up=v_w2_up,
                w2_down=v_w2_down, g_final=v_g_final)
    names = list(weights)

    t = x.shape[1]
    x0 = x[0]
    target = loss_target[0]
    tm = min(256, t)
    tm_ff = min(256, t)
    tn_ff = 256
    tq = min(512, t)
    tk = min(256, t)
    tk_w = min(512, t)

    def shard_rows(name):
        w = weights[name][0]
        return (w.T if name in TRANSPOSED else w).astype(BF16)

    rows_of = dict(SHARD_ROWS)
    full = {}

    def gather_of(group):
        return gather_exchange(jnp.concatenate([shard_rows(n) for n in group], axis=0), [rows_of[n] for n in group])

    def take(group, gathered):
        for n, g in zip(group, gathered):
            full[n] = g.reshape(N_DEV * rows_of[n], D_MODEL)

    first, second, third = ("w1_gate", "w1_up"), ("w1_down", "w_in", "w_out"), ("w2_gate", "w2_up", "w2_down")
    take(first, run_exchange(gather_of(first), "gather_first"))

    (h1, a1, b1, act1), gathered = ffn_up(x0, g_ffn1, full["w1_gate"], full["w1_up"], tm_ff, tn_ff, gather_of(second))
    take(second, gathered)
    w_in_t = full["w_in"]
    w_qkv_t, w_z_t = w_in_t[:D_QKV], w_in_t[D_QKV:]
    x1 = ffn_down(act1, full["w1_down"], x0, tm)

    qkv, z, h2 = input_projection(x1, g_mix, w_qkv_t, w_z_t, tm)
    cos_w, sin_w = _rope_tables(t)
    gq_w = jnp.tile(g_q, (1, N_HEADS))
    gk_w = jnp.tile(g_k, (1, N_KV_HEADS))
    mean_q, mean_k = _head_mean_matrix(D_ATTN), _head_mean_matrix(D_KV)
    q_t, k_rot, v_b = qk_prep(qkv, gq_w, gk_w, cos_w, sin_w, mean_q, mean_k, tq)
    k_tiles = _heads_to_tiles(k_rot, N_KV_HEADS, tk)
    kt_tiles = _heads_to_tiles_t(k_rot, N_KV_HEADS, tk)
    v_tiles = _heads_to_tiles(v_b, N_KV_HEADS, tk)
    vt_tiles = _heads_to_tiles_t(v_b, N_KV_HEADS, tk)
    (o_t, lse), gathered = attention_fwd(q_t, k_tiles, vt_tiles, gather_of(third))
    take(third, gathered)

    ws_b = w_s[0].astype(BF16)
    ws_tb = jnp.swapaxes(w_s[0], 1, 2).astype(BF16)
    bias = jnp.repeat(b_s[0].T, SGU_GROUP_DIM, axis=1)
    x2, mixed = mix_out(z, o_t, x1, g_sgu, g_attn_out, g_sgu_out, ws_b, bias, full["w_out"], tq)

    (h3, a2, b2, act2), _ = ffn_up(x2, g_ffn2, full["w2_gate"], full["w2_up"], tm_ff, tn_ff)
    x3 = ffn_down(act2, full["w2_down"], x2, tm)

    loss_part, dx3, dg_final = loss_bwd(x3, g_final, target, tm)
    loss = lax.psum(loss_part[0, 0], MESH_AXES)

    tmm = D_FF // 2
    (da2, db2), _ = ffn_bwd_act(dx3, full["w2_down"], a2, b2, tm_ff, tn_ff)
    (dx2, dg_ffn2), _ = norm_bwd_matmul(da2, full["w2_gate"], db2, full["w2_up"], x2, g_ffn2, dx3, tm)
    dwg2, dwu2 = matmul_tn(da2, h3, 1.0, tmm, tk_w), matmul_tn(db2, h3, 1.0, tmm, tk_w)
    dwd2 = matmul_tn(act2, dx3, 0.5, tmm, tk_w)

    group_ind = (jnp.arange(D_SGU)[:, None] // SGU_GROUP_DIM == jnp.arange(LANES)[None, :]).astype(F32)
    do_t, delta, dz, dg_mixrow, dws, dbs = mix_bwd(dx2, z, o_t, g_sgu, g_attn_out, g_sgu_out, ws_b, ws_tb, bias,
                                                   full["w_out"], group_ind, tq)
    dw_out = matmul_tn(mixed, dx2, 1.0, D_MODEL // 2, tk_w)

    group_a = ("w2_gate", "w2_up", "w2_down", "w_out")
    (dq_t, dk_t, dv_t), (parts_a,) = attention_bwd(q_t, do_t, lse, delta, k_tiles, kt_tiles, v_tiles,
                                                   scatter_exchange([dwg2, dwu2, dwd2, dw_out]))
    dqkv, dgq_w, dgk_w = qk_bwd(dq_t, _tiles_t_to_heads(dk_t), _tiles_t_to_heads(dv_t), qkv,
                                gq_w, gk_w, cos_w, sin_w, mean_q, mean_k, tq)
    (dx1, dg_mix), _ = norm_bwd_matmul(dqkv, w_qkv_t, dz, w_z_t, x1, g_mix, dx2, tm)
    dw_in = jnp.concatenate([matmul_tn(dqkv, h2, 1.0, D_QKV // 2, tk_w), matmul_tn(dz, h2, 1.0, D_SGU, tk_w)], axis=0)

    dwd1 = matmul_tn(act1, dx1, 0.5, tmm, tk_w)
    group_b = ("w_in", "w1_down")
    (da1, db1), (parts_b,) = ffn_bwd_act(dx1, full["w1_down"], a1, b1, tm_ff, tn_ff, scatter_exchange([dw_in, dwd1]))
    dwg1, dwu1 = matmul_tn(da1, h1, 1.0, tmm, tk_w), matmul_tn(db1, h1, 1.0, tmm, tk_w)
    group_c = ("w1_gate", "w1_up")
    (dx0, dg_ffn1), (parts_c,) = norm_bwd_matmul(da1, full["w1_gate"], db1, full["w1_up"], x0, g_ffn1, dx1, tm,
                                                 scatter_exchange([dwg1, dwu1]))

    small_grads = dict(
        g_ffn1=dg_ffn1, g_mix=dg_mix, g_ffn2=dg_ffn2, g_final=dg_final,
        g_q=dgq_w.reshape(N_HEADS, HEAD_DIM).sum(0)[None], g_k=dgk_w.reshape(N_KV_HEADS, HEAD_DIM).sum(0)[None],
        g_attn_out=dg_mixrow[0:1], g_sgu_out=dg_mixrow[1:2], g_sgu=dg_mixrow[2:3],
        w_s=dws[None], b_s=dbs[:, :N_SGU_GROUPS].T[None])
    small_names = [n for n in names if n not in rows_of]
    pieces = []
    for n in small_names:
        flat = small_grads[n].reshape(-1)
        flat = jnp.pad(flat, (0, (-flat.shape[0]) % (8 * LANES)))
        pieces.append(flat.reshape(-1, LANES))
    small_rows = [p.shape[0] for p in pieces]
    small_pack = jnp.concatenate(pieces, axis=0)
    (small_parts,) = run_exchange(gather_exchange(small_pack, [small_pack.shape[0]]), "gather_small_grads")
    small_sum = sum_parts(small_parts, small_pack.shape[0])

    grads = {}
    for group, parts in ((group_a, parts_a), (group_b, parts_b), (group_c, parts_c)):
        summed = sum_parts(parts, 32)
        off = 0
        for n in group:
            gsh = summed[off:off + rows_of[n]]
            grads[n] = (gsh.T if n in TRANSPOSED else gsh)[None]
            off += rows_of[n]
    off = 0
    for n, r in zip(small_names, small_rows):
        grads[n] = small_sum[off:off + r].reshape(-1)[:weights[n].size].reshape(weights[n].shape)
        off += r

    delta_w, new_m, new_v = {}, {}, {}
    for n in names:
        shape = weights[n].shape
        as2d = (lambda a: a.reshape(-1, shape[-1]))
        d, m2, v2 = adamw(as2d(weights[n]), as2d(grads[n]), as2d(m_in[n]), as2d(v_in[n]))
        delta_w[n], new_m[n], new_v[n] = d.reshape(shape), m2.reshape(shape), v2.reshape(shape)

    return (loss, dx0[None], *[grads[n] for n in names], *[delta_w[n] for n in names],
            *[new_m[n] for n in names], *[new_v[n] for n in names])
```

```python
import functools
import math

import jax
import jax.numpy as jnp
from jax import lax
from jax.experimental import pallas as pl
from jax.experimental.pallas import tpu as pltpu

F32 = jnp.float32
BF16 = jnp.bfloat16

D_MODEL = 1024
D_FF = 2816
N_HEADS = 8
HEAD_DIM = 64
N_KV_HEADS = 2
KV_GROUP = N_HEADS // N_KV_HEADS
D_ATTN = N_HEADS * HEAD_DIM
D_KV = N_KV_HEADS * HEAD_DIM
D_QKV = D_ATTN + 2 * D_KV
N_SGU_GROUPS = 8
SGU_GROUP_DIM = 64
D_SGU = N_SGU_GROUPS * SGU_GROUP_DIM
CHUNK = 128
GRID_W = 64
ROPE_THETA = 10000.0
EPS = 1e-6
N_DEV = 8
LANES = 128

ONES_ROWS = 16
LOG2_E = math.log2(math.e)
Q_SCALE = HEAD_DIM ** -0.5 * LOG2_E

ADAM_LR = 0.001
ADAM_B1 = 0.9
ADAM_B2 = 0.999
ADAM_EPS = 1e-08
ADAM_WD = 0.01
ADAM_STEP = 10

MESH_AXES = ("x", "y", "c")
MESH_IDS = pl.DeviceIdType.MESH

VMEM_LIMIT = 56 * 1024 * 1024

SHARD_ROWS = (("w1_gate", D_FF // N_DEV), ("w1_up", D_FF // N_DEV), ("w1_down", D_FF // N_DEV),
              ("w_in", (D_QKV + 2 * D_SGU) // N_DEV), ("w_out", D_MODEL // N_DEV),
              ("w2_gate", D_FF // N_DEV), ("w2_up", D_FF // N_DEV), ("w2_down", D_FF // N_DEV))
PACK_ROWS = sum(r for _, r in SHARD_ROWS)
TRANSPOSED = ("w1_gate", "w1_up", "w_in", "w2_gate", "w2_up")


def _params(n_grid):
    return pltpu.CompilerParams(dimension_semantics=("arbitrary",) * n_grid, vmem_limit_bytes=VMEM_LIMIT)


def _dot(a, b):
    return jnp.dot(a, b, preferred_element_type=F32)


def _dot_nt(a, b):
    return lax.dot_general(a, b, (((1,), (1,)), ((), ())), preferred_element_type=F32)


def _dot_tn(a, b):
    return lax.dot_general(a, b, (((0,), (0,)), ((), ())), preferred_element_type=F32)


def _dot_f32(a, b):
    return jnp.dot(a, b, preferred_element_type=F32, precision=lax.Precision.HIGHEST)


def _dot_split(a, b):
    hi = a.astype(BF16)
    lo = (a - hi.astype(F32)).astype(BF16)
    return _dot(hi, b) + _dot(lo, b)


def _rstd(x):
    return lax.rsqrt(jnp.mean(x * x, axis=-1, keepdims=True) + EPS)


def _rms_bwd(dy, n, r, g):
    dn = dy * g
    return r * (dn - n * jnp.mean(dn * n, axis=-1, keepdims=True)), dy * n


def _colsum(a):
    return jnp.sum(a, axis=0, keepdims=True)


_GELU_C = math.sqrt(2.0 / math.pi)


def _gelu(x):
    t = jnp.tanh(_GELU_C * (x + 0.044715 * (x * x * x)))
    return x * (0.5 * (1.0 + t)), t


def _gelu_grad(x, t):
    return 0.5 * (1.0 + t) + 0.5 * x * (1.0 - t * t) * (_GELU_C * (1.0 + 3 * 0.044715 * x * x))


def _pair_swap(a):
    w = a.shape[-1]
    lane = lax.broadcasted_iota(jnp.int32, a.shape, a.ndim - 1)
    return jnp.where(lane % 2 == 0, pltpu.roll(a, w - 1, a.ndim - 1), pltpu.roll(a, 1, a.ndim - 1))


def _tile_lanes(a, reps):
    return jnp.concatenate([a] * reps, axis=-1) if reps > 1 else a


def _loop_pairs(n, step, carry):
    assert n % 2 == 0, n

    def pair(jj, c):
        return step(2 * jj + 1, 1, step(2 * jj, 0, c))

    return lax.fori_loop(0, n // 2, pair, carry)


def _full(shape):
    nd = len(shape)
    return pl.BlockSpec(shape, lambda *_: (0,) * nd)


def _mesh_pos():
    return lax.axis_index("x"), lax.axis_index("y"), lax.axis_index("c")


def _peer(pos, d):
    x, y, c = pos
    px = 1 - x if d & 4 else x
    py = 1 - y if d & 2 else y
    pc = 1 - c if d & 1 else c
    return (px, py, pc), 4 * px + 2 * py + pc


class _Exchange:
    def __init__(self, operands, out_shape, n_local, plan):
        self.operands = list(operands)
        self.out_shape = list(out_shape)
        self.sem_shapes = [pltpu.SemaphoreType.DMA((N_DEV - 1,)), pltpu.SemaphoreType.DMA((N_DEV - 1,)),
                           pltpu.SemaphoreType.DMA((n_local,))]
        self._plan = plan

    def _copies(self, in_refs, out_refs):
        pos = _mesh_pos()
        return pos, self._plan(4 * pos[0] + 2 * pos[1] + pos[2], in_refs, out_refs)

    def start(self, in_refs, out_refs, sems):
        send_sems, recv_sems, local_sems = sems
        pos, (local, remote, _) = self._copies(in_refs, out_refs)
        for k, (src, dst) in enumerate(local):
            pltpu.make_async_copy(src, dst, local_sems.at[k]).start()
        for d in range(1, N_DEV):
            peer, peer_lin = _peer(pos, d)
            for src, dst in remote(peer_lin):
                pltpu.make_async_remote_copy(src_ref=src, dst_ref=dst, send_sem=send_sems.at[d - 1],
                                             recv_sem=recv_sems.at[d - 1], device_id=peer,
                                             device_id_type=MESH_IDS).start()

    def wait(self, in_refs, out_refs, sems):
        send_sems, recv_sems, local_sems = sems
        pos, (local, _, whole) = self._copies(in_refs, out_refs)
        for d in range(1, N_DEV):
            peer, peer_lin = _peer(pos, d)
            ref = whole(peer_lin)
            everything = pltpu.make_async_remote_copy(src_ref=ref, dst_ref=ref, send_sem=send_sems.at[d - 1],
                                                      recv_sem=recv_sems.at[d - 1], device_id=peer,
                                                      device_id_type=MESH_IDS)
            everything.wait_send()
            everything.wait_recv()
        for k, (src, dst) in enumerate(local):
            pltpu.make_async_copy(src, dst, local_sems.at[k]).wait()


def _offsets(rows):
    offs, o = [], 0
    for r in rows:
        offs.append(o)
        o += r
    return offs


def gather_exchange(src, rows):
    offs = _offsets(rows)

    def plan(me, in_refs, out_refs):
        pieces = [(in_refs[0].at[pl.ds(o, r)], out.at[me]) for o, r, out in zip(offs, rows, out_refs)]
        return pieces, (lambda peer_lin: pieces), (lambda peer_lin: in_refs[0])

    return _Exchange([src], [jax.ShapeDtypeStruct((N_DEV, r) + src.shape[1:], src.dtype) for r in rows],
                     len(rows), plan)


def scatter_exchange(grads):
    rows = [g.shape[0] // N_DEV for g in grads]
    offs = _offsets(rows)

    def plan(me, in_refs, out_refs):
        parts = out_refs[0]

        def slabs(owner):
            return [(g.at[pl.ds(pl.multiple_of(owner * r, 16), r)], parts.at[me, pl.ds(o, r)])
                    for g, o, r in zip(in_refs, offs, rows)]

        return slabs(me), slabs, (lambda peer_lin: parts.at[peer_lin])

    shape = jax.ShapeDtypeStruct((N_DEV, sum(rows)) + grads[0].shape[1:], grads[0].dtype)
    return _Exchange(grads, [shape], len(rows), plan)


def gather_two_level(src, rows, name):
    offs = _offsets(rows)
    n_p = len(rows)

    def body(src_ref, *refs):
        outs, (send_sems, recv_sems, local_sems) = refs[:n_p], refs[n_p:]
        x, y, c = _mesh_pos()
        me, sibling = (x, y, c), (x, y, 1 - c)
        chips = [(1 - x, y), (x, 1 - y), (1 - x, 1 - y)]

        def slab(w, dev):
            return outs[w].at[4 * dev[0] + 2 * dev[1] + dev[2]]

        def copy(w, k, block, to, from_src=False):
            return pltpu.make_async_remote_copy(
                src_ref=src_ref.at[pl.ds(offs[w], rows[w])] if from_src else slab(w, block), dst_ref=slab(w, block),
                send_sem=send_sems.at[w * 7 + k], recv_sem=recv_sems.at[w * 7 + k],
                device_id=to, device_id_type=MESH_IDS)

        mine = [pltpu.make_async_copy(src_ref.at[pl.ds(offs[w], rows[w])], slab(w, me), local_sems.at[w])
                for w in range(n_p)]
        for cp in mine:
            cp.start()
        first = []
        for w in range(n_p):
            first.append(copy(w, 0, me, sibling, True))
            first += [copy(w, 1 + j, me, (*chip, c), True) for j, chip in enumerate(chips)]
        for cp in first:
            cp.start()
        passed = []
        for j, chip in enumerate(chips):
            for w in range(n_p):
                copy(w, 1 + j, (*chip, c), me).wait_recv()
                cp = copy(w, 4 + j, (*chip, c), sibling)
                cp.start()
                passed.append(cp)
        for w in range(n_p):
            copy(w, 0, sibling, me).wait_recv()
            for j, chip in enumerate(chips):
                copy(w, 4 + j, (*chip, 1 - c), me).wait_recv()
        for cp in first + passed:
            cp.wait_send()
        for cp in mine:
            cp.wait()

    any_spec = pl.BlockSpec(memory_space=pl.ANY)
    return pl.pallas_call(
        functools.partial(body), name=name,
        out_shape=[jax.ShapeDtypeStruct((N_DEV, r) + src.shape[1:], src.dtype) for r in rows],
        in_specs=[any_spec], out_specs=[any_spec] * n_p,
        scratch_shapes=[pltpu.SemaphoreType.DMA((7 * n_p,)), pltpu.SemaphoreType.DMA((7 * n_p,)),
                        pltpu.SemaphoreType.DMA((n_p,))],
        compiler_params=pltpu.CompilerParams(has_side_effects=True),
    )(src)


def run_exchange(ex, name):
    n_in, n_out = len(ex.operands), len(ex.out_shape)

    def body(*refs):
        parts = refs[:n_in], refs[n_in:n_in + n_out], refs[n_in + n_out:]
        ex.start(*parts)
        ex.wait(*parts)

    any_spec = pl.BlockSpec(memory_space=pl.ANY)
    return pl.pallas_call(
        functools.partial(body), name=name, out_shape=ex.out_shape,
        in_specs=[any_spec] * n_in, out_specs=[any_spec] * n_out, scratch_shapes=ex.sem_shapes,
        compiler_params=pltpu.CompilerParams(has_side_effects=True),
    )(*ex.operands)


def _pallas(comm, body, *, name, grid, in_specs, out_specs, out_shape, args, scratch_shapes=()):
    params = _params(len(grid))
    if comm is None:
        res = pl.pallas_call(functools.partial(body), name=name, grid=grid, in_specs=list(in_specs),
                             out_specs=list(out_specs), out_shape=list(out_shape),
                             scratch_shapes=list(scratch_shapes), compiler_params=params)(*args)
        return list(res), []
    n_in, n_out, n_scr = len(in_specs), len(out_specs), len(scratch_shapes)
    c_in, c_out = len(comm.operands), len(comm.out_shape)

    def edge(last):
        conds = [pl.program_id(a) == (g - 1 if last else 0) for a, g in enumerate(grid)]
        return functools.reduce(jnp.logical_and, conds)

    def wrapped(*refs):
        refs = list(refs)
        ins, refs = refs[:n_in], refs[n_in:]
        cins, refs = refs[:c_in], refs[c_in:]
        outs, refs = refs[:n_out], refs[n_out:]
        couts, refs = refs[:c_out], refs[c_out:]
        scr, sems = refs[:n_scr], refs[n_scr:]

        @pl.when(edge(False))
        def _():
            comm.start(cins, couts, sems)

        body(*ins, *outs, *scr)

        @pl.when(edge(True))
        def _():
            comm.wait(cins, couts, sems)

    any_spec = pl.BlockSpec(memory_space=pl.ANY)
    res = pl.pallas_call(
        wrapped, name=name, grid=grid,
        in_specs=list(in_specs) + [any_spec] * c_in, out_specs=list(out_specs) + [any_spec] * c_out,
        out_shape=list(out_shape) + comm.out_shape, scratch_shapes=list(scratch_shapes) + comm.sem_shapes,
        compiler_params=pltpu.CompilerParams(dimension_semantics=("arbitrary",) * len(grid),
                                             vmem_limit_bytes=VMEM_LIMIT, has_side_effects=True),
    )(*args, *comm.operands)
    return res[:n_out], res[n_out:]


def sum_parts(parts, block_rows):
    n, rows, cols = parts.shape

    def body(p_ref, o_ref):
        acc = p_ref[0].astype(F32)
        for s in range(1, n):
            acc = acc + p_ref[s].astype(F32)
        o_ref[...] = acc

    return pl.pallas_call(
        functools.partial(body), name="sum_parts",
        grid=(rows // block_rows,),
        in_specs=[pl.BlockSpec((n, block_rows, cols), lambda i: (0, i, 0))],
        out_specs=pl.BlockSpec((block_rows, cols), lambda i: (i, 0)),
        out_shape=jax.ShapeDtypeStruct((rows, cols), F32),
        compiler_params=_params(1),
    )(parts)


def adamw(w, g, m, v):
    def body(w_ref, g_ref, m_ref, v_ref, d_ref, m_out, v_out):
        gg = g_ref[...]
        m2 = ADAM_B1 * m_ref[...] + (1.0 - ADAM_B1) * gg
        v2 = ADAM_B2 * v_ref[...] + (1.0 - ADAM_B2) * (gg * gg)
        m_hat = m2 / (1.0 - ADAM_B1 ** ADAM_STEP)
        v_hat = v2 / (1.0 - ADAM_B2 ** ADAM_STEP)
        d_ref[...] = -ADAM_LR * (m_hat / (jnp.sqrt(v_hat) + ADAM_EPS) + ADAM_WD * w_ref[...])
        m_out[...] = m2
        v_out[...] = v2

    spec = _full(w.shape)
    shape = jax.ShapeDtypeStruct(w.shape, F32)
    return pl.pallas_call(
        functools.partial(body), name="adamw",
        in_specs=[spec] * 4, out_specs=[spec] * 3, out_shape=[shape] * 3,
        compiler_params=pltpu.CompilerParams(vmem_limit_bytes=VMEM_LIMIT),
    )(w, g, m, v)


def ffn_up(x, g, wg_t, wu_t, tm, tn, comm=None):
    t = x.shape[0]

    def body(x_ref, g_ref, wg_ref, wu_ref, h_ref, a_ref, b_ref, act_ref):
        xx = x_ref[...]
        h = ((xx * _rstd(xx)) * g_ref[...]).astype(BF16)
        h_ref[...] = h
        for c in range(D_FF // tn):
            cols = slice(c * tn, (c + 1) * tn)
            a = _dot_nt(h, wg_ref[cols, :])
            b = _dot_nt(h, wu_ref[cols, :])
            a_ref[:, cols] = a.astype(BF16)
            b_ref[:, cols] = b.astype(BF16)
            act_ref[:, cols] = (a * jax.nn.sigmoid(a) * b).astype(BF16)

    wide = jax.ShapeDtypeStruct((t, D_FF), BF16)
    row = lambda n: pl.BlockSpec((tm, n), lambda i: (i, 0))
    return _pallas(
        comm, body, name="ffn_up",
        grid=(t // tm,),
        in_specs=[row(D_MODEL), _full((1, D_MODEL)), _full((D_FF, D_MODEL)), _full((D_FF, D_MODEL))],
        out_specs=[row(D_MODEL), row(D_FF), row(D_FF), row(D_FF)],
        out_shape=[jax.ShapeDtypeStruct((t, D_MODEL), BF16), wide, wide, wide],
        args=(x, g, wg_t, wu_t))


def ffn_down(act, wd, x, tm):
    t = x.shape[0]

    def body(act_ref, wd_ref, x_ref, o_ref):
        o_ref[...] = x_ref[...] + 0.5 * _dot(act_ref[...], wd_ref[...])

    return pl.pallas_call(
        functools.partial(body), name="ffn_down",
        grid=(t // tm,),
        in_specs=[pl.BlockSpec((tm, D_FF), lambda i: (i, 0)), _full((D_FF, D_MODEL)),
                  pl.BlockSpec((tm, D_MODEL), lambda i: (i, 0))],
        out_specs=pl.BlockSpec((tm, D_MODEL), lambda i: (i, 0)),
        out_shape=jax.ShapeDtypeStruct((t, D_MODEL), F32),
        compiler_params=_params(1),
    )(act, wd, x)


def ffn_bwd_act(dx, wd, a, b, tm, tn, comm=None):
    t = dx.shape[0]

    def body(dx_ref, wd_ref, a_ref, b_ref, da_ref, db_ref):
        dxb = (0.5 * dx_ref[...]).astype(BF16)
        for c in range(D_FF // tn):
            cols = slice(c * tn, (c + 1) * tn)
            dact = _dot_nt(dxb, wd_ref[cols, :])
            aa = a_ref[:, cols].astype(F32)
            sig = 0.5 * jnp.tanh(0.5 * aa) + 0.5
            silu = aa * sig
            da_ref[:, cols] = (dact * b_ref[:, cols].astype(F32) * (sig + silu * (1.0 - sig))).astype(BF16)
            db_ref[:, cols] = (dact * silu).astype(BF16)

    wide = jax.ShapeDtypeStruct((t, D_FF), BF16)
    row = lambda n: pl.BlockSpec((tm, n), lambda i: (i, 0))
    return _pallas(
        comm, body, name="ffn_bwd_act",
        grid=(t // tm,),
        in_specs=[row(D_MODEL), _full((D_FF, D_MODEL)), row(D_FF), row(D_FF)],
        out_specs=[row(D_FF), row(D_FF)],
        out_shape=[wide, wide],
        args=(dx, wd, a, b))


def norm_bwd_matmul(a1, w1, a2, w2, x, g, dx_in, tm, comm=None):
    t = x.shape[0]
    k1, k2 = a1.shape[1], a2.shape[1]

    def body(a1_ref, w1_ref, a2_ref, w2_ref, x_ref, g_ref, dxin_ref, dx_ref, dg_ref):
        dh = _dot(a1_ref[...], w1_ref[...]) + _dot(a2_ref[...], w2_ref[...])
        xx = x_ref[...]
        r = _rstd(xx)
        dx, dg_rows = _rms_bwd(dh, xx * r, r, g_ref[...])
        dx_ref[...] = dxin_ref[...] + dx

        @pl.when(pl.program_id(0) == 0)
        def _():
            dg_ref[...] = jnp.zeros_like(dg_ref)

        dg_ref[...] += _colsum(dg_rows)

    row = pl.BlockSpec((tm, D_MODEL), lambda i: (i, 0))
    return _pallas(
        comm, body, name="norm_bwd_matmul",
        grid=(t // tm,),
        in_specs=[pl.BlockSpec((tm, k1), lambda i: (i, 0)), _full((k1, D_MODEL)),
                  pl.BlockSpec((tm, k2), lambda i: (i, 0)), _full((k2, D_MODEL)),
                  row, _full((1, D_MODEL)), row],
        out_specs=[row, _full((1, D_MODEL))],
        out_shape=[jax.ShapeDtypeStruct((t, D_MODEL), F32), jax.ShapeDtypeStruct((1, D_MODEL), F32)],
        args=(a1, w1, a2, w2, x, g, dx_in))


def matmul_tn(a, b, scale, tmm, tk):
    t, m = a.shape
    n = b.shape[1]
    nk = t // tk

    def body(a_ref, b_ref, o_ref, acc_ref):
        k = pl.program_id(1)

        @pl.when(k == 0)
        def _():
            acc_ref[...] = jnp.zeros_like(acc_ref)

        acc_ref[...] += _dot_tn(a_ref[...].astype(BF16), b_ref[...].astype(BF16))

        @pl.when(k == nk - 1)
        def _():
            o_ref[...] = (scale * acc_ref[...]).astype(BF16)

    return pl.pallas_call(
        functools.partial(body), name="matmul_tn",
        grid=(m // tmm, nk),
        in_specs=[pl.BlockSpec((tk, tmm), lambda i, k: (k, i)), pl.BlockSpec((tk, n), lambda i, k: (k, 0))],
        out_specs=pl.BlockSpec((tmm, n), lambda i, k: (i, 0)),
        out_shape=jax.ShapeDtypeStruct((m, n), BF16),
        scratch_shapes=[pltpu.VMEM((tmm, n), F32)],
        compiler_params=_params(2),
    )(a, b)


def input_projection(x, g, w_qkv_t, w_z_t, tm):
    t = x.shape[0]

    def body(x_ref, g_ref, wq_ref, wz_ref, qkv_ref, z_ref, h_ref):
        xx = x_ref[...]
        h = ((xx * _rstd(xx)) * g_ref[...]).astype(BF16)
        h_ref[...] = h
        qkv_ref[...] = _dot_nt(h, wq_ref[...])
        z_ref[...] = _dot_nt(h, wz_ref[...])

    row = lambda n: pl.BlockSpec((tm, n), lambda i: (i, 0))
    return pl.pallas_call(
        functools.partial(body), name="input_projection", grid=(t // tm,),
        in_specs=[row(D_MODEL), _full((1, D_MODEL)), _full((D_QKV, D_MODEL)), _full((2 * D_SGU, D_MODEL))],
        out_specs=[row(D_QKV), row(2 * D_SGU), row(D_MODEL)],
        out_shape=[jax.ShapeDtypeStruct((t, D_QKV), F32), jax.ShapeDtypeStruct((t, 2 * D_SGU), F32),
                   jax.ShapeDtypeStruct((t, D_MODEL), BF16)],
        compiler_params=_params(1))(x, g, w_qkv_t, w_z_t)


def _head_tile_spec(tm, rows):
    return pl.BlockSpec((N_HEADS, None, rows, tm), lambda i: (0, i, 0, 0))


def _to_head_tiles(a):
    return a.T.reshape(N_HEADS, HEAD_DIM, a.shape[0])


def _from_head_tiles(a):
    return a.reshape(D_ATTN, a.shape[-1]).T


def _head_mean_matrix(width):
    head = jnp.arange(width) // HEAD_DIM
    return (head[:, None] == head[None, :]).astype(F32) / HEAD_DIM


def _kv_tile_spec(n_sub, rows, cols):
    return pl.BlockSpec((N_KV_HEADS, n_sub, rows, cols), lambda i: (0, i, 0, 0))


def qk_prep(qkv, gq_w, gk_w, cos_w, sin_w, mean_q, mean_k, tm, tk):
    t = qkv.shape[0]
    n_sub = tm // tk

    def body(p_ref, gq_ref, gk_ref, cos_ref, sin_ref, mq_ref, mk_ref, q_ref, k_ref, kt_ref, v_ref, vt_ref):
        cos2, sin2 = cos_ref[...], sin_ref[...]
        q = p_ref[:, :D_ATTN]
        k = p_ref[:, D_ATTN:D_ATTN + D_KV]
        qn = q * lax.rsqrt(_dot_split(q * q, mq_ref[...]) + EPS) * gq_ref[...]
        kn = k * lax.rsqrt(_dot_split(k * k, mk_ref[...]) + EPS) * gk_ref[...]
        cos8, sin8 = _tile_lanes(cos2, D_ATTN // LANES), _tile_lanes(sin2, D_ATTN // LANES)
        q_rot = (qn * cos8 + _pair_swap(qn) * sin8) * Q_SCALE
        q_ref[...] = _to_head_tiles(q_rot).astype(BF16)
        k_rot = kn * cos2 + _pair_swap(kn) * sin2
        vv = p_ref[:, D_ATTN + D_KV:]
        for a, tok_ref, feat_ref in ((k_rot, k_ref, kt_ref), (vv, v_ref, vt_ref)):
            second = pltpu.roll(a, HEAD_DIM, 1)
            for c in range(n_sub):
                rows = slice(c * tk, (c + 1) * tk)
                tok_ref[0, c] = a[rows, :HEAD_DIM].astype(BF16)
                tok_ref[1, c] = second[rows, :HEAD_DIM].astype(BF16)
                feat_ref[:, c, :HEAD_DIM, :] = a[rows].T.reshape(N_KV_HEADS, HEAD_DIM, tk).astype(BF16)
        vt_ref[:, :, HEAD_DIM:, :] = jnp.ones((N_KV_HEADS, n_sub, ONES_ROWS, tk), BF16)

    kv = lambda rows, cols: jax.ShapeDtypeStruct((N_KV_HEADS, t // tk, rows, cols), BF16)
    return pl.pallas_call(
        functools.partial(body), name="qk_prep", grid=(t // tm,),
        in_specs=[pl.BlockSpec((tm, D_QKV), lambda i: (i, 0)), _full((1, D_ATTN)), _full((1, D_KV)),
                  pl.BlockSpec((tm, LANES), lambda i: (i, 0)), pl.BlockSpec((tm, LANES), lambda i: (i, 0)),
                  _full((D_ATTN, D_ATTN)), _full((D_KV, D_KV))],
        out_specs=[_head_tile_spec(tm, HEAD_DIM), _kv_tile_spec(n_sub, tk, HEAD_DIM), _kv_tile_spec(n_sub, HEAD_DIM, tk),
                   _kv_tile_spec(n_sub, tk, HEAD_DIM), _kv_tile_spec(n_sub, HEAD_DIM + ONES_ROWS, tk)],
        out_shape=[jax.ShapeDtypeStruct((N_HEADS, t // tm, HEAD_DIM, tm), BF16), kv(tk, HEAD_DIM), kv(HEAD_DIM, tk),
                   kv(tk, HEAD_DIM), kv(HEAD_DIM + ONES_ROWS, tk)],
        compiler_params=_params(1),
    )(qkv, gq_w, gk_w, cos_w, sin_w, mean_q, mean_k)


def qk_bwd(dq_rot, dk_rot, dv, qkv, gq_w, gk_w, cos_w, sin_w, mean_q, mean_k, tm):
    t = qkv.shape[0]
    tk = dk_rot.shape[-1]
    n_sub = tm // tk

    def token_major(ref):
        return jnp.concatenate([ref[:, c].reshape(D_KV, tk).T for c in range(n_sub)], axis=0)

    def branch(raw, d_rot, gain, mean_mat, cos, sin, scale):
        r = lax.rsqrt(_dot_split(raw * raw, mean_mat) + EPS)
        n = raw * r
        dy = (d_rot * cos - _pair_swap(d_rot) * sin) * scale
        dn = dy * gain
        return r * (dn - n * _dot_split(dn * n, mean_mat)), dy * n

    def body(dq_ref, dk_ref, dv_ref, p_ref, gq_ref, gk_ref, cos_ref, sin_ref, mq_ref, mk_ref,
             dp_ref, dgq_ref, dgk_ref):
        cos2, sin2 = cos_ref[...], sin_ref[...]
        cos8, sin8 = _tile_lanes(cos2, D_ATTN // LANES), _tile_lanes(sin2, D_ATTN // LANES)
        dq, dgq = branch(p_ref[:, :D_ATTN], _from_head_tiles(dq_ref[...]), gq_ref[...], mq_ref[...], cos8, sin8,
                         HEAD_DIM ** -0.5)
        dk, dgk = branch(p_ref[:, D_ATTN:D_ATTN + D_KV], token_major(dk_ref), gk_ref[...], mk_ref[...], cos2, sin2, 1.0)
        dp_ref[...] = jnp.concatenate([dq, dk, token_major(dv_ref)], axis=-1).astype(BF16)

        @pl.when(pl.program_id(0) == 0)
        def _():
            dgq_ref[...] = jnp.zeros_like(dgq_ref)
            dgk_ref[...] = jnp.zeros_like(dgk_ref)

        dgq_ref[...] += _colsum(dgq)
        dgk_ref[...] += _colsum(dgk)

    return pl.pallas_call(
        functools.partial(body), name="qk_bwd", grid=(t // tm,),
        in_specs=[_head_tile_spec(tm, HEAD_DIM), _kv_tile_spec(n_sub, HEAD_DIM, tk),
                  _kv_tile_spec(n_sub, HEAD_DIM, tk), pl.BlockSpec((tm, D_QKV), lambda i: (i, 0)),
                  _full((1, D_ATTN)), _full((1, D_KV)),
                  pl.BlockSpec((tm, LANES), lambda i: (i, 0)), pl.BlockSpec((tm, LANES), lambda i: (i, 0)),
                  _full((D_ATTN, D_ATTN)), _full((D_KV, D_KV))],
        out_specs=[pl.BlockSpec((tm, D_QKV), lambda i: (i, 0)), _full((1, D_ATTN)), _full((1, D_KV))],
        out_shape=[jax.ShapeDtypeStruct((t, D_QKV), BF16), jax.ShapeDtypeStruct((1, D_ATTN), F32),
                   jax.ShapeDtypeStruct((1, D_KV), F32)],
        compiler_params=_params(1),
    )(dq_rot, dk_rot, dv, qkv, gq_w, gk_w, cos_w, sin_w, mean_q, mean_k)


def attention_fwd(q_t, k, v_t, comm=None):
    _, nq, _, tq = q_t.shape
    _, nk, tk, _ = k.shape

    def body(q_ref, k_ref, v_ref, o_ref, lse_ref, s_scr, p_scr):
        q = q_ref[...]
        s_scr[0] = _dot(k_ref[0], q)
        p_scr[1] = jnp.zeros((tk, tq), BF16)

        def step(j, slot, carry):
            m, acc = carry
            s = s_scr[slot]
            pv = _dot(v_ref[jnp.maximum(j - 1, 0)], p_scr[1 - slot])
            s_scr[1 - slot] = _dot(k_ref[jnp.minimum(j + 1, nk - 1)], q)
            m_new = jnp.maximum(m, jnp.max(s, axis=0, keepdims=True))
            p_scr[slot] = jnp.exp2(s - m_new).astype(BF16)
            return m_new, jnp.exp2(m - m_new) * (acc + pv)

        m, acc = _loop_pairs(nk, step, (jnp.full((1, tq), -1e30, F32), jnp.zeros((HEAD_DIM + ONES_ROWS, tq), F32)))
        acc = acc + _dot(v_ref[nk - 1], p_scr[(nk - 1) % 2])
        l = acc[HEAD_DIM:HEAD_DIM + 1]
        o_ref[...] = acc[:HEAD_DIM] / l
        lse_ref[...] = m + jnp.log2(l)

    return _pallas(
        comm, body, name="attention_fwd", grid=(N_HEADS, nq),
        in_specs=[pl.BlockSpec((None, None, HEAD_DIM, tq), lambda h, i: (h, i, 0, 0)),
                  pl.BlockSpec((None, nk, tk, HEAD_DIM), lambda h, i: (h // KV_GROUP, 0, 0, 0)),
                  pl.BlockSpec((None, nk, HEAD_DIM + ONES_ROWS, tk), lambda h, i: (h // KV_GROUP, 0, 0, 0))],
        out_specs=[pl.BlockSpec((None, None, HEAD_DIM, tq), lambda h, i: (h, i, 0, 0)),
                   pl.BlockSpec((None, None, 1, tq), lambda h, i: (h, i, 0, 0))],
        out_shape=[jax.ShapeDtypeStruct((N_HEADS, nq, HEAD_DIM, tq), F32),
                   jax.ShapeDtypeStruct((N_HEADS, nq, 1, tq), F32)],
        scratch_shapes=[pltpu.VMEM((2, tk, tq), F32), pltpu.VMEM((2, tk, tq), BF16)],
        args=(q_t, k, v_t))


def attention_bwd(q_t, do_t, lse, delta, k, k_t, v, comm=None):
    _, nq, _, tq = q_t.shape
    _, nk, tk, _ = k.shape

    def body(q_ref, do_ref, lse_ref, delta_ref, k_ref, kt_ref, v_ref, dq_ref, dk_ref, dv_ref,
             s_scr, dp_scr, p_scr, ds_scr):
        @pl.when(pl.program_id(1) == 0)
        def _():
            dq_ref[...] = jnp.zeros_like(dq_ref)

        kk, kt, vv = k_ref[...], kt_ref[...], v_ref[...]
        n = KV_GROUP * nq
        s_scr[0] = _dot(kk, q_ref[0, 0])
        dp_scr[0] = _dot(vv, do_ref[0, 0])
        p_scr[1] = jnp.zeros((tk, tq), BF16)
        ds_scr[1] = jnp.zeros((tk, tq), BF16)

        def products(t, slot, dk, dv):
            h, i = t // nq, t % nq
            ds = ds_scr[slot]
            dq_ref[h, i] += _dot(kt, ds)
            return dk + _dot_nt(q_ref[h, i], ds), dv + _dot_nt(do_ref[h, i], p_scr[slot])

        def step(t, slot, carry):
            s, dp = s_scr[slot], dp_scr[slot]
            dk, dv = products(jnp.maximum(t - 1, 0), 1 - slot, *carry)
            nxt = jnp.minimum(t + 1, n - 1)
            s_scr[1 - slot] = _dot(kk, q_ref[nxt // nq, nxt % nq])
            dp_scr[1 - slot] = _dot(vv, do_ref[nxt // nq, nxt % nq])
            h, i = t // nq, t % nq
            p = jnp.exp2(s - lse_ref[h, i])
            p_scr[slot] = p.astype(BF16)
            ds_scr[slot] = (p * (dp - delta_ref[h, i])).astype(BF16)
            return dk, dv

        zero = jnp.zeros((HEAD_DIM, tk), F32)
        dk, dv = products(n - 1, (n - 1) % 2, *_loop_pairs(n, step, (zero, zero)))
        dk_ref[...] = dk * (1.0 / LOG2_E)
        dv_ref[...] = dv

    group = lambda g, j: (g, 0, 0, 0)
    tile = lambda g, j: (g, j, 0, 0)
    return _pallas(
        comm, body, name="attention_bwd", grid=(N_KV_HEADS, nk),
        in_specs=[pl.BlockSpec((KV_GROUP, nq, HEAD_DIM, tq), group),
                  pl.BlockSpec((KV_GROUP, nq, HEAD_DIM, tq), group),
                  pl.BlockSpec((KV_GROUP, nq, 1, tq), group),
                  pl.BlockSpec((KV_GROUP, nq, 1, tq), group),
                  pl.BlockSpec((None, None, tk, HEAD_DIM), tile),
                  pl.BlockSpec((None, None, HEAD_DIM, tk), tile),
                  pl.BlockSpec((None, None, tk, HEAD_DIM), tile)],
        out_specs=[pl.BlockSpec((KV_GROUP, nq, HEAD_DIM, tq), group),
                   pl.BlockSpec((None, None, HEAD_DIM, tk), tile),
                   pl.BlockSpec((None, None, HEAD_DIM, tk), tile)],
        out_shape=[jax.ShapeDtypeStruct((N_HEADS, nq, HEAD_DIM, tq), F32),
                   jax.ShapeDtypeStruct((N_KV_HEADS, nk, HEAD_DIM, tk), F32),
                   jax.ShapeDtypeStruct((N_KV_HEADS, nk, HEAD_DIM, tk), F32)],
        scratch_shapes=[pltpu.VMEM((2, tk, tq), F32), pltpu.VMEM((2, tk, tq), F32),
                        pltpu.VMEM((2, tk, tq), BF16), pltpu.VMEM((2, tk, tq), BF16)],
        args=(q_t, do_t, lse, delta, k, k_t, v))


def _group_select(parts):
    lane_group = lax.broadcasted_iota(jnp.int32, parts[0].shape, 1) // SGU_GROUP_DIM
    out = parts[0]
    for g in range(1, N_SGU_GROUPS):
        out = jnp.where(lane_group == g, parts[g], out)
    return out


def _gate_forward(z, g_sgu, ws_ref, bias):
    gz, th = _gelu(z)
    u, vv = gz[:, :D_SGU], gz[:, D_SGU:]
    rv = _rstd(vv)
    nv = vv * rv
    vn = (nv * g_sgu).astype(BF16)
    fs = []
    for c in range(z.shape[0] // CHUNK):
        vc = vn[c * CHUNK:(c + 1) * CHUNK]
        fs.append(_group_select([_dot(ws_ref[g], vc) for g in range(N_SGU_GROUPS)]) + bias)
    f = jnp.concatenate(fs, axis=0) if len(fs) > 1 else fs[0]
    return th, u, rv, nv, vn, f


def mix_out(z, o, x, g_sgu, g_ao, g_so, ws, bias, w_out, tm):
    t = x.shape[0]

    def body(z_ref, o_ref, x_ref, gs_ref, gao_ref, gso_ref, ws_ref, bias_ref, wout_ref, x2_ref, mixed_ref):
        _, u, _, _, _, f = _gate_forward(z_ref[...], gs_ref[...], ws_ref, bias_ref[...])
        sgu = u * f
        oo = _from_head_tiles(o_ref[...])
        mixed = jnp.concatenate([oo * _rstd(oo) * gao_ref[...], sgu * _rstd(sgu) * gso_ref[...]], axis=-1).astype(BF16)
        mixed_ref[...] = mixed
        x2_ref[...] = x_ref[...] + _dot(mixed, wout_ref[...])

    row = lambda n: pl.BlockSpec((tm, n), lambda i: (i, 0))
    return pl.pallas_call(
        functools.partial(body), name="mix_out", grid=(t // tm,),
        in_specs=[row(2 * D_SGU), _head_tile_spec(tm, HEAD_DIM), row(D_MODEL), _full((1, D_SGU)), _full((1, D_ATTN)),
                  _full((1, D_SGU)),
                  _full((N_SGU_GROUPS, CHUNK, CHUNK)), _full((CHUNK, D_SGU)), _full((D_MODEL, D_MODEL))],
        out_specs=[row(D_MODEL), row(D_MODEL)],
        out_shape=[jax.ShapeDtypeStruct((t, D_MODEL), F32), jax.ShapeDtypeStruct((t, D_MODEL), BF16)],
        compiler_params=_params(1),
    )(z, o, x, g_sgu, g_ao, g_so, ws, bias, w_out)


def mix_bwd(dx2, z, o, g_sgu, g_ao, g_so, ws, ws_t, bias, w_out, group_ind, tm):
    t = dx2.shape[0]
    n_tiles = t // tm

    def body(dx_ref, z_ref, o_ref, gs_ref, gao_ref, gso_ref, ws_ref, wst_ref, bias_ref, wout_ref, ind_ref,
             do_ref, delta_ref, dz_ref, dg_ref, dws_ref, dbs_ref, df_sum):
        step = pl.program_id(0)

        @pl.when(step == 0)
        def _():
            dg_ref[...] = jnp.zeros_like(dg_ref)
            dws_ref[...] = jnp.zeros_like(dws_ref)
            df_sum[...] = jnp.zeros_like(df_sum)

        z = z_ref[...]
        th, u, rv, nv, vn, f = _gate_forward(z, gs_ref[...], ws_ref, bias_ref[...])
        dmixed = _dot_nt(dx_ref[...].astype(BF16), wout_ref[...])
        o_tiles = o_ref[...]
        oo = _from_head_tiles(o_tiles)
        ro = _rstd(oo)
        d_o, dgao = _rms_bwd(dmixed[:, :D_ATTN], oo * ro, ro, gao_ref[...])
        do_tiles = _to_head_tiles(d_o)
        do_ref[...] = do_tiles.astype(BF16)
        delta_ref[...] = jnp.sum(do_tiles * o_tiles, axis=1, keepdims=True)
        sgu = u * f
        rs = _rstd(sgu)
        dsgu, dgso = _rms_bwd(dmixed[:, D_ATTN:], sgu * rs, rs, gso_ref[...])
        du = dsgu * f
        df = dsgu * u
        lane_group = lax.broadcasted_iota(jnp.int32, (CHUNK, D_SGU), 1) // SGU_GROUP_DIM
        dvns = []
        df_acc = jnp.zeros((CHUNK, D_SGU), F32)
        for c in range(tm // CHUNK):
            dfc32 = df[c * CHUNK:(c + 1) * CHUNK]
            dfc = dfc32.astype(BF16)
            vc = vn[c * CHUNK:(c + 1) * CHUNK]
            dvns.append(_group_select([_dot(wst_ref[g], dfc) for g in range(N_SGU_GROUPS)]))
            for g in range(N_SGU_GROUPS):
                dws_ref[g] += _dot_nt(jnp.where(lane_group == g, dfc, jnp.zeros_like(dfc)), vc)
            df_acc = df_acc + dfc32
        df_sum[...] += df_acc
        dvn = jnp.concatenate(dvns, axis=0) if len(dvns) > 1 else dvns[0]
        dvv, dgs = _rms_bwd(dvn, nv, rv, gs_ref[...])
        dz_ref[...] = (jnp.concatenate([du, dvv], axis=-1) * _gelu_grad(z, th)).astype(BF16)
        dg_ref[0:1, :] += _colsum(dgao)
        dg_ref[1:2, :] += _colsum(dgso)
        dg_ref[2:3, :] += _colsum(dgs)

        @pl.when(step == n_tiles - 1)
        def _():
            dbs_ref[...] = _dot_f32(df_sum[...], ind_ref[...])

    row = lambda n: pl.BlockSpec((tm, n), lambda i: (i, 0))
    return pl.pallas_call(
        functools.partial(body), name="mix_bwd", grid=(n_tiles,),
        in_specs=[row(D_MODEL), row(2 * D_SGU), _head_tile_spec(tm, HEAD_DIM), _full((1, D_SGU)), _full((1, D_ATTN)),
                  _full((1, D_SGU)),
                  _full((N_SGU_GROUPS, CHUNK, CHUNK)), _full((N_SGU_GROUPS, CHUNK, CHUNK)), _full((CHUNK, D_SGU)),
                  _full((D_MODEL, D_MODEL)), _full((D_SGU, LANES))],
        out_specs=[_head_tile_spec(tm, HEAD_DIM), _head_tile_spec(tm, 1), row(2 * D_SGU), _full((8, D_SGU)),
                   _full((N_SGU_GROUPS, CHUNK, CHUNK)), _full((CHUNK, LANES))],
        out_shape=[jax.ShapeDtypeStruct((N_HEADS, n_tiles, HEAD_DIM, tm), BF16),
                   jax.ShapeDtypeStruct((N_HEADS, n_tiles, 1, tm), F32), jax.ShapeDtypeStruct((t, 2 * D_SGU), BF16),
                   jax.ShapeDtypeStruct((8, D_SGU), F32),
                   jax.ShapeDtypeStruct((N_SGU_GROUPS, CHUNK, CHUNK), F32),
                   jax.ShapeDtypeStruct((CHUNK, LANES), F32)],
        scratch_shapes=[pltpu.VMEM((CHUNK, D_SGU), F32)],
        compiler_params=_params(1),
    )(dx2, z, o, g_sgu, g_ao, g_so, ws, ws_t, bias, w_out, group_ind)


def loss_bwd(x, g, target, tm):
    t = x.shape[0]

    def body(x_ref, g_ref, t_ref, loss_ref, dx_ref, dg_ref):
        @pl.when(pl.program_id(0) == 0)
        def _():
            loss_ref[...] = jnp.zeros_like(loss_ref)
            dg_ref[...] = jnp.zeros_like(dg_ref)

        xx = x_ref[...]
        r = _rstd(xx)
        n = xx * r
        err = n * g_ref[...] - t_ref[...]
        per_token = jnp.mean(err * err, axis=-1, keepdims=True)
        loss_ref[...] += 0.5 * jnp.sum(per_token, axis=0, keepdims=True)
        dx, dg_rows = _rms_bwd(err * (1.0 / D_MODEL), n, r, g_ref[...])
        dx_ref[...] = dx
        dg_ref[...] += _colsum(dg_rows)

    row = pl.BlockSpec((tm, D_MODEL), lambda i: (i, 0))
    return pl.pallas_call(
        functools.partial(body), name="loss_bwd", grid=(t // tm,),
        in_specs=[row, _full((1, D_MODEL)), row],
        out_specs=[_full((1, LANES)), row, _full((1, D_MODEL))],
        out_shape=[jax.ShapeDtypeStruct((1, LANES), F32), jax.ShapeDtypeStruct((t, D_MODEL), F32),
                   jax.ShapeDtypeStruct((1, D_MODEL), F32)],
        compiler_params=_params(1),
    )(x, g, target)


def _rope_tables(t):
    rows = t // GRID_W
    row_idx = jnp.repeat(jnp.arange(rows, dtype=F32), GRID_W)
    col_idx = jnp.tile(jnp.arange(GRID_W, dtype=F32), rows)
    axis_dim = HEAD_DIM // 2
    inv = 1.0 / (ROPE_THETA ** (jnp.arange(0, axis_dim, 2, dtype=F32) / axis_dim))
    ang = jnp.concatenate([row_idx[:, None] * inv, col_idx[:, None] * inv], axis=-1)
    cos = jnp.repeat(jnp.cos(ang), 2, axis=-1)
    sin = jnp.repeat(jnp.sin(ang), 2, axis=-1) * jnp.tile(jnp.array([-1.0, 1.0], F32), HEAD_DIM // 2)
    return jnp.tile(cos, (1, LANES // HEAD_DIM)), jnp.tile(sin, (1, LANES // HEAD_DIM))


def _heads_to_tiles_t(a, n_heads, tile):
    t = a.shape[0]
    return a.reshape(t // tile, tile, n_heads, HEAD_DIM).transpose(2, 0, 3, 1)


def _heads_to_tiles(a, n_heads, tile):
    t = a.shape[0]
    return a.reshape(t // tile, tile, n_heads, HEAD_DIM).transpose(2, 0, 1, 3)


def _tiles_t_to_heads(a):
    h, n, _, tile = a.shape
    return a.transpose(1, 3, 0, 2).reshape(n * tile, h * HEAD_DIM)


def kernel(x, g_ffn1, w1_gate, w1_up, w1_down, g_mix, w_in, g_q, g_k, g_sgu, w_s, b_s, g_attn_out, g_sgu_out, w_out, g_ffn2, w2_gate, w2_up, w2_down, g_final, loss_target, m_g_ffn1, m_w1_gate, m_w1_up, m_w1_down, m_g_mix, m_w_in, m_g_q, m_g_k, m_g_sgu, m_w_s, m_b_s, m_g_attn_out, m_g_sgu_out, m_w_out, m_g_ffn2, m_w2_gate, m_w2_up, m_w2_down, m_g_final, v_g_ffn1, v_w1_gate, v_w1_up, v_w1_down, v_g_mix, v_w_in, v_g_q, v_g_k, v_g_sgu, v_w_s, v_b_s, v_g_attn_out, v_g_sgu_out, v_w_out, v_g_ffn2, v_w2_gate, v_w2_up, v_w2_down, v_g_final):
    weights = dict(g_ffn1=g_ffn1, w1_gate=w1_gate, w1_up=w1_up, w1_down=w1_down, g_mix=g_mix, w_in=w_in, g_q=g_q,
                   g_k=g_k, g_sgu=g_sgu, w_s=w_s, b_s=b_s, g_attn_out=g_attn_out, g_sgu_out=g_sgu_out, w_out=w_out,
                   g_ffn2=g_ffn2, w2_gate=w2_gate, w2_up=w2_up, w2_down=w2_down, g_final=g_final)
    m_in = dict(g_ffn1=m_g_ffn1, w1_gate=m_w1_gate, w1_up=m_w1_up, w1_down=m_w1_down, g_mix=m_g_mix, w_in=m_w_in,
                g_q=m_g_q, g_k=m_g_k, g_sgu=m_g_sgu, w_s=m_w_s, b_s=m_b_s, g_attn_out=m_g_attn_out,
                g_sgu_out=m_g_sgu_out, w_out=m_w_out, g_ffn2=m_g_ffn2, w2_gate=m_w2_gate, w2_up=m_w2_up,
                w2_down=m_w2_down, g_final=m_g_final)
    v_in = dict(g_ffn1=v_g_ffn1, w1_gate=v_w1_gate, w1_up=v_w1_up, w1_down=v_w1_down, g_mix=v_g_mix, w_in=v_w_in,
                g_q=v_g_q, g_k=v_g_k, g_sgu=v_g_sgu, w_s=v_w_s, b_s=v_b_s, g_attn_out=v_g_attn_out,
                g_sgu_out=v_g_sgu_out, w_out=v_w_out, g_ffn2=v_g_ffn2, w2_gate=v_w2_gate, w2_up=v_w2_up,
                w2_down=v_w2_down, g_final=v_g_final)
    names = list(weights)

    t = x.shape[1]
    x0 = x[0]
    target = loss_target[0]
    tm = min(256, t)
    tm_ff = min(256, t)
    tn_ff = 256
    tq = min(512, t)
    tk = min(256, t)
    tk_w = min(512, t)

    def shard_rows(name):
        w = weights[name][0]
        return (w.T if name in TRANSPOSED else w).astype(BF16)

    rows_of = dict(SHARD_ROWS)
    full = {}

    def packed(group):
        return jnp.concatenate([shard_rows(n) for n in group], axis=0), [rows_of[n] for n in group]

    def gather_of(group):
        return gather_exchange(*packed(group))

    def take(group, gathered):
        for n, g in zip(group, gathered):
            full[n] = g.reshape(N_DEV * rows_of[n], D_MODEL)

    first, second, third = ("w1_gate", "w1_up"), ("w1_down", "w_in", "w_out"), ("w2_gate", "w2_up", "w2_down")
    take(first, gather_two_level(*packed(first), "gather_first"))

    (h1, a1, b1, act1), gathered = ffn_up(x0, g_ffn1, full["w1_gate"], full["w1_up"], tm_ff, tn_ff, gather_of(second))
    take(second, gathered)
    w_in_t = full["w_in"]
    w_qkv_t, w_z_t = w_in_t[:D_QKV], w_in_t[D_QKV:]
    x1 = ffn_down(act1, full["w1_down"], x0, tm)

    qkv, z, h2 = input_projection(x1, g_mix, w_qkv_t, w_z_t, tm)
    cos_w, sin_w = _rope_tables(t)
    gq_w = jnp.tile(g_q, (1, N_HEADS))
    gk_w = jnp.tile(g_k, (1, N_KV_HEADS))
    mean_q, mean_k = _head_mean_matrix(D_ATTN).astype(BF16), _head_mean_matrix(D_KV).astype(BF16)
    q_t, k_tiles, kt_tiles, v_tiles, vt_tiles = qk_prep(qkv, gq_w, gk_w, cos_w, sin_w, mean_q, mean_k, tq, tk)
    (o_t, lse), gathered = attention_fwd(q_t, k_tiles, vt_tiles, gather_of(third))
    take(third, gathered)

    ws_b = w_s[0].astype(BF16)
    ws_tb = jnp.swapaxes(w_s[0], 1, 2).astype(BF16)
    bias = jnp.repeat(b_s[0].T, SGU_GROUP_DIM, axis=1)
    x2, mixed = mix_out(z, o_t, x1, g_sgu, g_attn_out, g_sgu_out, ws_b, bias, full["w_out"], tq)

    (h3, a2, b2, act2), _ = ffn_up(x2, g_ffn2, full["w2_gate"], full["w2_up"], tm_ff, tn_ff)
    x3 = ffn_down(act2, full["w2_down"], x2, tm)

    loss_part, dx3, dg_final = loss_bwd(x3, g_final, target, tm)

    tmm = D_FF // 2
    (da2, db2), _ = ffn_bwd_act(dx3, full["w2_down"], a2, b2, tm_ff, tn_ff)
    (dx2, dg_ffn2), _ = norm_bwd_matmul(da2, full["w2_gate"], db2, full["w2_up"], x2, g_ffn2, dx3, tm)
    dwg2, dwu2 = matmul_tn(da2, h3, 1.0, tmm, tk_w), matmul_tn(db2, h3, 1.0, tmm, tk_w)
    dwd2 = matmul_tn(act2, dx3, 0.5, tmm, tk_w)

    group_ind = (jnp.arange(D_SGU)[:, None] // SGU_GROUP_DIM == jnp.arange(LANES)[None, :]).astype(F32)
    do_t, delta, dz, dg_mixrow, dws, dbs = mix_bwd(dx2, z, o_t, g_sgu, g_attn_out, g_sgu_out, ws_b, ws_tb, bias,
                                                   full["w_out"], group_ind, tq)
    dw_out = matmul_tn(mixed, dx2, 1.0, D_MODEL // 2, tk_w)

    group_a = ("w2_gate", "w2_up", "w2_down", "w_out")
    (dq_t, dk_t, dv_t), (parts_a,) = attention_bwd(q_t, do_t, lse, delta, k_tiles, kt_tiles, v_tiles,
                                                   scatter_exchange([dwg2, dwu2, dwd2, dw_out]))
    dqkv, dgq_w, dgk_w = qk_bwd(dq_t, dk_t, dv_t, qkv, gq_w, gk_w, cos_w, sin_w, mean_q, mean_k, tq)

    def pack_small(arrays):
        pieces = []
        for a in arrays:
            flat = a.reshape(-1)
            pieces.append(jnp.pad(flat, (0, (-flat.shape[0]) % (8 * LANES))).reshape(-1, LANES))
        return jnp.concatenate(pieces, axis=0), [p.shape[0] for p in pieces]

    early = dict(g_ffn2=dg_ffn2, g_final=dg_final, g_q=dgq_w.reshape(N_HEADS, HEAD_DIM).sum(0),
                 g_k=dgk_w.reshape(N_KV_HEADS, HEAD_DIM).sum(0), g_attn_out=dg_mixrow[0], g_sgu_out=dg_mixrow[1],
                 g_sgu=dg_mixrow[2], w_s=dws, b_s=dbs[:, :N_SGU_GROUPS].T)
    early_pack, early_rows = pack_small(list(early.values()))
    (dx1, dg_mix), (early_parts,) = norm_bwd_matmul(dqkv, w_qkv_t, dz, w_z_t, x1, g_mix, dx2, tm,
                                                    gather_exchange(early_pack, [early_pack.shape[0]]))
    dw_in = jnp.concatenate([matmul_tn(dqkv, h2, 1.0, D_QKV // 2, tk_w), matmul_tn(dz, h2, 1.0, D_SGU, tk_w)], axis=0)

    dwd1 = matmul_tn(act1, dx1, 0.5, tmm, tk_w)
    group_b = ("w_in", "w1_down")
    (da1, db1), (parts_b,) = ffn_bwd_act(dx1, full["w1_down"], a1, b1, tm_ff, tn_ff, scatter_exchange([dw_in, dwd1]))
    dwg1, dwu1 = matmul_tn(da1, h1, 1.0, tmm, tk_w), matmul_tn(db1, h1, 1.0, tmm, tk_w)
    group_c = ("w1_gate", "w1_up")
    (dx0, dg_ffn1), (parts_c,) = norm_bwd_matmul(da1, full["w1_gate"], db1, full["w1_up"], x0, g_ffn1, dx1, tm,
                                                 scatter_exchange([dwg1, dwu1]))

    late = dict(g_mix=dg_mix, g_ffn1=dg_ffn1, loss=loss_part)
    late_pack, late_rows = pack_small(list(late.values()))
    (late_parts,) = run_exchange(gather_exchange(late_pack, [late_pack.shape[0]]), "gather_late_small_grads")
    small_sums = {}
    for entries, rows, parts in ((early, early_rows, early_parts), (late, late_rows, late_parts)):
        summed = sum_parts(parts, parts.shape[1])
        off = 0
        for n, r in zip(entries, rows):
            small_sums[n] = summed[off:off + r]
            off += r
    loss = small_sums.pop("loss")[0, 0]

    grads = {}
    for group, parts in ((group_a, parts_a), (group_b, parts_b), (group_c, parts_c)):
        summed = sum_parts(parts, 32)
        off = 0
        for n in group:
            gsh = summed[off:off + rows_of[n]]
            grads[n] = (gsh.T if n in TRANSPOSED else gsh)[None]
            off += rows_of[n]
    for n, summed in small_sums.items():
        grads[n] = summed.reshape(-1)[:weights[n].size].reshape(weights[n].shape)

    delta_w, new_m, new_v = {}, {}, {}
    for n in names:
        shape = weights[n].shape
        as2d = (lambda a: a.reshape(-1, shape[-1]))
        d, m2, v2 = adamw(as2d(weights[n]), as2d(grads[n]), as2d(m_in[n]), as2d(v_in[n]))
        delta_w[n], new_m[n], new_v[n] = d.reshape(shape), m2.reshape(shape), v2.reshape(shape)

    return (loss, dx0[None], *[grads[n] for n in names], *[delta_w[n] for n in names],
            *[new_m[n] for n in names], *[new_v[n] for n in names])
```

```python
import functools
import math

import jax
import jax.numpy as jnp
from jax import lax
from jax.experimental import pallas as pl
from jax.experimental.pallas import tpu as pltpu

F32 = jnp.float32
BF16 = jnp.bfloat16

D_MODEL = 1024
D_FF = 2816
N_HEADS = 8
HEAD_DIM = 64
N_KV_HEADS = 2
KV_GROUP = N_HEADS // N_KV_HEADS
D_ATTN = N_HEADS * HEAD_DIM
D_KV = N_KV_HEADS * HEAD_DIM
D_QKV = D_ATTN + 2 * D_KV
N_SGU_GROUPS = 8
SGU_GROUP_DIM = 64
D_SGU = N_SGU_GROUPS * SGU_GROUP_DIM
CHUNK = 128
GRID_W = 64
ROPE_THETA = 10000.0
EPS = 1e-6
N_DEV = 8
LANES = 128

ONES_ROWS = 16
LOG2_E = math.log2(math.e)
Q_SCALE = HEAD_DIM ** -0.5 * LOG2_E

ADAM_LR = 0.001
ADAM_B1 = 0.9
ADAM_B2 = 0.999
ADAM_EPS = 1e-08
ADAM_WD = 0.01
ADAM_STEP = 10

MESH_AXES = ("x", "y", "c")
MESH_IDS = pl.DeviceIdType.MESH

VMEM_LIMIT = 56 * 1024 * 1024

SHARD_ROWS = (("w1_gate", D_FF // N_DEV), ("w1_up", D_FF // N_DEV), ("w1_down", D_FF // N_DEV),
              ("w_in", (D_QKV + 2 * D_SGU) // N_DEV), ("w_out", D_MODEL // N_DEV),
              ("w2_gate", D_FF // N_DEV), ("w2_up", D_FF // N_DEV), ("w2_down", D_FF // N_DEV))
PACK_ROWS = sum(r for _, r in SHARD_ROWS)
TRANSPOSED = ("w1_gate", "w1_up", "w_in", "w2_gate", "w2_up")


def _params(n_grid):
    return pltpu.CompilerParams(dimension_semantics=("arbitrary",) * n_grid, vmem_limit_bytes=VMEM_LIMIT)


def _dot(a, b):
    return jnp.dot(a, b, preferred_element_type=F32)


def _dot_nt(a, b):
    return lax.dot_general(a, b, (((1,), (1,)), ((), ())), preferred_element_type=F32)


def _dot_tn(a, b):
    return lax.dot_general(a, b, (((0,), (0,)), ((), ())), preferred_element_type=F32)


def _dot_f32(a, b):
    return jnp.dot(a, b, preferred_element_type=F32, precision=lax.Precision.HIGHEST)


def _dot_split(a, b):
    hi = a.astype(BF16)
    lo = (a - hi.astype(F32)).astype(BF16)
    return _dot(hi, b) + _dot(lo, b)


def _rstd(x):
    return lax.rsqrt(jnp.mean(x * x, axis=-1, keepdims=True) + EPS)


def _rms_bwd(dy, n, r, g):
    dn = dy * g
    return r * (dn - n * jnp.mean(dn * n, axis=-1, keepdims=True)), dy * n


def _colsum(a):
    return jnp.sum(a, axis=0, keepdims=True)


_GELU_C = math.sqrt(2.0 / math.pi)


def _gelu(x):
    t = jnp.tanh(_GELU_C * (x + 0.044715 * (x * x * x)))
    return x * (0.5 * (1.0 + t)), t


def _gelu_grad(x, t):
    return 0.5 * (1.0 + t) + 0.5 * x * (1.0 - t * t) * (_GELU_C * (1.0 + 3 * 0.044715 * x * x))


def _pair_swap(a):
    w = a.shape[-1]
    lane = lax.broadcasted_iota(jnp.int32, a.shape, a.ndim - 1)
    return jnp.where(lane % 2 == 0, pltpu.roll(a, w - 1, a.ndim - 1), pltpu.roll(a, 1, a.ndim - 1))


def _tile_lanes(a, reps):
    return jnp.concatenate([a] * reps, axis=-1) if reps > 1 else a


def _loop_pairs(n, step, carry):
    assert n % 2 == 0, n

    def pair(jj, c):
        return step(2 * jj + 1, 1, step(2 * jj, 0, c))

    return lax.fori_loop(0, n // 2, pair, carry)


def _full(shape):
    nd = len(shape)
    return pl.BlockSpec(shape, lambda *_: (0,) * nd)


def _mesh_pos():
    return lax.axis_index("x"), lax.axis_index("y"), lax.axis_index("c")


def _peer(pos, d):
    x, y, c = pos
    px = 1 - x if d & 4 else x
    py = 1 - y if d & 2 else y
    pc = 1 - c if d & 1 else c
    return (px, py, pc), 4 * px + 2 * py + pc


class _Exchange:
    def __init__(self, operands, out_shape, n_local, plan):
        self.operands = list(operands)
        self.out_shape = list(out_shape)
        self.sem_shapes = [pltpu.SemaphoreType.DMA((N_DEV - 1,)), pltpu.SemaphoreType.DMA((N_DEV - 1,)),
                           pltpu.SemaphoreType.DMA((n_local,))]
        self._plan = plan

    def _copies(self, in_refs, out_refs):
        pos = _mesh_pos()
        return pos, self._plan(4 * pos[0] + 2 * pos[1] + pos[2], in_refs, out_refs)

    def start(self, in_refs, out_refs, sems):
        send_sems, recv_sems, local_sems = sems
        pos, (local, remote, _) = self._copies(in_refs, out_refs)
        for k, (src, dst) in enumerate(local):
            pltpu.make_async_copy(src, dst, local_sems.at[k]).start()
        for d in range(1, N_DEV):
            peer, peer_lin = _peer(pos, d)
            for src, dst in remote(peer_lin):
                pltpu.make_async_remote_copy(src_ref=src, dst_ref=dst, send_sem=send_sems.at[d - 1],
                                             recv_sem=recv_sems.at[d - 1], device_id=peer,
                                             device_id_type=MESH_IDS).start()

    def wait(self, in_refs, out_refs, sems):
        send_sems, recv_sems, local_sems = sems
        pos, (local, _, whole) = self._copies(in_refs, out_refs)
        for d in range(1, N_DEV):
            peer, peer_lin = _peer(pos, d)
            ref = whole(peer_lin)
            everything = pltpu.make_async_remote_copy(src_ref=ref, dst_ref=ref, send_sem=send_sems.at[d - 1],
                                                      recv_sem=recv_sems.at[d - 1], device_id=peer,
                                                      device_id_type=MESH_IDS)
            everything.wait_send()
            everything.wait_recv()
        for k, (src, dst) in enumerate(local):
            pltpu.make_async_copy(src, dst, local_sems.at[k]).wait()


def _offsets(rows):
    offs, o = [], 0
    for r in rows:
        offs.append(o)
        o += r
    return offs


def gather_exchange(src, rows):
    offs = _offsets(rows)

    def plan(me, in_refs, out_refs):
        pieces = [(in_refs[0].at[pl.ds(o, r)], out.at[me]) for o, r, out in zip(offs, rows, out_refs)]
        return pieces, (lambda peer_lin: pieces), (lambda peer_lin: in_refs[0])

    return _Exchange([src], [jax.ShapeDtypeStruct((N_DEV, r) + src.shape[1:], src.dtype) for r in rows],
                     len(rows), plan)


def scatter_exchange(grads):
    rows = [g.shape[0] // N_DEV for g in grads]
    offs = _offsets(rows)

    def plan(me, in_refs, out_refs):
        parts = out_refs[0]

        def slabs(owner):
            return [(g.at[pl.ds(pl.multiple_of(owner * r, 16), r)], parts.at[me, pl.ds(o, r)])
                    for g, o, r in zip(in_refs, offs, rows)]

        return slabs(me), slabs, (lambda peer_lin: parts.at[peer_lin])

    shape = jax.ShapeDtypeStruct((N_DEV, sum(rows)) + grads[0].shape[1:], grads[0].dtype)
    return _Exchange(grads, [shape], len(rows), plan)


def gather_two_level(src, rows, name):
    offs = _offsets(rows)
    n_p = len(rows)

    def body(src_ref, *refs):
        outs, (send_sems, recv_sems, local_sems) = refs[:n_p], refs[n_p:]
        x, y, c = _mesh_pos()
        me, sibling = (x, y, c), (x, y, 1 - c)
        chips = [(1 - x, y), (x, 1 - y), (1 - x, 1 - y)]

        def slab(w, dev):
            return outs[w].at[4 * dev[0] + 2 * dev[1] + dev[2]]

        def copy(w, k, block, to, from_src=False):
            return pltpu.make_async_remote_copy(
                src_ref=src_ref.at[pl.ds(offs[w], rows[w])] if from_src else slab(w, block), dst_ref=slab(w, block),
                send_sem=send_sems.at[w * 7 + k], recv_sem=recv_sems.at[w * 7 + k],
                device_id=to, device_id_type=MESH_IDS)

        mine = [pltpu.make_async_copy(src_ref.at[pl.ds(offs[w], rows[w])], slab(w, me), local_sems.at[w])
                for w in range(n_p)]
        for cp in mine:
            cp.start()
        first = []
        for w in range(n_p):
            first.append(copy(w, 0, me, sibling, True))
            first += [copy(w, 1 + j, me, (*chip, c), True) for j, chip in enumerate(chips)]
        for cp in first:
            cp.start()
        passed = []
        for j, chip in enumerate(chips):
            for w in range(n_p):
                copy(w, 1 + j, (*chip, c), me).wait_recv()
                cp = copy(w, 4 + j, (*chip, c), sibling)
                cp.start()
                passed.append(cp)
        for w in range(n_p):
            copy(w, 0, sibling, me).wait_recv()
            for j, chip in enumerate(chips):
                copy(w, 4 + j, (*chip, 1 - c), me).wait_recv()
        for cp in first + passed:
            cp.wait_send()
        for cp in mine:
            cp.wait()

    any_spec = pl.BlockSpec(memory_space=pl.ANY)
    return pl.pallas_call(
        functools.partial(body), name=name,
        out_shape=[jax.ShapeDtypeStruct((N_DEV, r) + src.shape[1:], src.dtype) for r in rows],
        in_specs=[any_spec], out_specs=[any_spec] * n_p,
        scratch_shapes=[pltpu.SemaphoreType.DMA((7 * n_p,)), pltpu.SemaphoreType.DMA((7 * n_p,)),
                        pltpu.SemaphoreType.DMA((n_p,))],
        compiler_params=pltpu.CompilerParams(has_side_effects=True),
    )(src)


def run_exchange(ex, name):
    n_in, n_out = len(ex.operands), len(ex.out_shape)

    def body(*refs):
        parts = refs[:n_in], refs[n_in:n_in + n_out], refs[n_in + n_out:]
        ex.start(*parts)
        ex.wait(*parts)

    any_spec = pl.BlockSpec(memory_space=pl.ANY)
    return pl.pallas_call(
        functools.partial(body), name=name, out_shape=ex.out_shape,
        in_specs=[any_spec] * n_in, out_specs=[any_spec] * n_out, scratch_shapes=ex.sem_shapes,
        compiler_params=pltpu.CompilerParams(has_side_effects=True),
    )(*ex.operands)


def _pallas(comm, body, *, name, grid, in_specs, out_specs, out_shape, args, scratch_shapes=()):
    params = _params(len(grid))
    if comm is None:
        res = pl.pallas_call(functools.partial(body), name=name, grid=grid, in_specs=list(in_specs),
                             out_specs=list(out_specs), out_shape=list(out_shape),
                             scratch_shapes=list(scratch_shapes), compiler_params=params)(*args)
        return list(res), []
    n_in, n_out, n_scr = len(in_specs), len(out_specs), len(scratch_shapes)
    c_in, c_out = len(comm.operands), len(comm.out_shape)

    def edge(last):
        conds = [pl.program_id(a) == (g - 1 if last else 0) for a, g in enumerate(grid)]
        return functools.reduce(jnp.logical_and, conds)

    def wrapped(*refs):
        refs = list(refs)
        ins, refs = refs[:n_in], refs[n_in:]
        cins, refs = refs[:c_in], refs[c_in:]
        outs, refs = refs[:n_out], refs[n_out:]
        couts, refs = refs[:c_out], refs[c_out:]
        scr, sems = refs[:n_scr], refs[n_scr:]

        @pl.when(edge(False))
        def _():
            comm.start(cins, couts, sems)

        body(*ins, *outs, *scr)

        @pl.when(edge(True))
        def _():
            comm.wait(cins, couts, sems)

    any_spec = pl.BlockSpec(memory_space=pl.ANY)
    res = pl.pallas_call(
        wrapped, name=name, grid=grid,
        in_specs=list(in_specs) + [any_spec] * c_in, out_specs=list(out_specs) + [any_spec] * c_out,
        out_shape=list(out_shape) + comm.out_shape, scratch_shapes=list(scratch_shapes) + comm.sem_shapes,
        compiler_params=pltpu.CompilerParams(dimension_semantics=("arbitrary",) * len(grid),
                                             vmem_limit_bytes=VMEM_LIMIT, has_side_effects=True),
    )(*args, *comm.operands)
    return res[:n_out], res[n_out:]


def sum_parts(parts, block_rows):
    n, rows, cols = parts.shape

    def body(p_ref, o_ref):
        acc = p_ref[0].astype(F32)
        for s in range(1, n):
            acc = acc + p_ref[s].astype(F32)
        o_ref[...] = acc

    return pl.pallas_call(
        functools.partial(body), name="sum_parts",
        grid=(rows // block_rows,),
        in_specs=[pl.BlockSpec((n, block_rows, cols), lambda i: (0, i, 0))],
        out_specs=pl.BlockSpec((block_rows, cols), lambda i: (i, 0)),
        out_shape=jax.ShapeDtypeStruct((rows, cols), F32),
        compiler_params=_params(1),
    )(parts)


def adamw(w, g, m, v):
    def body(w_ref, g_ref, m_ref, v_ref, d_ref, m_out, v_out):
        gg = g_ref[...]
        m2 = ADAM_B1 * m_ref[...] + (1.0 - ADAM_B1) * gg
        v2 = ADAM_B2 * v_ref[...] + (1.0 - ADAM_B2) * (gg * gg)
        m_hat = m2 / (1.0 - ADAM_B1 ** ADAM_STEP)
        v_hat = v2 / (1.0 - ADAM_B2 ** ADAM_STEP)
        d_ref[...] = -ADAM_LR * (m_hat / (jnp.sqrt(v_hat) + ADAM_EPS) + ADAM_WD * w_ref[...])
        m_out[...] = m2
        v_out[...] = v2

    spec = _full(w.shape)
    shape = jax.ShapeDtypeStruct(w.shape, F32)
    return pl.pallas_call(
        functools.partial(body), name="adamw",
        in_specs=[spec] * 4, out_specs=[spec] * 3, out_shape=[shape] * 3,
        compiler_params=pltpu.CompilerParams(vmem_limit_bytes=VMEM_LIMIT),
    )(w, g, m, v)


def ffn_up(x, g, wg_t, wu_t, tm, tn, comm=None):
    t = x.shape[0]

    def body(x_ref, g_ref, wg_ref, wu_ref, h_ref, silu_ref, dgate_ref, act_ref):
        xx = x_ref[...]
        h = ((xx * _rstd(xx)) * g_ref[...]).astype(BF16)
        h_ref[...] = h
        for c in range(D_FF // tn):
            cols = slice(c * tn, (c + 1) * tn)
            a = _dot_nt(h, wg_ref[cols, :])
            b = _dot_nt(h, wu_ref[cols, :])
            sig = 0.5 * jnp.tanh(0.5 * a) + 0.5
            silu = a * sig
            silu_ref[:, cols] = silu.astype(BF16)
            dgate_ref[:, cols] = (b * (sig + silu * (1.0 - sig))).astype(BF16)
            act_ref[:, cols] = (silu * b).astype(BF16)

    wide = jax.ShapeDtypeStruct((t, D_FF), BF16)
    row = lambda n: pl.BlockSpec((tm, n), lambda i: (i, 0))
    return _pallas(
        comm, body, name="ffn_up",
        grid=(t // tm,),
        in_specs=[row(D_MODEL), _full((1, D_MODEL)), _full((D_FF, D_MODEL)), _full((D_FF, D_MODEL))],
        out_specs=[row(D_MODEL), row(D_FF), row(D_FF), row(D_FF)],
        out_shape=[jax.ShapeDtypeStruct((t, D_MODEL), BF16), wide, wide, wide],
        args=(x, g, wg_t, wu_t))


def ffn_down(act, wd, x, tm):
    t = x.shape[0]

    def body(act_ref, wd_ref, x_ref, o_ref):
        o_ref[...] = x_ref[...] + 0.5 * _dot(act_ref[...], wd_ref[...])

    return pl.pallas_call(
        functools.partial(body), name="ffn_down",
        grid=(t // tm,),
        in_specs=[pl.BlockSpec((tm, D_FF), lambda i: (i, 0)), _full((D_FF, D_MODEL)),
                  pl.BlockSpec((tm, D_MODEL), lambda i: (i, 0))],
        out_specs=pl.BlockSpec((tm, D_MODEL), lambda i: (i, 0)),
        out_shape=jax.ShapeDtypeStruct((t, D_MODEL), F32),
        compiler_params=_params(1),
    )(act, wd, x)


def ffn_bwd_act(dx, wd, silu, dgate, tm, tn, comm=None):
    t = dx.shape[0]

    def body(dx_ref, wd_ref, silu_ref, dgate_ref, da_ref, db_ref):
        dxb = (0.5 * dx_ref[...]).astype(BF16)
        for c in range(D_FF // tn):
            cols = slice(c * tn, (c + 1) * tn)
            dact = _dot_nt(dxb, wd_ref[cols, :])
            da_ref[:, cols] = (dact * dgate_ref[:, cols].astype(F32)).astype(BF16)
            db_ref[:, cols] = (dact * silu_ref[:, cols].astype(F32)).astype(BF16)

    wide = jax.ShapeDtypeStruct((t, D_FF), BF16)
    row = lambda n: pl.BlockSpec((tm, n), lambda i: (i, 0))
    return _pallas(
        comm, body, name="ffn_bwd_act",
        grid=(t // tm,),
        in_specs=[row(D_MODEL), _full((D_FF, D_MODEL)), row(D_FF), row(D_FF)],
        out_specs=[row(D_FF), row(D_FF)],
        out_shape=[wide, wide],
        args=(dx, wd, silu, dgate))


def norm_bwd_matmul(a1, w1, a2, w2, x, g, dx_in, tm, comm=None):
    t = x.shape[0]
    k1, k2 = a1.shape[1], a2.shape[1]

    def body(a1_ref, w1_ref, a2_ref, w2_ref, x_ref, g_ref, dxin_ref, dx_ref, dg_ref):
        dh = _dot(a1_ref[...], w1_ref[...]) + _dot(a2_ref[...], w2_ref[...])
        xx = x_ref[...]
        r = _rstd(xx)
        dx, dg_rows = _rms_bwd(dh, xx * r, r, g_ref[...])
        dx_ref[...] = dxin_ref[...] + dx

        @pl.when(pl.program_id(0) == 0)
        def _():
            dg_ref[...] = jnp.zeros_like(dg_ref)

        dg_ref[...] += _colsum(dg_rows)

    row = pl.BlockSpec((tm, D_MODEL), lambda i: (i, 0))
    return _pallas(
        comm, body, name="norm_bwd_matmul",
        grid=(t // tm,),
        in_specs=[pl.BlockSpec((tm, k1), lambda i: (i, 0)), _full((k1, D_MODEL)),
                  pl.BlockSpec((tm, k2), lambda i: (i, 0)), _full((k2, D_MODEL)),
                  row, _full((1, D_MODEL)), row],
        out_specs=[row, _full((1, D_MODEL))],
        out_shape=[jax.ShapeDtypeStruct((t, D_MODEL), F32), jax.ShapeDtypeStruct((1, D_MODEL), F32)],
        args=(a1, w1, a2, w2, x, g, dx_in))


def matmul_tn(a, b, scale, tmm, tk):
    t, m = a.shape
    n = b.shape[1]
    nk = t // tk

    def body(a_ref, b_ref, o_ref, acc_ref):
        k = pl.program_id(1)

        @pl.when(k == 0)
        def _():
            acc_ref[...] = jnp.zeros_like(acc_ref)

        acc_ref[...] += _dot_tn(a_ref[...].astype(BF16), b_ref[...].astype(BF16))

        @pl.when(k == nk - 1)
        def _():
            o_ref[...] = (scale * acc_ref[...]).astype(BF16)

    return pl.pallas_call(
        functools.partial(body), name="matmul_tn",
        grid=(m // tmm, nk),
        in_specs=[pl.BlockSpec((tk, tmm), lambda i, k: (k, i)), pl.BlockSpec((tk, n), lambda i, k: (k, 0))],
        out_specs=pl.BlockSpec((tmm, n), lambda i, k: (i, 0)),
        out_shape=jax.ShapeDtypeStruct((m, n), BF16),
        scratch_shapes=[pltpu.VMEM((tmm, n), F32)],
        compiler_params=_params(2),
    )(a, b)


def input_projection(x, g, w_qkv_t, w_z_t, tm):
    t = x.shape[0]

    def body(x_ref, g_ref, wq_ref, wz_ref, qkv_ref, z_ref, h_ref):
        xx = x_ref[...]
        h = ((xx * _rstd(xx)) * g_ref[...]).astype(BF16)
        h_ref[...] = h
        qkv_ref[...] = _dot_nt(h, wq_ref[...])
        z_ref[...] = _dot_nt(h, wz_ref[...])

    row = lambda n: pl.BlockSpec((tm, n), lambda i: (i, 0))
    return pl.pallas_call(
        functools.partial(body), name="input_projection", grid=(t // tm,),
        in_specs=[row(D_MODEL), _full((1, D_MODEL)), _full((D_QKV, D_MODEL)), _full((2 * D_SGU, D_MODEL))],
        out_specs=[row(D_QKV), row(2 * D_SGU), row(D_MODEL)],
        out_shape=[jax.ShapeDtypeStruct((t, D_QKV), F32), jax.ShapeDtypeStruct((t, 2 * D_SGU), F32),
                   jax.ShapeDtypeStruct((t, D_MODEL), BF16)],
        compiler_params=_params(1))(x, g, w_qkv_t, w_z_t)


def _head_tile_spec(tm, rows):
    return pl.BlockSpec((N_HEADS, None, rows, tm), lambda i: (0, i, 0, 0))


def _to_head_tiles(a):
    return a.T.reshape(N_HEADS, HEAD_DIM, a.shape[0])


def _from_head_tiles(a):
    return a.reshape(D_ATTN, a.shape[-1]).T


def _head_mean_matrix(width):
    head = jnp.arange(width) // HEAD_DIM
    return (head[:, None] == head[None, :]).astype(F32) / HEAD_DIM


def _kv_tile_spec(n_sub, rows, cols):
    return pl.BlockSpec((N_KV_HEADS, n_sub, rows, cols), lambda i: (0, i, 0, 0))


def qk_prep(qkv, gq_w, gk_w, cos_w, sin_w, mean_q, mean_k, tm, tk, tk_v):
    t = qkv.shape[0]
    n_sub, n_sub_v = tm // tk, tm // tk_v

    def body(p_ref, gq_ref, gk_ref, cos_ref, sin_ref, mq_ref, mk_ref, q_ref, k_ref, kt_ref, v_ref, vt_ref):
        cos2, sin2 = cos_ref[...], sin_ref[...]
        q = p_ref[:, :D_ATTN]
        k = p_ref[:, D_ATTN:D_ATTN + D_KV]
        qn = q * lax.rsqrt(_dot_split(q * q, mq_ref[...]) + EPS) * gq_ref[...]
        kn = k * lax.rsqrt(_dot_split(k * k, mk_ref[...]) + EPS) * gk_ref[...]
        cos8, sin8 = _tile_lanes(cos2, D_ATTN // LANES), _tile_lanes(sin2, D_ATTN // LANES)
        q_rot = (qn * cos8 + _pair_swap(qn) * sin8) * Q_SCALE
        q_ref[...] = _to_head_tiles(q_rot).astype(BF16)
        k_rot = kn * cos2 + _pair_swap(kn) * sin2
        vv = p_ref[:, D_ATTN + D_KV:]
        for a, tok_ref in ((k_rot, k_ref), (vv, v_ref)):
            second = pltpu.roll(a, HEAD_DIM, 1)
            for c in range(n_sub):
                rows = slice(c * tk, (c + 1) * tk)
                tok_ref[0, c] = a[rows, :HEAD_DIM].astype(BF16)
                tok_ref[1, c] = second[rows, :HEAD_DIM].astype(BF16)
        for a, feat_ref, width, n in ((k_rot, kt_ref, tk, n_sub), (vv, vt_ref, tk_v, n_sub_v)):
            for c in range(n):
                tile = a[c * width:(c + 1) * width].T.reshape(N_KV_HEADS, HEAD_DIM, width)
                feat_ref[:, c, :HEAD_DIM, :] = tile.astype(BF16)
        vt_ref[:, :, HEAD_DIM:, :] = jnp.ones((N_KV_HEADS, n_sub_v, ONES_ROWS, tk_v), BF16)

    kv = lambda rows, cols: jax.ShapeDtypeStruct((N_KV_HEADS, t // tk, rows, cols), BF16)
    return pl.pallas_call(
        functools.partial(body), name="qk_prep", grid=(t // tm,),
        in_specs=[pl.BlockSpec((tm, D_QKV), lambda i: (i, 0)), _full((1, D_ATTN)), _full((1, D_KV)),
                  pl.BlockSpec((tm, LANES), lambda i: (i, 0)), pl.BlockSpec((tm, LANES), lambda i: (i, 0)),
                  _full((D_ATTN, D_ATTN)), _full((D_KV, D_KV))],
        out_specs=[_head_tile_spec(tm, HEAD_DIM), _kv_tile_spec(n_sub, tk, HEAD_DIM), _kv_tile_spec(n_sub, HEAD_DIM, tk),
                   _kv_tile_spec(n_sub, tk, HEAD_DIM), _kv_tile_spec(n_sub_v, HEAD_DIM + ONES_ROWS, tk_v)],
        out_shape=[jax.ShapeDtypeStruct((N_HEADS, t // tm, HEAD_DIM, tm), BF16), kv(tk, HEAD_DIM), kv(HEAD_DIM, tk),
                   kv(tk, HEAD_DIM),
                   jax.ShapeDtypeStruct((N_KV_HEADS, t // tk_v, HEAD_DIM + ONES_ROWS, tk_v), BF16)],
        compiler_params=_params(1),
    )(qkv, gq_w, gk_w, cos_w, sin_w, mean_q, mean_k)


def qk_bwd(dq_rot, dk_rot, dv, qkv, gq_w, gk_w, cos_w, sin_w, mean_q, mean_k, tm):
    t = qkv.shape[0]
    tk = dk_rot.shape[-1]
    n_sub = tm // tk

    def token_major(ref):
        return jnp.concatenate([ref[:, c].reshape(D_KV, tk).T for c in range(n_sub)], axis=0)

    def branch(raw, d_rot, gain, mean_mat, cos, sin, scale):
        r = lax.rsqrt(_dot_split(raw * raw, mean_mat) + EPS)
        n = raw * r
        dy = (d_rot * cos - _pair_swap(d_rot) * sin) * scale
        dn = dy * gain
        return r * (dn - n * _dot_split(dn * n, mean_mat)), dy * n

    def body(dq_ref, dk_ref, dv_ref, p_ref, gq_ref, gk_ref, cos_ref, sin_ref, mq_ref, mk_ref,
             dp_ref, dgq_ref, dgk_ref):
        cos2, sin2 = cos_ref[...], sin_ref[...]
        cos8, sin8 = _tile_lanes(cos2, D_ATTN // LANES), _tile_lanes(sin2, D_ATTN // LANES)
        dq, dgq = branch(p_ref[:, :D_ATTN], _from_head_tiles(dq_ref[...]), gq_ref[...], mq_ref[...], cos8, sin8,
                         HEAD_DIM ** -0.5)
        dk, dgk = branch(p_ref[:, D_ATTN:D_ATTN + D_KV], token_major(dk_ref), gk_ref[...], mk_ref[...], cos2, sin2, 1.0)
        dp_ref[...] = jnp.concatenate([dq, dk, token_major(dv_ref)], axis=-1).astype(BF16)

        @pl.when(pl.program_id(0) == 0)
        def _():
            dgq_ref[...] = jnp.zeros_like(dgq_ref)
            dgk_ref[...] = jnp.zeros_like(dgk_ref)

        dgq_ref[...] += _colsum(dgq)
        dgk_ref[...] += _colsum(dgk)

    return pl.pallas_call(
        functools.partial(body), name="qk_bwd", grid=(t // tm,),
        in_specs=[_head_tile_spec(tm, HEAD_DIM), _kv_tile_spec(n_sub, HEAD_DIM, tk),
                  _kv_tile_spec(n_sub, HEAD_DIM, tk), pl.BlockSpec((tm, D_QKV), lambda i: (i, 0)),
                  _full((1, D_ATTN)), _full((1, D_KV)),
                  pl.BlockSpec((tm, LANES), lambda i: (i, 0)), pl.BlockSpec((tm, LANES), lambda i: (i, 0)),
                  _full((D_ATTN, D_ATTN)), _full((D_KV, D_KV))],
        out_specs=[pl.BlockSpec((tm, D_QKV), lambda i: (i, 0)), _full((1, D_ATTN)), _full((1, D_KV))],
        out_shape=[jax.ShapeDtypeStruct((t, D_QKV), BF16), jax.ShapeDtypeStruct((1, D_ATTN), F32),
                   jax.ShapeDtypeStruct((1, D_KV), F32)],
        compiler_params=_params(1),
    )(dq_rot, dk_rot, dv, qkv, gq_w, gk_w, cos_w, sin_w, mean_q, mean_k)


def attention_fwd(q_t, k, v_t, comm=None):
    _, nq, _, tq = q_t.shape
    _, nk, tk, _ = k.shape

    def body(q_ref, k_ref, v_ref, o_ref, lse_ref, s_scr, p_scr):
        q = q_ref[...]
        s_scr[0] = _dot(k_ref[0], q)
        p_scr[1] = jnp.zeros((tk, tq), BF16)

        def step(j, slot, carry):
            m, acc = carry
            s = s_scr[slot]
            pv = _dot(v_ref[jnp.maximum(j - 1, 0)], p_scr[1 - slot])
            s_scr[1 - slot] = _dot(k_ref[jnp.minimum(j + 1, nk - 1)], q)
            m_new = jnp.maximum(m, jnp.max(s, axis=0, keepdims=True))
            p_scr[slot] = jnp.exp2(s - m_new).astype(BF16)
            return m_new, jnp.exp2(m - m_new) * (acc + pv)

        m, acc = _loop_pairs(nk, step, (jnp.full((1, tq), -1e30, F32), jnp.zeros((HEAD_DIM + ONES_ROWS, tq), F32)))
        acc = acc + _dot(v_ref[nk - 1], p_scr[(nk - 1) % 2])
        l = acc[HEAD_DIM:HEAD_DIM + 1]
        o_ref[...] = acc[:HEAD_DIM] / l
        lse_ref[...] = m + jnp.log2(l)

    return _pallas(
        comm, body, name="attention_fwd", grid=(N_HEADS, nq),
        in_specs=[pl.BlockSpec((None, None, HEAD_DIM, tq), lambda h, i: (h, i, 0, 0)),
                  pl.BlockSpec((None, nk, tk, HEAD_DIM), lambda h, i: (h // KV_GROUP, 0, 0, 0)),
                  pl.BlockSpec((None, nk, HEAD_DIM + ONES_ROWS, tk), lambda h, i: (h // KV_GROUP, 0, 0, 0))],
        out_specs=[pl.BlockSpec((None, None, HEAD_DIM, tq), lambda h, i: (h, i, 0, 0)),
                   pl.BlockSpec((None, None, 1, tq), lambda h, i: (h, i, 0, 0))],
        out_shape=[jax.ShapeDtypeStruct((N_HEADS, nq, HEAD_DIM, tq), F32),
                   jax.ShapeDtypeStruct((N_HEADS, nq, 1, tq), F32)],
        scratch_shapes=[pltpu.VMEM((2, tk, tq), F32), pltpu.VMEM((2, tk, tq), BF16)],
        args=(q_t, k, v_t))


def attention_bwd(q_t, do_t, lse, delta, k, k_t, v, comm=None):
    _, nq, _, tq = q_t.shape
    _, nk, tk, _ = k.shape

    def body(q_ref, do_ref, lse_ref, delta_ref, k_ref, kt_ref, v_ref, dq_ref, dk_ref, dv_ref,
             s_scr, dp_scr, p_scr, ds_scr):
        @pl.when(pl.program_id(1) == 0)
        def _():
            dq_ref[...] = jnp.zeros_like(dq_ref)

        kk, kt, vv = k_ref[...], kt_ref[...], v_ref[...]
        n = KV_GROUP * nq
        s_scr[0] = _dot(kk, q_ref[0, 0])
        dp_scr[0] = _dot(vv, do_ref[0, 0])
        p_scr[1] = jnp.zeros((tk, tq), BF16)
        ds_scr[1] = jnp.zeros((tk, tq), BF16)

        def products(t, slot, dk, dv):
            h, i = t // nq, t % nq
            ds = ds_scr[slot]
            dq_ref[h, i] += _dot(kt, ds)
            return dk + _dot_nt(q_ref[h, i], ds), dv + _dot_nt(do_ref[h, i], p_scr[slot])

        def step(t, slot, carry):
            s, dp = s_scr[slot], dp_scr[slot]
            dk, dv = products(jnp.maximum(t - 1, 0), 1 - slot, *carry)
            nxt = jnp.minimum(t + 1, n - 1)
            s_scr[1 - slot] = _dot(kk, q_ref[nxt // nq, nxt % nq])
            dp_scr[1 - slot] = _dot(vv, do_ref[nxt // nq, nxt % nq])
            h, i = t // nq, t % nq
            p = jnp.exp2(s - lse_ref[h, i])
            p_scr[slot] = p.astype(BF16)
            ds_scr[slot] = (p * (dp - delta_ref[h, i])).astype(BF16)
            return dk, dv

        zero = jnp.zeros((HEAD_DIM, tk), F32)
        dk, dv = products(n - 1, (n - 1) % 2, *_loop_pairs(n, step, (zero, zero)))
        dk_ref[...] = dk * (1.0 / LOG2_E)
        dv_ref[...] = dv

    group = lambda g, j: (g, 0, 0, 0)
    tile = lambda g, j: (g, j, 0, 0)
    return _pallas(
        comm, body, name="attention_bwd", grid=(N_KV_HEADS, nk),
        in_specs=[pl.BlockSpec((KV_GROUP, nq, HEAD_DIM, tq), group),
                  pl.BlockSpec((KV_GROUP, nq, HEAD_DIM, tq), group),
                  pl.BlockSpec((KV_GROUP, nq, 1, tq), group),
                  pl.BlockSpec((KV_GROUP, nq, 1, tq), group),
                  pl.BlockSpec((None, None, tk, HEAD_DIM), tile),
                  pl.BlockSpec((None, None, HEAD_DIM, tk), tile),
                  pl.BlockSpec((None, None, tk, HEAD_DIM), tile)],
        out_specs=[pl.BlockSpec((KV_GROUP, nq, HEAD_DIM, tq), group),
                   pl.BlockSpec((None, None, HEAD_DIM, tk), tile),
                   pl.BlockSpec((None, None, HEAD_DIM, tk), tile)],
        out_shape=[jax.ShapeDtypeStruct((N_HEADS, nq, HEAD_DIM, tq), F32),
                   jax.ShapeDtypeStruct((N_KV_HEADS, nk, HEAD_DIM, tk), F32),
                   jax.ShapeDtypeStruct((N_KV_HEADS, nk, HEAD_DIM, tk), F32)],
        scratch_shapes=[pltpu.VMEM((2, tk, tq), F32), pltpu.VMEM((2, tk, tq), F32),
                        pltpu.VMEM((2, tk, tq), BF16), pltpu.VMEM((2, tk, tq), BF16)],
        args=(q_t, do_t, lse, delta, k, k_t, v))


def _group_select(parts):
    lane_group = lax.broadcasted_iota(jnp.int32, parts[0].shape, 1) // SGU_GROUP_DIM
    out = parts[0]
    for g in range(1, N_SGU_GROUPS):
        out = jnp.where(lane_group == g, parts[g], out)
    return out


def _gate_forward(z, g_sgu, ws_ref, bias):
    gz, th = _gelu(z)
    u, vv = gz[:, :D_SGU], gz[:, D_SGU:]
    rv = _rstd(vv)
    nv = vv * rv
    vn = (nv * g_sgu).astype(BF16)
    fs = []
    for c in range(z.shape[0] // CHUNK):
        vc = vn[c * CHUNK:(c + 1) * CHUNK]
        fs.append(_group_select([_dot(ws_ref[g], vc) for g in range(N_SGU_GROUPS)]) + bias)
    f = jnp.concatenate(fs, axis=0) if len(fs) > 1 else fs[0]
    return th, u, rv, nv, vn, f


def mix_out(z, o, x, g_sgu, g_ao, g_so, ws, bias, w_out, tm):
    t = x.shape[0]

    def body(z_ref, o_ref, x_ref, gs_ref, gao_ref, gso_ref, ws_ref, bias_ref, wout_ref, x2_ref, mixed_ref):
        _, u, _, _, _, f = _gate_forward(z_ref[...], gs_ref[...], ws_ref, bias_ref[...])
        sgu = u * f
        oo = _from_head_tiles(o_ref[...])
        mixed = jnp.concatenate([oo * _rstd(oo) * gao_ref[...], sgu * _rstd(sgu) * gso_ref[...]], axis=-1).astype(BF16)
        mixed_ref[...] = mixed
        x2_ref[...] = x_ref[...] + _dot(mixed, wout_ref[...])

    row = lambda n: pl.BlockSpec((tm, n), lambda i: (i, 0))
    return pl.pallas_call(
        functools.partial(body), name="mix_out", grid=(t // tm,),
        in_specs=[row(2 * D_SGU), _head_tile_spec(tm, HEAD_DIM), row(D_MODEL), _full((1, D_SGU)), _full((1, D_ATTN)),
                  _full((1, D_SGU)),
                  _full((N_SGU_GROUPS, CHUNK, CHUNK)), _full((CHUNK, D_SGU)), _full((D_MODEL, D_MODEL))],
        out_specs=[row(D_MODEL), row(D_MODEL)],
        out_shape=[jax.ShapeDtypeStruct((t, D_MODEL), F32), jax.ShapeDtypeStruct((t, D_MODEL), BF16)],
        compiler_params=_params(1),
    )(z, o, x, g_sgu, g_ao, g_so, ws, bias, w_out)


def mix_bwd(dx2, z, o, g_sgu, g_ao, g_so, ws, ws_t, bias, w_out, group_ind, tm):
    t = dx2.shape[0]
    n_tiles = t // tm

    def body(dx_ref, z_ref, o_ref, gs_ref, gao_ref, gso_ref, ws_ref, wst_ref, bias_ref, wout_ref, ind_ref,
             do_ref, delta_ref, dz_ref, dg_ref, dws_ref, dbs_ref, df_sum):
        step = pl.program_id(0)

        @pl.when(step == 0)
        def _():
            dg_ref[...] = jnp.zeros_like(dg_ref)
            dws_ref[...] = jnp.zeros_like(dws_ref)
            df_sum[...] = jnp.zeros_like(df_sum)

        z = z_ref[...]
        th, u, rv, nv, vn, f = _gate_forward(z, gs_ref[...], ws_ref, bias_ref[...])
        dmixed = _dot_nt(dx_ref[...].astype(BF16), wout_ref[...])
        o_tiles = o_ref[...]
        oo = _from_head_tiles(o_tiles)
        ro = _rstd(oo)
        d_o, dgao = _rms_bwd(dmixed[:, :D_ATTN], oo * ro, ro, gao_ref[...])
        do_tiles = _to_head_tiles(d_o)
        do_ref[...] = do_tiles.astype(BF16)
        delta_ref[...] = jnp.sum(do_tiles * o_tiles, axis=1, keepdims=True)
        sgu = u * f
        rs = _rstd(sgu)
        dsgu, dgso = _rms_bwd(dmixed[:, D_ATTN:], sgu * rs, rs, gso_ref[...])
        du = dsgu * f
        df = dsgu * u
        lane_group = lax.broadcasted_iota(jnp.int32, (CHUNK, D_SGU), 1) // SGU_GROUP_DIM
        dvns = []
        df_acc = jnp.zeros((CHUNK, D_SGU), F32)
        for c in range(tm // CHUNK):
            dfc32 = df[c * CHUNK:(c + 1) * CHUNK]
            dfc = dfc32.astype(BF16)
            vc = vn[c * CHUNK:(c + 1) * CHUNK]
            dvns.append(_group_select([_dot(wst_ref[g], dfc) for g in range(N_SGU_GROUPS)]))
            for g in range(N_SGU_GROUPS):
                dws_ref[g] += _dot_nt(jnp.where(lane_group == g, dfc, jnp.zeros_like(dfc)), vc)
            df_acc = df_acc + dfc32
        df_sum[...] += df_acc
        dvn = jnp.concatenate(dvns, axis=0) if len(dvns) > 1 else dvns[0]
        dvv, dgs = _rms_bwd(dvn, nv, rv, gs_ref[...])
        dz_ref[...] = (jnp.concatenate([du, dvv], axis=-1) * _gelu_grad(z, th)).astype(BF16)
        dg_ref[0:1, :] += _colsum(dgao)
        dg_ref[1:2, :] += _colsum(dgso)
        dg_ref[2:3, :] += _colsum(dgs)

        @pl.when(step == n_tiles - 1)
        def _():
            dbs_ref[...] = _dot_f32(df_sum[...], ind_ref[...])

    row = lambda n: pl.BlockSpec((tm, n), lambda i: (i, 0))
    return pl.pallas_call(
        functools.partial(body), name="mix_bwd", grid=(n_tiles,),
        in_specs=[row(D_MODEL), row(2 * D_SGU), _head_tile_spec(tm, HEAD_DIM), _full((1, D_SGU)), _full((1, D_ATTN)),
                  _full((1, D_SGU)),
                  _full((N_SGU_GROUPS, CHUNK, CHUNK)), _full((N_SGU_GROUPS, CHUNK, CHUNK)), _full((CHUNK, D_SGU)),
                  _full((D_MODEL, D_MODEL)), _full((D_SGU, LANES))],
        out_specs=[_head_tile_spec(tm, HEAD_DIM), _head_tile_spec(tm, 1), row(2 * D_SGU), _full((8, D_SGU)),
                   _full((N_SGU_GROUPS, CHUNK, CHUNK)), _full((CHUNK, LANES))],
        out_shape=[jax.ShapeDtypeStruct((N_HEADS, n_tiles, HEAD_DIM, tm), BF16),
                   jax.ShapeDtypeStruct((N_HEADS, n_tiles, 1, tm), F32), jax.ShapeDtypeStruct((t, 2 * D_SGU), BF16),
                   jax.ShapeDtypeStruct((8, D_SGU), F32),
                   jax.ShapeDtypeStruct((N_SGU_GROUPS, CHUNK, CHUNK), F32),
                   jax.ShapeDtypeStruct((CHUNK, LANES), F32)],
        scratch_shapes=[pltpu.VMEM((CHUNK, D_SGU), F32)],
        compiler_params=_params(1),
    )(dx2, z, o, g_sgu, g_ao, g_so, ws, ws_t, bias, w_out, group_ind)


def ffn_down_loss(act, wd, x, g, target, tm):
    t = x.shape[0]

    def body(act_ref, wd_ref, x_ref, g_ref, t_ref, loss_ref, dx_ref, dg_ref):
        @pl.when(pl.program_id(0) == 0)
        def _():
            loss_ref[...] = jnp.zeros_like(loss_ref)
            dg_ref[...] = jnp.zeros_like(dg_ref)

        xx = x_ref[...] + 0.5 * _dot(act_ref[...], wd_ref[...])
        r = _rstd(xx)
        n = xx * r
        err = n * g_ref[...] - t_ref[...]
        per_token = jnp.mean(err * err, axis=-1, keepdims=True)
        loss_ref[...] += 0.5 * jnp.sum(per_token, axis=0, keepdims=True)
        dx, dg_rows = _rms_bwd(err * (1.0 / D_MODEL), n, r, g_ref[...])
        dx_ref[...] = dx
        dg_ref[...] += _colsum(dg_rows)

    row = pl.BlockSpec((tm, D_MODEL), lambda i: (i, 0))
    return pl.pallas_call(
        functools.partial(body), name="ffn_down_loss", grid=(t // tm,),
        in_specs=[pl.BlockSpec((tm, D_FF), lambda i: (i, 0)), _full((D_FF, D_MODEL)), row, _full((1, D_MODEL)), row],
        out_specs=[_full((1, LANES)), row, _full((1, D_MODEL))],
        out_shape=[jax.ShapeDtypeStruct((1, LANES), F32), jax.ShapeDtypeStruct((t, D_MODEL), F32),
                   jax.ShapeDtypeStruct((1, D_MODEL), F32)],
        compiler_params=_params(1),
    )(act, wd, x, g, target)


def _rope_tables(t):
    rows = t // GRID_W
    row_idx = jnp.repeat(jnp.arange(rows, dtype=F32), GRID_W)
    col_idx = jnp.tile(jnp.arange(GRID_W, dtype=F32), rows)
    axis_dim = HEAD_DIM // 2
    inv = 1.0 / (ROPE_THETA ** (jnp.arange(0, axis_dim, 2, dtype=F32) / axis_dim))
    ang = jnp.concatenate([row_idx[:, None] * inv, col_idx[:, None] * inv], axis=-1)
    cos = jnp.repeat(jnp.cos(ang), 2, axis=-1)
    sin = jnp.repeat(jnp.sin(ang), 2, axis=-1) * jnp.tile(jnp.array([-1.0, 1.0], F32), HEAD_DIM // 2)
    return jnp.tile(cos, (1, LANES // HEAD_DIM)), jnp.tile(sin, (1, LANES // HEAD_DIM))


def _heads_to_tiles_t(a, n_heads, tile):
    t = a.shape[0]
    return a.reshape(t // tile, tile, n_heads, HEAD_DIM).transpose(2, 0, 3, 1)


def _heads_to_tiles(a, n_heads, tile):
    t = a.shape[0]
    return a.reshape(t // tile, tile, n_heads, HEAD_DIM).transpose(2, 0, 1, 3)


def _tiles_t_to_heads(a):
    h, n, _, tile = a.shape
    return a.transpose(1, 3, 0, 2).reshape(n * tile, h * HEAD_DIM)


def kernel(x, g_ffn1, w1_gate, w1_up, w1_down, g_mix, w_in, g_q, g_k, g_sgu, w_s, b_s, g_attn_out, g_sgu_out, w_out, g_ffn2, w2_gate, w2_up, w2_down, g_final, loss_target, m_g_ffn1, m_w1_gate, m_w1_up, m_w1_down, m_g_mix, m_w_in, m_g_q, m_g_k, m_g_sgu, m_w_s, m_b_s, m_g_attn_out, m_g_sgu_out, m_w_out, m_g_ffn2, m_w2_gate, m_w2_up, m_w2_down, m_g_final, v_g_ffn1, v_w1_gate, v_w1_up, v_w1_down, v_g_mix, v_w_in, v_g_q, v_g_k, v_g_sgu, v_w_s, v_b_s, v_g_attn_out, v_g_sgu_out, v_w_out, v_g_ffn2, v_w2_gate, v_w2_up, v_w2_down, v_g_final):
    weights = dict(g_ffn1=g_ffn1, w1_gate=w1_gate, w1_up=w1_up, w1_down=w1_down, g_mix=g_mix, w_in=w_in, g_q=g_q,
                   g_k=g_k, g_sgu=g_sgu, w_s=w_s, b_s=b_s, g_attn_out=g_attn_out, g_sgu_out=g_sgu_out, w_out=w_out,
                   g_ffn2=g_ffn2, w2_gate=w2_gate, w2_up=w2_up, w2_down=w2_down, g_final=g_final)
    m_in = dict(g_ffn1=m_g_ffn1, w1_gate=m_w1_gate, w1_up=m_w1_up, w1_down=m_w1_down, g_mix=m_g_mix, w_in=m_w_in,
                g_q=m_g_q, g_k=m_g_k, g_sgu=m_g_sgu, w_s=m_w_s, b_s=m_b_s, g_attn_out=m_g_attn_out,
                g_sgu_out=m_g_sgu_out, w_out=m_w_out, g_ffn2=m_g_ffn2, w2_gate=m_w2_gate, w2_up=m_w2_up,
                w2_down=m_w2_down, g_final=m_g_final)
    v_in = dict(g_ffn1=v_g_ffn1, w1_gate=v_w1_gate, w1_up=v_w1_up, w1_down=v_w1_down, g_mix=v_g_mix, w_in=v_w_in,
                g_q=v_g_q, g_k=v_g_k, g_sgu=v_g_sgu, w_s=v_w_s, b_s=v_b_s, g_attn_out=v_g_attn_out,
                g_sgu_out=v_g_sgu_out, w_out=v_w_out, g_ffn2=v_g_ffn2, w2_gate=v_w2_gate, w2_up=v_w2_up,
                w2_down=v_w2_down, g_final=v_g_final)
    names = list(weights)

    t = x.shape[1]
    x0 = x[0]
    target = loss_target[0]
    tm = min(256, t)
    tm_ff = min(256, t)
    tn_ff = 256
    tq = min(512, t)
    tk = min(256, t)
    tk_fwd = min(512, t)
    tk_w = min(2048, t)

    def shard_rows(name):
        w = weights[name][0]
        return (w.T if name in TRANSPOSED else w).astype(BF16)

    rows_of = dict(SHARD_ROWS)
    full = {}

    def packed(group):
        return jnp.concatenate([shard_rows(n) for n in group], axis=0), [rows_of[n] for n in group]

    def gather_of(group):
        return gather_exchange(*packed(group))

    def take(group, gathered):
        for n, g in zip(group, gathered):
            full[n] = g.reshape(N_DEV * rows_of[n], D_MODEL)

    first, second, third = ("w1_gate", "w1_up"), ("w1_down", "w_in", "w_out"), ("w2_gate", "w2_up", "w2_down")
    take(first, gather_two_level(*packed(first), "gather_first"))

    (h1, a1, b1, act1), gathered = ffn_up(x0, g_ffn1, full["w1_gate"], full["w1_up"], tm_ff, tn_ff, gather_of(second))
    take(second, gathered)
    w_in_t = full["w_in"]
    w_qkv_t, w_z_t = w_in_t[:D_QKV], w_in_t[D_QKV:]
    x1 = ffn_down(act1, full["w1_down"], x0, tm)

    qkv, z, h2 = input_projection(x1, g_mix, w_qkv_t, w_z_t, tm)
    cos_w, sin_w = _rope_tables(t)
    gq_w = jnp.tile(g_q, (1, N_HEADS))
    gk_w = jnp.tile(g_k, (1, N_KV_HEADS))
    mean_q, mean_k = _head_mean_matrix(D_ATTN).astype(BF16), _head_mean_matrix(D_KV).astype(BF16)
    q_t, k_tiles, kt_tiles, v_tiles, vt_tiles = qk_prep(qkv, gq_w, gk_w, cos_w, sin_w, mean_q, mean_k, tq, tk, tk_fwd)
    k_tiles_fwd = k_tiles.reshape(N_KV_HEADS, t // tk_fwd, tk_fwd, HEAD_DIM)
    (o_t, lse), gathered = attention_fwd(q_t, k_tiles_fwd, vt_tiles, gather_of(third))
    take(third, gathered)

    ws_b = w_s[0].astype(BF16)
    ws_tb = jnp.swapaxes(w_s[0], 1, 2).astype(BF16)
    bias = jnp.repeat(b_s[0].T, SGU_GROUP_DIM, axis=1)
    x2, mixed = mix_out(z, o_t, x1, g_sgu, g_attn_out, g_sgu_out, ws_b, bias, full["w_out"], tq)

    (h3, a2, b2, act2), _ = ffn_up(x2, g_ffn2, full["w2_gate"], full["w2_up"], tm_ff, tn_ff)

    loss_part, dx3, dg_final = ffn_down_loss(act2, full["w2_down"], x2, g_final, target, tm)

    tmm = D_FF // 2
    (da2, db2), _ = ffn_bwd_act(dx3, full["w2_down"], a2, b2, tm_ff, tn_ff)
    (dx2, dg_ffn2), _ = norm_bwd_matmul(da2, full["w2_gate"], db2, full["w2_up"], x2, g_ffn2, dx3, tm)
    dwg2, dwu2 = matmul_tn(da2, h3, 1.0, tmm, tk_w), matmul_tn(db2, h3, 1.0, tmm, tk_w)
    dwd2 = matmul_tn(act2, dx3, 0.5, tmm, tk_w)

    group_ind = (jnp.arange(D_SGU)[:, None] // SGU_GROUP_DIM == jnp.arange(LANES)[None, :]).astype(F32)
    do_t, delta, dz, dg_mixrow, dws, dbs = mix_bwd(dx2, z, o_t, g_sgu, g_attn_out, g_sgu_out, ws_b, ws_tb, bias,
                                                   full["w_out"], group_ind, tq)
    dw_out = matmul_tn(mixed, dx2, 1.0, D_MODEL // 2, tk_w)

    group_a = ("w2_gate", "w2_up", "w2_down", "w_out")
    (dq_t, dk_t, dv_t), (parts_a,) = attention_bwd(q_t, do_t, lse, delta, k_tiles, kt_tiles, v_tiles,
                                                   scatter_exchange([dwg2, dwu2, dwd2, dw_out]))
    dqkv, dgq_w, dgk_w = qk_bwd(dq_t, dk_t, dv_t, qkv, gq_w, gk_w, cos_w, sin_w, mean_q, mean_k, tq)

    def pack_small(arrays):
        pieces = []
        for a in arrays:
            flat = a.reshape(-1)
            pieces.append(jnp.pad(flat, (0, (-flat.shape[0]) % (8 * LANES))).reshape(-1, LANES))
        return jnp.concatenate(pieces, axis=0), [p.shape[0] for p in pieces]

    early = dict(g_ffn2=dg_ffn2, g_final=dg_final, g_q=dgq_w.reshape(N_HEADS, HEAD_DIM).sum(0),
                 g_k=dgk_w.reshape(N_KV_HEADS, HEAD_DIM).sum(0), g_attn_out=dg_mixrow[0], g_sgu_out=dg_mixrow[1],
                 g_sgu=dg_mixrow[2], w_s=dws, b_s=dbs[:, :N_SGU_GROUPS].T)
    early_pack, early_rows = pack_small(list(early.values()))
    (dx1, dg_mix), (early_parts,) = norm_bwd_matmul(dqkv, w_qkv_t, dz, w_z_t, x1, g_mix, dx2, tm,
                                                    gather_exchange(early_pack, [early_pack.shape[0]]))
    dw_in = jnp.concatenate([matmul_tn(dqkv, h2, 1.0, D_QKV // 2, tk_w), matmul_tn(dz, h2, 1.0, D_SGU, tk_w)], axis=0)

    dwd1 = matmul_tn(act1, dx1, 0.5, tmm, tk_w)
    group_b = ("w_in", "w1_down")
    (da1, db1), (parts_b,) = ffn_bwd_act(dx1, full["w1_down"], a1, b1, tm_ff, tn_ff, scatter_exchange([dw_in, dwd1]))
    dwg1, dwu1 = matmul_tn(da1, h1, 1.0, tmm, tk_w), matmul_tn(db1, h1, 1.0, tmm, tk_w)
    group_c = ("w1_gate", "w1_up")
    (dx0, dg_ffn1), (parts_c,) = norm_bwd_matmul(da1, full["w1_gate"], db1, full["w1_up"], x0, g_ffn1, dx1, tm,
                                                 scatter_exchange([dwg1, dwu1]))

    late = dict(g_mix=dg_mix, g_ffn1=dg_ffn1, loss=loss_part)
    late_pack, late_rows = pack_small(list(late.values()))
    (late_parts,) = run_exchange(gather_exchange(late_pack, [late_pack.shape[0]]), "gather_late_small_grads")
    small_sums = {}
    for entries, rows, parts in ((early, early_rows, early_parts), (late, late_rows, late_parts)):
        summed = sum_parts(parts, parts.shape[1])
        off = 0
        for n, r in zip(entries, rows):
            small_sums[n] = summed[off:off + r]
            off += r
    loss = small_sums.pop("loss")[0, 0]

    grads = {}
    for group, parts in ((group_a, parts_a), (group_b, parts_b), (group_c, parts_c)):
        summed = sum_parts(parts, 32)
        off = 0
        for n in group:
            gsh = summed[off:off + rows_of[n]]
            grads[n] = (gsh.T if n in TRANSPOSED else gsh)[None]
            off += rows_of[n]
    for n, summed in small_sums.items():
        grads[n] = summed.reshape(-1)[:weights[n].size].reshape(weights[n].shape)

    delta_w, new_m, new_v = {}, {}, {}
    for n in names:
        shape = weights[n].shape
        as2d = (lambda a: a.reshape(-1, shape[-1]))
        d, m2, v2 = adamw(as2d(weights[n]), as2d(grads[n]), as2d(m_in[n]), as2d(v_in[n]))
        delta_w[n], new_m[n], new_v[n] = d.reshape(shape), m2.reshape(shape), v2.reshape(shape)

    return (loss, dx0[None], *[grads[n] for n in names], *[delta_w[n] for n in names],
            *[new_m[n] for n in names], *[new_v[n] for n in names])
```

```python
import functools
import math

import jax
import jax.numpy as jnp
from jax import lax
from jax.experimental import pallas as pl
from jax.experimental.pallas import tpu as pltpu

F32 = jnp.float32
BF16 = jnp.bfloat16

D_MODEL = 1024
D_FF = 2816
N_HEADS = 8
HEAD_DIM = 64
N_KV_HEADS = 2
KV_GROUP = N_HEADS // N_KV_HEADS
D_ATTN = N_HEADS * HEAD_DIM
D_KV = N_KV_HEADS * HEAD_DIM
D_QKV = D_ATTN + 2 * D_KV
N_SGU_GROUPS = 8
SGU_GROUP_DIM = 64
D_SGU = N_SGU_GROUPS * SGU_GROUP_DIM
CHUNK = 128
GRID_W = 64
ROPE_THETA = 10000.0
EPS = 1e-6
N_DEV = 8
LANES = 128

ONES_ROWS = 16
SAFE_SCORE_BOUND = 60.0
LOG2_E = math.log2(math.e)
Q_SCALE = HEAD_DIM ** -0.5 * LOG2_E

ADAM_LR = 0.001
ADAM_B1 = 0.9
ADAM_B2 = 0.999
ADAM_EPS = 1e-08
ADAM_WD = 0.01
ADAM_STEP = 10

MESH_AXES = ("x", "y", "c")
MESH_IDS = pl.DeviceIdType.MESH

VMEM_LIMIT = 56 * 1024 * 1024

SHARD_ROWS = (("w1_gate", D_FF // N_DEV), ("w1_up", D_FF // N_DEV), ("w1_down", D_FF // N_DEV),
              ("w_in", (D_QKV + 2 * D_SGU) // N_DEV), ("w_out", D_MODEL // N_DEV),
              ("w2_gate", D_FF // N_DEV), ("w2_up", D_FF // N_DEV), ("w2_down", D_FF // N_DEV))
PACK_ROWS = sum(r for _, r in SHARD_ROWS)
TRANSPOSED = ("w1_gate", "w1_up", "w_in", "w2_gate", "w2_up")


def _params(n_grid):
    return pltpu.CompilerParams(dimension_semantics=("arbitrary",) * n_grid, vmem_limit_bytes=VMEM_LIMIT)


def _dot(a, b):
    return jnp.dot(a, b, preferred_element_type=F32)


def _dot_nt(a, b):
    return lax.dot_general(a, b, (((1,), (1,)), ((), ())), preferred_element_type=F32)


def _dot_tn(a, b):
    return lax.dot_general(a, b, (((0,), (0,)), ((), ())), preferred_element_type=F32)


def _dot_f32(a, b):
    return jnp.dot(a, b, preferred_element_type=F32, precision=lax.Precision.HIGHEST)


def _dot_split(a, b):
    hi = a.astype(BF16)
    lo = (a - hi.astype(F32)).astype(BF16)
    return _dot(hi, b) + _dot(lo, b)


def _rstd(x):
    return lax.rsqrt(jnp.mean(x * x, axis=-1, keepdims=True) + EPS)


def _rms_bwd(dy, n, r, g):
    dn = dy * g
    return r * (dn - n * jnp.mean(dn * n, axis=-1, keepdims=True)), dy * n


def _colsum(a):
    return jnp.sum(a, axis=0, keepdims=True)


_GELU_C = math.sqrt(2.0 / math.pi)


def _gelu(x):
    t = jnp.tanh(_GELU_C * (x + 0.044715 * (x * x * x)))
    return x * (0.5 * (1.0 + t)), t


def _gelu_grad(x, t):
    return 0.5 * (1.0 + t) + 0.5 * x * (1.0 - t * t) * (_GELU_C * (1.0 + 3 * 0.044715 * x * x))


def _pair_swap(a):
    w = a.shape[-1]
    lane = lax.broadcasted_iota(jnp.int32, a.shape, a.ndim - 1)
    return jnp.where(lane % 2 == 0, pltpu.roll(a, w - 1, a.ndim - 1), pltpu.roll(a, 1, a.ndim - 1))


def _tile_lanes(a, reps):
    return jnp.concatenate([a] * reps, axis=-1) if reps > 1 else a


def _loop_pairs(n, step, carry):
    assert n % 2 == 0, n

    def pair(jj, c):
        return step(2 * jj + 1, 1, step(2 * jj, 0, c))

    return lax.fori_loop(0, n // 2, pair, carry)


def _full(shape):
    nd = len(shape)
    return pl.BlockSpec(shape, lambda *_: (0,) * nd)


def _mesh_pos():
    return lax.axis_index("x"), lax.axis_index("y"), lax.axis_index("c")


def _peer(pos, d):
    x, y, c = pos
    px = 1 - x if d & 4 else x
    py = 1 - y if d & 2 else y
    pc = 1 - c if d & 1 else c
    return (px, py, pc), 4 * px + 2 * py + pc


class _Exchange:
    def __init__(self, operands, out_shape, n_local, plan):
        self.operands = list(operands)
        self.out_shape = list(out_shape)
        self.sem_shapes = [pltpu.SemaphoreType.DMA((N_DEV - 1,)), pltpu.SemaphoreType.DMA((N_DEV - 1,)),
                           pltpu.SemaphoreType.DMA((n_local,))]
        self._plan = plan

    def _copies(self, in_refs, out_refs):
        pos = _mesh_pos()
        return pos, self._plan(4 * pos[0] + 2 * pos[1] + pos[2], in_refs, out_refs)

    def start(self, in_refs, out_refs, sems):
        send_sems, recv_sems, local_sems = sems
        pos, (local, remote, _) = self._copies(in_refs, out_refs)
        for k, (src, dst) in enumerate(local):
            pltpu.make_async_copy(src, dst, local_sems.at[k]).start()
        for d in range(1, N_DEV):
            peer, peer_lin = _peer(pos, d)
            for src, dst in remote(peer_lin):
                pltpu.make_async_remote_copy(src_ref=src, dst_ref=dst, send_sem=send_sems.at[d - 1],
                                             recv_sem=recv_sems.at[d - 1], device_id=peer,
                                             device_id_type=MESH_IDS).start()

    def wait(self, in_refs, out_refs, sems):
        send_sems, recv_sems, local_sems = sems
        pos, (local, _, whole) = self._copies(in_refs, out_refs)
        for d in range(1, N_DEV):
            peer, peer_lin = _peer(pos, d)
            ref = whole(peer_lin)
            everything = pltpu.make_async_remote_copy(src_ref=ref, dst_ref=ref, send_sem=send_sems.at[d - 1],
                                                      recv_sem=recv_sems.at[d - 1], device_id=peer,
                                                      device_id_type=MESH_IDS)
            everything.wait_send()
            everything.wait_recv()
        for k, (src, dst) in enumerate(local):
            pltpu.make_async_copy(src, dst, local_sems.at[k]).wait()


def _offsets(rows):
    offs, o = [], 0
    for r in rows:
        offs.append(o)
        o += r
    return offs


def gather_exchange(src, rows):
    offs = _offsets(rows)

    def plan(me, in_refs, out_refs):
        pieces = [(in_refs[0].at[pl.ds(o, r)], out.at[me]) for o, r, out in zip(offs, rows, out_refs)]
        return pieces, (lambda peer_lin: pieces), (lambda peer_lin: in_refs[0])

    return _Exchange([src], [jax.ShapeDtypeStruct((N_DEV, r) + src.shape[1:], src.dtype) for r in rows],
                     len(rows), plan)


def scatter_exchange(grads):
    rows = [g.shape[0] // N_DEV for g in grads]
    offs = _offsets(rows)

    def plan(me, in_refs, out_refs):
        parts = out_refs[0]

        def slabs(owner):
            return [(g.at[pl.ds(pl.multiple_of(owner * r, 16), r)], parts.at[me, pl.ds(o, r)])
                    for g, o, r in zip(in_refs, offs, rows)]

        return slabs(me), slabs, (lambda peer_lin: parts.at[peer_lin])

    shape = jax.ShapeDtypeStruct((N_DEV, sum(rows)) + grads[0].shape[1:], grads[0].dtype)
    return _Exchange(grads, [shape], len(rows), plan)


def gather_two_level(src, rows, name):
    offs = _offsets(rows)
    n_p = len(rows)

    def body(src_ref, *refs):
        outs, (send_sems, recv_sems, local_sems) = refs[:n_p], refs[n_p:]
        x, y, c = _mesh_pos()
        me, sibling = (x, y, c), (x, y, 1 - c)
        chips = [(1 - x, y), (x, 1 - y), (1 - x, 1 - y)]

        def slab(w, dev):
            return outs[w].at[4 * dev[0] + 2 * dev[1] + dev[2]]

        def copy(w, k, block, to, from_src=False):
            return pltpu.make_async_remote_copy(
                src_ref=src_ref.at[pl.ds(offs[w], rows[w])] if from_src else slab(w, block), dst_ref=slab(w, block),
                send_sem=send_sems.at[w * 7 + k], recv_sem=recv_sems.at[w * 7 + k],
                device_id=to, device_id_type=MESH_IDS)

        mine = [pltpu.make_async_copy(src_ref.at[pl.ds(offs[w], rows[w])], slab(w, me), local_sems.at[w])
                for w in range(n_p)]
        for cp in mine:
            cp.start()
        first = []
        for w in range(n_p):
            first.append(copy(w, 0, me, sibling, True))
            first += [copy(w, 1 + j, me, (*chip, c), True) for j, chip in enumerate(chips)]
        for cp in first:
            cp.start()
        passed = []
        for j, chip in enumerate(chips):
            for w in range(n_p):
                copy(w, 1 + j, (*chip, c), me).wait_recv()
                cp = copy(w, 4 + j, (*chip, c), sibling)
                cp.start()
                passed.append(cp)
        for w in range(n_p):
            copy(w, 0, sibling, me).wait_recv()
            for j, chip in enumerate(chips):
                copy(w, 4 + j, (*chip, 1 - c), me).wait_recv()
        for cp in first + passed:
            cp.wait_send()
        for cp in mine:
            cp.wait()

    any_spec = pl.BlockSpec(memory_space=pl.ANY)
    return pl.pallas_call(
        functools.partial(body), name=name,
        out_shape=[jax.ShapeDtypeStruct((N_DEV, r) + src.shape[1:], src.dtype) for r in rows],
        in_specs=[any_spec], out_specs=[any_spec] * n_p,
        scratch_shapes=[pltpu.SemaphoreType.DMA((7 * n_p,)), pltpu.SemaphoreType.DMA((7 * n_p,)),
                        pltpu.SemaphoreType.DMA((n_p,))],
        compiler_params=pltpu.CompilerParams(has_side_effects=True),
    )(src)


def run_exchange(ex, name):
    n_in, n_out = len(ex.operands), len(ex.out_shape)

    def body(*refs):
        parts = refs[:n_in], refs[n_in:n_in + n_out], refs[n_in + n_out:]
        ex.start(*parts)
        ex.wait(*parts)

    any_spec = pl.BlockSpec(memory_space=pl.ANY)
    return pl.pallas_call(
        functools.partial(body), name=name, out_shape=ex.out_shape,
        in_specs=[any_spec] * n_in, out_specs=[any_spec] * n_out, scratch_shapes=ex.sem_shapes,
        compiler_params=pltpu.CompilerParams(has_side_effects=True),
    )(*ex.operands)


def _pallas(comm, body, *, name, grid, in_specs, out_specs, out_shape, args, scratch_shapes=()):
    params = _params(len(grid))
    if comm is None:
        res = pl.pallas_call(functools.partial(body), name=name, grid=grid, in_specs=list(in_specs),
                             out_specs=list(out_specs), out_shape=list(out_shape),
                             scratch_shapes=list(scratch_shapes), compiler_params=params)(*args)
        return list(res), []
    n_in, n_out, n_scr = len(in_specs), len(out_specs), len(scratch_shapes)
    c_in, c_out = len(comm.operands), len(comm.out_shape)

    def edge(last):
        conds = [pl.program_id(a) == (g - 1 if last else 0) for a, g in enumerate(grid)]
        return functools.reduce(jnp.logical_and, conds)

    def wrapped(*refs):
        refs = list(refs)
        ins, refs = refs[:n_in], refs[n_in:]
        cins, refs = refs[:c_in], refs[c_in:]
        outs, refs = refs[:n_out], refs[n_out:]
        couts, refs = refs[:c_out], refs[c_out:]
        scr, sems = refs[:n_scr], refs[n_scr:]

        @pl.when(edge(False))
        def _():
            comm.start(cins, couts, sems)

        body(*ins, *outs, *scr)

        @pl.when(edge(True))
        def _():
            comm.wait(cins, couts, sems)

    any_spec = pl.BlockSpec(memory_space=pl.ANY)
    res = pl.pallas_call(
        wrapped, name=name, grid=grid,
        in_specs=list(in_specs) + [any_spec] * c_in, out_specs=list(out_specs) + [any_spec] * c_out,
        out_shape=list(out_shape) + comm.out_shape, scratch_shapes=list(scratch_shapes) + comm.sem_shapes,
        compiler_params=pltpu.CompilerParams(dimension_semantics=("arbitrary",) * len(grid),
                                             vmem_limit_bytes=VMEM_LIMIT, has_side_effects=True),
    )(*args, *comm.operands)
    return res[:n_out], res[n_out:]


def sum_parts(parts, block_rows):
    n, rows, cols = parts.shape

    def body(p_ref, o_ref):
        acc = p_ref[0].astype(F32)
        for s in range(1, n):
            acc = acc + p_ref[s].astype(F32)
        o_ref[...] = acc

    return pl.pallas_call(
        functools.partial(body), name="sum_parts",
        grid=(rows // block_rows,),
        in_specs=[pl.BlockSpec((n, block_rows, cols), lambda i: (0, i, 0))],
        out_specs=pl.BlockSpec((block_rows, cols), lambda i: (i, 0)),
        out_shape=jax.ShapeDtypeStruct((rows, cols), F32),
        compiler_params=_params(1),
    )(parts)


def adamw(w, g, m, v):
    def body(w_ref, g_ref, m_ref, v_ref, d_ref, m_out, v_out):
        gg = g_ref[...]
        m2 = ADAM_B1 * m_ref[...] + (1.0 - ADAM_B1) * gg
        v2 = ADAM_B2 * v_ref[...] + (1.0 - ADAM_B2) * (gg * gg)
        m_hat = m2 / (1.0 - ADAM_B1 ** ADAM_STEP)
        v_hat = v2 / (1.0 - ADAM_B2 ** ADAM_STEP)
        d_ref[...] = -ADAM_LR * (m_hat / (jnp.sqrt(v_hat) + ADAM_EPS) + ADAM_WD * w_ref[...])
        m_out[...] = m2
        v_out[...] = v2

    spec = _full(w.shape)
    shape = jax.ShapeDtypeStruct(w.shape, F32)
    return pl.pallas_call(
        functools.partial(body), name="adamw",
        in_specs=[spec] * 4, out_specs=[spec] * 3, out_shape=[shape] * 3,
        compiler_params=pltpu.CompilerParams(vmem_limit_bytes=VMEM_LIMIT),
    )(w, g, m, v)


def ffn_up(x, g, wg_t, wu_t, tm, tn, comm=None):
    t = x.shape[0]

    def body(x_ref, g_ref, wg_ref, wu_ref, h_ref, silu_ref, dgate_ref, act_ref):
        xx = x_ref[...]
        h = ((xx * _rstd(xx)) * g_ref[...]).astype(BF16)
        h_ref[...] = h
        for c in range(D_FF // tn):
            cols = slice(c * tn, (c + 1) * tn)
            a = _dot_nt(h, wg_ref[cols, :])
            b = _dot_nt(h, wu_ref[cols, :])
            sig = 0.5 * jnp.tanh(0.5 * a) + 0.5
            silu = a * sig
            silu_ref[:, cols] = silu.astype(BF16)
            dgate_ref[:, cols] = (b * (sig + silu * (1.0 - sig))).astype(BF16)
            act_ref[:, cols] = (silu * b).astype(BF16)

    wide = jax.ShapeDtypeStruct((t, D_FF), BF16)
    row = lambda n: pl.BlockSpec((tm, n), lambda i: (i, 0))
    return _pallas(
        comm, body, name="ffn_up",
        grid=(t // tm,),
        in_specs=[row(D_MODEL), _full((1, D_MODEL)), _full((D_FF, D_MODEL)), _full((D_FF, D_MODEL))],
        out_specs=[row(D_MODEL), row(D_FF), row(D_FF), row(D_FF)],
        out_shape=[jax.ShapeDtypeStruct((t, D_MODEL), BF16), wide, wide, wide],
        args=(x, g, wg_t, wu_t))


def ffn_down(act, wd, x, tm):
    t = x.shape[0]

    def body(act_ref, wd_ref, x_ref, o_ref):
        o_ref[...] = x_ref[...] + 0.5 * _dot(act_ref[...], wd_ref[...])

    return pl.pallas_call(
        functools.partial(body), name="ffn_down",
        grid=(t // tm,),
        in_specs=[pl.BlockSpec((tm, D_FF), lambda i: (i, 0)), _full((D_FF, D_MODEL)),
                  pl.BlockSpec((tm, D_MODEL), lambda i: (i, 0))],
        out_specs=pl.BlockSpec((tm, D_MODEL), lambda i: (i, 0)),
        out_shape=jax.ShapeDtypeStruct((t, D_MODEL), F32),
        compiler_params=_params(1),
    )(act, wd, x)


def ffn_bwd_act(dx, wd, silu, dgate, tm, tn, comm=None):
    t = dx.shape[0]

    def body(dx_ref, wd_ref, silu_ref, dgate_ref, da_ref, db_ref):
        dxb = (0.5 * dx_ref[...]).astype(BF16)
        for c in range(D_FF // tn):
            cols = slice(c * tn, (c + 1) * tn)
            dact = _dot_nt(dxb, wd_ref[cols, :])
            da_ref[:, cols] = (dact * dgate_ref[:, cols].astype(F32)).astype(BF16)
            db_ref[:, cols] = (dact * silu_ref[:, cols].astype(F32)).astype(BF16)

    wide = jax.ShapeDtypeStruct((t, D_FF), BF16)
    row = lambda n: pl.BlockSpec((tm, n), lambda i: (i, 0))
    return _pallas(
        comm, body, name="ffn_bwd_act",
        grid=(t // tm,),
        in_specs=[row(D_MODEL), _full((D_FF, D_MODEL)), row(D_FF), row(D_FF)],
        out_specs=[row(D_FF), row(D_FF)],
        out_shape=[wide, wide],
        args=(dx, wd, silu, dgate))


def norm_bwd_matmul(a1, w1, a2, w2, x, g, dx_in, tm, comm=None):
    t = x.shape[0]
    k1, k2 = a1.shape[1], a2.shape[1]

    def body(a1_ref, w1_ref, a2_ref, w2_ref, x_ref, g_ref, dxin_ref, dx_ref, dg_ref):
        dh = _dot(a1_ref[...], w1_ref[...]) + _dot(a2_ref[...], w2_ref[...])
        xx = x_ref[...]
        r = _rstd(xx)
        dx, dg_rows = _rms_bwd(dh, xx * r, r, g_ref[...])
        dx_ref[...] = dxin_ref[...] + dx

        @pl.when(pl.program_id(0) == 0)
        def _():
            dg_ref[...] = jnp.zeros_like(dg_ref)

        dg_ref[...] += _colsum(dg_rows)

    row = pl.BlockSpec((tm, D_MODEL), lambda i: (i, 0))
    return _pallas(
        comm, body, name="norm_bwd_matmul",
        grid=(t // tm,),
        in_specs=[pl.BlockSpec((tm, k1), lambda i: (i, 0)), _full((k1, D_MODEL)),
                  pl.BlockSpec((tm, k2), lambda i: (i, 0)), _full((k2, D_MODEL)),
                  row, _full((1, D_MODEL)), row],
        out_specs=[row, _full((1, D_MODEL))],
        out_shape=[jax.ShapeDtypeStruct((t, D_MODEL), F32), jax.ShapeDtypeStruct((1, D_MODEL), F32)],
        args=(a1, w1, a2, w2, x, g, dx_in))


def matmul_tn(a, b, scale, tmm, tk):
    t, m = a.shape
    n = b.shape[1]
    nk = t // tk

    def body(a_ref, b_ref, o_ref, acc_ref):
        k = pl.program_id(1)

        @pl.when(k == 0)
        def _():
            acc_ref[...] = jnp.zeros_like(acc_ref)

        acc_ref[...] += _dot_tn(a_ref[...].astype(BF16), b_ref[...].astype(BF16))

        @pl.when(k == nk - 1)
        def _():
            o_ref[...] = (scale * acc_ref[...]).astype(BF16)

    return pl.pallas_call(
        functools.partial(body), name="matmul_tn",
        grid=(m // tmm, nk),
        in_specs=[pl.BlockSpec((tk, tmm), lambda i, k: (k, i)), pl.BlockSpec((tk, n), lambda i, k: (k, 0))],
        out_specs=pl.BlockSpec((tmm, n), lambda i, k: (i, 0)),
        out_shape=jax.ShapeDtypeStruct((m, n), BF16),
        scratch_shapes=[pltpu.VMEM((tmm, n), F32)],
        compiler_params=_params(2),
    )(a, b)


def input_projection(x, g, w_qkv_t, w_z_t, tm):
    t = x.shape[0]

    def body(x_ref, g_ref, wq_ref, wz_ref, qkv_ref, z_ref, h_ref):
        xx = x_ref[...]
        h = ((xx * _rstd(xx)) * g_ref[...]).astype(BF16)
        h_ref[...] = h
        qkv_ref[...] = _dot_nt(h, wq_ref[...])
        z_ref[...] = _dot_nt(h, wz_ref[...])

    row = lambda n: pl.BlockSpec((tm, n), lambda i: (i, 0))
    return pl.pallas_call(
        functools.partial(body), name="input_projection", grid=(t // tm,),
        in_specs=[row(D_MODEL), _full((1, D_MODEL)), _full((D_QKV, D_MODEL)), _full((2 * D_SGU, D_MODEL))],
        out_specs=[row(D_QKV), row(2 * D_SGU), row(D_MODEL)],
        out_shape=[jax.ShapeDtypeStruct((t, D_QKV), F32), jax.ShapeDtypeStruct((t, 2 * D_SGU), F32),
                   jax.ShapeDtypeStruct((t, D_MODEL), BF16)],
        compiler_params=_params(1))(x, g, w_qkv_t, w_z_t)


def _head_tile_spec(tm, rows):
    return pl.BlockSpec((N_HEADS, None, rows, tm), lambda i: (0, i, 0, 0))


def _to_head_tiles(a):
    return a.T.reshape(N_HEADS, HEAD_DIM, a.shape[0])


def _from_head_tiles(a):
    return a.reshape(D_ATTN, a.shape[-1]).T


def _head_mean_matrix(width):
    head = jnp.arange(width) // HEAD_DIM
    return (head[:, None] == head[None, :]).astype(F32) / HEAD_DIM


def _kv_tile_spec(n_sub, rows, cols):
    return pl.BlockSpec((N_KV_HEADS, n_sub, rows, cols), lambda i: (0, i, 0, 0))


def qk_prep(qkv, gq_w, gk_w, cos_w, sin_w, mean_q, mean_k, tm, tk, tk_v):
    t = qkv.shape[0]
    n_sub, n_sub_v = tm // tk, tm // tk_v

    def body(p_ref, gq_ref, gk_ref, cos_ref, sin_ref, mq_ref, mk_ref, q_ref, k_ref, kt_ref, v_ref, vt_ref,
             qmax_ref, kmax_ref):
        @pl.when(pl.program_id(0) == 0)
        def _():
            qmax_ref[...] = jnp.zeros_like(qmax_ref)
            kmax_ref[...] = jnp.zeros_like(kmax_ref)

        cos2, sin2 = cos_ref[...], sin_ref[...]
        q = p_ref[:, :D_ATTN]
        k = p_ref[:, D_ATTN:D_ATTN + D_KV]
        qn = q * lax.rsqrt(_dot_split(q * q, mq_ref[...]) + EPS) * gq_ref[...]
        kn = k * lax.rsqrt(_dot_split(k * k, mk_ref[...]) + EPS) * gk_ref[...]
        cos8, sin8 = _tile_lanes(cos2, D_ATTN // LANES), _tile_lanes(sin2, D_ATTN // LANES)
        q_rot = (qn * cos8 + _pair_swap(qn) * sin8) * Q_SCALE
        q_ref[...] = _to_head_tiles(q_rot).astype(BF16)
        k_rot = kn * cos2 + _pair_swap(kn) * sin2
        q_sq = HEAD_DIM * _dot_split(q_rot * q_rot, mq_ref[...])
        k_sq = HEAD_DIM * _dot_split(k_rot * k_rot, mk_ref[...])
        qmax_ref[...] = jnp.maximum(qmax_ref[...], jnp.max(q_sq, axis=0, keepdims=True))
        kmax_ref[...] = jnp.maximum(kmax_ref[...], jnp.max(k_sq, axis=0, keepdims=True))
        vv = p_ref[:, D_ATTN + D_KV:]
        for a, tok_ref in ((k_rot, k_ref), (vv, v_ref)):
            second = pltpu.roll(a, HEAD_DIM, 1)
            for c in range(n_sub):
                rows = slice(c * tk, (c + 1) * tk)
                tok_ref[0, c] = a[rows, :HEAD_DIM].astype(BF16)
                tok_ref[1, c] = second[rows, :HEAD_DIM].astype(BF16)
        for a, feat_ref, width, n in ((k_rot, kt_ref, tk, n_sub), (vv, vt_ref, tk_v, n_sub_v)):
            for c in range(n):
                tile = a[c * width:(c + 1) * width].T.reshape(N_KV_HEADS, HEAD_DIM, width)
                feat_ref[:, c, :HEAD_DIM, :] = tile.astype(BF16)
        vt_ref[:, :, HEAD_DIM:, :] = jnp.ones((N_KV_HEADS, n_sub_v, ONES_ROWS, tk_v), BF16)

    kv = lambda rows, cols: jax.ShapeDtypeStruct((N_KV_HEADS, t // tk, rows, cols), BF16)
    return pl.pallas_call(
        functools.partial(body), name="qk_prep", grid=(t // tm,),
        in_specs=[pl.BlockSpec((tm, D_QKV), lambda i: (i, 0)), _full((1, D_ATTN)), _full((1, D_KV)),
                  pl.BlockSpec((tm, LANES), lambda i: (i, 0)), pl.BlockSpec((tm, LANES), lambda i: (i, 0)),
                  _full((D_ATTN, D_ATTN)), _full((D_KV, D_KV))],
        out_specs=[_head_tile_spec(tm, HEAD_DIM), _kv_tile_spec(n_sub, tk, HEAD_DIM), _kv_tile_spec(n_sub, HEAD_DIM, tk),
                   _kv_tile_spec(n_sub, tk, HEAD_DIM), _kv_tile_spec(n_sub_v, HEAD_DIM + ONES_ROWS, tk_v),
                   _full((1, D_ATTN)), _full((1, D_KV))],
        out_shape=[jax.ShapeDtypeStruct((N_HEADS, t // tm, HEAD_DIM, tm), BF16), kv(tk, HEAD_DIM), kv(HEAD_DIM, tk),
                   kv(tk, HEAD_DIM),
                   jax.ShapeDtypeStruct((N_KV_HEADS, t // tk_v, HEAD_DIM + ONES_ROWS, tk_v), BF16),
                   jax.ShapeDtypeStruct((1, D_ATTN), F32), jax.ShapeDtypeStruct((1, D_KV), F32)],
        compiler_params=_params(1),
    )(qkv, gq_w, gk_w, cos_w, sin_w, mean_q, mean_k)


def qk_bwd(dq_rot, dk_rot, dv, qkv, gq_w, gk_w, cos_w, sin_w, mean_q, mean_k, tm):
    t = qkv.shape[0]
    tk = dk_rot.shape[-1]
    n_sub = tm // tk

    def token_major(ref):
        return jnp.concatenate([ref[:, c].reshape(D_KV, tk).T for c in range(n_sub)], axis=0)

    def branch(raw, d_rot, gain, mean_mat, cos, sin, scale):
        r = lax.rsqrt(_dot_split(raw * raw, mean_mat) + EPS)
        n = raw * r
        dy = (d_rot * cos - _pair_swap(d_rot) * sin) * scale
        dn = dy * gain
        return r * (dn - n * _dot_split(dn * n, mean_mat)), dy * n

    def body(dq_ref, dk_ref, dv_ref, p_ref, gq_ref, gk_ref, cos_ref, sin_ref, mq_ref, mk_ref,
             dp_ref, dgq_ref, dgk_ref):
        cos2, sin2 = cos_ref[...], sin_ref[...]
        cos8, sin8 = _tile_lanes(cos2, D_ATTN // LANES), _tile_lanes(sin2, D_ATTN // LANES)
        dq, dgq = branch(p_ref[:, :D_ATTN], _from_head_tiles(dq_ref[...]), gq_ref[...], mq_ref[...], cos8, sin8,
                         HEAD_DIM ** -0.5)
        dk, dgk = branch(p_ref[:, D_ATTN:D_ATTN + D_KV], token_major(dk_ref), gk_ref[...], mk_ref[...], cos2, sin2, 1.0)
        dp_ref[...] = jnp.concatenate([dq, dk, token_major(dv_ref)], axis=-1).astype(BF16)

        @pl.when(pl.program_id(0) == 0)
        def _():
            dgq_ref[...] = jnp.zeros_like(dgq_ref)
            dgk_ref[...] = jnp.zeros_like(dgk_ref)

        dgq_ref[...] += _colsum(dgq)
        dgk_ref[...] += _colsum(dgk)

    return pl.pallas_call(
        functools.partial(body), name="qk_bwd", grid=(t // tm,),
        in_specs=[_head_tile_spec(tm, HEAD_DIM), _kv_tile_spec(n_sub, HEAD_DIM, tk),
                  _kv_tile_spec(n_sub, HEAD_DIM, tk), pl.BlockSpec((tm, D_QKV), lambda i: (i, 0)),
                  _full((1, D_ATTN)), _full((1, D_KV)),
                  pl.BlockSpec((tm, LANES), lambda i: (i, 0)), pl.BlockSpec((tm, LANES), lambda i: (i, 0)),
                  _full((D_ATTN, D_ATTN)), _full((D_KV, D_KV))],
        out_specs=[pl.BlockSpec((tm, D_QKV), lambda i: (i, 0)), _full((1, D_ATTN)), _full((1, D_KV))],
        out_shape=[jax.ShapeDtypeStruct((t, D_QKV), BF16), jax.ShapeDtypeStruct((1, D_ATTN), F32),
                   jax.ShapeDtypeStruct((1, D_KV), F32)],
        compiler_params=_params(1),
    )(dq_rot, dk_rot, dv, qkv, gq_w, gk_w, cos_w, sin_w, mean_q, mean_k)


def attention_fwd(bound, q_t, k, v_t, comm=None):
    _, nq, _, tq = q_t.shape
    _, nk, tk, _ = k.shape

    def body(bound_ref, q_ref, k_ref, v_ref, o_ref, lse_ref, s_scr, p_scr):
        head_bound = bound_ref[pl.program_id(0)]
        safe = head_bound <= SAFE_SCORE_BOUND
        q = q_ref[...]
        s_scr[0] = _dot(k_ref[0], q)
        p_scr[1] = jnp.zeros((tk, tq), BF16)
        zero = jnp.zeros((HEAD_DIM + ONES_ROWS, tq), F32)

        def matmuls(j, slot):
            pv = _dot(v_ref[jnp.maximum(j - 1, 0)], p_scr[1 - slot])
            s_scr[1 - slot] = _dot(k_ref[jnp.minimum(j + 1, nk - 1)], q)
            return pv

        def finish(m, acc):
            acc = acc + _dot(v_ref[nk - 1], p_scr[(nk - 1) % 2])
            l = acc[HEAD_DIM:HEAD_DIM + 1]
            o_ref[...] = acc[:HEAD_DIM] / l
            lse_ref[...] = m + jnp.log2(l)

        @pl.when(safe)
        def _():
            m = jnp.full((1, tq), head_bound, F32)

            def step(j, slot, acc):
                s = s_scr[slot]
                pv = matmuls(j, slot)
                p_scr[slot] = jnp.exp2(s - m).astype(BF16)
                return acc + pv

            finish(m, _loop_pairs(nk, step, zero))

        @pl.when(jnp.logical_not(safe))
        def _():
            def step(j, slot, carry):
                m, acc = carry
                s = s_scr[slot]
                pv = matmuls(j, slot)
                m_new = jnp.maximum(m, jnp.max(s, axis=0, keepdims=True))
                p_scr[slot] = jnp.exp2(s - m_new).astype(BF16)
                return m_new, jnp.exp2(m - m_new) * (acc + pv)

            finish(*_loop_pairs(nk, step, (jnp.full((1, tq), -1e30, F32), zero)))

    return _pallas(
        comm, body, name="attention_fwd", grid=(N_HEADS, nq),
        in_specs=[pl.BlockSpec(memory_space=pltpu.SMEM),
                  pl.BlockSpec((None, None, HEAD_DIM, tq), lambda h, i: (h, i, 0, 0)),
                  pl.BlockSpec((None, nk, tk, HEAD_DIM), lambda h, i: (h // KV_GROUP, 0, 0, 0)),
                  pl.BlockSpec((None, nk, HEAD_DIM + ONES_ROWS, tk), lambda h, i: (h // KV_GROUP, 0, 0, 0))],
        out_specs=[pl.BlockSpec((None, None, HEAD_DIM, tq), lambda h, i: (h, i, 0, 0)),
                   pl.BlockSpec((None, None, 1, tq), lambda h, i: (h, i, 0, 0))],
        out_shape=[jax.ShapeDtypeStruct((N_HEADS, nq, HEAD_DIM, tq), F32),
                   jax.ShapeDtypeStruct((N_HEADS, nq, 1, tq), F32)],
        scratch_shapes=[pltpu.VMEM((2, tk, tq), F32), pltpu.VMEM((2, tk, tq), BF16)],
        args=(bound, q_t, k, v_t))


def attention_bwd(q_t, do_t, lse, delta, k, k_t, v, comm=None):
    _, nq, _, tq = q_t.shape
    _, nk, tk, _ = k.shape

    def body(q_ref, do_ref, lse_ref, delta_ref, k_ref, kt_ref, v_ref, dq_ref, dk_ref, dv_ref,
             s_scr, dp_scr, p_scr, ds_scr):
        @pl.when(pl.program_id(1) == 0)
        def _():
            dq_ref[...] = jnp.zeros_like(dq_ref)

        kk, kt, vv = k_ref[...], kt_ref[...], v_ref[...]
        n = KV_GROUP * nq
        s_scr[0] = _dot(kk, q_ref[0, 0])
        dp_scr[0] = _dot(vv, do_ref[0, 0])
        p_scr[1] = jnp.zeros((tk, tq), BF16)
        ds_scr[1] = jnp.zeros((tk, tq), BF16)

        def products(t, slot, dk, dv):
            h, i = t // nq, t % nq
            ds = ds_scr[slot]
            dq_ref[h, i] += _dot(kt, ds)
            return dk + _dot_nt(q_ref[h, i], ds), dv + _dot_nt(do_ref[h, i], p_scr[slot])

        def step(t, slot, carry):
            s, dp = s_scr[slot], dp_scr[slot]
            dk, dv = products(jnp.maximum(t - 1, 0), 1 - slot, *carry)
            nxt = jnp.minimum(t + 1, n - 1)
            s_scr[1 - slot] = _dot(kk, q_ref[nxt // nq, nxt % nq])
            dp_scr[1 - slot] = _dot(vv, do_ref[nxt // nq, nxt % nq])
            h, i = t // nq, t % nq
            p = jnp.exp2(s - lse_ref[h, i])
            p_scr[slot] = p.astype(BF16)
            ds_scr[slot] = (p * (dp - delta_ref[h, i])).astype(BF16)
            return dk, dv

        zero = jnp.zeros((HEAD_DIM, tk), F32)
        dk, dv = products(n - 1, (n - 1) % 2, *_loop_pairs(n, step, (zero, zero)))
        dk_ref[...] = dk * (1.0 / LOG2_E)
        dv_ref[...] = dv

    group = lambda g, j: (g, 0, 0, 0)
    tile = lambda g, j: (g, j, 0, 0)
    return _pallas(
        comm, body, name="attention_bwd", grid=(N_KV_HEADS, nk),
        in_specs=[pl.BlockSpec((KV_GROUP, nq, HEAD_DIM, tq), group),
                  pl.BlockSpec((KV_GROUP, nq, HEAD_DIM, tq), group),
                  pl.BlockSpec((KV_GROUP, nq, 1, tq), group),
                  pl.BlockSpec((KV_GROUP, nq, 1, tq), group),
                  pl.BlockSpec((None, None, tk, HEAD_DIM), tile),
                  pl.BlockSpec((None, None, HEAD_DIM, tk), tile),
                  pl.BlockSpec((None, None, tk, HEAD_DIM), tile)],
        out_specs=[pl.BlockSpec((KV_GROUP, nq, HEAD_DIM, tq), group),
                   pl.BlockSpec((None, None, HEAD_DIM, tk), tile),
                   pl.BlockSpec((None, None, HEAD_DIM, tk), tile)],
        out_shape=[jax.ShapeDtypeStruct((N_HEADS, nq, HEAD_DIM, tq), F32),
                   jax.ShapeDtypeStruct((N_KV_HEADS, nk, HEAD_DIM, tk), F32),
                   jax.ShapeDtypeStruct((N_KV_HEADS, nk, HEAD_DIM, tk), F32)],
        scratch_shapes=[pltpu.VMEM((2, tk, tq), F32), pltpu.VMEM((2, tk, tq), F32),
                        pltpu.VMEM((2, tk, tq), BF16), pltpu.VMEM((2, tk, tq), BF16)],
        args=(q_t, do_t, lse, delta, k, k_t, v))


def _group_select(parts):
    lane_group = lax.broadcasted_iota(jnp.int32, parts[0].shape, 1) // SGU_GROUP_DIM
    out = parts[0]
    for g in range(1, N_SGU_GROUPS):
        out = jnp.where(lane_group == g, parts[g], out)
    return out


def _gate_forward(z, g_sgu, ws_ref, bias):
    gz, th = _gelu(z)
    u, vv = gz[:, :D_SGU], gz[:, D_SGU:]
    rv = _rstd(vv)
    nv = vv * rv
    vn = (nv * g_sgu).astype(BF16)
    fs = []
    for c in range(z.shape[0] // CHUNK):
        vc = vn[c * CHUNK:(c + 1) * CHUNK]
        fs.append(_group_select([_dot(ws_ref[g], vc) for g in range(N_SGU_GROUPS)]) + bias)
    f = jnp.concatenate(fs, axis=0) if len(fs) > 1 else fs[0]
    return th, u, rv, nv, vn, f


def mix_out(z, o, x, g_sgu, g_ao, g_so, ws, bias, w_out, tm):
    t = x.shape[0]

    def body(z_ref, o_ref, x_ref, gs_ref, gao_ref, gso_ref, ws_ref, bias_ref, wout_ref, x2_ref, mixed_ref):
        _, u, _, _, _, f = _gate_forward(z_ref[...], gs_ref[...], ws_ref, bias_ref[...])
        sgu = u * f
        oo = _from_head_tiles(o_ref[...])
        mixed = jnp.concatenate([oo * _rstd(oo) * gao_ref[...], sgu * _rstd(sgu) * gso_ref[...]], axis=-1).astype(BF16)
        mixed_ref[...] = mixed
        x2_ref[...] = x_ref[...] + _dot(mixed, wout_ref[...])

    row = lambda n: pl.BlockSpec((tm, n), lambda i: (i, 0))
    return pl.pallas_call(
        functools.partial(body), name="mix_out", grid=(t // tm,),
        in_specs=[row(2 * D_SGU), _head_tile_spec(tm, HEAD_DIM), row(D_MODEL), _full((1, D_SGU)), _full((1, D_ATTN)),
                  _full((1, D_SGU)),
                  _full((N_SGU_GROUPS, CHUNK, CHUNK)), _full((CHUNK, D_SGU)), _full((D_MODEL, D_MODEL))],
        out_specs=[row(D_MODEL), row(D_MODEL)],
        out_shape=[jax.ShapeDtypeStruct((t, D_MODEL), F32), jax.ShapeDtypeStruct((t, D_MODEL), BF16)],
        compiler_params=_params(1),
    )(z, o, x, g_sgu, g_ao, g_so, ws, bias, w_out)


def mix_bwd(dx2, z, o, g_sgu, g_ao, g_so, ws, ws_t, bias, w_out, group_ind, tm):
    t = dx2.shape[0]
    n_tiles = t // tm

    def body(dx_ref, z_ref, o_ref, gs_ref, gao_ref, gso_ref, ws_ref, wst_ref, bias_ref, wout_ref, ind_ref,
             do_ref, delta_ref, dz_ref, dg_ref, dws_ref, dbs_ref, df_sum):
        step = pl.program_id(0)

        @pl.when(step == 0)
        def _():
            dg_ref[...] = jnp.zeros_like(dg_ref)
            dws_ref[...] = jnp.zeros_like(dws_ref)
            df_sum[...] = jnp.zeros_like(df_sum)

        z = z_ref[...]
        th, u, rv, nv, vn, f = _gate_forward(z, gs_ref[...], ws_ref, bias_ref[...])
        dmixed = _dot_nt(dx_ref[...].astype(BF16), wout_ref[...])
        o_tiles = o_ref[...]
        oo = _from_head_tiles(o_tiles)
        ro = _rstd(oo)
        d_o, dgao = _rms_bwd(dmixed[:, :D_ATTN], oo * ro, ro, gao_ref[...])
        do_tiles = _to_head_tiles(d_o)
        do_ref[...] = do_tiles.astype(BF16)
        delta_ref[...] = jnp.sum(do_tiles * o_tiles, axis=1, keepdims=True)
        sgu = u * f
        rs = _rstd(sgu)
        dsgu, dgso = _rms_bwd(dmixed[:, D_ATTN:], sgu * rs, rs, gso_ref[...])
        du = dsgu * f
        df = dsgu * u
        lane_group = lax.broadcasted_iota(jnp.int32, (CHUNK, D_SGU), 1) // SGU_GROUP_DIM
        dvns = []
        df_acc = jnp.zeros((CHUNK, D_SGU), F32)
        for c in range(tm // CHUNK):
            dfc32 = df[c * CHUNK:(c + 1) * CHUNK]
            dfc = dfc32.astype(BF16)
            vc = vn[c * CHUNK:(c + 1) * CHUNK]
            dvns.append(_group_select([_dot(wst_ref[g], dfc) for g in range(N_SGU_GROUPS)]))
            for g in range(N_SGU_GROUPS):
                dws_ref[g] += _dot_nt(jnp.where(lane_group == g, dfc, jnp.zeros_like(dfc)), vc)
            df_acc = df_acc + dfc32
        df_sum[...] += df_acc
        dvn = jnp.concatenate(dvns, axis=0) if len(dvns) > 1 else dvns[0]
        dvv, dgs = _rms_bwd(dvn, nv, rv, gs_ref[...])
        dz_ref[...] = (jnp.concatenate([du, dvv], axis=-1) * _gelu_grad(z, th)).astype(BF16)
        dg_ref[0:1, :] += _colsum(dgao)
        dg_ref[1:2, :] += _colsum(dgso)
        dg_ref[2:3, :] += _colsum(dgs)

        @pl.when(step == n_tiles - 1)
        def _():
            dbs_ref[...] = _dot_f32(df_sum[...], ind_ref[...])

    row = lambda n: pl.BlockSpec((tm, n), lambda i: (i, 0))
    return pl.pallas_call(
        functools.partial(body), name="mix_bwd", grid=(n_tiles,),
        in_specs=[row(D_MODEL), row(2 * D_SGU), _head_tile_spec(tm, HEAD_DIM), _full((1, D_SGU)), _full((1, D_ATTN)),
                  _full((1, D_SGU)),
                  _full((N_SGU_GROUPS, CHUNK, CHUNK)), _full((N_SGU_GROUPS, CHUNK, CHUNK)), _full((CHUNK, D_SGU)),
                  _full((D_MODEL, D_MODEL)), _full((D_SGU, LANES))],
        out_specs=[_head_tile_spec(tm, HEAD_DIM), _head_tile_spec(tm, 1), row(2 * D_SGU), _full((8, D_SGU)),
                   _full((N_SGU_GROUPS, CHUNK, CHUNK)), _full((CHUNK, LANES))],
        out_shape=[jax.ShapeDtypeStruct((N_HEADS, n_tiles, HEAD_DIM, tm), BF16),
                   jax.ShapeDtypeStruct((N_HEADS, n_tiles, 1, tm), F32), jax.ShapeDtypeStruct((t, 2 * D_SGU), BF16),
                   jax.ShapeDtypeStruct((8, D_SGU), F32),
                   jax.ShapeDtypeStruct((N_SGU_GROUPS, CHUNK, CHUNK), F32),
                   jax.ShapeDtypeStruct((CHUNK, LANES), F32)],
        scratch_shapes=[pltpu.VMEM((CHUNK, D_SGU), F32)],
        compiler_params=_params(1),
    )(dx2, z, o, g_sgu, g_ao, g_so, ws, ws_t, bias, w_out, group_ind)


def ffn_down_loss(act, wd, x, g, target, tm):
    t = x.shape[0]

    def body(act_ref, wd_ref, x_ref, g_ref, t_ref, loss_ref, dx_ref, dg_ref):
        @pl.when(pl.program_id(0) == 0)
        def _():
            loss_ref[...] = jnp.zeros_like(loss_ref)
            dg_ref[...] = jnp.zeros_like(dg_ref)

        xx = x_ref[...] + 0.5 * _dot(act_ref[...], wd_ref[...])
        r = _rstd(xx)
        n = xx * r
        err = n * g_ref[...] - t_ref[...]
        per_token = jnp.mean(err * err, axis=-1, keepdims=True)
        loss_ref[...] += 0.5 * jnp.sum(per_token, axis=0, keepdims=True)
        dx, dg_rows = _rms_bwd(err * (1.0 / D_MODEL), n, r, g_ref[...])
        dx_ref[...] = dx
        dg_ref[...] += _colsum(dg_rows)

    row = pl.BlockSpec((tm, D_MODEL), lambda i: (i, 0))
    return pl.pallas_call(
        functools.partial(body), name="ffn_down_loss", grid=(t // tm,),
        in_specs=[pl.BlockSpec((tm, D_FF), lambda i: (i, 0)), _full((D_FF, D_MODEL)), row, _full((1, D_MODEL)), row],
        out_specs=[_full((1, LANES)), row, _full((1, D_MODEL))],
        out_shape=[jax.ShapeDtypeStruct((1, LANES), F32), jax.ShapeDtypeStruct((t, D_MODEL), F32),
                   jax.ShapeDtypeStruct((1, D_MODEL), F32)],
        compiler_params=_params(1),
    )(act, wd, x, g, target)


def _rope_tables(t):
    rows = t // GRID_W
    row_idx = jnp.repeat(jnp.arange(rows, dtype=F32), GRID_W)
    col_idx = jnp.tile(jnp.arange(GRID_W, dtype=F32), rows)
    axis_dim = HEAD_DIM // 2
    inv = 1.0 / (ROPE_THETA ** (jnp.arange(0, axis_dim, 2, dtype=F32) / axis_dim))
    ang = jnp.concatenate([row_idx[:, None] * inv, col_idx[:, None] * inv], axis=-1)
    cos = jnp.repeat(jnp.cos(ang), 2, axis=-1)
    sin = jnp.repeat(jnp.sin(ang), 2, axis=-1) * jnp.tile(jnp.array([-1.0, 1.0], F32), HEAD_DIM // 2)
    return jnp.tile(cos, (1, LANES // HEAD_DIM)), jnp.tile(sin, (1, LANES // HEAD_DIM))


def _heads_to_tiles_t(a, n_heads, tile):
    t = a.shape[0]
    return a.reshape(t // tile, tile, n_heads, HEAD_DIM).transpose(2, 0, 3, 1)


def _heads_to_tiles(a, n_heads, tile):
    t = a.shape[0]
    return a.reshape(t // tile, tile, n_heads, HEAD_DIM).transpose(2, 0, 1, 3)


def _tiles_t_to_heads(a):
    h, n, _, tile = a.shape
    return a.transpose(1, 3, 0, 2).reshape(n * tile, h * HEAD_DIM)


def kernel(x, g_ffn1, w1_gate, w1_up, w1_down, g_mix, w_in, g_q, g_k, g_sgu, w_s, b_s, g_attn_out, g_sgu_out, w_out, g_ffn2, w2_gate, w2_up, w2_down, g_final, loss_target, m_g_ffn1, m_w1_gate, m_w1_up, m_w1_down, m_g_mix, m_w_in, m_g_q, m_g_k, m_g_sgu, m_w_s, m_b_s, m_g_attn_out, m_g_sgu_out, m_w_out, m_g_ffn2, m_w2_gate, m_w2_up, m_w2_down, m_g_final, v_g_ffn1, v_w1_gate, v_w1_up, v_w1_down, v_g_mix, v_w_in, v_g_q, v_g_k, v_g_sgu, v_w_s, v_b_s, v_g_attn_out, v_g_sgu_out, v_w_out, v_g_ffn2, v_w2_gate, v_w2_up, v_w2_down, v_g_final):
    weights = dict(g_ffn1=g_ffn1, w1_gate=w1_gate, w1_up=w1_up, w1_down=w1_down, g_mix=g_mix, w_in=w_in, g_q=g_q,
                   g_k=g_k, g_sgu=g_sgu, w_s=w_s, b_s=b_s, g_attn_out=g_attn_out, g_sgu_out=g_sgu_out, w_out=w_out,
                   g_ffn2=g_ffn2, w2_gate=w2_gate, w2_up=w2_up, w2_down=w2_down, g_final=g_final)
    m_in = dict(g_ffn1=m_g_ffn1, w1_gate=m_w1_gate, w1_up=m_w1_up, w1_down=m_w1_down, g_mix=m_g_mix, w_in=m_w_in,
                g_q=m_g_q, g_k=m_g_k, g_sgu=m_g_sgu, w_s=m_w_s, b_s=m_b_s, g_attn_out=m_g_attn_out,
                g_sgu_out=m_g_sgu_out, w_out=m_w_out, g_ffn2=m_g_ffn2, w2_gate=m_w2_gate, w2_up=m_w2_up,
                w2_down=m_w2_down, g_final=m_g_final)
    v_in = dict(g_ffn1=v_g_ffn1, w1_gate=v_w1_gate, w1_up=v_w1_up, w1_down=v_w1_down, g_mix=v_g_mix, w_in=v_w_in,
                g_q=v_g_q, g_k=v_g_k, g_sgu=v_g_sgu, w_s=v_w_s, b_s=v_b_s, g_attn_out=v_g_attn_out,
                g_sgu_out=v_g_sgu_out, w_out=v_w_out, g_ffn2=v_g_ffn2, w2_gate=v_w2_gate, w2_up=v_w2_up,
                w2_down=v_w2_down, g_final=v_g_final)
    names = list(weights)

    t = x.shape[1]
    x0 = x[0]
    target = loss_target[0]
    tm = min(256, t)
    tm_ff = min(256, t)
    tn_ff = 256
    tq = min(512, t)
    tk = min(256, t)
    tk_fwd = min(512, t)
    tk_w = min(2048, t)

    def shard_rows(name):
        w = weights[name][0]
        return (w.T if name in TRANSPOSED else w).astype(BF16)

    rows_of = dict(SHARD_ROWS)
    full = {}

    def packed(group):
        return jnp.concatenate([shard_rows(n) for n in group], axis=0), [rows_of[n] for n in group]

    def gather_of(group):
        return gather_exchange(*packed(group))

    def take(group, gathered):
        for n, g in zip(group, gathered):
            full[n] = g.reshape(N_DEV * rows_of[n], D_MODEL)

    first, second, third = ("w1_gate", "w1_up"), ("w1_down", "w_in", "w_out"), ("w2_gate", "w2_up", "w2_down")
    take(first, gather_two_level(*packed(first), "gather_first"))

    (h1, a1, b1, act1), gathered = ffn_up(x0, g_ffn1, full["w1_gate"], full["w1_up"], tm_ff, tn_ff, gather_of(second))
    take(second, gathered)
    w_in_t = full["w_in"]
    w_qkv_t, w_z_t = w_in_t[:D_QKV], w_in_t[D_QKV:]
    x1 = ffn_down(act1, full["w1_down"], x0, tm)

    qkv, z, h2 = input_projection(x1, g_mix, w_qkv_t, w_z_t, tm)
    cos_w, sin_w = _rope_tables(t)
    gq_w = jnp.tile(g_q, (1, N_HEADS))
    gk_w = jnp.tile(g_k, (1, N_KV_HEADS))
    mean_q, mean_k = _head_mean_matrix(D_ATTN).astype(BF16), _head_mean_matrix(D_KV).astype(BF16)
    q_t, k_tiles, kt_tiles, v_tiles, vt_tiles, q_sq_max, k_sq_max = qk_prep(
        qkv, gq_w, gk_w, cos_w, sin_w, mean_q, mean_k, tq, tk, tk_fwd)
    score_bound = 1.02 * jnp.sqrt(q_sq_max.reshape(N_HEADS, HEAD_DIM)[:, 0]
                                  * jnp.repeat(k_sq_max.reshape(N_KV_HEADS, HEAD_DIM)[:, 0], KV_GROUP))
    k_tiles_fwd = k_tiles.reshape(N_KV_HEADS, t // tk_fwd, tk_fwd, HEAD_DIM)
    (o_t, lse), gathered = attention_fwd(score_bound, q_t, k_tiles_fwd, vt_tiles, gather_of(third))
    take(third, gathered)

    ws_b = w_s[0].astype(BF16)
    ws_tb = jnp.swapaxes(w_s[0], 1, 2).astype(BF16)
    bias = jnp.repeat(b_s[0].T, SGU_GROUP_DIM, axis=1)
    x2, mixed = mix_out(z, o_t, x1, g_sgu, g_attn_out, g_sgu_out, ws_b, bias, full["w_out"], tq)

    (h3, a2, b2, act2), _ = ffn_up(x2, g_ffn2, full["w2_gate"], full["w2_up"], tm_ff, tn_ff)

    loss_part, dx3, dg_final = ffn_down_loss(act2, full["w2_down"], x2, g_final, target, tm)

    tmm = D_FF // 2
    (da2, db2), _ = ffn_bwd_act(dx3, full["w2_down"], a2, b2, tm_ff, tn_ff)
    (dx2, dg_ffn2), _ = norm_bwd_matmul(da2, full["w2_gate"], db2, full["w2_up"], x2, g_ffn2, dx3, tm)
    dwg2, dwu2 = matmul_tn(da2, h3, 1.0, tmm, tk_w), matmul_tn(db2, h3, 1.0, tmm, tk_w)
    dwd2 = matmul_tn(act2, dx3, 0.5, tmm, tk_w)

    group_ind = (jnp.arange(D_SGU)[:, None] // SGU_GROUP_DIM == jnp.arange(LANES)[None, :]).astype(F32)
    do_t, delta, dz, dg_mixrow, dws, dbs = mix_bwd(dx2, z, o_t, g_sgu, g_attn_out, g_sgu_out, ws_b, ws_tb, bias,
                                                   full["w_out"], group_ind, tq)
    dw_out = matmul_tn(mixed, dx2, 1.0, D_MODEL // 2, tk_w)

    group_a = ("w2_gate", "w2_up", "w2_down", "w_out")
    (dq_t, dk_t, dv_t), (parts_a,) = attention_bwd(q_t, do_t, lse, delta, k_tiles, kt_tiles, v_tiles,
                                                   scatter_exchange([dwg2, dwu2, dwd2, dw_out]))
    dqkv, dgq_w, dgk_w = qk_bwd(dq_t, dk_t, dv_t, qkv, gq_w, gk_w, cos_w, sin_w, mean_q, mean_k, tq)

    def pack_small(arrays):
        pieces = []
        for a in arrays:
            flat = a.reshape(-1)
            pieces.append(jnp.pad(flat, (0, (-flat.shape[0]) % (8 * LANES))).reshape(-1, LANES))
        return jnp.concatenate(pieces, axis=0), [p.shape[0] for p in pieces]

    early = dict(g_ffn2=dg_ffn2, g_final=dg_final, g_q=dgq_w.reshape(N_HEADS, HEAD_DIM).sum(0),
                 g_k=dgk_w.reshape(N_KV_HEADS, HEAD_DIM).sum(0), g_attn_out=dg_mixrow[0], g_sgu_out=dg_mixrow[1],
                 g_sgu=dg_mixrow[2], w_s=dws, b_s=dbs[:, :N_SGU_GROUPS].T)
    early_pack, early_rows = pack_small(list(early.values()))
    (dx1, dg_mix), (early_parts,) = norm_bwd_matmul(dqkv, w_qkv_t, dz, w_z_t, x1, g_mix, dx2, tm,
                                                    gather_exchange(early_pack, [early_pack.shape[0]]))
    dw_in = jnp.concatenate([matmul_tn(dqkv, h2, 1.0, D_QKV // 2, tk_w), matmul_tn(dz, h2, 1.0, D_SGU, tk_w)], axis=0)

    dwd1 = matmul_tn(act1, dx1, 0.5, tmm, tk_w)
    group_b = ("w_in", "w1_down")
    (da1, db1), (parts_b,) = ffn_bwd_act(dx1, full["w1_down"], a1, b1, tm_ff, tn_ff, scatter_exchange([dw_in, dwd1]))
    dwg1, dwu1 = matmul_tn(da1, h1, 1.0, tmm, tk_w), matmul_tn(db1, h1, 1.0, tmm, tk_w)
    group_c = ("w1_gate", "w1_up")
    (dx0, dg_ffn1), (parts_c,) = norm_bwd_matmul(da1, full["w1_gate"], db1, full["w1_up"], x0, g_ffn1, dx1, tm,
                                                 scatter_exchange([dwg1, dwu1]))

    late = dict(g_mix=dg_mix, g_ffn1=dg_ffn1, loss=loss_part)
    late_pack, late_rows = pack_small(list(late.values()))
    (late_parts,) = run_exchange(gather_exchange(late_pack, [late_pack.shape[0]]), "gather_late_small_grads")
    small_sums = {}
    for entries, rows, parts in ((early, early_rows, early_parts), (late, late_rows, late_parts)):
        summed = sum_parts(parts, parts.shape[1])
        off = 0
        for n, r in zip(entries, rows):
            small_sums[n] = summed[off:off + r]
            off += r
    loss = small_sums.pop("loss")[0, 0]

    grads = {}
    for group, parts in ((group_a, parts_a), (group_b, parts_b), (group_c, parts_c)):
        summed = sum_parts(parts, 32)
        off = 0
        for n in group:
            gsh = summed[off:off + rows_of[n]]
            grads[n] = (gsh.T if n in TRANSPOSED else gsh)[None]
            off += rows_of[n]
    for n, summed in small_sums.items():
        grads[n] = summed.reshape(-1)[:weights[n].size].reshape(weights[n].shape)

    delta_w, new_m, new_v = {}, {}, {}
    for n in names:
        shape = weights[n].shape
        as2d = (lambda a: a.reshape(-1, shape[-1]))
        d, m2, v2 = adamw(as2d(weights[n]), as2d(grads[n]), as2d(m_in[n]), as2d(v_in[n]))
        delta_w[n], new_m[n], new_v[n] = d.reshape(shape), m2.reshape(shape), v2.reshape(shape)

    return (loss, dx0[None], *[grads[n] for n in names], *[delta_w[n] for n in names],
            *[new_m[n] for n in names], *[new_v[n] for n in names])
```

```python
import functools
import math

import jax
import jax.numpy as jnp
from jax import lax
from jax.experimental import pallas as pl
from jax.experimental.pallas import tpu as pltpu

F32 = jnp.float32
BF16 = jnp.bfloat16

D_MODEL = 1024
D_FF = 2816
N_HEADS = 8
HEAD_DIM = 64
N_KV_HEADS = 2
KV_GROUP = N_HEADS // N_KV_HEADS
D_ATTN = N_HEADS * HEAD_DIM
D_KV = N_KV_HEADS * HEAD_DIM
D_QKV = D_ATTN + 2 * D_KV
N_SGU_GROUPS = 8
SGU_GROUP_DIM = 64
D_SGU = N_SGU_GROUPS * SGU_GROUP_DIM
CHUNK = 128
GRID_W = 64
ROPE_THETA = 10000.0
EPS = 1e-6
N_DEV = 8
LANES = 128

ONES_ROWS = 16
SAFE_SCORE_BOUND = 60.0
LOG2_E = math.log2(math.e)
Q_SCALE = HEAD_DIM ** -0.5 * LOG2_E

ADAM_LR = 0.001
ADAM_B1 = 0.9
ADAM_B2 = 0.999
ADAM_EPS = 1e-08
ADAM_WD = 0.01
ADAM_STEP = 10

MESH_AXES = ("x", "y", "c")
MESH_IDS = pl.DeviceIdType.MESH

VMEM_LIMIT = 56 * 1024 * 1024

SHARD_ROWS = (("w1_gate", D_FF // N_DEV), ("w1_up", D_FF // N_DEV), ("w1_down", D_FF // N_DEV),
              ("w_in", (D_QKV + 2 * D_SGU) // N_DEV), ("w_out", D_MODEL // N_DEV),
              ("w2_gate", D_FF // N_DEV), ("w2_up", D_FF // N_DEV), ("w2_down", D_FF // N_DEV))
PACK_ROWS = sum(r for _, r in SHARD_ROWS)
TRANSPOSED = ("w1_gate", "w1_up", "w_in", "w2_gate", "w2_up")


def _params(n_grid):
    return pltpu.CompilerParams(dimension_semantics=("arbitrary",) * n_grid, vmem_limit_bytes=VMEM_LIMIT)


def _dot(a, b):
    return jnp.dot(a, b, preferred_element_type=F32)


def _dot_nt(a, b):
    return lax.dot_general(a, b, (((1,), (1,)), ((), ())), preferred_element_type=F32)


def _dot_tn(a, b):
    return lax.dot_general(a, b, (((0,), (0,)), ((), ())), preferred_element_type=F32)


def _dot_f32(a, b):
    return jnp.dot(a, b, preferred_element_type=F32, precision=lax.Precision.HIGHEST)


def _dot_split(a, b):
    hi = a.astype(BF16)
    lo = (a - hi.astype(F32)).astype(BF16)
    return _dot(hi, b) + _dot(lo, b)


def _rstd(x):
    return lax.rsqrt(jnp.mean(x * x, axis=-1, keepdims=True) + EPS)


def _rms_bwd(dy, n, r, g):
    dn = dy * g
    return r * (dn - n * jnp.mean(dn * n, axis=-1, keepdims=True)), dy * n


def _colsum(a):
    return jnp.sum(a, axis=0, keepdims=True)


_GELU_C = math.sqrt(2.0 / math.pi)


def _gelu(x):
    t = jnp.tanh(_GELU_C * (x + 0.044715 * (x * x * x)))
    return x * (0.5 * (1.0 + t)), t


def _gelu_grad(x, t):
    return 0.5 * (1.0 + t) + 0.5 * x * (1.0 - t * t) * (_GELU_C * (1.0 + 3 * 0.044715 * x * x))


def _pair_swap(a):
    w = a.shape[-1]
    lane = lax.broadcasted_iota(jnp.int32, a.shape, a.ndim - 1)
    return jnp.where(lane % 2 == 0, pltpu.roll(a, w - 1, a.ndim - 1), pltpu.roll(a, 1, a.ndim - 1))


def _tile_lanes(a, reps):
    return jnp.concatenate([a] * reps, axis=-1) if reps > 1 else a


def _loop_pairs(n, step, carry):
    assert n % 2 == 0, n

    def pair(jj, c):
        return step(2 * jj + 1, 1, step(2 * jj, 0, c))

    return lax.fori_loop(0, n // 2, pair, carry)


def _full(shape):
    nd = len(shape)
    return pl.BlockSpec(shape, lambda *_: (0,) * nd)


def _mesh_pos():
    return lax.axis_index("x"), lax.axis_index("y"), lax.axis_index("c")


def _peer(pos, d):
    x, y, c = pos
    px = 1 - x if d & 4 else x
    py = 1 - y if d & 2 else y
    pc = 1 - c if d & 1 else c
    return (px, py, pc), 4 * px + 2 * py + pc


class _Exchange:
    def __init__(self, operands, out_shape, n_local, plan):
        self.operands = list(operands)
        self.out_shape = list(out_shape)
        self.sem_shapes = [pltpu.SemaphoreType.DMA((N_DEV - 1,)), pltpu.SemaphoreType.DMA((N_DEV - 1,)),
                           pltpu.SemaphoreType.DMA((n_local,))]
        self._plan = plan

    def _copies(self, in_refs, out_refs):
        pos = _mesh_pos()
        return pos, self._plan(4 * pos[0] + 2 * pos[1] + pos[2], in_refs, out_refs)

    def start(self, in_refs, out_refs, sems):
        send_sems, recv_sems, local_sems = sems
        pos, (local, remote, _) = self._copies(in_refs, out_refs)
        for k, (src, dst) in enumerate(local):
            pltpu.make_async_copy(src, dst, local_sems.at[k]).start()
        for d in range(1, N_DEV):
            peer, peer_lin = _peer(pos, d)
            for src, dst in remote(peer_lin):
                pltpu.make_async_remote_copy(src_ref=src, dst_ref=dst, send_sem=send_sems.at[d - 1],
                                             recv_sem=recv_sems.at[d - 1], device_id=peer,
                                             device_id_type=MESH_IDS).start()

    def wait(self, in_refs, out_refs, sems):
        send_sems, recv_sems, local_sems = sems
        pos, (local, _, whole) = self._copies(in_refs, out_refs)
        for d in range(1, N_DEV):
            peer, peer_lin = _peer(pos, d)
            ref = whole(peer_lin)
            everything = pltpu.make_async_remote_copy(src_ref=ref, dst_ref=ref, send_sem=send_sems.at[d - 1],
                                                      recv_sem=recv_sems.at[d - 1], device_id=peer,
                                                      device_id_type=MESH_IDS)
            everything.wait_send()
            everything.wait_recv()
        for k, (src, dst) in enumerate(local):
            pltpu.make_async_copy(src, dst, local_sems.at[k]).wait()


def _offsets(rows):
    offs, o = [], 0
    for r in rows:
        offs.append(o)
        o += r
    return offs


def gather_exchange(src, rows):
    offs = _offsets(rows)

    def plan(me, in_refs, out_refs):
        pieces = [(in_refs[0].at[pl.ds(o, r)], out.at[me]) for o, r, out in zip(offs, rows, out_refs)]
        return pieces, (lambda peer_lin: pieces), (lambda peer_lin: in_refs[0])

    return _Exchange([src], [jax.ShapeDtypeStruct((N_DEV, r) + src.shape[1:], src.dtype) for r in rows],
                     len(rows), plan)


def scatter_exchange(grads):
    rows = [g.shape[0] // N_DEV for g in grads]
    offs = _offsets(rows)

    def plan(me, in_refs, out_refs):
        parts = out_refs[0]

        def slabs(owner):
            return [(g.at[pl.ds(pl.multiple_of(owner * r, 16), r)], parts.at[me, pl.ds(o, r)])
                    for g, o, r in zip(in_refs, offs, rows)]

        return slabs(me), slabs, (lambda peer_lin: parts.at[peer_lin])

    shape = jax.ShapeDtypeStruct((N_DEV, sum(rows)) + grads[0].shape[1:], grads[0].dtype)
    return _Exchange(grads, [shape], len(rows), plan)


def gather_two_level(src, rows, name):
    offs = _offsets(rows)
    n_p = len(rows)

    def body(src_ref, *refs):
        outs, (send_sems, recv_sems, local_sems) = refs[:n_p], refs[n_p:]
        x, y, c = _mesh_pos()
        me, sibling = (x, y, c), (x, y, 1 - c)
        chips = [(1 - x, y), (x, 1 - y), (1 - x, 1 - y)]

        def slab(w, dev):
            return outs[w].at[4 * dev[0] + 2 * dev[1] + dev[2]]

        def copy(w, k, block, to, from_src=False):
            return pltpu.make_async_remote_copy(
                src_ref=src_ref.at[pl.ds(offs[w], rows[w])] if from_src else slab(w, block), dst_ref=slab(w, block),
                send_sem=send_sems.at[w * 7 + k], recv_sem=recv_sems.at[w * 7 + k],
                device_id=to, device_id_type=MESH_IDS)

        mine = [pltpu.make_async_copy(src_ref.at[pl.ds(offs[w], rows[w])], slab(w, me), local_sems.at[w])
                for w in range(n_p)]
        for cp in mine:
            cp.start()
        first = []
        for w in range(n_p):
            first.append(copy(w, 0, me, sibling, True))
            first += [copy(w, 1 + j, me, (*chip, c), True) for j, chip in enumerate(chips)]
        for cp in first:
            cp.start()
        passed = []
        for j, chip in enumerate(chips):
            for w in range(n_p):
                copy(w, 1 + j, (*chip, c), me).wait_recv()
                cp = copy(w, 4 + j, (*chip, c), sibling)
                cp.start()
                passed.append(cp)
        for w in range(n_p):
            copy(w, 0, sibling, me).wait_recv()
            for j, chip in enumerate(chips):
                copy(w, 4 + j, (*chip, 1 - c), me).wait_recv()
        for cp in first + passed:
            cp.wait_send()
        for cp in mine:
            cp.wait()

    any_spec = pl.BlockSpec(memory_space=pl.ANY)
    return pl.pallas_call(
        functools.partial(body), name=name,
        out_shape=[jax.ShapeDtypeStruct((N_DEV, r) + src.shape[1:], src.dtype) for r in rows],
        in_specs=[any_spec], out_specs=[any_spec] * n_p,
        scratch_shapes=[pltpu.SemaphoreType.DMA((7 * n_p,)), pltpu.SemaphoreType.DMA((7 * n_p,)),
                        pltpu.SemaphoreType.DMA((n_p,))],
        compiler_params=pltpu.CompilerParams(has_side_effects=True),
    )(src)


def run_exchange(ex, name):
    n_in, n_out = len(ex.operands), len(ex.out_shape)

    def body(*refs):
        parts = refs[:n_in], refs[n_in:n_in + n_out], refs[n_in + n_out:]
        ex.start(*parts)
        ex.wait(*parts)

    any_spec = pl.BlockSpec(memory_space=pl.ANY)
    return pl.pallas_call(
        functools.partial(body), name=name, out_shape=ex.out_shape,
        in_specs=[any_spec] * n_in, out_specs=[any_spec] * n_out, scratch_shapes=ex.sem_shapes,
        compiler_params=pltpu.CompilerParams(has_side_effects=True),
    )(*ex.operands)


def _pallas(comm, body, *, name, grid, in_specs, out_specs, out_shape, args, scratch_shapes=()):
    params = _params(len(grid))
    if comm is None:
        res = pl.pallas_call(functools.partial(body), name=name, grid=grid, in_specs=list(in_specs),
                             out_specs=list(out_specs), out_shape=list(out_shape),
                             scratch_shapes=list(scratch_shapes), compiler_params=params)(*args)
        return list(res), []
    n_in, n_out, n_scr = len(in_specs), len(out_specs), len(scratch_shapes)
    c_in, c_out = len(comm.operands), len(comm.out_shape)

    def edge(last):
        conds = [pl.program_id(a) == (g - 1 if last else 0) for a, g in enumerate(grid)]
        return functools.reduce(jnp.logical_and, conds)

    def wrapped(*refs):
        refs = list(refs)
        ins, refs = refs[:n_in], refs[n_in:]
        cins, refs = refs[:c_in], refs[c_in:]
        outs, refs = refs[:n_out], refs[n_out:]
        couts, refs = refs[:c_out], refs[c_out:]
        scr, sems = refs[:n_scr], refs[n_scr:]

        @pl.when(edge(False))
        def _():
            comm.start(cins, couts, sems)

        body(*ins, *outs, *scr)

        @pl.when(edge(True))
        def _():
            comm.wait(cins, couts, sems)

    any_spec = pl.BlockSpec(memory_space=pl.ANY)
    res = pl.pallas_call(
        wrapped, name=name, grid=grid,
        in_specs=list(in_specs) + [any_spec] * c_in, out_specs=list(out_specs) + [any_spec] * c_out,
        out_shape=list(out_shape) + comm.out_shape, scratch_shapes=list(scratch_shapes) + comm.sem_shapes,
        compiler_params=pltpu.CompilerParams(dimension_semantics=("arbitrary",) * len(grid),
                                             vmem_limit_bytes=VMEM_LIMIT, has_side_effects=True),
    )(*args, *comm.operands)
    return res[:n_out], res[n_out:]


def sum_parts(parts, block_rows):
    n, rows, cols = parts.shape

    def body(p_ref, o_ref):
        acc = p_ref[0].astype(F32)
        for s in range(1, n):
            acc = acc + p_ref[s].astype(F32)
        o_ref[...] = acc

    return pl.pallas_call(
        functools.partial(body), name="sum_parts",
        grid=(rows // block_rows,),
        in_specs=[pl.BlockSpec((n, block_rows, cols), lambda i: (0, i, 0))],
        out_specs=pl.BlockSpec((block_rows, cols), lambda i: (i, 0)),
        out_shape=jax.ShapeDtypeStruct((rows, cols), F32),
        compiler_params=_params(1),
    )(parts)


def adamw(w, g, m, v):
    def body(w_ref, g_ref, m_ref, v_ref, d_ref, m_out, v_out):
        gg = g_ref[...]
        m2 = ADAM_B1 * m_ref[...] + (1.0 - ADAM_B1) * gg
        v2 = ADAM_B2 * v_ref[...] + (1.0 - ADAM_B2) * (gg * gg)
        m_hat = m2 / (1.0 - ADAM_B1 ** ADAM_STEP)
        v_hat = v2 / (1.0 - ADAM_B2 ** ADAM_STEP)
        d_ref[...] = -ADAM_LR * (m_hat / (jnp.sqrt(v_hat) + ADAM_EPS) + ADAM_WD * w_ref[...])
        m_out[...] = m2
        v_out[...] = v2

    spec = _full(w.shape)
    shape = jax.ShapeDtypeStruct(w.shape, F32)
    return pl.pallas_call(
        functools.partial(body), name="adamw",
        in_specs=[spec] * 4, out_specs=[spec] * 3, out_shape=[shape] * 3,
        compiler_params=pltpu.CompilerParams(vmem_limit_bytes=VMEM_LIMIT),
    )(w, g, m, v)


def ffn_up(x, g, wg_t, wu_t, tm, tn, comm=None):
    t = x.shape[0]

    def body(x_ref, g_ref, wg_ref, wu_ref, h_ref, silu_ref, dgate_ref, act_ref):
        xx = x_ref[...]
        h = ((xx * _rstd(xx)) * g_ref[...]).astype(BF16)
        h_ref[...] = h
        for c in range(D_FF // tn):
            cols = slice(c * tn, (c + 1) * tn)
            a = _dot_nt(h, wg_ref[cols, :])
            b = _dot_nt(h, wu_ref[cols, :])
            sig = 0.5 * jnp.tanh(0.5 * a) + 0.5
            silu = a * sig
            silu_ref[:, cols] = silu.astype(BF16)
            dgate_ref[:, cols] = (b * (sig + silu * (1.0 - sig))).astype(BF16)
            act_ref[:, cols] = (silu * b).astype(BF16)

    wide = jax.ShapeDtypeStruct((t, D_FF), BF16)
    row = lambda n: pl.BlockSpec((tm, n), lambda i: (i, 0))
    return _pallas(
        comm, body, name="ffn_up",
        grid=(t // tm,),
        in_specs=[row(D_MODEL), _full((1, D_MODEL)), _full((D_FF, D_MODEL)), _full((D_FF, D_MODEL))],
        out_specs=[row(D_MODEL), row(D_FF), row(D_FF), row(D_FF)],
        out_shape=[jax.ShapeDtypeStruct((t, D_MODEL), BF16), wide, wide, wide],
        args=(x, g, wg_t, wu_t))


def ffn_down(act, wd, x, tm):
    t = x.shape[0]

    def body(act_ref, wd_ref, x_ref, o_ref):
        o_ref[...] = x_ref[...] + 0.5 * _dot(act_ref[...], wd_ref[...])

    return pl.pallas_call(
        functools.partial(body), name="ffn_down",
        grid=(t // tm,),
        in_specs=[pl.BlockSpec((tm, D_FF), lambda i: (i, 0)), _full((D_FF, D_MODEL)),
                  pl.BlockSpec((tm, D_MODEL), lambda i: (i, 0))],
        out_specs=pl.BlockSpec((tm, D_MODEL), lambda i: (i, 0)),
        out_shape=jax.ShapeDtypeStruct((t, D_MODEL), F32),
        compiler_params=_params(1),
    )(act, wd, x)


def ffn_bwd_act(dx, wd, silu, dgate, tm, tn, comm=None):
    t = dx.shape[0]

    def body(dx_ref, wd_ref, silu_ref, dgate_ref, da_ref, db_ref):
        dxb = (0.5 * dx_ref[...]).astype(BF16)
        for c in range(D_FF // tn):
            cols = slice(c * tn, (c + 1) * tn)
            dact = _dot_nt(dxb, wd_ref[cols, :])
            da_ref[:, cols] = (dact * dgate_ref[:, cols].astype(F32)).astype(BF16)
            db_ref[:, cols] = (dact * silu_ref[:, cols].astype(F32)).astype(BF16)

    wide = jax.ShapeDtypeStruct((t, D_FF), BF16)
    row = lambda n: pl.BlockSpec((tm, n), lambda i: (i, 0))
    return _pallas(
        comm, body, name="ffn_bwd_act",
        grid=(t // tm,),
        in_specs=[row(D_MODEL), _full((D_FF, D_MODEL)), row(D_FF), row(D_FF)],
        out_specs=[row(D_FF), row(D_FF)],
        out_shape=[wide, wide],
        args=(dx, wd, silu, dgate))


def norm_bwd_matmul(a1, w1, a2, w2, x, g, dx_in, tm, comm=None):
    t = x.shape[0]
    k1, k2 = a1.shape[1], a2.shape[1]

    def body(a1_ref, w1_ref, a2_ref, w2_ref, x_ref, g_ref, dxin_ref, dx_ref, dg_ref):
        dh = _dot(a1_ref[...], w1_ref[...]) + _dot(a2_ref[...], w2_ref[...])
        xx = x_ref[...]
        r = _rstd(xx)
        dx, dg_rows = _rms_bwd(dh, xx * r, r, g_ref[...])
        dx_ref[...] = dxin_ref[...] + dx

        @pl.when(pl.program_id(0) == 0)
        def _():
            dg_ref[...] = jnp.zeros_like(dg_ref)

        dg_ref[...] += _colsum(dg_rows)

    row = pl.BlockSpec((tm, D_MODEL), lambda i: (i, 0))
    return _pallas(
        comm, body, name="norm_bwd_matmul",
        grid=(t // tm,),
        in_specs=[pl.BlockSpec((tm, k1), lambda i: (i, 0)), _full((k1, D_MODEL)),
                  pl.BlockSpec((tm, k2), lambda i: (i, 0)), _full((k2, D_MODEL)),
                  row, _full((1, D_MODEL)), row],
        out_specs=[row, _full((1, D_MODEL))],
        out_shape=[jax.ShapeDtypeStruct((t, D_MODEL), F32), jax.ShapeDtypeStruct((1, D_MODEL), F32)],
        args=(a1, w1, a2, w2, x, g, dx_in))


def matmul_tn(a, b, scale, tmm, tk):
    t, m = a.shape
    n = b.shape[1]
    nk = t // tk

    def body(a_ref, b_ref, o_ref, acc_ref):
        k = pl.program_id(1)

        @pl.when(k == 0)
        def _():
            acc_ref[...] = jnp.zeros_like(acc_ref)

        acc_ref[...] += _dot_tn(a_ref[...].astype(BF16), b_ref[...].astype(BF16))

        @pl.when(k == nk - 1)
        def _():
            o_ref[...] = (scale * acc_ref[...]).astype(BF16)

    return pl.pallas_call(
        functools.partial(body), name="matmul_tn",
        grid=(m // tmm, nk),
        in_specs=[pl.BlockSpec((tk, tmm), lambda i, k: (k, i)), pl.BlockSpec((tk, n), lambda i, k: (k, 0))],
        out_specs=pl.BlockSpec((tmm, n), lambda i, k: (i, 0)),
        out_shape=jax.ShapeDtypeStruct((m, n), BF16),
        scratch_shapes=[pltpu.VMEM((tmm, n), F32)],
        compiler_params=_params(2),
    )(a, b)


def input_projection(x, g, w_qkv_t, w_z_t, tm):
    t = x.shape[0]

    def body(x_ref, g_ref, wq_ref, wz_ref, qkv_ref, z_ref, h_ref):
        xx = x_ref[...]
        h = ((xx * _rstd(xx)) * g_ref[...]).astype(BF16)
        h_ref[...] = h
        qkv_ref[...] = _dot_nt(h, wq_ref[...])
        z_ref[...] = _dot_nt(h, wz_ref[...])

    row = lambda n: pl.BlockSpec((tm, n), lambda i: (i, 0))
    return pl.pallas_call(
        functools.partial(body), name="input_projection", grid=(t // tm,),
        in_specs=[row(D_MODEL), _full((1, D_MODEL)), _full((D_QKV, D_MODEL)), _full((2 * D_SGU, D_MODEL))],
        out_specs=[row(D_QKV), row(2 * D_SGU), row(D_MODEL)],
        out_shape=[jax.ShapeDtypeStruct((t, D_QKV), F32), jax.ShapeDtypeStruct((t, 2 * D_SGU), F32),
                   jax.ShapeDtypeStruct((t, D_MODEL), BF16)],
        compiler_params=_params(1))(x, g, w_qkv_t, w_z_t)


def _shift_rows(shape, first, second):
    row = lax.broadcasted_iota(jnp.int32, shape, len(shape) - 2)
    return jnp.where(row == 0, first, jnp.where(row == 1, second, 0.0))


def _hi_lo(a):
    hi = a.astype(BF16).astype(F32)
    return hi, a - hi


def _head_tile_spec(tm, rows):
    return pl.BlockSpec((N_HEADS, None, rows, tm), lambda i: (0, i, 0, 0))


def _to_head_tiles(a):
    return a.T.reshape(N_HEADS, HEAD_DIM, a.shape[0])


def _from_head_tiles(a):
    return a.reshape(D_ATTN, a.shape[-1]).T


def _head_mean_matrix(width):
    head = jnp.arange(width) // HEAD_DIM
    return (head[:, None] == head[None, :]).astype(F32) / HEAD_DIM


def _kv_tile_spec(n_sub, rows, cols):
    return pl.BlockSpec((N_KV_HEADS, n_sub, rows, cols), lambda i: (0, i, 0, 0))


def qk_prep(qkv, gq_w, gk_w, cos_w, sin_w, mean_q, mean_k, tm, tk, tk_v):
    t = qkv.shape[0]
    n_sub, n_sub_v = tm // tk, tm // tk_v

    def body(p_ref, gq_ref, gk_ref, cos_ref, sin_ref, mq_ref, mk_ref, q_ref, k_ref, kt_ref, vt_ref, vtb_ref,
             qmax_ref, kmax_ref):
        @pl.when(pl.program_id(0) == 0)
        def _():
            qmax_ref[...] = jnp.zeros_like(qmax_ref)
            kmax_ref[...] = jnp.zeros_like(kmax_ref)

        cos2, sin2 = cos_ref[...], sin_ref[...]
        q = p_ref[:, :D_ATTN]
        k = p_ref[:, D_ATTN:D_ATTN + D_KV]
        qn = q * lax.rsqrt(_dot_split(q * q, mq_ref[...]) + EPS) * gq_ref[...]
        kn = k * lax.rsqrt(_dot_split(k * k, mk_ref[...]) + EPS) * gk_ref[...]
        cos8, sin8 = _tile_lanes(cos2, D_ATTN // LANES), _tile_lanes(sin2, D_ATTN // LANES)
        q_rot = (qn * cos8 + _pair_swap(qn) * sin8) * Q_SCALE
        q_ref[...] = _to_head_tiles(q_rot).astype(BF16)
        k_rot = kn * cos2 + _pair_swap(kn) * sin2
        q_sq = HEAD_DIM * _dot_split(q_rot * q_rot, mq_ref[...])
        k_sq = HEAD_DIM * _dot_split(k_rot * k_rot, mk_ref[...])
        qmax_ref[...] = jnp.maximum(qmax_ref[...], jnp.max(q_sq, axis=0, keepdims=True))
        kmax_ref[...] = jnp.maximum(kmax_ref[...], jnp.max(k_sq, axis=0, keepdims=True))
        vv = p_ref[:, D_ATTN + D_KV:]
        second = pltpu.roll(k_rot, HEAD_DIM, 1)
        for c in range(n_sub):
            rows = slice(c * tk, (c + 1) * tk)
            k_ref[0, c] = k_rot[rows, :HEAD_DIM].astype(BF16)
            k_ref[1, c] = second[rows, :HEAD_DIM].astype(BF16)
        for a, feat_ref, width, n in ((k_rot, kt_ref, tk, n_sub), (vv, vtb_ref, tk, n_sub), (vv, vt_ref, tk_v, n_sub_v)):
            for c in range(n):
                tile = a[c * width:(c + 1) * width].T.reshape(N_KV_HEADS, HEAD_DIM, width)
                feat_ref[:, c, :HEAD_DIM, :] = tile.astype(BF16)
        vt_ref[:, :, HEAD_DIM:, :] = jnp.ones((N_KV_HEADS, n_sub_v, ONES_ROWS, tk_v), BF16)
        minus = _shift_rows((N_KV_HEADS, n_sub, HEAD_DIM, tk), -1.0, -1.0).astype(BF16)
        kt_ref[:, :, HEAD_DIM:, :] = minus
        vtb_ref[:, :, HEAD_DIM:, :] = minus

    kv = lambda rows, cols: jax.ShapeDtypeStruct((N_KV_HEADS, t // tk, rows, cols), BF16)
    return pl.pallas_call(
        functools.partial(body), name="qk_prep", grid=(t // tm,),
        in_specs=[pl.BlockSpec((tm, D_QKV), lambda i: (i, 0)), _full((1, D_ATTN)), _full((1, D_KV)),
                  pl.BlockSpec((tm, LANES), lambda i: (i, 0)), pl.BlockSpec((tm, LANES), lambda i: (i, 0)),
                  _full((D_ATTN, D_ATTN)), _full((D_KV, D_KV))],
        out_specs=[_head_tile_spec(tm, HEAD_DIM), _kv_tile_spec(n_sub, tk, HEAD_DIM),
                   _kv_tile_spec(n_sub, 2 * HEAD_DIM, tk),
                   _kv_tile_spec(n_sub_v, HEAD_DIM + ONES_ROWS, tk_v), _kv_tile_spec(n_sub, 2 * HEAD_DIM, tk),
                   _full((1, D_ATTN)), _full((1, D_KV))],
        out_shape=[jax.ShapeDtypeStruct((N_HEADS, t // tm, HEAD_DIM, tm), BF16), kv(tk, HEAD_DIM), kv(2 * HEAD_DIM, tk),
                   jax.ShapeDtypeStruct((N_KV_HEADS, t // tk_v, HEAD_DIM + ONES_ROWS, tk_v), BF16),
                   kv(2 * HEAD_DIM, tk),
                   jax.ShapeDtypeStruct((1, D_ATTN), F32), jax.ShapeDtypeStruct((1, D_KV), F32)],
        compiler_params=_params(1),
    )(qkv, gq_w, gk_w, cos_w, sin_w, mean_q, mean_k)


def qk_bwd(dq_rot, dk_rot, dv, qkv, gq_w, gk_w, cos_w, sin_w, mean_q, mean_k, tm):
    t = qkv.shape[0]
    tk = dk_rot.shape[-1]
    n_sub = tm // tk

    def token_major(ref):
        return jnp.concatenate([ref[:, c].reshape(D_KV, tk).T for c in range(n_sub)], axis=0)

    def branch(raw, d_rot, gain, mean_mat, cos, sin, scale):
        r = lax.rsqrt(_dot_split(raw * raw, mean_mat) + EPS)
        n = raw * r
        dy = (d_rot * cos - _pair_swap(d_rot) * sin) * scale
        dn = dy * gain
        return r * (dn - n * _dot_split(dn * n, mean_mat)), dy * n

    def body(dq_ref, dk_ref, dv_ref, p_ref, gq_ref, gk_ref, cos_ref, sin_ref, mq_ref, mk_ref,
             dp_ref, dgq_ref, dgk_ref):
        cos2, sin2 = cos_ref[...], sin_ref[...]
        cos8, sin8 = _tile_lanes(cos2, D_ATTN // LANES), _tile_lanes(sin2, D_ATTN // LANES)
        dq, dgq = branch(p_ref[:, :D_ATTN], _from_head_tiles(dq_ref[...]), gq_ref[...], mq_ref[...], cos8, sin8,
                         HEAD_DIM ** -0.5)
        dk, dgk = branch(p_ref[:, D_ATTN:D_ATTN + D_KV], token_major(dk_ref), gk_ref[...], mk_ref[...], cos2, sin2, 1.0)
        dp_ref[...] = jnp.concatenate([dq, dk, token_major(dv_ref)], axis=-1).astype(BF16)

        @pl.when(pl.program_id(0) == 0)
        def _():
            dgq_ref[...] = jnp.zeros_like(dgq_ref)
            dgk_ref[...] = jnp.zeros_like(dgk_ref)

        dgq_ref[...] += _colsum(dgq)
        dgk_ref[...] += _colsum(dgk)

    return pl.pallas_call(
        functools.partial(body), name="qk_bwd", grid=(t // tm,),
        in_specs=[_head_tile_spec(tm, HEAD_DIM), _kv_tile_spec(n_sub, HEAD_DIM, tk),
                  _kv_tile_spec(n_sub, HEAD_DIM, tk), pl.BlockSpec((tm, D_QKV), lambda i: (i, 0)),
                  _full((1, D_ATTN)), _full((1, D_KV)),
                  pl.BlockSpec((tm, LANES), lambda i: (i, 0)), pl.BlockSpec((tm, LANES), lambda i: (i, 0)),
                  _full((D_ATTN, D_ATTN)), _full((D_KV, D_KV))],
        out_specs=[pl.BlockSpec((tm, D_QKV), lambda i: (i, 0)), _full((1, D_ATTN)), _full((1, D_KV))],
        out_shape=[jax.ShapeDtypeStruct((t, D_QKV), BF16), jax.ShapeDtypeStruct((1, D_ATTN), F32),
                   jax.ShapeDtypeStruct((1, D_KV), F32)],
        compiler_params=_params(1),
    )(dq_rot, dk_rot, dv, qkv, gq_w, gk_w, cos_w, sin_w, mean_q, mean_k)


def attention_fwd(bound, q_t, k, v_t, comm=None):
    _, nq, _, tq = q_t.shape
    _, nk, tk, _ = k.shape

    def body(bound_ref, q_ref, k_ref, v_ref, o_ref, qtok_ref, s_scr, p_scr):
        head_bound = bound_ref[pl.program_id(0)]
        safe = head_bound <= SAFE_SCORE_BOUND
        q = q_ref[...]
        s_scr[0] = _dot(k_ref[0], q)
        p_scr[1] = jnp.zeros((tk, tq), BF16)
        zero = jnp.zeros((HEAD_DIM + ONES_ROWS, tq), F32)

        def matmuls(j, slot):
            pv = _dot(v_ref[jnp.maximum(j - 1, 0)], p_scr[1 - slot])
            s_scr[1 - slot] = _dot(k_ref[jnp.minimum(j + 1, nk - 1)], q)
            return pv

        def finish(m, acc):
            acc = acc + _dot(v_ref[nk - 1], p_scr[(nk - 1) % 2])
            l = acc[HEAD_DIM:HEAD_DIM + 1]
            o_ref[...] = acc[:HEAD_DIM] / l
            lse_rows = _shift_rows((HEAD_DIM, tq), *_hi_lo(m + jnp.log2(l)))
            qtok_ref[...] = jnp.concatenate([q.astype(F32), lse_rows], axis=0).T.astype(BF16)

        @pl.when(safe)
        def _():
            m = jnp.full((1, tq), head_bound, F32)

            def step(j, slot, acc):
                s = s_scr[slot]
                pv = matmuls(j, slot)
                p_scr[slot] = jnp.exp2(s - m).astype(BF16)
                return acc + pv

            finish(m, _loop_pairs(nk, step, zero))

        @pl.when(jnp.logical_not(safe))
        def _():
            def step(j, slot, carry):
                m, acc = carry
                s = s_scr[slot]
                pv = matmuls(j, slot)
                m_new = jnp.maximum(m, jnp.max(s, axis=0, keepdims=True))
                p_scr[slot] = jnp.exp2(s - m_new).astype(BF16)
                return m_new, jnp.exp2(m - m_new) * (acc + pv)

            finish(*_loop_pairs(nk, step, (jnp.full((1, tq), -1e30, F32), zero)))

    return _pallas(
        comm, body, name="attention_fwd", grid=(N_HEADS, nq),
        in_specs=[pl.BlockSpec(memory_space=pltpu.SMEM),
                  pl.BlockSpec((None, None, HEAD_DIM, tq), lambda h, i: (h, i, 0, 0)),
                  pl.BlockSpec((None, nk, tk, HEAD_DIM), lambda h, i: (h // KV_GROUP, 0, 0, 0)),
                  pl.BlockSpec((None, nk, HEAD_DIM + ONES_ROWS, tk), lambda h, i: (h // KV_GROUP, 0, 0, 0))],
        out_specs=[pl.BlockSpec((None, None, HEAD_DIM, tq), lambda h, i: (h, i, 0, 0)),
                   pl.BlockSpec((None, None, tq, 2 * HEAD_DIM), lambda h, i: (h, i, 0, 0))],
        out_shape=[jax.ShapeDtypeStruct((N_HEADS, nq, HEAD_DIM, tq), F32),
                   jax.ShapeDtypeStruct((N_HEADS, nq, tq, 2 * HEAD_DIM), BF16)],
        scratch_shapes=[pltpu.VMEM((2, tk, tq), F32), pltpu.VMEM((2, tk, tq), BF16)],
        args=(bound, q_t, k, v_t))


def attention_bwd(q_tok, do_tok, q_t, do_t, k_t, v_t, comm=None):
    _, nq, _, tq = q_t.shape
    _, nk, _, tk = k_t.shape

    def body(qtok_ref, dotok_ref, q_ref, do_ref, kt_ref, vt_ref, dq_ref, dk_ref, dv_ref,
             s_scr, dp_scr, p_scr, ds_scr):
        @pl.when(pl.program_id(1) == 0)
        def _():
            dq_ref[...] = jnp.zeros_like(dq_ref)

        kt_aug, vt_aug = kt_ref[...], vt_ref[...]
        kt = kt_aug[:HEAD_DIM]
        n = KV_GROUP * nq
        s_scr[0] = _dot(qtok_ref[0, 0], kt_aug)
        dp_scr[0] = _dot(dotok_ref[0, 0], vt_aug)
        p_scr[1] = jnp.zeros((tq, tk), BF16)
        ds_scr[1] = jnp.zeros((tq, tk), BF16)

        def products(t, slot, dk, dv):
            h, i = t // nq, t % nq
            ds = ds_scr[slot]
            dq_ref[h, i] += _dot_nt(kt, ds)
            return dk + _dot(q_ref[h, i], ds), dv + _dot(do_ref[h, i], p_scr[slot])

        def step(t, slot, carry):
            s, dp = s_scr[slot], dp_scr[slot]
            dk, dv = products(jnp.maximum(t - 1, 0), 1 - slot, *carry)
            nxt = jnp.minimum(t + 1, n - 1)
            s_scr[1 - slot] = _dot(qtok_ref[nxt // nq, nxt % nq], kt_aug)
            dp_scr[1 - slot] = _dot(dotok_ref[nxt // nq, nxt % nq], vt_aug)
            p = jnp.exp2(s)
            p_scr[slot] = p.astype(BF16)
            ds_scr[slot] = (p * dp).astype(BF16)
            return dk, dv

        zero = jnp.zeros((HEAD_DIM, tk), F32)
        dk, dv = products(n - 1, (n - 1) % 2, *_loop_pairs(n, step, (zero, zero)))
        dk_ref[...] = dk * (1.0 / LOG2_E)
        dv_ref[...] = dv

    group = lambda g, j: (g, 0, 0, 0)
    tile = lambda g, j: (g, j, 0, 0)
    once = pl.Buffered(1)
    return _pallas(
        comm, body, name="attention_bwd", grid=(N_KV_HEADS, nk),
        in_specs=[pl.BlockSpec((KV_GROUP, nq, tq, 2 * HEAD_DIM), group, pipeline_mode=once),
                  pl.BlockSpec((KV_GROUP, nq, tq, 2 * HEAD_DIM), group, pipeline_mode=once),
                  pl.BlockSpec((KV_GROUP, nq, HEAD_DIM, tq), group, pipeline_mode=once),
                  pl.BlockSpec((KV_GROUP, nq, HEAD_DIM, tq), group, pipeline_mode=once),
                  pl.BlockSpec((None, None, 2 * HEAD_DIM, tk), tile),
                  pl.BlockSpec((None, None, 2 * HEAD_DIM, tk), tile)],
        out_specs=[pl.BlockSpec((KV_GROUP, nq, HEAD_DIM, tq), group),
                   pl.BlockSpec((None, None, HEAD_DIM, tk), tile),
                   pl.BlockSpec((None, None, HEAD_DIM, tk), tile)],
        out_shape=[jax.ShapeDtypeStruct((N_HEADS, nq, HEAD_DIM, tq), F32),
                   jax.ShapeDtypeStruct((N_KV_HEADS, nk, HEAD_DIM, tk), F32),
                   jax.ShapeDtypeStruct((N_KV_HEADS, nk, HEAD_DIM, tk), F32)],
        scratch_shapes=[pltpu.VMEM((2, tq, tk), F32), pltpu.VMEM((2, tq, tk), F32),
                        pltpu.VMEM((2, tq, tk), BF16), pltpu.VMEM((2, tq, tk), BF16)],
        args=(q_tok, do_tok, q_t, do_t, k_t, v_t))


def _group_select(parts):
    lane_group = lax.broadcasted_iota(jnp.int32, parts[0].shape, 1) // SGU_GROUP_DIM
    out = parts[0]
    for g in range(1, N_SGU_GROUPS):
        out = jnp.where(lane_group == g, parts[g], out)
    return out


def _gate_forward(z, g_sgu, ws_ref, bias):
    gz, th = _gelu(z)
    u, vv = gz[:, :D_SGU], gz[:, D_SGU:]
    rv = _rstd(vv)
    nv = vv * rv
    vn = (nv * g_sgu).astype(BF16)
    fs = []
    for c in range(z.shape[0] // CHUNK):
        vc = vn[c * CHUNK:(c + 1) * CHUNK]
        fs.append(_group_select([_dot(ws_ref[g], vc) for g in range(N_SGU_GROUPS)]) + bias)
    f = jnp.concatenate(fs, axis=0) if len(fs) > 1 else fs[0]
    return th, u, rv, nv, vn, f


def mix_out(z, o, x, g_sgu, g_ao, g_so, ws, bias, w_out, tm):
    t = x.shape[0]

    def body(z_ref, o_ref, x_ref, gs_ref, gao_ref, gso_ref, ws_ref, bias_ref, wout_ref, x2_ref, mixed_ref):
        _, u, _, _, _, f = _gate_forward(z_ref[...], gs_ref[...], ws_ref, bias_ref[...])
        sgu = u * f
        oo = _from_head_tiles(o_ref[...])
        mixed = jnp.concatenate([oo * _rstd(oo) * gao_ref[...], sgu * _rstd(sgu) * gso_ref[...]], axis=-1).astype(BF16)
        mixed_ref[...] = mixed
        x2_ref[...] = x_ref[...] + _dot(mixed, wout_ref[...])

    row = lambda n: pl.BlockSpec((tm, n), lambda i: (i, 0))
    return pl.pallas_call(
        functools.partial(body), name="mix_out", grid=(t // tm,),
        in_specs=[row(2 * D_SGU), _head_tile_spec(tm, HEAD_DIM), row(D_MODEL), _full((1, D_SGU)), _full((1, D_ATTN)),
                  _full((1, D_SGU)),
                  _full((N_SGU_GROUPS, CHUNK, CHUNK)), _full((CHUNK, D_SGU)), _full((D_MODEL, D_MODEL))],
        out_specs=[row(D_MODEL), row(D_MODEL)],
        out_shape=[jax.ShapeDtypeStruct((t, D_MODEL), F32), jax.ShapeDtypeStruct((t, D_MODEL), BF16)],
        compiler_params=_params(1),
    )(z, o, x, g_sgu, g_ao, g_so, ws, bias, w_out)


def mix_bwd(dx2, z, o, g_sgu, g_ao, g_so, ws, ws_t, bias, w_out, group_ind, tm):
    t = dx2.shape[0]
    n_tiles = t // tm

    def body(dx_ref, z_ref, o_ref, gs_ref, gao_ref, gso_ref, ws_ref, wst_ref, bias_ref, wout_ref, ind_ref,
             do_ref, dotok_ref, dz_ref, dg_ref, dws_ref, dbs_ref, df_sum):
        step = pl.program_id(0)

        @pl.when(step == 0)
        def _():
            dg_ref[...] = jnp.zeros_like(dg_ref)
            dws_ref[...] = jnp.zeros_like(dws_ref)
            df_sum[...] = jnp.zeros_like(df_sum)

        z = z_ref[...]
        th, u, rv, nv, vn, f = _gate_forward(z, gs_ref[...], ws_ref, bias_ref[...])
        dmixed = _dot_nt(dx_ref[...].astype(BF16), wout_ref[...])
        o_tiles = o_ref[...]
        oo = _from_head_tiles(o_tiles)
        ro = _rstd(oo)
        d_o, dgao = _rms_bwd(dmixed[:, :D_ATTN], oo * ro, ro, gao_ref[...])
        do_tiles = _to_head_tiles(d_o)
        do_ref[...] = do_tiles.astype(BF16)
        delta_hi, delta_lo = _hi_lo(jnp.sum(do_tiles * o_tiles, axis=1, keepdims=True))
        for h in range(N_HEADS):
            delta_rows = _shift_rows((HEAD_DIM, tm), delta_hi[h], delta_lo[h])
            dotok_ref[h] = jnp.concatenate([do_tiles[h], delta_rows], axis=0).T.astype(BF16)
        sgu = u * f
        rs = _rstd(sgu)
        dsgu, dgso = _rms_bwd(dmixed[:, D_ATTN:], sgu * rs, rs, gso_ref[...])
        du = dsgu * f
        df = dsgu * u
        lane_group = lax.broadcasted_iota(jnp.int32, (CHUNK, D_SGU), 1) // SGU_GROUP_DIM
        dvns = []
        df_acc = jnp.zeros((CHUNK, D_SGU), F32)
        for c in range(tm // CHUNK):
            dfc32 = df[c * CHUNK:(c + 1) * CHUNK]
            dfc = dfc32.astype(BF16)
            vc = vn[c * CHUNK:(c + 1) * CHUNK]
            dvns.append(_group_select([_dot(wst_ref[g], dfc) for g in range(N_SGU_GROUPS)]))
            for g in range(N_SGU_GROUPS):
                dws_ref[g] += _dot_nt(jnp.where(lane_group == g, dfc, jnp.zeros_like(dfc)), vc)
            df_acc = df_acc + dfc32
        df_sum[...] += df_acc
        dvn = jnp.concatenate(dvns, axis=0) if len(dvns) > 1 else dvns[0]
        dvv, dgs = _rms_bwd(dvn, nv, rv, gs_ref[...])
        dz_ref[...] = (jnp.concatenate([du, dvv], axis=-1) * _gelu_grad(z, th)).astype(BF16)
        dg_ref[0:1, :] += _colsum(dgao)
        dg_ref[1:2, :] += _colsum(dgso)
        dg_ref[2:3, :] += _colsum(dgs)

        @pl.when(step == n_tiles - 1)
        def _():
            dbs_ref[...] = _dot_f32(df_sum[...], ind_ref[...])

    row = lambda n: pl.BlockSpec((tm, n), lambda i: (i, 0))
    return pl.pallas_call(
        functools.partial(body), name="mix_bwd", grid=(n_tiles,),
        in_specs=[row(D_MODEL), row(2 * D_SGU), _head_tile_spec(tm, HEAD_DIM), _full((1, D_SGU)), _full((1, D_ATTN)),
                  _full((1, D_SGU)),
                  _full((N_SGU_GROUPS, CHUNK, CHUNK)), _full((N_SGU_GROUPS, CHUNK, CHUNK)), _full((CHUNK, D_SGU)),
                  _full((D_MODEL, D_MODEL)), _full((D_SGU, LANES))],
        out_specs=[_head_tile_spec(tm, HEAD_DIM), pl.BlockSpec((N_HEADS, None, tm, 2 * HEAD_DIM), lambda i: (0, i, 0, 0)),
                   row(2 * D_SGU), _full((8, D_SGU)),
                   _full((N_SGU_GROUPS, CHUNK, CHUNK)), _full((CHUNK, LANES))],
        out_shape=[jax.ShapeDtypeStruct((N_HEADS, n_tiles, HEAD_DIM, tm), BF16),
                   jax.ShapeDtypeStruct((N_HEADS, n_tiles, tm, 2 * HEAD_DIM), BF16),
                   jax.ShapeDtypeStruct((t, 2 * D_SGU), BF16),
                   jax.ShapeDtypeStruct((8, D_SGU), F32),
                   jax.ShapeDtypeStruct((N_SGU_GROUPS, CHUNK, CHUNK), F32),
                   jax.ShapeDtypeStruct((CHUNK, LANES), F32)],
        scratch_shapes=[pltpu.VMEM((CHUNK, D_SGU), F32)],
        compiler_params=_params(1),
    )(dx2, z, o, g_sgu, g_ao, g_so, ws, ws_t, bias, w_out, group_ind)


def ffn_down_loss(act, wd, x, g, target, tm):
    t = x.shape[0]

    def body(act_ref, wd_ref, x_ref, g_ref, t_ref, loss_ref, dx_ref, dg_ref):
        @pl.when(pl.program_id(0) == 0)
        def _():
            loss_ref[...] = jnp.zeros_like(loss_ref)
            dg_ref[...] = jnp.zeros_like(dg_ref)

        xx = x_ref[...] + 0.5 * _dot(act_ref[...], wd_ref[...])
        r = _rstd(xx)
        n = xx * r
        err = n * g_ref[...] - t_ref[...]
        per_token = jnp.mean(err * err, axis=-1, keepdims=True)
        loss_ref[...] += 0.5 * jnp.sum(per_token, axis=0, keepdims=True)
        dx, dg_rows = _rms_bwd(err * (1.0 / D_MODEL), n, r, g_ref[...])
        dx_ref[...] = dx
        dg_ref[...] += _colsum(dg_rows)

    row = pl.BlockSpec((tm, D_MODEL), lambda i: (i, 0))
    return pl.pallas_call(
        functools.partial(body), name="ffn_down_loss", grid=(t // tm,),
        in_specs=[pl.BlockSpec((tm, D_FF), lambda i: (i, 0)), _full((D_FF, D_MODEL)), row, _full((1, D_MODEL)), row],
        out_specs=[_full((1, LANES)), row, _full((1, D_MODEL))],
        out_shape=[jax.ShapeDtypeStruct((1, LANES), F32), jax.ShapeDtypeStruct((t, D_MODEL), F32),
                   jax.ShapeDtypeStruct((1, D_MODEL), F32)],
        compiler_params=_params(1),
    )(act, wd, x, g, target)


def _rope_tables(t):
    rows = t // GRID_W
    row_idx = jnp.repeat(jnp.arange(rows, dtype=F32), GRID_W)
    col_idx = jnp.tile(jnp.arange(GRID_W, dtype=F32), rows)
    axis_dim = HEAD_DIM // 2
    inv = 1.0 / (ROPE_THETA ** (jnp.arange(0, axis_dim, 2, dtype=F32) / axis_dim))
    ang = jnp.concatenate([row_idx[:, None] * inv, col_idx[:, None] * inv], axis=-1)
    cos = jnp.repeat(jnp.cos(ang), 2, axis=-1)
    sin = jnp.repeat(jnp.sin(ang), 2, axis=-1) * jnp.tile(jnp.array([-1.0, 1.0], F32), HEAD_DIM // 2)
    return jnp.tile(cos, (1, LANES // HEAD_DIM)), jnp.tile(sin, (1, LANES // HEAD_DIM))


def _heads_to_tiles_t(a, n_heads, tile):
    t = a.shape[0]
    return a.reshape(t // tile, tile, n_heads, HEAD_DIM).transpose(2, 0, 3, 1)


def _heads_to_tiles(a, n_heads, tile):
    t = a.shape[0]
    return a.reshape(t // tile, tile, n_heads, HEAD_DIM).transpose(2, 0, 1, 3)


def _tiles_t_to_heads(a):
    h, n, _, tile = a.shape
    return a.transpose(1, 3, 0, 2).reshape(n * tile, h * HEAD_DIM)


def kernel(x, g_ffn1, w1_gate, w1_up, w1_down, g_mix, w_in, g_q, g_k, g_sgu, w_s, b_s, g_attn_out, g_sgu_out, w_out, g_ffn2, w2_gate, w2_up, w2_down, g_final, loss_target, m_g_ffn1, m_w1_gate, m_w1_up, m_w1_down, m_g_mix, m_w_in, m_g_q, m_g_k, m_g_sgu, m_w_s, m_b_s, m_g_attn_out, m_g_sgu_out, m_w_out, m_g_ffn2, m_w2_gate, m_w2_up, m_w2_down, m_g_final, v_g_ffn1, v_w1_gate, v_w1_up, v_w1_down, v_g_mix, v_w_in, v_g_q, v_g_k, v_g_sgu, v_w_s, v_b_s, v_g_attn_out, v_g_sgu_out, v_w_out, v_g_ffn2, v_w2_gate, v_w2_up, v_w2_down, v_g_final):
    weights = dict(g_ffn1=g_ffn1, w1_gate=w1_gate, w1_up=w1_up, w1_down=w1_down, g_mix=g_mix, w_in=w_in, g_q=g_q,
                   g_k=g_k, g_sgu=g_sgu, w_s=w_s, b_s=b_s, g_attn_out=g_attn_out, g_sgu_out=g_sgu_out, w_out=w_out,
                   g_ffn2=g_ffn2, w2_gate=w2_gate, w2_up=w2_up, w2_down=w2_down, g_final=g_final)
    m_in = dict(g_ffn1=m_g_ffn1, w1_gate=m_w1_gate, w1_up=m_w1_up, w1_down=m_w1_down, g_mix=m_g_mix, w_in=m_w_in,
                g_q=m_g_q, g_k=m_g_k, g_sgu=m_g_sgu, w_s=m_w_s, b_s=m_b_s, g_attn_out=m_g_attn_out,
                g_sgu_out=m_g_sgu_out, w_out=m_w_out, g_ffn2=m_g_ffn2, w2_gate=m_w2_gate, w2_up=m_w2_up,
                w2_down=m_w2_down, g_final=m_g_final)
    v_in = dict(g_ffn1=v_g_ffn1, w1_gate=v_w1_gate, w1_up=v_w1_up, w1_down=v_w1_down, g_mix=v_g_mix, w_in=v_w_in,
                g_q=v_g_q, g_k=v_g_k, g_sgu=v_g_sgu, w_s=v_w_s, b_s=v_b_s, g_attn_out=v_g_attn_out,
                g_sgu_out=v_g_sgu_out, w_out=v_w_out, g_ffn2=v_g_ffn2, w2_gate=v_w2_gate, w2_up=v_w2_up,
                w2_down=v_w2_down, g_final=v_g_final)
    names = list(weights)

    t = x.shape[1]
    x0 = x[0]
    target = loss_target[0]
    tm = min(256, t)
    tm_ff = min(256, t)
    tn_ff = 256
    tq = min(512, t)
    tk = min(256, t)
    tk_fwd = min(512, t)
    tk_w = min(2048, t)

    def shard_rows(name):
        w = weights[name][0]
        return (w.T if name in TRANSPOSED else w).astype(BF16)

    rows_of = dict(SHARD_ROWS)
    full = {}

    def packed(group):
        return jnp.concatenate([shard_rows(n) for n in group], axis=0), [rows_of[n] for n in group]

    def gather_of(group):
        return gather_exchange(*packed(group))

    def take(group, gathered):
        for n, g in zip(group, gathered):
            full[n] = g.reshape(N_DEV * rows_of[n], D_MODEL)

    first, second, third = ("w1_gate", "w1_up"), ("w1_down", "w_in", "w_out"), ("w2_gate", "w2_up", "w2_down")
    take(first, gather_two_level(*packed(first), "gather_first"))

    (h1, a1, b1, act1), gathered = ffn_up(x0, g_ffn1, full["w1_gate"], full["w1_up"], tm_ff, tn_ff, gather_of(second))
    take(second, gathered)
    w_in_t = full["w_in"]
    w_qkv_t, w_z_t = w_in_t[:D_QKV], w_in_t[D_QKV:]
    x1 = ffn_down(act1, full["w1_down"], x0, tm)

    qkv, z, h2 = input_projection(x1, g_mix, w_qkv_t, w_z_t, tm)
    cos_w, sin_w = _rope_tables(t)
    gq_w = jnp.tile(g_q, (1, N_HEADS))
    gk_w = jnp.tile(g_k, (1, N_KV_HEADS))
    mean_q, mean_k = _head_mean_matrix(D_ATTN).astype(BF16), _head_mean_matrix(D_KV).astype(BF16)
    q_t, k_tiles, kt_tiles, vt_tiles, vt_tiles_bwd, q_sq_max, k_sq_max = qk_prep(
        qkv, gq_w, gk_w, cos_w, sin_w, mean_q, mean_k, tq, tk, tk_fwd)
    score_bound = 1.02 * jnp.sqrt(q_sq_max.reshape(N_HEADS, HEAD_DIM)[:, 0]
                                  * jnp.repeat(k_sq_max.reshape(N_KV_HEADS, HEAD_DIM)[:, 0], KV_GROUP))
    k_tiles_fwd = k_tiles.reshape(N_KV_HEADS, t // tk_fwd, tk_fwd, HEAD_DIM)
    (o_t, q_tok), gathered = attention_fwd(score_bound, q_t, k_tiles_fwd, vt_tiles, gather_of(third))
    take(third, gathered)

    ws_b = w_s[0].astype(BF16)
    ws_tb = jnp.swapaxes(w_s[0], 1, 2).astype(BF16)
    bias = jnp.repeat(b_s[0].T, SGU_GROUP_DIM, axis=1)
    x2, mixed = mix_out(z, o_t, x1, g_sgu, g_attn_out, g_sgu_out, ws_b, bias, full["w_out"], tq)

    (h3, a2, b2, act2), _ = ffn_up(x2, g_ffn2, full["w2_gate"], full["w2_up"], tm_ff, tn_ff)

    loss_part, dx3, dg_final = ffn_down_loss(act2, full["w2_down"], x2, g_final, target, tm)

    tmm = D_FF // 2
    (da2, db2), _ = ffn_bwd_act(dx3, full["w2_down"], a2, b2, tm_ff, tn_ff)
    (dx2, dg_ffn2), _ = norm_bwd_matmul(da2, full["w2_gate"], db2, full["w2_up"], x2, g_ffn2, dx3, tm)
    dwg2, dwu2 = matmul_tn(da2, h3, 1.0, tmm, tk_w), matmul_tn(db2, h3, 1.0, tmm, tk_w)
    dwd2 = matmul_tn(act2, dx3, 0.5, tmm, tk_w)

    group_ind = (jnp.arange(D_SGU)[:, None] // SGU_GROUP_DIM == jnp.arange(LANES)[None, :]).astype(F32)
    do_t, do_tok, dz, dg_mixrow, dws, dbs = mix_bwd(dx2, z, o_t, g_sgu, g_attn_out, g_sgu_out, ws_b, ws_tb, bias,
                                                    full["w_out"], group_ind, tq)
    dw_out = matmul_tn(mixed, dx2, 1.0, D_MODEL // 2, tk_w)

    group_a = ("w2_gate", "w2_up", "w2_down", "w_out")
    (dq_t, dk_t, dv_t), (parts_a,) = attention_bwd(q_tok, do_tok, q_t, do_t, kt_tiles, vt_tiles_bwd,
                                                   scatter_exchange([dwg2, dwu2, dwd2, dw_out]))
    dqkv, dgq_w, dgk_w = qk_bwd(dq_t, dk_t, dv_t, qkv, gq_w, gk_w, cos_w, sin_w, mean_q, mean_k, tq)

    def pack_small(arrays):
        pieces = []
        for a in arrays:
            flat = a.reshape(-1)
            pieces.append(jnp.pad(flat, (0, (-flat.shape[0]) % (8 * LANES))).reshape(-1, LANES))
        return jnp.concatenate(pieces, axis=0), [p.shape[0] for p in pieces]

    early = dict(g_ffn2=dg_ffn2, g_final=dg_final, g_q=dgq_w.reshape(N_HEADS, HEAD_DIM).sum(0),
                 g_k=dgk_w.reshape(N_KV_HEADS, HEAD_DIM).sum(0), g_attn_out=dg_mixrow[0], g_sgu_out=dg_mixrow[1],
                 g_sgu=dg_mixrow[2], w_s=dws, b_s=dbs[:, :N_SGU_GROUPS].T)
    early_pack, early_rows = pack_small(list(early.values()))
    (dx1, dg_mix), (early_parts,) = norm_bwd_matmul(dqkv, w_qkv_t, dz, w_z_t, x1, g_mix, dx2, tm,
                                                    gather_exchange(early_pack, [early_pack.shape[0]]))
    dw_in = jnp.concatenate([matmul_tn(dqkv, h2, 1.0, D_QKV // 2, tk_w), matmul_tn(dz, h2, 1.0, D_SGU, tk_w)], axis=0)

    dwd1 = matmul_tn(act1, dx1, 0.5, tmm, tk_w)
    group_b = ("w_in", "w1_down")
    (da1, db1), (parts_b,) = ffn_bwd_act(dx1, full["w1_down"], a1, b1, tm_ff, tn_ff, scatter_exchange([dw_in, dwd1]))
    dwg1, dwu1 = matmul_tn(da1, h1, 1.0, tmm, tk_w), matmul_tn(db1, h1, 1.0, tmm, tk_w)
    group_c = ("w1_gate", "w1_up")
    (dx0, dg_ffn1), (parts_c,) = norm_bwd_matmul(da1, full["w1_gate"], db1, full["w1_up"], x0, g_ffn1, dx1, tm,
                                                 scatter_exchange([dwg1, dwu1]))

    late = dict(g_mix=dg_mix, g_ffn1=dg_ffn1, loss=loss_part)
    late_pack, late_rows = pack_small(list(late.values()))
    (late_parts,) = run_exchange(gather_exchange(late_pack, [late_pack.shape[0]]), "gather_late_small_grads")
    small_sums = {}
    for entries, rows, parts in ((early, early_rows, early_parts), (late, late_rows, late_parts)):
        summed = sum_parts(parts, parts.shape[1])
        off = 0
        for n, r in zip(entries, rows):
            small_sums[n] = summed[off:off + r]
            off += r
    loss = small_sums.pop("loss")[0, 0]

    grads = {}
    for group, parts in ((group_a, parts_a), (group_b, parts_b), (group_c, parts_c)):
        summed = sum_parts(parts, 32)
        off = 0
        for n in group:
            gsh = summed[off:off + rows_of[n]]
            grads[n] = (gsh.T if n in TRANSPOSED else gsh)[None]
            off += rows_of[n]
    for n, summed in small_sums.items():
        grads[n] = summed.reshape(-1)[:weights[n].size].reshape(weights[n].shape)

    delta_w, new_m, new_v = {}, {}, {}
    for n in names:
        shape = weights[n].shape
        as2d = (lambda a: a.reshape(-1, shape[-1]))
        d, m2, v2 = adamw(as2d(weights[n]), as2d(grads[n]), as2d(m_in[n]), as2d(v_in[n]))
        delta_w[n], new_m[n], new_v[n] = d.reshape(shape), m2.reshape(shape), v2.reshape(shape)

    return (loss, dx0[None], *[grads[n] for n in names], *[delta_w[n] for n in names],
            *[new_m[n] for n in names], *[new_v[n] for n in names])
```

```python
import functools
import math

import jax
import jax.numpy as jnp
from jax import lax
from jax.experimental import pallas as pl
from jax.experimental.pallas import tpu as pltpu

F32 = jnp.float32
BF16 = jnp.bfloat16

D_MODEL = 1024
D_FF = 2816
N_HEADS = 8
HEAD_DIM = 64
N_KV_HEADS = 2
KV_GROUP = N_HEADS // N_KV_HEADS
D_ATTN = N_HEADS * HEAD_DIM
D_KV = N_KV_HEADS * HEAD_DIM
D_QKV = D_ATTN + 2 * D_KV
N_SGU_GROUPS = 8
SGU_GROUP_DIM = 64
D_SGU = N_SGU_GROUPS * SGU_GROUP_DIM
CHUNK = 128
GRID_W = 64
ROPE_THETA = 10000.0
EPS = 1e-6
N_DEV = 8
LANES = 128

ONES_ROWS = 16
SAFE_SCORE_BOUND = 60.0
LOG2_E = math.log2(math.e)
Q_SCALE = HEAD_DIM ** -0.5 * LOG2_E

ADAM_LR = 0.001
ADAM_B1 = 0.9
ADAM_B2 = 0.999
ADAM_EPS = 1e-08
ADAM_WD = 0.01
ADAM_STEP = 10

MESH_AXES = ("x", "y", "c")
MESH_IDS = pl.DeviceIdType.MESH

VMEM_LIMIT = 56 * 1024 * 1024

SHARD_ROWS = (("w1_gate", D_FF // N_DEV), ("w1_up", D_FF // N_DEV), ("w1_down", D_FF // N_DEV),
              ("w_in", (D_QKV + 2 * D_SGU) // N_DEV), ("w_out", D_MODEL // N_DEV),
              ("w2_gate", D_FF // N_DEV), ("w2_up", D_FF // N_DEV), ("w2_down", D_FF // N_DEV))
PACK_ROWS = sum(r for _, r in SHARD_ROWS)
TRANSPOSED = ("w1_gate", "w1_up", "w_in", "w2_gate", "w2_up")


def _params(n_grid):
    return pltpu.CompilerParams(dimension_semantics=("arbitrary",) * n_grid, vmem_limit_bytes=VMEM_LIMIT)


def _dot(a, b):
    return jnp.dot(a, b, preferred_element_type=F32)


def _dot_nt(a, b):
    return lax.dot_general(a, b, (((1,), (1,)), ((), ())), preferred_element_type=F32)


def _dot_tn(a, b):
    return lax.dot_general(a, b, (((0,), (0,)), ((), ())), preferred_element_type=F32)


def _dot_f32(a, b):
    return jnp.dot(a, b, preferred_element_type=F32, precision=lax.Precision.HIGHEST)


def _dot_split(a, b):
    hi = a.astype(BF16)
    lo = (a - hi.astype(F32)).astype(BF16)
    return _dot(hi, b) + _dot(lo, b)


def _rstd(x):
    return lax.rsqrt(jnp.mean(x * x, axis=-1, keepdims=True) + EPS)


def _rms_bwd(dy, n, r, g):
    dn = dy * g
    return r * (dn - n * jnp.mean(dn * n, axis=-1, keepdims=True)), dy * n


def _colsum(a):
    return jnp.sum(a, axis=0, keepdims=True)


_GELU_C = math.sqrt(2.0 / math.pi)


def _gelu(x):
    t = jnp.tanh(_GELU_C * (x + 0.044715 * (x * x * x)))
    return x * (0.5 * (1.0 + t)), t


def _gelu_grad(x, t):
    return 0.5 * (1.0 + t) + 0.5 * x * (1.0 - t * t) * (_GELU_C * (1.0 + 3 * 0.044715 * x * x))


def _pair_swap(a):
    w = a.shape[-1]
    lane = lax.broadcasted_iota(jnp.int32, a.shape, a.ndim - 1)
    return jnp.where(lane % 2 == 0, pltpu.roll(a, w - 1, a.ndim - 1), pltpu.roll(a, 1, a.ndim - 1))


def _tile_lanes(a, reps):
    return jnp.concatenate([a] * reps, axis=-1) if reps > 1 else a


def _loop_pairs(n, step, carry):
    assert n % 2 == 0, n

    def pair(jj, c):
        return step(2 * jj + 1, 1, step(2 * jj, 0, c))

    return lax.fori_loop(0, n // 2, pair, carry)


def _full(shape):
    nd = len(shape)
    return pl.BlockSpec(shape, lambda *_: (0,) * nd)


def _weight(shape):
    nd = len(shape)
    return pl.BlockSpec(shape, lambda *_: (0,) * nd, pipeline_mode=pl.Buffered(1))


def _mesh_pos():
    return lax.axis_index("x"), lax.axis_index("y"), lax.axis_index("c")


def _peer(pos, d):
    x, y, c = pos
    px = 1 - x if d & 4 else x
    py = 1 - y if d & 2 else y
    pc = 1 - c if d & 1 else c
    return (px, py, pc), 4 * px + 2 * py + pc


class _Exchange:
    def __init__(self, operands, out_shape, n_local, plan):
        self.operands = list(operands)
        self.out_shape = list(out_shape)
        self.sem_shapes = [pltpu.SemaphoreType.DMA((N_DEV - 1,)), pltpu.SemaphoreType.DMA((N_DEV - 1,)),
                           pltpu.SemaphoreType.DMA((n_local,))]
        self._plan = plan

    def _copies(self, in_refs, out_refs):
        pos = _mesh_pos()
        return pos, self._plan(4 * pos[0] + 2 * pos[1] + pos[2], in_refs, out_refs)

    def start(self, in_refs, out_refs, sems):
        send_sems, recv_sems, local_sems = sems
        pos, (local, remote, _) = self._copies(in_refs, out_refs)
        for k, (src, dst) in enumerate(local):
            pltpu.make_async_copy(src, dst, local_sems.at[k]).start()
        for d in range(1, N_DEV):
            peer, peer_lin = _peer(pos, d)
            for src, dst in remote(peer_lin):
                pltpu.make_async_remote_copy(src_ref=src, dst_ref=dst, send_sem=send_sems.at[d - 1],
                                             recv_sem=recv_sems.at[d - 1], device_id=peer,
                                             device_id_type=MESH_IDS).start()

    def wait(self, in_refs, out_refs, sems):
        send_sems, recv_sems, local_sems = sems
        pos, (local, _, whole) = self._copies(in_refs, out_refs)
        for d in range(1, N_DEV):
            peer, peer_lin = _peer(pos, d)
            ref = whole(peer_lin)
            everything = pltpu.make_async_remote_copy(src_ref=ref, dst_ref=ref, send_sem=send_sems.at[d - 1],
                                                      recv_sem=recv_sems.at[d - 1], device_id=peer,
                                                      device_id_type=MESH_IDS)
            everything.wait_send()
            everything.wait_recv()
        for k, (src, dst) in enumerate(local):
            pltpu.make_async_copy(src, dst, local_sems.at[k]).wait()


def _offsets(rows):
    offs, o = [], 0
    for r in rows:
        offs.append(o)
        o += r
    return offs


def gather_exchange(src, rows):
    offs = _offsets(rows)

    def plan(me, in_refs, out_refs):
        pieces = [(in_refs[0].at[pl.ds(o, r)], out.at[me]) for o, r, out in zip(offs, rows, out_refs)]
        return pieces, (lambda peer_lin: pieces), (lambda peer_lin: in_refs[0])

    return _Exchange([src], [jax.ShapeDtypeStruct((N_DEV, r) + src.shape[1:], src.dtype) for r in rows],
                     len(rows), plan)


def scatter_exchange(grads):
    rows = [g.shape[0] // N_DEV for g in grads]
    offs = _offsets(rows)

    def plan(me, in_refs, out_refs):
        parts = out_refs[0]

        def slabs(owner):
            return [(g.at[pl.ds(pl.multiple_of(owner * r, 16), r)], parts.at[me, pl.ds(o, r)])
                    for g, o, r in zip(in_refs, offs, rows)]

        return slabs(me), slabs, (lambda peer_lin: parts.at[peer_lin])

    shape = jax.ShapeDtypeStruct((N_DEV, sum(rows)) + grads[0].shape[1:], grads[0].dtype)
    return _Exchange(grads, [shape], len(rows), plan)


def gather_two_level(src, rows, name):
    offs = _offsets(rows)
    n_p = len(rows)

    def body(src_ref, *refs):
        outs, (send_sems, recv_sems, local_sems) = refs[:n_p], refs[n_p:]
        x, y, c = _mesh_pos()
        me, sibling = (x, y, c), (x, y, 1 - c)
        chips = [(1 - x, y), (x, 1 - y), (1 - x, 1 - y)]

        def slab(w, dev):
            return outs[w].at[4 * dev[0] + 2 * dev[1] + dev[2]]

        def copy(w, k, block, to, from_src=False):
            return pltpu.make_async_remote_copy(
                src_ref=src_ref.at[pl.ds(offs[w], rows[w])] if from_src else slab(w, block), dst_ref=slab(w, block),
                send_sem=send_sems.at[w * 7 + k], recv_sem=recv_sems.at[w * 7 + k],
                device_id=to, device_id_type=MESH_IDS)

        mine = [pltpu.make_async_copy(src_ref.at[pl.ds(offs[w], rows[w])], slab(w, me), local_sems.at[w])
                for w in range(n_p)]
        for cp in mine:
            cp.start()
        first = []
        for w in range(n_p):
            first.append(copy(w, 0, me, sibling, True))
            first += [copy(w, 1 + j, me, (*chip, c), True) for j, chip in enumerate(chips)]
        for cp in first:
            cp.start()
        passed = []
        for j, chip in enumerate(chips):
            for w in range(n_p):
                copy(w, 1 + j, (*chip, c), me).wait_recv()
                cp = copy(w, 4 + j, (*chip, c), sibling)
                cp.start()
                passed.append(cp)
        for w in range(n_p):
            copy(w, 0, sibling, me).wait_recv()
            for j, chip in enumerate(chips):
                copy(w, 4 + j, (*chip, 1 - c), me).wait_recv()
        for cp in first + passed:
            cp.wait_send()
        for cp in mine:
            cp.wait()

    any_spec = pl.BlockSpec(memory_space=pl.ANY)
    return pl.pallas_call(
        functools.partial(body), name=name,
        out_shape=[jax.ShapeDtypeStruct((N_DEV, r) + src.shape[1:], src.dtype) for r in rows],
        in_specs=[any_spec], out_specs=[any_spec] * n_p,
        scratch_shapes=[pltpu.SemaphoreType.DMA((7 * n_p,)), pltpu.SemaphoreType.DMA((7 * n_p,)),
                        pltpu.SemaphoreType.DMA((n_p,))],
        compiler_params=pltpu.CompilerParams(has_side_effects=True),
    )(src)


def run_exchange(ex, name):
    n_in, n_out = len(ex.operands), len(ex.out_shape)

    def body(*refs):
        parts = refs[:n_in], refs[n_in:n_in + n_out], refs[n_in + n_out:]
        ex.start(*parts)
        ex.wait(*parts)

    any_spec = pl.BlockSpec(memory_space=pl.ANY)
    return pl.pallas_call(
        functools.partial(body), name=name, out_shape=ex.out_shape,
        in_specs=[any_spec] * n_in, out_specs=[any_spec] * n_out, scratch_shapes=ex.sem_shapes,
        compiler_params=pltpu.CompilerParams(has_side_effects=True),
    )(*ex.operands)


def _pallas(comm, body, *, name, grid, in_specs, out_specs, out_shape, args, scratch_shapes=()):
    params = _params(len(grid))
    if comm is None:
        res = pl.pallas_call(functools.partial(body), name=name, grid=grid, in_specs=list(in_specs),
                             out_specs=list(out_specs), out_shape=list(out_shape),
                             scratch_shapes=list(scratch_shapes), compiler_params=params)(*args)
        return list(res), []
    n_in, n_out, n_scr = len(in_specs), len(out_specs), len(scratch_shapes)
    c_in, c_out = len(comm.operands), len(comm.out_shape)

    def edge(last):
        conds = [pl.program_id(a) == (g - 1 if last else 0) for a, g in enumerate(grid)]
        return functools.reduce(jnp.logical_and, conds)

    def wrapped(*refs):
        refs = list(refs)
        ins, refs = refs[:n_in], refs[n_in:]
        cins, refs = refs[:c_in], refs[c_in:]
        outs, refs = refs[:n_out], refs[n_out:]
        couts, refs = refs[:c_out], refs[c_out:]
        scr, sems = refs[:n_scr], refs[n_scr:]

        @pl.when(edge(False))
        def _():
            comm.start(cins, couts, sems)

        body(*ins, *outs, *scr)

        @pl.when(edge(True))
        def _():
            comm.wait(cins, couts, sems)

    any_spec = pl.BlockSpec(memory_space=pl.ANY)
    res = pl.pallas_call(
        wrapped, name=name, grid=grid,
        in_specs=list(in_specs) + [any_spec] * c_in, out_specs=list(out_specs) + [any_spec] * c_out,
        out_shape=list(out_shape) + comm.out_shape, scratch_shapes=list(scratch_shapes) + comm.sem_shapes,
        compiler_params=pltpu.CompilerParams(dimension_semantics=("arbitrary",) * len(grid),
                                             vmem_limit_bytes=VMEM_LIMIT, has_side_effects=True),
    )(*args, *comm.operands)
    return res[:n_out], res[n_out:]


def sum_parts(parts, block_rows):
    n, rows, cols = parts.shape

    def body(p_ref, o_ref):
        acc = p_ref[0].astype(F32)
        for s in range(1, n):
            acc = acc + p_ref[s].astype(F32)
        o_ref[...] = acc

    return pl.pallas_call(
        functools.partial(body), name="sum_parts",
        grid=(rows // block_rows,),
        in_specs=[pl.BlockSpec((n, block_rows, cols), lambda i: (0, i, 0))],
        out_specs=pl.BlockSpec((block_rows, cols), lambda i: (i, 0)),
        out_shape=jax.ShapeDtypeStruct((rows, cols), F32),
        compiler_params=_params(1),
    )(parts)


def adamw(w, g, m, v):
    def body(w_ref, g_ref, m_ref, v_ref, d_ref, m_out, v_out):
        gg = g_ref[...]
        m2 = ADAM_B1 * m_ref[...] + (1.0 - ADAM_B1) * gg
        v2 = ADAM_B2 * v_ref[...] + (1.0 - ADAM_B2) * (gg * gg)
        m_hat = m2 / (1.0 - ADAM_B1 ** ADAM_STEP)
        v_hat = v2 / (1.0 - ADAM_B2 ** ADAM_STEP)
        d_ref[...] = -ADAM_LR * (m_hat / (jnp.sqrt(v_hat) + ADAM_EPS) + ADAM_WD * w_ref[...])
        m_out[...] = m2
        v_out[...] = v2

    spec = _full(w.shape)
    shape = jax.ShapeDtypeStruct(w.shape, F32)
    return pl.pallas_call(
        functools.partial(body), name="adamw",
        in_specs=[spec] * 4, out_specs=[spec] * 3, out_shape=[shape] * 3,
        compiler_params=pltpu.CompilerParams(vmem_limit_bytes=VMEM_LIMIT),
    )(w, g, m, v)


def ffn_up(x, g, wg_t, wu_t, tm, tn, comm=None):
    t = x.shape[0]

    def body(x_ref, g_ref, wg_ref, wu_ref, h_ref, silu_ref, dgate_ref, act_ref):
        xx = x_ref[...]
        h = ((xx * _rstd(xx)) * g_ref[...]).astype(BF16)
        h_ref[...] = h
        for c in range(D_FF // tn):
            cols = slice(c * tn, (c + 1) * tn)
            a = _dot_nt(h, wg_ref[cols, :])
            b = _dot_nt(h, wu_ref[cols, :])
            sig = 0.5 * jnp.tanh(0.5 * a) + 0.5
            silu = a * sig
            silu_ref[:, cols] = silu.astype(BF16)
            dgate_ref[:, cols] = (b * (sig + silu * (1.0 - sig))).astype(BF16)
            act_ref[:, cols] = (silu * b).astype(BF16)

    wide = jax.ShapeDtypeStruct((t, D_FF), BF16)
    row = lambda n: pl.BlockSpec((tm, n), lambda i: (i, 0))
    return _pallas(
        comm, body, name="ffn_up",
        grid=(t // tm,),
        in_specs=[row(D_MODEL), _full((1, D_MODEL)), _weight((D_FF, D_MODEL)), _weight((D_FF, D_MODEL))],
        out_specs=[row(D_MODEL), row(D_FF), row(D_FF), row(D_FF)],
        out_shape=[jax.ShapeDtypeStruct((t, D_MODEL), BF16), wide, wide, wide],
        args=(x, g, wg_t, wu_t))


def ffn_down(act, wd, x, tm):
    t = x.shape[0]

    def body(act_ref, wd_ref, x_ref, o_ref):
        o_ref[...] = x_ref[...] + 0.5 * _dot(act_ref[...], wd_ref[...])

    return pl.pallas_call(
        functools.partial(body), name="ffn_down",
        grid=(t // tm,),
        in_specs=[pl.BlockSpec((tm, D_FF), lambda i: (i, 0)), _weight((D_FF, D_MODEL)),
                  pl.BlockSpec((tm, D_MODEL), lambda i: (i, 0))],
        out_specs=pl.BlockSpec((tm, D_MODEL), lambda i: (i, 0)),
        out_shape=jax.ShapeDtypeStruct((t, D_MODEL), F32),
        compiler_params=_params(1),
    )(act, wd, x)


def ffn_bwd_act(dx, wd, silu, dgate, tm, tn, comm=None):
    t = dx.shape[0]

    def body(dx_ref, wd_ref, silu_ref, dgate_ref, da_ref, db_ref):
        dxb = (0.5 * dx_ref[...]).astype(BF16)
        for c in range(D_FF // tn):
            cols = slice(c * tn, (c + 1) * tn)
            dact = _dot_nt(dxb, wd_ref[cols, :])
            da_ref[:, cols] = (dact * dgate_ref[:, cols].astype(F32)).astype(BF16)
            db_ref[:, cols] = (dact * silu_ref[:, cols].astype(F32)).astype(BF16)

    wide = jax.ShapeDtypeStruct((t, D_FF), BF16)
    row = lambda n: pl.BlockSpec((tm, n), lambda i: (i, 0))
    return _pallas(
        comm, body, name="ffn_bwd_act",
        grid=(t // tm,),
        in_specs=[row(D_MODEL), _weight((D_FF, D_MODEL)), row(D_FF), row(D_FF)],
        out_specs=[row(D_FF), row(D_FF)],
        out_shape=[wide, wide],
        args=(dx, wd, silu, dgate))


def norm_bwd_matmul(a1, w1, a2, w2, x, g, dx_in, tm, comm=None):
    t = x.shape[0]
    k1, k2 = a1.shape[1], a2.shape[1]

    def body(a1_ref, w1_ref, a2_ref, w2_ref, x_ref, g_ref, dxin_ref, dx_ref, dg_ref):
        dh = _dot(a1_ref[...], w1_ref[...]) + _dot(a2_ref[...], w2_ref[...])
        xx = x_ref[...]
        r = _rstd(xx)
        dx, dg_rows = _rms_bwd(dh, xx * r, r, g_ref[...])
        dx_ref[...] = dxin_ref[...] + dx

        @pl.when(pl.program_id(0) == 0)
        def _():
            dg_ref[...] = jnp.zeros_like(dg_ref)

        dg_ref[...] += _colsum(dg_rows)

    row = pl.BlockSpec((tm, D_MODEL), lambda i: (i, 0))
    return _pallas(
        comm, body, name="norm_bwd_matmul",
        grid=(t // tm,),
        in_specs=[pl.BlockSpec((tm, k1), lambda i: (i, 0)), _weight((k1, D_MODEL)),
                  pl.BlockSpec((tm, k2), lambda i: (i, 0)), _weight((k2, D_MODEL)),
                  row, _full((1, D_MODEL)), row],
        out_specs=[row, _full((1, D_MODEL))],
        out_shape=[jax.ShapeDtypeStruct((t, D_MODEL), F32), jax.ShapeDtypeStruct((1, D_MODEL), F32)],
        args=(a1, w1, a2, w2, x, g, dx_in))


def matmul_tn(a, b, scale, tmm, tk):
    t, m = a.shape
    n = b.shape[1]
    nk = t // tk

    def body(a_ref, b_ref, o_ref, acc_ref):
        k = pl.program_id(1)

        @pl.when(k == 0)
        def _():
            acc_ref[...] = jnp.zeros_like(acc_ref)

        acc_ref[...] += _dot_tn(a_ref[...].astype(BF16), b_ref[...].astype(BF16))

        @pl.when(k == nk - 1)
        def _():
            o_ref[...] = (scale * acc_ref[...]).astype(BF16)

    return pl.pallas_call(
        functools.partial(body), name="matmul_tn",
        grid=(m // tmm, nk),
        in_specs=[pl.BlockSpec((tk, tmm), lambda i, k: (k, i)), pl.BlockSpec((tk, n), lambda i, k: (k, 0))],
        out_specs=pl.BlockSpec((tmm, n), lambda i, k: (i, 0)),
        out_shape=jax.ShapeDtypeStruct((m, n), BF16),
        scratch_shapes=[pltpu.VMEM((tmm, n), F32)],
        compiler_params=_params(2),
    )(a, b)


def input_projection(x, g, w_qkv_t, w_z_t, tm):
    t = x.shape[0]

    def body(x_ref, g_ref, wq_ref, wz_ref, qkv_ref, z_ref, h_ref):
        xx = x_ref[...]
        h = ((xx * _rstd(xx)) * g_ref[...]).astype(BF16)
        h_ref[...] = h
        qkv_ref[...] = _dot_nt(h, wq_ref[...])
        z_ref[...] = _dot_nt(h, wz_ref[...])

    row = lambda n: pl.BlockSpec((tm, n), lambda i: (i, 0))
    return pl.pallas_call(
        functools.partial(body), name="input_projection", grid=(t // tm,),
        in_specs=[row(D_MODEL), _full((1, D_MODEL)), _weight((D_QKV, D_MODEL)), _weight((2 * D_SGU, D_MODEL))],
        out_specs=[row(D_QKV), row(2 * D_SGU), row(D_MODEL)],
        out_shape=[jax.ShapeDtypeStruct((t, D_QKV), F32), jax.ShapeDtypeStruct((t, 2 * D_SGU), F32),
                   jax.ShapeDtypeStruct((t, D_MODEL), BF16)],
        compiler_params=_params(1))(x, g, w_qkv_t, w_z_t)


def _shift_rows(shape, first, second):
    row = lax.broadcasted_iota(jnp.int32, shape, len(shape) - 2)
    return jnp.where(row == 0, first, jnp.where(row == 1, second, 0.0))


def _hi_lo(a):
    hi = a.astype(BF16).astype(F32)
    return hi, a - hi


def _head_tile_spec(tm, rows):
    return pl.BlockSpec((N_HEADS, None, rows, tm), lambda i: (0, i, 0, 0))


def _to_head_tiles(a):
    return a.T.reshape(N_HEADS, HEAD_DIM, a.shape[0])


def _from_head_tiles(a):
    return a.reshape(D_ATTN, a.shape[-1]).T


def _head_mean_matrix(width):
    head = jnp.arange(width) // HEAD_DIM
    return (head[:, None] == head[None, :]).astype(F32) / HEAD_DIM


def _kv_tile_spec(n_sub, rows, cols):
    return pl.BlockSpec((N_KV_HEADS, n_sub, rows, cols), lambda i: (0, i, 0, 0))


def qk_prep(qkv, gq_w, gk_w, cos_w, sin_w, mean_q, mean_k, tm, tk, tk_v):
    t = qkv.shape[0]
    n_sub, n_sub_v = tm // tk, tm // tk_v

    def body(p_ref, gq_ref, gk_ref, cos_ref, sin_ref, mq_ref, mk_ref, q_ref, k_ref, kt_ref, vt_ref, vtb_ref,
             qmax_ref, kmax_ref):
        @pl.when(pl.program_id(0) == 0)
        def _():
            qmax_ref[...] = jnp.zeros_like(qmax_ref)
            kmax_ref[...] = jnp.zeros_like(kmax_ref)

        cos2, sin2 = cos_ref[...], sin_ref[...]
        q = p_ref[:, :D_ATTN]
        k = p_ref[:, D_ATTN:D_ATTN + D_KV]
        qn = q * lax.rsqrt(_dot_split(q * q, mq_ref[...]) + EPS) * gq_ref[...]
        kn = k * lax.rsqrt(_dot_split(k * k, mk_ref[...]) + EPS) * gk_ref[...]
        cos8, sin8 = _tile_lanes(cos2, D_ATTN // LANES), _tile_lanes(sin2, D_ATTN // LANES)
        q_rot = (qn * cos8 + _pair_swap(qn) * sin8) * Q_SCALE
        q_ref[...] = _to_head_tiles(q_rot).astype(BF16)
        k_rot = kn * cos2 + _pair_swap(kn) * sin2
        q_sq = HEAD_DIM * _dot_split(q_rot * q_rot, mq_ref[...])
        k_sq = HEAD_DIM * _dot_split(k_rot * k_rot, mk_ref[...])
        qmax_ref[...] = jnp.maximum(qmax_ref[...], jnp.max(q_sq, axis=0, keepdims=True))
        kmax_ref[...] = jnp.maximum(kmax_ref[...], jnp.max(k_sq, axis=0, keepdims=True))
        vv = p_ref[:, D_ATTN + D_KV:]
        second = pltpu.roll(k_rot, HEAD_DIM, 1)
        for c in range(n_sub):
            rows = slice(c * tk, (c + 1) * tk)
            k_ref[0, c] = k_rot[rows, :HEAD_DIM].astype(BF16)
            k_ref[1, c] = second[rows, :HEAD_DIM].astype(BF16)
        for a, feat_ref, width, n in ((k_rot, kt_ref, tk, n_sub), (vv, vtb_ref, tk, n_sub), (vv, vt_ref, tk_v, n_sub_v)):
            for c in range(n):
                tile = a[c * width:(c + 1) * width].T.reshape(N_KV_HEADS, HEAD_DIM, width)
                feat_ref[:, c, :HEAD_DIM, :] = tile.astype(BF16)
        vt_ref[:, :, HEAD_DIM:, :] = jnp.ones((N_KV_HEADS, n_sub_v, ONES_ROWS, tk_v), BF16)
        minus = _shift_rows((N_KV_HEADS, n_sub, HEAD_DIM, tk), -1.0, -1.0).astype(BF16)
        kt_ref[:, :, HEAD_DIM:, :] = minus
        vtb_ref[:, :, HEAD_DIM:, :] = minus

    kv = lambda rows, cols: jax.ShapeDtypeStruct((N_KV_HEADS, t // tk, rows, cols), BF16)
    return pl.pallas_call(
        functools.partial(body), name="qk_prep", grid=(t // tm,),
        in_specs=[pl.BlockSpec((tm, D_QKV), lambda i: (i, 0)), _full((1, D_ATTN)), _full((1, D_KV)),
                  pl.BlockSpec((tm, LANES), lambda i: (i, 0)), pl.BlockSpec((tm, LANES), lambda i: (i, 0)),
                  _full((D_ATTN, D_ATTN)), _full((D_KV, D_KV))],
        out_specs=[_head_tile_spec(tm, HEAD_DIM), _kv_tile_spec(n_sub, tk, HEAD_DIM),
                   _kv_tile_spec(n_sub, 2 * HEAD_DIM, tk),
                   _kv_tile_spec(n_sub_v, HEAD_DIM + ONES_ROWS, tk_v), _kv_tile_spec(n_sub, 2 * HEAD_DIM, tk),
                   _full((1, D_ATTN)), _full((1, D_KV))],
        out_shape=[jax.ShapeDtypeStruct((N_HEADS, t // tm, HEAD_DIM, tm), BF16), kv(tk, HEAD_DIM), kv(2 * HEAD_DIM, tk),
                   jax.ShapeDtypeStruct((N_KV_HEADS, t // tk_v, HEAD_DIM + ONES_ROWS, tk_v), BF16),
                   kv(2 * HEAD_DIM, tk),
                   jax.ShapeDtypeStruct((1, D_ATTN), F32), jax.ShapeDtypeStruct((1, D_KV), F32)],
        compiler_params=_params(1),
    )(qkv, gq_w, gk_w, cos_w, sin_w, mean_q, mean_k)


def qk_bwd(dq_rot, dk_rot, dv, qkv, gq_w, gk_w, cos_w, sin_w, mean_q, mean_k, tm):
    t = qkv.shape[0]
    tk = dk_rot.shape[-1]
    n_sub = tm // tk

    def token_major(ref):
        return jnp.concatenate([ref[:, c].reshape(D_KV, tk).T for c in range(n_sub)], axis=0)

    def branch(raw, d_rot, gain, mean_mat, cos, sin, scale):
        r = lax.rsqrt(_dot_split(raw * raw, mean_mat) + EPS)
        n = raw * r
        dy = (d_rot * cos - _pair_swap(d_rot) * sin) * scale
        dn = dy * gain
        return r * (dn - n * _dot_split(dn * n, mean_mat)), dy * n

    def body(dq_ref, dk_ref, dv_ref, p_ref, gq_ref, gk_ref, cos_ref, sin_ref, mq_ref, mk_ref,
             dp_ref, dgq_ref, dgk_ref):
        cos2, sin2 = cos_ref[...], sin_ref[...]
        cos8, sin8 = _tile_lanes(cos2, D_ATTN // LANES), _tile_lanes(sin2, D_ATTN // LANES)
        dq, dgq = branch(p_ref[:, :D_ATTN], _from_head_tiles(dq_ref[...]), gq_ref[...], mq_ref[...], cos8, sin8,
                         HEAD_DIM ** -0.5)
        dk, dgk = branch(p_ref[:, D_ATTN:D_ATTN + D_KV], token_major(dk_ref), gk_ref[...], mk_ref[...], cos2, sin2, 1.0)
        dp_ref[...] = jnp.concatenate([dq, dk, token_major(dv_ref)], axis=-1).astype(BF16)

        @pl.when(pl.program_id(0) == 0)
        def _():
            dgq_ref[...] = jnp.zeros_like(dgq_ref)
            dgk_ref[...] = jnp.zeros_like(dgk_ref)

        dgq_ref[...] += _colsum(dgq)
        dgk_ref[...] += _colsum(dgk)

    return pl.pallas_call(
        functools.partial(body), name="qk_bwd", grid=(t // tm,),
        in_specs=[_head_tile_spec(tm, HEAD_DIM), _kv_tile_spec(n_sub, HEAD_DIM, tk),
                  _kv_tile_spec(n_sub, HEAD_DIM, tk), pl.BlockSpec((tm, D_QKV), lambda i: (i, 0)),
                  _full((1, D_ATTN)), _full((1, D_KV)),
                  pl.BlockSpec((tm, LANES), lambda i: (i, 0)), pl.BlockSpec((tm, LANES), lambda i: (i, 0)),
                  _full((D_ATTN, D_ATTN)), _full((D_KV, D_KV))],
        out_specs=[pl.BlockSpec((tm, D_QKV), lambda i: (i, 0)), _full((1, D_ATTN)), _full((1, D_KV))],
        out_shape=[jax.ShapeDtypeStruct((t, D_QKV), BF16), jax.ShapeDtypeStruct((1, D_ATTN), F32),
                   jax.ShapeDtypeStruct((1, D_KV), F32)],
        compiler_params=_params(1),
    )(dq_rot, dk_rot, dv, qkv, gq_w, gk_w, cos_w, sin_w, mean_q, mean_k)


def attention_fwd(bound, q_t, k, v_t, comm=None):
    _, nq, _, tq = q_t.shape
    _, nk, tk, _ = k.shape

    def body(bound_ref, q_ref, k_ref, v_ref, o_ref, qtok_ref, s_scr, p_scr):
        head_bound = bound_ref[pl.program_id(0)]
        safe = head_bound <= SAFE_SCORE_BOUND
        q = q_ref[...]
        s_scr[0] = _dot(k_ref[0], q)
        p_scr[1] = jnp.zeros((tk, tq), BF16)
        zero = jnp.zeros((HEAD_DIM + ONES_ROWS, tq), F32)

        def matmuls(j, slot):
            pv = _dot(v_ref[jnp.maximum(j - 1, 0)], p_scr[1 - slot])
            s_scr[1 - slot] = _dot(k_ref[jnp.minimum(j + 1, nk - 1)], q)
            return pv

        def finish(m, acc):
            acc = acc + _dot(v_ref[nk - 1], p_scr[(nk - 1) % 2])
            l = acc[HEAD_DIM:HEAD_DIM + 1]
            o_ref[...] = acc[:HEAD_DIM] / l
            lse_rows = _shift_rows((HEAD_DIM, tq), *_hi_lo(m + jnp.log2(l)))
            qtok_ref[...] = jnp.concatenate([q.astype(F32), lse_rows], axis=0).T.astype(BF16)

        @pl.when(safe)
        def _():
            m = jnp.full((1, tq), head_bound, F32)

            def step(j, slot, acc):
                s = s_scr[slot]
                pv = matmuls(j, slot)
                p_scr[slot] = jnp.exp2(s - m).astype(BF16)
                return acc + pv

            finish(m, _loop_pairs(nk, step, zero))

        @pl.when(jnp.logical_not(safe))
        def _():
            def step(j, slot, carry):
                m, acc = carry
                s = s_scr[slot]
                pv = matmuls(j, slot)
                m_new = jnp.maximum(m, jnp.max(s, axis=0, keepdims=True))
                p_scr[slot] = jnp.exp2(s - m_new).astype(BF16)
                return m_new, jnp.exp2(m - m_new) * (acc + pv)

            finish(*_loop_pairs(nk, step, (jnp.full((1, tq), -1e30, F32), zero)))

    return _pallas(
        comm, body, name="attention_fwd", grid=(N_HEADS, nq),
        in_specs=[pl.BlockSpec(memory_space=pltpu.SMEM),
                  pl.BlockSpec((None, None, HEAD_DIM, tq), lambda h, i: (h, i, 0, 0)),
                  pl.BlockSpec((None, nk, tk, HEAD_DIM), lambda h, i: (h // KV_GROUP, 0, 0, 0)),
                  pl.BlockSpec((None, nk, HEAD_DIM + ONES_ROWS, tk), lambda h, i: (h // KV_GROUP, 0, 0, 0))],
        out_specs=[pl.BlockSpec((None, None, HEAD_DIM, tq), lambda h, i: (h, i, 0, 0)),
                   pl.BlockSpec((None, None, tq, 2 * HEAD_DIM), lambda h, i: (h, i, 0, 0))],
        out_shape=[jax.ShapeDtypeStruct((N_HEADS, nq, HEAD_DIM, tq), F32),
                   jax.ShapeDtypeStruct((N_HEADS, nq, tq, 2 * HEAD_DIM), BF16)],
        scratch_shapes=[pltpu.VMEM((2, tk, tq), F32), pltpu.VMEM((2, tk, tq), BF16)],
        args=(bound, q_t, k, v_t))


def attention_bwd(q_tok, do_tok, q_t, do_t, k_t, v_t, comm=None):
    _, nq, _, tq = q_t.shape
    _, nk, _, tk = k_t.shape

    def body(qtok_ref, dotok_ref, q_ref, do_ref, kt_ref, vt_ref, dq_ref, dk_ref, dv_ref,
             s_scr, dp_scr, p_scr, ds_scr):
        @pl.when(pl.program_id(1) == 0)
        def _():
            dq_ref[...] = jnp.zeros_like(dq_ref)

        kt_aug, vt_aug = kt_ref[...], vt_ref[...]
        kt = kt_aug[:HEAD_DIM]
        n = KV_GROUP * nq
        s_scr[0] = _dot(qtok_ref[0, 0], kt_aug)
        dp_scr[0] = _dot(dotok_ref[0, 0], vt_aug)
        p_scr[1] = jnp.zeros((tq, tk), BF16)
        ds_scr[1] = jnp.zeros((tq, tk), BF16)

        def products(t, slot, dk, dv):
            h, i = t // nq, t % nq
            ds = ds_scr[slot]
            dq_ref[h, i] += _dot_nt(kt, ds)
            return dk + _dot(q_ref[h, i], ds), dv + _dot(do_ref[h, i], p_scr[slot])

        def step(t, slot, carry):
            s, dp = s_scr[slot], dp_scr[slot]
            dk, dv = products(jnp.maximum(t - 1, 0), 1 - slot, *carry)
            nxt = jnp.minimum(t + 1, n - 1)
            s_scr[1 - slot] = _dot(qtok_ref[nxt // nq, nxt % nq], kt_aug)
            dp_scr[1 - slot] = _dot(dotok_ref[nxt // nq, nxt % nq], vt_aug)
            p = jnp.exp2(s)
            p_scr[slot] = p.astype(BF16)
            ds_scr[slot] = (p * dp).astype(BF16)
            return dk, dv

        zero = jnp.zeros((HEAD_DIM, tk), F32)
        dk, dv = products(n - 1, (n - 1) % 2, *_loop_pairs(n, step, (zero, zero)))
        dk_ref[...] = dk * (1.0 / LOG2_E)
        dv_ref[...] = dv

    group = lambda g, j: (g, 0, 0, 0)
    tile = lambda g, j: (g, j, 0, 0)
    once = pl.Buffered(1)
    return _pallas(
        comm, body, name="attention_bwd", grid=(N_KV_HEADS, nk),
        in_specs=[pl.BlockSpec((KV_GROUP, nq, tq, 2 * HEAD_DIM), group, pipeline_mode=once),
                  pl.BlockSpec((KV_GROUP, nq, tq, 2 * HEAD_DIM), group, pipeline_mode=once),
                  pl.BlockSpec((KV_GROUP, nq, HEAD_DIM, tq), group, pipeline_mode=once),
                  pl.BlockSpec((KV_GROUP, nq, HEAD_DIM, tq), group, pipeline_mode=once),
                  pl.BlockSpec((None, None, 2 * HEAD_DIM, tk), tile),
                  pl.BlockSpec((None, None, 2 * HEAD_DIM, tk), tile)],
        out_specs=[pl.BlockSpec((KV_GROUP, nq, HEAD_DIM, tq), group),
                   pl.BlockSpec((None, None, HEAD_DIM, tk), tile),
                   pl.BlockSpec((None, None, HEAD_DIM, tk), tile)],
        out_shape=[jax.ShapeDtypeStruct((N_HEADS, nq, HEAD_DIM, tq), F32),
                   jax.ShapeDtypeStruct((N_KV_HEADS, nk, HEAD_DIM, tk), F32),
                   jax.ShapeDtypeStruct((N_KV_HEADS, nk, HEAD_DIM, tk), F32)],
        scratch_shapes=[pltpu.VMEM((2, tq, tk), F32), pltpu.VMEM((2, tq, tk), F32),
                        pltpu.VMEM((2, tq, tk), BF16), pltpu.VMEM((2, tq, tk), BF16)],
        args=(q_tok, do_tok, q_t, do_t, k_t, v_t))


def _group_select(parts):
    lane_group = lax.broadcasted_iota(jnp.int32, parts[0].shape, 1) // SGU_GROUP_DIM
    out = parts[0]
    for g in range(1, N_SGU_GROUPS):
        out = jnp.where(lane_group == g, parts[g], out)
    return out


def _gate_forward(z, g_sgu, ws_ref, bias):
    gz, th = _gelu(z)
    u, vv = gz[:, :D_SGU], gz[:, D_SGU:]
    rv = _rstd(vv)
    nv = vv * rv
    vn = (nv * g_sgu).astype(BF16)
    fs = []
    for c in range(z.shape[0] // CHUNK):
        vc = vn[c * CHUNK:(c + 1) * CHUNK]
        fs.append(_group_select([_dot(ws_ref[g], vc) for g in range(N_SGU_GROUPS)]) + bias)
    f = jnp.concatenate(fs, axis=0) if len(fs) > 1 else fs[0]
    return th, u, rv, nv, vn, f


def mix_out(z, o, x, g_sgu, g_ao, g_so, ws, bias, w_out, tm):
    t = x.shape[0]

    def body(z_ref, o_ref, x_ref, gs_ref, gao_ref, gso_ref, ws_ref, bias_ref, wout_ref, x2_ref, mixed_ref):
        _, u, _, _, _, f = _gate_forward(z_ref[...], gs_ref[...], ws_ref, bias_ref[...])
        sgu = u * f
        oo = _from_head_tiles(o_ref[...])
        mixed = jnp.concatenate([oo * _rstd(oo) * gao_ref[...], sgu * _rstd(sgu) * gso_ref[...]], axis=-1).astype(BF16)
        mixed_ref[...] = mixed
        x2_ref[...] = x_ref[...] + _dot(mixed, wout_ref[...])

    row = lambda n: pl.BlockSpec((tm, n), lambda i: (i, 0))
    return pl.pallas_call(
        functools.partial(body), name="mix_out", grid=(t // tm,),
        in_specs=[row(2 * D_SGU), _head_tile_spec(tm, HEAD_DIM), row(D_MODEL), _full((1, D_SGU)), _full((1, D_ATTN)),
                  _full((1, D_SGU)),
                  _full((N_SGU_GROUPS, CHUNK, CHUNK)), _full((CHUNK, D_SGU)), _full((D_MODEL, D_MODEL))],
        out_specs=[row(D_MODEL), row(D_MODEL)],
        out_shape=[jax.ShapeDtypeStruct((t, D_MODEL), F32), jax.ShapeDtypeStruct((t, D_MODEL), BF16)],
        compiler_params=_params(1),
    )(z, o, x, g_sgu, g_ao, g_so, ws, bias, w_out)


def mix_bwd(dx2, z, o, g_sgu, g_ao, g_so, ws, ws_t, bias, w_out, group_ind, tm):
    t = dx2.shape[0]
    n_tiles = t // tm

    def body(dx_ref, z_ref, o_ref, gs_ref, gao_ref, gso_ref, ws_ref, wst_ref, bias_ref, wout_ref, ind_ref,
             do_ref, dotok_ref, dz_ref, dg_ref, dws_ref, dbs_ref, df_sum):
        step = pl.program_id(0)

        @pl.when(step == 0)
        def _():
            dg_ref[...] = jnp.zeros_like(dg_ref)
            dws_ref[...] = jnp.zeros_like(dws_ref)
            df_sum[...] = jnp.zeros_like(df_sum)

        z = z_ref[...]
        th, u, rv, nv, vn, f = _gate_forward(z, gs_ref[...], ws_ref, bias_ref[...])
        dmixed = _dot_nt(dx_ref[...].astype(BF16), wout_ref[...])
        o_tiles = o_ref[...]
        oo = _from_head_tiles(o_tiles)
        ro = _rstd(oo)
        d_o, dgao = _rms_bwd(dmixed[:, :D_ATTN], oo * ro, ro, gao_ref[...])
        do_tiles = _to_head_tiles(d_o)
        do_ref[...] = do_tiles.astype(BF16)
        delta_hi, delta_lo = _hi_lo(jnp.sum(do_tiles * o_tiles, axis=1, keepdims=True))
        for h in range(N_HEADS):
            delta_rows = _shift_rows((HEAD_DIM, tm), delta_hi[h], delta_lo[h])
            dotok_ref[h] = jnp.concatenate([do_tiles[h], delta_rows], axis=0).T.astype(BF16)
        sgu = u * f
        rs = _rstd(sgu)
        dsgu, dgso = _rms_bwd(dmixed[:, D_ATTN:], sgu * rs, rs, gso_ref[...])
        du = dsgu * f
        df = dsgu * u
        lane_group = lax.broadcasted_iota(jnp.int32, (CHUNK, D_SGU), 1) // SGU_GROUP_DIM
        dvns = []
        df_acc = jnp.zeros((CHUNK, D_SGU), F32)
        for c in range(tm // CHUNK):
            dfc32 = df[c * CHUNK:(c + 1) * CHUNK]
            dfc = dfc32.astype(BF16)
            vc = vn[c * CHUNK:(c + 1) * CHUNK]
            dvns.append(_group_select([_dot(wst_ref[g], dfc) for g in range(N_SGU_GROUPS)]))
            for g in range(N_SGU_GROUPS):
                dws_ref[g] += _dot_nt(jnp.where(lane_group == g, dfc, jnp.zeros_like(dfc)), vc)
            df_acc = df_acc + dfc32
        df_sum[...] += df_acc
        dvn = jnp.concatenate(dvns, axis=0) if len(dvns) > 1 else dvns[0]
        dvv, dgs = _rms_bwd(dvn, nv, rv, gs_ref[...])
        dz_ref[...] = (jnp.concatenate([du, dvv], axis=-1) * _gelu_grad(z, th)).astype(BF16)
        dg_ref[0:1, :] += _colsum(dgao)
        dg_ref[1:2, :] += _colsum(dgso)
        dg_ref[2:3, :] += _colsum(dgs)

        @pl.when(step == n_tiles - 1)
        def _():
            dbs_ref[...] = _dot_f32(df_sum[...], ind_ref[...])

    row = lambda n: pl.BlockSpec((tm, n), lambda i: (i, 0))
    return pl.pallas_call(
        functools.partial(body), name="mix_bwd", grid=(n_tiles,),
        in_specs=[row(D_MODEL), row(2 * D_SGU), _head_tile_spec(tm, HEAD_DIM), _full((1, D_SGU)), _full((1, D_ATTN)),
                  _full((1, D_SGU)),
                  _full((N_SGU_GROUPS, CHUNK, CHUNK)), _full((N_SGU_GROUPS, CHUNK, CHUNK)), _full((CHUNK, D_SGU)),
                  _full((D_MODEL, D_MODEL)), _full((D_SGU, LANES))],
        out_specs=[_head_tile_spec(tm, HEAD_DIM), pl.BlockSpec((N_HEADS, None, tm, 2 * HEAD_DIM), lambda i: (0, i, 0, 0)),
                   row(2 * D_SGU), _full((8, D_SGU)),
                   _full((N_SGU_GROUPS, CHUNK, CHUNK)), _full((CHUNK, LANES))],
        out_shape=[jax.ShapeDtypeStruct((N_HEADS, n_tiles, HEAD_DIM, tm), BF16),
                   jax.ShapeDtypeStruct((N_HEADS, n_tiles, tm, 2 * HEAD_DIM), BF16),
                   jax.ShapeDtypeStruct((t, 2 * D_SGU), BF16),
                   jax.ShapeDtypeStruct((8, D_SGU), F32),
                   jax.ShapeDtypeStruct((N_SGU_GROUPS, CHUNK, CHUNK), F32),
                   jax.ShapeDtypeStruct((CHUNK, LANES), F32)],
        scratch_shapes=[pltpu.VMEM((CHUNK, D_SGU), F32)],
        compiler_params=_params(1),
    )(dx2, z, o, g_sgu, g_ao, g_so, ws, ws_t, bias, w_out, group_ind)


def ffn_down_loss(act, wd, x, g, target, tm):
    t = x.shape[0]

    def body(act_ref, wd_ref, x_ref, g_ref, t_ref, loss_ref, dx_ref, dg_ref):
        @pl.when(pl.program_id(0) == 0)
        def _():
            loss_ref[...] = jnp.zeros_like(loss_ref)
            dg_ref[...] = jnp.zeros_like(dg_ref)

        xx = x_ref[...] + 0.5 * _dot(act_ref[...], wd_ref[...])
        r = _rstd(xx)
        n = xx * r
        err = n * g_ref[...] - t_ref[...]
        per_token = jnp.mean(err * err, axis=-1, keepdims=True)
        loss_ref[...] += 0.5 * jnp.sum(per_token, axis=0, keepdims=True)
        dx, dg_rows = _rms_bwd(err * (1.0 / D_MODEL), n, r, g_ref[...])
        dx_ref[...] = dx
        dg_ref[...] += _colsum(dg_rows)

    row = pl.BlockSpec((tm, D_MODEL), lambda i: (i, 0))
    return pl.pallas_call(
        functools.partial(body), name="ffn_down_loss", grid=(t // tm,),
        in_specs=[pl.BlockSpec((tm, D_FF), lambda i: (i, 0)), _weight((D_FF, D_MODEL)), row, _full((1, D_MODEL)), row],
        out_specs=[_full((1, LANES)), row, _full((1, D_MODEL))],
        out_shape=[jax.ShapeDtypeStruct((1, LANES), F32), jax.ShapeDtypeStruct((t, D_MODEL), F32),
                   jax.ShapeDtypeStruct((1, D_MODEL), F32)],
        compiler_params=_params(1),
    )(act, wd, x, g, target)


def _rope_tables(t):
    rows = t // GRID_W
    row_idx = jnp.repeat(jnp.arange(rows, dtype=F32), GRID_W)
    col_idx = jnp.tile(jnp.arange(GRID_W, dtype=F32), rows)
    axis_dim = HEAD_DIM // 2
    inv = 1.0 / (ROPE_THETA ** (jnp.arange(0, axis_dim, 2, dtype=F32) / axis_dim))
    ang = jnp.concatenate([row_idx[:, None] * inv, col_idx[:, None] * inv], axis=-1)
    cos = jnp.repeat(jnp.cos(ang), 2, axis=-1)
    sin = jnp.repeat(jnp.sin(ang), 2, axis=-1) * jnp.tile(jnp.array([-1.0, 1.0], F32), HEAD_DIM // 2)
    return jnp.tile(cos, (1, LANES // HEAD_DIM)), jnp.tile(sin, (1, LANES // HEAD_DIM))


def _heads_to_tiles_t(a, n_heads, tile):
    t = a.shape[0]
    return a.reshape(t // tile, tile, n_heads, HEAD_DIM).transpose(2, 0, 3, 1)


def _heads_to_tiles(a, n_heads, tile):
    t = a.shape[0]
    return a.reshape(t // tile, tile, n_heads, HEAD_DIM).transpose(2, 0, 1, 3)


def _tiles_t_to_heads(a):
    h, n, _, tile = a.shape
    return a.transpose(1, 3, 0, 2).reshape(n * tile, h * HEAD_DIM)


def kernel(x, g_ffn1, w1_gate, w1_up, w1_down, g_mix, w_in, g_q, g_k, g_sgu, w_s, b_s, g_attn_out, g_sgu_out, w_out, g_ffn2, w2_gate, w2_up, w2_down, g_final, loss_target, m_g_ffn1, m_w1_gate, m_w1_up, m_w1_down, m_g_mix, m_w_in, m_g_q, m_g_k, m_g_sgu, m_w_s, m_b_s, m_g_attn_out, m_g_sgu_out, m_w_out, m_g_ffn2, m_w2_gate, m_w2_up, m_w2_down, m_g_final, v_g_ffn1, v_w1_gate, v_w1_up, v_w1_down, v_g_mix, v_w_in, v_g_q, v_g_k, v_g_sgu, v_w_s, v_b_s, v_g_attn_out, v_g_sgu_out, v_w_out, v_g_ffn2, v_w2_gate, v_w2_up, v_w2_down, v_g_final):
    weights = dict(g_ffn1=g_ffn1, w1_gate=w1_gate, w1_up=w1_up, w1_down=w1_down, g_mix=g_mix, w_in=w_in, g_q=g_q,
                   g_k=g_k, g_sgu=g_sgu, w_s=w_s, b_s=b_s, g_attn_out=g_attn_out, g_sgu_out=g_sgu_out, w_out=w_out,
                   g_ffn2=g_ffn2, w2_gate=w2_gate, w2_up=w2_up, w2_down=w2_down, g_final=g_final)
    m_in = dict(g_ffn1=m_g_ffn1, w1_gate=m_w1_gate, w1_up=m_w1_up, w1_down=m_w1_down, g_mix=m_g_mix, w_in=m_w_in,
                g_q=m_g_q, g_k=m_g_k, g_sgu=m_g_sgu, w_s=m_w_s, b_s=m_b_s, g_attn_out=m_g_attn_out,
                g_sgu_out=m_g_sgu_out, w_out=m_w_out, g_ffn2=m_g_ffn2, w2_gate=m_w2_gate, w2_up=m_w2_up,
                w2_down=m_w2_down, g_final=m_g_final)
    v_in = dict(g_ffn1=v_g_ffn1, w1_gate=v_w1_gate, w1_up=v_w1_up, w1_down=v_w1_down, g_mix=v_g_mix, w_in=v_w_in,
                g_q=v_g_q, g_k=v_g_k, g_sgu=v_g_sgu, w_s=v_w_s, b_s=v_b_s, g_attn_out=v_g_attn_out,
                g_sgu_out=v_g_sgu_out, w_out=v_w_out, g_ffn2=v_g_ffn2, w2_gate=v_w2_gate, w2_up=v_w2_up,
                w2_down=v_w2_down, g_final=v_g_final)
    names = list(weights)

    t = x.shape[1]
    x0 = x[0]
    target = loss_target[0]
    tm = min(256, t)
    tm_ff = min(512, t)
    tn_ff = 256
    tq = min(512, t)
    tk = min(256, t)
    tk_fwd = min(512, t)
    tk_w = min(2048, t)

    def shard_rows(name):
        w = weights[name][0]
        return (w.T if name in TRANSPOSED else w).astype(BF16)

    rows_of = dict(SHARD_ROWS)
    full = {}

    def packed(group):
        return jnp.concatenate([shard_rows(n) for n in group], axis=0), [rows_of[n] for n in group]

    def gather_of(group):
        return gather_exchange(*packed(group))

    def take(group, gathered):
        for n, g in zip(group, gathered):
            full[n] = g.reshape(N_DEV * rows_of[n], D_MODEL)

    first, second, third = ("w1_gate", "w1_up"), ("w1_down", "w_in", "w_out"), ("w2_gate", "w2_up", "w2_down")
    take(first, gather_two_level(*packed(first), "gather_first"))

    (h1, a1, b1, act1), gathered = ffn_up(x0, g_ffn1, full["w1_gate"], full["w1_up"], tm_ff, tn_ff, gather_of(second))
    take(second, gathered)
    w_in_t = full["w_in"]
    w_qkv_t, w_z_t = w_in_t[:D_QKV], w_in_t[D_QKV:]
    x1 = ffn_down(act1, full["w1_down"], x0, tm)

    qkv, z, h2 = input_projection(x1, g_mix, w_qkv_t, w_z_t, tm)
    cos_w, sin_w = _rope_tables(t)
    gq_w = jnp.tile(g_q, (1, N_HEADS))
    gk_w = jnp.tile(g_k, (1, N_KV_HEADS))
    mean_q, mean_k = _head_mean_matrix(D_ATTN).astype(BF16), _head_mean_matrix(D_KV).astype(BF16)
    q_t, k_tiles, kt_tiles, vt_tiles, vt_tiles_bwd, q_sq_max, k_sq_max = qk_prep(
        qkv, gq_w, gk_w, cos_w, sin_w, mean_q, mean_k, tq, tk, tk_fwd)
    score_bound = 1.02 * jnp.sqrt(q_sq_max.reshape(N_HEADS, HEAD_DIM)[:, 0]
                                  * jnp.repeat(k_sq_max.reshape(N_KV_HEADS, HEAD_DIM)[:, 0], KV_GROUP))
    k_tiles_fwd = k_tiles.reshape(N_KV_HEADS, t // tk_fwd, tk_fwd, HEAD_DIM)
    (o_t, q_tok), gathered = attention_fwd(score_bound, q_t, k_tiles_fwd, vt_tiles, gather_of(third))
    take(third, gathered)

    ws_b = w_s[0].astype(BF16)
    ws_tb = jnp.swapaxes(w_s[0], 1, 2).astype(BF16)
    bias = jnp.repeat(b_s[0].T, SGU_GROUP_DIM, axis=1)
    x2, mixed = mix_out(z, o_t, x1, g_sgu, g_attn_out, g_sgu_out, ws_b, bias, full["w_out"], tq)

    (h3, a2, b2, act2), _ = ffn_up(x2, g_ffn2, full["w2_gate"], full["w2_up"], tm_ff, tn_ff)

    loss_part, dx3, dg_final = ffn_down_loss(act2, full["w2_down"], x2, g_final, target, tm)

    tmm = D_FF // 2
    (da2, db2), _ = ffn_bwd_act(dx3, full["w2_down"], a2, b2, tm_ff, tn_ff)
    (dx2, dg_ffn2), _ = norm_bwd_matmul(da2, full["w2_gate"], db2, full["w2_up"], x2, g_ffn2, dx3, tm)
    dwg2, dwu2 = matmul_tn(da2, h3, 1.0, tmm, tk_w), matmul_tn(db2, h3, 1.0, tmm, tk_w)
    dwd2 = matmul_tn(act2, dx3, 0.5, tmm, tk_w)

    group_ind = (jnp.arange(D_SGU)[:, None] // SGU_GROUP_DIM == jnp.arange(LANES)[None, :]).astype(F32)
    do_t, do_tok, dz, dg_mixrow, dws, dbs = mix_bwd(dx2, z, o_t, g_sgu, g_attn_out, g_sgu_out, ws_b, ws_tb, bias,
                                                    full["w_out"], group_ind, tq)
    dw_out = matmul_tn(mixed, dx2, 1.0, D_MODEL // 2, tk_w)

    group_a = ("w2_gate", "w2_up", "w2_down", "w_out")
    (dq_t, dk_t, dv_t), (parts_a,) = attention_bwd(q_tok, do_tok, q_t, do_t, kt_tiles, vt_tiles_bwd,
                                                   scatter_exchange([dwg2, dwu2, dwd2, dw_out]))
    dqkv, dgq_w, dgk_w = qk_bwd(dq_t, dk_t, dv_t, qkv, gq_w, gk_w, cos_w, sin_w, mean_q, mean_k, tq)

    def pack_small(arrays):
        pieces = []
        for a in arrays:
            flat = a.reshape(-1)
            pieces.append(jnp.pad(flat, (0, (-flat.shape[0]) % (8 * LANES))).reshape(-1, LANES))
        return jnp.concatenate(pieces, axis=0), [p.shape[0] for p in pieces]

    early = dict(g_ffn2=dg_ffn2, g_final=dg_final, g_q=dgq_w.reshape(N_HEADS, HEAD_DIM).sum(0),
                 g_k=dgk_w.reshape(N_KV_HEADS, HEAD_DIM).sum(0), g_attn_out=dg_mixrow[0], g_sgu_out=dg_mixrow[1],
                 g_sgu=dg_mixrow[2], w_s=dws, b_s=dbs[:, :N_SGU_GROUPS].T)
    early_pack, early_rows = pack_small(list(early.values()))
    (dx1, dg_mix), (early_parts,) = norm_bwd_matmul(dqkv, w_qkv_t, dz, w_z_t, x1, g_mix, dx2, tm,
                                                    gather_exchange(early_pack, [early_pack.shape[0]]))
    dw_in = jnp.concatenate([matmul_tn(dqkv, h2, 1.0, D_QKV // 2, tk_w), matmul_tn(dz, h2, 1.0, D_SGU, tk_w)], axis=0)

    dwd1 = matmul_tn(act1, dx1, 0.5, tmm, tk_w)
    group_b = ("w_in", "w1_down")
    (da1, db1), (parts_b,) = ffn_bwd_act(dx1, full["w1_down"], a1, b1, tm_ff, tn_ff, scatter_exchange([dw_in, dwd1]))
    dwg1, dwu1 = matmul_tn(da1, h1, 1.0, tmm, tk_w), matmul_tn(db1, h1, 1.0, tmm, tk_w)
    group_c = ("w1_gate", "w1_up")
    (dx0, dg_ffn1), (parts_c,) = norm_bwd_matmul(da1, full["w1_gate"], db1, full["w1_up"], x0, g_ffn1, dx1, tm,
                                                 scatter_exchange([dwg1, dwu1]))

    late = dict(g_mix=dg_mix, g_ffn1=dg_ffn1, loss=loss_part)
    late_pack, late_rows = pack_small(list(late.values()))
    (late_parts,) = run_exchange(gather_exchange(late_pack, [late_pack.shape[0]]), "gather_late_small_grads")
    small_sums = {}
    for entries, rows, parts in ((early, early_rows, early_parts), (late, late_rows, late_parts)):
        summed = sum_parts(parts, parts.shape[1])
        off = 0
        for n, r in zip(entries, rows):
            small_sums[n] = summed[off:off + r]
            off += r
    loss = small_sums.pop("loss")[0, 0]

    grads = {}
    for group, parts in ((group_a, parts_a), (group_b, parts_b), (group_c, parts_c)):
        summed = sum_parts(parts, 32)
        off = 0
        for n in group:
            gsh = summed[off:off + rows_of[n]]
            grads[n] = (gsh.T if n in TRANSPOSED else gsh)[None]
            off += rows_of[n]
    for n, summed in small_sums.items():
        grads[n] = summed.reshape(-1)[:weights[n].size].reshape(weights[n].shape)

    delta_w, new_m, new_v = {}, {}, {}
    for n in names:
        shape = weights[n].shape
        as2d = (lambda a: a.reshape(-1, shape[-1]))
        d, m2, v2 = adamw(as2d(weights[n]), as2d(grads[n]), as2d(m_in[n]), as2d(v_in[n]))
        delta_w[n], new_m[n], new_v[n] = d.reshape(shape), m2.reshape(shape), v2.reshape(shape)

    return (loss, dx0[None], *[grads[n] for n in names], *[delta_w[n] for n in names],
            *[new_m[n] for n in names], *[new_v[n] for n in names])
```

```python
import functools
import math

import jax
import jax.numpy as jnp
from jax import lax
from jax.experimental import pallas as pl
from jax.experimental.pallas import tpu as pltpu

F32 = jnp.float32
BF16 = jnp.bfloat16

D_MODEL = 1024
D_FF = 2816
N_HEADS = 8
HEAD_DIM = 64
N_KV_HEADS = 2
KV_GROUP = N_HEADS // N_KV_HEADS
D_ATTN = N_HEADS * HEAD_DIM
D_KV = N_KV_HEADS * HEAD_DIM
D_QKV = D_ATTN + 2 * D_KV
N_SGU_GROUPS = 8
SGU_GROUP_DIM = 64
D_SGU = N_SGU_GROUPS * SGU_GROUP_DIM
CHUNK = 128
GRID_W = 64
ROPE_THETA = 10000.0
EPS = 1e-6
N_DEV = 8
LANES = 128

ONES_ROWS = 16
SAFE_SCORE_BOUND = 60.0
LOG2_E = math.log2(math.e)
Q_SCALE = HEAD_DIM ** -0.5 * LOG2_E

ADAM_LR = 0.001
ADAM_B1 = 0.9
ADAM_B2 = 0.999
ADAM_EPS = 1e-08
ADAM_WD = 0.01
ADAM_STEP = 10

MESH_AXES = ("x", "y", "c")
MESH_IDS = pl.DeviceIdType.MESH

VMEM_LIMIT = 56 * 1024 * 1024

SHARD_ROWS = (("w1_gate", D_FF // N_DEV), ("w1_up", D_FF // N_DEV), ("w1_down", D_FF // N_DEV),
              ("w_in", (D_QKV + 2 * D_SGU) // N_DEV), ("w_out", D_MODEL // N_DEV),
              ("w2_gate", D_FF // N_DEV), ("w2_up", D_FF // N_DEV), ("w2_down", D_FF // N_DEV))
PACK_ROWS = sum(r for _, r in SHARD_ROWS)
TRANSPOSED = ("w1_gate", "w1_up", "w_in", "w2_gate", "w2_up")


def _params(n_grid):
    return pltpu.CompilerParams(dimension_semantics=("arbitrary",) * n_grid, vmem_limit_bytes=VMEM_LIMIT)


def _dot(a, b):
    return jnp.dot(a, b, preferred_element_type=F32)


def _dot_nt(a, b):
    return lax.dot_general(a, b, (((1,), (1,)), ((), ())), preferred_element_type=F32)


def _dot_tn(a, b):
    return lax.dot_general(a, b, (((0,), (0,)), ((), ())), preferred_element_type=F32)


def _dot_f32(a, b):
    return jnp.dot(a, b, preferred_element_type=F32, precision=lax.Precision.HIGHEST)


def _dot_split(a, b):
    hi = a.astype(BF16)
    lo = (a - hi.astype(F32)).astype(BF16)
    return _dot(hi, b) + _dot(lo, b)


def _rstd(x):
    return lax.rsqrt(jnp.mean(x * x, axis=-1, keepdims=True) + EPS)


def _rms_bwd(dy, n, r, g):
    dn = dy * g
    return r * (dn - n * jnp.mean(dn * n, axis=-1, keepdims=True)), dy * n


def _colsum(a):
    return jnp.sum(a, axis=0, keepdims=True)


_GELU_C = math.sqrt(2.0 / math.pi)


def _gelu(x):
    t = jnp.tanh(_GELU_C * (x + 0.044715 * (x * x * x)))
    return x * (0.5 * (1.0 + t)), t


def _gelu_grad(x, t):
    return 0.5 * (1.0 + t) + 0.5 * x * (1.0 - t * t) * (_GELU_C * (1.0 + 3 * 0.044715 * x * x))


def _pair_swap(a):
    w = a.shape[-1]
    lane = lax.broadcasted_iota(jnp.int32, a.shape, a.ndim - 1)
    return jnp.where(lane % 2 == 0, pltpu.roll(a, w - 1, a.ndim - 1), pltpu.roll(a, 1, a.ndim - 1))


def _tile_lanes(a, reps):
    return jnp.concatenate([a] * reps, axis=-1) if reps > 1 else a


def _loop_pairs(n, step, carry):
    assert n % 2 == 0, n

    def pair(jj, c):
        return step(2 * jj + 1, 1, step(2 * jj, 0, c))

    return lax.fori_loop(0, n // 2, pair, carry)


def _full(shape):
    nd = len(shape)
    return pl.BlockSpec(shape, lambda *_: (0,) * nd)


def _mesh_pos():
    return lax.axis_index("x"), lax.axis_index("y"), lax.axis_index("c")


def _peer(pos, d):
    x, y, c = pos
    px = 1 - x if d & 4 else x
    py = 1 - y if d & 2 else y
    pc = 1 - c if d & 1 else c
    return (px, py, pc), 4 * px + 2 * py + pc


class _Exchange:
    def __init__(self, operands, out_shape, n_local, plan):
        self.operands = list(operands)
        self.out_shape = list(out_shape)
        self.sem_shapes = [pltpu.SemaphoreType.DMA((N_DEV - 1,)), pltpu.SemaphoreType.DMA((N_DEV - 1,)),
                           pltpu.SemaphoreType.DMA((n_local,))]
        self._plan = plan

    def _copies(self, in_refs, out_refs):
        pos = _mesh_pos()
        return pos, self._plan(4 * pos[0] + 2 * pos[1] + pos[2], in_refs, out_refs)

    def start(self, in_refs, out_refs, sems):
        send_sems, recv_sems, local_sems = sems
        pos, (local, remote, _) = self._copies(in_refs, out_refs)
        for k, (src, dst) in enumerate(local):
            pltpu.make_async_copy(src, dst, local_sems.at[k]).start()
        for d in range(1, N_DEV):
            peer, peer_lin = _peer(pos, d)
            for src, dst in remote(peer_lin):
                pltpu.make_async_remote_copy(src_ref=src, dst_ref=dst, send_sem=send_sems.at[d - 1],
                                             recv_sem=recv_sems.at[d - 1], device_id=peer,
                                             device_id_type=MESH_IDS).start()

    def wait(self, in_refs, out_refs, sems):
        send_sems, recv_sems, local_sems = sems
        pos, (local, _, whole) = self._copies(in_refs, out_refs)
        for d in range(1, N_DEV):
            peer, peer_lin = _peer(pos, d)
            ref = whole(peer_lin)
            everything = pltpu.make_async_remote_copy(src_ref=ref, dst_ref=ref, send_sem=send_sems.at[d - 1],
                                                      recv_sem=recv_sems.at[d - 1], device_id=peer,
                                                      device_id_type=MESH_IDS)
            everything.wait_send()
            everything.wait_recv()
        for k, (src, dst) in enumerate(local):
            pltpu.make_async_copy(src, dst, local_sems.at[k]).wait()


def _offsets(rows):
    offs, o = [], 0
    for r in rows:
        offs.append(o)
        o += r
    return offs


def gather_exchange(src, rows):
    offs = _offsets(rows)

    def plan(me, in_refs, out_refs):
        pieces = [(in_refs[0].at[pl.ds(o, r)], out.at[me]) for o, r, out in zip(offs, rows, out_refs)]
        return pieces, (lambda peer_lin: pieces), (lambda peer_lin: in_refs[0])

    return _Exchange([src], [jax.ShapeDtypeStruct((N_DEV, r) + src.shape[1:], src.dtype) for r in rows],
                     len(rows), plan)


def scatter_exchange(grads):
    rows = [g.shape[0] // N_DEV for g in grads]
    offs = _offsets(rows)

    def plan(me, in_refs, out_refs):
        parts = out_refs[0]

        def slabs(owner):
            return [(g.at[pl.ds(pl.multiple_of(owner * r, 16), r)], parts.at[me, pl.ds(o, r)])
                    for g, o, r in zip(in_refs, offs, rows)]

        return slabs(me), slabs, (lambda peer_lin: parts.at[peer_lin])

    shape = jax.ShapeDtypeStruct((N_DEV, sum(rows)) + grads[0].shape[1:], grads[0].dtype)
    return _Exchange(grads, [shape], len(rows), plan)


def gather_two_level(src, rows, name):
    offs = _offsets(rows)
    n_p = len(rows)

    def body(src_ref, *refs):
        outs, (send_sems, recv_sems, local_sems) = refs[:n_p], refs[n_p:]
        x, y, c = _mesh_pos()
        me, sibling = (x, y, c), (x, y, 1 - c)
        chips = [(1 - x, y), (x, 1 - y), (1 - x, 1 - y)]

        def slab(w, dev):
            return outs[w].at[4 * dev[0] + 2 * dev[1] + dev[2]]

        def copy(w, k, block, to, from_src=False):
            return pltpu.make_async_remote_copy(
                src_ref=src_ref.at[pl.ds(offs[w], rows[w])] if from_src else slab(w, block), dst_ref=slab(w, block),
                send_sem=send_sems.at[w * 7 + k], recv_sem=recv_sems.at[w * 7 + k],
                device_id=to, device_id_type=MESH_IDS)

        mine = [pltpu.make_async_copy(src_ref.at[pl.ds(offs[w], rows[w])], slab(w, me), local_sems.at[w])
                for w in range(n_p)]
        for cp in mine:
            cp.start()
        first = []
        for w in range(n_p):
            first.append(copy(w, 0, me, sibling, True))
            first += [copy(w, 1 + j, me, (*chip, c), True) for j, chip in enumerate(chips)]
        for cp in first:
            cp.start()
        passed = []
        for j, chip in enumerate(chips):
            for w in range(n_p):
                copy(w, 1 + j, (*chip, c), me).wait_recv()
                cp = copy(w, 4 + j, (*chip, c), sibling)
                cp.start()
                passed.append(cp)
        for w in range(n_p):
            copy(w, 0, sibling, me).wait_recv()
            for j, chip in enumerate(chips):
                copy(w, 4 + j, (*chip, 1 - c), me).wait_recv()
        for cp in first + passed:
            cp.wait_send()
        for cp in mine:
            cp.wait()

    any_spec = pl.BlockSpec(memory_space=pl.ANY)
    return pl.pallas_call(
        functools.partial(body), name=name,
        out_shape=[jax.ShapeDtypeStruct((N_DEV, r) + src.shape[1:], src.dtype) for r in rows],
        in_specs=[any_spec], out_specs=[any_spec] * n_p,
        scratch_shapes=[pltpu.SemaphoreType.DMA((7 * n_p,)), pltpu.SemaphoreType.DMA((7 * n_p,)),
                        pltpu.SemaphoreType.DMA((n_p,))],
        compiler_params=pltpu.CompilerParams(has_side_effects=True),
    )(src)


def run_exchange(ex, name):
    n_in, n_out = len(ex.operands), len(ex.out_shape)

    def body(*refs):
        parts = refs[:n_in], refs[n_in:n_in + n_out], refs[n_in + n_out:]
        ex.start(*parts)
        ex.wait(*parts)

    any_spec = pl.BlockSpec(memory_space=pl.ANY)
    return pl.pallas_call(
        functools.partial(body), name=name, out_shape=ex.out_shape,
        in_specs=[any_spec] * n_in, out_specs=[any_spec] * n_out, scratch_shapes=ex.sem_shapes,
        compiler_params=pltpu.CompilerParams(has_side_effects=True),
    )(*ex.operands)


def _pallas(comm, body, *, name, grid, in_specs, out_specs, out_shape, args, scratch_shapes=()):
    params = _params(len(grid))
    if comm is None:
        res = pl.pallas_call(functools.partial(body), name=name, grid=grid, in_specs=list(in_specs),
                             out_specs=list(out_specs), out_shape=list(out_shape),
                             scratch_shapes=list(scratch_shapes), compiler_params=params)(*args)
        return list(res), []
    n_in, n_out, n_scr = len(in_specs), len(out_specs), len(scratch_shapes)
    c_in, c_out = len(comm.operands), len(comm.out_shape)

    def edge(last):
        conds = [pl.program_id(a) == (g - 1 if last else 0) for a, g in enumerate(grid)]
        return functools.reduce(jnp.logical_and, conds)

    def wrapped(*refs):
        refs = list(refs)
        ins, refs = refs[:n_in], refs[n_in:]
        cins, refs = refs[:c_in], refs[c_in:]
        outs, refs = refs[:n_out], refs[n_out:]
        couts, refs = refs[:c_out], refs[c_out:]
        scr, sems = refs[:n_scr], refs[n_scr:]

        body(*ins, *outs, *scr)

        @pl.when(edge(False))
        def _():
            comm.start(cins, couts, sems)

        @pl.when(edge(True))
        def _():
            comm.wait(cins, couts, sems)

    any_spec = pl.BlockSpec(memory_space=pl.ANY)
    res = pl.pallas_call(
        wrapped, name=name, grid=grid,
        in_specs=list(in_specs) + [any_spec] * c_in, out_specs=list(out_specs) + [any_spec] * c_out,
        out_shape=list(out_shape) + comm.out_shape, scratch_shapes=list(scratch_shapes) + comm.sem_shapes,
        compiler_params=pltpu.CompilerParams(dimension_semantics=("arbitrary",) * len(grid),
                                             vmem_limit_bytes=VMEM_LIMIT, has_side_effects=True),
    )(*args, *comm.operands)
    return res[:n_out], res[n_out:]


def sum_parts(parts, block_rows):
    n, rows, cols = parts.shape

    def body(p_ref, o_ref):
        acc = p_ref[0].astype(F32)
        for s in range(1, n):
            acc = acc + p_ref[s].astype(F32)
        o_ref[...] = acc

    return pl.pallas_call(
        functools.partial(body), name="sum_parts",
        grid=(rows // block_rows,),
        in_specs=[pl.BlockSpec((n, block_rows, cols), lambda i: (0, i, 0))],
        out_specs=pl.BlockSpec((block_rows, cols), lambda i: (i, 0)),
        out_shape=jax.ShapeDtypeStruct((rows, cols), F32),
        compiler_params=_params(1),
    )(parts)


def adamw(w, g, m, v):
    def body(w_ref, g_ref, m_ref, v_ref, d_ref, m_out, v_out):
        gg = g_ref[...]
        m2 = ADAM_B1 * m_ref[...] + (1.0 - ADAM_B1) * gg
        v2 = ADAM_B2 * v_ref[...] + (1.0 - ADAM_B2) * (gg * gg)
        m_hat = m2 / (1.0 - ADAM_B1 ** ADAM_STEP)
        v_hat = v2 / (1.0 - ADAM_B2 ** ADAM_STEP)
        d_ref[...] = -ADAM_LR * (m_hat / (jnp.sqrt(v_hat) + ADAM_EPS) + ADAM_WD * w_ref[...])
        m_out[...] = m2
        v_out[...] = v2

    spec = _full(w.shape)
    shape = jax.ShapeDtypeStruct(w.shape, F32)
    return pl.pallas_call(
        functools.partial(body), name="adamw",
        in_specs=[spec] * 4, out_specs=[spec] * 3, out_shape=[shape] * 3,
        compiler_params=pltpu.CompilerParams(vmem_limit_bytes=VMEM_LIMIT),
    )(w, g, m, v)


def ffn_up(x, g, wg_t, wu_t, tm, tn, comm=None):
    t = x.shape[0]

    def body(x_ref, g_ref, wg_ref, wu_ref, h_ref, silu_ref, dgate_ref, act_ref):
        xx = x_ref[...]
        h = ((xx * _rstd(xx)) * g_ref[...]).astype(BF16)
        h_ref[...] = h
        for c in range(D_FF // tn):
            cols = slice(c * tn, (c + 1) * tn)
            a = _dot_nt(h, wg_ref[cols, :])
            b = _dot_nt(h, wu_ref[cols, :])
            sig = 0.5 * jnp.tanh(0.5 * a) + 0.5
            silu = a * sig
            silu_ref[:, cols] = silu.astype(BF16)
            dgate_ref[:, cols] = (b * (sig + silu * (1.0 - sig))).astype(BF16)
            act_ref[:, cols] = (silu * b).astype(BF16)

    wide = jax.ShapeDtypeStruct((t, D_FF), BF16)
    row = lambda n: pl.BlockSpec((tm, n), lambda i: (i, 0))
    return _pallas(
        comm, body, name="ffn_up",
        grid=(t // tm,),
        in_specs=[row(D_MODEL), _full((1, D_MODEL)), _full((D_FF, D_MODEL)), _full((D_FF, D_MODEL))],
        out_specs=[row(D_MODEL), row(D_FF), row(D_FF), row(D_FF)],
        out_shape=[jax.ShapeDtypeStruct((t, D_MODEL), BF16), wide, wide, wide],
        args=(x, g, wg_t, wu_t))


def ffn_down(act, wd, x, tm):
    t = x.shape[0]

    def body(act_ref, wd_ref, x_ref, o_ref):
        o_ref[...] = x_ref[...] + 0.5 * _dot(act_ref[...], wd_ref[...])

    return pl.pallas_call(
        functools.partial(body), name="ffn_down",
        grid=(t // tm,),
        in_specs=[pl.BlockSpec((tm, D_FF), lambda i: (i, 0)), _full((D_FF, D_MODEL)),
                  pl.BlockSpec((tm, D_MODEL), lambda i: (i, 0))],
        out_specs=pl.BlockSpec((tm, D_MODEL), lambda i: (i, 0)),
        out_shape=jax.ShapeDtypeStruct((t, D_MODEL), F32),
        compiler_params=_params(1),
    )(act, wd, x)


def ffn_bwd_act(dx, wd, silu, dgate, tm, tn, comm=None):
    t = dx.shape[0]

    def body(dx_ref, wd_ref, silu_ref, dgate_ref, da_ref, db_ref):
        dxb = (0.5 * dx_ref[...]).astype(BF16)
        for c in range(D_FF // tn):
            cols = slice(c * tn, (c + 1) * tn)
            dact = _dot_nt(dxb, wd_ref[cols, :])
            da_ref[:, cols] = (dact * dgate_ref[:, cols].astype(F32)).astype(BF16)
            db_ref[:, cols] = (dact * silu_ref[:, cols].astype(F32)).astype(BF16)

    wide = jax.ShapeDtypeStruct((t, D_FF), BF16)
    row = lambda n: pl.BlockSpec((tm, n), lambda i: (i, 0))
    return _pallas(
        comm, body, name="ffn_bwd_act",
        grid=(t // tm,),
        in_specs=[row(D_MODEL), _full((D_FF, D_MODEL)), row(D_FF), row(D_FF)],
        out_specs=[row(D_FF), row(D_FF)],
        out_shape=[wide, wide],
        args=(dx, wd, silu, dgate))


def norm_bwd_matmul(a1, w1, a2, w2, x, g, dx_in, tm, comm=None):
    t = x.shape[0]
    k1, k2 = a1.shape[1], a2.shape[1]

    def body(a1_ref, w1_ref, a2_ref, w2_ref, x_ref, g_ref, dxin_ref, dx_ref, dg_ref):
        dh = _dot(a1_ref[...], w1_ref[...]) + _dot(a2_ref[...], w2_ref[...])
        xx = x_ref[...]
        r = _rstd(xx)
        dx, dg_rows = _rms_bwd(dh, xx * r, r, g_ref[...])
        dx_ref[...] = dxin_ref[...] + dx

        @pl.when(pl.program_id(0) == 0)
        def _():
            dg_ref[...] = jnp.zeros_like(dg_ref)

        dg_ref[...] += _colsum(dg_rows)

    row = pl.BlockSpec((tm, D_MODEL), lambda i: (i, 0))
    return _pallas(
        comm, body, name="norm_bwd_matmul",
        grid=(t // tm,),
        in_specs=[pl.BlockSpec((tm, k1), lambda i: (i, 0)), _full((k1, D_MODEL)),
                  pl.BlockSpec((tm, k2), lambda i: (i, 0)), _full((k2, D_MODEL)),
                  row, _full((1, D_MODEL)), row],
        out_specs=[row, _full((1, D_MODEL))],
        out_shape=[jax.ShapeDtypeStruct((t, D_MODEL), F32), jax.ShapeDtypeStruct((1, D_MODEL), F32)],
        args=(a1, w1, a2, w2, x, g, dx_in))


def matmul_tn(a, b, scale, tmm, tk):
    t, m = a.shape
    n = b.shape[1]
    nk = t // tk

    def body(a_ref, b_ref, o_ref, acc_ref):
        k = pl.program_id(1)

        @pl.when(k == 0)
        def _():
            acc_ref[...] = jnp.zeros_like(acc_ref)

        acc_ref[...] += _dot_tn(a_ref[...].astype(BF16), b_ref[...].astype(BF16))

        @pl.when(k == nk - 1)
        def _():
            o_ref[...] = (scale * acc_ref[...]).astype(BF16)

    return pl.pallas_call(
        functools.partial(body), name="matmul_tn",
        grid=(m // tmm, nk),
        in_specs=[pl.BlockSpec((tk, tmm), lambda i, k: (k, i)), pl.BlockSpec((tk, n), lambda i, k: (k, 0))],
        out_specs=pl.BlockSpec((tmm, n), lambda i, k: (i, 0)),
        out_shape=jax.ShapeDtypeStruct((m, n), BF16),
        scratch_shapes=[pltpu.VMEM((tmm, n), F32)],
        compiler_params=_params(2),
    )(a, b)


def input_projection(x, g, w_qkv_t, w_z_t, tm):
    t = x.shape[0]

    def body(x_ref, g_ref, wq_ref, wz_ref, qkv_ref, z_ref, h_ref):
        xx = x_ref[...]
        h = ((xx * _rstd(xx)) * g_ref[...]).astype(BF16)
        h_ref[...] = h
        qkv_ref[...] = _dot_nt(h, wq_ref[...])
        z_ref[...] = _dot_nt(h, wz_ref[...])

    row = lambda n: pl.BlockSpec((tm, n), lambda i: (i, 0))
    return pl.pallas_call(
        functools.partial(body), name="input_projection", grid=(t // tm,),
        in_specs=[row(D_MODEL), _full((1, D_MODEL)), _full((D_QKV, D_MODEL)), _full((2 * D_SGU, D_MODEL))],
        out_specs=[row(D_QKV), row(2 * D_SGU), row(D_MODEL)],
        out_shape=[jax.ShapeDtypeStruct((t, D_QKV), F32), jax.ShapeDtypeStruct((t, 2 * D_SGU), F32),
                   jax.ShapeDtypeStruct((t, D_MODEL), BF16)],
        compiler_params=_params(1))(x, g, w_qkv_t, w_z_t)


def _shift_rows(shape, first, second):
    row = lax.broadcasted_iota(jnp.int32, shape, len(shape) - 2)
    return jnp.where(row == 0, first, jnp.where(row == 1, second, 0.0))


def _hi_lo(a):
    hi = a.astype(BF16).astype(F32)
    return hi, a - hi


def _head_tile_spec(tm, rows):
    return pl.BlockSpec((N_HEADS, None, rows, tm), lambda i: (0, i, 0, 0))


def _to_head_tiles(a):
    return a.T.reshape(N_HEADS, HEAD_DIM, a.shape[0])


def _from_head_tiles(a):
    return a.reshape(D_ATTN, a.shape[-1]).T


def _head_mean_matrix(width):
    head = jnp.arange(width) // HEAD_DIM
    return (head[:, None] == head[None, :]).astype(F32) / HEAD_DIM


def _kv_tile_spec(n_sub, rows, cols):
    return pl.BlockSpec((N_KV_HEADS, n_sub, rows, cols), lambda i: (0, i, 0, 0))


def qk_prep(qkv, gq_w, gk_w, cos_w, sin_w, mean_q, mean_k, tm, tk, tk_v):
    t = qkv.shape[0]
    n_sub, n_sub_v = tm // tk, tm // tk_v

    def body(p_ref, gq_ref, gk_ref, cos_ref, sin_ref, mq_ref, mk_ref, q_ref, k_ref, kt_ref, vt_ref, vtb_ref,
             qmax_ref, kmax_ref):
        @pl.when(pl.program_id(0) == 0)
        def _():
            qmax_ref[...] = jnp.zeros_like(qmax_ref)
            kmax_ref[...] = jnp.zeros_like(kmax_ref)

        cos2, sin2 = cos_ref[...], sin_ref[...]
        q = p_ref[:, :D_ATTN]
        k = p_ref[:, D_ATTN:D_ATTN + D_KV]
        qn = q * lax.rsqrt(_dot_split(q * q, mq_ref[...]) + EPS) * gq_ref[...]
        kn = k * lax.rsqrt(_dot_split(k * k, mk_ref[...]) + EPS) * gk_ref[...]
        cos8, sin8 = _tile_lanes(cos2, D_ATTN // LANES), _tile_lanes(sin2, D_ATTN // LANES)
        q_rot = (qn * cos8 + _pair_swap(qn) * sin8) * Q_SCALE
        q_ref[...] = _to_head_tiles(q_rot).astype(BF16)
        k_rot = kn * cos2 + _pair_swap(kn) * sin2
        q_sq = HEAD_DIM * _dot_split(q_rot * q_rot, mq_ref[...])
        k_sq = HEAD_DIM * _dot_split(k_rot * k_rot, mk_ref[...])
        qmax_ref[...] = jnp.maximum(qmax_ref[...], jnp.max(q_sq, axis=0, keepdims=True))
        kmax_ref[...] = jnp.maximum(kmax_ref[...], jnp.max(k_sq, axis=0, keepdims=True))
        vv = p_ref[:, D_ATTN + D_KV:]
        second = pltpu.roll(k_rot, HEAD_DIM, 1)
        for c in range(n_sub):
            rows = slice(c * tk, (c + 1) * tk)
            k_ref[0, c] = k_rot[rows, :HEAD_DIM].astype(BF16)
            k_ref[1, c] = second[rows, :HEAD_DIM].astype(BF16)
        for a, feat_ref, width, n in ((k_rot, kt_ref, tk, n_sub), (vv, vtb_ref, tk, n_sub), (vv, vt_ref, tk_v, n_sub_v)):
            for c in range(n):
                tile = a[c * width:(c + 1) * width].T.reshape(N_KV_HEADS, HEAD_DIM, width)
                feat_ref[:, c, :HEAD_DIM, :] = tile.astype(BF16)
        vt_ref[:, :, HEAD_DIM:, :] = jnp.ones((N_KV_HEADS, n_sub_v, ONES_ROWS, tk_v), BF16)
        minus = _shift_rows((N_KV_HEADS, n_sub, HEAD_DIM, tk), -1.0, -1.0).astype(BF16)
        kt_ref[:, :, HEAD_DIM:, :] = minus
        vtb_ref[:, :, HEAD_DIM:, :] = minus

    kv = lambda rows, cols: jax.ShapeDtypeStruct((N_KV_HEADS, t // tk, rows, cols), BF16)
    return pl.pallas_call(
        functools.partial(body), name="qk_prep", grid=(t // tm,),
        in_specs=[pl.BlockSpec((tm, D_QKV), lambda i: (i, 0)), _full((1, D_ATTN)), _full((1, D_KV)),
                  pl.BlockSpec((tm, LANES), lambda i: (i, 0)), pl.BlockSpec((tm, LANES), lambda i: (i, 0)),
                  _full((D_ATTN, D_ATTN)), _full((D_KV, D_KV))],
        out_specs=[_head_tile_spec(tm, HEAD_DIM), _kv_tile_spec(n_sub, tk, HEAD_DIM),
                   _kv_tile_spec(n_sub, 2 * HEAD_DIM, tk),
                   _kv_tile_spec(n_sub_v, HEAD_DIM + ONES_ROWS, tk_v), _kv_tile_spec(n_sub, 2 * HEAD_DIM, tk),
                   _full((1, D_ATTN)), _full((1, D_KV))],
        out_shape=[jax.ShapeDtypeStruct((N_HEADS, t // tm, HEAD_DIM, tm), BF16), kv(tk, HEAD_DIM), kv(2 * HEAD_DIM, tk),
                   jax.ShapeDtypeStruct((N_KV_HEADS, t // tk_v, HEAD_DIM + ONES_ROWS, tk_v), BF16),
                   kv(2 * HEAD_DIM, tk),
                   jax.ShapeDtypeStruct((1, D_ATTN), F32), jax.ShapeDtypeStruct((1, D_KV), F32)],
        compiler_params=_params(1),
    )(qkv, gq_w, gk_w, cos_w, sin_w, mean_q, mean_k)


def qk_bwd(dq_rot, dk_rot, dv, qkv, gq_w, gk_w, cos_w, sin_w, mean_q, mean_k, tm):
    t = qkv.shape[0]
    tk = dk_rot.shape[-1]
    n_sub = tm // tk

    def token_major(ref):
        return jnp.concatenate([ref[:, c].reshape(D_KV, tk).T for c in range(n_sub)], axis=0)

    def branch(raw, d_rot, gain, mean_mat, cos, sin, scale):
        r = lax.rsqrt(_dot_split(raw * raw, mean_mat) + EPS)
        n = raw * r
        dy = (d_rot * cos - _pair_swap(d_rot) * sin) * scale
        dn = dy * gain
        return r * (dn - n * _dot_split(dn * n, mean_mat)), dy * n

    def body(dq_ref, dk_ref, dv_ref, p_ref, gq_ref, gk_ref, cos_ref, sin_ref, mq_ref, mk_ref,
             dp_ref, dgq_ref, dgk_ref):
        cos2, sin2 = cos_ref[...], sin_ref[...]
        cos8, sin8 = _tile_lanes(cos2, D_ATTN // LANES), _tile_lanes(sin2, D_ATTN // LANES)
        dq, dgq = branch(p_ref[:, :D_ATTN], _from_head_tiles(dq_ref[...]), gq_ref[...], mq_ref[...], cos8, sin8,
                         HEAD_DIM ** -0.5)
        dk, dgk = branch(p_ref[:, D_ATTN:D_ATTN + D_KV], token_major(dk_ref), gk_ref[...], mk_ref[...], cos2, sin2, 1.0)
        dp_ref[...] = jnp.concatenate([dq, dk, token_major(dv_ref)], axis=-1).astype(BF16)

        @pl.when(pl.program_id(0) == 0)
        def _():
            dgq_ref[...] = jnp.zeros_like(dgq_ref)
            dgk_ref[...] = jnp.zeros_like(dgk_ref)

        dgq_ref[...] += _colsum(dgq)
        dgk_ref[...] += _colsum(dgk)

    return pl.pallas_call(
        functools.partial(body), name="qk_bwd", grid=(t // tm,),
        in_specs=[_head_tile_spec(tm, HEAD_DIM), _kv_tile_spec(n_sub, HEAD_DIM, tk),
                  _kv_tile_spec(n_sub, HEAD_DIM, tk), pl.BlockSpec((tm, D_QKV), lambda i: (i, 0)),
                  _full((1, D_ATTN)), _full((1, D_KV)),
                  pl.BlockSpec((tm, LANES), lambda i: (i, 0)), pl.BlockSpec((tm, LANES), lambda i: (i, 0)),
                  _full((D_ATTN, D_ATTN)), _full((D_KV, D_KV))],
        out_specs=[pl.BlockSpec((tm, D_QKV), lambda i: (i, 0)), _full((1, D_ATTN)), _full((1, D_KV))],
        out_shape=[jax.ShapeDtypeStruct((t, D_QKV), BF16), jax.ShapeDtypeStruct((1, D_ATTN), F32),
                   jax.ShapeDtypeStruct((1, D_KV), F32)],
        compiler_params=_params(1),
    )(dq_rot, dk_rot, dv, qkv, gq_w, gk_w, cos_w, sin_w, mean_q, mean_k)


def attention_fwd(bound, q_t, k, v_t, comm=None):
    _, nq, _, tq = q_t.shape
    _, nk, tk, _ = k.shape

    def body(bound_ref, q_ref, k_ref, v_ref, o_ref, qtok_ref, s_scr, p_scr):
        head_bound = bound_ref[pl.program_id(0)]
        safe = head_bound <= SAFE_SCORE_BOUND
        q = q_ref[...]
        s_scr[0] = _dot(k_ref[0], q)
        p_scr[1] = jnp.zeros((tk, tq), BF16)
        zero = jnp.zeros((HEAD_DIM + ONES_ROWS, tq), F32)

        def matmuls(j, slot):
            pv = _dot(v_ref[jnp.maximum(j - 1, 0)], p_scr[1 - slot])
            s_scr[1 - slot] = _dot(k_ref[jnp.minimum(j + 1, nk - 1)], q)
            return pv

        def finish(m, acc):
            acc = acc + _dot(v_ref[nk - 1], p_scr[(nk - 1) % 2])
            l = acc[HEAD_DIM:HEAD_DIM + 1]
            o_ref[...] = acc[:HEAD_DIM] / l
            lse_rows = _shift_rows((HEAD_DIM, tq), *_hi_lo(m + jnp.log2(l)))
            qtok_ref[...] = jnp.concatenate([q.astype(F32), lse_rows], axis=0).T.astype(BF16)

        @pl.when(safe)
        def _():
            m = jnp.full((1, tq), head_bound, F32)

            def step(j, slot, acc):
                s = s_scr[slot]
                pv = matmuls(j, slot)
                p_scr[slot] = jnp.exp2(s - m).astype(BF16)
                return acc + pv

            finish(m, _loop_pairs(nk, step, zero))

        @pl.when(jnp.logical_not(safe))
        def _():
            def step(j, slot, carry):
                m, acc = carry
                s = s_scr[slot]
                pv = matmuls(j, slot)
                m_new = jnp.maximum(m, jnp.max(s, axis=0, keepdims=True))
                p_scr[slot] = jnp.exp2(s - m_new).astype(BF16)
                return m_new, jnp.exp2(m - m_new) * (acc + pv)

            finish(*_loop_pairs(nk, step, (jnp.full((1, tq), -1e30, F32), zero)))

    return _pallas(
        comm, body, name="attention_fwd", grid=(N_HEADS, nq),
        in_specs=[pl.BlockSpec(memory_space=pltpu.SMEM),
                  pl.BlockSpec((None, None, HEAD_DIM, tq), lambda h, i: (h, i, 0, 0)),
                  pl.BlockSpec((None, nk, tk, HEAD_DIM), lambda h, i: (h // KV_GROUP, 0, 0, 0)),
                  pl.BlockSpec((None, nk, HEAD_DIM + ONES_ROWS, tk), lambda h, i: (h // KV_GROUP, 0, 0, 0))],
        out_specs=[pl.BlockSpec((None, None, HEAD_DIM, tq), lambda h, i: (h, i, 0, 0)),
                   pl.BlockSpec((None, None, tq, 2 * HEAD_DIM), lambda h, i: (h, i, 0, 0))],
        out_shape=[jax.ShapeDtypeStruct((N_HEADS, nq, HEAD_DIM, tq), F32),
                   jax.ShapeDtypeStruct((N_HEADS, nq, tq, 2 * HEAD_DIM), BF16)],
        scratch_shapes=[pltpu.VMEM((2, tk, tq), F32), pltpu.VMEM((2, tk, tq), BF16)],
        args=(bound, q_t, k, v_t))


def attention_bwd(q_tok, do_tok, q_t, do_t, k_t, v_t, comm=None):
    _, nq, _, tq = q_t.shape
    _, nk, _, tk = k_t.shape

    def body(qtok_ref, dotok_ref, q_ref, do_ref, kt_ref, vt_ref, dq_ref, dk_ref, dv_ref,
             s_scr, dp_scr, p_scr, ds_scr):
        @pl.when(pl.program_id(1) == 0)
        def _():
            dq_ref[...] = jnp.zeros_like(dq_ref)

        kt_aug, vt_aug = kt_ref[...], vt_ref[...]
        kt = kt_aug[:HEAD_DIM]
        n = KV_GROUP * nq
        s_scr[0] = _dot(qtok_ref[0, 0], kt_aug)
        dp_scr[0] = _dot(dotok_ref[0, 0], vt_aug)
        p_scr[1] = jnp.zeros((tq, tk), BF16)
        ds_scr[1] = jnp.zeros((tq, tk), BF16)

        def products(t, slot, dk, dv):
            h, i = t // nq, t % nq
            ds = ds_scr[slot]
            dq_ref[h, i] += _dot_nt(kt, ds)
            return dk + _dot(q_ref[h, i], ds), dv + _dot(do_ref[h, i], p_scr[slot])

        def step(t, slot, carry):
            s, dp = s_scr[slot], dp_scr[slot]
            dk, dv = products(jnp.maximum(t - 1, 0), 1 - slot, *carry)
            nxt = jnp.minimum(t + 1, n - 1)
            s_scr[1 - slot] = _dot(qtok_ref[nxt // nq, nxt % nq], kt_aug)
            dp_scr[1 - slot] = _dot(dotok_ref[nxt // nq, nxt % nq], vt_aug)
            p = jnp.exp2(s)
            p_scr[slot] = p.astype(BF16)
            ds_scr[slot] = (p * dp).astype(BF16)
            return dk, dv

        zero = jnp.zeros((HEAD_DIM, tk), F32)
        dk, dv = products(n - 1, (n - 1) % 2, *_loop_pairs(n, step, (zero, zero)))
        dk_ref[...] = dk * (1.0 / LOG2_E)
        dv_ref[...] = dv

    group = lambda g, j: (g, 0, 0, 0)
    tile = lambda g, j: (g, j, 0, 0)
    once = pl.Buffered(1)
    return _pallas(
        comm, body, name="attention_bwd", grid=(N_KV_HEADS, nk),
        in_specs=[pl.BlockSpec((KV_GROUP, nq, tq, 2 * HEAD_DIM), group, pipeline_mode=once),
                  pl.BlockSpec((KV_GROUP, nq, tq, 2 * HEAD_DIM), group, pipeline_mode=once),
                  pl.BlockSpec((KV_GROUP, nq, HEAD_DIM, tq), group, pipeline_mode=once),
                  pl.BlockSpec((KV_GROUP, nq, HEAD_DIM, tq), group, pipeline_mode=once),
                  pl.BlockSpec((None, None, 2 * HEAD_DIM, tk), tile),
                  pl.BlockSpec((None, None, 2 * HEAD_DIM, tk), tile)],
        out_specs=[pl.BlockSpec((KV_GROUP, nq, HEAD_DIM, tq), group),
                   pl.BlockSpec((None, None, HEAD_DIM, tk), tile),
                   pl.BlockSpec((None, None, HEAD_DIM, tk), tile)],
        out_shape=[jax.ShapeDtypeStruct((N_HEADS, nq, HEAD_DIM, tq), F32),
                   jax.ShapeDtypeStruct((N_KV_HEADS, nk, HEAD_DIM, tk), F32),
                   jax.ShapeDtypeStruct((N_KV_HEADS, nk, HEAD_DIM, tk), F32)],
        scratch_shapes=[pltpu.VMEM((2, tq, tk), F32), pltpu.VMEM((2, tq, tk), F32),
                        pltpu.VMEM((2, tq, tk), BF16), pltpu.VMEM((2, tq, tk), BF16)],
        args=(q_tok, do_tok, q_t, do_t, k_t, v_t))


def _group_select(parts):
    lane_group = lax.broadcasted_iota(jnp.int32, parts[0].shape, 1) // SGU_GROUP_DIM
    out = parts[0]
    for g in range(1, N_SGU_GROUPS):
        out = jnp.where(lane_group == g, parts[g], out)
    return out


def _gate_forward(z, g_sgu, ws_ref, bias):
    gz, th = _gelu(z)
    u, vv = gz[:, :D_SGU], gz[:, D_SGU:]
    rv = _rstd(vv)
    nv = vv * rv
    vn = (nv * g_sgu).astype(BF16)
    fs = []
    for c in range(z.shape[0] // CHUNK):
        vc = vn[c * CHUNK:(c + 1) * CHUNK]
        fs.append(_group_select([_dot(ws_ref[g], vc) for g in range(N_SGU_GROUPS)]) + bias)
    f = jnp.concatenate(fs, axis=0) if len(fs) > 1 else fs[0]
    return th, u, rv, nv, vn, f


def mix_out(z, o, x, g_sgu, g_ao, g_so, ws, bias, w_out, tm):
    t = x.shape[0]

    def body(z_ref, o_ref, x_ref, gs_ref, gao_ref, gso_ref, ws_ref, bias_ref, wout_ref, x2_ref, mixed_ref):
        _, u, _, _, _, f = _gate_forward(z_ref[...], gs_ref[...], ws_ref, bias_ref[...])
        sgu = u * f
        oo = _from_head_tiles(o_ref[...])
        mixed = jnp.concatenate([oo * _rstd(oo) * gao_ref[...], sgu * _rstd(sgu) * gso_ref[...]], axis=-1).astype(BF16)
        mixed_ref[...] = mixed
        x2_ref[...] = x_ref[...] + _dot(mixed, wout_ref[...])

    row = lambda n: pl.BlockSpec((tm, n), lambda i: (i, 0))
    return pl.pallas_call(
        functools.partial(body), name="mix_out", grid=(t // tm,),
        in_specs=[row(2 * D_SGU), _head_tile_spec(tm, HEAD_DIM), row(D_MODEL), _full((1, D_SGU)), _full((1, D_ATTN)),
                  _full((1, D_SGU)),
                  _full((N_SGU_GROUPS, CHUNK, CHUNK)), _full((CHUNK, D_SGU)), _full((D_MODEL, D_MODEL))],
        out_specs=[row(D_MODEL), row(D_MODEL)],
        out_shape=[jax.ShapeDtypeStruct((t, D_MODEL), F32), jax.ShapeDtypeStruct((t, D_MODEL), BF16)],
        compiler_params=_params(1),
    )(z, o, x, g_sgu, g_ao, g_so, ws, bias, w_out)


def mix_bwd(dx2, z, o, g_sgu, g_ao, g_so, ws, ws_t, bias, w_out, group_ind, tm):
    t = dx2.shape[0]
    n_tiles = t // tm

    def body(dx_ref, z_ref, o_ref, gs_ref, gao_ref, gso_ref, ws_ref, wst_ref, bias_ref, wout_ref, ind_ref,
             do_ref, dotok_ref, dz_ref, dg_ref, dws_ref, dbs_ref, df_sum):
        step = pl.program_id(0)

        @pl.when(step == 0)
        def _():
            dg_ref[...] = jnp.zeros_like(dg_ref)
            dws_ref[...] = jnp.zeros_like(dws_ref)
            df_sum[...] = jnp.zeros_like(df_sum)

        z = z_ref[...]
        th, u, rv, nv, vn, f = _gate_forward(z, gs_ref[...], ws_ref, bias_ref[...])
        dmixed = _dot_nt(dx_ref[...].astype(BF16), wout_ref[...])
        o_tiles = o_ref[...]
        oo = _from_head_tiles(o_tiles)
        ro = _rstd(oo)
        d_o, dgao = _rms_bwd(dmixed[:, :D_ATTN], oo * ro, ro, gao_ref[...])
        do_tiles = _to_head_tiles(d_o)
        do_ref[...] = do_tiles.astype(BF16)
        delta_hi, delta_lo = _hi_lo(jnp.sum(do_tiles * o_tiles, axis=1, keepdims=True))
        for h in range(N_HEADS):
            delta_rows = _shift_rows((HEAD_DIM, tm), delta_hi[h], delta_lo[h])
            dotok_ref[h] = jnp.concatenate([do_tiles[h], delta_rows], axis=0).T.astype(BF16)
        sgu = u * f
        rs = _rstd(sgu)
        dsgu, dgso = _rms_bwd(dmixed[:, D_ATTN:], sgu * rs, rs, gso_ref[...])
        du = dsgu * f
        df = dsgu * u
        lane_group = lax.broadcasted_iota(jnp.int32, (CHUNK, D_SGU), 1) // SGU_GROUP_DIM
        dvns = []
        df_acc = jnp.zeros((CHUNK, D_SGU), F32)
        for c in range(tm // CHUNK):
            dfc32 = df[c * CHUNK:(c + 1) * CHUNK]
            dfc = dfc32.astype(BF16)
            vc = vn[c * CHUNK:(c + 1) * CHUNK]
            dvns.append(_group_select([_dot(wst_ref[g], dfc) for g in range(N_SGU_GROUPS)]))
            for g in range(N_SGU_GROUPS):
                dws_ref[g] += _dot_nt(jnp.where(lane_group == g, dfc, jnp.zeros_like(dfc)), vc)
            df_acc = df_acc + dfc32
        df_sum[...] += df_acc
        dvn = jnp.concatenate(dvns, axis=0) if len(dvns) > 1 else dvns[0]
        dvv, dgs = _rms_bwd(dvn, nv, rv, gs_ref[...])
        dz_ref[...] = (jnp.concatenate([du, dvv], axis=-1) * _gelu_grad(z, th)).astype(BF16)
        dg_ref[0:1, :] += _colsum(dgao)
        dg_ref[1:2, :] += _colsum(dgso)
        dg_ref[2:3, :] += _colsum(dgs)

        @pl.when(step == n_tiles - 1)
        def _():
            dbs_ref[...] = _dot_f32(df_sum[...], ind_ref[...])

    row = lambda n: pl.BlockSpec((tm, n), lambda i: (i, 0))
    return pl.pallas_call(
        functools.partial(body), name="mix_bwd", grid=(n_tiles,),
        in_specs=[row(D_MODEL), row(2 * D_SGU), _head_tile_spec(tm, HEAD_DIM), _full((1, D_SGU)), _full((1, D_ATTN)),
                  _full((1, D_SGU)),
                  _full((N_SGU_GROUPS, CHUNK, CHUNK)), _full((N_SGU_GROUPS, CHUNK, CHUNK)), _full((CHUNK, D_SGU)),
                  _full((D_MODEL, D_MODEL)), _full((D_SGU, LANES))],
        out_specs=[_head_tile_spec(tm, HEAD_DIM), pl.BlockSpec((N_HEADS, None, tm, 2 * HEAD_DIM), lambda i: (0, i, 0, 0)),
                   row(2 * D_SGU), _full((8, D_SGU)),
                   _full((N_SGU_GROUPS, CHUNK, CHUNK)), _full((CHUNK, LANES))],
        out_shape=[jax.ShapeDtypeStruct((N_HEADS, n_tiles, HEAD_DIM, tm), BF16),
                   jax.ShapeDtypeStruct((N_HEADS, n_tiles, tm, 2 * HEAD_DIM), BF16),
                   jax.ShapeDtypeStruct((t, 2 * D_SGU), BF16),
                   jax.ShapeDtypeStruct((8, D_SGU), F32),
                   jax.ShapeDtypeStruct((N_SGU_GROUPS, CHUNK, CHUNK), F32),
                   jax.ShapeDtypeStruct((CHUNK, LANES), F32)],
        scratch_shapes=[pltpu.VMEM((CHUNK, D_SGU), F32)],
        compiler_params=_params(1),
    )(dx2, z, o, g_sgu, g_ao, g_so, ws, ws_t, bias, w_out, group_ind)


def ffn_down_loss(act, wd, x, g, target, tm):
    t = x.shape[0]

    def body(act_ref, wd_ref, x_ref, g_ref, t_ref, loss_ref, dx_ref, dg_ref):
        @pl.when(pl.program_id(0) == 0)
        def _():
            loss_ref[...] = jnp.zeros_like(loss_ref)
            dg_ref[...] = jnp.zeros_like(dg_ref)

        xx = x_ref[...] + 0.5 * _dot(act_ref[...], wd_ref[...])
        r = _rstd(xx)
        n = xx * r
        err = n * g_ref[...] - t_ref[...]
        per_token = jnp.mean(err * err, axis=-1, keepdims=True)
        loss_ref[...] += 0.5 * jnp.sum(per_token, axis=0, keepdims=True)
        dx, dg_rows = _rms_bwd(err * (1.0 / D_MODEL), n, r, g_ref[...])
        dx_ref[...] = dx
        dg_ref[...] += _colsum(dg_rows)

    row = pl.BlockSpec((tm, D_MODEL), lambda i: (i, 0))
    return pl.pallas_call(
        functools.partial(body), name="ffn_down_loss", grid=(t // tm,),
        in_specs=[pl.BlockSpec((tm, D_FF), lambda i: (i, 0)), _full((D_FF, D_MODEL)), row, _full((1, D_MODEL)), row],
        out_specs=[_full((1, LANES)), row, _full((1, D_MODEL))],
        out_shape=[jax.ShapeDtypeStruct((1, LANES), F32), jax.ShapeDtypeStruct((t, D_MODEL), F32),
                   jax.ShapeDtypeStruct((1, D_MODEL), F32)],
        compiler_params=_params(1),
    )(act, wd, x, g, target)


def _rope_tables(t):
    rows = t // GRID_W
    row_idx = jnp.repeat(jnp.arange(rows, dtype=F32), GRID_W)
    col_idx = jnp.tile(jnp.arange(GRID_W, dtype=F32), rows)
    axis_dim = HEAD_DIM // 2
    inv = 1.0 / (ROPE_THETA ** (jnp.arange(0, axis_dim, 2, dtype=F32) / axis_dim))
    ang = jnp.concatenate([row_idx[:, None] * inv, col_idx[:, None] * inv], axis=-1)
    cos = jnp.repeat(jnp.cos(ang), 2, axis=-1)
    sin = jnp.repeat(jnp.sin(ang), 2, axis=-1) * jnp.tile(jnp.array([-1.0, 1.0], F32), HEAD_DIM // 2)
    return jnp.tile(cos, (1, LANES // HEAD_DIM)), jnp.tile(sin, (1, LANES // HEAD_DIM))


def _heads_to_tiles_t(a, n_heads, tile):
    t = a.shape[0]
    return a.reshape(t // tile, tile, n_heads, HEAD_DIM).transpose(2, 0, 3, 1)


def _heads_to_tiles(a, n_heads, tile):
    t = a.shape[0]
    return a.reshape(t // tile, tile, n_heads, HEAD_DIM).transpose(2, 0, 1, 3)


def _tiles_t_to_heads(a):
    h, n, _, tile = a.shape
    return a.transpose(1, 3, 0, 2).reshape(n * tile, h * HEAD_DIM)


def kernel(x, g_ffn1, w1_gate, w1_up, w1_down, g_mix, w_in, g_q, g_k, g_sgu, w_s, b_s, g_attn_out, g_sgu_out, w_out, g_ffn2, w2_gate, w2_up, w2_down, g_final, loss_target, m_g_ffn1, m_w1_gate, m_w1_up, m_w1_down, m_g_mix, m_w_in, m_g_q, m_g_k, m_g_sgu, m_w_s, m_b_s, m_g_attn_out, m_g_sgu_out, m_w_out, m_g_ffn2, m_w2_gate, m_w2_up, m_w2_down, m_g_final, v_g_ffn1, v_w1_gate, v_w1_up, v_w1_down, v_g_mix, v_w_in, v_g_q, v_g_k, v_g_sgu, v_w_s, v_b_s, v_g_attn_out, v_g_sgu_out, v_w_out, v_g_ffn2, v_w2_gate, v_w2_up, v_w2_down, v_g_final):
    weights = dict(g_ffn1=g_ffn1, w1_gate=w1_gate, w1_up=w1_up, w1_down=w1_down, g_mix=g_mix, w_in=w_in, g_q=g_q,
                   g_k=g_k, g_sgu=g_sgu, w_s=w_s, b_s=b_s, g_attn_out=g_attn_out, g_sgu_out=g_sgu_out, w_out=w_out,
                   g_ffn2=g_ffn2, w2_gate=w2_gate, w2_up=w2_up, w2_down=w2_down, g_final=g_final)
    m_in = dict(g_ffn1=m_g_ffn1, w1_gate=m_w1_gate, w1_up=m_w1_up, w1_down=m_w1_down, g_mix=m_g_mix, w_in=m_w_in,
                g_q=m_g_q, g_k=m_g_k, g_sgu=m_g_sgu, w_s=m_w_s, b_s=m_b_s, g_attn_out=m_g_attn_out,
                g_sgu_out=m_g_sgu_out, w_out=m_w_out, g_ffn2=m_g_ffn2, w2_gate=m_w2_gate, w2_up=m_w2_up,
                w2_down=m_w2_down, g_final=m_g_final)
    v_in = dict(g_ffn1=v_g_ffn1, w1_gate=v_w1_gate, w1_up=v_w1_up, w1_down=v_w1_down, g_mix=v_g_mix, w_in=v_w_in,
                g_q=v_g_q, g_k=v_g_k, g_sgu=v_g_sgu, w_s=v_w_s, b_s=v_b_s, g_attn_out=v_g_attn_out,
                g_sgu_out=v_g_sgu_out, w_out=v_w_out, g_ffn2=v_g_ffn2, w2_gate=v_w2_gate, w2_up=v_w2_up,
                w2_down=v_w2_down, g_final=v_g_final)
    names = list(weights)

    t = x.shape[1]
    x0 = x[0]
    target = loss_target[0]
    tm = min(256, t)
    tm_ff = min(256, t)
    tn_ff = 256
    tq = min(512, t)
    tk = min(256, t)
    tk_fwd = min(512, t)
    tk_w = min(2048, t)

    def shard_rows(name):
        w = weights[name][0]
        return (w.T if name in TRANSPOSED else w).astype(BF16)

    rows_of = dict(SHARD_ROWS)
    full = {}

    def packed(group):
        return jnp.concatenate([shard_rows(n) for n in group], axis=0), [rows_of[n] for n in group]

    def gather_of(group):
        return gather_exchange(*packed(group))

    def take(group, gathered):
        for n, g in zip(group, gathered):
            full[n] = g.reshape(N_DEV * rows_of[n], D_MODEL)

    first, second, third = ("w1_gate", "w1_up"), ("w1_down", "w_in", "w_out"), ("w2_gate", "w2_up", "w2_down")
    take(first, gather_two_level(*packed(first), "gather_first"))

    (h1, a1, b1, act1), gathered = ffn_up(x0, g_ffn1, full["w1_gate"], full["w1_up"], tm_ff, tn_ff, gather_of(second))
    take(second, gathered)
    w_in_t = full["w_in"]
    w_qkv_t, w_z_t = w_in_t[:D_QKV], w_in_t[D_QKV:]
    x1 = ffn_down(act1, full["w1_down"], x0, tm)

    qkv, z, h2 = input_projection(x1, g_mix, w_qkv_t, w_z_t, tm)
    cos_w, sin_w = _rope_tables(t)
    gq_w = jnp.tile(g_q, (1, N_HEADS))
    gk_w = jnp.tile(g_k, (1, N_KV_HEADS))
    mean_q, mean_k = _head_mean_matrix(D_ATTN).astype(BF16), _head_mean_matrix(D_KV).astype(BF16)
    q_t, k_tiles, kt_tiles, vt_tiles, vt_tiles_bwd, q_sq_max, k_sq_max = qk_prep(
        qkv, gq_w, gk_w, cos_w, sin_w, mean_q, mean_k, tq, tk, tk_fwd)
    score_bound = 1.02 * jnp.sqrt(q_sq_max.reshape(N_HEADS, HEAD_DIM)[:, 0]
                                  * jnp.repeat(k_sq_max.reshape(N_KV_HEADS, HEAD_DIM)[:, 0], KV_GROUP))
    k_tiles_fwd = k_tiles.reshape(N_KV_HEADS, t // tk_fwd, tk_fwd, HEAD_DIM)
    (o_t, q_tok), gathered = attention_fwd(score_bound, q_t, k_tiles_fwd, vt_tiles, gather_of(third))
    take(third, gathered)

    ws_b = w_s[0].astype(BF16)
    ws_tb = jnp.swapaxes(w_s[0], 1, 2).astype(BF16)
    bias = jnp.repeat(b_s[0].T, SGU_GROUP_DIM, axis=1)
    x2, mixed = mix_out(z, o_t, x1, g_sgu, g_attn_out, g_sgu_out, ws_b, bias, full["w_out"], tq)

    (h3, a2, b2, act2), _ = ffn_up(x2, g_ffn2, full["w2_gate"], full["w2_up"], tm_ff, tn_ff)

    loss_part, dx3, dg_final = ffn_down_loss(act2, full["w2_down"], x2, g_final, target, tm)

    tmm = D_FF // 2
    (da2, db2), _ = ffn_bwd_act(dx3, full["w2_down"], a2, b2, tm_ff, tn_ff)
    (dx2, dg_ffn2), _ = norm_bwd_matmul(da2, full["w2_gate"], db2, full["w2_up"], x2, g_ffn2, dx3, tm)
    dwg2, dwu2 = matmul_tn(da2, h3, 1.0, tmm, tk_w), matmul_tn(db2, h3, 1.0, tmm, tk_w)
    dwd2 = matmul_tn(act2, dx3, 0.5, tmm, tk_w)

    group_ind = (jnp.arange(D_SGU)[:, None] // SGU_GROUP_DIM == jnp.arange(LANES)[None, :]).astype(F32)
    do_t, do_tok, dz, dg_mixrow, dws, dbs = mix_bwd(dx2, z, o_t, g_sgu, g_attn_out, g_sgu_out, ws_b, ws_tb, bias,
                                                    full["w_out"], group_ind, tq)
    dw_out = matmul_tn(mixed, dx2, 1.0, D_MODEL // 2, tk_w)

    group_a = ("w2_gate", "w2_up", "w2_down", "w_out")
    (dq_t, dk_t, dv_t), (parts_a,) = attention_bwd(q_tok, do_tok, q_t, do_t, kt_tiles, vt_tiles_bwd,
                                                   scatter_exchange([dwg2, dwu2, dwd2, dw_out]))
    dqkv, dgq_w, dgk_w = qk_bwd(dq_t, dk_t, dv_t, qkv, gq_w, gk_w, cos_w, sin_w, mean_q, mean_k, tq)

    def pack_small(arrays):
        pieces = []
        for a in arrays:
            flat = a.reshape(-1)
            pieces.append(jnp.pad(flat, (0, (-flat.shape[0]) % (8 * LANES))).reshape(-1, LANES))
        return jnp.concatenate(pieces, axis=0), [p.shape[0] for p in pieces]

    early = dict(g_ffn2=dg_ffn2, g_final=dg_final, g_q=dgq_w.reshape(N_HEADS, HEAD_DIM).sum(0),
                 g_k=dgk_w.reshape(N_KV_HEADS, HEAD_DIM).sum(0), g_attn_out=dg_mixrow[0], g_sgu_out=dg_mixrow[1],
                 g_sgu=dg_mixrow[2], w_s=dws, b_s=dbs[:, :N_SGU_GROUPS].T)
    early_pack, early_rows = pack_small(list(early.values()))
    (dx1, dg_mix), (early_parts,) = norm_bwd_matmul(dqkv, w_qkv_t, dz, w_z_t, x1, g_mix, dx2, tm,
                                                    gather_exchange(early_pack, [early_pack.shape[0]]))
    dw_in = jnp.concatenate([matmul_tn(dqkv, h2, 1.0, D_QKV // 2, tk_w), matmul_tn(dz, h2, 1.0, D_SGU, tk_w)], axis=0)

    dwd1 = matmul_tn(act1, dx1, 0.5, tmm, tk_w)
    group_b = ("w_in", "w1_down")
    (da1, db1), (parts_b,) = ffn_bwd_act(dx1, full["w1_down"], a1, b1, tm_ff, tn_ff, scatter_exchange([dw_in, dwd1]))
    dwg1, dwu1 = matmul_tn(da1, h1, 1.0, tmm, tk_w), matmul_tn(db1, h1, 1.0, tmm, tk_w)
    group_c = ("w1_gate", "w1_up")
    (dx0, dg_ffn1), (parts_c,) = norm_bwd_matmul(da1, full["w1_gate"], db1, full["w1_up"], x0, g_ffn1, dx1, tm,
                                                 scatter_exchange([dwg1, dwu1]))

    late = dict(g_mix=dg_mix, g_ffn1=dg_ffn1, loss=loss_part)
    late_pack, late_rows = pack_small(list(late.values()))
    (late_parts,) = run_exchange(gather_exchange(late_pack, [late_pack.shape[0]]), "gather_late_small_grads")
    small_sums = {}
    for entries, rows, parts in ((early, early_rows, early_parts), (late, late_rows, late_parts)):
        summed = sum_parts(parts, parts.shape[1])
        off = 0
        for n, r in zip(entries, rows):
            small_sums[n] = summed[off:off + r]
            off += r
    loss = small_sums.pop("loss")[0, 0]

    grads = {}
    for group, parts in ((group_a, parts_a), (group_b, parts_b), (group_c, parts_c)):
        summed = sum_parts(parts, 32)
        off = 0
        for n in group:
            gsh = summed[off:off + rows_of[n]]
            grads[n] = (gsh.T if n in TRANSPOSED else gsh)[None]
            off += rows_of[n]
    for n, summed in small_sums.items():
        grads[n] = summed.reshape(-1)[:weights[n].size].reshape(weights[n].shape)

    delta_w, new_m, new_v = {}, {}, {}
    for n in names:
        shape = weights[n].shape
        as2d = (lambda a: a.reshape(-1, shape[-1]))
        d, m2, v2 = adamw(as2d(weights[n]), as2d(grads[n]), as2d(m_in[n]), as2d(v_in[n]))
        delta_w[n], new_m[n], new_v[n] = d.reshape(shape), m2.reshape(shape), v2.reshape(shape)

    return (loss, dx0[None], *[grads[n] for n in names], *[delta_w[n] for n in names],
            *[new_m[n] for n in names], *[new_v[n] for n in names])
```

```python
import functools
import math

import jax
import jax.numpy as jnp
from jax import lax
from jax.experimental import pallas as pl
from jax.experimental.pallas import tpu as pltpu

F32 = jnp.float32
BF16 = jnp.bfloat16

D_MODEL = 1024
D_FF = 2816
N_HEADS = 8
HEAD_DIM = 64
N_KV_HEADS = 2
KV_GROUP = N_HEADS // N_KV_HEADS
D_ATTN = N_HEADS * HEAD_DIM
D_KV = N_KV_HEADS * HEAD_DIM
D_QKV = D_ATTN + 2 * D_KV
N_SGU_GROUPS = 8
SGU_GROUP_DIM = 64
D_SGU = N_SGU_GROUPS * SGU_GROUP_DIM
CHUNK = 128
GRID_W = 64
ROPE_THETA = 10000.0
EPS = 1e-6
N_DEV = 8
LANES = 128

ONES_ROWS = 16
SAFE_SCORE_BOUND = 60.0
LOG2_E = math.log2(math.e)
Q_SCALE = HEAD_DIM ** -0.5 * LOG2_E

ADAM_LR = 0.001
ADAM_B1 = 0.9
ADAM_B2 = 0.999
ADAM_EPS = 1e-08
ADAM_WD = 0.01
ADAM_STEP = 10

MESH_AXES = ("x", "y", "c")
MESH_IDS = pl.DeviceIdType.MESH

VMEM_LIMIT = 56 * 1024 * 1024

SHARD_ROWS = (("w1_gate", D_FF // N_DEV), ("w1_up", D_FF // N_DEV), ("w1_down", D_FF // N_DEV),
              ("w_in", (D_QKV + 2 * D_SGU) // N_DEV), ("w_out", D_MODEL // N_DEV),
              ("w2_gate", D_FF // N_DEV), ("w2_up", D_FF // N_DEV), ("w2_down", D_FF // N_DEV))
PACK_ROWS = sum(r for _, r in SHARD_ROWS)
TRANSPOSED = ("w1_gate", "w1_up", "w_in", "w2_gate", "w2_up")


def _params(n_grid):
    return pltpu.CompilerParams(dimension_semantics=("arbitrary",) * n_grid, vmem_limit_bytes=VMEM_LIMIT)


def _dot(a, b):
    return jnp.dot(a, b, preferred_element_type=F32)


def _dot_nt(a, b):
    return lax.dot_general(a, b, (((1,), (1,)), ((), ())), preferred_element_type=F32)


def _dot_tn(a, b):
    return lax.dot_general(a, b, (((0,), (0,)), ((), ())), preferred_element_type=F32)


def _dot_f32(a, b):
    return jnp.dot(a, b, preferred_element_type=F32, precision=lax.Precision.HIGHEST)


def _dot_split(a, b):
    hi = a.astype(BF16)
    lo = (a - hi.astype(F32)).astype(BF16)
    return _dot(hi, b) + _dot(lo, b)


def _rstd(x):
    return lax.rsqrt(jnp.mean(x * x, axis=-1, keepdims=True) + EPS)


def _rms_bwd(dy, n, r, g):
    dn = dy * g
    return r * (dn - n * jnp.mean(dn * n, axis=-1, keepdims=True)), dy * n


def _colsum(a):
    return jnp.sum(a, axis=0, keepdims=True)


_GELU_C = math.sqrt(2.0 / math.pi)


def _gelu(x):
    t = jnp.tanh(_GELU_C * (x + 0.044715 * (x * x * x)))
    return x * (0.5 * (1.0 + t)), t


def _gelu_grad(x, t):
    return 0.5 * (1.0 + t) + 0.5 * x * (1.0 - t * t) * (_GELU_C * (1.0 + 3 * 0.044715 * x * x))


def _pair_swap(a):
    w = a.shape[-1]
    lane = lax.broadcasted_iota(jnp.int32, a.shape, a.ndim - 1)
    return jnp.where(lane % 2 == 0, pltpu.roll(a, w - 1, a.ndim - 1), pltpu.roll(a, 1, a.ndim - 1))


def _tile_lanes(a, reps):
    return jnp.concatenate([a] * reps, axis=-1) if reps > 1 else a


def _loop_pairs(n, step, carry):
    assert n % 2 == 0, n

    def pair(jj, c):
        return step(2 * jj + 1, 1, step(2 * jj, 0, c))

    return lax.fori_loop(0, n // 2, pair, carry)


def _full(shape):
    nd = len(shape)
    return pl.BlockSpec(shape, lambda *_: (0,) * nd)


def _mesh_pos():
    return lax.axis_index("x"), lax.axis_index("y"), lax.axis_index("c")


def _peer(pos, d):
    x, y, c = pos
    px = 1 - x if d & 4 else x
    py = 1 - y if d & 2 else y
    pc = 1 - c if d & 1 else c
    return (px, py, pc), 4 * px + 2 * py + pc


class _Exchange:
    def __init__(self, operands, out_shape, n_local, plan):
        self.operands = list(operands)
        self.out_shape = list(out_shape)
        self.sem_shapes = [pltpu.SemaphoreType.DMA((N_DEV - 1,)), pltpu.SemaphoreType.DMA((N_DEV - 1,)),
                           pltpu.SemaphoreType.DMA((n_local,))]
        self._plan = plan

    def _copies(self, in_refs, out_refs):
        pos = _mesh_pos()
        return pos, self._plan(4 * pos[0] + 2 * pos[1] + pos[2], in_refs, out_refs)

    def start(self, in_refs, out_refs, sems):
        send_sems, recv_sems, local_sems = sems
        pos, (local, remote, _) = self._copies(in_refs, out_refs)
        for k, (src, dst) in enumerate(local):
            pltpu.make_async_copy(src, dst, local_sems.at[k]).start()
        for d in range(1, N_DEV):
            peer, peer_lin = _peer(pos, d)
            for src, dst in remote(peer_lin):
                pltpu.make_async_remote_copy(src_ref=src, dst_ref=dst, send_sem=send_sems.at[d - 1],
                                             recv_sem=recv_sems.at[d - 1], device_id=peer,
                                             device_id_type=MESH_IDS).start()

    def wait(self, in_refs, out_refs, sems):
        send_sems, recv_sems, local_sems = sems
        pos, (local, _, whole) = self._copies(in_refs, out_refs)
        for d in range(1, N_DEV):
            peer, peer_lin = _peer(pos, d)
            ref = whole(peer_lin)
            everything = pltpu.make_async_remote_copy(src_ref=ref, dst_ref=ref, send_sem=send_sems.at[d - 1],
                                                      recv_sem=recv_sems.at[d - 1], device_id=peer,
                                                      device_id_type=MESH_IDS)
            everything.wait_send()
            everything.wait_recv()
        for k, (src, dst) in enumerate(local):
            pltpu.make_async_copy(src, dst, local_sems.at[k]).wait()


def _offsets(rows):
    offs, o = [], 0
    for r in rows:
        offs.append(o)
        o += r
    return offs


def gather_exchange(src, rows):
    offs = _offsets(rows)

    def plan(me, in_refs, out_refs):
        pieces = [(in_refs[0].at[pl.ds(o, r)], out.at[me]) for o, r, out in zip(offs, rows, out_refs)]
        return pieces, (lambda peer_lin: pieces), (lambda peer_lin: in_refs[0])

    return _Exchange([src], [jax.ShapeDtypeStruct((N_DEV, r) + src.shape[1:], src.dtype) for r in rows],
                     len(rows), plan)


def scatter_exchange(grads):
    rows = [g.shape[0] // N_DEV for g in grads]
    offs = _offsets(rows)

    def plan(me, in_refs, out_refs):
        parts = out_refs[0]

        def slabs(owner):
            return [(g.at[pl.ds(pl.multiple_of(owner * r, 16), r)], parts.at[me, pl.ds(o, r)])
                    for g, o, r in zip(in_refs, offs, rows)]

        return slabs(me), slabs, (lambda peer_lin: parts.at[peer_lin])

    shape = jax.ShapeDtypeStruct((N_DEV, sum(rows)) + grads[0].shape[1:], grads[0].dtype)
    return _Exchange(grads, [shape], len(rows), plan)


def gather_two_level(src, rows, name):
    offs = _offsets(rows)
    n_p = len(rows)

    def body(src_ref, *refs):
        outs, (send_sems, recv_sems, local_sems) = refs[:n_p], refs[n_p:]
        x, y, c = _mesh_pos()
        me, sibling = (x, y, c), (x, y, 1 - c)
        chips = [(1 - x, y), (x, 1 - y), (1 - x, 1 - y)]

        def slab(w, dev):
            return outs[w].at[4 * dev[0] + 2 * dev[1] + dev[2]]

        def copy(w, k, block, to, from_src=False):
            return pltpu.make_async_remote_copy(
                src_ref=src_ref.at[pl.ds(offs[w], rows[w])] if from_src else slab(w, block), dst_ref=slab(w, block),
                send_sem=send_sems.at[w * 7 + k], recv_sem=recv_sems.at[w * 7 + k],
                device_id=to, device_id_type=MESH_IDS)

        mine = [pltpu.make_async_copy(src_ref.at[pl.ds(offs[w], rows[w])], slab(w, me), local_sems.at[w])
                for w in range(n_p)]
        for cp in mine:
            cp.start()
        first = []
        for w in range(n_p):
            first.append(copy(w, 0, me, sibling, True))
            first += [copy(w, 1 + j, me, (*chip, c), True) for j, chip in enumerate(chips)]
        for cp in first:
            cp.start()
        passed = []
        for j, chip in enumerate(chips):
            for w in range(n_p):
                copy(w, 1 + j, (*chip, c), me).wait_recv()
                cp = copy(w, 4 + j, (*chip, c), sibling)
                cp.start()
                passed.append(cp)
        for w in range(n_p):
            copy(w, 0, sibling, me).wait_recv()
            for j, chip in enumerate(chips):
                copy(w, 4 + j, (*chip, 1 - c), me).wait_recv()
        for cp in first + passed:
            cp.wait_send()
        for cp in mine:
            cp.wait()

    any_spec = pl.BlockSpec(memory_space=pl.ANY)
    return pl.pallas_call(
        functools.partial(body), name=name,
        out_shape=[jax.ShapeDtypeStruct((N_DEV, r) + src.shape[1:], src.dtype) for r in rows],
        in_specs=[any_spec], out_specs=[any_spec] * n_p,
        scratch_shapes=[pltpu.SemaphoreType.DMA((7 * n_p,)), pltpu.SemaphoreType.DMA((7 * n_p,)),
                        pltpu.SemaphoreType.DMA((n_p,))],
        compiler_params=pltpu.CompilerParams(has_side_effects=True),
    )(src)


def run_exchange(ex, name):
    n_in, n_out = len(ex.operands), len(ex.out_shape)

    def body(*refs):
        parts = refs[:n_in], refs[n_in:n_in + n_out], refs[n_in + n_out:]
        ex.start(*parts)
        ex.wait(*parts)

    any_spec = pl.BlockSpec(memory_space=pl.ANY)
    return pl.pallas_call(
        functools.partial(body), name=name, out_shape=ex.out_shape,
        in_specs=[any_spec] * n_in, out_specs=[any_spec] * n_out, scratch_shapes=ex.sem_shapes,
        compiler_params=pltpu.CompilerParams(has_side_effects=True),
    )(*ex.operands)


def _pallas(comm, body, *, name, grid, in_specs, out_specs, out_shape, args, scratch_shapes=()):
    params = _params(len(grid))
    if comm is None:
        res = pl.pallas_call(functools.partial(body), name=name, grid=grid, in_specs=list(in_specs),
                             out_specs=list(out_specs), out_shape=list(out_shape),
                             scratch_shapes=list(scratch_shapes), compiler_params=params)(*args)
        return list(res), []
    n_in, n_out, n_scr = len(in_specs), len(out_specs), len(scratch_shapes)
    c_in, c_out = len(comm.operands), len(comm.out_shape)

    def edge(last):
        conds = [pl.program_id(a) == (g - 1 if last else 0) for a, g in enumerate(grid)]
        return functools.reduce(jnp.logical_and, conds)

    def wrapped(*refs):
        refs = list(refs)
        ins, refs = refs[:n_in], refs[n_in:]
        cins, refs = refs[:c_in], refs[c_in:]
        outs, refs = refs[:n_out], refs[n_out:]
        couts, refs = refs[:c_out], refs[c_out:]
        scr, sems = refs[:n_scr], refs[n_scr:]

        @pl.when(edge(False))
        def _():
            comm.start(cins, couts, sems)

        body(*ins, *outs, *scr)

        @pl.when(edge(True))
        def _():
            comm.wait(cins, couts, sems)

    any_spec = pl.BlockSpec(memory_space=pl.ANY)
    res = pl.pallas_call(
        wrapped, name=name, grid=grid,
        in_specs=list(in_specs) + [any_spec] * c_in, out_specs=list(out_specs) + [any_spec] * c_out,
        out_shape=list(out_shape) + comm.out_shape, scratch_shapes=list(scratch_shapes) + comm.sem_shapes,
        compiler_params=pltpu.CompilerParams(dimension_semantics=("arbitrary",) * len(grid),
                                             vmem_limit_bytes=VMEM_LIMIT, has_side_effects=True),
    )(*args, *comm.operands)
    return res[:n_out], res[n_out:]


def sum_parts(parts, block_rows):
    n, rows, cols = parts.shape

    def body(p_ref, o_ref):
        acc = p_ref[0].astype(F32)
        for s in range(1, n):
            acc = acc + p_ref[s].astype(F32)
        o_ref[...] = acc

    return pl.pallas_call(
        functools.partial(body), name="sum_parts",
        grid=(rows // block_rows,),
        in_specs=[pl.BlockSpec((n, block_rows, cols), lambda i: (0, i, 0))],
        out_specs=pl.BlockSpec((block_rows, cols), lambda i: (i, 0)),
        out_shape=jax.ShapeDtypeStruct((rows, cols), F32),
        compiler_params=_params(1),
    )(parts)


def adamw(w, g, m, v):
    def body(w_ref, g_ref, m_ref, v_ref, d_ref, m_out, v_out):
        gg = g_ref[...]
        m2 = ADAM_B1 * m_ref[...] + (1.0 - ADAM_B1) * gg
        v2 = ADAM_B2 * v_ref[...] + (1.0 - ADAM_B2) * (gg * gg)
        m_hat = m2 / (1.0 - ADAM_B1 ** ADAM_STEP)
        v_hat = v2 / (1.0 - ADAM_B2 ** ADAM_STEP)
        d_ref[...] = -ADAM_LR * (m_hat / (jnp.sqrt(v_hat) + ADAM_EPS) + ADAM_WD * w_ref[...])
        m_out[...] = m2
        v_out[...] = v2

    spec = _full(w.shape)
    shape = jax.ShapeDtypeStruct(w.shape, F32)
    return pl.pallas_call(
        functools.partial(body), name="adamw",
        in_specs=[spec] * 4, out_specs=[spec] * 3, out_shape=[shape] * 3,
        compiler_params=pltpu.CompilerParams(vmem_limit_bytes=VMEM_LIMIT),
    )(w, g, m, v)


def ffn_up(x, g, wg_t, wu_t, tm, tn, comm=None):
    t = x.shape[0]

    def body(x_ref, g_ref, wg_ref, wu_ref, h_ref, silu_ref, dgate_ref, act_ref):
        xx = x_ref[...]
        h = ((xx * _rstd(xx)) * g_ref[...]).astype(BF16)
        h_ref[...] = h
        for c in range(D_FF // tn):
            cols = slice(c * tn, (c + 1) * tn)
            a = _dot_nt(h, wg_ref[cols, :])
            b = _dot_nt(h, wu_ref[cols, :])
            sig = 0.5 * jnp.tanh(0.5 * a) + 0.5
            silu = a * sig
            silu_ref[:, cols] = silu.astype(BF16)
            dgate_ref[:, cols] = (b * (sig + silu * (1.0 - sig))).astype(BF16)
            act_ref[:, cols] = (silu * b).astype(BF16)

    wide = jax.ShapeDtypeStruct((t, D_FF), BF16)
    row = lambda n: pl.BlockSpec((tm, n), lambda i: (i, 0))
    return _pallas(
        comm, body, name="ffn_up",
        grid=(t // tm,),
        in_specs=[row(D_MODEL), _full((1, D_MODEL)), _full((D_FF, D_MODEL)), _full((D_FF, D_MODEL))],
        out_specs=[row(D_MODEL), row(D_FF), row(D_FF), row(D_FF)],
        out_shape=[jax.ShapeDtypeStruct((t, D_MODEL), BF16), wide, wide, wide],
        args=(x, g, wg_t, wu_t))


def ffn_down(act, wd, x, tm):
    t = x.shape[0]

    def body(act_ref, wd_ref, x_ref, o_ref):
        o_ref[...] = x_ref[...] + 0.5 * _dot(act_ref[...], wd_ref[...])

    return pl.pallas_call(
        functools.partial(body), name="ffn_down",
        grid=(t // tm,),
        in_specs=[pl.BlockSpec((tm, D_FF), lambda i: (i, 0)), _full((D_FF, D_MODEL)),
                  pl.BlockSpec((tm, D_MODEL), lambda i: (i, 0))],
        out_specs=pl.BlockSpec((tm, D_MODEL), lambda i: (i, 0)),
        out_shape=jax.ShapeDtypeStruct((t, D_MODEL), F32),
        compiler_params=_params(1),
    )(act, wd, x)


def ffn_bwd_act(dx, wd, silu, dgate, tm, tn, comm=None):
    t = dx.shape[0]

    def body(dx_ref, wd_ref, silu_ref, dgate_ref, da_ref, db_ref):
        dxb = (0.5 * dx_ref[...]).astype(BF16)
        for c in range(D_FF // tn):
            cols = slice(c * tn, (c + 1) * tn)
            dact = _dot_nt(dxb, wd_ref[cols, :])
            da_ref[:, cols] = (dact * dgate_ref[:, cols].astype(F32)).astype(BF16)
            db_ref[:, cols] = (dact * silu_ref[:, cols].astype(F32)).astype(BF16)

    wide = jax.ShapeDtypeStruct((t, D_FF), BF16)
    row = lambda n: pl.BlockSpec((tm, n), lambda i: (i, 0))
    return _pallas(
        comm, body, name="ffn_bwd_act",
        grid=(t // tm,),
        in_specs=[row(D_MODEL), _full((D_FF, D_MODEL)), row(D_FF), row(D_FF)],
        out_specs=[row(D_FF), row(D_FF)],
        out_shape=[wide, wide],
        args=(dx, wd, silu, dgate))


def norm_bwd_matmul(a1, w1, a2, w2, x, g, dx_in, tm, comm=None):
    t = x.shape[0]
    k1, k2 = a1.shape[1], a2.shape[1]

    def body(a1_ref, w1_ref, a2_ref, w2_ref, x_ref, g_ref, dxin_ref, dx_ref, dg_ref):
        dh = _dot(a1_ref[...], w1_ref[...]) + _dot(a2_ref[...], w2_ref[...])
        xx = x_ref[...]
        r = _rstd(xx)
        dx, dg_rows = _rms_bwd(dh, xx * r, r, g_ref[...])
        dx_ref[...] = dxin_ref[...] + dx

        @pl.when(pl.program_id(0) == 0)
        def _():
            dg_ref[...] = jnp.zeros_like(dg_ref)

        dg_ref[...] += _colsum(dg_rows)

    row = pl.BlockSpec((tm, D_MODEL), lambda i: (i, 0))
    return _pallas(
        comm, body, name="norm_bwd_matmul",
        grid=(t // tm,),
        in_specs=[pl.BlockSpec((tm, k1), lambda i: (i, 0)), _full((k1, D_MODEL)),
                  pl.BlockSpec((tm, k2), lambda i: (i, 0)), _full((k2, D_MODEL)),
                  row, _full((1, D_MODEL)), row],
        out_specs=[row, _full((1, D_MODEL))],
        out_shape=[jax.ShapeDtypeStruct((t, D_MODEL), F32), jax.ShapeDtypeStruct((1, D_MODEL), F32)],
        args=(a1, w1, a2, w2, x, g, dx_in))


def matmul_tn(a, b, scale, tmm, tk):
    t, m = a.shape
    n = b.shape[1]
    nk = t // tk

    def body(a_ref, b_ref, o_ref, acc_ref):
        k = pl.program_id(1)

        @pl.when(k == 0)
        def _():
            acc_ref[...] = jnp.zeros_like(acc_ref)

        acc_ref[...] += _dot_tn(a_ref[...].astype(BF16), b_ref[...].astype(BF16))

        @pl.when(k == nk - 1)
        def _():
            o_ref[...] = (scale * acc_ref[...]).astype(BF16)

    return pl.pallas_call(
        functools.partial(body), name="matmul_tn",
        grid=(m // tmm, nk),
        in_specs=[pl.BlockSpec((tk, tmm), lambda i, k: (k, i)), pl.BlockSpec((tk, n), lambda i, k: (k, 0))],
        out_specs=pl.BlockSpec((tmm, n), lambda i, k: (i, 0)),
        out_shape=jax.ShapeDtypeStruct((m, n), BF16),
        scratch_shapes=[pltpu.VMEM((tmm, n), F32)],
        compiler_params=_params(2),
    )(a, b)


def input_projection(x, g, w_qkv_t, w_z_t, tm):
    t = x.shape[0]

    def body(x_ref, g_ref, wq_ref, wz_ref, qkv_ref, z_ref, h_ref):
        xx = x_ref[...]
        h = ((xx * _rstd(xx)) * g_ref[...]).astype(BF16)
        h_ref[...] = h
        qkv_ref[...] = _dot_nt(h, wq_ref[...])
        z_ref[...] = _dot_nt(h, wz_ref[...])

    row = lambda n: pl.BlockSpec((tm, n), lambda i: (i, 0))
    return pl.pallas_call(
        functools.partial(body), name="input_projection", grid=(t // tm,),
        in_specs=[row(D_MODEL), _full((1, D_MODEL)), _full((D_QKV, D_MODEL)), _full((2 * D_SGU, D_MODEL))],
        out_specs=[row(D_QKV), row(2 * D_SGU), row(D_MODEL)],
        out_shape=[jax.ShapeDtypeStruct((t, D_QKV), F32), jax.ShapeDtypeStruct((t, 2 * D_SGU), F32),
                   jax.ShapeDtypeStruct((t, D_MODEL), BF16)],
        compiler_params=_params(1))(x, g, w_qkv_t, w_z_t)


def _shift_rows(shape, first, second):
    row = lax.broadcasted_iota(jnp.int32, shape, len(shape) - 2)
    return jnp.where(row == 0, first, jnp.where(row == 1, second, 0.0))


def _hi_lo(a):
    hi = a.astype(BF16).astype(F32)
    return hi, a - hi


def _head_tile_spec(tm, rows):
    return pl.BlockSpec((N_HEADS, None, rows, tm), lambda i: (0, i, 0, 0))


def _to_head_tiles(a):
    return a.T.reshape(N_HEADS, HEAD_DIM, a.shape[0])


def _from_head_tiles(a):
    return a.reshape(D_ATTN, a.shape[-1]).T


def _head_mean_matrix(width):
    head = jnp.arange(width) // HEAD_DIM
    return (head[:, None] == head[None, :]).astype(F32) / HEAD_DIM


def _kv_tile_spec(n_sub, rows, cols):
    return pl.BlockSpec((N_KV_HEADS, n_sub, rows, cols), lambda i: (0, i, 0, 0))


def qk_prep(qkv, gq_w, gk_w, cos_w, sin_w, mean_q, mean_k, tm, tk, tk_v):
    t = qkv.shape[0]
    n_sub, n_sub_v = tm // tk, tm // tk_v

    def body(p_ref, gq_ref, gk_ref, cos_ref, sin_ref, mq_ref, mk_ref, q_ref, k_ref, kt_ref, vt_ref, vtb_ref,
             qmax_ref, kmax_ref):
        @pl.when(pl.program_id(0) == 0)
        def _():
            qmax_ref[...] = jnp.zeros_like(qmax_ref)
            kmax_ref[...] = jnp.zeros_like(kmax_ref)

        cos2, sin2 = cos_ref[...], sin_ref[...]
        q = p_ref[:, :D_ATTN]
        k = p_ref[:, D_ATTN:D_ATTN + D_KV]
        qn = q * lax.rsqrt(_dot_split(q * q, mq_ref[...]) + EPS) * gq_ref[...]
        kn = k * lax.rsqrt(_dot_split(k * k, mk_ref[...]) + EPS) * gk_ref[...]
        cos8, sin8 = _tile_lanes(cos2, D_ATTN // LANES), _tile_lanes(sin2, D_ATTN // LANES)
        q_rot = (qn * cos8 + _pair_swap(qn) * sin8) * Q_SCALE
        q_ref[...] = _to_head_tiles(q_rot).astype(BF16)
        k_rot = kn * cos2 + _pair_swap(kn) * sin2
        q_sq = HEAD_DIM * _dot_split(q_rot * q_rot, mq_ref[...])
        k_sq = HEAD_DIM * _dot_split(k_rot * k_rot, mk_ref[...])
        qmax_ref[...] = jnp.maximum(qmax_ref[...], jnp.max(q_sq, axis=0, keepdims=True))
        kmax_ref[...] = jnp.maximum(kmax_ref[...], jnp.max(k_sq, axis=0, keepdims=True))
        vv = p_ref[:, D_ATTN + D_KV:]
        second = pltpu.roll(k_rot, HEAD_DIM, 1)
        for c in range(n_sub):
            rows = slice(c * tk, (c + 1) * tk)
            k_ref[0, c] = k_rot[rows, :HEAD_DIM].astype(BF16)
            k_ref[1, c] = second[rows, :HEAD_DIM].astype(BF16)
        for a, feat_ref, width, n in ((k_rot, kt_ref, tk, n_sub), (vv, vtb_ref, tk, n_sub), (vv, vt_ref, tk_v, n_sub_v)):
            for c in range(n):
                tile = a[c * width:(c + 1) * width].T.reshape(N_KV_HEADS, HEAD_DIM, width)
                feat_ref[:, c, :HEAD_DIM, :] = tile.astype(BF16)
        vt_ref[:, :, HEAD_DIM:, :] = jnp.ones((N_KV_HEADS, n_sub_v, ONES_ROWS, tk_v), BF16)
        minus = _shift_rows((N_KV_HEADS, n_sub, HEAD_DIM, tk), -1.0, -1.0).astype(BF16)
        kt_ref[:, :, HEAD_DIM:, :] = minus
        vtb_ref[:, :, HEAD_DIM:, :] = minus

    kv = lambda rows, cols: jax.ShapeDtypeStruct((N_KV_HEADS, t // tk, rows, cols), BF16)
    return pl.pallas_call(
        functools.partial(body), name="qk_prep", grid=(t // tm,),
        in_specs=[pl.BlockSpec((tm, D_QKV), lambda i: (i, 0)), _full((1, D_ATTN)), _full((1, D_KV)),
                  pl.BlockSpec((tm, LANES), lambda i: (i, 0)), pl.BlockSpec((tm, LANES), lambda i: (i, 0)),
                  _full((D_ATTN, D_ATTN)), _full((D_KV, D_KV))],
        out_specs=[_head_tile_spec(tm, HEAD_DIM), _kv_tile_spec(n_sub, tk, HEAD_DIM),
                   _kv_tile_spec(n_sub, 2 * HEAD_DIM, tk),
                   _kv_tile_spec(n_sub_v, HEAD_DIM + ONES_ROWS, tk_v), _kv_tile_spec(n_sub, 2 * HEAD_DIM, tk),
                   _full((1, D_ATTN)), _full((1, D_KV))],
        out_shape=[jax.ShapeDtypeStruct((N_HEADS, t // tm, HEAD_DIM, tm), BF16), kv(tk, HEAD_DIM), kv(2 * HEAD_DIM, tk),
                   jax.ShapeDtypeStruct((N_KV_HEADS, t // tk_v, HEAD_DIM + ONES_ROWS, tk_v), BF16),
                   kv(2 * HEAD_DIM, tk),
                   jax.ShapeDtypeStruct((1, D_ATTN), F32), jax.ShapeDtypeStruct((1, D_KV), F32)],
        compiler_params=_params(1),
    )(qkv, gq_w, gk_w, cos_w, sin_w, mean_q, mean_k)


def qk_bwd(dq_rot, dk_rot, dv, qkv, gq_w, gk_w, cos_w, sin_w, mean_q, mean_k, tm):
    t = qkv.shape[0]
    tk = dk_rot.shape[-1]
    n_sub = tm // tk

    def token_major(ref):
        return jnp.concatenate([ref[:, c].reshape(D_KV, tk).T for c in range(n_sub)], axis=0)

    def branch(raw, d_rot, gain, mean_mat, cos, sin, scale):
        r = lax.rsqrt(_dot_split(raw * raw, mean_mat) + EPS)
        n = raw * r
        dy = (d_rot * cos - _pair_swap(d_rot) * sin) * scale
        dn = dy * gain
        return r * (dn - n * _dot_split(dn * n, mean_mat)), dy * n

    def body(dq_ref, dk_ref, dv_ref, p_ref, gq_ref, gk_ref, cos_ref, sin_ref, mq_ref, mk_ref,
             dp_ref, dgq_ref, dgk_ref):
        cos2, sin2 = cos_ref[...], sin_ref[...]
        cos8, sin8 = _tile_lanes(cos2, D_ATTN // LANES), _tile_lanes(sin2, D_ATTN // LANES)
        dq, dgq = branch(p_ref[:, :D_ATTN], _from_head_tiles(dq_ref[...]), gq_ref[...], mq_ref[...], cos8, sin8,
                         HEAD_DIM ** -0.5)
        dk, dgk = branch(p_ref[:, D_ATTN:D_ATTN + D_KV], token_major(dk_ref), gk_ref[...], mk_ref[...], cos2, sin2, 1.0)
        dp_ref[...] = jnp.concatenate([dq, dk, token_major(dv_ref)], axis=-1).astype(BF16)

        @pl.when(pl.program_id(0) == 0)
        def _():
            dgq_ref[...] = jnp.zeros_like(dgq_ref)
            dgk_ref[...] = jnp.zeros_like(dgk_ref)

        dgq_ref[...] += _colsum(dgq)
        dgk_ref[...] += _colsum(dgk)

    return pl.pallas_call(
        functools.partial(body), name="qk_bwd", grid=(t // tm,),
        in_specs=[_head_tile_spec(tm, HEAD_DIM), _kv_tile_spec(n_sub, HEAD_DIM, tk),
                  _kv_tile_spec(n_sub, HEAD_DIM, tk), pl.BlockSpec((tm, D_QKV), lambda i: (i, 0)),
                  _full((1, D_ATTN)), _full((1, D_KV)),
                  pl.BlockSpec((tm, LANES), lambda i: (i, 0)), pl.BlockSpec((tm, LANES), lambda i: (i, 0)),
                  _full((D_ATTN, D_ATTN)), _full((D_KV, D_KV))],
        out_specs=[pl.BlockSpec((tm, D_QKV), lambda i: (i, 0)), _full((1, D_ATTN)), _full((1, D_KV))],
        out_shape=[jax.ShapeDtypeStruct((t, D_QKV), BF16), jax.ShapeDtypeStruct((1, D_ATTN), F32),
                   jax.ShapeDtypeStruct((1, D_KV), F32)],
        compiler_params=_params(1),
    )(dq_rot, dk_rot, dv, qkv, gq_w, gk_w, cos_w, sin_w, mean_q, mean_k)


def attention_fwd(bound, q_t, k, v_t, comm=None):
    _, nq, _, tq = q_t.shape
    _, nk, tk, _ = k.shape

    def body(bound_ref, q_ref, k_ref, v_ref, o_ref, qtok_ref, s_scr, p_scr):
        head_bound = bound_ref[pl.program_id(0)]
        safe = head_bound <= SAFE_SCORE_BOUND
        q = q_ref[...]
        s_scr[0] = _dot(k_ref[0], q)
        p_scr[1] = jnp.zeros((tk, tq), BF16)
        zero = jnp.zeros((HEAD_DIM + ONES_ROWS, tq), F32)

        def matmuls(j, slot):
            pv = _dot(v_ref[jnp.maximum(j - 1, 0)], p_scr[1 - slot])
            s_scr[1 - slot] = _dot(k_ref[jnp.minimum(j + 1, nk - 1)], q)
            return pv

        def finish(m, acc):
            acc = acc + _dot(v_ref[nk - 1], p_scr[(nk - 1) % 2])
            l = acc[HEAD_DIM:HEAD_DIM + 1]
            o_ref[...] = acc[:HEAD_DIM] / l
            lse_rows = _shift_rows((HEAD_DIM, tq), *_hi_lo(m + jnp.log2(l)))
            qtok_ref[...] = jnp.concatenate([q.astype(F32), lse_rows], axis=0).T.astype(BF16)

        @pl.when(safe)
        def _():
            m = jnp.full((1, tq), head_bound, F32)

            def step(j, slot, acc):
                s = s_scr[slot]
                pv = matmuls(j, slot)
                p_scr[slot] = jnp.exp2(s - m).astype(BF16)
                return acc + pv

            finish(m, _loop_pairs(nk, step, zero))

        @pl.when(jnp.logical_not(safe))
        def _():
            def step(j, slot, carry):
                m, acc = carry
                s = s_scr[slot]
                pv = matmuls(j, slot)
                m_new = jnp.maximum(m, jnp.max(s, axis=0, keepdims=True))
                p_scr[slot] = jnp.exp2(s - m_new).astype(BF16)
                return m_new, jnp.exp2(m - m_new) * (acc + pv)

            finish(*_loop_pairs(nk, step, (jnp.full((1, tq), -1e30, F32), zero)))

    return _pallas(
        comm, body, name="attention_fwd", grid=(N_HEADS, nq),
        in_specs=[pl.BlockSpec(memory_space=pltpu.SMEM),
                  pl.BlockSpec((None, None, HEAD_DIM, tq), lambda h, i: (h, i, 0, 0)),
                  pl.BlockSpec((None, nk, tk, HEAD_DIM), lambda h, i: (h // KV_GROUP, 0, 0, 0)),
                  pl.BlockSpec((None, nk, HEAD_DIM + ONES_ROWS, tk), lambda h, i: (h // KV_GROUP, 0, 0, 0))],
        out_specs=[pl.BlockSpec((None, None, HEAD_DIM, tq), lambda h, i: (h, i, 0, 0)),
                   pl.BlockSpec((None, None, tq, 2 * HEAD_DIM), lambda h, i: (h, i, 0, 0))],
        out_shape=[jax.ShapeDtypeStruct((N_HEADS, nq, HEAD_DIM, tq), F32),
                   jax.ShapeDtypeStruct((N_HEADS, nq, tq, 2 * HEAD_DIM), BF16)],
        scratch_shapes=[pltpu.VMEM((2, tk, tq), F32), pltpu.VMEM((2, tk, tq), BF16)],
        args=(bound, q_t, k, v_t))


def attention_bwd(q_tok, do_tok, q_t, do_t, k_t, v_t, comm=None):
    _, nq, _, tq = q_t.shape
    _, nk, _, tk = k_t.shape

    def body(qtok_ref, dotok_ref, q_ref, do_ref, kt_ref, vt_ref, dq_ref, dk_ref, dv_ref,
             s_scr, dp_scr, p_scr, ds_scr):
        @pl.when(pl.program_id(1) == 0)
        def _():
            dq_ref[...] = jnp.zeros_like(dq_ref)

        kt_aug, vt_aug = kt_ref[...], vt_ref[...]
        kt = kt_aug[:HEAD_DIM]
        n = KV_GROUP * nq
        s_scr[0] = _dot(qtok_ref[0, 0], kt_aug)
        dp_scr[0] = _dot(dotok_ref[0, 0], vt_aug)
        p_scr[1] = jnp.zeros((tq, tk), BF16)
        ds_scr[1] = jnp.zeros((tq, tk), BF16)

        def products(t, slot, dk, dv):
            h, i = t // nq, t % nq
            ds = ds_scr[slot]
            dq_ref[h, i] += _dot_nt(kt, ds)
            return dk + _dot(q_ref[h, i], ds), dv + _dot(do_ref[h, i], p_scr[slot])

        def step(t, slot, carry):
            s, dp = s_scr[slot], dp_scr[slot]
            dk, dv = products(jnp.maximum(t - 1, 0), 1 - slot, *carry)
            nxt = jnp.minimum(t + 1, n - 1)
            s_scr[1 - slot] = _dot(qtok_ref[nxt // nq, nxt % nq], kt_aug)
            dp_scr[1 - slot] = _dot(dotok_ref[nxt // nq, nxt % nq], vt_aug)
            p = jnp.exp2(s)
            p_scr[slot] = p.astype(BF16)
            ds_scr[slot] = (p * dp).astype(BF16)
            return dk, dv

        zero = jnp.zeros((HEAD_DIM, tk), F32)
        dk, dv = products(n - 1, (n - 1) % 2, *_loop_pairs(n, step, (zero, zero)))
        dk_ref[...] = dk * (1.0 / LOG2_E)
        dv_ref[...] = dv

    group = lambda g, j: (g, 0, 0, 0)
    tile = lambda g, j: (g, j, 0, 0)
    once = pl.Buffered(1)
    return _pallas(
        comm, body, name="attention_bwd", grid=(N_KV_HEADS, nk),
        in_specs=[pl.BlockSpec((KV_GROUP, nq, tq, 2 * HEAD_DIM), group, pipeline_mode=once),
                  pl.BlockSpec((KV_GROUP, nq, tq, 2 * HEAD_DIM), group, pipeline_mode=once),
                  pl.BlockSpec((KV_GROUP, nq, HEAD_DIM, tq), group, pipeline_mode=once),
                  pl.BlockSpec((KV_GROUP, nq, HEAD_DIM, tq), group, pipeline_mode=once),
                  pl.BlockSpec((None, None, 2 * HEAD_DIM, tk), tile),
                  pl.BlockSpec((None, None, 2 * HEAD_DIM, tk), tile)],
        out_specs=[pl.BlockSpec((KV_GROUP, nq, HEAD_DIM, tq), group),
                   pl.BlockSpec((None, None, HEAD_DIM, tk), tile),
                   pl.BlockSpec((None, None, HEAD_DIM, tk), tile)],
        out_shape=[jax.ShapeDtypeStruct((N_HEADS, nq, HEAD_DIM, tq), F32),
                   jax.ShapeDtypeStruct((N_KV_HEADS, nk, HEAD_DIM, tk), F32),
                   jax.ShapeDtypeStruct((N_KV_HEADS, nk, HEAD_DIM, tk), F32)],
        scratch_shapes=[pltpu.VMEM((2, tq, tk), F32), pltpu.VMEM((2, tq, tk), F32),
                        pltpu.VMEM((2, tq, tk), BF16), pltpu.VMEM((2, tq, tk), BF16)],
        args=(q_tok, do_tok, q_t, do_t, k_t, v_t))


def _group_select(parts):
    lane_group = lax.broadcasted_iota(jnp.int32, parts[0].shape, 1) // SGU_GROUP_DIM
    out = parts[0]
    for g in range(1, N_SGU_GROUPS):
        out = jnp.where(lane_group == g, parts[g], out)
    return out


def _gate_forward(z, g_sgu, ws_ref, bias):
    gz, th = _gelu(z)
    u, vv = gz[:, :D_SGU], gz[:, D_SGU:]
    rv = _rstd(vv)
    nv = vv * rv
    vn = (nv * g_sgu).astype(BF16)
    fs = []
    for c in range(z.shape[0] // CHUNK):
        vc = vn[c * CHUNK:(c + 1) * CHUNK]
        fs.append(_group_select([_dot(ws_ref[g], vc) for g in range(N_SGU_GROUPS)]) + bias)
    f = jnp.concatenate(fs, axis=0) if len(fs) > 1 else fs[0]
    return th, u, rv, nv, vn, f


def mix_out(z, o, x, g_sgu, g_ao, g_so, ws, bias, w_out, tm):
    t = x.shape[0]

    def body(z_ref, o_ref, x_ref, gs_ref, gao_ref, gso_ref, ws_ref, bias_ref, wout_ref, x2_ref, mixed_ref):
        _, u, _, _, _, f = _gate_forward(z_ref[...], gs_ref[...], ws_ref, bias_ref[...])
        sgu = u * f
        oo = _from_head_tiles(o_ref[...])
        mixed = jnp.concatenate([oo * _rstd(oo) * gao_ref[...], sgu * _rstd(sgu) * gso_ref[...]], axis=-1).astype(BF16)
        mixed_ref[...] = mixed
        x2_ref[...] = x_ref[...] + _dot(mixed, wout_ref[...])

    row = lambda n: pl.BlockSpec((tm, n), lambda i: (i, 0))
    return pl.pallas_call(
        functools.partial(body), name="mix_out", grid=(t // tm,),
        in_specs=[row(2 * D_SGU), _head_tile_spec(tm, HEAD_DIM), row(D_MODEL), _full((1, D_SGU)), _full((1, D_ATTN)),
                  _full((1, D_SGU)),
                  _full((N_SGU_GROUPS, CHUNK, CHUNK)), _full((CHUNK, D_SGU)), _full((D_MODEL, D_MODEL))],
        out_specs=[row(D_MODEL), row(D_MODEL)],
        out_shape=[jax.ShapeDtypeStruct((t, D_MODEL), F32), jax.ShapeDtypeStruct((t, D_MODEL), BF16)],
        compiler_params=_params(1),
    )(z, o, x, g_sgu, g_ao, g_so, ws, bias, w_out)


def mix_bwd(dx2, z, o, g_sgu, g_ao, g_so, ws, ws_t, bias, w_out, group_ind, tm):
    t = dx2.shape[0]
    n_tiles = t // tm

    def body(dx_ref, z_ref, o_ref, gs_ref, gao_ref, gso_ref, ws_ref, wst_ref, bias_ref, wout_ref, ind_ref,
             do_ref, dotok_ref, dz_ref, dg_ref, dws_ref, dbs_ref, df_sum):
        step = pl.program_id(0)

        @pl.when(step == 0)
        def _():
            dg_ref[...] = jnp.zeros_like(dg_ref)
            dws_ref[...] = jnp.zeros_like(dws_ref)
            df_sum[...] = jnp.zeros_like(df_sum)

        z = z_ref[...]
        th, u, rv, nv, vn, f = _gate_forward(z, gs_ref[...], ws_ref, bias_ref[...])
        dmixed = _dot_nt(dx_ref[...].astype(BF16), wout_ref[...])
        o_tiles = o_ref[...]
        oo = _from_head_tiles(o_tiles)
        ro = _rstd(oo)
        d_o, dgao = _rms_bwd(dmixed[:, :D_ATTN], oo * ro, ro, gao_ref[...])
        do_tiles = _to_head_tiles(d_o)
        do_ref[...] = do_tiles.astype(BF16)
        delta_hi, delta_lo = _hi_lo(jnp.sum(do_tiles * o_tiles, axis=1, keepdims=True))
        for h in range(N_HEADS):
            delta_rows = _shift_rows((HEAD_DIM, tm), delta_hi[h], delta_lo[h])
            dotok_ref[h] = jnp.concatenate([do_tiles[h], delta_rows], axis=0).T.astype(BF16)
        sgu = u * f
        rs = _rstd(sgu)
        dsgu, dgso = _rms_bwd(dmixed[:, D_ATTN:], sgu * rs, rs, gso_ref[...])
        du = dsgu * f
        df = dsgu * u
        lane_group = lax.broadcasted_iota(jnp.int32, (CHUNK, D_SGU), 1) // SGU_GROUP_DIM
        dvns = []
        df_acc = jnp.zeros((CHUNK, D_SGU), F32)
        for c in range(tm // CHUNK):
            dfc32 = df[c * CHUNK:(c + 1) * CHUNK]
            dfc = dfc32.astype(BF16)
            vc = vn[c * CHUNK:(c + 1) * CHUNK]
            dvns.append(_group_select([_dot(wst_ref[g], dfc) for g in range(N_SGU_GROUPS)]))
            for g in range(N_SGU_GROUPS):
                dws_ref[g] += _dot_nt(jnp.where(lane_group == g, dfc, jnp.zeros_like(dfc)), vc)
            df_acc = df_acc + dfc32
        df_sum[...] += df_acc
        dvn = jnp.concatenate(dvns, axis=0) if len(dvns) > 1 else dvns[0]
        dvv, dgs = _rms_bwd(dvn, nv, rv, gs_ref[...])
        dz_ref[...] = (jnp.concatenate([du, dvv], axis=-1) * _gelu_grad(z, th)).astype(BF16)
        dg_ref[0:1, :] += _colsum(dgao)
        dg_ref[1:2, :] += _colsum(dgso)
        dg_ref[2:3, :] += _colsum(dgs)

        @pl.when(step == n_tiles - 1)
        def _():
            dbs_ref[...] = _dot_f32(df_sum[...], ind_ref[...])

    row = lambda n: pl.BlockSpec((tm, n), lambda i: (i, 0))
    return pl.pallas_call(
        functools.partial(body), name="mix_bwd", grid=(n_tiles,),
        in_specs=[row(D_MODEL), row(2 * D_SGU), _head_tile_spec(tm, HEAD_DIM), _full((1, D_SGU)), _full((1, D_ATTN)),
                  _full((1, D_SGU)),
                  _full((N_SGU_GROUPS, CHUNK, CHUNK)), _full((N_SGU_GROUPS, CHUNK, CHUNK)), _full((CHUNK, D_SGU)),
                  _full((D_MODEL, D_MODEL)), _full((D_SGU, LANES))],
        out_specs=[_head_tile_spec(tm, HEAD_DIM), pl.BlockSpec((N_HEADS, None, tm, 2 * HEAD_DIM), lambda i: (0, i, 0, 0)),
                   row(2 * D_SGU), _full((8, D_SGU)),
                   _full((N_SGU_GROUPS, CHUNK, CHUNK)), _full((CHUNK, LANES))],
        out_shape=[jax.ShapeDtypeStruct((N_HEADS, n_tiles, HEAD_DIM, tm), BF16),
                   jax.ShapeDtypeStruct((N_HEADS, n_tiles, tm, 2 * HEAD_DIM), BF16),
                   jax.ShapeDtypeStruct((t, 2 * D_SGU), BF16),
                   jax.ShapeDtypeStruct((8, D_SGU), F32),
                   jax.ShapeDtypeStruct((N_SGU_GROUPS, CHUNK, CHUNK), F32),
                   jax.ShapeDtypeStruct((CHUNK, LANES), F32)],
        scratch_shapes=[pltpu.VMEM((CHUNK, D_SGU), F32)],
        compiler_params=_params(1),
    )(dx2, z, o, g_sgu, g_ao, g_so, ws, ws_t, bias, w_out, group_ind)


def ffn_down_loss(act, wd, x, g, target, tm):
    t = x.shape[0]

    def body(act_ref, wd_ref, x_ref, g_ref, t_ref, loss_ref, dx_ref, dg_ref):
        @pl.when(pl.program_id(0) == 0)
        def _():
            loss_ref[...] = jnp.zeros_like(loss_ref)
            dg_ref[...] = jnp.zeros_like(dg_ref)

        xx = x_ref[...] + 0.5 * _dot(act_ref[...], wd_ref[...])
        r = _rstd(xx)
        n = xx * r
        err = n * g_ref[...] - t_ref[...]
        per_token = jnp.mean(err * err, axis=-1, keepdims=True)
        loss_ref[...] += 0.5 * jnp.sum(per_token, axis=0, keepdims=True)
        dx, dg_rows = _rms_bwd(err * (1.0 / D_MODEL), n, r, g_ref[...])
        dx_ref[...] = dx
        dg_ref[...] += _colsum(dg_rows)

    row = pl.BlockSpec((tm, D_MODEL), lambda i: (i, 0))
    return pl.pallas_call(
        functools.partial(body), name="ffn_down_loss", grid=(t // tm,),
        in_specs=[pl.BlockSpec((tm, D_FF), lambda i: (i, 0)), _full((D_FF, D_MODEL)), row, _full((1, D_MODEL)), row],
        out_specs=[_full((1, LANES)), row, _full((1, D_MODEL))],
        out_shape=[jax.ShapeDtypeStruct((1, LANES), F32), jax.ShapeDtypeStruct((t, D_MODEL), F32),
                   jax.ShapeDtypeStruct((1, D_MODEL), F32)],
        compiler_params=_params(1),
    )(act, wd, x, g, target)


def _rope_tables(t):
    rows = t // GRID_W
    row_idx = jnp.repeat(jnp.arange(rows, dtype=F32), GRID_W)
    col_idx = jnp.tile(jnp.arange(GRID_W, dtype=F32), rows)
    axis_dim = HEAD_DIM // 2
    inv = 1.0 / (ROPE_THETA ** (jnp.arange(0, axis_dim, 2, dtype=F32) / axis_dim))
    ang = jnp.concatenate([row_idx[:, None] * inv, col_idx[:, None] * inv], axis=-1)
    cos = jnp.repeat(jnp.cos(ang), 2, axis=-1)
    sin = jnp.repeat(jnp.sin(ang), 2, axis=-1) * jnp.tile(jnp.array([-1.0, 1.0], F32), HEAD_DIM // 2)
    return jnp.tile(cos, (1, LANES // HEAD_DIM)), jnp.tile(sin, (1, LANES // HEAD_DIM))


def _heads_to_tiles_t(a, n_heads, tile):
    t = a.shape[0]
    return a.reshape(t // tile, tile, n_heads, HEAD_DIM).transpose(2, 0, 3, 1)


def _heads_to_tiles(a, n_heads, tile):
    t = a.shape[0]
    return a.reshape(t // tile, tile, n_heads, HEAD_DIM).transpose(2, 0, 1, 3)


def _tiles_t_to_heads(a):
    h, n, _, tile = a.shape
    return a.transpose(1, 3, 0, 2).reshape(n * tile, h * HEAD_DIM)


def kernel(x, g_ffn1, w1_gate, w1_up, w1_down, g_mix, w_in, g_q, g_k, g_sgu, w_s, b_s, g_attn_out, g_sgu_out, w_out, g_ffn2, w2_gate, w2_up, w2_down, g_final, loss_target, m_g_ffn1, m_w1_gate, m_w1_up, m_w1_down, m_g_mix, m_w_in, m_g_q, m_g_k, m_g_sgu, m_w_s, m_b_s, m_g_attn_out, m_g_sgu_out, m_w_out, m_g_ffn2, m_w2_gate, m_w2_up, m_w2_down, m_g_final, v_g_ffn1, v_w1_gate, v_w1_up, v_w1_down, v_g_mix, v_w_in, v_g_q, v_g_k, v_g_sgu, v_w_s, v_b_s, v_g_attn_out, v_g_sgu_out, v_w_out, v_g_ffn2, v_w2_gate, v_w2_up, v_w2_down, v_g_final):
    weights = dict(g_ffn1=g_ffn1, w1_gate=w1_gate, w1_up=w1_up, w1_down=w1_down, g_mix=g_mix, w_in=w_in, g_q=g_q,
                   g_k=g_k, g_sgu=g_sgu, w_s=w_s, b_s=b_s, g_attn_out=g_attn_out, g_sgu_out=g_sgu_out, w_out=w_out,
                   g_ffn2=g_ffn2, w2_gate=w2_gate, w2_up=w2_up, w2_down=w2_down, g_final=g_final)
    m_in = dict(g_ffn1=m_g_ffn1, w1_gate=m_w1_gate, w1_up=m_w1_up, w1_down=m_w1_down, g_mix=m_g_mix, w_in=m_w_in,
                g_q=m_g_q, g_k=m_g_k, g_sgu=m_g_sgu, w_s=m_w_s, b_s=m_b_s, g_attn_out=m_g_attn_out,
                g_sgu_out=m_g_sgu_out, w_out=m_w_out, g_ffn2=m_g_ffn2, w2_gate=m_w2_gate, w2_up=m_w2_up,
                w2_down=m_w2_down, g_final=m_g_final)
    v_in = dict(g_ffn1=v_g_ffn1, w1_gate=v_w1_gate, w1_up=v_w1_up, w1_down=v_w1_down, g_mix=v_g_mix, w_in=v_w_in,
                g_q=v_g_q, g_k=v_g_k, g_sgu=v_g_sgu, w_s=v_w_s, b_s=v_b_s, g_attn_out=v_g_attn_out,
                g_sgu_out=v_g_sgu_out, w_out=v_w_out, g_ffn2=v_g_ffn2, w2_gate=v_w2_gate, w2_up=v_w2_up,
                w2_down=v_w2_down, g_final=v_g_final)
    names = list(weights)

    t = x.shape[1]
    x0 = x[0]
    target = loss_target[0]
    tm = min(256, t)
    tm_ff = min(256, t)
    tn_ff = 256
    tq = min(512, t)
    tk = min(256, t)
    tk_fwd = min(512, t)
    tk_w = min(2048, t)

    def shard_rows(name):
        w = weights[name][0]
        return (w.T if name in TRANSPOSED else w).astype(BF16)

    rows_of = dict(SHARD_ROWS)
    full = {}

    def packed(group):
        return jnp.concatenate([shard_rows(n) for n in group], axis=0), [rows_of[n] for n in group]

    def gather_of(group):
        return gather_exchange(*packed(group))

    def take(group, gathered):
        for n, g in zip(group, gathered):
            full[n] = g.reshape(N_DEV * rows_of[n], D_MODEL)

    first, second, third = ("w1_gate", "w1_up"), ("w1_down", "w_in", "w_out"), ("w2_gate", "w2_up", "w2_down")
    take(first, gather_two_level(*packed(first), "gather_first"))

    (h1, a1, b1, act1), gathered = ffn_up(x0, g_ffn1, full["w1_gate"], full["w1_up"], tm_ff, tn_ff, gather_of(second))
    take(second, gathered)
    w_in_t = full["w_in"]
    w_qkv_t, w_z_t = w_in_t[:D_QKV], w_in_t[D_QKV:]
    x1 = ffn_down(act1, full["w1_down"], x0, tm)

    qkv, z, h2 = input_projection(x1, g_mix, w_qkv_t, w_z_t, tm)
    cos_w, sin_w = _rope_tables(t)
    gq_w = jnp.tile(g_q, (1, N_HEADS))
    gk_w = jnp.tile(g_k, (1, N_KV_HEADS))
    mean_q, mean_k = _head_mean_matrix(D_ATTN).astype(BF16), _head_mean_matrix(D_KV).astype(BF16)
    q_t, k_tiles, kt_tiles, vt_tiles, vt_tiles_bwd, q_sq_max, k_sq_max = qk_prep(
        qkv, gq_w, gk_w, cos_w, sin_w, mean_q, mean_k, tq, tk, tk_fwd)
    score_bound = 1.02 * jnp.sqrt(q_sq_max.reshape(N_HEADS, HEAD_DIM)[:, 0]
                                  * jnp.repeat(k_sq_max.reshape(N_KV_HEADS, HEAD_DIM)[:, 0], KV_GROUP))
    k_tiles_fwd = k_tiles.reshape(N_KV_HEADS, t // tk_fwd, tk_fwd, HEAD_DIM)
    (o_t, q_tok), gathered = attention_fwd(score_bound, q_t, k_tiles_fwd, vt_tiles, gather_of(third))
    take(third, gathered)

    ws_b = w_s[0].astype(BF16)
    ws_tb = jnp.swapaxes(w_s[0], 1, 2).astype(BF16)
    bias = jnp.repeat(b_s[0].T, SGU_GROUP_DIM, axis=1)
    x2, mixed = mix_out(z, o_t, x1, g_sgu, g_attn_out, g_sgu_out, ws_b, bias, full["w_out"], tq)

    (h3, a2, b2, act2), _ = ffn_up(x2, g_ffn2, full["w2_gate"], full["w2_up"], tm_ff, tn_ff)

    loss_part, dx3, dg_final = ffn_down_loss(act2, full["w2_down"], x2, g_final, target, tm)

    tmm = D_FF // 2
    (da2, db2), _ = ffn_bwd_act(dx3, full["w2_down"], a2, b2, tm_ff, tn_ff)
    (dx2, dg_ffn2), _ = norm_bwd_matmul(da2, full["w2_gate"], db2, full["w2_up"], x2, g_ffn2, dx3, tm)
    dwg2, dwu2 = matmul_tn(da2, h3, 1.0, tmm, tk_w), matmul_tn(db2, h3, 1.0, tmm, tk_w)
    dwd2 = matmul_tn(act2, dx3, 0.5, tmm, tk_w)

    group_ind = (jnp.arange(D_SGU)[:, None] // SGU_GROUP_DIM == jnp.arange(LANES)[None, :]).astype(F32)
    do_t, do_tok, dz, dg_mixrow, dws, dbs = mix_bwd(dx2, z, o_t, g_sgu, g_attn_out, g_sgu_out, ws_b, ws_tb, bias,
                                                    full["w_out"], group_ind, tq)
    dw_out = matmul_tn(mixed, dx2, 1.0, D_MODEL // 2, tk_w)

    group_a = ("w2_gate", "w2_up", "w2_down", "w_out")
    (dq_t, dk_t, dv_t), (parts_a,) = attention_bwd(q_tok, do_tok, q_t, do_t, kt_tiles, vt_tiles_bwd,
                                                   scatter_exchange([dwg2, dwu2, dwd2, dw_out]))
    dqkv, dgq_w, dgk_w = qk_bwd(dq_t, dk_t, dv_t, qkv, gq_w, gk_w, cos_w, sin_w, mean_q, mean_k, tq)

    def pack_small(arrays):
        pieces = []
        for a in arrays:
            flat = a.reshape(-1)
            pieces.append(jnp.pad(flat, (0, (-flat.shape[0]) % (8 * LANES))).reshape(-1, LANES))
        return jnp.concatenate(pieces, axis=0), [p.shape[0] for p in pieces]

    early = dict(g_ffn2=dg_ffn2, g_final=dg_final, g_q=dgq_w.reshape(N_HEADS, HEAD_DIM).sum(0),
                 g_k=dgk_w.reshape(N_KV_HEADS, HEAD_DIM).sum(0), g_attn_out=dg_mixrow[0], g_sgu_out=dg_mixrow[1],
                 g_sgu=dg_mixrow[2], w_s=dws, b_s=dbs[:, :N_SGU_GROUPS].T)
    early_pack, early_rows = pack_small(list(early.values()))
    (dx1, dg_mix), (early_parts,) = norm_bwd_matmul(dqkv, w_qkv_t, dz, w_z_t, x1, g_mix, dx2, tm,
                                                    gather_exchange(early_pack, [early_pack.shape[0]]))
    dw_in = jnp.concatenate([matmul_tn(dqkv, h2, 1.0, D_QKV // 2, tk_w), matmul_tn(dz, h2, 1.0, D_SGU, tk_w)], axis=0)

    dwd1 = matmul_tn(act1, dx1, 0.5, tmm, tk_w)
    group_b = ("w_in", "w1_down")
    (da1, db1), (parts_b,) = ffn_bwd_act(dx1, full["w1_down"], a1, b1, tm_ff, tn_ff, scatter_exchange([dw_in, dwd1]))
    dwg1, dwu1 = matmul_tn(da1, h1, 1.0, tmm, tk_w), matmul_tn(db1, h1, 1.0, tmm, tk_w)
    group_c = ("w1_gate", "w1_up")
    (dx0, dg_ffn1), (parts_c,) = norm_bwd_matmul(da1, full["w1_gate"], db1, full["w1_up"], x0, g_ffn1, dx1, tm,
                                                 scatter_exchange([dwg1, dwu1]))

    late = dict(g_mix=dg_mix, g_ffn1=dg_ffn1, loss=loss_part)
    late_pack, late_rows = pack_small(list(late.values()))
    (late_parts,) = run_exchange(gather_exchange(late_pack, [late_pack.shape[0]]), "gather_late_small_grads")
    small_sums = {}
    for entries, rows, parts in ((early, early_rows, early_parts), (late, late_rows, late_parts)):
        summed = sum_parts(parts, parts.shape[1])
        off = 0
        for n, r in zip(entries, rows):
            small_sums[n] = summed[off:off + r]
            off += r
    loss = small_sums.pop("loss")[0, 0]

    grads, row_grads = {}, {}
    for group, parts in ((group_a, parts_a), (group_b, parts_b), (group_c, parts_c)):
        summed = sum_parts(parts, 32)
        off = 0
        for n in group:
            row_grads[n] = summed[off:off + rows_of[n]]
            grads[n] = (row_grads[n].T if n in TRANSPOSED else row_grads[n])[None]
            off += rows_of[n]
    for n, summed in small_sums.items():
        grads[n] = summed.reshape(-1)[:weights[n].size].reshape(weights[n].shape)

    delta_w, new_m, new_v = {}, {}, {}
    for n in names:
        shape = weights[n].shape
        if n in TRANSPOSED:
            view, unview, g = (lambda a: a[0].T), (lambda a: a.T[None]), row_grads[n]
        else:
            view, unview = (lambda a: a.reshape(-1, shape[-1])), (lambda a: a.reshape(shape))
            g = view(grads[n])
        d, m2, v2 = adamw(view(weights[n]), g, view(m_in[n]), view(v_in[n]))
        delta_w[n], new_m[n], new_v[n] = unview(d), unview(m2), unview(v2)

    return (loss, dx0[None], *[grads[n] for n in names], *[delta_w[n] for n in names],
            *[new_m[n] for n in names], *[new_v[n] for n in names])
```

```python
import functools
import math

import jax
import jax.numpy as jnp
from jax import lax
from jax.experimental import pallas as pl
from jax.experimental.pallas import tpu as pltpu

F32 = jnp.float32
BF16 = jnp.bfloat16

D_MODEL = 1024
D_FF = 2816
N_HEADS = 8
HEAD_DIM = 64
N_KV_HEADS = 2
KV_GROUP = N_HEADS // N_KV_HEADS
D_ATTN = N_HEADS * HEAD_DIM
D_KV = N_KV_HEADS * HEAD_DIM
D_QKV = D_ATTN + 2 * D_KV
N_SGU_GROUPS = 8
SGU_GROUP_DIM = 64
D_SGU = N_SGU_GROUPS * SGU_GROUP_DIM
CHUNK = 128
GRID_W = 64
ROPE_THETA = 10000.0
EPS = 1e-6
N_DEV = 8
LANES = 128

ONES_ROWS = 16
SAFE_SCORE_BOUND = 60.0
LOG2_E = math.log2(math.e)
Q_SCALE = HEAD_DIM ** -0.5 * LOG2_E

ADAM_LR = 0.001
ADAM_B1 = 0.9
ADAM_B2 = 0.999
ADAM_EPS = 1e-08
ADAM_WD = 0.01
ADAM_STEP = 10

MESH_AXES = ("x", "y", "c")
MESH_IDS = pl.DeviceIdType.MESH

VMEM_LIMIT = 56 * 1024 * 1024

SHARD_ROWS = (("w1_gate", D_FF // N_DEV), ("w1_up", D_FF // N_DEV), ("w1_down", D_FF // N_DEV),
              ("w_in", (D_QKV + 2 * D_SGU) // N_DEV), ("w_out", D_MODEL // N_DEV),
              ("w2_gate", D_FF // N_DEV), ("w2_up", D_FF // N_DEV), ("w2_down", D_FF // N_DEV))
PACK_ROWS = sum(r for _, r in SHARD_ROWS)
TRANSPOSED = ("w1_gate", "w1_up", "w_in", "w2_gate", "w2_up")


def _params(n_grid):
    return pltpu.CompilerParams(dimension_semantics=("arbitrary",) * n_grid, vmem_limit_bytes=VMEM_LIMIT)


def _dot(a, b):
    return jnp.dot(a, b, preferred_element_type=F32)


def _dot_nt(a, b):
    return lax.dot_general(a, b, (((1,), (1,)), ((), ())), preferred_element_type=F32)


def _dot_tn(a, b):
    return lax.dot_general(a, b, (((0,), (0,)), ((), ())), preferred_element_type=F32)


def _dot_f32(a, b):
    return jnp.dot(a, b, preferred_element_type=F32, precision=lax.Precision.HIGHEST)


def _dot_split(a, b):
    hi = a.astype(BF16)
    lo = (a - hi.astype(F32)).astype(BF16)
    return _dot(hi, b) + _dot(lo, b)


def _rstd(x):
    return lax.rsqrt(jnp.mean(x * x, axis=-1, keepdims=True) + EPS)


def _rms_bwd(dy, n, r, g):
    dn = dy * g
    return r * (dn - n * jnp.mean(dn * n, axis=-1, keepdims=True)), dy * n


def _colsum(a):
    return jnp.sum(a, axis=0, keepdims=True)


_GELU_C = math.sqrt(2.0 / math.pi)


def _gelu(x):
    t = jnp.tanh(_GELU_C * (x + 0.044715 * (x * x * x)))
    return x * (0.5 * (1.0 + t)), t


def _gelu_grad(x, t):
    return 0.5 * (1.0 + t) + 0.5 * x * (1.0 - t * t) * (_GELU_C * (1.0 + 3 * 0.044715 * x * x))


def _pair_swap(a):
    w = a.shape[-1]
    lane = lax.broadcasted_iota(jnp.int32, a.shape, a.ndim - 1)
    return jnp.where(lane % 2 == 0, pltpu.roll(a, w - 1, a.ndim - 1), pltpu.roll(a, 1, a.ndim - 1))


def _tile_lanes(a, reps):
    return jnp.concatenate([a] * reps, axis=-1) if reps > 1 else a


def _loop_pairs(n, step, carry):
    assert n % 2 == 0, n

    def pair(jj, c):
        return step(2 * jj + 1, 1, step(2 * jj, 0, c))

    return lax.fori_loop(0, n // 2, pair, carry)


def _full(shape):
    nd = len(shape)
    return pl.BlockSpec(shape, lambda *_: (0,) * nd)


def _mesh_pos():
    return lax.axis_index("x"), lax.axis_index("y"), lax.axis_index("c")


def _peer(pos, d):
    x, y, c = pos
    px = 1 - x if d & 4 else x
    py = 1 - y if d & 2 else y
    pc = 1 - c if d & 1 else c
    return (px, py, pc), 4 * px + 2 * py + pc


class _Exchange:
    def __init__(self, operands, out_shape, n_local, plan):
        self.operands = list(operands)
        self.out_shape = list(out_shape)
        self.sem_shapes = [pltpu.SemaphoreType.DMA((N_DEV - 1,)), pltpu.SemaphoreType.DMA((N_DEV - 1,)),
                           pltpu.SemaphoreType.DMA((n_local,))]
        self._plan = plan

    def _copies(self, in_refs, out_refs):
        pos = _mesh_pos()
        return pos, self._plan(4 * pos[0] + 2 * pos[1] + pos[2], in_refs, out_refs)

    def start(self, in_refs, out_refs, sems):
        send_sems, recv_sems, local_sems = sems
        pos, (local, remote, _) = self._copies(in_refs, out_refs)
        for k, (src, dst) in enumerate(local):
            pltpu.make_async_copy(src, dst, local_sems.at[k]).start()
        for d in range(1, N_DEV):
            peer, peer_lin = _peer(pos, d)
            for src, dst in remote(peer_lin):
                pltpu.make_async_remote_copy(src_ref=src, dst_ref=dst, send_sem=send_sems.at[d - 1],
                                             recv_sem=recv_sems.at[d - 1], device_id=peer,
                                             device_id_type=MESH_IDS).start()

    def wait(self, in_refs, out_refs, sems):
        send_sems, recv_sems, local_sems = sems
        pos, (local, _, whole) = self._copies(in_refs, out_refs)
        for d in range(1, N_DEV):
            peer, peer_lin = _peer(pos, d)
            ref = whole(peer_lin)
            everything = pltpu.make_async_remote_copy(src_ref=ref, dst_ref=ref, send_sem=send_sems.at[d - 1],
                                                      recv_sem=recv_sems.at[d - 1], device_id=peer,
                                                      device_id_type=MESH_IDS)
            everything.wait_send()
            everything.wait_recv()
        for k, (src, dst) in enumerate(local):
            pltpu.make_async_copy(src, dst, local_sems.at[k]).wait()


def _offsets(rows):
    offs, o = [], 0
    for r in rows:
        offs.append(o)
        o += r
    return offs


def gather_exchange(src, rows):
    offs = _offsets(rows)

    def plan(me, in_refs, out_refs):
        pieces = [(in_refs[0].at[pl.ds(o, r)], out.at[me]) for o, r, out in zip(offs, rows, out_refs)]
        return pieces, (lambda peer_lin: pieces), (lambda peer_lin: in_refs[0])

    return _Exchange([src], [jax.ShapeDtypeStruct((N_DEV, r) + src.shape[1:], src.dtype) for r in rows],
                     len(rows), plan)


def scatter_exchange(grads):
    rows = [g.shape[0] // N_DEV for g in grads]
    offs = _offsets(rows)

    def plan(me, in_refs, out_refs):
        parts = out_refs[0]

        def slabs(owner):
            return [(g.at[pl.ds(pl.multiple_of(owner * r, 16), r)], parts.at[me, pl.ds(o, r)])
                    for g, o, r in zip(in_refs, offs, rows)]

        return slabs(me), slabs, (lambda peer_lin: parts.at[peer_lin])

    shape = jax.ShapeDtypeStruct((N_DEV, sum(rows)) + grads[0].shape[1:], grads[0].dtype)
    return _Exchange(grads, [shape], len(rows), plan)


def gather_two_level(src, rows, name):
    offs = _offsets(rows)
    n_p = len(rows)

    def body(src_ref, *refs):
        outs, (send_sems, recv_sems, local_sems) = refs[:n_p], refs[n_p:]
        x, y, c = _mesh_pos()
        me, sibling = (x, y, c), (x, y, 1 - c)
        chips = [(1 - x, y), (x, 1 - y), (1 - x, 1 - y)]

        def slab(w, dev):
            return outs[w].at[4 * dev[0] + 2 * dev[1] + dev[2]]

        def copy(w, k, block, to, from_src=False):
            return pltpu.make_async_remote_copy(
                src_ref=src_ref.at[pl.ds(offs[w], rows[w])] if from_src else slab(w, block), dst_ref=slab(w, block),
                send_sem=send_sems.at[w * 7 + k], recv_sem=recv_sems.at[w * 7 + k],
                device_id=to, device_id_type=MESH_IDS)

        mine = [pltpu.make_async_copy(src_ref.at[pl.ds(offs[w], rows[w])], slab(w, me), local_sems.at[w])
                for w in range(n_p)]
        for cp in mine:
            cp.start()
        first = []
        for w in range(n_p):
            first.append(copy(w, 0, me, sibling, True))
            first += [copy(w, 1 + j, me, (*chip, c), True) for j, chip in enumerate(chips)]
        for cp in first:
            cp.start()
        passed = []
        for j, chip in enumerate(chips):
            for w in range(n_p):
                copy(w, 1 + j, (*chip, c), me).wait_recv()
                cp = copy(w, 4 + j, (*chip, c), sibling)
                cp.start()
                passed.append(cp)
        for w in range(n_p):
            copy(w, 0, sibling, me).wait_recv()
            for j, chip in enumerate(chips):
                copy(w, 4 + j, (*chip, 1 - c), me).wait_recv()
        for cp in first + passed:
            cp.wait_send()
        for cp in mine:
            cp.wait()

    any_spec = pl.BlockSpec(memory_space=pl.ANY)
    return pl.pallas_call(
        functools.partial(body), name=name,
        out_shape=[jax.ShapeDtypeStruct((N_DEV, r) + src.shape[1:], src.dtype) for r in rows],
        in_specs=[any_spec], out_specs=[any_spec] * n_p,
        scratch_shapes=[pltpu.SemaphoreType.DMA((7 * n_p,)), pltpu.SemaphoreType.DMA((7 * n_p,)),
                        pltpu.SemaphoreType.DMA((n_p,))],
        compiler_params=pltpu.CompilerParams(has_side_effects=True),
    )(src)


def run_exchange(ex, name):
    n_in, n_out = len(ex.operands), len(ex.out_shape)

    def body(*refs):
        parts = refs[:n_in], refs[n_in:n_in + n_out], refs[n_in + n_out:]
        ex.start(*parts)
        ex.wait(*parts)

    any_spec = pl.BlockSpec(memory_space=pl.ANY)
    return pl.pallas_call(
        functools.partial(body), name=name, out_shape=ex.out_shape,
        in_specs=[any_spec] * n_in, out_specs=[any_spec] * n_out, scratch_shapes=ex.sem_shapes,
        compiler_params=pltpu.CompilerParams(has_side_effects=True),
    )(*ex.operands)


def _pallas(comm, body, *, name, grid, in_specs, out_specs, out_shape, args, scratch_shapes=()):
    params = _params(len(grid))
    if comm is None:
        res = pl.pallas_call(functools.partial(body), name=name, grid=grid, in_specs=list(in_specs),
                             out_specs=list(out_specs), out_shape=list(out_shape),
                             scratch_shapes=list(scratch_shapes), compiler_params=params)(*args)
        return list(res), []
    n_in, n_out, n_scr = len(in_specs), len(out_specs), len(scratch_shapes)
    c_in, c_out = len(comm.operands), len(comm.out_shape)

    def edge(last):
        conds = [pl.program_id(a) == (g - 1 if last else 0) for a, g in enumerate(grid)]
        return functools.reduce(jnp.logical_and, conds)

    def wrapped(*refs):
        refs = list(refs)
        ins, refs = refs[:n_in], refs[n_in:]
        cins, refs = refs[:c_in], refs[c_in:]
        outs, refs = refs[:n_out], refs[n_out:]
        couts, refs = refs[:c_out], refs[c_out:]
        scr, sems = refs[:n_scr], refs[n_scr:]

        @pl.when(edge(False))
        def _():
            comm.start(cins, couts, sems)

        body(*ins, *outs, *scr)

        @pl.when(edge(True))
        def _():
            comm.wait(cins, couts, sems)

    any_spec = pl.BlockSpec(memory_space=pl.ANY)
    res = pl.pallas_call(
        wrapped, name=name, grid=grid,
        in_specs=list(in_specs) + [any_spec] * c_in, out_specs=list(out_specs) + [any_spec] * c_out,
        out_shape=list(out_shape) + comm.out_shape, scratch_shapes=list(scratch_shapes) + comm.sem_shapes,
        compiler_params=pltpu.CompilerParams(dimension_semantics=("arbitrary",) * len(grid),
                                             vmem_limit_bytes=VMEM_LIMIT, has_side_effects=True),
    )(*args, *comm.operands)
    return res[:n_out], res[n_out:]


def sum_parts(parts, block_rows):
    n, rows, cols = parts.shape

    def body(p_ref, o_ref):
        acc = p_ref[0].astype(F32)
        for s in range(1, n):
            acc = acc + p_ref[s].astype(F32)
        o_ref[...] = acc

    return pl.pallas_call(
        functools.partial(body), name="sum_parts",
        grid=(rows // block_rows,),
        in_specs=[pl.BlockSpec((n, block_rows, cols), lambda i: (0, i, 0))],
        out_specs=pl.BlockSpec((block_rows, cols), lambda i: (i, 0)),
        out_shape=jax.ShapeDtypeStruct((rows, cols), F32),
        compiler_params=_params(1),
    )(parts)


def adamw(w, g, m, v):
    def body(w_ref, g_ref, m_ref, v_ref, d_ref, m_out, v_out):
        gg = g_ref[...]
        m2 = ADAM_B1 * m_ref[...] + (1.0 - ADAM_B1) * gg
        v2 = ADAM_B2 * v_ref[...] + (1.0 - ADAM_B2) * (gg * gg)
        m_hat = m2 / (1.0 - ADAM_B1 ** ADAM_STEP)
        v_hat = v2 / (1.0 - ADAM_B2 ** ADAM_STEP)
        d_ref[...] = -ADAM_LR * (m_hat / (jnp.sqrt(v_hat) + ADAM_EPS) + ADAM_WD * w_ref[...])
        m_out[...] = m2
        v_out[...] = v2

    spec = _full(w.shape)
    shape = jax.ShapeDtypeStruct(w.shape, F32)
    return pl.pallas_call(
        functools.partial(body), name="adamw",
        in_specs=[spec] * 4, out_specs=[spec] * 3, out_shape=[shape] * 3,
        compiler_params=pltpu.CompilerParams(vmem_limit_bytes=VMEM_LIMIT),
    )(w, g, m, v)


def ffn_up(x, g, wg_t, wu_t, tm, tn, comm=None):
    t = x.shape[0]

    def body(x_ref, g_ref, wg_ref, wu_ref, h_ref, silu_ref, dgate_ref, act_ref):
        xx = x_ref[...]
        h = ((xx * _rstd(xx)) * g_ref[...]).astype(BF16)
        h_ref[...] = h
        for c in range(D_FF // tn):
            cols = slice(c * tn, (c + 1) * tn)
            a = _dot_nt(h, wg_ref[cols, :])
            b = _dot_nt(h, wu_ref[cols, :])
            sig = 0.5 * jnp.tanh(0.5 * a) + 0.5
            silu = a * sig
            silu_ref[:, cols] = silu.astype(BF16)
            dgate_ref[:, cols] = (b * (sig + silu * (1.0 - sig))).astype(BF16)
            act_ref[:, cols] = (silu * b).astype(BF16)

    wide = jax.ShapeDtypeStruct((t, D_FF), BF16)
    row = lambda n: pl.BlockSpec((tm, n), lambda i: (i, 0))
    return _pallas(
        comm, body, name="ffn_up",
        grid=(t // tm,),
        in_specs=[row(D_MODEL), _full((1, D_MODEL)), _full((D_FF, D_MODEL)), _full((D_FF, D_MODEL))],
        out_specs=[row(D_MODEL), row(D_FF), row(D_FF), row(D_FF)],
        out_shape=[jax.ShapeDtypeStruct((t, D_MODEL), BF16), wide, wide, wide],
        args=(x, g, wg_t, wu_t))


def ffn_down(act, wd, x, tm):
    t = x.shape[0]

    def body(act_ref, wd_ref, x_ref, o_ref):
        o_ref[...] = x_ref[...] + 0.5 * _dot(act_ref[...], wd_ref[...])

    return pl.pallas_call(
        functools.partial(body), name="ffn_down",
        grid=(t // tm,),
        in_specs=[pl.BlockSpec((tm, D_FF), lambda i: (i, 0)), _full((D_FF, D_MODEL)),
                  pl.BlockSpec((tm, D_MODEL), lambda i: (i, 0))],
        out_specs=pl.BlockSpec((tm, D_MODEL), lambda i: (i, 0)),
        out_shape=jax.ShapeDtypeStruct((t, D_MODEL), F32),
        compiler_params=_params(1),
    )(act, wd, x)


def ffn_bwd_act(dx, wd, silu, dgate, tm, tn, comm=None):
    t = dx.shape[0]

    def body(dx_ref, wd_ref, silu_ref, dgate_ref, da_ref, db_ref):
        dxb = (0.5 * dx_ref[...]).astype(BF16)
        for c in range(D_FF // tn):
            cols = slice(c * tn, (c + 1) * tn)
            dact = _dot_nt(dxb, wd_ref[cols, :])
            da_ref[:, cols] = (dact * dgate_ref[:, cols].astype(F32)).astype(BF16)
            db_ref[:, cols] = (dact * silu_ref[:, cols].astype(F32)).astype(BF16)

    wide = jax.ShapeDtypeStruct((t, D_FF), BF16)
    row = lambda n: pl.BlockSpec((tm, n), lambda i: (i, 0))
    return _pallas(
        comm, body, name="ffn_bwd_act",
        grid=(t // tm,),
        in_specs=[row(D_MODEL), _full((D_FF, D_MODEL)), row(D_FF), row(D_FF)],
        out_specs=[row(D_FF), row(D_FF)],
        out_shape=[wide, wide],
        args=(dx, wd, silu, dgate))


def norm_bwd_matmul(a1, w1, a2, w2, x, g, dx_in, tm, comm=None):
    t = x.shape[0]
    k1, k2 = a1.shape[1], a2.shape[1]

    def body(a1_ref, w1_ref, a2_ref, w2_ref, x_ref, g_ref, dxin_ref, dx_ref, dg_ref):
        dh = _dot(a1_ref[...], w1_ref[...]) + _dot(a2_ref[...], w2_ref[...])
        xx = x_ref[...]
        r = _rstd(xx)
        dx, dg_rows = _rms_bwd(dh, xx * r, r, g_ref[...])
        dx_ref[...] = dxin_ref[...] + dx

        @pl.when(pl.program_id(0) == 0)
        def _():
            dg_ref[...] = jnp.zeros_like(dg_ref)

        dg_ref[...] += _colsum(dg_rows)

    row = pl.BlockSpec((tm, D_MODEL), lambda i: (i, 0))
    return _pallas(
        comm, body, name="norm_bwd_matmul",
        grid=(t // tm,),
        in_specs=[pl.BlockSpec((tm, k1), lambda i: (i, 0)), _full((k1, D_MODEL)),
                  pl.BlockSpec((tm, k2), lambda i: (i, 0)), _full((k2, D_MODEL)),
                  row, _full((1, D_MODEL)), row],
        out_specs=[row, _full((1, D_MODEL))],
        out_shape=[jax.ShapeDtypeStruct((t, D_MODEL), F32), jax.ShapeDtypeStruct((1, D_MODEL), F32)],
        args=(a1, w1, a2, w2, x, g, dx_in))


def matmul_tn(a, b, scale, tmm, tk, comm=None):
    t, m = a.shape
    n = b.shape[1]
    nk = t // tk

    def body(a_ref, b_ref, o_ref, acc_ref):
        k = pl.program_id(1)

        @pl.when(k == 0)
        def _():
            acc_ref[...] = jnp.zeros_like(acc_ref)

        acc_ref[...] += _dot_tn(a_ref[...].astype(BF16), b_ref[...].astype(BF16))

        @pl.when(k == nk - 1)
        def _():
            o_ref[...] = (scale * acc_ref[...]).astype(BF16)

    (out,), comm_outs = _pallas(
        comm, body, name="matmul_tn",
        grid=(m // tmm, nk),
        in_specs=[pl.BlockSpec((tk, tmm), lambda i, k: (k, i)), pl.BlockSpec((tk, n), lambda i, k: (k, 0))],
        out_specs=[pl.BlockSpec((tmm, n), lambda i, k: (i, 0))],
        out_shape=[jax.ShapeDtypeStruct((m, n), BF16)],
        scratch_shapes=[pltpu.VMEM((tmm, n), F32)],
        args=(a, b))
    return out if comm is None else (out, comm_outs)


def input_projection(x, g, w_qkv_t, w_z_t, tm):
    t = x.shape[0]

    def body(x_ref, g_ref, wq_ref, wz_ref, qkv_ref, z_ref, h_ref):
        xx = x_ref[...]
        h = ((xx * _rstd(xx)) * g_ref[...]).astype(BF16)
        h_ref[...] = h
        qkv_ref[...] = _dot_nt(h, wq_ref[...])
        z_ref[...] = _dot_nt(h, wz_ref[...])

    row = lambda n: pl.BlockSpec((tm, n), lambda i: (i, 0))
    return pl.pallas_call(
        functools.partial(body), name="input_projection", grid=(t // tm,),
        in_specs=[row(D_MODEL), _full((1, D_MODEL)), _full((D_QKV, D_MODEL)), _full((2 * D_SGU, D_MODEL))],
        out_specs=[row(D_QKV), row(2 * D_SGU), row(D_MODEL)],
        out_shape=[jax.ShapeDtypeStruct((t, D_QKV), F32), jax.ShapeDtypeStruct((t, 2 * D_SGU), F32),
                   jax.ShapeDtypeStruct((t, D_MODEL), BF16)],
        compiler_params=_params(1))(x, g, w_qkv_t, w_z_t)


def _shift_rows(shape, first, second):
    row = lax.broadcasted_iota(jnp.int32, shape, len(shape) - 2)
    return jnp.where(row == 0, first, jnp.where(row == 1, second, 0.0))


def _hi_lo(a):
    hi = a.astype(BF16).astype(F32)
    return hi, a - hi


def _head_tile_spec(tm, rows):
    return pl.BlockSpec((N_HEADS, None, rows, tm), lambda i: (0, i, 0, 0))


def _to_head_tiles(a):
    return a.T.reshape(N_HEADS, HEAD_DIM, a.shape[0])


def _from_head_tiles(a):
    return a.reshape(D_ATTN, a.shape[-1]).T


def _head_mean_matrix(width):
    head = jnp.arange(width) // HEAD_DIM
    return (head[:, None] == head[None, :]).astype(F32) / HEAD_DIM


def _kv_tile_spec(n_sub, rows, cols):
    return pl.BlockSpec((N_KV_HEADS, n_sub, rows, cols), lambda i: (0, i, 0, 0))


def qk_prep(qkv, gq_w, gk_w, cos_w, sin_w, mean_q, mean_k, tm, tk, tk_v):
    t = qkv.shape[0]
    n_sub, n_sub_v = tm // tk, tm // tk_v

    def body(p_ref, gq_ref, gk_ref, cos_ref, sin_ref, mq_ref, mk_ref, q_ref, k_ref, kt_ref, vt_ref, vtb_ref,
             qmax_ref, kmax_ref):
        @pl.when(pl.program_id(0) == 0)
        def _():
            qmax_ref[...] = jnp.zeros_like(qmax_ref)
            kmax_ref[...] = jnp.zeros_like(kmax_ref)

        cos2, sin2 = cos_ref[...], sin_ref[...]
        q = p_ref[:, :D_ATTN]
        k = p_ref[:, D_ATTN:D_ATTN + D_KV]
        qn = q * lax.rsqrt(_dot_split(q * q, mq_ref[...]) + EPS) * gq_ref[...]
        kn = k * lax.rsqrt(_dot_split(k * k, mk_ref[...]) + EPS) * gk_ref[...]
        cos8, sin8 = _tile_lanes(cos2, D_ATTN // LANES), _tile_lanes(sin2, D_ATTN // LANES)
        q_rot = (qn * cos8 + _pair_swap(qn) * sin8) * Q_SCALE
        q_ref[...] = _to_head_tiles(q_rot).astype(BF16)
        k_rot = kn * cos2 + _pair_swap(kn) * sin2
        q_sq = HEAD_DIM * _dot_split(q_rot * q_rot, mq_ref[...])
        k_sq = HEAD_DIM * _dot_split(k_rot * k_rot, mk_ref[...])
        qmax_ref[...] = jnp.maximum(qmax_ref[...], jnp.max(q_sq, axis=0, keepdims=True))
        kmax_ref[...] = jnp.maximum(kmax_ref[...], jnp.max(k_sq, axis=0, keepdims=True))
        vv = p_ref[:, D_ATTN + D_KV:]
        second = pltpu.roll(k_rot, HEAD_DIM, 1)
        for c in range(n_sub):
            rows = slice(c * tk, (c + 1) * tk)
            k_ref[0, c] = k_rot[rows, :HEAD_DIM].astype(BF16)
            k_ref[1, c] = second[rows, :HEAD_DIM].astype(BF16)
        for a, feat_ref, width, n in ((k_rot, kt_ref, tk, n_sub), (vv, vtb_ref, tk, n_sub), (vv, vt_ref, tk_v, n_sub_v)):
            for c in range(n):
                tile = a[c * width:(c + 1) * width].T.reshape(N_KV_HEADS, HEAD_DIM, width)
                feat_ref[:, c, :HEAD_DIM, :] = tile.astype(BF16)
        vt_ref[:, :, HEAD_DIM:, :] = jnp.ones((N_KV_HEADS, n_sub_v, ONES_ROWS, tk_v), BF16)
        minus = _shift_rows((N_KV_HEADS, n_sub, HEAD_DIM, tk), -1.0, -1.0).astype(BF16)
        kt_ref[:, :, HEAD_DIM:, :] = minus
        vtb_ref[:, :, HEAD_DIM:, :] = minus

    kv = lambda rows, cols: jax.ShapeDtypeStruct((N_KV_HEADS, t // tk, rows, cols), BF16)
    return pl.pallas_call(
        functools.partial(body), name="qk_prep", grid=(t // tm,),
        in_specs=[pl.BlockSpec((tm, D_QKV), lambda i: (i, 0)), _full((1, D_ATTN)), _full((1, D_KV)),
                  pl.BlockSpec((tm, LANES), lambda i: (i, 0)), pl.BlockSpec((tm, LANES), lambda i: (i, 0)),
                  _full((D_ATTN, D_ATTN)), _full((D_KV, D_KV))],
        out_specs=[_head_tile_spec(tm, HEAD_DIM), _kv_tile_spec(n_sub, tk, HEAD_DIM),
                   _kv_tile_spec(n_sub, 2 * HEAD_DIM, tk),
                   _kv_tile_spec(n_sub_v, HEAD_DIM + ONES_ROWS, tk_v), _kv_tile_spec(n_sub, 2 * HEAD_DIM, tk),
                   _full((1, D_ATTN)), _full((1, D_KV))],
        out_shape=[jax.ShapeDtypeStruct((N_HEADS, t // tm, HEAD_DIM, tm), BF16), kv(tk, HEAD_DIM), kv(2 * HEAD_DIM, tk),
                   jax.ShapeDtypeStruct((N_KV_HEADS, t // tk_v, HEAD_DIM + ONES_ROWS, tk_v), BF16),
                   kv(2 * HEAD_DIM, tk),
                   jax.ShapeDtypeStruct((1, D_ATTN), F32), jax.ShapeDtypeStruct((1, D_KV), F32)],
        compiler_params=_params(1),
    )(qkv, gq_w, gk_w, cos_w, sin_w, mean_q, mean_k)


def qk_bwd(dq_rot, dk_rot, dv, qkv, gq_w, gk_w, cos_w, sin_w, mean_q, mean_k, tm):
    t = qkv.shape[0]
    tk = dk_rot.shape[-1]
    n_sub = tm // tk

    def token_major(ref):
        return jnp.concatenate([ref[:, c].reshape(D_KV, tk).T for c in range(n_sub)], axis=0)

    def branch(raw, d_rot, gain, mean_mat, cos, sin, scale):
        r = lax.rsqrt(_dot_split(raw * raw, mean_mat) + EPS)
        n = raw * r
        dy = (d_rot * cos - _pair_swap(d_rot) * sin) * scale
        dn = dy * gain
        return r * (dn - n * _dot_split(dn * n, mean_mat)), dy * n

    def body(dq_ref, dk_ref, dv_ref, p_ref, gq_ref, gk_ref, cos_ref, sin_ref, mq_ref, mk_ref,
             dp_ref, dgq_ref, dgk_ref):
        cos2, sin2 = cos_ref[...], sin_ref[...]
        cos8, sin8 = _tile_lanes(cos2, D_ATTN // LANES), _tile_lanes(sin2, D_ATTN // LANES)
        dq, dgq = branch(p_ref[:, :D_ATTN], _from_head_tiles(dq_ref[...]), gq_ref[...], mq_ref[...], cos8, sin8,
                         HEAD_DIM ** -0.5)
        dk, dgk = branch(p_ref[:, D_ATTN:D_ATTN + D_KV], token_major(dk_ref), gk_ref[...], mk_ref[...], cos2, sin2, 1.0)
        dp_ref[...] = jnp.concatenate([dq, dk, token_major(dv_ref)], axis=-1).astype(BF16)

        @pl.when(pl.program_id(0) == 0)
        def _():
            dgq_ref[...] = jnp.zeros_like(dgq_ref)
            dgk_ref[...] = jnp.zeros_like(dgk_ref)

        dgq_ref[...] += _colsum(dgq)
        dgk_ref[...] += _colsum(dgk)

    return pl.pallas_call(
        functools.partial(body), name="qk_bwd", grid=(t // tm,),
        in_specs=[_head_tile_spec(tm, HEAD_DIM), _kv_tile_spec(n_sub, HEAD_DIM, tk),
                  _kv_tile_spec(n_sub, HEAD_DIM, tk), pl.BlockSpec((tm, D_QKV), lambda i: (i, 0)),
                  _full((1, D_ATTN)), _full((1, D_KV)),
                  pl.BlockSpec((tm, LANES), lambda i: (i, 0)), pl.BlockSpec((tm, LANES), lambda i: (i, 0)),
                  _full((D_ATTN, D_ATTN)), _full((D_KV, D_KV))],
        out_specs=[pl.BlockSpec((tm, D_QKV), lambda i: (i, 0)), _full((1, D_ATTN)), _full((1, D_KV))],
        out_shape=[jax.ShapeDtypeStruct((t, D_QKV), BF16), jax.ShapeDtypeStruct((1, D_ATTN), F32),
                   jax.ShapeDtypeStruct((1, D_KV), F32)],
        compiler_params=_params(1),
    )(dq_rot, dk_rot, dv, qkv, gq_w, gk_w, cos_w, sin_w, mean_q, mean_k)


def attention_fwd(bound, q_t, k, v_t, comm=None):
    _, nq, _, tq = q_t.shape
    _, nk, tk, _ = k.shape

    def body(bound_ref, q_ref, k_ref, v_ref, o_ref, qtok_ref, s_scr, p_scr):
        head_bound = bound_ref[pl.program_id(0)]
        safe = head_bound <= SAFE_SCORE_BOUND
        q = q_ref[...]
        s_scr[0] = _dot(k_ref[0], q)
        p_scr[1] = jnp.zeros((tk, tq), BF16)
        zero = jnp.zeros((HEAD_DIM + ONES_ROWS, tq), F32)

        def matmuls(j, slot):
            pv = _dot(v_ref[jnp.maximum(j - 1, 0)], p_scr[1 - slot])
            s_scr[1 - slot] = _dot(k_ref[jnp.minimum(j + 1, nk - 1)], q)
            return pv

        def finish(m, acc):
            acc = acc + _dot(v_ref[nk - 1], p_scr[(nk - 1) % 2])
            l = acc[HEAD_DIM:HEAD_DIM + 1]
            o_ref[...] = acc[:HEAD_DIM] / l
            lse_rows = _shift_rows((HEAD_DIM, tq), *_hi_lo(m + jnp.log2(l)))
            qtok_ref[...] = jnp.concatenate([q.astype(F32), lse_rows], axis=0).T.astype(BF16)

        @pl.when(safe)
        def _():
            m = jnp.full((1, tq), head_bound, F32)

            def step(j, slot, acc):
                s = s_scr[slot]
                pv = matmuls(j, slot)
                p_scr[slot] = jnp.exp2(s - m).astype(BF16)
                return acc + pv

            finish(m, _loop_pairs(nk, step, zero))

        @pl.when(jnp.logical_not(safe))
        def _():
            def step(j, slot, carry):
                m, acc = carry
                s = s_scr[slot]
                pv = matmuls(j, slot)
                m_new = jnp.maximum(m, jnp.max(s, axis=0, keepdims=True))
                p_scr[slot] = jnp.exp2(s - m_new).astype(BF16)
                return m_new, jnp.exp2(m - m_new) * (acc + pv)

            finish(*_loop_pairs(nk, step, (jnp.full((1, tq), -1e30, F32), zero)))

    return _pallas(
        comm, body, name="attention_fwd", grid=(N_HEADS, nq),
        in_specs=[pl.BlockSpec(memory_space=pltpu.SMEM),
                  pl.BlockSpec((None, None, HEAD_DIM, tq), lambda h, i: (h, i, 0, 0)),
                  pl.BlockSpec((None, nk, tk, HEAD_DIM), lambda h, i: (h // KV_GROUP, 0, 0, 0)),
                  pl.BlockSpec((None, nk, HEAD_DIM + ONES_ROWS, tk), lambda h, i: (h // KV_GROUP, 0, 0, 0))],
        out_specs=[pl.BlockSpec((None, None, HEAD_DIM, tq), lambda h, i: (h, i, 0, 0)),
                   pl.BlockSpec((None, None, tq, 2 * HEAD_DIM), lambda h, i: (h, i, 0, 0))],
        out_shape=[jax.ShapeDtypeStruct((N_HEADS, nq, HEAD_DIM, tq), F32),
                   jax.ShapeDtypeStruct((N_HEADS, nq, tq, 2 * HEAD_DIM), BF16)],
        scratch_shapes=[pltpu.VMEM((2, tk, tq), F32), pltpu.VMEM((2, tk, tq), BF16)],
        args=(bound, q_t, k, v_t))


def attention_bwd(q_tok, do_tok, q_t, do_t, k_t, v_t, comm=None):
    _, nq, _, tq = q_t.shape
    _, nk, _, tk = k_t.shape

    def body(qtok_ref, dotok_ref, q_ref, do_ref, kt_ref, vt_ref, dq_ref, dk_ref, dv_ref,
             s_scr, dp_scr, p_scr, ds_scr):
        @pl.when(pl.program_id(1) == 0)
        def _():
            dq_ref[...] = jnp.zeros_like(dq_ref)

        kt_aug, vt_aug = kt_ref[...], vt_ref[...]
        kt = kt_aug[:HEAD_DIM]
        n = KV_GROUP * nq
        s_scr[0] = _dot(qtok_ref[0, 0], kt_aug)
        dp_scr[0] = _dot(dotok_ref[0, 0], vt_aug)
        p_scr[1] = jnp.zeros((tq, tk), BF16)
        ds_scr[1] = jnp.zeros((tq, tk), BF16)

        def products(t, slot, dk, dv):
            h, i = t // nq, t % nq
            ds = ds_scr[slot]
            dq_ref[h, i] += _dot_nt(kt, ds)
            return dk + _dot(q_ref[h, i], ds), dv + _dot(do_ref[h, i], p_scr[slot])

        def step(t, slot, carry):
            s, dp = s_scr[slot], dp_scr[slot]
            dk, dv = products(jnp.maximum(t - 1, 0), 1 - slot, *carry)
            nxt = jnp.minimum(t + 1, n - 1)
            s_scr[1 - slot] = _dot(qtok_ref[nxt // nq, nxt % nq], kt_aug)
            dp_scr[1 - slot] = _dot(dotok_ref[nxt // nq, nxt % nq], vt_aug)
            p = jnp.exp2(s)
            p_scr[slot] = p.astype(BF16)
            ds_scr[slot] = (p * dp).astype(BF16)
            return dk, dv

        zero = jnp.zeros((HEAD_DIM, tk), F32)
        dk, dv = products(n - 1, (n - 1) % 2, *_loop_pairs(n, step, (zero, zero)))
        dk_ref[...] = dk * (1.0 / LOG2_E)
        dv_ref[...] = dv

    group = lambda g, j: (g, 0, 0, 0)
    tile = lambda g, j: (g, j, 0, 0)
    once = pl.Buffered(1)
    return _pallas(
        comm, body, name="attention_bwd", grid=(N_KV_HEADS, nk),
        in_specs=[pl.BlockSpec((KV_GROUP, nq, tq, 2 * HEAD_DIM), group, pipeline_mode=once),
                  pl.BlockSpec((KV_GROUP, nq, tq, 2 * HEAD_DIM), group, pipeline_mode=once),
                  pl.BlockSpec((KV_GROUP, nq, HEAD_DIM, tq), group, pipeline_mode=once),
                  pl.BlockSpec((KV_GROUP, nq, HEAD_DIM, tq), group, pipeline_mode=once),
                  pl.BlockSpec((None, None, 2 * HEAD_DIM, tk), tile),
                  pl.BlockSpec((None, None, 2 * HEAD_DIM, tk), tile)],
        out_specs=[pl.BlockSpec((KV_GROUP, nq, HEAD_DIM, tq), group),
                   pl.BlockSpec((None, None, HEAD_DIM, tk), tile),
                   pl.BlockSpec((None, None, HEAD_DIM, tk), tile)],
        out_shape=[jax.ShapeDtypeStruct((N_HEADS, nq, HEAD_DIM, tq), F32),
                   jax.ShapeDtypeStruct((N_KV_HEADS, nk, HEAD_DIM, tk), F32),
                   jax.ShapeDtypeStruct((N_KV_HEADS, nk, HEAD_DIM, tk), F32)],
        scratch_shapes=[pltpu.VMEM((2, tq, tk), F32), pltpu.VMEM((2, tq, tk), F32),
                        pltpu.VMEM((2, tq, tk), BF16), pltpu.VMEM((2, tq, tk), BF16)],
        args=(q_tok, do_tok, q_t, do_t, k_t, v_t))


def _group_select(parts):
    lane_group = lax.broadcasted_iota(jnp.int32, parts[0].shape, 1) // SGU_GROUP_DIM
    out = parts[0]
    for g in range(1, N_SGU_GROUPS):
        out = jnp.where(lane_group == g, parts[g], out)
    return out


def _gate_forward(z, g_sgu, ws_ref, bias):
    gz, th = _gelu(z)
    u, vv = gz[:, :D_SGU], gz[:, D_SGU:]
    rv = _rstd(vv)
    nv = vv * rv
    vn = (nv * g_sgu).astype(BF16)
    fs = []
    for c in range(z.shape[0] // CHUNK):
        vc = vn[c * CHUNK:(c + 1) * CHUNK]
        fs.append(_group_select([_dot(ws_ref[g], vc) for g in range(N_SGU_GROUPS)]) + bias)
    f = jnp.concatenate(fs, axis=0) if len(fs) > 1 else fs[0]
    return th, u, rv, nv, vn, f


def mix_out(z, o, x, g_sgu, g_ao, g_so, ws, bias, w_out, tm):
    t = x.shape[0]

    def body(z_ref, o_ref, x_ref, gs_ref, gao_ref, gso_ref, ws_ref, bias_ref, wout_ref, x2_ref, mixed_ref):
        _, u, _, _, _, f = _gate_forward(z_ref[...], gs_ref[...], ws_ref, bias_ref[...])
        sgu = u * f
        oo = _from_head_tiles(o_ref[...])
        mixed = jnp.concatenate([oo * _rstd(oo) * gao_ref[...], sgu * _rstd(sgu) * gso_ref[...]], axis=-1).astype(BF16)
        mixed_ref[...] = mixed
        x2_ref[...] = x_ref[...] + _dot(mixed, wout_ref[...])

    row = lambda n: pl.BlockSpec((tm, n), lambda i: (i, 0))
    return pl.pallas_call(
        functools.partial(body), name="mix_out", grid=(t // tm,),
        in_specs=[row(2 * D_SGU), _head_tile_spec(tm, HEAD_DIM), row(D_MODEL), _full((1, D_SGU)), _full((1, D_ATTN)),
                  _full((1, D_SGU)),
                  _full((N_SGU_GROUPS, CHUNK, CHUNK)), _full((CHUNK, D_SGU)), _full((D_MODEL, D_MODEL))],
        out_specs=[row(D_MODEL), row(D_MODEL)],
        out_shape=[jax.ShapeDtypeStruct((t, D_MODEL), F32), jax.ShapeDtypeStruct((t, D_MODEL), BF16)],
        compiler_params=_params(1),
    )(z, o, x, g_sgu, g_ao, g_so, ws, bias, w_out)


def mix_bwd(dx2, z, o, g_sgu, g_ao, g_so, ws, ws_t, bias, w_out, group_ind, tm):
    t = dx2.shape[0]
    n_tiles = t // tm

    def body(dx_ref, z_ref, o_ref, gs_ref, gao_ref, gso_ref, ws_ref, wst_ref, bias_ref, wout_ref, ind_ref,
             do_ref, dotok_ref, dz_ref, dg_ref, dws_ref, dbs_ref, df_sum):
        step = pl.program_id(0)

        @pl.when(step == 0)
        def _():
            dg_ref[...] = jnp.zeros_like(dg_ref)
            dws_ref[...] = jnp.zeros_like(dws_ref)
            df_sum[...] = jnp.zeros_like(df_sum)

        z = z_ref[...]
        th, u, rv, nv, vn, f = _gate_forward(z, gs_ref[...], ws_ref, bias_ref[...])
        dmixed = _dot_nt(dx_ref[...].astype(BF16), wout_ref[...])
        o_tiles = o_ref[...]
        oo = _from_head_tiles(o_tiles)
        ro = _rstd(oo)
        d_o, dgao = _rms_bwd(dmixed[:, :D_ATTN], oo * ro, ro, gao_ref[...])
        do_tiles = _to_head_tiles(d_o)
        do_ref[...] = do_tiles.astype(BF16)
        delta_hi, delta_lo = _hi_lo(jnp.sum(do_tiles * o_tiles, axis=1, keepdims=True))
        for h in range(N_HEADS):
            delta_rows = _shift_rows((HEAD_DIM, tm), delta_hi[h], delta_lo[h])
            dotok_ref[h] = jnp.concatenate([do_tiles[h], delta_rows], axis=0).T.astype(BF16)
        sgu = u * f
        rs = _rstd(sgu)
        dsgu, dgso = _rms_bwd(dmixed[:, D_ATTN:], sgu * rs, rs, gso_ref[...])
        du = dsgu * f
        df = dsgu * u
        lane_group = lax.broadcasted_iota(jnp.int32, (CHUNK, D_SGU), 1) // SGU_GROUP_DIM
        dvns = []
        df_acc = jnp.zeros((CHUNK, D_SGU), F32)
        for c in range(tm // CHUNK):
            dfc32 = df[c * CHUNK:(c + 1) * CHUNK]
            dfc = dfc32.astype(BF16)
            vc = vn[c * CHUNK:(c + 1) * CHUNK]
            dvns.append(_group_select([_dot(wst_ref[g], dfc) for g in range(N_SGU_GROUPS)]))
            for g in range(N_SGU_GROUPS):
                dws_ref[g] += _dot_nt(jnp.where(lane_group == g, dfc, jnp.zeros_like(dfc)), vc)
            df_acc = df_acc + dfc32
        df_sum[...] += df_acc
        dvn = jnp.concatenate(dvns, axis=0) if len(dvns) > 1 else dvns[0]
        dvv, dgs = _rms_bwd(dvn, nv, rv, gs_ref[...])
        dz_ref[...] = (jnp.concatenate([du, dvv], axis=-1) * _gelu_grad(z, th)).astype(BF16)
        dg_ref[0:1, :] += _colsum(dgao)
        dg_ref[1:2, :] += _colsum(dgso)
        dg_ref[2:3, :] += _colsum(dgs)

        @pl.when(step == n_tiles - 1)
        def _():
            dbs_ref[...] = _dot_f32(df_sum[...], ind_ref[...])

    row = lambda n: pl.BlockSpec((tm, n), lambda i: (i, 0))
    return pl.pallas_call(
        functools.partial(body), name="mix_bwd", grid=(n_tiles,),
        in_specs=[row(D_MODEL), row(2 * D_SGU), _head_tile_spec(tm, HEAD_DIM), _full((1, D_SGU)), _full((1, D_ATTN)),
                  _full((1, D_SGU)),
                  _full((N_SGU_GROUPS, CHUNK, CHUNK)), _full((N_SGU_GROUPS, CHUNK, CHUNK)), _full((CHUNK, D_SGU)),
                  _full((D_MODEL, D_MODEL)), _full((D_SGU, LANES))],
        out_specs=[_head_tile_spec(tm, HEAD_DIM), pl.BlockSpec((N_HEADS, None, tm, 2 * HEAD_DIM), lambda i: (0, i, 0, 0)),
                   row(2 * D_SGU), _full((8, D_SGU)),
                   _full((N_SGU_GROUPS, CHUNK, CHUNK)), _full((CHUNK, LANES))],
        out_shape=[jax.ShapeDtypeStruct((N_HEADS, n_tiles, HEAD_DIM, tm), BF16),
                   jax.ShapeDtypeStruct((N_HEADS, n_tiles, tm, 2 * HEAD_DIM), BF16),
                   jax.ShapeDtypeStruct((t, 2 * D_SGU), BF16),
                   jax.ShapeDtypeStruct((8, D_SGU), F32),
                   jax.ShapeDtypeStruct((N_SGU_GROUPS, CHUNK, CHUNK), F32),
                   jax.ShapeDtypeStruct((CHUNK, LANES), F32)],
        scratch_shapes=[pltpu.VMEM((CHUNK, D_SGU), F32)],
        compiler_params=_params(1),
    )(dx2, z, o, g_sgu, g_ao, g_so, ws, ws_t, bias, w_out, group_ind)


def ffn_down_loss(act, wd, x, g, target, tm):
    t = x.shape[0]

    def body(act_ref, wd_ref, x_ref, g_ref, t_ref, loss_ref, dx_ref, dg_ref):
        @pl.when(pl.program_id(0) == 0)
        def _():
            loss_ref[...] = jnp.zeros_like(loss_ref)
            dg_ref[...] = jnp.zeros_like(dg_ref)

        xx = x_ref[...] + 0.5 * _dot(act_ref[...], wd_ref[...])
        r = _rstd(xx)
        n = xx * r
        err = n * g_ref[...] - t_ref[...]
        per_token = jnp.mean(err * err, axis=-1, keepdims=True)
        loss_ref[...] += 0.5 * jnp.sum(per_token, axis=0, keepdims=True)
        dx, dg_rows = _rms_bwd(err * (1.0 / D_MODEL), n, r, g_ref[...])
        dx_ref[...] = dx
        dg_ref[...] += _colsum(dg_rows)

    row = pl.BlockSpec((tm, D_MODEL), lambda i: (i, 0))
    return pl.pallas_call(
        functools.partial(body), name="ffn_down_loss", grid=(t // tm,),
        in_specs=[pl.BlockSpec((tm, D_FF), lambda i: (i, 0)), _full((D_FF, D_MODEL)), row, _full((1, D_MODEL)), row],
        out_specs=[_full((1, LANES)), row, _full((1, D_MODEL))],
        out_shape=[jax.ShapeDtypeStruct((1, LANES), F32), jax.ShapeDtypeStruct((t, D_MODEL), F32),
                   jax.ShapeDtypeStruct((1, D_MODEL), F32)],
        compiler_params=_params(1),
    )(act, wd, x, g, target)


def _rope_tables(t):
    rows = t // GRID_W
    row_idx = jnp.repeat(jnp.arange(rows, dtype=F32), GRID_W)
    col_idx = jnp.tile(jnp.arange(GRID_W, dtype=F32), rows)
    axis_dim = HEAD_DIM // 2
    inv = 1.0 / (ROPE_THETA ** (jnp.arange(0, axis_dim, 2, dtype=F32) / axis_dim))
    ang = jnp.concatenate([row_idx[:, None] * inv, col_idx[:, None] * inv], axis=-1)
    cos = jnp.repeat(jnp.cos(ang), 2, axis=-1)
    sin = jnp.repeat(jnp.sin(ang), 2, axis=-1) * jnp.tile(jnp.array([-1.0, 1.0], F32), HEAD_DIM // 2)
    return jnp.tile(cos, (1, LANES // HEAD_DIM)), jnp.tile(sin, (1, LANES // HEAD_DIM))


def _heads_to_tiles_t(a, n_heads, tile):
    t = a.shape[0]
    return a.reshape(t // tile, tile, n_heads, HEAD_DIM).transpose(2, 0, 3, 1)


def _heads_to_tiles(a, n_heads, tile):
    t = a.shape[0]
    return a.reshape(t // tile, tile, n_heads, HEAD_DIM).transpose(2, 0, 1, 3)


def _tiles_t_to_heads(a):
    h, n, _, tile = a.shape
    return a.transpose(1, 3, 0, 2).reshape(n * tile, h * HEAD_DIM)


def kernel(x, g_ffn1, w1_gate, w1_up, w1_down, g_mix, w_in, g_q, g_k, g_sgu, w_s, b_s, g_attn_out, g_sgu_out, w_out, g_ffn2, w2_gate, w2_up, w2_down, g_final, loss_target, m_g_ffn1, m_w1_gate, m_w1_up, m_w1_down, m_g_mix, m_w_in, m_g_q, m_g_k, m_g_sgu, m_w_s, m_b_s, m_g_attn_out, m_g_sgu_out, m_w_out, m_g_ffn2, m_w2_gate, m_w2_up, m_w2_down, m_g_final, v_g_ffn1, v_w1_gate, v_w1_up, v_w1_down, v_g_mix, v_w_in, v_g_q, v_g_k, v_g_sgu, v_w_s, v_b_s, v_g_attn_out, v_g_sgu_out, v_w_out, v_g_ffn2, v_w2_gate, v_w2_up, v_w2_down, v_g_final):
    weights = dict(g_ffn1=g_ffn1, w1_gate=w1_gate, w1_up=w1_up, w1_down=w1_down, g_mix=g_mix, w_in=w_in, g_q=g_q,
                   g_k=g_k, g_sgu=g_sgu, w_s=w_s, b_s=b_s, g_attn_out=g_attn_out, g_sgu_out=g_sgu_out, w_out=w_out,
                   g_ffn2=g_ffn2, w2_gate=w2_gate, w2_up=w2_up, w2_down=w2_down, g_final=g_final)
    m_in = dict(g_ffn1=m_g_ffn1, w1_gate=m_w1_gate, w1_up=m_w1_up, w1_down=m_w1_down, g_mix=m_g_mix, w_in=m_w_in,
                g_q=m_g_q, g_k=m_g_k, g_sgu=m_g_sgu, w_s=m_w_s, b_s=m_b_s, g_attn_out=m_g_attn_out,
                g_sgu_out=m_g_sgu_out, w_out=m_w_out, g_ffn2=m_g_ffn2, w2_gate=m_w2_gate, w2_up=m_w2_up,
                w2_down=m_w2_down, g_final=m_g_final)
    v_in = dict(g_ffn1=v_g_ffn1, w1_gate=v_w1_gate, w1_up=v_w1_up, w1_down=v_w1_down, g_mix=v_g_mix, w_in=v_w_in,
                g_q=v_g_q, g_k=v_g_k, g_sgu=v_g_sgu, w_s=v_w_s, b_s=v_b_s, g_attn_out=v_g_attn_out,
                g_sgu_out=v_g_sgu_out, w_out=v_w_out, g_ffn2=v_g_ffn2, w2_gate=v_w2_gate, w2_up=v_w2_up,
                w2_down=v_w2_down, g_final=v_g_final)
    names = list(weights)

    t = x.shape[1]
    x0 = x[0]
    target = loss_target[0]
    tm = min(256, t)
    tm_ff = min(256, t)
    tn_ff = 256
    tq = min(512, t)
    tk = min(256, t)
    tk_fwd = min(512, t)
    tk_w = min(2048, t)

    def shard_rows(name):
        w = weights[name][0]
        return (w.T if name in TRANSPOSED else w).astype(BF16)

    rows_of = dict(SHARD_ROWS)
    full = {}

    def packed(group):
        return jnp.concatenate([shard_rows(n) for n in group], axis=0), [rows_of[n] for n in group]

    def gather_of(group):
        return gather_exchange(*packed(group))

    def take(group, gathered):
        for n, g in zip(group, gathered):
            full[n] = g.reshape(N_DEV * rows_of[n], D_MODEL)

    first, second, third = ("w1_gate", "w1_up"), ("w1_down", "w_in", "w_out"), ("w2_gate", "w2_up", "w2_down")
    take(first, gather_two_level(*packed(first), "gather_first"))

    (h1, a1, b1, act1), gathered = ffn_up(x0, g_ffn1, full["w1_gate"], full["w1_up"], tm_ff, tn_ff, gather_of(second))
    take(second, gathered)
    w_in_t = full["w_in"]
    w_qkv_t, w_z_t = w_in_t[:D_QKV], w_in_t[D_QKV:]
    x1 = ffn_down(act1, full["w1_down"], x0, tm)

    qkv, z, h2 = input_projection(x1, g_mix, w_qkv_t, w_z_t, tm)
    cos_w, sin_w = _rope_tables(t)
    gq_w = jnp.tile(g_q, (1, N_HEADS))
    gk_w = jnp.tile(g_k, (1, N_KV_HEADS))
    mean_q, mean_k = _head_mean_matrix(D_ATTN).astype(BF16), _head_mean_matrix(D_KV).astype(BF16)
    q_t, k_tiles, kt_tiles, vt_tiles, vt_tiles_bwd, q_sq_max, k_sq_max = qk_prep(
        qkv, gq_w, gk_w, cos_w, sin_w, mean_q, mean_k, tq, tk, tk_fwd)
    score_bound = 1.02 * jnp.sqrt(q_sq_max.reshape(N_HEADS, HEAD_DIM)[:, 0]
                                  * jnp.repeat(k_sq_max.reshape(N_KV_HEADS, HEAD_DIM)[:, 0], KV_GROUP))
    k_tiles_fwd = k_tiles.reshape(N_KV_HEADS, t // tk_fwd, tk_fwd, HEAD_DIM)
    (o_t, q_tok), gathered = attention_fwd(score_bound, q_t, k_tiles_fwd, vt_tiles, gather_of(third))
    take(third, gathered)

    ws_b = w_s[0].astype(BF16)
    ws_tb = jnp.swapaxes(w_s[0], 1, 2).astype(BF16)
    bias = jnp.repeat(b_s[0].T, SGU_GROUP_DIM, axis=1)
    x2, mixed = mix_out(z, o_t, x1, g_sgu, g_attn_out, g_sgu_out, ws_b, bias, full["w_out"], tq)

    (h3, a2, b2, act2), _ = ffn_up(x2, g_ffn2, full["w2_gate"], full["w2_up"], tm_ff, tn_ff)

    loss_part, dx3, dg_final = ffn_down_loss(act2, full["w2_down"], x2, g_final, target, tm)

    tmm = D_FF // 2
    (da2, db2), _ = ffn_bwd_act(dx3, full["w2_down"], a2, b2, tm_ff, tn_ff)
    (dx2, dg_ffn2), _ = norm_bwd_matmul(da2, full["w2_gate"], db2, full["w2_up"], x2, g_ffn2, dx3, tm)
    dwg2, dwu2 = matmul_tn(da2, h3, 1.0, tmm, tk_w), matmul_tn(db2, h3, 1.0, tmm, tk_w)
    dwd2 = matmul_tn(act2, dx3, 0.5, tmm, tk_w)

    group_ind = (jnp.arange(D_SGU)[:, None] // SGU_GROUP_DIM == jnp.arange(LANES)[None, :]).astype(F32)
    do_t, do_tok, dz, dg_mixrow, dws, dbs = mix_bwd(dx2, z, o_t, g_sgu, g_attn_out, g_sgu_out, ws_b, ws_tb, bias,
                                                    full["w_out"], group_ind, tq)
    dw_out = matmul_tn(mixed, dx2, 1.0, D_MODEL // 2, tk_w)

    group_a = ("w2_gate", "w2_up", "w2_down", "w_out")
    (dq_t, dk_t, dv_t), (parts_a,) = attention_bwd(q_tok, do_tok, q_t, do_t, kt_tiles, vt_tiles_bwd,
                                                   scatter_exchange([dwg2, dwu2, dwd2, dw_out]))
    dqkv, dgq_w, dgk_w = qk_bwd(dq_t, dk_t, dv_t, qkv, gq_w, gk_w, cos_w, sin_w, mean_q, mean_k, tq)

    def pack_small(arrays):
        pieces = []
        for a in arrays:
            flat = a.reshape(-1)
            pieces.append(jnp.pad(flat, (0, (-flat.shape[0]) % (8 * LANES))).reshape(-1, LANES))
        return jnp.concatenate(pieces, axis=0), [p.shape[0] for p in pieces]

    early = dict(g_ffn2=dg_ffn2, g_final=dg_final, g_q=dgq_w.reshape(N_HEADS, HEAD_DIM).sum(0),
                 g_k=dgk_w.reshape(N_KV_HEADS, HEAD_DIM).sum(0), g_attn_out=dg_mixrow[0], g_sgu_out=dg_mixrow[1],
                 g_sgu=dg_mixrow[2], w_s=dws, b_s=dbs[:, :N_SGU_GROUPS].T)
    early_pack, early_rows = pack_small(list(early.values()))
    (dx1, dg_mix), (early_parts,) = norm_bwd_matmul(dqkv, w_qkv_t, dz, w_z_t, x1, g_mix, dx2, tm,
                                                    gather_exchange(early_pack, [early_pack.shape[0]]))
    dw_in = jnp.concatenate([matmul_tn(dqkv, h2, 1.0, D_QKV // 2, tk_w), matmul_tn(dz, h2, 1.0, D_SGU, tk_w)], axis=0)

    dwd1, (parts_in,) = matmul_tn(act1, dx1, 0.5, tmm, tk_w, scatter_exchange([dw_in]))
    (da1, db1), (parts_d1,) = ffn_bwd_act(dx1, full["w1_down"], a1, b1, tm_ff, tn_ff, scatter_exchange([dwd1]))
    dwg1 = matmul_tn(da1, h1, 1.0, tmm, tk_w)
    dwu1, (parts_g1,) = matmul_tn(db1, h1, 1.0, tmm, tk_w, scatter_exchange([dwg1]))
    (dx0, dg_ffn1), (parts_u1,) = norm_bwd_matmul(da1, full["w1_gate"], db1, full["w1_up"], x0, g_ffn1, dx1, tm,
                                                  scatter_exchange([dwu1]))
    scattered = ((group_a, parts_a), (("w_in",), parts_in), (("w1_down",), parts_d1), (("w1_gate",), parts_g1),
                 (("w1_up",), parts_u1))

    late = dict(g_mix=dg_mix, g_ffn1=dg_ffn1, loss=loss_part)
    late_pack, late_rows = pack_small(list(late.values()))
    (late_parts,) = run_exchange(gather_exchange(late_pack, [late_pack.shape[0]]), "gather_late_small_grads")
    small_sums = {}
    for entries, rows, parts in ((early, early_rows, early_parts), (late, late_rows, late_parts)):
        summed = sum_parts(parts, parts.shape[1])
        off = 0
        for n, r in zip(entries, rows):
            small_sums[n] = summed[off:off + r]
            off += r
    loss = small_sums.pop("loss")[0, 0]

    grads, row_grads = {}, {}
    for group, parts in scattered:
        rows = parts.shape[1]
        summed = sum_parts(parts, rows if rows <= 2 * rows_of["w1_gate"] else rows // 2)
        off = 0
        for n in group:
            row_grads[n] = summed[off:off + rows_of[n]]
            grads[n] = (row_grads[n].T if n in TRANSPOSED else row_grads[n])[None]
            off += rows_of[n]
    for n, summed in small_sums.items():
        grads[n] = summed.reshape(-1)[:weights[n].size].reshape(weights[n].shape)

    delta_w, new_m, new_v = {}, {}, {}
    for n in names:
        shape = weights[n].shape
        if n in TRANSPOSED:
            view, unview, g = (lambda a: a[0].T), (lambda a: a.T[None]), row_grads[n]
        else:
            view, unview = (lambda a: a.reshape(-1, shape[-1])), (lambda a: a.reshape(shape))
            g = view(grads[n])
        d, m2, v2 = adamw(view(weights[n]), g, view(m_in[n]), view(v_in[n]))
        delta_w[n], new_m[n], new_v[n] = unview(d), unview(m2), unview(v2)

    return (loss, dx0[None], *[grads[n] for n in names], *[delta_w[n] for n in names],
            *[new_m[n] for n in names], *[new_v[n] for n in names])
```

```python
import functools
import math

import jax
import jax.numpy as jnp
from jax import lax
from jax.experimental import pallas as pl
from jax.experimental.pallas import tpu as pltpu

F32 = jnp.float32
BF16 = jnp.bfloat16

D_MODEL = 1024
D_FF = 2816
N_HEADS = 8
HEAD_DIM = 64
N_KV_HEADS = 2
KV_GROUP = N_HEADS // N_KV_HEADS
D_ATTN = N_HEADS * HEAD_DIM
D_KV = N_KV_HEADS * HEAD_DIM
D_QKV = D_ATTN + 2 * D_KV
N_SGU_GROUPS = 8
SGU_GROUP_DIM = 64
D_SGU = N_SGU_GROUPS * SGU_GROUP_DIM
CHUNK = 128
GRID_W = 64
ROPE_THETA = 10000.0
EPS = 1e-6
N_DEV = 8
LANES = 128

ONES_ROWS = 16
SAFE_SCORE_BOUND = 60.0
LOG2_E = math.log2(math.e)
Q_SCALE = HEAD_DIM ** -0.5 * LOG2_E

ADAM_LR = 0.001
ADAM_B1 = 0.9
ADAM_B2 = 0.999
ADAM_EPS = 1e-08
ADAM_WD = 0.01
ADAM_STEP = 10

MESH_AXES = ("x", "y", "c")
MESH_IDS = pl.DeviceIdType.MESH

VMEM_LIMIT = 56 * 1024 * 1024

SHARD_ROWS = (("w1_gate", D_FF // N_DEV), ("w1_up", D_FF // N_DEV), ("w1_down", D_FF // N_DEV),
              ("w_in", (D_QKV + 2 * D_SGU) // N_DEV), ("w_out", D_MODEL // N_DEV),
              ("w2_gate", D_FF // N_DEV), ("w2_up", D_FF // N_DEV), ("w2_down", D_FF // N_DEV))
PACK_ROWS = sum(r for _, r in SHARD_ROWS)
TRANSPOSED = ("w1_gate", "w1_up", "w_in", "w2_gate", "w2_up")


def _params(n_grid):
    return pltpu.CompilerParams(dimension_semantics=("arbitrary",) * n_grid, vmem_limit_bytes=VMEM_LIMIT)


def _dot(a, b):
    return jnp.dot(a, b, preferred_element_type=F32)


def _dot_nt(a, b):
    return lax.dot_general(a, b, (((1,), (1,)), ((), ())), preferred_element_type=F32)


def _dot_tn(a, b):
    return lax.dot_general(a, b, (((0,), (0,)), ((), ())), preferred_element_type=F32)


def _dot_f32(a, b):
    return jnp.dot(a, b, preferred_element_type=F32, precision=lax.Precision.HIGHEST)


def _dot_split(a, b):
    hi = a.astype(BF16)
    lo = (a - hi.astype(F32)).astype(BF16)
    return _dot(hi, b) + _dot(lo, b)


def _rstd(x):
    return lax.rsqrt(jnp.mean(x * x, axis=-1, keepdims=True) + EPS)


def _rms_bwd(dy, n, r, g):
    dn = dy * g
    return r * (dn - n * jnp.mean(dn * n, axis=-1, keepdims=True)), dy * n


def _colsum(a):
    return jnp.sum(a, axis=0, keepdims=True)


_GELU_C = math.sqrt(2.0 / math.pi)


def _gelu(x):
    t = jnp.tanh(_GELU_C * (x + 0.044715 * (x * x * x)))
    return x * (0.5 * (1.0 + t)), t


def _gelu_grad(x, t):
    return 0.5 * (1.0 + t) + 0.5 * x * (1.0 - t * t) * (_GELU_C * (1.0 + 3 * 0.044715 * x * x))


def _pair_swap(a):
    w = a.shape[-1]
    lane = lax.broadcasted_iota(jnp.int32, a.shape, a.ndim - 1)
    return jnp.where(lane % 2 == 0, pltpu.roll(a, w - 1, a.ndim - 1), pltpu.roll(a, 1, a.ndim - 1))


def _tile_lanes(a, reps):
    return jnp.concatenate([a] * reps, axis=-1) if reps > 1 else a


def _loop_pairs(n, step, carry):
    assert n % 2 == 0, n

    def pair(jj, c):
        return step(2 * jj + 1, 1, step(2 * jj, 0, c))

    return lax.fori_loop(0, n // 2, pair, carry)


def _full(shape):
    nd = len(shape)
    return pl.BlockSpec(shape, lambda *_: (0,) * nd)


def _mesh_pos():
    return lax.axis_index("x"), lax.axis_index("y"), lax.axis_index("c")


def _peer(pos, d):
    x, y, c = pos
    px = 1 - x if d & 4 else x
    py = 1 - y if d & 2 else y
    pc = 1 - c if d & 1 else c
    return (px, py, pc), 4 * px + 2 * py + pc


class _Exchange:
    def __init__(self, operands, out_shape, n_local, plan):
        self.operands = list(operands)
        self.out_shape = list(out_shape)
        self.sem_shapes = [pltpu.SemaphoreType.DMA((N_DEV - 1,)), pltpu.SemaphoreType.DMA((N_DEV - 1,)),
                           pltpu.SemaphoreType.DMA((n_local,))]
        self._plan = plan

    def _copies(self, in_refs, out_refs):
        pos = _mesh_pos()
        return pos, self._plan(4 * pos[0] + 2 * pos[1] + pos[2], in_refs, out_refs)

    def start(self, in_refs, out_refs, sems):
        send_sems, recv_sems, local_sems = sems
        pos, (local, remote, _) = self._copies(in_refs, out_refs)
        for k, (src, dst) in enumerate(local):
            pltpu.make_async_copy(src, dst, local_sems.at[k]).start()
        for d in range(1, N_DEV):
            peer, peer_lin = _peer(pos, d)
            for src, dst in remote(peer_lin):
                pltpu.make_async_remote_copy(src_ref=src, dst_ref=dst, send_sem=send_sems.at[d - 1],
                                             recv_sem=recv_sems.at[d - 1], device_id=peer,
                                             device_id_type=MESH_IDS).start()

    def wait(self, in_refs, out_refs, sems):
        send_sems, recv_sems, local_sems = sems
        pos, (local, _, whole) = self._copies(in_refs, out_refs)
        for d in range(1, N_DEV):
            peer, peer_lin = _peer(pos, d)
            ref = whole(peer_lin)
            everything = pltpu.make_async_remote_copy(src_ref=ref, dst_ref=ref, send_sem=send_sems.at[d - 1],
                                                      recv_sem=recv_sems.at[d - 1], device_id=peer,
                                                      device_id_type=MESH_IDS)
            everything.wait_send()
            everything.wait_recv()
        for k, (src, dst) in enumerate(local):
            pltpu.make_async_copy(src, dst, local_sems.at[k]).wait()


def _offsets(rows):
    offs, o = [], 0
    for r in rows:
        offs.append(o)
        o += r
    return offs


def gather_exchange(src, rows):
    offs = _offsets(rows)

    def plan(me, in_refs, out_refs):
        pieces = [(in_refs[0].at[pl.ds(o, r)], out.at[me]) for o, r, out in zip(offs, rows, out_refs)]
        return pieces, (lambda peer_lin: pieces), (lambda peer_lin: in_refs[0])

    return _Exchange([src], [jax.ShapeDtypeStruct((N_DEV, r) + src.shape[1:], src.dtype) for r in rows],
                     len(rows), plan)


def scatter_exchange(grads):
    rows = [g.shape[0] // N_DEV for g in grads]
    offs = _offsets(rows)

    def plan(me, in_refs, out_refs):
        parts = out_refs[0]

        def slabs(owner):
            return [(g.at[pl.ds(pl.multiple_of(owner * r, 16), r)], parts.at[me, pl.ds(o, r)])
                    for g, o, r in zip(in_refs, offs, rows)]

        return slabs(me), slabs, (lambda peer_lin: parts.at[peer_lin])

    shape = jax.ShapeDtypeStruct((N_DEV, sum(rows)) + grads[0].shape[1:], grads[0].dtype)
    return _Exchange(grads, [shape], len(rows), plan)


def gather_two_level(src, rows, name):
    offs = _offsets(rows)
    n_p = len(rows)

    def body(src_ref, *refs):
        outs, (send_sems, recv_sems, local_sems) = refs[:n_p], refs[n_p:]
        x, y, c = _mesh_pos()
        me, sibling = (x, y, c), (x, y, 1 - c)
        chips = [(1 - x, y), (x, 1 - y), (1 - x, 1 - y)]

        def slab(w, dev):
            return outs[w].at[4 * dev[0] + 2 * dev[1] + dev[2]]

        def copy(w, k, block, to, from_src=False):
            return pltpu.make_async_remote_copy(
                src_ref=src_ref.at[pl.ds(offs[w], rows[w])] if from_src else slab(w, block), dst_ref=slab(w, block),
                send_sem=send_sems.at[w * 7 + k], recv_sem=recv_sems.at[w * 7 + k],
                device_id=to, device_id_type=MESH_IDS)

        mine = [pltpu.make_async_copy(src_ref.at[pl.ds(offs[w], rows[w])], slab(w, me), local_sems.at[w])
                for w in range(n_p)]
        for cp in mine:
            cp.start()
        first = []
        for w in range(n_p):
            first.append(copy(w, 0, me, sibling, True))
            first += [copy(w, 1 + j, me, (*chip, c), True) for j, chip in enumerate(chips)]
        for cp in first:
            cp.start()
        passed = []
        for j, chip in enumerate(chips):
            for w in range(n_p):
                copy(w, 1 + j, (*chip, c), me).wait_recv()
                cp = copy(w, 4 + j, (*chip, c), sibling)
                cp.start()
                passed.append(cp)
        for w in range(n_p):
            copy(w, 0, sibling, me).wait_recv()
            for j, chip in enumerate(chips):
                copy(w, 4 + j, (*chip, 1 - c), me).wait_recv()
        for cp in first + passed:
            cp.wait_send()
        for cp in mine:
            cp.wait()

    any_spec = pl.BlockSpec(memory_space=pl.ANY)
    return pl.pallas_call(
        functools.partial(body), name=name,
        out_shape=[jax.ShapeDtypeStruct((N_DEV, r) + src.shape[1:], src.dtype) for r in rows],
        in_specs=[any_spec], out_specs=[any_spec] * n_p,
        scratch_shapes=[pltpu.SemaphoreType.DMA((7 * n_p,)), pltpu.SemaphoreType.DMA((7 * n_p,)),
                        pltpu.SemaphoreType.DMA((n_p,))],
        compiler_params=pltpu.CompilerParams(has_side_effects=True),
    )(src)


def run_exchange(ex, name):
    n_in, n_out = len(ex.operands), len(ex.out_shape)

    def body(*refs):
        parts = refs[:n_in], refs[n_in:n_in + n_out], refs[n_in + n_out:]
        ex.start(*parts)
        ex.wait(*parts)

    any_spec = pl.BlockSpec(memory_space=pl.ANY)
    return pl.pallas_call(
        functools.partial(body), name=name, out_shape=ex.out_shape,
        in_specs=[any_spec] * n_in, out_specs=[any_spec] * n_out, scratch_shapes=ex.sem_shapes,
        compiler_params=pltpu.CompilerParams(has_side_effects=True),
    )(*ex.operands)


def _pallas(comm, body, *, name, grid, in_specs, out_specs, out_shape, args, scratch_shapes=()):
    params = _params(len(grid))
    if comm is None:
        res = pl.pallas_call(functools.partial(body), name=name, grid=grid, in_specs=list(in_specs),
                             out_specs=list(out_specs), out_shape=list(out_shape),
                             scratch_shapes=list(scratch_shapes), compiler_params=params)(*args)
        return list(res), []
    n_in, n_out, n_scr = len(in_specs), len(out_specs), len(scratch_shapes)
    c_in, c_out = len(comm.operands), len(comm.out_shape)

    def edge(last):
        conds = [pl.program_id(a) == (g - 1 if last else 0) for a, g in enumerate(grid)]
        return functools.reduce(jnp.logical_and, conds)

    def wrapped(*refs):
        refs = list(refs)
        ins, refs = refs[:n_in], refs[n_in:]
        cins, refs = refs[:c_in], refs[c_in:]
        outs, refs = refs[:n_out], refs[n_out:]
        couts, refs = refs[:c_out], refs[c_out:]
        scr, sems = refs[:n_scr], refs[n_scr:]

        @pl.when(edge(False))
        def _():
            comm.start(cins, couts, sems)

        body(*ins, *outs, *scr)

        @pl.when(edge(True))
        def _():
            comm.wait(cins, couts, sems)

    any_spec = pl.BlockSpec(memory_space=pl.ANY)
    res = pl.pallas_call(
        wrapped, name=name, grid=grid,
        in_specs=list(in_specs) + [any_spec] * c_in, out_specs=list(out_specs) + [any_spec] * c_out,
        out_shape=list(out_shape) + comm.out_shape, scratch_shapes=list(scratch_shapes) + comm.sem_shapes,
        compiler_params=pltpu.CompilerParams(dimension_semantics=("arbitrary",) * len(grid),
                                             vmem_limit_bytes=VMEM_LIMIT, has_side_effects=True),
    )(*args, *comm.operands)
    return res[:n_out], res[n_out:]


def sum_parts(parts, block_rows):
    n, rows, cols = parts.shape

    def body(p_ref, o_ref):
        acc = p_ref[0].astype(F32)
        for s in range(1, n):
            acc = acc + p_ref[s].astype(F32)
        o_ref[...] = acc

    return pl.pallas_call(
        functools.partial(body), name="sum_parts",
        grid=(rows // block_rows,),
        in_specs=[pl.BlockSpec((n, block_rows, cols), lambda i: (0, i, 0))],
        out_specs=pl.BlockSpec((block_rows, cols), lambda i: (i, 0)),
        out_shape=jax.ShapeDtypeStruct((rows, cols), F32),
        compiler_params=_params(1),
    )(parts)


def adamw(w, g, m, v):
    def body(w_ref, g_ref, m_ref, v_ref, d_ref, m_out, v_out):
        gg = g_ref[...]
        m2 = ADAM_B1 * m_ref[...] + (1.0 - ADAM_B1) * gg
        v2 = ADAM_B2 * v_ref[...] + (1.0 - ADAM_B2) * (gg * gg)
        m_hat = m2 / (1.0 - ADAM_B1 ** ADAM_STEP)
        v_hat = v2 / (1.0 - ADAM_B2 ** ADAM_STEP)
        d_ref[...] = -ADAM_LR * (m_hat / (jnp.sqrt(v_hat) + ADAM_EPS) + ADAM_WD * w_ref[...])
        m_out[...] = m2
        v_out[...] = v2

    spec = _full(w.shape)
    shape = jax.ShapeDtypeStruct(w.shape, F32)
    return pl.pallas_call(
        functools.partial(body), name="adamw",
        in_specs=[spec] * 4, out_specs=[spec] * 3, out_shape=[shape] * 3,
        compiler_params=pltpu.CompilerParams(vmem_limit_bytes=VMEM_LIMIT),
    )(w, g, m, v)


def ffn_up(x, g, wg_t, wu_t, tm, tn, comm=None):
    t = x.shape[0]

    def body(x_ref, g_ref, wg_ref, wu_ref, h_ref, silu_ref, dgate_ref, act_ref):
        xx = x_ref[...]
        h = ((xx * _rstd(xx)) * g_ref[...]).astype(BF16)
        h_ref[...] = h
        for c in range(D_FF // tn):
            cols = slice(c * tn, (c + 1) * tn)
            a = _dot_nt(h, wg_ref[cols, :])
            b = _dot_nt(h, wu_ref[cols, :])
            sig = 0.5 * jnp.tanh(0.5 * a) + 0.5
            silu = a * sig
            silu_ref[:, cols] = silu.astype(BF16)
            dgate_ref[:, cols] = (b * (sig + silu * (1.0 - sig))).astype(BF16)
            act_ref[:, cols] = (silu * b).astype(BF16)

    wide = jax.ShapeDtypeStruct((t, D_FF), BF16)
    row = lambda n: pl.BlockSpec((tm, n), lambda i: (i, 0))
    return _pallas(
        comm, body, name="ffn_up",
        grid=(t // tm,),
        in_specs=[row(D_MODEL), _full((1, D_MODEL)), _full((D_FF, D_MODEL)), _full((D_FF, D_MODEL))],
        out_specs=[row(D_MODEL), row(D_FF), row(D_FF), row(D_FF)],
        out_shape=[jax.ShapeDtypeStruct((t, D_MODEL), BF16), wide, wide, wide],
        args=(x, g, wg_t, wu_t))


def ffn_down(act, wd, x, tm):
    t = x.shape[0]

    def body(act_ref, wd_ref, x_ref, o_ref):
        o_ref[...] = x_ref[...] + 0.5 * _dot(act_ref[...], wd_ref[...])

    return pl.pallas_call(
        functools.partial(body), name="ffn_down",
        grid=(t // tm,),
        in_specs=[pl.BlockSpec((tm, D_FF), lambda i: (i, 0)), _full((D_FF, D_MODEL)),
                  pl.BlockSpec((tm, D_MODEL), lambda i: (i, 0))],
        out_specs=pl.BlockSpec((tm, D_MODEL), lambda i: (i, 0)),
        out_shape=jax.ShapeDtypeStruct((t, D_MODEL), F32),
        compiler_params=_params(1),
    )(act, wd, x)


def ffn_bwd_act(dx, wd, silu, dgate, tm, tn, comm=None):
    t = dx.shape[0]

    def body(dx_ref, wd_ref, silu_ref, dgate_ref, da_ref, db_ref):
        dxb = (0.5 * dx_ref[...]).astype(BF16)
        for c in range(D_FF // tn):
            cols = slice(c * tn, (c + 1) * tn)
            dact = _dot_nt(dxb, wd_ref[cols, :])
            da_ref[:, cols] = (dact * dgate_ref[:, cols].astype(F32)).astype(BF16)
            db_ref[:, cols] = (dact * silu_ref[:, cols].astype(F32)).astype(BF16)

    wide = jax.ShapeDtypeStruct((t, D_FF), BF16)
    row = lambda n: pl.BlockSpec((tm, n), lambda i: (i, 0))
    return _pallas(
        comm, body, name="ffn_bwd_act",
        grid=(t // tm,),
        in_specs=[row(D_MODEL), _full((D_FF, D_MODEL)), row(D_FF), row(D_FF)],
        out_specs=[row(D_FF), row(D_FF)],
        out_shape=[wide, wide],
        args=(dx, wd, silu, dgate))


def norm_bwd_matmul(a1, w1, a2, w2, x, g, dx_in, tm, comm=None):
    t = x.shape[0]
    k1, k2 = a1.shape[1], a2.shape[1]

    def body(a1_ref, w1_ref, a2_ref, w2_ref, x_ref, g_ref, dxin_ref, dx_ref, dg_ref):
        dh = _dot(a1_ref[...], w1_ref[...]) + _dot(a2_ref[...], w2_ref[...])
        xx = x_ref[...]
        r = _rstd(xx)
        dx, dg_rows = _rms_bwd(dh, xx * r, r, g_ref[...])
        dx_ref[...] = dxin_ref[...] + dx

        @pl.when(pl.program_id(0) == 0)
        def _():
            dg_ref[...] = jnp.zeros_like(dg_ref)

        dg_ref[...] += _colsum(dg_rows)

    row = pl.BlockSpec((tm, D_MODEL), lambda i: (i, 0))
    return _pallas(
        comm, body, name="norm_bwd_matmul",
        grid=(t // tm,),
        in_specs=[pl.BlockSpec((tm, k1), lambda i: (i, 0)), _full((k1, D_MODEL)),
                  pl.BlockSpec((tm, k2), lambda i: (i, 0)), _full((k2, D_MODEL)),
                  row, _full((1, D_MODEL)), row],
        out_specs=[row, _full((1, D_MODEL))],
        out_shape=[jax.ShapeDtypeStruct((t, D_MODEL), F32), jax.ShapeDtypeStruct((1, D_MODEL), F32)],
        args=(a1, w1, a2, w2, x, g, dx_in))


def matmul_tn(a, b, scale, tmm, tk, comm=None):
    t, m = a.shape
    n = b.shape[1]
    nk = t // tk

    def body(a_ref, b_ref, o_ref, acc_ref):
        k = pl.program_id(1)

        @pl.when(k == 0)
        def _():
            acc_ref[...] = jnp.zeros_like(acc_ref)

        acc_ref[...] += _dot_tn(a_ref[...].astype(BF16), b_ref[...].astype(BF16))

        @pl.when(k == nk - 1)
        def _():
            o_ref[...] = (scale * acc_ref[...]).astype(BF16)

    (out,), comm_outs = _pallas(
        comm, body, name="matmul_tn",
        grid=(m // tmm, nk),
        in_specs=[pl.BlockSpec((tk, tmm), lambda i, k: (k, i)), pl.BlockSpec((tk, n), lambda i, k: (k, 0))],
        out_specs=[pl.BlockSpec((tmm, n), lambda i, k: (i, 0))],
        out_shape=[jax.ShapeDtypeStruct((m, n), BF16)],
        scratch_shapes=[pltpu.VMEM((tmm, n), F32)],
        args=(a, b))
    return out if comm is None else (out, comm_outs)


def input_projection(x, g, w_qkv_t, w_z_t, tm):
    t = x.shape[0]

    def body(x_ref, g_ref, wq_ref, wz_ref, qkv_ref, z_ref, h_ref):
        xx = x_ref[...]
        h = ((xx * _rstd(xx)) * g_ref[...]).astype(BF16)
        h_ref[...] = h
        qkv_ref[...] = _dot_nt(h, wq_ref[...])
        z_ref[...] = _dot_nt(h, wz_ref[...])

    row = lambda n: pl.BlockSpec((tm, n), lambda i: (i, 0))
    return pl.pallas_call(
        functools.partial(body), name="input_projection", grid=(t // tm,),
        in_specs=[row(D_MODEL), _full((1, D_MODEL)), _full((D_QKV, D_MODEL)), _full((2 * D_SGU, D_MODEL))],
        out_specs=[row(D_QKV), row(2 * D_SGU), row(D_MODEL)],
        out_shape=[jax.ShapeDtypeStruct((t, D_QKV), F32), jax.ShapeDtypeStruct((t, 2 * D_SGU), F32),
                   jax.ShapeDtypeStruct((t, D_MODEL), BF16)],
        compiler_params=_params(1))(x, g, w_qkv_t, w_z_t)


def _shift_rows(shape, first, second):
    row = lax.broadcasted_iota(jnp.int32, shape, len(shape) - 2)
    return jnp.where(row == 0, first, jnp.where(row == 1, second, 0.0))


def _hi_lo(a):
    hi = a.astype(BF16).astype(F32)
    return hi, a - hi


def _head_tile_spec(tm, rows):
    return pl.BlockSpec((N_HEADS, None, rows, tm), lambda i: (0, i, 0, 0))


def _to_head_tiles(a):
    return a.T.reshape(N_HEADS, HEAD_DIM, a.shape[0])


def _from_head_tiles(a):
    return a.reshape(D_ATTN, a.shape[-1]).T


def _head_mean_matrix(width):
    head = jnp.arange(width) // HEAD_DIM
    return (head[:, None] == head[None, :]).astype(F32) / HEAD_DIM


def _kv_tile_spec(n_sub, rows, cols):
    return pl.BlockSpec((N_KV_HEADS, n_sub, rows, cols), lambda i: (0, i, 0, 0))


def qk_prep(qkv, gq_w, gk_w, cos_w, sin_w, mean_q, mean_k, tm, tk, tk_v):
    t = qkv.shape[0]
    n_sub, n_sub_v = tm // tk, tm // tk_v

    def body(p_ref, gq_ref, gk_ref, cos_ref, sin_ref, mq_ref, mk_ref, q_ref, k_ref, kt_ref, vt_ref, vtb_ref,
             qmax_ref, kmax_ref):
        @pl.when(pl.program_id(0) == 0)
        def _():
            qmax_ref[...] = jnp.zeros_like(qmax_ref)
            kmax_ref[...] = jnp.zeros_like(kmax_ref)

        cos2, sin2 = cos_ref[...], sin_ref[...]
        q = p_ref[:, :D_ATTN]
        k = p_ref[:, D_ATTN:D_ATTN + D_KV]
        qn = q * lax.rsqrt(_dot_split(q * q, mq_ref[...]) + EPS) * gq_ref[...]
        kn = k * lax.rsqrt(_dot_split(k * k, mk_ref[...]) + EPS) * gk_ref[...]
        cos8, sin8 = _tile_lanes(cos2, D_ATTN // LANES), _tile_lanes(sin2, D_ATTN // LANES)
        q_rot = (qn * cos8 + _pair_swap(qn) * sin8) * Q_SCALE
        q_ref[...] = _to_head_tiles(q_rot).astype(BF16)
        k_rot = kn * cos2 + _pair_swap(kn) * sin2
        q_sq = HEAD_DIM * _dot_split(q_rot * q_rot, mq_ref[...])
        k_sq = HEAD_DIM * _dot_split(k_rot * k_rot, mk_ref[...])
        qmax_ref[...] = jnp.maximum(qmax_ref[...], jnp.max(q_sq, axis=0, keepdims=True))
        kmax_ref[...] = jnp.maximum(kmax_ref[...], jnp.max(k_sq, axis=0, keepdims=True))
        vv = p_ref[:, D_ATTN + D_KV:]
        second = pltpu.roll(k_rot, HEAD_DIM, 1)
        for c in range(n_sub):
            rows = slice(c * tk, (c + 1) * tk)
            k_ref[0, c] = k_rot[rows, :HEAD_DIM].astype(BF16)
            k_ref[1, c] = second[rows, :HEAD_DIM].astype(BF16)
        for a, feat_ref, width, n in ((k_rot, kt_ref, tk, n_sub), (vv, vtb_ref, tk, n_sub), (vv, vt_ref, tk_v, n_sub_v)):
            for c in range(n):
                tile = a[c * width:(c + 1) * width].T.reshape(N_KV_HEADS, HEAD_DIM, width)
                feat_ref[:, c, :HEAD_DIM, :] = tile.astype(BF16)
        vt_ref[:, :, HEAD_DIM:, :] = jnp.ones((N_KV_HEADS, n_sub_v, ONES_ROWS, tk_v), BF16)
        minus = _shift_rows((N_KV_HEADS, n_sub, HEAD_DIM, tk), -1.0, -1.0).astype(BF16)
        kt_ref[:, :, HEAD_DIM:, :] = minus
        vtb_ref[:, :, HEAD_DIM:, :] = minus

    kv = lambda rows, cols: jax.ShapeDtypeStruct((N_KV_HEADS, t // tk, rows, cols), BF16)
    return pl.pallas_call(
        functools.partial(body), name="qk_prep", grid=(t // tm,),
        in_specs=[pl.BlockSpec((tm, D_QKV), lambda i: (i, 0)), _full((1, D_ATTN)), _full((1, D_KV)),
                  pl.BlockSpec((tm, LANES), lambda i: (i, 0)), pl.BlockSpec((tm, LANES), lambda i: (i, 0)),
                  _full((D_ATTN, D_ATTN)), _full((D_KV, D_KV))],
        out_specs=[_head_tile_spec(tm, HEAD_DIM), _kv_tile_spec(n_sub, tk, HEAD_DIM),
                   _kv_tile_spec(n_sub, 2 * HEAD_DIM, tk),
                   _kv_tile_spec(n_sub_v, HEAD_DIM + ONES_ROWS, tk_v), _kv_tile_spec(n_sub, 2 * HEAD_DIM, tk),
                   _full((1, D_ATTN)), _full((1, D_KV))],
        out_shape=[jax.ShapeDtypeStruct((N_HEADS, t // tm, HEAD_DIM, tm), BF16), kv(tk, HEAD_DIM), kv(2 * HEAD_DIM, tk),
                   jax.ShapeDtypeStruct((N_KV_HEADS, t // tk_v, HEAD_DIM + ONES_ROWS, tk_v), BF16),
                   kv(2 * HEAD_DIM, tk),
                   jax.ShapeDtypeStruct((1, D_ATTN), F32), jax.ShapeDtypeStruct((1, D_KV), F32)],
        compiler_params=_params(1),
    )(qkv, gq_w, gk_w, cos_w, sin_w, mean_q, mean_k)


def qk_bwd(dq_rot, dk_rot, dv, qkv, gq_w, gk_w, cos_w, sin_w, mean_q, mean_k, tm):
    t = qkv.shape[0]
    tk = dk_rot.shape[-1]
    n_sub = tm // tk

    def token_major(ref):
        return jnp.concatenate([ref[:, c].reshape(D_KV, tk).T for c in range(n_sub)], axis=0)

    def branch(raw, d_rot, gain, mean_mat, cos, sin, scale):
        r = lax.rsqrt(_dot_split(raw * raw, mean_mat) + EPS)
        n = raw * r
        dy = (d_rot * cos - _pair_swap(d_rot) * sin) * scale
        dn = dy * gain
        return r * (dn - n * _dot_split(dn * n, mean_mat)), dy * n

    def body(dq_ref, dk_ref, dv_ref, p_ref, gq_ref, gk_ref, cos_ref, sin_ref, mq_ref, mk_ref,
             dp_ref, dgq_ref, dgk_ref):
        cos2, sin2 = cos_ref[...], sin_ref[...]
        cos8, sin8 = _tile_lanes(cos2, D_ATTN // LANES), _tile_lanes(sin2, D_ATTN // LANES)
        dq, dgq = branch(p_ref[:, :D_ATTN], _from_head_tiles(dq_ref[...]), gq_ref[...], mq_ref[...], cos8, sin8,
                         HEAD_DIM ** -0.5)
        dk, dgk = branch(p_ref[:, D_ATTN:D_ATTN + D_KV], token_major(dk_ref), gk_ref[...], mk_ref[...], cos2, sin2, 1.0)
        dp_ref[...] = jnp.concatenate([dq, dk, token_major(dv_ref)], axis=-1).astype(BF16)

        @pl.when(pl.program_id(0) == 0)
        def _():
            dgq_ref[...] = jnp.zeros_like(dgq_ref)
            dgk_ref[...] = jnp.zeros_like(dgk_ref)

        dgq_ref[...] += _colsum(dgq)
        dgk_ref[...] += _colsum(dgk)

    return pl.pallas_call(
        functools.partial(body), name="qk_bwd", grid=(t // tm,),
        in_specs=[_head_tile_spec(tm, HEAD_DIM), _kv_tile_spec(n_sub, HEAD_DIM, tk),
                  _kv_tile_spec(n_sub, HEAD_DIM, tk), pl.BlockSpec((tm, D_QKV), lambda i: (i, 0)),
                  _full((1, D_ATTN)), _full((1, D_KV)),
                  pl.BlockSpec((tm, LANES), lambda i: (i, 0)), pl.BlockSpec((tm, LANES), lambda i: (i, 0)),
                  _full((D_ATTN, D_ATTN)), _full((D_KV, D_KV))],
        out_specs=[pl.BlockSpec((tm, D_QKV), lambda i: (i, 0)), _full((1, D_ATTN)), _full((1, D_KV))],
        out_shape=[jax.ShapeDtypeStruct((t, D_QKV), BF16), jax.ShapeDtypeStruct((1, D_ATTN), F32),
                   jax.ShapeDtypeStruct((1, D_KV), F32)],
        compiler_params=_params(1),
    )(dq_rot, dk_rot, dv, qkv, gq_w, gk_w, cos_w, sin_w, mean_q, mean_k)


def attention_fwd(bound, q_t, k, v_t, comm=None):
    _, nq, _, tq = q_t.shape
    _, nk, tk, _ = k.shape

    def body(bound_ref, q_ref, k_ref, v_ref, o_ref, qtok_ref, s_scr, p_scr):
        head_bound = bound_ref[pl.program_id(0)]
        safe = head_bound <= SAFE_SCORE_BOUND
        q = q_ref[...]
        s_scr[0] = _dot(k_ref[0], q)
        p_scr[1] = jnp.zeros((tk, tq), BF16)
        zero = jnp.zeros((HEAD_DIM + ONES_ROWS, tq), F32)

        def matmuls(j, slot):
            pv = _dot(v_ref[jnp.maximum(j - 1, 0)], p_scr[1 - slot])
            s_scr[1 - slot] = _dot(k_ref[jnp.minimum(j + 1, nk - 1)], q)
            return pv

        def finish(m, acc):
            acc = acc + _dot(v_ref[nk - 1], p_scr[(nk - 1) % 2])
            l = acc[HEAD_DIM:HEAD_DIM + 1]
            o_ref[...] = acc[:HEAD_DIM] / l
            lse_rows = _shift_rows((HEAD_DIM, tq), *_hi_lo(m + jnp.log2(l)))
            qtok_ref[...] = jnp.concatenate([q.astype(F32), lse_rows], axis=0).T.astype(BF16)

        @pl.when(safe)
        def _():
            m = jnp.full((1, tq), head_bound, F32)

            def step(j, slot, acc):
                s = s_scr[slot]
                pv = matmuls(j, slot)
                p_scr[slot] = jnp.exp2(s - m).astype(BF16)
                return acc + pv

            finish(m, _loop_pairs(nk, step, zero))

        @pl.when(jnp.logical_not(safe))
        def _():
            def step(j, slot, carry):
                m, acc = carry
                s = s_scr[slot]
                pv = matmuls(j, slot)
                m_new = jnp.maximum(m, jnp.max(s, axis=0, keepdims=True))
                p_scr[slot] = jnp.exp2(s - m_new).astype(BF16)
                return m_new, jnp.exp2(m - m_new) * (acc + pv)

            finish(*_loop_pairs(nk, step, (jnp.full((1, tq), -1e30, F32), zero)))

    return _pallas(
        comm, body, name="attention_fwd", grid=(N_HEADS, nq),
        in_specs=[pl.BlockSpec(memory_space=pltpu.SMEM),
                  pl.BlockSpec((None, None, HEAD_DIM, tq), lambda h, i: (h, i, 0, 0)),
                  pl.BlockSpec((None, nk, tk, HEAD_DIM), lambda h, i: (h // KV_GROUP, 0, 0, 0)),
                  pl.BlockSpec((None, nk, HEAD_DIM + ONES_ROWS, tk), lambda h, i: (h // KV_GROUP, 0, 0, 0))],
        out_specs=[pl.BlockSpec((None, None, HEAD_DIM, tq), lambda h, i: (h, i, 0, 0)),
                   pl.BlockSpec((None, None, tq, 2 * HEAD_DIM), lambda h, i: (h, i, 0, 0))],
        out_shape=[jax.ShapeDtypeStruct((N_HEADS, nq, HEAD_DIM, tq), F32),
                   jax.ShapeDtypeStruct((N_HEADS, nq, tq, 2 * HEAD_DIM), BF16)],
        scratch_shapes=[pltpu.VMEM((2, tk, tq), F32), pltpu.VMEM((2, tk, tq), BF16)],
        args=(bound, q_t, k, v_t))


def attention_bwd(q_tok, do_tok, q_t, do_t, k_t, v_t, comm=None):
    _, nq, _, tq = q_t.shape
    _, nk, _, tk = k_t.shape

    def body(qtok_ref, dotok_ref, q_ref, do_ref, kt_ref, vt_ref, dq_ref, dk_ref, dv_ref,
             s_scr, dp_scr, p_scr, ds_scr):
        @pl.when(pl.program_id(1) == 0)
        def _():
            dq_ref[...] = jnp.zeros_like(dq_ref)

        kt_aug, vt_aug = kt_ref[...], vt_ref[...]
        kt = kt_aug[:HEAD_DIM]
        n = KV_GROUP * nq
        s_scr[0] = _dot(qtok_ref[0, 0], kt_aug)
        dp_scr[0] = _dot(dotok_ref[0, 0], vt_aug)
        p_scr[1] = jnp.zeros((tq, tk), BF16)
        ds_scr[1] = jnp.zeros((tq, tk), BF16)

        def products(t, slot, dk, dv):
            h, i = t // nq, t % nq
            ds = ds_scr[slot]
            dq_ref[h, i] += _dot_nt(kt, ds)
            return dk + _dot(q_ref[h, i], ds), dv + _dot(do_ref[h, i], p_scr[slot])

        def step(t, slot, carry):
            s, dp = s_scr[slot], dp_scr[slot]
            dk, dv = products(jnp.maximum(t - 1, 0), 1 - slot, *carry)
            nxt = jnp.minimum(t + 1, n - 1)
            s_scr[1 - slot] = _dot(qtok_ref[nxt // nq, nxt % nq], kt_aug)
            dp_scr[1 - slot] = _dot(dotok_ref[nxt // nq, nxt % nq], vt_aug)
            p = jnp.exp2(s)
            p_scr[slot] = p.astype(BF16)
            ds_scr[slot] = (p * dp).astype(BF16)
            return dk, dv

        zero = jnp.zeros((HEAD_DIM, tk), F32)
        dk, dv = products(n - 1, (n - 1) % 2, *_loop_pairs(n, step, (zero, zero)))
        dk_ref[...] = dk * (1.0 / LOG2_E)
        dv_ref[...] = dv

    group = lambda g, j: (g, 0, 0, 0)
    tile = lambda g, j: (g, j, 0, 0)
    once = pl.Buffered(1)
    return _pallas(
        comm, body, name="attention_bwd", grid=(N_KV_HEADS, nk),
        in_specs=[pl.BlockSpec((KV_GROUP, nq, tq, 2 * HEAD_DIM), group, pipeline_mode=once),
                  pl.BlockSpec((KV_GROUP, nq, tq, 2 * HEAD_DIM), group, pipeline_mode=once),
                  pl.BlockSpec((KV_GROUP, nq, HEAD_DIM, tq), group, pipeline_mode=once),
                  pl.BlockSpec((KV_GROUP, nq, HEAD_DIM, tq), group, pipeline_mode=once),
                  pl.BlockSpec((None, None, 2 * HEAD_DIM, tk), tile),
                  pl.BlockSpec((None, None, 2 * HEAD_DIM, tk), tile)],
        out_specs=[pl.BlockSpec((KV_GROUP, nq, HEAD_DIM, tq), group),
                   pl.BlockSpec((None, None, HEAD_DIM, tk), tile),
                   pl.BlockSpec((None, None, HEAD_DIM, tk), tile)],
        out_shape=[jax.ShapeDtypeStruct((N_HEADS, nq, HEAD_DIM, tq), F32),
                   jax.ShapeDtypeStruct((N_KV_HEADS, nk, HEAD_DIM, tk), F32),
                   jax.ShapeDtypeStruct((N_KV_HEADS, nk, HEAD_DIM, tk), F32)],
        scratch_shapes=[pltpu.VMEM((2, tq, tk), F32), pltpu.VMEM((2, tq, tk), F32),
                        pltpu.VMEM((2, tq, tk), BF16), pltpu.VMEM((2, tq, tk), BF16)],
        args=(q_tok, do_tok, q_t, do_t, k_t, v_t))


def _group_matmul(a_t, w_ref):
    return jnp.concatenate([_dot(a_t[g * SGU_GROUP_DIM:(g + 1) * SGU_GROUP_DIM], w_ref[g])
                            for g in range(N_SGU_GROUPS)], axis=0)


def _gate_forward(z, g_sgu, wst_ref, bias):
    gz, th = _gelu(z)
    u, vv = gz[:, :D_SGU], gz[:, D_SGU:]
    rv = _rstd(vv)
    nv = vv * rv
    vn = nv * g_sgu
    v_chunks, fs = [], []
    for c in range(z.shape[0] // CHUNK):
        vt = vn[c * CHUNK:(c + 1) * CHUNK].T.astype(BF16)
        v_chunks.append(vt)
        fs.append(_group_matmul(vt, wst_ref).T + bias)
    f = jnp.concatenate(fs, axis=0) if len(fs) > 1 else fs[0]
    return th, u, rv, nv, v_chunks, f


def mix_out(z, o, x, g_sgu, g_ao, g_so, ws_t, bias, w_out, tm):
    t = x.shape[0]

    def body(z_ref, o_ref, x_ref, gs_ref, gao_ref, gso_ref, ws_ref, bias_ref, wout_ref, x2_ref, mixed_ref):
        _, u, _, _, _, f = _gate_forward(z_ref[...], gs_ref[...], ws_ref, bias_ref[...])
        sgu = u * f
        oo = _from_head_tiles(o_ref[...])
        mixed = jnp.concatenate([oo * _rstd(oo) * gao_ref[...], sgu * _rstd(sgu) * gso_ref[...]], axis=-1).astype(BF16)
        mixed_ref[...] = mixed
        x2_ref[...] = x_ref[...] + _dot(mixed, wout_ref[...])

    row = lambda n: pl.BlockSpec((tm, n), lambda i: (i, 0))
    return pl.pallas_call(
        functools.partial(body), name="mix_out", grid=(t // tm,),
        in_specs=[row(2 * D_SGU), _head_tile_spec(tm, HEAD_DIM), row(D_MODEL), _full((1, D_SGU)), _full((1, D_ATTN)),
                  _full((1, D_SGU)),
                  _full((N_SGU_GROUPS, CHUNK, CHUNK)), _full((CHUNK, D_SGU)), _full((D_MODEL, D_MODEL))],
        out_specs=[row(D_MODEL), row(D_MODEL)],
        out_shape=[jax.ShapeDtypeStruct((t, D_MODEL), F32), jax.ShapeDtypeStruct((t, D_MODEL), BF16)],
        compiler_params=_params(1),
    )(z, o, x, g_sgu, g_ao, g_so, ws_t, bias, w_out)


def mix_bwd(dx2, z, o, g_sgu, g_ao, g_so, ws, ws_t, bias, w_out, group_ind, tm):
    t = dx2.shape[0]
    n_tiles = t // tm

    def body(dx_ref, z_ref, o_ref, gs_ref, gao_ref, gso_ref, ws_ref, wst_ref, bias_ref, wout_ref, ind_ref,
             do_ref, dotok_ref, dz_ref, dg_ref, dws_ref, dbs_ref, df_sum):
        step = pl.program_id(0)

        @pl.when(step == 0)
        def _():
            dg_ref[...] = jnp.zeros_like(dg_ref)
            dws_ref[...] = jnp.zeros_like(dws_ref)
            df_sum[...] = jnp.zeros_like(df_sum)

        z = z_ref[...]
        th, u, rv, nv, v_chunks, f = _gate_forward(z, gs_ref[...], wst_ref, bias_ref[...])
        dmixed = _dot_nt(dx_ref[...].astype(BF16), wout_ref[...])
        o_tiles = o_ref[...]
        oo = _from_head_tiles(o_tiles)
        ro = _rstd(oo)
        d_o, dgao = _rms_bwd(dmixed[:, :D_ATTN], oo * ro, ro, gao_ref[...])
        do_tiles = _to_head_tiles(d_o)
        do_ref[...] = do_tiles.astype(BF16)
        delta_hi, delta_lo = _hi_lo(jnp.sum(do_tiles * o_tiles, axis=1, keepdims=True))
        for h in range(N_HEADS):
            delta_rows = _shift_rows((HEAD_DIM, tm), delta_hi[h], delta_lo[h])
            dotok_ref[h] = jnp.concatenate([do_tiles[h], delta_rows], axis=0).T.astype(BF16)
        sgu = u * f
        rs = _rstd(sgu)
        dsgu, dgso = _rms_bwd(dmixed[:, D_ATTN:], sgu * rs, rs, gso_ref[...])
        du = dsgu * f
        df = dsgu * u
        dvns = []
        df_acc = jnp.zeros((CHUNK, D_SGU), F32)
        for c in range(tm // CHUNK):
            dfc32 = df[c * CHUNK:(c + 1) * CHUNK]
            dft = dfc32.T.astype(BF16)
            dvns.append(_group_matmul(dft, ws_ref).T)
            for g in range(N_SGU_GROUPS):
                rows = slice(g * SGU_GROUP_DIM, (g + 1) * SGU_GROUP_DIM)
                dws_ref[g] += _dot_tn(dft[rows], v_chunks[c][rows])
            df_acc = df_acc + dfc32
        df_sum[...] += df_acc
        dvn = jnp.concatenate(dvns, axis=0) if len(dvns) > 1 else dvns[0]
        dvv, dgs = _rms_bwd(dvn, nv, rv, gs_ref[...])
        dz_ref[...] = (jnp.concatenate([du, dvv], axis=-1) * _gelu_grad(z, th)).astype(BF16)
        dg_ref[0:1, :] += _colsum(dgao)
        dg_ref[1:2, :] += _colsum(dgso)
        dg_ref[2:3, :] += _colsum(dgs)

        @pl.when(step == n_tiles - 1)
        def _():
            dbs_ref[...] = _dot_f32(df_sum[...], ind_ref[...])

    row = lambda n: pl.BlockSpec((tm, n), lambda i: (i, 0))
    return pl.pallas_call(
        functools.partial(body), name="mix_bwd", grid=(n_tiles,),
        in_specs=[row(D_MODEL), row(2 * D_SGU), _head_tile_spec(tm, HEAD_DIM), _full((1, D_SGU)), _full((1, D_ATTN)),
                  _full((1, D_SGU)),
                  _full((N_SGU_GROUPS, CHUNK, CHUNK)), _full((N_SGU_GROUPS, CHUNK, CHUNK)), _full((CHUNK, D_SGU)),
                  _full((D_MODEL, D_MODEL)), _full((D_SGU, LANES))],
        out_specs=[_head_tile_spec(tm, HEAD_DIM), pl.BlockSpec((N_HEADS, None, tm, 2 * HEAD_DIM), lambda i: (0, i, 0, 0)),
                   row(2 * D_SGU), _full((8, D_SGU)),
                   _full((N_SGU_GROUPS, CHUNK, CHUNK)), _full((CHUNK, LANES))],
        out_shape=[jax.ShapeDtypeStruct((N_HEADS, n_tiles, HEAD_DIM, tm), BF16),
                   jax.ShapeDtypeStruct((N_HEADS, n_tiles, tm, 2 * HEAD_DIM), BF16),
                   jax.ShapeDtypeStruct((t, 2 * D_SGU), BF16),
                   jax.ShapeDtypeStruct((8, D_SGU), F32),
                   jax.ShapeDtypeStruct((N_SGU_GROUPS, CHUNK, CHUNK), F32),
                   jax.ShapeDtypeStruct((CHUNK, LANES), F32)],
        scratch_shapes=[pltpu.VMEM((CHUNK, D_SGU), F32)],
        compiler_params=_params(1),
    )(dx2, z, o, g_sgu, g_ao, g_so, ws, ws_t, bias, w_out, group_ind)


def ffn_down_loss(act, wd, x, g, target, tm):
    t = x.shape[0]

    def body(act_ref, wd_ref, x_ref, g_ref, t_ref, loss_ref, dx_ref, dg_ref):
        @pl.when(pl.program_id(0) == 0)
        def _():
            loss_ref[...] = jnp.zeros_like(loss_ref)
            dg_ref[...] = jnp.zeros_like(dg_ref)

        xx = x_ref[...] + 0.5 * _dot(act_ref[...], wd_ref[...])
        r = _rstd(xx)
        n = xx * r
        err = n * g_ref[...] - t_ref[...]
        per_token = jnp.mean(err * err, axis=-1, keepdims=True)
        loss_ref[...] += 0.5 * jnp.sum(per_token, axis=0, keepdims=True)
        dx, dg_rows = _rms_bwd(err * (1.0 / D_MODEL), n, r, g_ref[...])
        dx_ref[...] = dx
        dg_ref[...] += _colsum(dg_rows)

    row = pl.BlockSpec((tm, D_MODEL), lambda i: (i, 0))
    return pl.pallas_call(
        functools.partial(body), name="ffn_down_loss", grid=(t // tm,),
        in_specs=[pl.BlockSpec((tm, D_FF), lambda i: (i, 0)), _full((D_FF, D_MODEL)), row, _full((1, D_MODEL)), row],
        out_specs=[_full((1, LANES)), row, _full((1, D_MODEL))],
        out_shape=[jax.ShapeDtypeStruct((1, LANES), F32), jax.ShapeDtypeStruct((t, D_MODEL), F32),
                   jax.ShapeDtypeStruct((1, D_MODEL), F32)],
        compiler_params=_params(1),
    )(act, wd, x, g, target)


def _rope_tables(t):
    rows = t // GRID_W
    row_idx = jnp.repeat(jnp.arange(rows, dtype=F32), GRID_W)
    col_idx = jnp.tile(jnp.arange(GRID_W, dtype=F32), rows)
    axis_dim = HEAD_DIM // 2
    inv = 1.0 / (ROPE_THETA ** (jnp.arange(0, axis_dim, 2, dtype=F32) / axis_dim))
    ang = jnp.concatenate([row_idx[:, None] * inv, col_idx[:, None] * inv], axis=-1)
    cos = jnp.repeat(jnp.cos(ang), 2, axis=-1)
    sin = jnp.repeat(jnp.sin(ang), 2, axis=-1) * jnp.tile(jnp.array([-1.0, 1.0], F32), HEAD_DIM // 2)
    return jnp.tile(cos, (1, LANES // HEAD_DIM)), jnp.tile(sin, (1, LANES // HEAD_DIM))


def _heads_to_tiles_t(a, n_heads, tile):
    t = a.shape[0]
    return a.reshape(t // tile, tile, n_heads, HEAD_DIM).transpose(2, 0, 3, 1)


def _heads_to_tiles(a, n_heads, tile):
    t = a.shape[0]
    return a.reshape(t // tile, tile, n_heads, HEAD_DIM).transpose(2, 0, 1, 3)


def _tiles_t_to_heads(a):
    h, n, _, tile = a.shape
    return a.transpose(1, 3, 0, 2).reshape(n * tile, h * HEAD_DIM)


def kernel(x, g_ffn1, w1_gate, w1_up, w1_down, g_mix, w_in, g_q, g_k, g_sgu, w_s, b_s, g_attn_out, g_sgu_out, w_out, g_ffn2, w2_gate, w2_up, w2_down, g_final, loss_target, m_g_ffn1, m_w1_gate, m_w1_up, m_w1_down, m_g_mix, m_w_in, m_g_q, m_g_k, m_g_sgu, m_w_s, m_b_s, m_g_attn_out, m_g_sgu_out, m_w_out, m_g_ffn2, m_w2_gate, m_w2_up, m_w2_down, m_g_final, v_g_ffn1, v_w1_gate, v_w1_up, v_w1_down, v_g_mix, v_w_in, v_g_q, v_g_k, v_g_sgu, v_w_s, v_b_s, v_g_attn_out, v_g_sgu_out, v_w_out, v_g_ffn2, v_w2_gate, v_w2_up, v_w2_down, v_g_final):
    weights = dict(g_ffn1=g_ffn1, w1_gate=w1_gate, w1_up=w1_up, w1_down=w1_down, g_mix=g_mix, w_in=w_in, g_q=g_q,
                   g_k=g_k, g_sgu=g_sgu, w_s=w_s, b_s=b_s, g_attn_out=g_attn_out, g_sgu_out=g_sgu_out, w_out=w_out,
                   g_ffn2=g_ffn2, w2_gate=w2_gate, w2_up=w2_up, w2_down=w2_down, g_final=g_final)
    m_in = dict(g_ffn1=m_g_ffn1, w1_gate=m_w1_gate, w1_up=m_w1_up, w1_down=m_w1_down, g_mix=m_g_mix, w_in=m_w_in,
                g_q=m_g_q, g_k=m_g_k, g_sgu=m_g_sgu, w_s=m_w_s, b_s=m_b_s, g_attn_out=m_g_attn_out,
                g_sgu_out=m_g_sgu_out, w_out=m_w_out, g_ffn2=m_g_ffn2, w2_gate=m_w2_gate, w2_up=m_w2_up,
                w2_down=m_w2_down, g_final=m_g_final)
    v_in = dict(g_ffn1=v_g_ffn1, w1_gate=v_w1_gate, w1_up=v_w1_up, w1_down=v_w1_down, g_mix=v_g_mix, w_in=v_w_in,
                g_q=v_g_q, g_k=v_g_k, g_sgu=v_g_sgu, w_s=v_w_s, b_s=v_b_s, g_attn_out=v_g_attn_out,
                g_sgu_out=v_g_sgu_out, w_out=v_w_out, g_ffn2=v_g_ffn2, w2_gate=v_w2_gate, w2_up=v_w2_up,
                w2_down=v_w2_down, g_final=v_g_final)
    names = list(weights)

    t = x.shape[1]
    x0 = x[0]
    target = loss_target[0]
    tm = min(256, t)
    tm_ff = min(256, t)
    tn_ff = 256
    tq = min(512, t)
    tk = min(256, t)
    tk_fwd = min(512, t)
    tk_w = min(2048, t)

    def shard_rows(name):
        w = weights[name][0]
        return (w.T if name in TRANSPOSED else w).astype(BF16)

    rows_of = dict(SHARD_ROWS)
    full = {}

    def packed(group):
        return jnp.concatenate([shard_rows(n) for n in group], axis=0), [rows_of[n] for n in group]

    def gather_of(group):
        return gather_exchange(*packed(group))

    def take(group, gathered):
        for n, g in zip(group, gathered):
            full[n] = g.reshape(N_DEV * rows_of[n], D_MODEL)

    first, second, third = ("w1_gate", "w1_up"), ("w1_down", "w_in", "w_out"), ("w2_gate", "w2_up", "w2_down")
    take(first, gather_two_level(*packed(first), "gather_first"))

    (h1, a1, b1, act1), gathered = ffn_up(x0, g_ffn1, full["w1_gate"], full["w1_up"], tm_ff, tn_ff, gather_of(second))
    take(second, gathered)
    w_in_t = full["w_in"]
    w_qkv_t, w_z_t = w_in_t[:D_QKV], w_in_t[D_QKV:]
    x1 = ffn_down(act1, full["w1_down"], x0, tm)

    qkv, z, h2 = input_projection(x1, g_mix, w_qkv_t, w_z_t, tm)
    cos_w, sin_w = _rope_tables(t)
    gq_w = jnp.tile(g_q, (1, N_HEADS))
    gk_w = jnp.tile(g_k, (1, N_KV_HEADS))
    mean_q, mean_k = _head_mean_matrix(D_ATTN).astype(BF16), _head_mean_matrix(D_KV).astype(BF16)
    q_t, k_tiles, kt_tiles, vt_tiles, vt_tiles_bwd, q_sq_max, k_sq_max = qk_prep(
        qkv, gq_w, gk_w, cos_w, sin_w, mean_q, mean_k, tq, tk, tk_fwd)
    score_bound = 1.02 * jnp.sqrt(q_sq_max.reshape(N_HEADS, HEAD_DIM)[:, 0]
                                  * jnp.repeat(k_sq_max.reshape(N_KV_HEADS, HEAD_DIM)[:, 0], KV_GROUP))
    k_tiles_fwd = k_tiles.reshape(N_KV_HEADS, t // tk_fwd, tk_fwd, HEAD_DIM)
    (o_t, q_tok), gathered = attention_fwd(score_bound, q_t, k_tiles_fwd, vt_tiles, gather_of(third))
    take(third, gathered)

    ws_b = w_s[0].astype(BF16)
    ws_tb = jnp.swapaxes(w_s[0], 1, 2).astype(BF16)
    bias = jnp.repeat(b_s[0].T, SGU_GROUP_DIM, axis=1)
    x2, mixed = mix_out(z, o_t, x1, g_sgu, g_attn_out, g_sgu_out, ws_tb, bias, full["w_out"], tq)

    (h3, a2, b2, act2), _ = ffn_up(x2, g_ffn2, full["w2_gate"], full["w2_up"], tm_ff, tn_ff)

    loss_part, dx3, dg_final = ffn_down_loss(act2, full["w2_down"], x2, g_final, target, tm)

    tmm = D_FF // 2
    (da2, db2), _ = ffn_bwd_act(dx3, full["w2_down"], a2, b2, tm_ff, tn_ff)
    (dx2, dg_ffn2), _ = norm_bwd_matmul(da2, full["w2_gate"], db2, full["w2_up"], x2, g_ffn2, dx3, tm)
    dwg2, dwu2 = matmul_tn(da2, h3, 1.0, tmm, tk_w), matmul_tn(db2, h3, 1.0, tmm, tk_w)
    dwd2 = matmul_tn(act2, dx3, 0.5, tmm, tk_w)

    group_ind = (jnp.arange(D_SGU)[:, None] // SGU_GROUP_DIM == jnp.arange(LANES)[None, :]).astype(F32)
    do_t, do_tok, dz, dg_mixrow, dws, dbs = mix_bwd(dx2, z, o_t, g_sgu, g_attn_out, g_sgu_out, ws_b, ws_tb, bias,
                                                    full["w_out"], group_ind, tq)
    dw_out = matmul_tn(mixed, dx2, 1.0, D_MODEL // 2, tk_w)

    group_a = ("w2_gate", "w2_up", "w2_down", "w_out")
    (dq_t, dk_t, dv_t), (parts_a,) = attention_bwd(q_tok, do_tok, q_t, do_t, kt_tiles, vt_tiles_bwd,
                                                   scatter_exchange([dwg2, dwu2, dwd2, dw_out]))
    dqkv, dgq_w, dgk_w = qk_bwd(dq_t, dk_t, dv_t, qkv, gq_w, gk_w, cos_w, sin_w, mean_q, mean_k, tq)

    def pack_small(arrays):
        pieces = []
        for a in arrays:
            flat = a.reshape(-1)
            pieces.append(jnp.pad(flat, (0, (-flat.shape[0]) % (8 * LANES))).reshape(-1, LANES))
        return jnp.concatenate(pieces, axis=0), [p.shape[0] for p in pieces]

    early = dict(g_ffn2=dg_ffn2, g_final=dg_final, g_q=dgq_w.reshape(N_HEADS, HEAD_DIM).sum(0),
                 g_k=dgk_w.reshape(N_KV_HEADS, HEAD_DIM).sum(0), g_attn_out=dg_mixrow[0], g_sgu_out=dg_mixrow[1],
                 g_sgu=dg_mixrow[2], w_s=dws, b_s=dbs[:, :N_SGU_GROUPS].T)
    early_pack, early_rows = pack_small(list(early.values()))
    (dx1, dg_mix), (early_parts,) = norm_bwd_matmul(dqkv, w_qkv_t, dz, w_z_t, x1, g_mix, dx2, tm,
                                                    gather_exchange(early_pack, [early_pack.shape[0]]))
    dw_in = jnp.concatenate([matmul_tn(dqkv, h2, 1.0, D_QKV // 2, tk_w), matmul_tn(dz, h2, 1.0, D_SGU, tk_w)], axis=0)

    dwd1, (parts_in,) = matmul_tn(act1, dx1, 0.5, tmm, tk_w, scatter_exchange([dw_in]))
    (da1, db1), (parts_d1,) = ffn_bwd_act(dx1, full["w1_down"], a1, b1, tm_ff, tn_ff, scatter_exchange([dwd1]))
    dwg1 = matmul_tn(da1, h1, 1.0, tmm, tk_w)
    dwu1, (parts_g1,) = matmul_tn(db1, h1, 1.0, tmm, tk_w, scatter_exchange([dwg1]))
    (dx0, dg_ffn1), (parts_u1,) = norm_bwd_matmul(da1, full["w1_gate"], db1, full["w1_up"], x0, g_ffn1, dx1, tm,
                                                  scatter_exchange([dwu1]))
    scattered = ((group_a, parts_a), (("w_in",), parts_in), (("w1_down",), parts_d1), (("w1_gate",), parts_g1),
                 (("w1_up",), parts_u1))

    late = dict(g_mix=dg_mix, g_ffn1=dg_ffn1, loss=loss_part)
    late_pack, late_rows = pack_small(list(late.values()))
    (late_parts,) = run_exchange(gather_exchange(late_pack, [late_pack.shape[0]]), "gather_late_small_grads")
    small_sums = {}
    for entries, rows, parts in ((early, early_rows, early_parts), (late, late_rows, late_parts)):
        summed = sum_parts(parts, parts.shape[1])
        off = 0
        for n, r in zip(entries, rows):
            small_sums[n] = summed[off:off + r]
            off += r
    loss = small_sums.pop("loss")[0, 0]

    grads, row_grads = {}, {}
    for group, parts in scattered:
        rows = parts.shape[1]
        summed = sum_parts(parts, rows if rows <= 2 * rows_of["w1_gate"] else rows // 2)
        off = 0
        for n in group:
            row_grads[n] = summed[off:off + rows_of[n]]
            grads[n] = (row_grads[n].T if n in TRANSPOSED else row_grads[n])[None]
            off += rows_of[n]
    for n, summed in small_sums.items():
        grads[n] = summed.reshape(-1)[:weights[n].size].reshape(weights[n].shape)

    delta_w, new_m, new_v = {}, {}, {}
    for n in names:
        shape = weights[n].shape
        if n in TRANSPOSED:
            view, unview, g = (lambda a: a[0].T), (lambda a: a.T[None]), row_grads[n]
        else:
            view, unview = (lambda a: a.reshape(-1, shape[-1])), (lambda a: a.reshape(shape))
            g = view(grads[n])
        d, m2, v2 = adamw(view(weights[n]), g, view(m_in[n]), view(v_in[n]))
        delta_w[n], new_m[n], new_v[n] = unview(d), unview(m2), unview(v2)

    return (loss, dx0[None], *[grads[n] for n in names], *[delta_w[n] for n in names],
            *[new_m[n] for n in names], *[new_v[n] for n in names])
```

```python
import functools
import math

import jax
import jax.numpy as jnp
from jax import lax
from jax.experimental import pallas as pl
from jax.experimental.pallas import tpu as pltpu

F32 = jnp.float32
BF16 = jnp.bfloat16

D_MODEL = 1024
D_FF = 2816
N_HEADS = 8
HEAD_DIM = 64
N_KV_HEADS = 2
KV_GROUP = N_HEADS // N_KV_HEADS
D_ATTN = N_HEADS * HEAD_DIM
D_KV = N_KV_HEADS * HEAD_DIM
D_QKV = D_ATTN + 2 * D_KV
N_SGU_GROUPS = 8
SGU_GROUP_DIM = 64
D_SGU = N_SGU_GROUPS * SGU_GROUP_DIM
CHUNK = 128
GRID_W = 64
ROPE_THETA = 10000.0
EPS = 1e-6
N_DEV = 8
LANES = 128

ONES_ROWS = 16
SAFE_SCORE_BOUND = 60.0
LOG2_E = math.log2(math.e)
Q_SCALE = HEAD_DIM ** -0.5 * LOG2_E

ADAM_LR = 0.001
ADAM_B1 = 0.9
ADAM_B2 = 0.999
ADAM_EPS = 1e-08
ADAM_WD = 0.01
ADAM_STEP = 10

MESH_IDS = pl.DeviceIdType.MESH

VMEM_LIMIT = 56 * 1024 * 1024

SHARD_ROWS = (("w1_gate", D_FF // N_DEV), ("w1_up", D_FF // N_DEV), ("w1_down", D_FF // N_DEV),
              ("w_in", (D_QKV + 2 * D_SGU) // N_DEV), ("w_out", D_MODEL // N_DEV),
              ("w2_gate", D_FF // N_DEV), ("w2_up", D_FF // N_DEV), ("w2_down", D_FF // N_DEV))
TRANSPOSED = ("w1_gate", "w1_up", "w_in", "w2_gate", "w2_up")


def _params(n_grid):
    return pltpu.CompilerParams(dimension_semantics=("arbitrary",) * n_grid, vmem_limit_bytes=VMEM_LIMIT)


def _dot(a, b):
    return jnp.dot(a, b, preferred_element_type=F32)


def _dot_nt(a, b):
    return lax.dot_general(a, b, (((1,), (1,)), ((), ())), preferred_element_type=F32)


def _dot_tn(a, b):
    return lax.dot_general(a, b, (((0,), (0,)), ((), ())), preferred_element_type=F32)


def _dot_f32(a, b):
    return jnp.dot(a, b, preferred_element_type=F32, precision=lax.Precision.HIGHEST)


def _dot_split(a, b):
    hi = a.astype(BF16)
    lo = (a - hi.astype(F32)).astype(BF16)
    return _dot(hi, b) + _dot(lo, b)


def _rstd(x):
    return lax.rsqrt(jnp.mean(x * x, axis=-1, keepdims=True) + EPS)


def _rms_bwd(dy, n, r, g):
    dn = dy * g
    return r * (dn - n * jnp.mean(dn * n, axis=-1, keepdims=True)), dy * n


def _colsum(a):
    return jnp.sum(a, axis=0, keepdims=True)


_GELU_C = math.sqrt(2.0 / math.pi)


def _gelu(x):
    t = jnp.tanh(_GELU_C * (x + 0.044715 * (x * x * x)))
    return x * (0.5 * (1.0 + t)), t


def _gelu_grad(x, t):
    return 0.5 * (1.0 + t) + 0.5 * x * (1.0 - t * t) * (_GELU_C * (1.0 + 3 * 0.044715 * x * x))


def _pair_swap(a):
    w = a.shape[-1]
    lane = lax.broadcasted_iota(jnp.int32, a.shape, a.ndim - 1)
    return jnp.where(lane % 2 == 0, pltpu.roll(a, w - 1, a.ndim - 1), pltpu.roll(a, 1, a.ndim - 1))


def _tile_lanes(a, reps):
    return jnp.concatenate([a] * reps, axis=-1) if reps > 1 else a


def _loop_pairs(n, step, carry):
    assert n % 2 == 0, n

    def pair(jj, c):
        return step(2 * jj + 1, 1, step(2 * jj, 0, c))

    return lax.fori_loop(0, n // 2, pair, carry)


def _full(shape):
    nd = len(shape)
    return pl.BlockSpec(shape, lambda *_: (0,) * nd)


def _mesh_pos():
    return lax.axis_index("x"), lax.axis_index("y"), lax.axis_index("c")


def _peer(pos, d):
    x, y, c = pos
    px = 1 - x if d & 4 else x
    py = 1 - y if d & 2 else y
    pc = 1 - c if d & 1 else c
    return (px, py, pc), 4 * px + 2 * py + pc


class _Exchange:
    def __init__(self, operands, out_shape, n_local, plan):
        self.operands = list(operands)
        self.out_shape = list(out_shape)
        self.sem_shapes = [pltpu.SemaphoreType.DMA((N_DEV - 1,)), pltpu.SemaphoreType.DMA((N_DEV - 1,)),
                           pltpu.SemaphoreType.DMA((n_local,))]
        self._plan = plan

    def _copies(self, in_refs, out_refs):
        pos = _mesh_pos()
        return pos, self._plan(4 * pos[0] + 2 * pos[1] + pos[2], in_refs, out_refs)

    def start(self, in_refs, out_refs, sems):
        send_sems, recv_sems, local_sems = sems
        pos, (local, remote, _) = self._copies(in_refs, out_refs)
        for k, (src, dst) in enumerate(local):
            pltpu.make_async_copy(src, dst, local_sems.at[k]).start()
        for d in range(1, N_DEV):
            peer, peer_lin = _peer(pos, d)
            for src, dst in remote(peer_lin):
                pltpu.make_async_remote_copy(src_ref=src, dst_ref=dst, send_sem=send_sems.at[d - 1],
                                             recv_sem=recv_sems.at[d - 1], device_id=peer,
                                             device_id_type=MESH_IDS).start()

    def wait(self, in_refs, out_refs, sems):
        send_sems, recv_sems, local_sems = sems
        pos, (local, _, whole) = self._copies(in_refs, out_refs)
        for d in range(1, N_DEV):
            peer, peer_lin = _peer(pos, d)
            ref = whole(peer_lin)
            everything = pltpu.make_async_remote_copy(src_ref=ref, dst_ref=ref, send_sem=send_sems.at[d - 1],
                                                      recv_sem=recv_sems.at[d - 1], device_id=peer,
                                                      device_id_type=MESH_IDS)
            everything.wait_send()
            everything.wait_recv()
        for k, (src, dst) in enumerate(local):
            pltpu.make_async_copy(src, dst, local_sems.at[k]).wait()


def _offsets(rows):
    offs, o = [], 0
    for r in rows:
        offs.append(o)
        o += r
    return offs


def gather_exchange(src, rows):
    offs = _offsets(rows)

    def plan(me, in_refs, out_refs):
        pieces = [(in_refs[0].at[pl.ds(o, r)], out.at[me]) for o, r, out in zip(offs, rows, out_refs)]
        return pieces, (lambda peer_lin: pieces), (lambda peer_lin: in_refs[0])

    return _Exchange([src], [jax.ShapeDtypeStruct((N_DEV, r) + src.shape[1:], src.dtype) for r in rows],
                     len(rows), plan)


def scatter_exchange(grads):
    rows = [g.shape[0] // N_DEV for g in grads]
    offs = _offsets(rows)

    def plan(me, in_refs, out_refs):
        parts = out_refs[0]

        def slabs(owner):
            return [(g.at[pl.ds(pl.multiple_of(owner * r, 16), r)], parts.at[me, pl.ds(o, r)])
                    for g, o, r in zip(in_refs, offs, rows)]

        return slabs(me), slabs, (lambda peer_lin: parts.at[peer_lin])

    shape = jax.ShapeDtypeStruct((N_DEV, sum(rows)) + grads[0].shape[1:], grads[0].dtype)
    return _Exchange(grads, [shape], len(rows), plan)


def gather_two_level(src, rows, name):
    offs = _offsets(rows)
    n_p = len(rows)

    def body(src_ref, *refs):
        outs, (send_sems, recv_sems, local_sems) = refs[:n_p], refs[n_p:]
        x, y, c = _mesh_pos()
        me, sibling = (x, y, c), (x, y, 1 - c)
        chips = [(1 - x, y), (x, 1 - y), (1 - x, 1 - y)]

        def slab(w, dev):
            return outs[w].at[4 * dev[0] + 2 * dev[1] + dev[2]]

        def copy(w, k, block, to, from_src=False):
            return pltpu.make_async_remote_copy(
                src_ref=src_ref.at[pl.ds(offs[w], rows[w])] if from_src else slab(w, block), dst_ref=slab(w, block),
                send_sem=send_sems.at[w * 7 + k], recv_sem=recv_sems.at[w * 7 + k],
                device_id=to, device_id_type=MESH_IDS)

        mine = [pltpu.make_async_copy(src_ref.at[pl.ds(offs[w], rows[w])], slab(w, me), local_sems.at[w])
                for w in range(n_p)]
        for cp in mine:
            cp.start()
        first = []
        for w in range(n_p):
            first.append(copy(w, 0, me, sibling, True))
            first += [copy(w, 1 + j, me, (*chip, c), True) for j, chip in enumerate(chips)]
        for cp in first:
            cp.start()
        passed = []
        for j, chip in enumerate(chips):
            for w in range(n_p):
                copy(w, 1 + j, (*chip, c), me).wait_recv()
                cp = copy(w, 4 + j, (*chip, c), sibling)
                cp.start()
                passed.append(cp)
        for w in range(n_p):
            copy(w, 0, sibling, me).wait_recv()
            for j, chip in enumerate(chips):
                copy(w, 4 + j, (*chip, 1 - c), me).wait_recv()
        for cp in first + passed:
            cp.wait_send()
        for cp in mine:
            cp.wait()

    any_spec = pl.BlockSpec(memory_space=pl.ANY)
    return pl.pallas_call(
        functools.partial(body), name=name,
        out_shape=[jax.ShapeDtypeStruct((N_DEV, r) + src.shape[1:], src.dtype) for r in rows],
        in_specs=[any_spec], out_specs=[any_spec] * n_p,
        scratch_shapes=[pltpu.SemaphoreType.DMA((7 * n_p,)), pltpu.SemaphoreType.DMA((7 * n_p,)),
                        pltpu.SemaphoreType.DMA((n_p,))],
        compiler_params=pltpu.CompilerParams(has_side_effects=True),
    )(src)


def run_exchange(ex, name):
    n_in, n_out = len(ex.operands), len(ex.out_shape)

    def body(*refs):
        parts = refs[:n_in], refs[n_in:n_in + n_out], refs[n_in + n_out:]
        ex.start(*parts)
        ex.wait(*parts)

    any_spec = pl.BlockSpec(memory_space=pl.ANY)
    return pl.pallas_call(
        functools.partial(body), name=name, out_shape=ex.out_shape,
        in_specs=[any_spec] * n_in, out_specs=[any_spec] * n_out, scratch_shapes=ex.sem_shapes,
        compiler_params=pltpu.CompilerParams(has_side_effects=True),
    )(*ex.operands)


def _pallas(comm, body, *, name, grid, in_specs, out_specs, out_shape, args, scratch_shapes=()):
    params = _params(len(grid))
    if comm is None:
        res = pl.pallas_call(functools.partial(body), name=name, grid=grid, in_specs=list(in_specs),
                             out_specs=list(out_specs), out_shape=list(out_shape),
                             scratch_shapes=list(scratch_shapes), compiler_params=params)(*args)
        return list(res), []
    n_in, n_out, n_scr = len(in_specs), len(out_specs), len(scratch_shapes)
    c_in, c_out = len(comm.operands), len(comm.out_shape)

    def edge(last):
        conds = [pl.program_id(a) == (g - 1 if last else 0) for a, g in enumerate(grid)]
        return functools.reduce(jnp.logical_and, conds)

    def wrapped(*refs):
        refs = list(refs)
        ins, refs = refs[:n_in], refs[n_in:]
        cins, refs = refs[:c_in], refs[c_in:]
        outs, refs = refs[:n_out], refs[n_out:]
        couts, refs = refs[:c_out], refs[c_out:]
        scr, sems = refs[:n_scr], refs[n_scr:]

        @pl.when(edge(False))
        def _():
            comm.start(cins, couts, sems)

        body(*ins, *outs, *scr)

        @pl.when(edge(True))
        def _():
            comm.wait(cins, couts, sems)

    any_spec = pl.BlockSpec(memory_space=pl.ANY)
    res = pl.pallas_call(
        wrapped, name=name, grid=grid,
        in_specs=list(in_specs) + [any_spec] * c_in, out_specs=list(out_specs) + [any_spec] * c_out,
        out_shape=list(out_shape) + comm.out_shape, scratch_shapes=list(scratch_shapes) + comm.sem_shapes,
        compiler_params=pltpu.CompilerParams(dimension_semantics=("arbitrary",) * len(grid),
                                             vmem_limit_bytes=VMEM_LIMIT, has_side_effects=True),
    )(*args, *comm.operands)
    return res[:n_out], res[n_out:]


def sum_parts(parts, block_rows):
    n, rows, cols = parts.shape

    def body(p_ref, o_ref):
        acc = p_ref[0].astype(F32)
        for s in range(1, n):
            acc = acc + p_ref[s].astype(F32)
        o_ref[...] = acc

    return pl.pallas_call(
        functools.partial(body), name="sum_parts",
        grid=(rows // block_rows,),
        in_specs=[pl.BlockSpec((n, block_rows, cols), lambda i: (0, i, 0))],
        out_specs=pl.BlockSpec((block_rows, cols), lambda i: (i, 0)),
        out_shape=jax.ShapeDtypeStruct((rows, cols), F32),
        compiler_params=_params(1),
    )(parts)


def adamw(w, g, m, v):
    def body(w_ref, g_ref, m_ref, v_ref, d_ref, m_out, v_out):
        gg = g_ref[...]
        m2 = ADAM_B1 * m_ref[...] + (1.0 - ADAM_B1) * gg
        v2 = ADAM_B2 * v_ref[...] + (1.0 - ADAM_B2) * (gg * gg)
        m_hat = m2 / (1.0 - ADAM_B1 ** ADAM_STEP)
        v_hat = v2 / (1.0 - ADAM_B2 ** ADAM_STEP)
        d_ref[...] = -ADAM_LR * (m_hat / (jnp.sqrt(v_hat) + ADAM_EPS) + ADAM_WD * w_ref[...])
        m_out[...] = m2
        v_out[...] = v2

    spec = _full(w.shape)
    shape = jax.ShapeDtypeStruct(w.shape, F32)
    return pl.pallas_call(
        functools.partial(body), name="adamw",
        in_specs=[spec] * 4, out_specs=[spec] * 3, out_shape=[shape] * 3,
        compiler_params=pltpu.CompilerParams(vmem_limit_bytes=VMEM_LIMIT),
    )(w, g, m, v)


def ffn_up(x, g, wg_t, wu_t, tm, tn, comm=None):
    t = x.shape[0]

    def body(x_ref, g_ref, wg_ref, wu_ref, h_ref, silu_ref, dgate_ref, act_ref):
        xx = x_ref[...]
        h = ((xx * _rstd(xx)) * g_ref[...]).astype(BF16)
        h_ref[...] = h
        for c in range(D_FF // tn):
            cols = slice(c * tn, (c + 1) * tn)
            a = _dot_nt(h, wg_ref[cols, :])
            b = _dot_nt(h, wu_ref[cols, :])
            sig = 0.5 * jnp.tanh(0.5 * a) + 0.5
            silu = a * sig
            silu_ref[:, cols] = silu.astype(BF16)
            dgate_ref[:, cols] = (b * (sig + silu * (1.0 - sig))).astype(BF16)
            act_ref[:, cols] = (silu * b).astype(BF16)

    wide = jax.ShapeDtypeStruct((t, D_FF), BF16)
    row = lambda n: pl.BlockSpec((tm, n), lambda i: (i, 0))
    return _pallas(
        comm, body, name="ffn_up",
        grid=(t // tm,),
        in_specs=[row(D_MODEL), _full((1, D_MODEL)), _full((D_FF, D_MODEL)), _full((D_FF, D_MODEL))],
        out_specs=[row(D_MODEL), row(D_FF), row(D_FF), row(D_FF)],
        out_shape=[jax.ShapeDtypeStruct((t, D_MODEL), BF16), wide, wide, wide],
        args=(x, g, wg_t, wu_t))


def ffn_down(act, wd, x, tm):
    t = x.shape[0]

    def body(act_ref, wd_ref, x_ref, o_ref):
        o_ref[...] = x_ref[...] + 0.5 * _dot(act_ref[...], wd_ref[...])

    return pl.pallas_call(
        functools.partial(body), name="ffn_down",
        grid=(t // tm,),
        in_specs=[pl.BlockSpec((tm, D_FF), lambda i: (i, 0)), _full((D_FF, D_MODEL)),
                  pl.BlockSpec((tm, D_MODEL), lambda i: (i, 0))],
        out_specs=pl.BlockSpec((tm, D_MODEL), lambda i: (i, 0)),
        out_shape=jax.ShapeDtypeStruct((t, D_MODEL), F32),
        compiler_params=_params(1),
    )(act, wd, x)


def ffn_bwd_act(dx, wd, silu, dgate, tm, tn, comm=None):
    t = dx.shape[0]

    def body(dx_ref, wd_ref, silu_ref, dgate_ref, da_ref, db_ref):
        dxb = (0.5 * dx_ref[...]).astype(BF16)
        for c in range(D_FF // tn):
            cols = slice(c * tn, (c + 1) * tn)
            dact = _dot_nt(dxb, wd_ref[cols, :])
            da_ref[:, cols] = (dact * dgate_ref[:, cols].astype(F32)).astype(BF16)
            db_ref[:, cols] = (dact * silu_ref[:, cols].astype(F32)).astype(BF16)

    wide = jax.ShapeDtypeStruct((t, D_FF), BF16)
    row = lambda n: pl.BlockSpec((tm, n), lambda i: (i, 0))
    return _pallas(
        comm, body, name="ffn_bwd_act",
        grid=(t // tm,),
        in_specs=[row(D_MODEL), _full((D_FF, D_MODEL)), row(D_FF), row(D_FF)],
        out_specs=[row(D_FF), row(D_FF)],
        out_shape=[wide, wide],
        args=(dx, wd, silu, dgate))


def norm_bwd_matmul(a1, w1, a2, w2, x, g, dx_in, tm, comm=None):
    t = x.shape[0]
    k1, k2 = a1.shape[1], a2.shape[1]

    def body(a1_ref, w1_ref, a2_ref, w2_ref, x_ref, g_ref, dxin_ref, dx_ref, dg_ref):
        dh = _dot(a1_ref[...], w1_ref[...]) + _dot(a2_ref[...], w2_ref[...])
        xx = x_ref[...]
        r = _rstd(xx)
        dx, dg_rows = _rms_bwd(dh, xx * r, r, g_ref[...])
        dx_ref[...] = dxin_ref[...] + dx

        @pl.when(pl.program_id(0) == 0)
        def _():
            dg_ref[...] = jnp.zeros_like(dg_ref)

        dg_ref[...] += _colsum(dg_rows)

    row = pl.BlockSpec((tm, D_MODEL), lambda i: (i, 0))
    return _pallas(
        comm, body, name="norm_bwd_matmul",
        grid=(t // tm,),
        in_specs=[pl.BlockSpec((tm, k1), lambda i: (i, 0)), _full((k1, D_MODEL)),
                  pl.BlockSpec((tm, k2), lambda i: (i, 0)), _full((k2, D_MODEL)),
                  row, _full((1, D_MODEL)), row],
        out_specs=[row, _full((1, D_MODEL))],
        out_shape=[jax.ShapeDtypeStruct((t, D_MODEL), F32), jax.ShapeDtypeStruct((1, D_MODEL), F32)],
        args=(a1, w1, a2, w2, x, g, dx_in))


def matmul_tn(a, b, scale, tmm, tk, comm=None):
    t, m = a.shape
    n = b.shape[1]
    nk = t // tk

    def body(a_ref, b_ref, o_ref, acc_ref):
        k = pl.program_id(1)

        @pl.when(k == 0)
        def _():
            acc_ref[...] = jnp.zeros_like(acc_ref)

        acc_ref[...] += _dot_tn(a_ref[...].astype(BF16), b_ref[...].astype(BF16))

        @pl.when(k == nk - 1)
        def _():
            o_ref[...] = (scale * acc_ref[...]).astype(BF16)

    (out,), comm_outs = _pallas(
        comm, body, name="matmul_tn",
        grid=(m // tmm, nk),
        in_specs=[pl.BlockSpec((tk, tmm), lambda i, k: (k, i)), pl.BlockSpec((tk, n), lambda i, k: (k, 0))],
        out_specs=[pl.BlockSpec((tmm, n), lambda i, k: (i, 0))],
        out_shape=[jax.ShapeDtypeStruct((m, n), BF16)],
        scratch_shapes=[pltpu.VMEM((tmm, n), F32)],
        args=(a, b))
    return out if comm is None else (out, comm_outs)


def input_projection(x, g, w_qkv_t, w_z_t, tm):
    t = x.shape[0]

    def body(x_ref, g_ref, wq_ref, wz_ref, qkv_ref, z_ref, h_ref):
        xx = x_ref[...]
        h = ((xx * _rstd(xx)) * g_ref[...]).astype(BF16)
        h_ref[...] = h
        qkv_ref[...] = _dot_nt(h, wq_ref[...])
        z_ref[...] = _dot_nt(h, wz_ref[...])

    row = lambda n: pl.BlockSpec((tm, n), lambda i: (i, 0))
    return pl.pallas_call(
        functools.partial(body), name="input_projection", grid=(t // tm,),
        in_specs=[row(D_MODEL), _full((1, D_MODEL)), _full((D_QKV, D_MODEL)), _full((2 * D_SGU, D_MODEL))],
        out_specs=[row(D_QKV), row(2 * D_SGU), row(D_MODEL)],
        out_shape=[jax.ShapeDtypeStruct((t, D_QKV), F32), jax.ShapeDtypeStruct((t, 2 * D_SGU), F32),
                   jax.ShapeDtypeStruct((t, D_MODEL), BF16)],
        compiler_params=_params(1))(x, g, w_qkv_t, w_z_t)


def _shift_rows(shape, first, second):
    row = lax.broadcasted_iota(jnp.int32, shape, len(shape) - 2)
    return jnp.where(row == 0, first, jnp.where(row == 1, second, 0.0))


def _hi_lo(a):
    hi = a.astype(BF16).astype(F32)
    return hi, a - hi


def _head_tile_spec(tm, rows):
    return pl.BlockSpec((N_HEADS, None, rows, tm), lambda i: (0, i, 0, 0))


def _to_head_tiles(a):
    return a.T.reshape(N_HEADS, HEAD_DIM, a.shape[0])


def _from_head_tiles(a):
    return a.reshape(D_ATTN, a.shape[-1]).T


def _head_mean_matrix(width):
    head = jnp.arange(width) // HEAD_DIM
    return (head[:, None] == head[None, :]).astype(F32) / HEAD_DIM


def _kv_tile_spec(n_sub, rows, cols):
    return pl.BlockSpec((N_KV_HEADS, n_sub, rows, cols), lambda i: (0, i, 0, 0))


def qk_prep(qkv, gq_w, gk_w, cos_w, sin_w, mean_q, mean_k, tm, tk, tk_v):
    t = qkv.shape[0]
    n_sub, n_sub_v = tm // tk, tm // tk_v

    def body(p_ref, gq_ref, gk_ref, cos_ref, sin_ref, mq_ref, mk_ref, q_ref, k_ref, kt_ref, vt_ref, vtb_ref,
             qmax_ref, kmax_ref):
        @pl.when(pl.program_id(0) == 0)
        def _():
            qmax_ref[...] = jnp.zeros_like(qmax_ref)
            kmax_ref[...] = jnp.zeros_like(kmax_ref)

        cos2, sin2 = cos_ref[...], sin_ref[...]
        q = p_ref[:, :D_ATTN]
        k = p_ref[:, D_ATTN:D_ATTN + D_KV]
        qn = q * lax.rsqrt(_dot_split(q * q, mq_ref[...]) + EPS) * gq_ref[...]
        kn = k * lax.rsqrt(_dot_split(k * k, mk_ref[...]) + EPS) * gk_ref[...]
        cos8, sin8 = _tile_lanes(cos2, D_ATTN // LANES), _tile_lanes(sin2, D_ATTN // LANES)
        q_rot = (qn * cos8 + _pair_swap(qn) * sin8) * Q_SCALE
        q_ref[...] = _to_head_tiles(q_rot).astype(BF16)
        k_rot = kn * cos2 + _pair_swap(kn) * sin2
        q_sq = HEAD_DIM * _dot_split(q_rot * q_rot, mq_ref[...])
        k_sq = HEAD_DIM * _dot_split(k_rot * k_rot, mk_ref[...])
        qmax_ref[...] = jnp.maximum(qmax_ref[...], jnp.max(q_sq, axis=0, keepdims=True))
        kmax_ref[...] = jnp.maximum(kmax_ref[...], jnp.max(k_sq, axis=0, keepdims=True))
        vv = p_ref[:, D_ATTN + D_KV:]
        second = pltpu.roll(k_rot, HEAD_DIM, 1)
        for c in range(n_sub):
            rows = slice(c * tk, (c + 1) * tk)
            k_ref[0, c] = k_rot[rows, :HEAD_DIM].astype(BF16)
            k_ref[1, c] = second[rows, :HEAD_DIM].astype(BF16)
        for a, feat_ref, width, n in ((k_rot, kt_ref, tk, n_sub), (vv, vtb_ref, tk, n_sub), (vv, vt_ref, tk_v, n_sub_v)):
            for c in range(n):
                tile = a[c * width:(c + 1) * width].T.reshape(N_KV_HEADS, HEAD_DIM, width)
                feat_ref[:, c, :HEAD_DIM, :] = tile.astype(BF16)
        vt_ref[:, :, HEAD_DIM:, :] = jnp.ones((N_KV_HEADS, n_sub_v, ONES_ROWS, tk_v), BF16)
        minus = _shift_rows((N_KV_HEADS, n_sub, HEAD_DIM, tk), -1.0, -1.0).astype(BF16)
        kt_ref[:, :, HEAD_DIM:, :] = minus
        vtb_ref[:, :, HEAD_DIM:, :] = minus

    kv = lambda rows, cols: jax.ShapeDtypeStruct((N_KV_HEADS, t // tk, rows, cols), BF16)
    return pl.pallas_call(
        functools.partial(body), name="qk_prep", grid=(t // tm,),
        in_specs=[pl.BlockSpec((tm, D_QKV), lambda i: (i, 0)), _full((1, D_ATTN)), _full((1, D_KV)),
                  pl.BlockSpec((tm, LANES), lambda i: (i, 0)), pl.BlockSpec((tm, LANES), lambda i: (i, 0)),
                  _full((D_ATTN, D_ATTN)), _full((D_KV, D_KV))],
        out_specs=[_head_tile_spec(tm, HEAD_DIM), _kv_tile_spec(n_sub, tk, HEAD_DIM),
                   _kv_tile_spec(n_sub, 2 * HEAD_DIM, tk),
                   _kv_tile_spec(n_sub_v, HEAD_DIM + ONES_ROWS, tk_v), _kv_tile_spec(n_sub, 2 * HEAD_DIM, tk),
                   _full((1, D_ATTN)), _full((1, D_KV))],
        out_shape=[jax.ShapeDtypeStruct((N_HEADS, t // tm, HEAD_DIM, tm), BF16), kv(tk, HEAD_DIM), kv(2 * HEAD_DIM, tk),
                   jax.ShapeDtypeStruct((N_KV_HEADS, t // tk_v, HEAD_DIM + ONES_ROWS, tk_v), BF16),
                   kv(2 * HEAD_DIM, tk),
                   jax.ShapeDtypeStruct((1, D_ATTN), F32), jax.ShapeDtypeStruct((1, D_KV), F32)],
        compiler_params=_params(1),
    )(qkv, gq_w, gk_w, cos_w, sin_w, mean_q, mean_k)


def qk_bwd(dq_rot, dk_rot, dv, qkv, gq_w, gk_w, cos_w, sin_w, mean_q, mean_k, tm):
    t = qkv.shape[0]
    tk = dk_rot.shape[-1]
    n_sub = tm // tk

    def token_major(ref):
        return jnp.concatenate([ref[:, c].reshape(D_KV, tk).T for c in range(n_sub)], axis=0)

    def branch(raw, d_rot, gain, mean_mat, cos, sin, scale):
        r = lax.rsqrt(_dot_split(raw * raw, mean_mat) + EPS)
        n = raw * r
        dy = (d_rot * cos - _pair_swap(d_rot) * sin) * scale
        dn = dy * gain
        return r * (dn - n * _dot_split(dn * n, mean_mat)), dy * n

    def body(dq_ref, dk_ref, dv_ref, p_ref, gq_ref, gk_ref, cos_ref, sin_ref, mq_ref, mk_ref,
             dp_ref, dgq_ref, dgk_ref):
        cos2, sin2 = cos_ref[...], sin_ref[...]
        cos8, sin8 = _tile_lanes(cos2, D_ATTN // LANES), _tile_lanes(sin2, D_ATTN // LANES)
        dq, dgq = branch(p_ref[:, :D_ATTN], _from_head_tiles(dq_ref[...]), gq_ref[...], mq_ref[...], cos8, sin8,
                         HEAD_DIM ** -0.5)
        dk, dgk = branch(p_ref[:, D_ATTN:D_ATTN + D_KV], token_major(dk_ref), gk_ref[...], mk_ref[...], cos2, sin2, 1.0)
        dp_ref[...] = jnp.concatenate([dq, dk, token_major(dv_ref)], axis=-1).astype(BF16)

        @pl.when(pl.program_id(0) == 0)
        def _():
            dgq_ref[...] = jnp.zeros_like(dgq_ref)
            dgk_ref[...] = jnp.zeros_like(dgk_ref)

        dgq_ref[...] += _colsum(dgq)
        dgk_ref[...] += _colsum(dgk)

    return pl.pallas_call(
        functools.partial(body), name="qk_bwd", grid=(t // tm,),
        in_specs=[_head_tile_spec(tm, HEAD_DIM), _kv_tile_spec(n_sub, HEAD_DIM, tk),
                  _kv_tile_spec(n_sub, HEAD_DIM, tk), pl.BlockSpec((tm, D_QKV), lambda i: (i, 0)),
                  _full((1, D_ATTN)), _full((1, D_KV)),
                  pl.BlockSpec((tm, LANES), lambda i: (i, 0)), pl.BlockSpec((tm, LANES), lambda i: (i, 0)),
                  _full((D_ATTN, D_ATTN)), _full((D_KV, D_KV))],
        out_specs=[pl.BlockSpec((tm, D_QKV), lambda i: (i, 0)), _full((1, D_ATTN)), _full((1, D_KV))],
        out_shape=[jax.ShapeDtypeStruct((t, D_QKV), BF16), jax.ShapeDtypeStruct((1, D_ATTN), F32),
                   jax.ShapeDtypeStruct((1, D_KV), F32)],
        compiler_params=_params(1),
    )(dq_rot, dk_rot, dv, qkv, gq_w, gk_w, cos_w, sin_w, mean_q, mean_k)


def attention_fwd(bound, q_t, k, v_t, comm=None):
    _, nq, _, tq = q_t.shape
    _, nk, tk, _ = k.shape

    def body(bound_ref, q_ref, k_ref, v_ref, o_ref, qtok_ref, s_scr, p_scr):
        head_bound = bound_ref[pl.program_id(0)]
        safe = head_bound <= SAFE_SCORE_BOUND
        q = q_ref[...]
        s_scr[0] = _dot(k_ref[0], q)
        p_scr[1] = jnp.zeros((tk, tq), BF16)
        zero = jnp.zeros((HEAD_DIM + ONES_ROWS, tq), F32)

        def matmuls(j, slot):
            pv = _dot(v_ref[jnp.maximum(j - 1, 0)], p_scr[1 - slot])
            s_scr[1 - slot] = _dot(k_ref[jnp.minimum(j + 1, nk - 1)], q)
            return pv

        def finish(m, acc):
            acc = acc + _dot(v_ref[nk - 1], p_scr[(nk - 1) % 2])
            l = acc[HEAD_DIM:HEAD_DIM + 1]
            o_ref[...] = acc[:HEAD_DIM] / l
            lse_rows = _shift_rows((HEAD_DIM, tq), *_hi_lo(m + jnp.log2(l)))
            qtok_ref[...] = jnp.concatenate([q.astype(F32), lse_rows], axis=0).T.astype(BF16)

        @pl.when(safe)
        def _():
            m = jnp.full((1, tq), head_bound, F32)

            def step(j, slot, acc):
                s = s_scr[slot]
                pv = matmuls(j, slot)
                p_scr[slot] = jnp.exp2(s - m).astype(BF16)
                return acc + pv

            finish(m, _loop_pairs(nk, step, zero))

        @pl.when(jnp.logical_not(safe))
        def _():
            def step(j, slot, carry):
                m, acc = carry
                s = s_scr[slot]
                pv = matmuls(j, slot)
                m_new = jnp.maximum(m, jnp.max(s, axis=0, keepdims=True))
                p_scr[slot] = jnp.exp2(s - m_new).astype(BF16)
                return m_new, jnp.exp2(m - m_new) * (acc + pv)

            finish(*_loop_pairs(nk, step, (jnp.full((1, tq), -1e30, F32), zero)))

    return _pallas(
        comm, body, name="attention_fwd", grid=(N_HEADS, nq),
        in_specs=[pl.BlockSpec(memory_space=pltpu.SMEM),
                  pl.BlockSpec((None, None, HEAD_DIM, tq), lambda h, i: (h, i, 0, 0)),
                  pl.BlockSpec((None, nk, tk, HEAD_DIM), lambda h, i: (h // KV_GROUP, 0, 0, 0)),
                  pl.BlockSpec((None, nk, HEAD_DIM + ONES_ROWS, tk), lambda h, i: (h // KV_GROUP, 0, 0, 0))],
        out_specs=[pl.BlockSpec((None, None, HEAD_DIM, tq), lambda h, i: (h, i, 0, 0)),
                   pl.BlockSpec((None, None, tq, 2 * HEAD_DIM), lambda h, i: (h, i, 0, 0))],
        out_shape=[jax.ShapeDtypeStruct((N_HEADS, nq, HEAD_DIM, tq), F32),
                   jax.ShapeDtypeStruct((N_HEADS, nq, tq, 2 * HEAD_DIM), BF16)],
        scratch_shapes=[pltpu.VMEM((2, tk, tq), F32), pltpu.VMEM((2, tk, tq), BF16)],
        args=(bound, q_t, k, v_t))


def attention_bwd(q_tok, do_tok, q_t, do_t, k_t, v_t, comm=None):
    _, nq, _, tq = q_t.shape
    _, nk, _, tk = k_t.shape

    def body(qtok_ref, dotok_ref, q_ref, do_ref, kt_ref, vt_ref, dq_ref, dk_ref, dv_ref,
             s_scr, dp_scr, p_scr, ds_scr):
        @pl.when(pl.program_id(1) == 0)
        def _():
            dq_ref[...] = jnp.zeros_like(dq_ref)

        kt_aug, vt_aug = kt_ref[...], vt_ref[...]
        kt = kt_aug[:HEAD_DIM]
        n = KV_GROUP * nq
        s_scr[0] = _dot(qtok_ref[0, 0], kt_aug)
        dp_scr[0] = _dot(dotok_ref[0, 0], vt_aug)
        p_scr[1] = jnp.zeros((tq, tk), BF16)
        ds_scr[1] = jnp.zeros((tq, tk), BF16)

        def products(t, slot, dk, dv):
            h, i = t // nq, t % nq
            ds = ds_scr[slot]
            dq_ref[h, i] += _dot_nt(kt, ds)
            return dk + _dot(q_ref[h, i], ds), dv + _dot(do_ref[h, i], p_scr[slot])

        def step(t, slot, carry):
            s, dp = s_scr[slot], dp_scr[slot]
            dk, dv = products(jnp.maximum(t - 1, 0), 1 - slot, *carry)
            nxt = jnp.minimum(t + 1, n - 1)
            s_scr[1 - slot] = _dot(qtok_ref[nxt // nq, nxt % nq], kt_aug)
            dp_scr[1 - slot] = _dot(dotok_ref[nxt // nq, nxt % nq], vt_aug)
            p = jnp.exp2(s)
            p_scr[slot] = p.astype(BF16)
            ds_scr[slot] = (p * dp).astype(BF16)
            return dk, dv

        zero = jnp.zeros((HEAD_DIM, tk), F32)
        dk, dv = products(n - 1, (n - 1) % 2, *_loop_pairs(n, step, (zero, zero)))
        dk_ref[...] = dk * (1.0 / LOG2_E)
        dv_ref[...] = dv

    group = lambda g, j: (g, 0, 0, 0)
    tile = lambda g, j: (g, j, 0, 0)
    once = pl.Buffered(1)
    return _pallas(
        comm, body, name="attention_bwd", grid=(N_KV_HEADS, nk),
        in_specs=[pl.BlockSpec((KV_GROUP, nq, tq, 2 * HEAD_DIM), group, pipeline_mode=once),
                  pl.BlockSpec((KV_GROUP, nq, tq, 2 * HEAD_DIM), group, pipeline_mode=once),
                  pl.BlockSpec((KV_GROUP, nq, HEAD_DIM, tq), group, pipeline_mode=once),
                  pl.BlockSpec((KV_GROUP, nq, HEAD_DIM, tq), group, pipeline_mode=once),
                  pl.BlockSpec((None, None, 2 * HEAD_DIM, tk), tile),
                  pl.BlockSpec((None, None, 2 * HEAD_DIM, tk), tile)],
        out_specs=[pl.BlockSpec((KV_GROUP, nq, HEAD_DIM, tq), group),
                   pl.BlockSpec((None, None, HEAD_DIM, tk), tile),
                   pl.BlockSpec((None, None, HEAD_DIM, tk), tile)],
        out_shape=[jax.ShapeDtypeStruct((N_HEADS, nq, HEAD_DIM, tq), F32),
                   jax.ShapeDtypeStruct((N_KV_HEADS, nk, HEAD_DIM, tk), F32),
                   jax.ShapeDtypeStruct((N_KV_HEADS, nk, HEAD_DIM, tk), F32)],
        scratch_shapes=[pltpu.VMEM((2, tq, tk), F32), pltpu.VMEM((2, tq, tk), F32),
                        pltpu.VMEM((2, tq, tk), BF16), pltpu.VMEM((2, tq, tk), BF16)],
        args=(q_tok, do_tok, q_t, do_t, k_t, v_t))


def _group_matmul(a_t, w_ref):
    return jnp.concatenate([_dot(a_t[g * SGU_GROUP_DIM:(g + 1) * SGU_GROUP_DIM], w_ref[g])
                            for g in range(N_SGU_GROUPS)], axis=0)


def _gate_forward(z, g_sgu, wst_ref, bias):
    gz, th = _gelu(z)
    u, vv = gz[:, :D_SGU], gz[:, D_SGU:]
    rv = _rstd(vv)
    nv = vv * rv
    vn = nv * g_sgu
    v_chunks, fs = [], []
    for c in range(z.shape[0] // CHUNK):
        vt = vn[c * CHUNK:(c + 1) * CHUNK].T.astype(BF16)
        v_chunks.append(vt)
        fs.append(_group_matmul(vt, wst_ref).T + bias)
    f = jnp.concatenate(fs, axis=0) if len(fs) > 1 else fs[0]
    return th, u, rv, nv, v_chunks, f


def mix_out(z, o, x, g_sgu, g_ao, g_so, ws_t, bias, w_out, tm):
    t = x.shape[0]

    def body(z_ref, o_ref, x_ref, gs_ref, gao_ref, gso_ref, ws_ref, bias_ref, wout_ref, x2_ref, mixed_ref):
        _, u, _, _, _, f = _gate_forward(z_ref[...], gs_ref[...], ws_ref, bias_ref[...])
        sgu = u * f
        oo = _from_head_tiles(o_ref[...])
        mixed = jnp.concatenate([oo * _rstd(oo) * gao_ref[...], sgu * _rstd(sgu) * gso_ref[...]], axis=-1).astype(BF16)
        mixed_ref[...] = mixed
        x2_ref[...] = x_ref[...] + _dot(mixed, wout_ref[...])

    row = lambda n: pl.BlockSpec((tm, n), lambda i: (i, 0))
    return pl.pallas_call(
        functools.partial(body), name="mix_out", grid=(t // tm,),
        in_specs=[row(2 * D_SGU), _head_tile_spec(tm, HEAD_DIM), row(D_MODEL), _full((1, D_SGU)), _full((1, D_ATTN)),
                  _full((1, D_SGU)),
                  _full((N_SGU_GROUPS, CHUNK, CHUNK)), _full((CHUNK, D_SGU)), _full((D_MODEL, D_MODEL))],
        out_specs=[row(D_MODEL), row(D_MODEL)],
        out_shape=[jax.ShapeDtypeStruct((t, D_MODEL), F32), jax.ShapeDtypeStruct((t, D_MODEL), BF16)],
        compiler_params=_params(1),
    )(z, o, x, g_sgu, g_ao, g_so, ws_t, bias, w_out)


def mix_bwd(dx2, z, o, g_sgu, g_ao, g_so, ws, ws_t, bias, w_out, group_ind, tm):
    t = dx2.shape[0]
    n_tiles = t // tm

    def body(dx_ref, z_ref, o_ref, gs_ref, gao_ref, gso_ref, ws_ref, wst_ref, bias_ref, wout_ref, ind_ref,
             do_ref, dotok_ref, dz_ref, dg_ref, dws_ref, dbs_ref, df_sum):
        step = pl.program_id(0)

        @pl.when(step == 0)
        def _():
            dg_ref[...] = jnp.zeros_like(dg_ref)
            dws_ref[...] = jnp.zeros_like(dws_ref)
            df_sum[...] = jnp.zeros_like(df_sum)

        z = z_ref[...]
        th, u, rv, nv, v_chunks, f = _gate_forward(z, gs_ref[...], wst_ref, bias_ref[...])
        dmixed = _dot_nt(dx_ref[...].astype(BF16), wout_ref[...])
        o_tiles = o_ref[...]
        oo = _from_head_tiles(o_tiles)
        ro = _rstd(oo)
        d_o, dgao = _rms_bwd(dmixed[:, :D_ATTN], oo * ro, ro, gao_ref[...])
        do_tiles = _to_head_tiles(d_o)
        do_ref[...] = do_tiles.astype(BF16)
        delta_hi, delta_lo = _hi_lo(jnp.sum(do_tiles * o_tiles, axis=1, keepdims=True))
        for h in range(N_HEADS):
            delta_rows = _shift_rows((HEAD_DIM, tm), delta_hi[h], delta_lo[h])
            dotok_ref[h] = jnp.concatenate([do_tiles[h], delta_rows], axis=0).T.astype(BF16)
        sgu = u * f
        rs = _rstd(sgu)
        dsgu, dgso = _rms_bwd(dmixed[:, D_ATTN:], sgu * rs, rs, gso_ref[...])
        du = dsgu * f
        df = dsgu * u
        dvns = []
        df_acc = jnp.zeros((CHUNK, D_SGU), F32)
        for c in range(tm // CHUNK):
            dfc32 = df[c * CHUNK:(c + 1) * CHUNK]
            dft = dfc32.T.astype(BF16)
            dvns.append(_group_matmul(dft, ws_ref).T)
            for g in range(N_SGU_GROUPS):
                rows = slice(g * SGU_GROUP_DIM, (g + 1) * SGU_GROUP_DIM)
                dws_ref[g] += _dot_tn(dft[rows], v_chunks[c][rows])
            df_acc = df_acc + dfc32
        df_sum[...] += df_acc
        dvn = jnp.concatenate(dvns, axis=0) if len(dvns) > 1 else dvns[0]
        dvv, dgs = _rms_bwd(dvn, nv, rv, gs_ref[...])
        dz_ref[...] = (jnp.concatenate([du, dvv], axis=-1) * _gelu_grad(z, th)).astype(BF16)
        dg_ref[0:1, :] += _colsum(dgao)
        dg_ref[1:2, :] += _colsum(dgso)
        dg_ref[2:3, :] += _colsum(dgs)

        @pl.when(step == n_tiles - 1)
        def _():
            dbs_ref[...] = _dot_f32(df_sum[...], ind_ref[...])

    row = lambda n: pl.BlockSpec((tm, n), lambda i: (i, 0))
    return pl.pallas_call(
        functools.partial(body), name="mix_bwd", grid=(n_tiles,),
        in_specs=[row(D_MODEL), row(2 * D_SGU), _head_tile_spec(tm, HEAD_DIM), _full((1, D_SGU)), _full((1, D_ATTN)),
                  _full((1, D_SGU)),
                  _full((N_SGU_GROUPS, CHUNK, CHUNK)), _full((N_SGU_GROUPS, CHUNK, CHUNK)), _full((CHUNK, D_SGU)),
                  _full((D_MODEL, D_MODEL)), _full((D_SGU, LANES))],
        out_specs=[_head_tile_spec(tm, HEAD_DIM), pl.BlockSpec((N_HEADS, None, tm, 2 * HEAD_DIM), lambda i: (0, i, 0, 0)),
                   row(2 * D_SGU), _full((8, D_SGU)),
                   _full((N_SGU_GROUPS, CHUNK, CHUNK)), _full((CHUNK, LANES))],
        out_shape=[jax.ShapeDtypeStruct((N_HEADS, n_tiles, HEAD_DIM, tm), BF16),
                   jax.ShapeDtypeStruct((N_HEADS, n_tiles, tm, 2 * HEAD_DIM), BF16),
                   jax.ShapeDtypeStruct((t, 2 * D_SGU), BF16),
                   jax.ShapeDtypeStruct((8, D_SGU), F32),
                   jax.ShapeDtypeStruct((N_SGU_GROUPS, CHUNK, CHUNK), F32),
                   jax.ShapeDtypeStruct((CHUNK, LANES), F32)],
        scratch_shapes=[pltpu.VMEM((CHUNK, D_SGU), F32)],
        compiler_params=_params(1),
    )(dx2, z, o, g_sgu, g_ao, g_so, ws, ws_t, bias, w_out, group_ind)


def ffn_down_loss(act, wd, x, g, target, tm):
    t = x.shape[0]

    def body(act_ref, wd_ref, x_ref, g_ref, t_ref, loss_ref, dx_ref, dg_ref):
        @pl.when(pl.program_id(0) == 0)
        def _():
            loss_ref[...] = jnp.zeros_like(loss_ref)
            dg_ref[...] = jnp.zeros_like(dg_ref)

        xx = x_ref[...] + 0.5 * _dot(act_ref[...], wd_ref[...])
        r = _rstd(xx)
        n = xx * r
        err = n * g_ref[...] - t_ref[...]
        per_token = jnp.mean(err * err, axis=-1, keepdims=True)
        loss_ref[...] += 0.5 * jnp.sum(per_token, axis=0, keepdims=True)
        dx, dg_rows = _rms_bwd(err * (1.0 / D_MODEL), n, r, g_ref[...])
        dx_ref[...] = dx
        dg_ref[...] += _colsum(dg_rows)

    row = pl.BlockSpec((tm, D_MODEL), lambda i: (i, 0))
    return pl.pallas_call(
        functools.partial(body), name="ffn_down_loss", grid=(t // tm,),
        in_specs=[pl.BlockSpec((tm, D_FF), lambda i: (i, 0)), _full((D_FF, D_MODEL)), row, _full((1, D_MODEL)), row],
        out_specs=[_full((1, LANES)), row, _full((1, D_MODEL))],
        out_shape=[jax.ShapeDtypeStruct((1, LANES), F32), jax.ShapeDtypeStruct((t, D_MODEL), F32),
                   jax.ShapeDtypeStruct((1, D_MODEL), F32)],
        compiler_params=_params(1),
    )(act, wd, x, g, target)


def _rope_tables(t):
    rows = t // GRID_W
    row_idx = jnp.repeat(jnp.arange(rows, dtype=F32), GRID_W)
    col_idx = jnp.tile(jnp.arange(GRID_W, dtype=F32), rows)
    axis_dim = HEAD_DIM // 2
    inv = 1.0 / (ROPE_THETA ** (jnp.arange(0, axis_dim, 2, dtype=F32) / axis_dim))
    ang = jnp.concatenate([row_idx[:, None] * inv, col_idx[:, None] * inv], axis=-1)
    cos = jnp.repeat(jnp.cos(ang), 2, axis=-1)
    sin = jnp.repeat(jnp.sin(ang), 2, axis=-1) * jnp.tile(jnp.array([-1.0, 1.0], F32), HEAD_DIM // 2)
    return jnp.tile(cos, (1, LANES // HEAD_DIM)), jnp.tile(sin, (1, LANES // HEAD_DIM))


def kernel(x, g_ffn1, w1_gate, w1_up, w1_down, g_mix, w_in, g_q, g_k, g_sgu, w_s, b_s, g_attn_out, g_sgu_out, w_out, g_ffn2, w2_gate, w2_up, w2_down, g_final, loss_target, m_g_ffn1, m_w1_gate, m_w1_up, m_w1_down, m_g_mix, m_w_in, m_g_q, m_g_k, m_g_sgu, m_w_s, m_b_s, m_g_attn_out, m_g_sgu_out, m_w_out, m_g_ffn2, m_w2_gate, m_w2_up, m_w2_down, m_g_final, v_g_ffn1, v_w1_gate, v_w1_up, v_w1_down, v_g_mix, v_w_in, v_g_q, v_g_k, v_g_sgu, v_w_s, v_b_s, v_g_attn_out, v_g_sgu_out, v_w_out, v_g_ffn2, v_w2_gate, v_w2_up, v_w2_down, v_g_final):
    weights = dict(g_ffn1=g_ffn1, w1_gate=w1_gate, w1_up=w1_up, w1_down=w1_down, g_mix=g_mix, w_in=w_in, g_q=g_q,
                   g_k=g_k, g_sgu=g_sgu, w_s=w_s, b_s=b_s, g_attn_out=g_attn_out, g_sgu_out=g_sgu_out, w_out=w_out,
                   g_ffn2=g_ffn2, w2_gate=w2_gate, w2_up=w2_up, w2_down=w2_down, g_final=g_final)
    m_in = dict(g_ffn1=m_g_ffn1, w1_gate=m_w1_gate, w1_up=m_w1_up, w1_down=m_w1_down, g_mix=m_g_mix, w_in=m_w_in,
                g_q=m_g_q, g_k=m_g_k, g_sgu=m_g_sgu, w_s=m_w_s, b_s=m_b_s, g_attn_out=m_g_attn_out,
                g_sgu_out=m_g_sgu_out, w_out=m_w_out, g_ffn2=m_g_ffn2, w2_gate=m_w2_gate, w2_up=m_w2_up,
                w2_down=m_w2_down, g_final=m_g_final)
    v_in = dict(g_ffn1=v_g_ffn1, w1_gate=v_w1_gate, w1_up=v_w1_up, w1_down=v_w1_down, g_mix=v_g_mix, w_in=v_w_in,
                g_q=v_g_q, g_k=v_g_k, g_sgu=v_g_sgu, w_s=v_w_s, b_s=v_b_s, g_attn_out=v_g_attn_out,
                g_sgu_out=v_g_sgu_out, w_out=v_w_out, g_ffn2=v_g_ffn2, w2_gate=v_w2_gate, w2_up=v_w2_up,
                w2_down=v_w2_down, g_final=v_g_final)
    names = list(weights)

    t = x.shape[1]
    x0 = x[0]
    target = loss_target[0]
    tm = min(512, t)
    tm_ff = min(256, t)
    tn_ff = 256
    tq = min(512, t)
    tk = min(256, t)
    tk_fwd = min(512, t)
    tk_w = min(2048, t)

    def shard_rows(name):
        w = weights[name][0]
        return (w.T if name in TRANSPOSED else w).astype(BF16)

    rows_of = dict(SHARD_ROWS)
    full = {}

    def packed(group):
        return jnp.concatenate([shard_rows(n) for n in group], axis=0), [rows_of[n] for n in group]

    def gather_of(group):
        return gather_exchange(*packed(group))

    def take(group, gathered):
        for n, g in zip(group, gathered):
            full[n] = g.reshape(N_DEV * rows_of[n], D_MODEL)

    first, second, third = ("w1_gate", "w1_up"), ("w1_down", "w_in", "w_out"), ("w2_gate", "w2_up", "w2_down")
    take(first, gather_two_level(*packed(first), "gather_first"))

    (h1, a1, b1, act1), gathered = ffn_up(x0, g_ffn1, full["w1_gate"], full["w1_up"], tm_ff, tn_ff, gather_of(second))
    take(second, gathered)
    w_in_t = full["w_in"]
    w_qkv_t, w_z_t = w_in_t[:D_QKV], w_in_t[D_QKV:]
    x1 = ffn_down(act1, full["w1_down"], x0, tm)

    qkv, z, h2 = input_projection(x1, g_mix, w_qkv_t, w_z_t, tm)
    cos_w, sin_w = _rope_tables(t)
    gq_w = jnp.tile(g_q, (1, N_HEADS))
    gk_w = jnp.tile(g_k, (1, N_KV_HEADS))
    mean_q, mean_k = _head_mean_matrix(D_ATTN).astype(BF16), _head_mean_matrix(D_KV).astype(BF16)
    q_t, k_tiles, kt_tiles, vt_tiles, vt_tiles_bwd, q_sq_max, k_sq_max = qk_prep(
        qkv, gq_w, gk_w, cos_w, sin_w, mean_q, mean_k, tq, tk, tk_fwd)
    score_bound = 1.02 * jnp.sqrt(q_sq_max.reshape(N_HEADS, HEAD_DIM)[:, 0]
                                  * jnp.repeat(k_sq_max.reshape(N_KV_HEADS, HEAD_DIM)[:, 0], KV_GROUP))
    k_tiles_fwd = k_tiles.reshape(N_KV_HEADS, t // tk_fwd, tk_fwd, HEAD_DIM)
    (o_t, q_tok), gathered = attention_fwd(score_bound, q_t, k_tiles_fwd, vt_tiles, gather_of(third))
    take(third, gathered)

    ws_b = w_s[0].astype(BF16)
    ws_tb = jnp.swapaxes(w_s[0], 1, 2).astype(BF16)
    bias = jnp.repeat(b_s[0].T, SGU_GROUP_DIM, axis=1)
    x2, mixed = mix_out(z, o_t, x1, g_sgu, g_attn_out, g_sgu_out, ws_tb, bias, full["w_out"], tq)

    (h3, a2, b2, act2), _ = ffn_up(x2, g_ffn2, full["w2_gate"], full["w2_up"], tm_ff, tn_ff)

    loss_part, dx3, dg_final = ffn_down_loss(act2, full["w2_down"], x2, g_final, target, tm)

    tmm = D_FF // 2
    (da2, db2), _ = ffn_bwd_act(dx3, full["w2_down"], a2, b2, tm_ff, tn_ff)
    (dx2, dg_ffn2), _ = norm_bwd_matmul(da2, full["w2_gate"], db2, full["w2_up"], x2, g_ffn2, dx3, tm)
    dwg2, dwu2 = matmul_tn(da2, h3, 1.0, tmm, tk_w), matmul_tn(db2, h3, 1.0, tmm, tk_w)
    dwd2 = matmul_tn(act2, dx3, 0.5, tmm, tk_w)

    group_ind = (jnp.arange(D_SGU)[:, None] // SGU_GROUP_DIM == jnp.arange(LANES)[None, :]).astype(F32)
    do_t, do_tok, dz, dg_mixrow, dws, dbs = mix_bwd(dx2, z, o_t, g_sgu, g_attn_out, g_sgu_out, ws_b, ws_tb, bias,
                                                    full["w_out"], group_ind, tq)
    dw_out = matmul_tn(mixed, dx2, 1.0, D_MODEL // 2, tk_w)

    group_a = ("w2_gate", "w2_up", "w2_down", "w_out")
    (dq_t, dk_t, dv_t), (parts_a,) = attention_bwd(q_tok, do_tok, q_t, do_t, kt_tiles, vt_tiles_bwd,
                                                   scatter_exchange([dwg2, dwu2, dwd2, dw_out]))
    dqkv, dgq_w, dgk_w = qk_bwd(dq_t, dk_t, dv_t, qkv, gq_w, gk_w, cos_w, sin_w, mean_q, mean_k, tq)

    def pack_small(arrays):
        pieces = []
        for a in arrays:
            flat = a.reshape(-1)
            pieces.append(jnp.pad(flat, (0, (-flat.shape[0]) % (8 * LANES))).reshape(-1, LANES))
        return jnp.concatenate(pieces, axis=0), [p.shape[0] for p in pieces]

    early = dict(g_ffn2=dg_ffn2, g_final=dg_final, g_q=dgq_w.reshape(N_HEADS, HEAD_DIM).sum(0),
                 g_k=dgk_w.reshape(N_KV_HEADS, HEAD_DIM).sum(0), g_attn_out=dg_mixrow[0], g_sgu_out=dg_mixrow[1],
                 g_sgu=dg_mixrow[2], w_s=dws, b_s=dbs[:, :N_SGU_GROUPS].T)
    early_pack, early_rows = pack_small(list(early.values()))
    (dx1, dg_mix), (early_parts,) = norm_bwd_matmul(dqkv, w_qkv_t, dz, w_z_t, x1, g_mix, dx2, tm,
                                                    gather_exchange(early_pack, [early_pack.shape[0]]))
    dw_in = jnp.concatenate([matmul_tn(dqkv, h2, 1.0, D_QKV // 2, tk_w), matmul_tn(dz, h2, 1.0, D_SGU, tk_w)], axis=0)

    dwd1, (parts_in,) = matmul_tn(act1, dx1, 0.5, tmm, tk_w, scatter_exchange([dw_in]))
    (da1, db1), (parts_d1,) = ffn_bwd_act(dx1, full["w1_down"], a1, b1, tm_ff, tn_ff, scatter_exchange([dwd1]))
    dwg1 = matmul_tn(da1, h1, 1.0, tmm, tk_w)
    dwu1, (parts_g1,) = matmul_tn(db1, h1, 1.0, tmm, tk_w, scatter_exchange([dwg1]))
    (dx0, dg_ffn1), (parts_u1,) = norm_bwd_matmul(da1, full["w1_gate"], db1, full["w1_up"], x0, g_ffn1, dx1, tm,
                                                  scatter_exchange([dwu1]))
    scattered = ((group_a, parts_a), (("w_in",), parts_in), (("w1_down",), parts_d1), (("w1_gate",), parts_g1),
                 (("w1_up",), parts_u1))

    late = dict(g_mix=dg_mix, g_ffn1=dg_ffn1, loss=loss_part)
    late_pack, late_rows = pack_small(list(late.values()))
    (late_parts,) = run_exchange(gather_exchange(late_pack, [late_pack.shape[0]]), "gather_late_small_grads")
    small_sums = {}
    for entries, rows, parts in ((early, early_rows, early_parts), (late, late_rows, late_parts)):
        summed = sum_parts(parts, parts.shape[1])
        off = 0
        for n, r in zip(entries, rows):
            small_sums[n] = summed[off:off + r]
            off += r
    loss = small_sums.pop("loss")[0, 0]

    grads, row_grads = {}, {}
    for group, parts in scattered:
        rows = parts.shape[1]
        summed = sum_parts(parts, rows if rows <= 2 * rows_of["w1_gate"] else rows // 2)
        off = 0
        for n in group:
            row_grads[n] = summed[off:off + rows_of[n]]
            grads[n] = (row_grads[n].T if n in TRANSPOSED else row_grads[n])[None]
            off += rows_of[n]
    for n, summed in small_sums.items():
        grads[n] = summed.reshape(-1)[:weights[n].size].reshape(weights[n].shape)

    delta_w, new_m, new_v = {}, {}, {}
    for n in names:
        shape = weights[n].shape
        if n in TRANSPOSED:
            view, unview, g = (lambda a: a[0].T), (lambda a: a.T[None]), row_grads[n]
        else:
            view, unview = (lambda a: a.reshape(-1, shape[-1])), (lambda a: a.reshape(shape))
            g = view(grads[n])
        d, m2, v2 = adamw(view(weights[n]), g, view(m_in[n]), view(v_in[n]))
        delta_w[n], new_m[n], new_v[n] = unview(d), unview(m2), unview(v2)

    return (loss, dx0[None], *[grads[n] for n in names], *[delta_w[n] for n in names],
            *[new_m[n] for n in names], *[new_v[n] for n in names])
```

```python
import functools
import math

import jax
import jax.numpy as jnp
from jax import lax
from jax.experimental import pallas as pl
from jax.experimental.pallas import tpu as pltpu

F32 = jnp.float32
BF16 = jnp.bfloat16

D_MODEL = 1024
D_FF = 2816
N_HEADS = 8
HEAD_DIM = 64
N_KV_HEADS = 2
KV_GROUP = N_HEADS // N_KV_HEADS
D_ATTN = N_HEADS * HEAD_DIM
D_KV = N_KV_HEADS * HEAD_DIM
D_QKV = D_ATTN + 2 * D_KV
N_SGU_GROUPS = 8
SGU_GROUP_DIM = 64
D_SGU = N_SGU_GROUPS * SGU_GROUP_DIM
CHUNK = 128
GRID_W = 64
ROPE_THETA = 10000.0
EPS = 1e-6
N_DEV = 8
LANES = 128

ONES_ROWS = 16
SAFE_SCORE_BOUND = 60.0
LOG2_E = math.log2(math.e)
Q_SCALE = HEAD_DIM ** -0.5 * LOG2_E

ADAM_LR = 0.001
ADAM_B1 = 0.9
ADAM_B2 = 0.999
ADAM_EPS = 1e-08
ADAM_WD = 0.01
ADAM_STEP = 10

MESH_IDS = pl.DeviceIdType.MESH

VMEM_LIMIT = 56 * 1024 * 1024

SHARD_ROWS = (("w1_gate", D_FF // N_DEV), ("w1_up", D_FF // N_DEV), ("w1_down", D_FF // N_DEV),
              ("w_in", (D_QKV + 2 * D_SGU) // N_DEV), ("w_out", D_MODEL // N_DEV),
              ("w2_gate", D_FF // N_DEV), ("w2_up", D_FF // N_DEV), ("w2_down", D_FF // N_DEV))
TRANSPOSED = ("w1_gate", "w1_up", "w_in", "w2_gate", "w2_up")


def _params(n_grid):
    return pltpu.CompilerParams(dimension_semantics=("arbitrary",) * n_grid, vmem_limit_bytes=VMEM_LIMIT)


def _dot(a, b):
    return jnp.dot(a, b, preferred_element_type=F32)


def _dot_nt(a, b):
    return lax.dot_general(a, b, (((1,), (1,)), ((), ())), preferred_element_type=F32)


def _dot_tn(a, b):
    return lax.dot_general(a, b, (((0,), (0,)), ((), ())), preferred_element_type=F32)


def _dot_f32(a, b):
    return jnp.dot(a, b, preferred_element_type=F32, precision=lax.Precision.HIGHEST)


def _dot_split(a, b):
    hi = a.astype(BF16)
    lo = (a - hi.astype(F32)).astype(BF16)
    return _dot(hi, b) + _dot(lo, b)


def _rstd(x):
    return lax.rsqrt(jnp.mean(x * x, axis=-1, keepdims=True) + EPS)


def _rms_bwd(dy, n, r, g):
    dn = dy * g
    return r * (dn - n * jnp.mean(dn * n, axis=-1, keepdims=True)), dy * n


def _colsum(a):
    return jnp.sum(a, axis=0, keepdims=True)


_GELU_C = math.sqrt(2.0 / math.pi)


def _gelu(x):
    t = jnp.tanh(_GELU_C * (x + 0.044715 * (x * x * x)))
    return x * (0.5 * (1.0 + t)), t


def _gelu_grad(x, t):
    return 0.5 * (1.0 + t) + 0.5 * x * (1.0 - t * t) * (_GELU_C * (1.0 + 3 * 0.044715 * x * x))


def _pair_swap(a):
    w = a.shape[-1]
    lane = lax.broadcasted_iota(jnp.int32, a.shape, a.ndim - 1)
    return jnp.where(lane % 2 == 0, pltpu.roll(a, w - 1, a.ndim - 1), pltpu.roll(a, 1, a.ndim - 1))


def _tile_lanes(a, reps):
    return jnp.concatenate([a] * reps, axis=-1) if reps > 1 else a


def _loop_pairs(n, step, carry):
    assert n % 2 == 0, n

    def pair(jj, c):
        return step(2 * jj + 1, 1, step(2 * jj, 0, c))

    return lax.fori_loop(0, n // 2, pair, carry)


def _full(shape):
    nd = len(shape)
    return pl.BlockSpec(shape, lambda *_: (0,) * nd)


def _mesh_pos():
    return lax.axis_index("x"), lax.axis_index("y"), lax.axis_index("c")


def _peer(pos, d):
    x, y, c = pos
    px = 1 - x if d & 4 else x
    py = 1 - y if d & 2 else y
    pc = 1 - c if d & 1 else c
    return (px, py, pc), 4 * px + 2 * py + pc


class _Exchange:
    def __init__(self, operands, out_shape, n_local, plan):
        self.operands = list(operands)
        self.out_shape = list(out_shape)
        self.sem_shapes = [pltpu.SemaphoreType.DMA((N_DEV - 1,)), pltpu.SemaphoreType.DMA((N_DEV - 1,)),
                           pltpu.SemaphoreType.DMA((n_local,))]
        self._plan = plan

    def _copies(self, in_refs, out_refs):
        pos = _mesh_pos()
        return pos, self._plan(4 * pos[0] + 2 * pos[1] + pos[2], in_refs, out_refs)

    def start(self, in_refs, out_refs, sems):
        send_sems, recv_sems, local_sems = sems
        pos, (local, remote, _) = self._copies(in_refs, out_refs)
        for k, (src, dst) in enumerate(local):
            pltpu.make_async_copy(src, dst, local_sems.at[k]).start()
        for d in range(1, N_DEV):
            peer, peer_lin = _peer(pos, d)
            for src, dst in remote(peer_lin):
                pltpu.make_async_remote_copy(src_ref=src, dst_ref=dst, send_sem=send_sems.at[d - 1],
                                             recv_sem=recv_sems.at[d - 1], device_id=peer,
                                             device_id_type=MESH_IDS).start()

    def wait(self, in_refs, out_refs, sems):
        send_sems, recv_sems, local_sems = sems
        pos, (local, _, whole) = self._copies(in_refs, out_refs)
        for d in range(1, N_DEV):
            peer, peer_lin = _peer(pos, d)
            ref = whole(peer_lin)
            everything = pltpu.make_async_remote_copy(src_ref=ref, dst_ref=ref, send_sem=send_sems.at[d - 1],
                                                      recv_sem=recv_sems.at[d - 1], device_id=peer,
                                                      device_id_type=MESH_IDS)
            everything.wait_send()
            everything.wait_recv()
        for k, (src, dst) in enumerate(local):
            pltpu.make_async_copy(src, dst, local_sems.at[k]).wait()


def _offsets(rows):
    offs, o = [], 0
    for r in rows:
        offs.append(o)
        o += r
    return offs


def gather_exchange(src, rows):
    offs = _offsets(rows)

    def plan(me, in_refs, out_refs):
        pieces = [(in_refs[0].at[pl.ds(o, r)], out.at[me]) for o, r, out in zip(offs, rows, out_refs)]
        return pieces, (lambda peer_lin: pieces), (lambda peer_lin: in_refs[0])

    return _Exchange([src], [jax.ShapeDtypeStruct((N_DEV, r) + src.shape[1:], src.dtype) for r in rows],
                     len(rows), plan)


def scatter_exchange(grads):
    rows = [g.shape[0] // N_DEV for g in grads]
    offs = _offsets(rows)

    def plan(me, in_refs, out_refs):
        parts = out_refs[0]

        def slabs(owner):
            return [(g.at[pl.ds(pl.multiple_of(owner * r, 16), r)], parts.at[me, pl.ds(o, r)])
                    for g, o, r in zip(in_refs, offs, rows)]

        return slabs(me), slabs, (lambda peer_lin: parts.at[peer_lin])

    shape = jax.ShapeDtypeStruct((N_DEV, sum(rows)) + grads[0].shape[1:], grads[0].dtype)
    return _Exchange(grads, [shape], len(rows), plan)


def gather_two_level(src, rows, name):
    offs = _offsets(rows)
    n_p = len(rows)

    def body(src_ref, *refs):
        outs, (send_sems, recv_sems, local_sems) = refs[:n_p], refs[n_p:]
        x, y, c = _mesh_pos()
        me, sibling = (x, y, c), (x, y, 1 - c)
        chips = [(1 - x, y), (x, 1 - y), (1 - x, 1 - y)]

        def slab(w, dev):
            return outs[w].at[4 * dev[0] + 2 * dev[1] + dev[2]]

        def copy(w, k, block, to, from_src=False):
            return pltpu.make_async_remote_copy(
                src_ref=src_ref.at[pl.ds(offs[w], rows[w])] if from_src else slab(w, block), dst_ref=slab(w, block),
                send_sem=send_sems.at[w * 7 + k], recv_sem=recv_sems.at[w * 7 + k],
                device_id=to, device_id_type=MESH_IDS)

        mine = [pltpu.make_async_copy(src_ref.at[pl.ds(offs[w], rows[w])], slab(w, me), local_sems.at[w])
                for w in range(n_p)]
        for cp in mine:
            cp.start()
        first = []
        for w in range(n_p):
            first.append(copy(w, 0, me, sibling, True))
            first += [copy(w, 1 + j, me, (*chip, c), True) for j, chip in enumerate(chips)]
        for cp in first:
            cp.start()
        passed = []
        for j, chip in enumerate(chips):
            for w in range(n_p):
                copy(w, 1 + j, (*chip, c), me).wait_recv()
                cp = copy(w, 4 + j, (*chip, c), sibling)
                cp.start()
                passed.append(cp)
        for w in range(n_p):
            copy(w, 0, sibling, me).wait_recv()
            for j, chip in enumerate(chips):
                copy(w, 4 + j, (*chip, 1 - c), me).wait_recv()
        for cp in first + passed:
            cp.wait_send()
        for cp in mine:
            cp.wait()

    any_spec = pl.BlockSpec(memory_space=pl.ANY)
    return pl.pallas_call(
        functools.partial(body), name=name,
        out_shape=[jax.ShapeDtypeStruct((N_DEV, r) + src.shape[1:], src.dtype) for r in rows],
        in_specs=[any_spec], out_specs=[any_spec] * n_p,
        scratch_shapes=[pltpu.SemaphoreType.DMA((7 * n_p,)), pltpu.SemaphoreType.DMA((7 * n_p,)),
                        pltpu.SemaphoreType.DMA((n_p,))],
        compiler_params=pltpu.CompilerParams(has_side_effects=True),
    )(src)


def run_exchange(ex, name):
    n_in, n_out = len(ex.operands), len(ex.out_shape)

    def body(*refs):
        parts = refs[:n_in], refs[n_in:n_in + n_out], refs[n_in + n_out:]
        ex.start(*parts)
        ex.wait(*parts)

    any_spec = pl.BlockSpec(memory_space=pl.ANY)
    return pl.pallas_call(
        functools.partial(body), name=name, out_shape=ex.out_shape,
        in_specs=[any_spec] * n_in, out_specs=[any_spec] * n_out, scratch_shapes=ex.sem_shapes,
        compiler_params=pltpu.CompilerParams(has_side_effects=True),
    )(*ex.operands)


def _pallas(comm, body, *, name, grid, in_specs, out_specs, out_shape, args, scratch_shapes=()):
    params = _params(len(grid))
    if comm is None:
        res = pl.pallas_call(functools.partial(body), name=name, grid=grid, in_specs=list(in_specs),
                             out_specs=list(out_specs), out_shape=list(out_shape),
                             scratch_shapes=list(scratch_shapes), compiler_params=params)(*args)
        return list(res), []
    n_in, n_out, n_scr = len(in_specs), len(out_specs), len(scratch_shapes)
    c_in, c_out = len(comm.operands), len(comm.out_shape)

    def edge(last):
        conds = [pl.program_id(a) == (g - 1 if last else 0) for a, g in enumerate(grid)]
        return functools.reduce(jnp.logical_and, conds)

    def wrapped(*refs):
        refs = list(refs)
        ins, refs = refs[:n_in], refs[n_in:]
        cins, refs = refs[:c_in], refs[c_in:]
        outs, refs = refs[:n_out], refs[n_out:]
        couts, refs = refs[:c_out], refs[c_out:]
        scr, sems = refs[:n_scr], refs[n_scr:]

        @pl.when(edge(False))
        def _():
            comm.start(cins, couts, sems)

        body(*ins, *outs, *scr)

        @pl.when(edge(True))
        def _():
            comm.wait(cins, couts, sems)

    any_spec = pl.BlockSpec(memory_space=pl.ANY)
    res = pl.pallas_call(
        wrapped, name=name, grid=grid,
        in_specs=list(in_specs) + [any_spec] * c_in, out_specs=list(out_specs) + [any_spec] * c_out,
        out_shape=list(out_shape) + comm.out_shape, scratch_shapes=list(scratch_shapes) + comm.sem_shapes,
        compiler_params=pltpu.CompilerParams(dimension_semantics=("arbitrary",) * len(grid),
                                             vmem_limit_bytes=VMEM_LIMIT, has_side_effects=True),
    )(*args, *comm.operands)
    return res[:n_out], res[n_out:]


def sum_parts(parts, block_rows):
    n, rows, cols = parts.shape

    def body(p_ref, o_ref):
        acc = p_ref[0].astype(F32)
        for s in range(1, n):
            acc = acc + p_ref[s].astype(F32)
        o_ref[...] = acc

    return pl.pallas_call(
        functools.partial(body), name="sum_parts",
        grid=(rows // block_rows,),
        in_specs=[pl.BlockSpec((n, block_rows, cols), lambda i: (0, i, 0))],
        out_specs=pl.BlockSpec((block_rows, cols), lambda i: (i, 0)),
        out_shape=jax.ShapeDtypeStruct((rows, cols), F32),
        compiler_params=_params(1),
    )(parts)


def adamw(w, g, m, v):
    def body(w_ref, g_ref, m_ref, v_ref, d_ref, m_out, v_out):
        gg = g_ref[...]
        m2 = ADAM_B1 * m_ref[...] + (1.0 - ADAM_B1) * gg
        v2 = ADAM_B2 * v_ref[...] + (1.0 - ADAM_B2) * (gg * gg)
        m_hat = m2 / (1.0 - ADAM_B1 ** ADAM_STEP)
        v_hat = v2 / (1.0 - ADAM_B2 ** ADAM_STEP)
        d_ref[...] = -ADAM_LR * (m_hat / (jnp.sqrt(v_hat) + ADAM_EPS) + ADAM_WD * w_ref[...])
        m_out[...] = m2
        v_out[...] = v2

    spec = _full(w.shape)
    shape = jax.ShapeDtypeStruct(w.shape, F32)
    return pl.pallas_call(
        functools.partial(body), name="adamw",
        in_specs=[spec] * 4, out_specs=[spec] * 3, out_shape=[shape] * 3,
        compiler_params=pltpu.CompilerParams(vmem_limit_bytes=VMEM_LIMIT),
    )(w, g, m, v)


def ffn_up(x, g, wg_t, wu_t, tm, tn, comm=None):
    t = x.shape[0]

    def body(x_ref, g_ref, wg_ref, wu_ref, h_ref, silu_ref, dgate_ref, act_ref):
        xx = x_ref[...]
        h = ((xx * _rstd(xx)) * g_ref[...]).astype(BF16)
        h_ref[...] = h
        for c in range(D_FF // tn):
            cols = slice(c * tn, (c + 1) * tn)
            a = _dot_nt(h, wg_ref[cols, :])
            b = _dot_nt(h, wu_ref[cols, :])
            sig = 0.5 * jnp.tanh(0.5 * a) + 0.5
            silu = a * sig
            silu_ref[:, cols] = silu.astype(BF16)
            dgate_ref[:, cols] = (b * (sig + silu * (1.0 - sig))).astype(BF16)
            act_ref[:, cols] = (silu * b).astype(BF16)

    wide = jax.ShapeDtypeStruct((t, D_FF), BF16)
    row = lambda n: pl.BlockSpec((tm, n), lambda i: (i, 0))
    return _pallas(
        comm, body, name="ffn_up",
        grid=(t // tm,),
        in_specs=[row(D_MODEL), _full((1, D_MODEL)), _full((D_FF, D_MODEL)), _full((D_FF, D_MODEL))],
        out_specs=[row(D_MODEL), row(D_FF), row(D_FF), row(D_FF)],
        out_shape=[jax.ShapeDtypeStruct((t, D_MODEL), BF16), wide, wide, wide],
        args=(x, g, wg_t, wu_t))


def ffn_down(act, wd, x, tm):
    t = x.shape[0]

    def body(act_ref, wd_ref, x_ref, o_ref):
        o_ref[...] = x_ref[...] + 0.5 * _dot(act_ref[...], wd_ref[...])

    return pl.pallas_call(
        functools.partial(body), name="ffn_down",
        grid=(t // tm,),
        in_specs=[pl.BlockSpec((tm, D_FF), lambda i: (i, 0)), _full((D_FF, D_MODEL)),
                  pl.BlockSpec((tm, D_MODEL), lambda i: (i, 0))],
        out_specs=pl.BlockSpec((tm, D_MODEL), lambda i: (i, 0)),
        out_shape=jax.ShapeDtypeStruct((t, D_MODEL), F32),
        compiler_params=_params(1),
    )(act, wd, x)


def ffn_bwd_act(dx, wd, silu, dgate, tm, tn, comm=None):
    t = dx.shape[0]

    def body(dx_ref, wd_ref, silu_ref, dgate_ref, da_ref, db_ref):
        dxb = (0.5 * dx_ref[...]).astype(BF16)
        for c in range(D_FF // tn):
            cols = slice(c * tn, (c + 1) * tn)
            dact = _dot_nt(dxb, wd_ref[cols, :])
            da_ref[:, cols] = (dact * dgate_ref[:, cols].astype(F32)).astype(BF16)
            db_ref[:, cols] = (dact * silu_ref[:, cols].astype(F32)).astype(BF16)

    wide = jax.ShapeDtypeStruct((t, D_FF), BF16)
    row = lambda n: pl.BlockSpec((tm, n), lambda i: (i, 0))
    return _pallas(
        comm, body, name="ffn_bwd_act",
        grid=(t // tm,),
        in_specs=[row(D_MODEL), _full((D_FF, D_MODEL)), row(D_FF), row(D_FF)],
        out_specs=[row(D_FF), row(D_FF)],
        out_shape=[wide, wide],
        args=(dx, wd, silu, dgate))


def norm_bwd_matmul(a1, w1, a2, w2, x, g, dx_in, tm, comm=None):
    t = x.shape[0]
    k1, k2 = a1.shape[1], a2.shape[1]

    def body(a1_ref, w1_ref, a2_ref, w2_ref, x_ref, g_ref, dxin_ref, dx_ref, dg_ref):
        dh = _dot(a1_ref[...], w1_ref[...]) + _dot(a2_ref[...], w2_ref[...])
        xx = x_ref[...]
        r = _rstd(xx)
        dx, dg_rows = _rms_bwd(dh, xx * r, r, g_ref[...])
        dx_ref[...] = dxin_ref[...] + dx

        @pl.when(pl.program_id(0) == 0)
        def _():
            dg_ref[...] = jnp.zeros_like(dg_ref)

        dg_ref[...] += _colsum(dg_rows)

    row = pl.BlockSpec((tm, D_MODEL), lambda i: (i, 0))
    return _pallas(
        comm, body, name="norm_bwd_matmul",
        grid=(t // tm,),
        in_specs=[pl.BlockSpec((tm, k1), lambda i: (i, 0)), _full((k1, D_MODEL)),
                  pl.BlockSpec((tm, k2), lambda i: (i, 0)), _full((k2, D_MODEL)),
                  row, _full((1, D_MODEL)), row],
        out_specs=[row, _full((1, D_MODEL))],
        out_shape=[jax.ShapeDtypeStruct((t, D_MODEL), F32), jax.ShapeDtypeStruct((1, D_MODEL), F32)],
        args=(a1, w1, a2, w2, x, g, dx_in))


def matmul_tn(a, b, scale, tmm, tk, comm=None):
    t, m = a.shape
    n = b.shape[1]
    nk = t // tk

    def body(a_ref, b_ref, o_ref, acc_ref):
        k = pl.program_id(1)

        @pl.when(k == 0)
        def _():
            acc_ref[...] = jnp.zeros_like(acc_ref)

        acc_ref[...] += _dot_tn(a_ref[...].astype(BF16), b_ref[...].astype(BF16))

        @pl.when(k == nk - 1)
        def _():
            o_ref[...] = (scale * acc_ref[...]).astype(BF16)

    (out,), comm_outs = _pallas(
        comm, body, name="matmul_tn",
        grid=(m // tmm, nk),
        in_specs=[pl.BlockSpec((tk, tmm), lambda i, k: (k, i)), pl.BlockSpec((tk, n), lambda i, k: (k, 0))],
        out_specs=[pl.BlockSpec((tmm, n), lambda i, k: (i, 0))],
        out_shape=[jax.ShapeDtypeStruct((m, n), BF16)],
        scratch_shapes=[pltpu.VMEM((tmm, n), F32)],
        args=(a, b))
    return out if comm is None else (out, comm_outs)


def input_projection(x, g, w_qkv_t, w_z_t, tm):
    t = x.shape[0]

    def body(x_ref, g_ref, wq_ref, wz_ref, qkv_ref, z_ref, h_ref):
        xx = x_ref[...]
        h = ((xx * _rstd(xx)) * g_ref[...]).astype(BF16)
        h_ref[...] = h
        qkv_ref[...] = _dot_nt(h, wq_ref[...])
        z_ref[...] = _dot_nt(h, wz_ref[...])

    row = lambda n: pl.BlockSpec((tm, n), lambda i: (i, 0))
    return pl.pallas_call(
        functools.partial(body), name="input_projection", grid=(t // tm,),
        in_specs=[row(D_MODEL), _full((1, D_MODEL)), _full((D_QKV, D_MODEL)), _full((2 * D_SGU, D_MODEL))],
        out_specs=[row(D_QKV), row(2 * D_SGU), row(D_MODEL)],
        out_shape=[jax.ShapeDtypeStruct((t, D_QKV), F32), jax.ShapeDtypeStruct((t, 2 * D_SGU), F32),
                   jax.ShapeDtypeStruct((t, D_MODEL), BF16)],
        compiler_params=_params(1))(x, g, w_qkv_t, w_z_t)


def _shift_rows(shape, first, second):
    row = lax.broadcasted_iota(jnp.int32, shape, len(shape) - 2)
    return jnp.where(row == 0, first, jnp.where(row == 1, second, 0.0))


def _hi_lo(a):
    hi = a.astype(BF16).astype(F32)
    return hi, a - hi


def _head_tile_spec(tm, rows):
    return pl.BlockSpec((N_HEADS, None, rows, tm), lambda i: (0, i, 0, 0))


def _to_head_tiles(a):
    return a.T.reshape(N_HEADS, HEAD_DIM, a.shape[0])


def _from_head_tiles(a):
    return a.reshape(D_ATTN, a.shape[-1]).T


def _head_mean_matrix(width):
    head = jnp.arange(width) // HEAD_DIM
    return (head[:, None] == head[None, :]).astype(F32) / HEAD_DIM


def _kv_tile_spec(n_sub, rows, cols):
    return pl.BlockSpec((N_KV_HEADS, n_sub, rows, cols), lambda i: (0, i, 0, 0))


def qk_prep(qkv, gq_w, gk_w, cos_w, sin_w, mean_q, mean_k, tm, tk, tk_v):
    t = qkv.shape[0]
    n_sub, n_sub_v = tm // tk, tm // tk_v

    def body(p_ref, gq_ref, gk_ref, cos_ref, sin_ref, mq_ref, mk_ref, q_ref, k_ref, kt_ref, vt_ref, vtb_ref,
             qmax_ref, kmax_ref):
        @pl.when(pl.program_id(0) == 0)
        def _():
            qmax_ref[...] = jnp.zeros_like(qmax_ref)
            kmax_ref[...] = jnp.zeros_like(kmax_ref)

        cos2, sin2 = cos_ref[...], sin_ref[...]
        q = p_ref[:, :D_ATTN]
        k = p_ref[:, D_ATTN:D_ATTN + D_KV]
        qn = q * lax.rsqrt(_dot_split(q * q, mq_ref[...]) + EPS) * gq_ref[...]
        kn = k * lax.rsqrt(_dot_split(k * k, mk_ref[...]) + EPS) * gk_ref[...]
        cos8, sin8 = _tile_lanes(cos2, D_ATTN // LANES), _tile_lanes(sin2, D_ATTN // LANES)
        q_rot = (qn * cos8 + _pair_swap(qn) * sin8) * Q_SCALE
        q_ref[...] = _to_head_tiles(q_rot).astype(BF16)
        k_rot = kn * cos2 + _pair_swap(kn) * sin2
        q_sq = HEAD_DIM * _dot_split(q_rot * q_rot, mq_ref[...])
        k_sq = HEAD_DIM * _dot_split(k_rot * k_rot, mk_ref[...])
        qmax_ref[...] = jnp.maximum(qmax_ref[...], jnp.max(q_sq, axis=0, keepdims=True))
        kmax_ref[...] = jnp.maximum(kmax_ref[...], jnp.max(k_sq, axis=0, keepdims=True))
        vv = p_ref[:, D_ATTN + D_KV:]
        second = pltpu.roll(k_rot, HEAD_DIM, 1)
        for c in range(n_sub):
            rows = slice(c * tk, (c + 1) * tk)
            k_ref[0, c] = k_rot[rows, :HEAD_DIM].astype(BF16)
            k_ref[1, c] = second[rows, :HEAD_DIM].astype(BF16)
        for a, feat_ref, width, n in ((k_rot, kt_ref, tk, n_sub), (vv, vtb_ref, tk, n_sub), (vv, vt_ref, tk_v, n_sub_v)):
            for c in range(n):
                tile = a[c * width:(c + 1) * width].T.reshape(N_KV_HEADS, HEAD_DIM, width)
                feat_ref[:, c, :HEAD_DIM, :] = tile.astype(BF16)
        vt_ref[:, :, HEAD_DIM:, :] = jnp.ones((N_KV_HEADS, n_sub_v, ONES_ROWS, tk_v), BF16)
        minus = _shift_rows((N_KV_HEADS, n_sub, HEAD_DIM, tk), -1.0, -1.0).astype(BF16)
        kt_ref[:, :, HEAD_DIM:, :] = minus
        vtb_ref[:, :, HEAD_DIM:, :] = minus

    kv = lambda rows, cols: jax.ShapeDtypeStruct((N_KV_HEADS, t // tk, rows, cols), BF16)
    return pl.pallas_call(
        functools.partial(body), name="qk_prep", grid=(t // tm,),
        in_specs=[pl.BlockSpec((tm, D_QKV), lambda i: (i, 0)), _full((1, D_ATTN)), _full((1, D_KV)),
                  pl.BlockSpec((tm, LANES), lambda i: (i, 0)), pl.BlockSpec((tm, LANES), lambda i: (i, 0)),
                  _full((D_ATTN, D_ATTN)), _full((D_KV, D_KV))],
        out_specs=[_head_tile_spec(tm, HEAD_DIM), _kv_tile_spec(n_sub, tk, HEAD_DIM),
                   _kv_tile_spec(n_sub, 2 * HEAD_DIM, tk),
                   _kv_tile_spec(n_sub_v, HEAD_DIM + ONES_ROWS, tk_v), _kv_tile_spec(n_sub, 2 * HEAD_DIM, tk),
                   _full((1, D_ATTN)), _full((1, D_KV))],
        out_shape=[jax.ShapeDtypeStruct((N_HEADS, t // tm, HEAD_DIM, tm), BF16), kv(tk, HEAD_DIM), kv(2 * HEAD_DIM, tk),
                   jax.ShapeDtypeStruct((N_KV_HEADS, t // tk_v, HEAD_DIM + ONES_ROWS, tk_v), BF16),
                   kv(2 * HEAD_DIM, tk),
                   jax.ShapeDtypeStruct((1, D_ATTN), F32), jax.ShapeDtypeStruct((1, D_KV), F32)],
        compiler_params=_params(1),
    )(qkv, gq_w, gk_w, cos_w, sin_w, mean_q, mean_k)


def qk_bwd(dq_rot, dk_rot, dv, qkv, gq_w, gk_w, cos_w, sin_w, mean_q, mean_k, tm):
    t = qkv.shape[0]
    tk = dk_rot.shape[-1]
    n_sub = tm // tk

    def token_major(ref):
        return jnp.concatenate([ref[:, c].reshape(D_KV, tk).T for c in range(n_sub)], axis=0)

    def branch(raw, d_rot, gain, mean_mat, cos, sin, scale):
        r = lax.rsqrt(_dot_split(raw * raw, mean_mat) + EPS)
        n = raw * r
        dy = (d_rot * cos - _pair_swap(d_rot) * sin) * scale
        dn = dy * gain
        return r * (dn - n * _dot_split(dn * n, mean_mat)), dy * n

    def body(dq_ref, dk_ref, dv_ref, p_ref, gq_ref, gk_ref, cos_ref, sin_ref, mq_ref, mk_ref,
             dp_ref, dgq_ref, dgk_ref):
        cos2, sin2 = cos_ref[...], sin_ref[...]
        cos8, sin8 = _tile_lanes(cos2, D_ATTN // LANES), _tile_lanes(sin2, D_ATTN // LANES)
        dq, dgq = branch(p_ref[:, :D_ATTN], _from_head_tiles(dq_ref[...]), gq_ref[...], mq_ref[...], cos8, sin8,
                         HEAD_DIM ** -0.5)
        dk, dgk = branch(p_ref[:, D_ATTN:D_ATTN + D_KV], token_major(dk_ref), gk_ref[...], mk_ref[...], cos2, sin2, 1.0)
        dp_ref[...] = jnp.concatenate([dq, dk, token_major(dv_ref)], axis=-1).astype(BF16)

        @pl.when(pl.program_id(0) == 0)
        def _():
            dgq_ref[...] = jnp.zeros_like(dgq_ref)
            dgk_ref[...] = jnp.zeros_like(dgk_ref)

        dgq_ref[...] += _colsum(dgq)
        dgk_ref[...] += _colsum(dgk)

    return pl.pallas_call(
        functools.partial(body), name="qk_bwd", grid=(t // tm,),
        in_specs=[_head_tile_spec(tm, HEAD_DIM), _kv_tile_spec(n_sub, HEAD_DIM, tk),
                  _kv_tile_spec(n_sub, HEAD_DIM, tk), pl.BlockSpec((tm, D_QKV), lambda i: (i, 0)),
                  _full((1, D_ATTN)), _full((1, D_KV)),
                  pl.BlockSpec((tm, LANES), lambda i: (i, 0)), pl.BlockSpec((tm, LANES), lambda i: (i, 0)),
                  _full((D_ATTN, D_ATTN)), _full((D_KV, D_KV))],
        out_specs=[pl.BlockSpec((tm, D_QKV), lambda i: (i, 0)), _full((1, D_ATTN)), _full((1, D_KV))],
        out_shape=[jax.ShapeDtypeStruct((t, D_QKV), BF16), jax.ShapeDtypeStruct((1, D_ATTN), F32),
                   jax.ShapeDtypeStruct((1, D_KV), F32)],
        compiler_params=_params(1),
    )(dq_rot, dk_rot, dv, qkv, gq_w, gk_w, cos_w, sin_w, mean_q, mean_k)


def attention_fwd(bound, q_t, k, v_t, comm=None):
    _, nq, _, tq = q_t.shape
    _, nk, tk, _ = k.shape

    def body(bound_ref, q_ref, k_ref, v_ref, o_ref, qtok_ref, s_scr, p_scr):
        head_bound = bound_ref[pl.program_id(0)]
        safe = head_bound <= SAFE_SCORE_BOUND
        q = q_ref[...]
        s_scr[0] = _dot(k_ref[0], q)
        p_scr[1] = jnp.zeros((tk, tq), BF16)
        zero = jnp.zeros((HEAD_DIM + ONES_ROWS, tq), F32)

        def matmuls(j, slot):
            pv = _dot(v_ref[jnp.maximum(j - 1, 0)], p_scr[1 - slot])
            s_scr[1 - slot] = _dot(k_ref[jnp.minimum(j + 1, nk - 1)], q)
            return pv

        def finish(m, acc):
            acc = acc + _dot(v_ref[nk - 1], p_scr[(nk - 1) % 2])
            l = acc[HEAD_DIM:HEAD_DIM + 1]
            o_ref[...] = acc[:HEAD_DIM] / l
            lse_rows = _shift_rows((HEAD_DIM, tq), *_hi_lo(m + jnp.log2(l)))
            qtok_ref[...] = jnp.concatenate([q.astype(F32), lse_rows], axis=0).T.astype(BF16)

        @pl.when(safe)
        def _():
            m = jnp.full((1, tq), head_bound, F32)

            def step(j, slot, acc):
                s = s_scr[slot]
                pv = matmuls(j, slot)
                p_scr[slot] = jnp.exp2(s - m).astype(BF16)
                return acc + pv

            finish(m, _loop_pairs(nk, step, zero))

        @pl.when(jnp.logical_not(safe))
        def _():
            def step(j, slot, carry):
                m, acc = carry
                s = s_scr[slot]
                pv = matmuls(j, slot)
                m_new = jnp.maximum(m, jnp.max(s, axis=0, keepdims=True))
                p_scr[slot] = jnp.exp2(s - m_new).astype(BF16)
                return m_new, jnp.exp2(m - m_new) * (acc + pv)

            finish(*_loop_pairs(nk, step, (jnp.full((1, tq), -1e30, F32), zero)))

    return _pallas(
        comm, body, name="attention_fwd", grid=(N_HEADS, nq),
        in_specs=[pl.BlockSpec(memory_space=pltpu.SMEM),
                  pl.BlockSpec((None, None, HEAD_DIM, tq), lambda h, i: (h, i, 0, 0)),
                  pl.BlockSpec((None, nk, tk, HEAD_DIM), lambda h, i: (h // KV_GROUP, 0, 0, 0)),
                  pl.BlockSpec((None, nk, HEAD_DIM + ONES_ROWS, tk), lambda h, i: (h // KV_GROUP, 0, 0, 0))],
        out_specs=[pl.BlockSpec((None, None, HEAD_DIM, tq), lambda h, i: (h, i, 0, 0)),
                   pl.BlockSpec((None, None, tq, 2 * HEAD_DIM), lambda h, i: (h, i, 0, 0))],
        out_shape=[jax.ShapeDtypeStruct((N_HEADS, nq, HEAD_DIM, tq), F32),
                   jax.ShapeDtypeStruct((N_HEADS, nq, tq, 2 * HEAD_DIM), BF16)],
        scratch_shapes=[pltpu.VMEM((2, tk, tq), F32), pltpu.VMEM((2, tk, tq), BF16)],
        args=(bound, q_t, k, v_t))


def attention_bwd(q_tok, do_tok, q_t, do_t, k_t, v_t, comm=None):
    _, nq, _, tq = q_t.shape
    _, nk, _, tk = k_t.shape

    def body(qtok_ref, dotok_ref, q_ref, do_ref, kt_ref, vt_ref, dq_ref, dk_ref, dv_ref,
             s_scr, dp_scr, p_scr, ds_scr):
        @pl.when(pl.program_id(1) == 0)
        def _():
            dq_ref[...] = jnp.zeros_like(dq_ref)

        kt_aug, vt_aug = kt_ref[...], vt_ref[...]
        kt = kt_aug[:HEAD_DIM]
        n = KV_GROUP * nq
        s_scr[0] = _dot(qtok_ref[0, 0], kt_aug)
        dp_scr[0] = _dot(dotok_ref[0, 0], vt_aug)
        p_scr[1] = jnp.zeros((tq, tk), BF16)
        ds_scr[1] = jnp.zeros((tq, tk), BF16)

        def products(t, slot, dk, dv):
            h, i = t // nq, t % nq
            ds = ds_scr[slot]
            dq_ref[h, i] += _dot_nt(kt, ds)
            return dk + _dot(q_ref[h, i], ds), dv + _dot(do_ref[h, i], p_scr[slot])

        def step(t, slot, carry):
            s, dp = s_scr[slot], dp_scr[slot]
            dk, dv = products(jnp.maximum(t - 1, 0), 1 - slot, *carry)
            nxt = jnp.minimum(t + 1, n - 1)
            s_scr[1 - slot] = _dot(qtok_ref[nxt // nq, nxt % nq], kt_aug)
            dp_scr[1 - slot] = _dot(dotok_ref[nxt // nq, nxt % nq], vt_aug)
            p = jnp.exp2(s)
            p_scr[slot] = p.astype(BF16)
            ds_scr[slot] = (p * dp).astype(BF16)
            return dk, dv

        zero = jnp.zeros((HEAD_DIM, tk), F32)
        dk, dv = products(n - 1, (n - 1) % 2, *_loop_pairs(n, step, (zero, zero)))
        dk_ref[...] = dk * (1.0 / LOG2_E)
        dv_ref[...] = dv

    group = lambda g, j: (g, 0, 0, 0)
    tile = lambda g, j: (g, j, 0, 0)
    once = pl.Buffered(1)
    return _pallas(
        comm, body, name="attention_bwd", grid=(N_KV_HEADS, nk),
        in_specs=[pl.BlockSpec((KV_GROUP, nq, tq, 2 * HEAD_DIM), group, pipeline_mode=once),
                  pl.BlockSpec((KV_GROUP, nq, tq, 2 * HEAD_DIM), group, pipeline_mode=once),
                  pl.BlockSpec((KV_GROUP, nq, HEAD_DIM, tq), group, pipeline_mode=once),
                  pl.BlockSpec((KV_GROUP, nq, HEAD_DIM, tq), group, pipeline_mode=once),
                  pl.BlockSpec((None, None, 2 * HEAD_DIM, tk), tile),
                  pl.BlockSpec((None, None, 2 * HEAD_DIM, tk), tile)],
        out_specs=[pl.BlockSpec((KV_GROUP, nq, HEAD_DIM, tq), group),
                   pl.BlockSpec((None, None, HEAD_DIM, tk), tile),
                   pl.BlockSpec((None, None, HEAD_DIM, tk), tile)],
        out_shape=[jax.ShapeDtypeStruct((N_HEADS, nq, HEAD_DIM, tq), F32),
                   jax.ShapeDtypeStruct((N_KV_HEADS, nk, HEAD_DIM, tk), F32),
                   jax.ShapeDtypeStruct((N_KV_HEADS, nk, HEAD_DIM, tk), F32)],
        scratch_shapes=[pltpu.VMEM((2, tq, tk), F32), pltpu.VMEM((2, tq, tk), F32),
                        pltpu.VMEM((2, tq, tk), BF16), pltpu.VMEM((2, tq, tk), BF16)],
        args=(q_tok, do_tok, q_t, do_t, k_t, v_t))


def _group_matmul(a_t, w_ref):
    return jnp.concatenate([_dot(a_t[g * SGU_GROUP_DIM:(g + 1) * SGU_GROUP_DIM], w_ref[g])
                            for g in range(N_SGU_GROUPS)], axis=0)


def _gate_forward(z, g_sgu, wst_ref, bias):
    gz, th = _gelu(z)
    u, vv = gz[:, :D_SGU], gz[:, D_SGU:]
    rv = _rstd(vv)
    nv = vv * rv
    vn = nv * g_sgu
    v_chunks, fs = [], []
    for c in range(z.shape[0] // CHUNK):
        vt = vn[c * CHUNK:(c + 1) * CHUNK].T.astype(BF16)
        v_chunks.append(vt)
        fs.append(_group_matmul(vt, wst_ref).T + bias)
    f = jnp.concatenate(fs, axis=0) if len(fs) > 1 else fs[0]
    return th, u, rv, nv, v_chunks, f


def mix_out(z, o, x, g_sgu, g_ao, g_so, ws_t, bias, w_out, tm):
    t = x.shape[0]

    def body(z_ref, o_ref, x_ref, gs_ref, gao_ref, gso_ref, ws_ref, bias_ref, wout_ref, x2_ref, mixed_ref):
        _, u, _, _, _, f = _gate_forward(z_ref[...], gs_ref[...], ws_ref, bias_ref[...])
        sgu = u * f
        oo = _from_head_tiles(o_ref[...])
        mixed = jnp.concatenate([oo * _rstd(oo) * gao_ref[...], sgu * _rstd(sgu) * gso_ref[...]], axis=-1).astype(BF16)
        mixed_ref[...] = mixed
        x2_ref[...] = x_ref[...] + _dot(mixed, wout_ref[...])

    row = lambda n: pl.BlockSpec((tm, n), lambda i: (i, 0))
    return pl.pallas_call(
        functools.partial(body), name="mix_out", grid=(t // tm,),
        in_specs=[row(2 * D_SGU), _head_tile_spec(tm, HEAD_DIM), row(D_MODEL), _full((1, D_SGU)), _full((1, D_ATTN)),
                  _full((1, D_SGU)),
                  _full((N_SGU_GROUPS, CHUNK, CHUNK)), _full((CHUNK, D_SGU)), _full((D_MODEL, D_MODEL))],
        out_specs=[row(D_MODEL), row(D_MODEL)],
        out_shape=[jax.ShapeDtypeStruct((t, D_MODEL), F32), jax.ShapeDtypeStruct((t, D_MODEL), BF16)],
        compiler_params=_params(1),
    )(z, o, x, g_sgu, g_ao, g_so, ws_t, bias, w_out)


def mix_bwd(dx2, z, o, g_sgu, g_ao, g_so, ws, ws_t, bias, w_out, group_ind, tm):
    t = dx2.shape[0]
    n_tiles = t // tm

    def body(dx_ref, z_ref, o_ref, gs_ref, gao_ref, gso_ref, ws_ref, wst_ref, bias_ref, wout_ref, ind_ref,
             do_ref, dotok_ref, dz_ref, dg_ref, dws_ref, dbs_ref, df_sum):
        step = pl.program_id(0)

        @pl.when(step == 0)
        def _():
            dg_ref[...] = jnp.zeros_like(dg_ref)
            dws_ref[...] = jnp.zeros_like(dws_ref)
            df_sum[...] = jnp.zeros_like(df_sum)

        z = z_ref[...]
        th, u, rv, nv, v_chunks, f = _gate_forward(z, gs_ref[...], wst_ref, bias_ref[...])
        dmixed = _dot_nt(dx_ref[...].astype(BF16), wout_ref[...])
        o_tiles = o_ref[...]
        oo = _from_head_tiles(o_tiles)
        ro = _rstd(oo)
        d_o, dgao = _rms_bwd(dmixed[:, :D_ATTN], oo * ro, ro, gao_ref[...])
        do_tiles = _to_head_tiles(d_o)
        do_ref[...] = do_tiles.astype(BF16)
        delta_hi, delta_lo = _hi_lo(jnp.sum(do_tiles * o_tiles, axis=1, keepdims=True))
        for h in range(N_HEADS):
            delta_rows = _shift_rows((HEAD_DIM, tm), delta_hi[h], delta_lo[h])
            dotok_ref[h] = jnp.concatenate([do_tiles[h], delta_rows], axis=0).T.astype(BF16)
        sgu = u * f
        rs = _rstd(sgu)
        dsgu, dgso = _rms_bwd(dmixed[:, D_ATTN:], sgu * rs, rs, gso_ref[...])
        du = dsgu * f
        df = dsgu * u
        dvns = []
        df_acc = jnp.zeros((CHUNK, D_SGU), F32)
        for c in range(tm // CHUNK):
            dfc32 = df[c * CHUNK:(c + 1) * CHUNK]
            dft = dfc32.T.astype(BF16)
            dvns.append(_group_matmul(dft, ws_ref).T)
            for g in range(N_SGU_GROUPS):
                rows = slice(g * SGU_GROUP_DIM, (g + 1) * SGU_GROUP_DIM)
                dws_ref[g] += _dot_tn(dft[rows], v_chunks[c][rows])
            df_acc = df_acc + dfc32
        df_sum[...] += df_acc
        dvn = jnp.concatenate(dvns, axis=0) if len(dvns) > 1 else dvns[0]
        dvv, dgs = _rms_bwd(dvn, nv, rv, gs_ref[...])
        dz_ref[...] = (jnp.concatenate([du, dvv], axis=-1) * _gelu_grad(z, th)).astype(BF16)
        dg_ref[0:1, :] += _colsum(dgao)
        dg_ref[1:2, :] += _colsum(dgso)
        dg_ref[2:3, :] += _colsum(dgs)

        @pl.when(step == n_tiles - 1)
        def _():
            dbs_ref[...] = _dot_f32(df_sum[...], ind_ref[...])

    row = lambda n: pl.BlockSpec((tm, n), lambda i: (i, 0))
    return pl.pallas_call(
        functools.partial(body), name="mix_bwd", grid=(n_tiles,),
        in_specs=[row(D_MODEL), row(2 * D_SGU), _head_tile_spec(tm, HEAD_DIM), _full((1, D_SGU)), _full((1, D_ATTN)),
                  _full((1, D_SGU)),
                  _full((N_SGU_GROUPS, CHUNK, CHUNK)), _full((N_SGU_GROUPS, CHUNK, CHUNK)), _full((CHUNK, D_SGU)),
                  _full((D_MODEL, D_MODEL)), _full((D_SGU, LANES))],
        out_specs=[_head_tile_spec(tm, HEAD_DIM), pl.BlockSpec((N_HEADS, None, tm, 2 * HEAD_DIM), lambda i: (0, i, 0, 0)),
                   row(2 * D_SGU), _full((8, D_SGU)),
                   _full((N_SGU_GROUPS, CHUNK, CHUNK)), _full((CHUNK, LANES))],
        out_shape=[jax.ShapeDtypeStruct((N_HEADS, n_tiles, HEAD_DIM, tm), BF16),
                   jax.ShapeDtypeStruct((N_HEADS, n_tiles, tm, 2 * HEAD_DIM), BF16),
                   jax.ShapeDtypeStruct((t, 2 * D_SGU), BF16),
                   jax.ShapeDtypeStruct((8, D_SGU), F32),
                   jax.ShapeDtypeStruct((N_SGU_GROUPS, CHUNK, CHUNK), F32),
                   jax.ShapeDtypeStruct((CHUNK, LANES), F32)],
        scratch_shapes=[pltpu.VMEM((CHUNK, D_SGU), F32)],
        compiler_params=_params(1),
    )(dx2, z, o, g_sgu, g_ao, g_so, ws, ws_t, bias, w_out, group_ind)


def ffn_down_loss(act, wd, x, g, target, tm):
    t = x.shape[0]

    def body(act_ref, wd_ref, x_ref, g_ref, t_ref, loss_ref, dx_ref, dg_ref):
        @pl.when(pl.program_id(0) == 0)
        def _():
            loss_ref[...] = jnp.zeros_like(loss_ref)
            dg_ref[...] = jnp.zeros_like(dg_ref)

        xx = x_ref[...] + 0.5 * _dot(act_ref[...], wd_ref[...])
        r = _rstd(xx)
        n = xx * r
        err = n * g_ref[...] - t_ref[...]
        per_token = jnp.mean(err * err, axis=-1, keepdims=True)
        loss_ref[...] += 0.5 * jnp.sum(per_token, axis=0, keepdims=True)
        dx, dg_rows = _rms_bwd(err * (1.0 / D_MODEL), n, r, g_ref[...])
        dx_ref[...] = dx
        dg_ref[...] += _colsum(dg_rows)

    row = pl.BlockSpec((tm, D_MODEL), lambda i: (i, 0))
    return pl.pallas_call(
        functools.partial(body), name="ffn_down_loss", grid=(t // tm,),
        in_specs=[pl.BlockSpec((tm, D_FF), lambda i: (i, 0)), _full((D_FF, D_MODEL)), row, _full((1, D_MODEL)), row],
        out_specs=[_full((1, LANES)), row, _full((1, D_MODEL))],
        out_shape=[jax.ShapeDtypeStruct((1, LANES), F32), jax.ShapeDtypeStruct((t, D_MODEL), F32),
                   jax.ShapeDtypeStruct((1, D_MODEL), F32)],
        compiler_params=_params(1),
    )(act, wd, x, g, target)


def _rope_tables(t):
    rows = t // GRID_W
    row_idx = jnp.repeat(jnp.arange(rows, dtype=F32), GRID_W)
    col_idx = jnp.tile(jnp.arange(GRID_W, dtype=F32), rows)
    axis_dim = HEAD_DIM // 2
    inv = 1.0 / (ROPE_THETA ** (jnp.arange(0, axis_dim, 2, dtype=F32) / axis_dim))
    ang = jnp.concatenate([row_idx[:, None] * inv, col_idx[:, None] * inv], axis=-1)
    cos = jnp.repeat(jnp.cos(ang), 2, axis=-1)
    sin = jnp.repeat(jnp.sin(ang), 2, axis=-1) * jnp.tile(jnp.array([-1.0, 1.0], F32), HEAD_DIM // 2)
    return jnp.tile(cos, (1, LANES // HEAD_DIM)), jnp.tile(sin, (1, LANES // HEAD_DIM))


def kernel(x, g_ffn1, w1_gate, w1_up, w1_down, g_mix, w_in, g_q, g_k, g_sgu, w_s, b_s, g_attn_out, g_sgu_out, w_out, g_ffn2, w2_gate, w2_up, w2_down, g_final, loss_target, m_g_ffn1, m_w1_gate, m_w1_up, m_w1_down, m_g_mix, m_w_in, m_g_q, m_g_k, m_g_sgu, m_w_s, m_b_s, m_g_attn_out, m_g_sgu_out, m_w_out, m_g_ffn2, m_w2_gate, m_w2_up, m_w2_down, m_g_final, v_g_ffn1, v_w1_gate, v_w1_up, v_w1_down, v_g_mix, v_w_in, v_g_q, v_g_k, v_g_sgu, v_w_s, v_b_s, v_g_attn_out, v_g_sgu_out, v_w_out, v_g_ffn2, v_w2_gate, v_w2_up, v_w2_down, v_g_final):
    weights = dict(g_ffn1=g_ffn1, w1_gate=w1_gate, w1_up=w1_up, w1_down=w1_down, g_mix=g_mix, w_in=w_in, g_q=g_q,
                   g_k=g_k, g_sgu=g_sgu, w_s=w_s, b_s=b_s, g_attn_out=g_attn_out, g_sgu_out=g_sgu_out, w_out=w_out,
                   g_ffn2=g_ffn2, w2_gate=w2_gate, w2_up=w2_up, w2_down=w2_down, g_final=g_final)
    m_in = dict(g_ffn1=m_g_ffn1, w1_gate=m_w1_gate, w1_up=m_w1_up, w1_down=m_w1_down, g_mix=m_g_mix, w_in=m_w_in,
                g_q=m_g_q, g_k=m_g_k, g_sgu=m_g_sgu, w_s=m_w_s, b_s=m_b_s, g_attn_out=m_g_attn_out,
                g_sgu_out=m_g_sgu_out, w_out=m_w_out, g_ffn2=m_g_ffn2, w2_gate=m_w2_gate, w2_up=m_w2_up,
                w2_down=m_w2_down, g_final=m_g_final)
    v_in = dict(g_ffn1=v_g_ffn1, w1_gate=v_w1_gate, w1_up=v_w1_up, w1_down=v_w1_down, g_mix=v_g_mix, w_in=v_w_in,
                g_q=v_g_q, g_k=v_g_k, g_sgu=v_g_sgu, w_s=v_w_s, b_s=v_b_s, g_attn_out=v_g_attn_out,
                g_sgu_out=v_g_sgu_out, w_out=v_w_out, g_ffn2=v_g_ffn2, w2_gate=v_w2_gate, w2_up=v_w2_up,
                w2_down=v_w2_down, g_final=v_g_final)
    names = list(weights)

    t = x.shape[1]
    x0 = x[0]
    target = loss_target[0]
    tm = min(512, t)
    tm_ff = min(256, t)
    tn_ff = 256
    tq = min(512, t)
    tk = min(256, t)
    tk_fwd = min(512, t)
    tk_w = min(2048, t)

    def shard_rows(name):
        w = weights[name][0]
        return (w.T if name in TRANSPOSED else w).astype(BF16)

    rows_of = dict(SHARD_ROWS)
    full = {}

    def packed(group):
        return jnp.concatenate([shard_rows(n) for n in group], axis=0), [rows_of[n] for n in group]

    def gather_of(group):
        return gather_exchange(*packed(group))

    def take(group, gathered):
        for n, g in zip(group, gathered):
            full[n] = g.reshape(N_DEV * rows_of[n], D_MODEL)

    first, second, third = ("w1_gate", "w1_up"), ("w1_down", "w_in", "w_out"), ("w2_gate", "w2_up", "w2_down")
    take(first, gather_two_level(*packed(first), "gather_first"))

    (h1, a1, b1, act1), gathered = ffn_up(x0, g_ffn1, full["w1_gate"], full["w1_up"], tm_ff, tn_ff, gather_of(second))
    take(second, gathered)
    w_in_t = full["w_in"]
    w_qkv_t, w_z_t = w_in_t[:D_QKV], w_in_t[D_QKV:]
    x1 = ffn_down(act1, full["w1_down"], x0, tm)

    qkv, z, h2 = input_projection(x1, g_mix, w_qkv_t, w_z_t, tm)
    cos_w, sin_w = _rope_tables(t)
    gq_w = jnp.tile(g_q, (1, N_HEADS))
    gk_w = jnp.tile(g_k, (1, N_KV_HEADS))
    mean_q, mean_k = _head_mean_matrix(D_ATTN).astype(BF16), _head_mean_matrix(D_KV).astype(BF16)
    q_t, k_tiles, kt_tiles, vt_tiles, vt_tiles_bwd, q_sq_max, k_sq_max = qk_prep(
        qkv, gq_w, gk_w, cos_w, sin_w, mean_q, mean_k, tq, tk, tk_fwd)
    score_bound = 1.02 * jnp.sqrt(q_sq_max.reshape(N_HEADS, HEAD_DIM)[:, 0]
                                  * jnp.repeat(k_sq_max.reshape(N_KV_HEADS, HEAD_DIM)[:, 0], KV_GROUP))
    k_tiles_fwd = k_tiles.reshape(N_KV_HEADS, t // tk_fwd, tk_fwd, HEAD_DIM)
    (o_t, q_tok), gathered = attention_fwd(score_bound, q_t, k_tiles_fwd, vt_tiles, gather_of(third))
    take(third, gathered)

    ws_b = w_s[0].astype(BF16)
    ws_tb = jnp.swapaxes(w_s[0], 1, 2).astype(BF16)
    bias = jnp.repeat(b_s[0].T, SGU_GROUP_DIM, axis=1)
    x2, mixed = mix_out(z, o_t, x1, g_sgu, g_attn_out, g_sgu_out, ws_tb, bias, full["w_out"], tq)

    (h3, a2, b2, act2), _ = ffn_up(x2, g_ffn2, full["w2_gate"], full["w2_up"], tm_ff, tn_ff)

    loss_part, dx3, dg_final = ffn_down_loss(act2, full["w2_down"], x2, g_final, target, tm)

    tmm = D_FF // 2
    (da2, db2), _ = ffn_bwd_act(dx3, full["w2_down"], a2, b2, tm, tn_ff)
    (dx2, dg_ffn2), _ = norm_bwd_matmul(da2, full["w2_gate"], db2, full["w2_up"], x2, g_ffn2, dx3, tm)
    dwg2, dwu2 = matmul_tn(da2, h3, 1.0, tmm, tk_w), matmul_tn(db2, h3, 1.0, tmm, tk_w)
    dwd2 = matmul_tn(act2, dx3, 0.5, tmm, tk_w)

    group_ind = (jnp.arange(D_SGU)[:, None] // SGU_GROUP_DIM == jnp.arange(LANES)[None, :]).astype(F32)
    do_t, do_tok, dz, dg_mixrow, dws, dbs = mix_bwd(dx2, z, o_t, g_sgu, g_attn_out, g_sgu_out, ws_b, ws_tb, bias,
                                                    full["w_out"], group_ind, tq)
    dw_out = matmul_tn(mixed, dx2, 1.0, D_MODEL // 2, tk_w)

    group_a = ("w2_gate", "w2_up", "w2_down", "w_out")
    (dq_t, dk_t, dv_t), (parts_a,) = attention_bwd(q_tok, do_tok, q_t, do_t, kt_tiles, vt_tiles_bwd,
                                                   scatter_exchange([dwg2, dwu2, dwd2, dw_out]))
    dqkv, dgq_w, dgk_w = qk_bwd(dq_t, dk_t, dv_t, qkv, gq_w, gk_w, cos_w, sin_w, mean_q, mean_k, tq)

    def pack_small(arrays):
        pieces = []
        for a in arrays:
            flat = a.reshape(-1)
            pieces.append(jnp.pad(flat, (0, (-flat.shape[0]) % (8 * LANES))).reshape(-1, LANES))
        return jnp.concatenate(pieces, axis=0), [p.shape[0] for p in pieces]

    early = dict(g_ffn2=dg_ffn2, g_final=dg_final, g_q=dgq_w.reshape(N_HEADS, HEAD_DIM).sum(0),
                 g_k=dgk_w.reshape(N_KV_HEADS, HEAD_DIM).sum(0), g_attn_out=dg_mixrow[0], g_sgu_out=dg_mixrow[1],
                 g_sgu=dg_mixrow[2], w_s=dws, b_s=dbs[:, :N_SGU_GROUPS].T)
    early_pack, early_rows = pack_small(list(early.values()))
    (dx1, dg_mix), (early_parts,) = norm_bwd_matmul(dqkv, w_qkv_t, dz, w_z_t, x1, g_mix, dx2, tm,
                                                    gather_exchange(early_pack, [early_pack.shape[0]]))
    dw_in = jnp.concatenate([matmul_tn(dqkv, h2, 1.0, D_QKV // 2, tk_w), matmul_tn(dz, h2, 1.0, D_SGU, tk_w)], axis=0)

    dwd1, (parts_in,) = matmul_tn(act1, dx1, 0.5, tmm, tk_w, scatter_exchange([dw_in]))
    (da1, db1), (parts_d1,) = ffn_bwd_act(dx1, full["w1_down"], a1, b1, tm, tn_ff, scatter_exchange([dwd1]))
    dwg1 = matmul_tn(da1, h1, 1.0, tmm, tk_w)
    dwu1, (parts_g1,) = matmul_tn(db1, h1, 1.0, tmm, tk_w, scatter_exchange([dwg1]))
    (dx0, dg_ffn1), (parts_u1,) = norm_bwd_matmul(da1, full["w1_gate"], db1, full["w1_up"], x0, g_ffn1, dx1, tm,
                                                  scatter_exchange([dwu1]))
    scattered = ((group_a, parts_a), (("w_in",), parts_in), (("w1_down",), parts_d1), (("w1_gate",), parts_g1),
                 (("w1_up",), parts_u1))

    late = dict(g_mix=dg_mix, g_ffn1=dg_ffn1, loss=loss_part)
    late_pack, late_rows = pack_small(list(late.values()))
    (late_parts,) = run_exchange(gather_exchange(late_pack, [late_pack.shape[0]]), "gather_late_small_grads")
    small_sums = {}
    for entries, rows, parts in ((early, early_rows, early_parts), (late, late_rows, late_parts)):
        summed = sum_parts(parts, parts.shape[1])
        off = 0
        for n, r in zip(entries, rows):
            small_sums[n] = summed[off:off + r]
            off += r
    loss = small_sums.pop("loss")[0, 0]

    grads, row_grads = {}, {}
    for group, parts in scattered:
        rows = parts.shape[1]
        summed = sum_parts(parts, rows if rows <= 2 * rows_of["w1_gate"] else rows // 2)
        off = 0
        for n in group:
            row_grads[n] = summed[off:off + rows_of[n]]
            grads[n] = (row_grads[n].T if n in TRANSPOSED else row_grads[n])[None]
            off += rows_of[n]
    for n, summed in small_sums.items():
        grads[n] = summed.reshape(-1)[:weights[n].size].reshape(weights[n].shape)

    delta_w, new_m, new_v = {}, {}, {}
    for n in names:
        shape = weights[n].shape
        if n in TRANSPOSED:
            view, unview, g = (lambda a: a[0].T), (lambda a: a.T[None]), row_grads[n]
        else:
            view, unview = (lambda a: a.reshape(-1, shape[-1])), (lambda a: a.reshape(shape))
            g = view(grads[n])
        d, m2, v2 = adamw(view(weights[n]), g, view(m_in[n]), view(v_in[n]))
        delta_w[n], new_m[n], new_v[n] = unview(d), unview(m2), unview(v2)

    return (loss, dx0[None], *[grads[n] for n in names], *[delta_w[n] for n in names],
            *[new_m[n] for n in names], *[new_v[n] for n in names])
```

```python
import functools
import math

import jax
import jax.numpy as jnp
from jax import lax
from jax.experimental import pallas as pl
from jax.experimental.pallas import tpu as pltpu

F32 = jnp.float32
BF16 = jnp.bfloat16

D_MODEL = 1024
D_FF = 2816
N_HEADS = 8
HEAD_DIM = 64
N_KV_HEADS = 2
KV_GROUP = N_HEADS // N_KV_HEADS
D_ATTN = N_HEADS * HEAD_DIM
D_KV = N_KV_HEADS * HEAD_DIM
D_QKV = D_ATTN + 2 * D_KV
N_SGU_GROUPS = 8
SGU_GROUP_DIM = 64
D_SGU = N_SGU_GROUPS * SGU_GROUP_DIM
CHUNK = 128
GRID_W = 64
ROPE_THETA = 10000.0
EPS = 1e-6
N_DEV = 8
LANES = 128

ONES_ROWS = 16
SAFE_SCORE_BOUND = 60.0
LOG2_E = math.log2(math.e)
Q_SCALE = HEAD_DIM ** -0.5 * LOG2_E

ADAM_LR = 0.001
ADAM_B1 = 0.9
ADAM_B2 = 0.999
ADAM_EPS = 1e-08
ADAM_WD = 0.01
ADAM_STEP = 10

MESH_IDS = pl.DeviceIdType.MESH

VMEM_LIMIT = 56 * 1024 * 1024

SHARD_ROWS = (("w1_gate", D_FF // N_DEV), ("w1_up", D_FF // N_DEV), ("w1_down", D_FF // N_DEV),
              ("w_in", (D_QKV + 2 * D_SGU) // N_DEV), ("w_out", D_MODEL // N_DEV),
              ("w2_gate", D_FF // N_DEV), ("w2_up", D_FF // N_DEV), ("w2_down", D_FF // N_DEV))
TRANSPOSED = ("w1_gate", "w1_up", "w_in", "w2_gate", "w2_up")


def _params(n_grid):
    return pltpu.CompilerParams(dimension_semantics=("arbitrary",) * n_grid, vmem_limit_bytes=VMEM_LIMIT)


def _dot(a, b):
    return jnp.dot(a, b, preferred_element_type=F32)


def _dot_nt(a, b):
    return lax.dot_general(a, b, (((1,), (1,)), ((), ())), preferred_element_type=F32)


def _dot_tn(a, b):
    return lax.dot_general(a, b, (((0,), (0,)), ((), ())), preferred_element_type=F32)


def _dot_f32(a, b):
    return jnp.dot(a, b, preferred_element_type=F32, precision=lax.Precision.HIGHEST)


def _dot_split(a, b):
    hi = a.astype(BF16)
    lo = (a - hi.astype(F32)).astype(BF16)
    return _dot(hi, b) + _dot(lo, b)


def _rstd(x):
    return lax.rsqrt(jnp.mean(x * x, axis=-1, keepdims=True) + EPS)


def _rms_bwd(dy, n, r, g):
    dn = dy * g
    return r * (dn - n * jnp.mean(dn * n, axis=-1, keepdims=True)), dy * n


def _colsum(a):
    return jnp.sum(a, axis=0, keepdims=True)


_GELU_C = math.sqrt(2.0 / math.pi)


def _gelu(x):
    t = jnp.tanh(_GELU_C * (x + 0.044715 * (x * x * x)))
    return x * (0.5 * (1.0 + t)), t


def _gelu_grad(x, t):
    return 0.5 * (1.0 + t) + 0.5 * x * (1.0 - t * t) * (_GELU_C * (1.0 + 3 * 0.044715 * x * x))


def _pair_swap(a):
    w = a.shape[-1]
    lane = lax.broadcasted_iota(jnp.int32, a.shape, a.ndim - 1)
    return jnp.where(lane % 2 == 0, pltpu.roll(a, w - 1, a.ndim - 1), pltpu.roll(a, 1, a.ndim - 1))


def _tile_lanes(a, reps):
    return jnp.concatenate([a] * reps, axis=-1) if reps > 1 else a


def _loop_pairs(n, step, carry):
    assert n % 2 == 0, n

    def pair(jj, c):
        return step(2 * jj + 1, 1, step(2 * jj, 0, c))

    return lax.fori_loop(0, n // 2, pair, carry)


def _full(shape):
    nd = len(shape)
    return pl.BlockSpec(shape, lambda *_: (0,) * nd)


def _mesh_pos():
    return lax.axis_index("x"), lax.axis_index("y"), lax.axis_index("c")


def _peer(pos, d):
    x, y, c = pos
    px = 1 - x if d & 4 else x
    py = 1 - y if d & 2 else y
    pc = 1 - c if d & 1 else c
    return (px, py, pc), 4 * px + 2 * py + pc


class _Exchange:
    def __init__(self, operands, out_shape, n_local, plan):
        self.operands = list(operands)
        self.out_shape = list(out_shape)
        self.sem_shapes = [pltpu.SemaphoreType.DMA((N_DEV - 1,)), pltpu.SemaphoreType.DMA((N_DEV - 1,)),
                           pltpu.SemaphoreType.DMA((n_local,))]
        self._plan = plan

    def _copies(self, in_refs, out_refs):
        pos = _mesh_pos()
        return pos, self._plan(4 * pos[0] + 2 * pos[1] + pos[2], in_refs, out_refs)

    def start(self, in_refs, out_refs, sems):
        send_sems, recv_sems, local_sems = sems
        pos, (local, remote, _) = self._copies(in_refs, out_refs)
        for k, (src, dst) in enumerate(local):
            pltpu.make_async_copy(src, dst, local_sems.at[k]).start()
        for d in range(1, N_DEV):
            peer, peer_lin = _peer(pos, d)
            for src, dst in remote(peer_lin):
                pltpu.make_async_remote_copy(src_ref=src, dst_ref=dst, send_sem=send_sems.at[d - 1],
                                             recv_sem=recv_sems.at[d - 1], device_id=peer,
                                             device_id_type=MESH_IDS).start()

    def wait(self, in_refs, out_refs, sems):
        send_sems, recv_sems, local_sems = sems
        pos, (local, _, whole) = self._copies(in_refs, out_refs)
        for d in range(1, N_DEV):
            peer, peer_lin = _peer(pos, d)
            ref = whole(peer_lin)
            everything = pltpu.make_async_remote_copy(src_ref=ref, dst_ref=ref, send_sem=send_sems.at[d - 1],
                                                      recv_sem=recv_sems.at[d - 1], device_id=peer,
                                                      device_id_type=MESH_IDS)
            everything.wait_send()
            everything.wait_recv()
        for k, (src, dst) in enumerate(local):
            pltpu.make_async_copy(src, dst, local_sems.at[k]).wait()


def _offsets(rows):
    offs, o = [], 0
    for r in rows:
        offs.append(o)
        o += r
    return offs


def gather_exchange(src, rows):
    offs = _offsets(rows)

    def plan(me, in_refs, out_refs):
        pieces = [(in_refs[0].at[pl.ds(o, r)], out.at[me]) for o, r, out in zip(offs, rows, out_refs)]
        return pieces, (lambda peer_lin: pieces), (lambda peer_lin: in_refs[0])

    return _Exchange([src], [jax.ShapeDtypeStruct((N_DEV, r) + src.shape[1:], src.dtype) for r in rows],
                     len(rows), plan)


def scatter_exchange(grads):
    rows = [g.shape[0] // N_DEV for g in grads]
    offs = _offsets(rows)

    def plan(me, in_refs, out_refs):
        parts = out_refs[0]

        def slabs(owner):
            return [(g.at[pl.ds(pl.multiple_of(owner * r, 16), r)], parts.at[me, pl.ds(o, r)])
                    for g, o, r in zip(in_refs, offs, rows)]

        return slabs(me), slabs, (lambda peer_lin: parts.at[peer_lin])

    shape = jax.ShapeDtypeStruct((N_DEV, sum(rows)) + grads[0].shape[1:], grads[0].dtype)
    return _Exchange(grads, [shape], len(rows), plan)


def gather_two_level(src, rows, name):
    offs = _offsets(rows)
    n_p = len(rows)

    def body(src_ref, *refs):
        outs, (send_sems, recv_sems, local_sems) = refs[:n_p], refs[n_p:]
        x, y, c = _mesh_pos()
        me, sibling = (x, y, c), (x, y, 1 - c)
        chips = [(1 - x, y), (x, 1 - y), (1 - x, 1 - y)]

        def slab(w, dev):
            return outs[w].at[4 * dev[0] + 2 * dev[1] + dev[2]]

        def copy(w, k, block, to, from_src=False):
            return pltpu.make_async_remote_copy(
                src_ref=src_ref.at[pl.ds(offs[w], rows[w])] if from_src else slab(w, block), dst_ref=slab(w, block),
                send_sem=send_sems.at[w * 7 + k], recv_sem=recv_sems.at[w * 7 + k],
                device_id=to, device_id_type=MESH_IDS)

        mine = [pltpu.make_async_copy(src_ref.at[pl.ds(offs[w], rows[w])], slab(w, me), local_sems.at[w])
                for w in range(n_p)]
        for cp in mine:
            cp.start()
        first = []
        for w in range(n_p):
            first.append(copy(w, 0, me, sibling, True))
            first += [copy(w, 1 + j, me, (*chip, c), True) for j, chip in enumerate(chips)]
        for cp in first:
            cp.start()
        passed = []
        for j, chip in enumerate(chips):
            for w in range(n_p):
                copy(w, 1 + j, (*chip, c), me).wait_recv()
                cp = copy(w, 4 + j, (*chip, c), sibling)
                cp.start()
                passed.append(cp)
        for w in range(n_p):
            copy(w, 0, sibling, me).wait_recv()
            for j, chip in enumerate(chips):
                copy(w, 4 + j, (*chip, 1 - c), me).wait_recv()
        for cp in first + passed:
            cp.wait_send()
        for cp in mine:
            cp.wait()

    any_spec = pl.BlockSpec(memory_space=pl.ANY)
    return pl.pallas_call(
        functools.partial(body), name=name,
        out_shape=[jax.ShapeDtypeStruct((N_DEV, r) + src.shape[1:], src.dtype) for r in rows],
        in_specs=[any_spec], out_specs=[any_spec] * n_p,
        scratch_shapes=[pltpu.SemaphoreType.DMA((7 * n_p,)), pltpu.SemaphoreType.DMA((7 * n_p,)),
                        pltpu.SemaphoreType.DMA((n_p,))],
        compiler_params=pltpu.CompilerParams(has_side_effects=True),
    )(src)


def run_exchange(ex, name):
    n_in, n_out = len(ex.operands), len(ex.out_shape)

    def body(*refs):
        parts = refs[:n_in], refs[n_in:n_in + n_out], refs[n_in + n_out:]
        ex.start(*parts)
        ex.wait(*parts)

    any_spec = pl.BlockSpec(memory_space=pl.ANY)
    return pl.pallas_call(
        functools.partial(body), name=name, out_shape=ex.out_shape,
        in_specs=[any_spec] * n_in, out_specs=[any_spec] * n_out, scratch_shapes=ex.sem_shapes,
        compiler_params=pltpu.CompilerParams(has_side_effects=True),
    )(*ex.operands)


def _pallas(comm, body, *, name, grid, in_specs, out_specs, out_shape, args, scratch_shapes=()):
    params = _params(len(grid))
    if comm is None:
        res = pl.pallas_call(functools.partial(body), name=name, grid=grid, in_specs=list(in_specs),
                             out_specs=list(out_specs), out_shape=list(out_shape),
                             scratch_shapes=list(scratch_shapes), compiler_params=params)(*args)
        return list(res), []
    n_in, n_out, n_scr = len(in_specs), len(out_specs), len(scratch_shapes)
    c_in, c_out = len(comm.operands), len(comm.out_shape)

    def edge(last):
        conds = [pl.program_id(a) == (g - 1 if last else 0) for a, g in enumerate(grid)]
        return functools.reduce(jnp.logical_and, conds)

    def wrapped(*refs):
        refs = list(refs)
        ins, refs = refs[:n_in], refs[n_in:]
        cins, refs = refs[:c_in], refs[c_in:]
        outs, refs = refs[:n_out], refs[n_out:]
        couts, refs = refs[:c_out], refs[c_out:]
        scr, sems = refs[:n_scr], refs[n_scr:]

        @pl.when(edge(False))
        def _():
            comm.start(cins, couts, sems)

        body(*ins, *outs, *scr)

        @pl.when(edge(True))
        def _():
            comm.wait(cins, couts, sems)

    any_spec = pl.BlockSpec(memory_space=pl.ANY)
    res = pl.pallas_call(
        wrapped, name=name, grid=grid,
        in_specs=list(in_specs) + [any_spec] * c_in, out_specs=list(out_specs) + [any_spec] * c_out,
        out_shape=list(out_shape) + comm.out_shape, scratch_shapes=list(scratch_shapes) + comm.sem_shapes,
        compiler_params=pltpu.CompilerParams(dimension_semantics=("arbitrary",) * len(grid),
                                             vmem_limit_bytes=VMEM_LIMIT, has_side_effects=True),
    )(*args, *comm.operands)
    return res[:n_out], res[n_out:]


def sum_parts(parts, block_rows):
    n, rows, cols = parts.shape

    def body(p_ref, o_ref):
        acc = p_ref[0].astype(F32)
        for s in range(1, n):
            acc = acc + p_ref[s].astype(F32)
        o_ref[...] = acc

    return pl.pallas_call(
        functools.partial(body), name="sum_parts",
        grid=(rows // block_rows,),
        in_specs=[pl.BlockSpec((n, block_rows, cols), lambda i: (0, i, 0))],
        out_specs=pl.BlockSpec((block_rows, cols), lambda i: (i, 0)),
        out_shape=jax.ShapeDtypeStruct((rows, cols), F32),
        compiler_params=_params(1),
    )(parts)


def adamw(w, g, m, v):
    def body(w_ref, g_ref, m_ref, v_ref, d_ref, m_out, v_out):
        gg = g_ref[...]
        m2 = ADAM_B1 * m_ref[...] + (1.0 - ADAM_B1) * gg
        v2 = ADAM_B2 * v_ref[...] + (1.0 - ADAM_B2) * (gg * gg)
        m_hat = m2 / (1.0 - ADAM_B1 ** ADAM_STEP)
        v_hat = v2 / (1.0 - ADAM_B2 ** ADAM_STEP)
        d_ref[...] = -ADAM_LR * (m_hat / (jnp.sqrt(v_hat) + ADAM_EPS) + ADAM_WD * w_ref[...])
        m_out[...] = m2
        v_out[...] = v2

    spec = _full(w.shape)
    shape = jax.ShapeDtypeStruct(w.shape, F32)
    return pl.pallas_call(
        functools.partial(body), name="adamw",
        in_specs=[spec] * 4, out_specs=[spec] * 3, out_shape=[shape] * 3,
        compiler_params=pltpu.CompilerParams(vmem_limit_bytes=VMEM_LIMIT),
    )(w, g, m, v)


def ffn_up(x, g, wg_t, wu_t, tm, tn, comm=None):
    t = x.shape[0]

    def body(x_ref, g_ref, wg_ref, wu_ref, h_ref, silu_ref, dgate_ref, act_ref):
        xx = x_ref[...]
        h = ((xx * _rstd(xx)) * g_ref[...]).astype(BF16)
        h_ref[...] = h
        for c in range(D_FF // tn):
            cols = slice(c * tn, (c + 1) * tn)
            a = _dot_nt(h, wg_ref[cols, :])
            b = _dot_nt(h, wu_ref[cols, :])
            sig = 0.5 * jnp.tanh(0.5 * a) + 0.5
            silu = a * sig
            silu_ref[:, cols] = silu.astype(BF16)
            dgate_ref[:, cols] = (b * (sig + silu * (1.0 - sig))).astype(BF16)
            act_ref[:, cols] = (silu * b).astype(BF16)

    wide = jax.ShapeDtypeStruct((t, D_FF), BF16)
    row = lambda n: pl.BlockSpec((tm, n), lambda i: (i, 0))
    return _pallas(
        comm, body, name="ffn_up",
        grid=(t // tm,),
        in_specs=[row(D_MODEL), _full((1, D_MODEL)), _full((D_FF, D_MODEL)), _full((D_FF, D_MODEL))],
        out_specs=[row(D_MODEL), row(D_FF), row(D_FF), row(D_FF)],
        out_shape=[jax.ShapeDtypeStruct((t, D_MODEL), BF16), wide, wide, wide],
        args=(x, g, wg_t, wu_t))


def ffn_down(act, wd, x, tm):
    t = x.shape[0]

    def body(act_ref, wd_ref, x_ref, o_ref):
        o_ref[...] = x_ref[...] + 0.5 * _dot(act_ref[...], wd_ref[...])

    return pl.pallas_call(
        functools.partial(body), name="ffn_down",
        grid=(t // tm,),
        in_specs=[pl.BlockSpec((tm, D_FF), lambda i: (i, 0)), _full((D_FF, D_MODEL)),
                  pl.BlockSpec((tm, D_MODEL), lambda i: (i, 0))],
        out_specs=pl.BlockSpec((tm, D_MODEL), lambda i: (i, 0)),
        out_shape=jax.ShapeDtypeStruct((t, D_MODEL), F32),
        compiler_params=_params(1),
    )(act, wd, x)


def ffn_bwd_act(dx, wd, silu, dgate, tm, tn, comm=None):
    t = dx.shape[0]

    def body(dx_ref, wd_ref, silu_ref, dgate_ref, da_ref, db_ref):
        dxb = (0.5 * dx_ref[...]).astype(BF16)
        for c in range(D_FF // tn):
            cols = slice(c * tn, (c + 1) * tn)
            dact = _dot_nt(dxb, wd_ref[cols, :])
            da_ref[:, cols] = (dact * dgate_ref[:, cols].astype(F32)).astype(BF16)
            db_ref[:, cols] = (dact * silu_ref[:, cols].astype(F32)).astype(BF16)

    wide = jax.ShapeDtypeStruct((t, D_FF), BF16)
    row = lambda n: pl.BlockSpec((tm, n), lambda i: (i, 0))
    return _pallas(
        comm, body, name="ffn_bwd_act",
        grid=(t // tm,),
        in_specs=[row(D_MODEL), _full((D_FF, D_MODEL)), row(D_FF), row(D_FF)],
        out_specs=[row(D_FF), row(D_FF)],
        out_shape=[wide, wide],
        args=(dx, wd, silu, dgate))


def norm_bwd_matmul(a1, w1, a2, w2, x, g, dx_in, tm, comm=None):
    t = x.shape[0]
    k1, k2 = a1.shape[1], a2.shape[1]

    def body(a1_ref, w1_ref, a2_ref, w2_ref, x_ref, g_ref, dxin_ref, dx_ref, dg_ref):
        dh = _dot(a1_ref[...], w1_ref[...]) + _dot(a2_ref[...], w2_ref[...])
        xx = x_ref[...]
        r = _rstd(xx)
        dx, dg_rows = _rms_bwd(dh, xx * r, r, g_ref[...])
        dx_ref[...] = dxin_ref[...] + dx

        @pl.when(pl.program_id(0) == 0)
        def _():
            dg_ref[...] = jnp.zeros_like(dg_ref)

        dg_ref[...] += _colsum(dg_rows)

    row = pl.BlockSpec((tm, D_MODEL), lambda i: (i, 0))
    return _pallas(
        comm, body, name="norm_bwd_matmul",
        grid=(t // tm,),
        in_specs=[pl.BlockSpec((tm, k1), lambda i: (i, 0)), _full((k1, D_MODEL)),
                  pl.BlockSpec((tm, k2), lambda i: (i, 0)), _full((k2, D_MODEL)),
                  row, _full((1, D_MODEL)), row],
        out_specs=[row, _full((1, D_MODEL))],
        out_shape=[jax.ShapeDtypeStruct((t, D_MODEL), F32), jax.ShapeDtypeStruct((1, D_MODEL), F32)],
        args=(a1, w1, a2, w2, x, g, dx_in))


def matmul_tn(a, b, scale, tmm, tk, comm=None):
    t, m = a.shape
    n = b.shape[1]
    nk = t // tk

    def body(a_ref, b_ref, o_ref, acc_ref):
        k = pl.program_id(1)

        @pl.when(k == 0)
        def _():
            acc_ref[...] = jnp.zeros_like(acc_ref)

        acc_ref[...] += _dot_tn(a_ref[...].astype(BF16), b_ref[...].astype(BF16))

        @pl.when(k == nk - 1)
        def _():
            o_ref[...] = (scale * acc_ref[...]).astype(BF16)

    (out,), comm_outs = _pallas(
        comm, body, name="matmul_tn",
        grid=(m // tmm, nk),
        in_specs=[pl.BlockSpec((tk, tmm), lambda i, k: (k, i)), pl.BlockSpec((tk, n), lambda i, k: (k, 0))],
        out_specs=[pl.BlockSpec((tmm, n), lambda i, k: (i, 0))],
        out_shape=[jax.ShapeDtypeStruct((m, n), BF16)],
        scratch_shapes=[pltpu.VMEM((tmm, n), F32)],
        args=(a, b))
    return out if comm is None else (out, comm_outs)


def input_projection(x, g, w_qkv_t, w_z_t, tm):
    t = x.shape[0]

    def body(x_ref, g_ref, wq_ref, wz_ref, qkv_ref, z_ref, h_ref):
        xx = x_ref[...]
        h = ((xx * _rstd(xx)) * g_ref[...]).astype(BF16)
        h_ref[...] = h
        qkv_ref[...] = _dot_nt(h, wq_ref[...])
        z_ref[...] = _dot_nt(h, wz_ref[...])

    row = lambda n: pl.BlockSpec((tm, n), lambda i: (i, 0))
    return pl.pallas_call(
        functools.partial(body), name="input_projection", grid=(t // tm,),
        in_specs=[row(D_MODEL), _full((1, D_MODEL)), _full((D_QKV, D_MODEL)), _full((2 * D_SGU, D_MODEL))],
        out_specs=[row(D_QKV), row(2 * D_SGU), row(D_MODEL)],
        out_shape=[jax.ShapeDtypeStruct((t, D_QKV), F32), jax.ShapeDtypeStruct((t, 2 * D_SGU), F32),
                   jax.ShapeDtypeStruct((t, D_MODEL), BF16)],
        compiler_params=_params(1))(x, g, w_qkv_t, w_z_t)


def _shift_rows(shape, first, second):
    row = lax.broadcasted_iota(jnp.int32, shape, len(shape) - 2)
    return jnp.where(row == 0, first, jnp.where(row == 1, second, 0.0))


def _hi_lo(a):
    hi = a.astype(BF16).astype(F32)
    return hi, a - hi


def _head_tile_spec(tm, rows):
    return pl.BlockSpec((N_HEADS, None, rows, tm), lambda i: (0, i, 0, 0))


def _to_head_tiles(a):
    return a.T.reshape(N_HEADS, HEAD_DIM, a.shape[0])


def _from_head_tiles(a):
    return a.reshape(D_ATTN, a.shape[-1]).T


def _head_mean_matrix(width):
    head = jnp.arange(width) // HEAD_DIM
    return (head[:, None] == head[None, :]).astype(F32) / HEAD_DIM


def _kv_tile_spec(n_sub, rows, cols):
    return pl.BlockSpec((N_KV_HEADS, n_sub, rows, cols), lambda i: (0, i, 0, 0))


def qk_prep(qkv, gq_w, gk_w, cos_w, sin_w, mean_q, mean_k, tm, tk, tk_v):
    t = qkv.shape[0]
    n_sub, n_sub_v = tm // tk, tm // tk_v

    def body(p_ref, gq_ref, gk_ref, cos_ref, sin_ref, mq_ref, mk_ref, q_ref, k_ref, kt_ref, vt_ref, vtb_ref,
             qmax_ref, kmax_ref):
        @pl.when(pl.program_id(0) == 0)
        def _():
            qmax_ref[...] = jnp.zeros_like(qmax_ref)
            kmax_ref[...] = jnp.zeros_like(kmax_ref)

        cos2, sin2 = cos_ref[...], sin_ref[...]
        q = p_ref[:, :D_ATTN]
        k = p_ref[:, D_ATTN:D_ATTN + D_KV]
        qn = q * lax.rsqrt(_dot_split(q * q, mq_ref[...]) + EPS) * gq_ref[...]
        kn = k * lax.rsqrt(_dot_split(k * k, mk_ref[...]) + EPS) * gk_ref[...]
        cos8, sin8 = _tile_lanes(cos2, D_ATTN // LANES), _tile_lanes(sin2, D_ATTN // LANES)
        q_rot = (qn * cos8 + _pair_swap(qn) * sin8) * Q_SCALE
        q_ref[...] = _to_head_tiles(q_rot).astype(BF16)
        k_rot = kn * cos2 + _pair_swap(kn) * sin2
        q_sq = HEAD_DIM * _dot_split(q_rot * q_rot, mq_ref[...])
        k_sq = HEAD_DIM * _dot_split(k_rot * k_rot, mk_ref[...])
        qmax_ref[...] = jnp.maximum(qmax_ref[...], jnp.max(q_sq, axis=0, keepdims=True))
        kmax_ref[...] = jnp.maximum(kmax_ref[...], jnp.max(k_sq, axis=0, keepdims=True))
        vv = p_ref[:, D_ATTN + D_KV:]
        second = pltpu.roll(k_rot, HEAD_DIM, 1)
        for c in range(n_sub):
            rows = slice(c * tk, (c + 1) * tk)
            k_ref[0, c] = k_rot[rows, :HEAD_DIM].astype(BF16)
            k_ref[1, c] = second[rows, :HEAD_DIM].astype(BF16)
        for a, feat_ref, width, n in ((k_rot, kt_ref, tk, n_sub), (vv, vtb_ref, tk, n_sub), (vv, vt_ref, tk_v, n_sub_v)):
            for c in range(n):
                tile = a[c * width:(c + 1) * width].T.reshape(N_KV_HEADS, HEAD_DIM, width)
                feat_ref[:, c, :HEAD_DIM, :] = tile.astype(BF16)
        vt_ref[:, :, HEAD_DIM:, :] = jnp.ones((N_KV_HEADS, n_sub_v, ONES_ROWS, tk_v), BF16)
        minus = _shift_rows((N_KV_HEADS, n_sub, HEAD_DIM, tk), -1.0, -1.0).astype(BF16)
        kt_ref[:, :, HEAD_DIM:, :] = minus
        vtb_ref[:, :, HEAD_DIM:, :] = minus

    kv = lambda rows, cols: jax.ShapeDtypeStruct((N_KV_HEADS, t // tk, rows, cols), BF16)
    return pl.pallas_call(
        functools.partial(body), name="qk_prep", grid=(t // tm,),
        in_specs=[pl.BlockSpec((tm, D_QKV), lambda i: (i, 0)), _full((1, D_ATTN)), _full((1, D_KV)),
                  pl.BlockSpec((tm, LANES), lambda i: (i, 0)), pl.BlockSpec((tm, LANES), lambda i: (i, 0)),
                  _full((D_ATTN, D_ATTN)), _full((D_KV, D_KV))],
        out_specs=[_head_tile_spec(tm, HEAD_DIM), _kv_tile_spec(n_sub, tk, HEAD_DIM),
                   _kv_tile_spec(n_sub, 2 * HEAD_DIM, tk),
                   _kv_tile_spec(n_sub_v, HEAD_DIM + ONES_ROWS, tk_v), _kv_tile_spec(n_sub, 2 * HEAD_DIM, tk),
                   _full((1, D_ATTN)), _full((1, D_KV))],
        out_shape=[jax.ShapeDtypeStruct((N_HEADS, t // tm, HEAD_DIM, tm), BF16), kv(tk, HEAD_DIM), kv(2 * HEAD_DIM, tk),
                   jax.ShapeDtypeStruct((N_KV_HEADS, t // tk_v, HEAD_DIM + ONES_ROWS, tk_v), BF16),
                   kv(2 * HEAD_DIM, tk),
                   jax.ShapeDtypeStruct((1, D_ATTN), F32), jax.ShapeDtypeStruct((1, D_KV), F32)],
        compiler_params=_params(1),
    )(qkv, gq_w, gk_w, cos_w, sin_w, mean_q, mean_k)


def qk_bwd(dq_rot, dk_rot, dv, qkv, gq_w, gk_w, cos_w, sin_w, mean_q, mean_k, tm):
    t = qkv.shape[0]
    tk = dk_rot.shape[-1]
    n_sub = tm // tk

    def token_major(ref):
        return jnp.concatenate([ref[:, c].reshape(D_KV, tk).T for c in range(n_sub)], axis=0)

    def branch(raw, d_rot, gain, mean_mat, cos, sin, scale):
        r = lax.rsqrt(_dot_split(raw * raw, mean_mat) + EPS)
        n = raw * r
        dy = (d_rot * cos - _pair_swap(d_rot) * sin) * scale
        dn = dy * gain
        return r * (dn - n * _dot_split(dn * n, mean_mat)), dy * n

    def body(dq_ref, dk_ref, dv_ref, p_ref, gq_ref, gk_ref, cos_ref, sin_ref, mq_ref, mk_ref,
             dp_ref, dgq_ref, dgk_ref):
        cos2, sin2 = cos_ref[...], sin_ref[...]
        cos8, sin8 = _tile_lanes(cos2, D_ATTN // LANES), _tile_lanes(sin2, D_ATTN // LANES)
        dq, dgq = branch(p_ref[:, :D_ATTN], _from_head_tiles(dq_ref[...]), gq_ref[...], mq_ref[...], cos8, sin8,
                         HEAD_DIM ** -0.5)
        dk, dgk = branch(p_ref[:, D_ATTN:D_ATTN + D_KV], token_major(dk_ref), gk_ref[...], mk_ref[...], cos2, sin2, 1.0)
        dp_ref[...] = jnp.concatenate([dq, dk, token_major(dv_ref)], axis=-1).astype(BF16)

        @pl.when(pl.program_id(0) == 0)
        def _():
            dgq_ref[...] = jnp.zeros_like(dgq_ref)
            dgk_ref[...] = jnp.zeros_like(dgk_ref)

        dgq_ref[...] += _colsum(dgq)
        dgk_ref[...] += _colsum(dgk)

    return pl.pallas_call(
        functools.partial(body), name="qk_bwd", grid=(t // tm,),
        in_specs=[_head_tile_spec(tm, HEAD_DIM), _kv_tile_spec(n_sub, HEAD_DIM, tk),
                  _kv_tile_spec(n_sub, HEAD_DIM, tk), pl.BlockSpec((tm, D_QKV), lambda i: (i, 0)),
                  _full((1, D_ATTN)), _full((1, D_KV)),
                  pl.BlockSpec((tm, LANES), lambda i: (i, 0)), pl.BlockSpec((tm, LANES), lambda i: (i, 0)),
                  _full((D_ATTN, D_ATTN)), _full((D_KV, D_KV))],
        out_specs=[pl.BlockSpec((tm, D_QKV), lambda i: (i, 0)), _full((1, D_ATTN)), _full((1, D_KV))],
        out_shape=[jax.ShapeDtypeStruct((t, D_QKV), BF16), jax.ShapeDtypeStruct((1, D_ATTN), F32),
                   jax.ShapeDtypeStruct((1, D_KV), F32)],
        compiler_params=_params(1),
    )(dq_rot, dk_rot, dv, qkv, gq_w, gk_w, cos_w, sin_w, mean_q, mean_k)


def attention_fwd(q_t, k, v_t, comm=None):
    _, nq, _, tq = q_t.shape
    _, nk, tk, _ = k.shape

    def body(q_ref, k_ref, v_ref, o_ref, qtok_ref, s_scr, p_scr):
        q = q_ref[...]
        s_scr[0] = _dot(k_ref[0], q)
        p_scr[1] = jnp.zeros((tk, tq), BF16)
        zero = jnp.zeros((HEAD_DIM + ONES_ROWS, tq), F32)

        def matmuls(j, slot):
            pv = _dot(v_ref[jnp.maximum(j - 1, 0)], p_scr[1 - slot])
            s_scr[1 - slot] = _dot(k_ref[jnp.minimum(j + 1, nk - 1)], q)
            return pv

        def finish(m, acc):
            acc = acc + _dot(v_ref[nk - 1], p_scr[(nk - 1) % 2])
            l = acc[HEAD_DIM:HEAD_DIM + 1]
            o_ref[...] = acc[:HEAD_DIM] / l
            lse_rows = _shift_rows((HEAD_DIM, tq), *_hi_lo(m + jnp.log2(l)))
            qtok_ref[...] = jnp.concatenate([q.astype(F32), lse_rows], axis=0).T.astype(BF16)

        def step(j, slot, carry):
            m, acc = carry
            s = s_scr[slot]
            pv = matmuls(j, slot)
            m_new = jnp.maximum(m, jnp.max(s, axis=0, keepdims=True))
            p_scr[slot] = jnp.exp2(s - m_new).astype(BF16)
            return m_new, jnp.exp2(m - m_new) * (acc + pv)

        finish(*_loop_pairs(nk, step, (jnp.full((1, tq), -1e30, F32), zero)))

    return _pallas(
        comm, body, name="attention_fwd", grid=(N_HEADS, nq),
        in_specs=[pl.BlockSpec((None, None, HEAD_DIM, tq), lambda h, i: (h, i, 0, 0)),
                  pl.BlockSpec((None, nk, tk, HEAD_DIM), lambda h, i: (h // KV_GROUP, 0, 0, 0)),
                  pl.BlockSpec((None, nk, HEAD_DIM + ONES_ROWS, tk), lambda h, i: (h // KV_GROUP, 0, 0, 0))],
        out_specs=[pl.BlockSpec((None, None, HEAD_DIM, tq), lambda h, i: (h, i, 0, 0)),
                   pl.BlockSpec((None, None, tq, 2 * HEAD_DIM), lambda h, i: (h, i, 0, 0))],
        out_shape=[jax.ShapeDtypeStruct((N_HEADS, nq, HEAD_DIM, tq), F32),
                   jax.ShapeDtypeStruct((N_HEADS, nq, tq, 2 * HEAD_DIM), BF16)],
        scratch_shapes=[pltpu.VMEM((2, tk, tq), F32), pltpu.VMEM((2, tk, tq), BF16)],
        args=(q_t, k, v_t))


def attention_bwd(q_tok, do_tok, q_t, do_t, k_t, v_t, comm=None):
    _, nq, _, tq = q_t.shape
    _, nk, _, tk = k_t.shape

    def body(qtok_ref, dotok_ref, q_ref, do_ref, kt_ref, vt_ref, dq_ref, dk_ref, dv_ref,
             s_scr, dp_scr, p_scr, ds_scr):
        @pl.when(pl.program_id(1) == 0)
        def _():
            dq_ref[...] = jnp.zeros_like(dq_ref)

        kt_aug, vt_aug = kt_ref[...], vt_ref[...]
        kt = kt_aug[:HEAD_DIM]
        n = KV_GROUP * nq
        s_scr[0] = _dot(qtok_ref[0, 0], kt_aug)
        dp_scr[0] = _dot(dotok_ref[0, 0], vt_aug)
        p_scr[1] = jnp.zeros((tq, tk), BF16)
        ds_scr[1] = jnp.zeros((tq, tk), BF16)

        def products(t, slot, dk, dv):
            h, i = t // nq, t % nq
            ds = ds_scr[slot]
            dq_ref[h, i] += _dot_nt(kt, ds)
            return dk + _dot(q_ref[h, i], ds), dv + _dot(do_ref[h, i], p_scr[slot])

        def step(t, slot, carry):
            s, dp = s_scr[slot], dp_scr[slot]
            dk, dv = products(jnp.maximum(t - 1, 0), 1 - slot, *carry)
            nxt = jnp.minimum(t + 1, n - 1)
            s_scr[1 - slot] = _dot(qtok_ref[nxt // nq, nxt % nq], kt_aug)
            dp_scr[1 - slot] = _dot(dotok_ref[nxt // nq, nxt % nq], vt_aug)
            p = jnp.exp2(s)
            p_scr[slot] = p.astype(BF16)
            ds_scr[slot] = (p * dp).astype(BF16)
            return dk, dv

        zero = jnp.zeros((HEAD_DIM, tk), F32)
        dk, dv = products(n - 1, (n - 1) % 2, *_loop_pairs(n, step, (zero, zero)))
        dk_ref[...] = dk * (1.0 / LOG2_E)
        dv_ref[...] = dv

    group = lambda g, j: (g, 0, 0, 0)
    tile = lambda g, j: (g, j, 0, 0)
    once = pl.Buffered(1)
    return _pallas(
        comm, body, name="attention_bwd", grid=(N_KV_HEADS, nk),
        in_specs=[pl.BlockSpec((KV_GROUP, nq, tq, 2 * HEAD_DIM), group, pipeline_mode=once),
                  pl.BlockSpec((KV_GROUP, nq, tq, 2 * HEAD_DIM), group, pipeline_mode=once),
                  pl.BlockSpec((KV_GROUP, nq, HEAD_DIM, tq), group, pipeline_mode=once),
                  pl.BlockSpec((KV_GROUP, nq, HEAD_DIM, tq), group, pipeline_mode=once),
                  pl.BlockSpec((None, None, 2 * HEAD_DIM, tk), tile),
                  pl.BlockSpec((None, None, 2 * HEAD_DIM, tk), tile)],
        out_specs=[pl.BlockSpec((KV_GROUP, nq, HEAD_DIM, tq), group),
                   pl.BlockSpec((None, None, HEAD_DIM, tk), tile),
                   pl.BlockSpec((None, None, HEAD_DIM, tk), tile)],
        out_shape=[jax.ShapeDtypeStruct((N_HEADS, nq, HEAD_DIM, tq), F32),
                   jax.ShapeDtypeStruct((N_KV_HEADS, nk, HEAD_DIM, tk), F32),
                   jax.ShapeDtypeStruct((N_KV_HEADS, nk, HEAD_DIM, tk), F32)],
        scratch_shapes=[pltpu.VMEM((2, tq, tk), F32), pltpu.VMEM((2, tq, tk), F32),
                        pltpu.VMEM((2, tq, tk), BF16), pltpu.VMEM((2, tq, tk), BF16)],
        args=(q_tok, do_tok, q_t, do_t, k_t, v_t))


def _group_matmul(a_t, w_ref):
    return jnp.concatenate([_dot(a_t[g * SGU_GROUP_DIM:(g + 1) * SGU_GROUP_DIM], w_ref[g])
                            for g in range(N_SGU_GROUPS)], axis=0)


def _gate_forward(z, g_sgu, wst_ref, bias):
    gz, th = _gelu(z)
    u, vv = gz[:, :D_SGU], gz[:, D_SGU:]
    rv = _rstd(vv)
    nv = vv * rv
    vn = nv * g_sgu
    v_chunks, fs = [], []
    for c in range(z.shape[0] // CHUNK):
        vt = vn[c * CHUNK:(c + 1) * CHUNK].T.astype(BF16)
        v_chunks.append(vt)
        fs.append(_group_matmul(vt, wst_ref).T + bias)
    f = jnp.concatenate(fs, axis=0) if len(fs) > 1 else fs[0]
    return th, u, rv, nv, v_chunks, f


def mix_out(z, o, x, g_sgu, g_ao, g_so, ws_t, bias, w_out, tm):
    t = x.shape[0]

    def body(z_ref, o_ref, x_ref, gs_ref, gao_ref, gso_ref, ws_ref, bias_ref, wout_ref, x2_ref, mixed_ref):
        _, u, _, _, _, f = _gate_forward(z_ref[...], gs_ref[...], ws_ref, bias_ref[...])
        sgu = u * f
        oo = _from_head_tiles(o_ref[...])
        mixed = jnp.concatenate([oo * _rstd(oo) * gao_ref[...], sgu * _rstd(sgu) * gso_ref[...]], axis=-1).astype(BF16)
        mixed_ref[...] = mixed
        x2_ref[...] = x_ref[...] + _dot(mixed, wout_ref[...])

    row = lambda n: pl.BlockSpec((tm, n), lambda i: (i, 0))
    return pl.pallas_call(
        functools.partial(body), name="mix_out", grid=(t // tm,),
        in_specs=[row(2 * D_SGU), _head_tile_spec(tm, HEAD_DIM), row(D_MODEL), _full((1, D_SGU)), _full((1, D_ATTN)),
                  _full((1, D_SGU)),
                  _full((N_SGU_GROUPS, CHUNK, CHUNK)), _full((CHUNK, D_SGU)), _full((D_MODEL, D_MODEL))],
        out_specs=[row(D_MODEL), row(D_MODEL)],
        out_shape=[jax.ShapeDtypeStruct((t, D_MODEL), F32), jax.ShapeDtypeStruct((t, D_MODEL), BF16)],
        compiler_params=_params(1),
    )(z, o, x, g_sgu, g_ao, g_so, ws_t, bias, w_out)


def mix_bwd(dx2, z, o, g_sgu, g_ao, g_so, ws, ws_t, bias, w_out, group_ind, tm):
    t = dx2.shape[0]
    n_tiles = t // tm

    def body(dx_ref, z_ref, o_ref, gs_ref, gao_ref, gso_ref, ws_ref, wst_ref, bias_ref, wout_ref, ind_ref,
             do_ref, dotok_ref, dz_ref, dg_ref, dws_ref, dbs_ref, df_sum):
        step = pl.program_id(0)

        @pl.when(step == 0)
        def _():
            dg_ref[...] = jnp.zeros_like(dg_ref)
            dws_ref[...] = jnp.zeros_like(dws_ref)
            df_sum[...] = jnp.zeros_like(df_sum)

        z = z_ref[...]
        th, u, rv, nv, v_chunks, f = _gate_forward(z, gs_ref[...], wst_ref, bias_ref[...])
        dmixed = _dot_nt(dx_ref[...].astype(BF16), wout_ref[...])
        o_tiles = o_ref[...]
        oo = _from_head_tiles(o_tiles)
        ro = _rstd(oo)
        d_o, dgao = _rms_bwd(dmixed[:, :D_ATTN], oo * ro, ro, gao_ref[...])
        do_tiles = _to_head_tiles(d_o)
        do_ref[...] = do_tiles.astype(BF16)
        delta_hi, delta_lo = _hi_lo(jnp.sum(do_tiles * o_tiles, axis=1, keepdims=True))
        for h in range(N_HEADS):
            delta_rows = _shift_rows((HEAD_DIM, tm), delta_hi[h], delta_lo[h])
            dotok_ref[h] = jnp.concatenate([do_tiles[h], delta_rows], axis=0).T.astype(BF16)
        sgu = u * f
        rs = _rstd(sgu)
        dsgu, dgso = _rms_bwd(dmixed[:, D_ATTN:], sgu * rs, rs, gso_ref[...])
        du = dsgu * f
        df = dsgu * u
        dvns = []
        df_acc = jnp.zeros((CHUNK, D_SGU), F32)
        for c in range(tm // CHUNK):
            dfc32 = df[c * CHUNK:(c + 1) * CHUNK]
            dft = dfc32.T.astype(BF16)
            dvns.append(_group_matmul(dft, ws_ref).T)
            for g in range(N_SGU_GROUPS):
                rows = slice(g * SGU_GROUP_DIM, (g + 1) * SGU_GROUP_DIM)
                dws_ref[g] += _dot_tn(dft[rows], v_chunks[c][rows])
            df_acc = df_acc + dfc32
        df_sum[...] += df_acc
        dvn = jnp.concatenate(dvns, axis=0) if len(dvns) > 1 else dvns[0]
        dvv, dgs = _rms_bwd(dvn, nv, rv, gs_ref[...])
        dz_ref[...] = (jnp.concatenate([du, dvv], axis=-1) * _gelu_grad(z, th)).astype(BF16)
        dg_ref[0:1, :] += _colsum(dgao)
        dg_ref[1:2, :] += _colsum(dgso)
        dg_ref[2:3, :] += _colsum(dgs)

        @pl.when(step == n_tiles - 1)
        def _():
            dbs_ref[...] = _dot_f32(df_sum[...], ind_ref[...])

    row = lambda n: pl.BlockSpec((tm, n), lambda i: (i, 0))
    return pl.pallas_call(
        functools.partial(body), name="mix_bwd", grid=(n_tiles,),
        in_specs=[row(D_MODEL), row(2 * D_SGU), _head_tile_spec(tm, HEAD_DIM), _full((1, D_SGU)), _full((1, D_ATTN)),
                  _full((1, D_SGU)),
                  _full((N_SGU_GROUPS, CHUNK, CHUNK)), _full((N_SGU_GROUPS, CHUNK, CHUNK)), _full((CHUNK, D_SGU)),
                  _full((D_MODEL, D_MODEL)), _full((D_SGU, LANES))],
        out_specs=[_head_tile_spec(tm, HEAD_DIM), pl.BlockSpec((N_HEADS, None, tm, 2 * HEAD_DIM), lambda i: (0, i, 0, 0)),
                   row(2 * D_SGU), _full((8, D_SGU)),
                   _full((N_SGU_GROUPS, CHUNK, CHUNK)), _full((CHUNK, LANES))],
        out_shape=[jax.ShapeDtypeStruct((N_HEADS, n_tiles, HEAD_DIM, tm), BF16),
                   jax.ShapeDtypeStruct((N_HEADS, n_tiles, tm, 2 * HEAD_DIM), BF16),
                   jax.ShapeDtypeStruct((t, 2 * D_SGU), BF16),
                   jax.ShapeDtypeStruct((8, D_SGU), F32),
                   jax.ShapeDtypeStruct((N_SGU_GROUPS, CHUNK, CHUNK), F32),
                   jax.ShapeDtypeStruct((CHUNK, LANES), F32)],
        scratch_shapes=[pltpu.VMEM((CHUNK, D_SGU), F32)],
        compiler_params=_params(1),
    )(dx2, z, o, g_sgu, g_ao, g_so, ws, ws_t, bias, w_out, group_ind)


def ffn_down_loss(act, wd, x, g, target, tm):
    t = x.shape[0]

    def body(act_ref, wd_ref, x_ref, g_ref, t_ref, loss_ref, dx_ref, dg_ref):
        @pl.when(pl.program_id(0) == 0)
        def _():
            loss_ref[...] = jnp.zeros_like(loss_ref)
            dg_ref[...] = jnp.zeros_like(dg_ref)

        xx = x_ref[...] + 0.5 * _dot(act_ref[...], wd_ref[...])
        r = _rstd(xx)
        n = xx * r
        err = n * g_ref[...] - t_ref[...]
        per_token = jnp.mean(err * err, axis=-1, keepdims=True)
        loss_ref[...] += 0.5 * jnp.sum(per_token, axis=0, keepdims=True)
        dx, dg_rows = _rms_bwd(err * (1.0 / D_MODEL), n, r, g_ref[...])
        dx_ref[...] = dx
        dg_ref[...] += _colsum(dg_rows)

    row = pl.BlockSpec((tm, D_MODEL), lambda i: (i, 0))
    return pl.pallas_call(
        functools.partial(body), name="ffn_down_loss", grid=(t // tm,),
        in_specs=[pl.BlockSpec((tm, D_FF), lambda i: (i, 0)), _full((D_FF, D_MODEL)), row, _full((1, D_MODEL)), row],
        out_specs=[_full((1, LANES)), row, _full((1, D_MODEL))],
        out_shape=[jax.ShapeDtypeStruct((1, LANES), F32), jax.ShapeDtypeStruct((t, D_MODEL), F32),
                   jax.ShapeDtypeStruct((1, D_MODEL), F32)],
        compiler_params=_params(1),
    )(act, wd, x, g, target)


def _rope_tables(t):
    rows = t // GRID_W
    row_idx = jnp.repeat(jnp.arange(rows, dtype=F32), GRID_W)
    col_idx = jnp.tile(jnp.arange(GRID_W, dtype=F32), rows)
    axis_dim = HEAD_DIM // 2
    inv = 1.0 / (ROPE_THETA ** (jnp.arange(0, axis_dim, 2, dtype=F32) / axis_dim))
    ang = jnp.concatenate([row_idx[:, None] * inv, col_idx[:, None] * inv], axis=-1)
    cos = jnp.repeat(jnp.cos(ang), 2, axis=-1)
    sin = jnp.repeat(jnp.sin(ang), 2, axis=-1) * jnp.tile(jnp.array([-1.0, 1.0], F32), HEAD_DIM // 2)
    return jnp.tile(cos, (1, LANES // HEAD_DIM)), jnp.tile(sin, (1, LANES // HEAD_DIM))


def kernel(x, g_ffn1, w1_gate, w1_up, w1_down, g_mix, w_in, g_q, g_k, g_sgu, w_s, b_s, g_attn_out, g_sgu_out, w_out, g_ffn2, w2_gate, w2_up, w2_down, g_final, loss_target, m_g_ffn1, m_w1_gate, m_w1_up, m_w1_down, m_g_mix, m_w_in, m_g_q, m_g_k, m_g_sgu, m_w_s, m_b_s, m_g_attn_out, m_g_sgu_out, m_w_out, m_g_ffn2, m_w2_gate, m_w2_up, m_w2_down, m_g_final, v_g_ffn1, v_w1_gate, v_w1_up, v_w1_down, v_g_mix, v_w_in, v_g_q, v_g_k, v_g_sgu, v_w_s, v_b_s, v_g_attn_out, v_g_sgu_out, v_w_out, v_g_ffn2, v_w2_gate, v_w2_up, v_w2_down, v_g_final):
    weights = dict(g_ffn1=g_ffn1, w1_gate=w1_gate, w1_up=w1_up, w1_down=w1_down, g_mix=g_mix, w_in=w_in, g_q=g_q,
                   g_k=g_k, g_sgu=g_sgu, w_s=w_s, b_s=b_s, g_attn_out=g_attn_out, g_sgu_out=g_sgu_out, w_out=w_out,
                   g_ffn2=g_ffn2, w2_gate=w2_gate, w2_up=w2_up, w2_down=w2_down, g_final=g_final)
    m_in = dict(g_ffn1=m_g_ffn1, w1_gate=m_w1_gate, w1_up=m_w1_up, w1_down=m_w1_down, g_mix=m_g_mix, w_in=m_w_in,
                g_q=m_g_q, g_k=m_g_k, g_sgu=m_g_sgu, w_s=m_w_s, b_s=m_b_s, g_attn_out=m_g_attn_out,
                g_sgu_out=m_g_sgu_out, w_out=m_w_out, g_ffn2=m_g_ffn2, w2_gate=m_w2_gate, w2_up=m_w2_up,
                w2_down=m_w2_down, g_final=m_g_final)
    v_in = dict(g_ffn1=v_g_ffn1, w1_gate=v_w1_gate, w1_up=v_w1_up, w1_down=v_w1_down, g_mix=v_g_mix, w_in=v_w_in,
                g_q=v_g_q, g_k=v_g_k, g_sgu=v_g_sgu, w_s=v_w_s, b_s=v_b_s, g_attn_out=v_g_attn_out,
                g_sgu_out=v_g_sgu_out, w_out=v_w_out, g_ffn2=v_g_ffn2, w2_gate=v_w2_gate, w2_up=v_w2_up,
                w2_down=v_w2_down, g_final=v_g_final)
    names = list(weights)

    t = x.shape[1]
    x0 = x[0]
    target = loss_target[0]
    tm = min(512, t)
    tm_ff = min(256, t)
    tn_ff = 256
    tq = min(512, t)
    tk = min(256, t)
    tk_fwd = min(512, t)
    tk_w = min(2048, t)

    def shard_rows(name):
        w = weights[name][0]
        return (w.T if name in TRANSPOSED else w).astype(BF16)

    rows_of = dict(SHARD_ROWS)
    full = {}

    def packed(group):
        return jnp.concatenate([shard_rows(n) for n in group], axis=0), [rows_of[n] for n in group]

    def gather_of(group):
        return gather_exchange(*packed(group))

    def take(group, gathered):
        for n, g in zip(group, gathered):
            full[n] = g.reshape(N_DEV * rows_of[n], D_MODEL)

    first, second, third = ("w1_gate", "w1_up"), ("w1_down", "w_in", "w_out"), ("w2_gate", "w2_up", "w2_down")
    take(first, gather_two_level(*packed(first), "gather_first"))

    (h1, a1, b1, act1), gathered = ffn_up(x0, g_ffn1, full["w1_gate"], full["w1_up"], tm_ff, tn_ff, gather_of(second))
    take(second, gathered)
    w_in_t = full["w_in"]
    w_qkv_t, w_z_t = w_in_t[:D_QKV], w_in_t[D_QKV:]
    x1 = ffn_down(act1, full["w1_down"], x0, tm)

    qkv, z, h2 = input_projection(x1, g_mix, w_qkv_t, w_z_t, tm)
    cos_w, sin_w = _rope_tables(t)
    gq_w = jnp.tile(g_q, (1, N_HEADS))
    gk_w = jnp.tile(g_k, (1, N_KV_HEADS))
    mean_q, mean_k = _head_mean_matrix(D_ATTN).astype(BF16), _head_mean_matrix(D_KV).astype(BF16)
    q_t, k_tiles, kt_tiles, vt_tiles, vt_tiles_bwd, _, _ = qk_prep(
        qkv, gq_w, gk_w, cos_w, sin_w, mean_q, mean_k, tq, tk, tk_fwd)
    k_tiles_fwd = k_tiles.reshape(N_KV_HEADS, t // tk_fwd, tk_fwd, HEAD_DIM)
    (o_t, q_tok), gathered = attention_fwd(q_t, k_tiles_fwd, vt_tiles, gather_of(third))
    take(third, gathered)

    ws_b = w_s[0].astype(BF16)
    ws_tb = jnp.swapaxes(w_s[0], 1, 2).astype(BF16)
    bias = jnp.repeat(b_s[0].T, SGU_GROUP_DIM, axis=1)
    x2, mixed = mix_out(z, o_t, x1, g_sgu, g_attn_out, g_sgu_out, ws_tb, bias, full["w_out"], tq)

    (h3, a2, b2, act2), _ = ffn_up(x2, g_ffn2, full["w2_gate"], full["w2_up"], tm_ff, tn_ff)

    loss_part, dx3, dg_final = ffn_down_loss(act2, full["w2_down"], x2, g_final, target, tm)

    tmm = D_FF // 2
    (da2, db2), _ = ffn_bwd_act(dx3, full["w2_down"], a2, b2, tm, tn_ff)
    (dx2, dg_ffn2), _ = norm_bwd_matmul(da2, full["w2_gate"], db2, full["w2_up"], x2, g_ffn2, dx3, tm)
    dwg2, dwu2 = matmul_tn(da2, h3, 1.0, tmm, tk_w), matmul_tn(db2, h3, 1.0, tmm, tk_w)
    dwd2 = matmul_tn(act2, dx3, 0.5, tmm, tk_w)

    group_ind = (jnp.arange(D_SGU)[:, None] // SGU_GROUP_DIM == jnp.arange(LANES)[None, :]).astype(F32)
    do_t, do_tok, dz, dg_mixrow, dws, dbs = mix_bwd(dx2, z, o_t, g_sgu, g_attn_out, g_sgu_out, ws_b, ws_tb, bias,
                                                    full["w_out"], group_ind, tq)
    dw_out = matmul_tn(mixed, dx2, 1.0, D_MODEL // 2, tk_w)

    group_a = ("w2_gate", "w2_up", "w2_down", "w_out")
    (dq_t, dk_t, dv_t), (parts_a,) = attention_bwd(q_tok, do_tok, q_t, do_t, kt_tiles, vt_tiles_bwd,
                                                   scatter_exchange([dwg2, dwu2, dwd2, dw_out]))
    dqkv, dgq_w, dgk_w = qk_bwd(dq_t, dk_t, dv_t, qkv, gq_w, gk_w, cos_w, sin_w, mean_q, mean_k, tq)

    def pack_small(arrays):
        pieces = []
        for a in arrays:
            flat = a.reshape(-1)
            pieces.append(jnp.pad(flat, (0, (-flat.shape[0]) % (8 * LANES))).reshape(-1, LANES))
        return jnp.concatenate(pieces, axis=0), [p.shape[0] for p in pieces]

    early = dict(g_ffn2=dg_ffn2, g_final=dg_final, g_q=dgq_w.reshape(N_HEADS, HEAD_DIM).sum(0),
                 g_k=dgk_w.reshape(N_KV_HEADS, HEAD_DIM).sum(0), g_attn_out=dg_mixrow[0], g_sgu_out=dg_mixrow[1],
                 g_sgu=dg_mixrow[2], w_s=dws, b_s=dbs[:, :N_SGU_GROUPS].T)
    early_pack, early_rows = pack_small(list(early.values()))
    (dx1, dg_mix), (early_parts,) = norm_bwd_matmul(dqkv, w_qkv_t, dz, w_z_t, x1, g_mix, dx2, tm,
                                                    gather_exchange(early_pack, [early_pack.shape[0]]))
    dw_in = jnp.concatenate([matmul_tn(dqkv, h2, 1.0, D_QKV // 2, tk_w), matmul_tn(dz, h2, 1.0, D_SGU, tk_w)], axis=0)

    dwd1, (parts_in,) = matmul_tn(act1, dx1, 0.5, tmm, tk_w, scatter_exchange([dw_in]))
    (da1, db1), (parts_d1,) = ffn_bwd_act(dx1, full["w1_down"], a1, b1, tm, tn_ff, scatter_exchange([dwd1]))
    dwg1 = matmul_tn(da1, h1, 1.0, tmm, tk_w)
    dwu1, (parts_g1,) = matmul_tn(db1, h1, 1.0, tmm, tk_w, scatter_exchange([dwg1]))
    (dx0, dg_ffn1), (parts_u1,) = norm_bwd_matmul(da1, full["w1_gate"], db1, full["w1_up"], x0, g_ffn1, dx1, tm,
                                                  scatter_exchange([dwu1]))
    scattered = ((group_a, parts_a), (("w_in",), parts_in), (("w1_down",), parts_d1), (("w1_gate",), parts_g1),
                 (("w1_up",), parts_u1))

    late = dict(g_mix=dg_mix, g_ffn1=dg_ffn1, loss=loss_part)
    late_pack, late_rows = pack_small(list(late.values()))
    (late_parts,) = run_exchange(gather_exchange(late_pack, [late_pack.shape[0]]), "gather_late_small_grads")
    small_sums = {}
    for entries, rows, parts in ((early, early_rows, early_parts), (late, late_rows, late_parts)):
        summed = sum_parts(parts, parts.shape[1])
        off = 0
        for n, r in zip(entries, rows):
            small_sums[n] = summed[off:off + r]
            off += r
    loss = small_sums.pop("loss")[0, 0]

    grads, row_grads = {}, {}
    for group, parts in scattered:
        rows = parts.shape[1]
        summed = sum_parts(parts, rows if rows <= 2 * rows_of["w1_gate"] else rows // 2)
        off = 0
        for n in group:
            row_grads[n] = summed[off:off + rows_of[n]]
            grads[n] = (row_grads[n].T if n in TRANSPOSED else row_grads[n])[None]
            off += rows_of[n]
    for n, summed in small_sums.items():
        grads[n] = summed.reshape(-1)[:weights[n].size].reshape(weights[n].shape)

    delta_w, new_m, new_v = {}, {}, {}
    for n in names:
        shape = weights[n].shape
        if n in TRANSPOSED:
            view, unview, g = (lambda a: a[0].T), (lambda a: a.T[None]), row_grads[n]
        else:
            view, unview = (lambda a: a.reshape(-1, shape[-1])), (lambda a: a.reshape(shape))
            g = view(grads[n])
        d, m2, v2 = adamw(view(weights[n]), g, view(m_in[n]), view(v_in[n]))
        delta_w[n], new_m[n], new_v[n] = unview(d), unview(m2), unview(v2)

    return (loss, dx0[None], *[grads[n] for n in names], *[delta_w[n] for n in names],
            *[new_m[n] for n in names], *[new_v[n] for n in names])
```

```python
import functools
import math

import jax
import jax.numpy as jnp
from jax import lax
from jax.experimental import pallas as pl
from jax.experimental.pallas import tpu as pltpu

F32 = jnp.float32
BF16 = jnp.bfloat16

D_MODEL = 1024
D_FF = 2816
N_HEADS = 8
HEAD_DIM = 64
N_KV_HEADS = 2
KV_GROUP = N_HEADS // N_KV_HEADS
D_ATTN = N_HEADS * HEAD_DIM
D_KV = N_KV_HEADS * HEAD_DIM
D_QKV = D_ATTN + 2 * D_KV
N_SGU_GROUPS = 8
SGU_GROUP_DIM = 64
D_SGU = N_SGU_GROUPS * SGU_GROUP_DIM
CHUNK = 128
GRID_W = 64
ROPE_THETA = 10000.0
EPS = 1e-6
N_DEV = 8
LANES = 128

ONES_ROWS = 16
SAFE_SCORE_BOUND = 60.0
LOG2_E = math.log2(math.e)
Q_SCALE = HEAD_DIM ** -0.5 * LOG2_E

ADAM_LR = 0.001
ADAM_B1 = 0.9
ADAM_B2 = 0.999
ADAM_EPS = 1e-08
ADAM_WD = 0.01
ADAM_STEP = 10

MESH_IDS = pl.DeviceIdType.MESH

VMEM_LIMIT = 56 * 1024 * 1024

SHARD_ROWS = (("w1_gate", D_FF // N_DEV), ("w1_up", D_FF // N_DEV), ("w1_down", D_FF // N_DEV),
              ("w_in", (D_QKV + 2 * D_SGU) // N_DEV), ("w_out", D_MODEL // N_DEV),
              ("w2_gate", D_FF // N_DEV), ("w2_up", D_FF // N_DEV), ("w2_down", D_FF // N_DEV))
TRANSPOSED = ("w1_gate", "w1_up", "w_in", "w2_gate", "w2_up")


def _params(n_grid):
    return pltpu.CompilerParams(dimension_semantics=("arbitrary",) * n_grid, vmem_limit_bytes=VMEM_LIMIT)


def _dot(a, b):
    return jnp.dot(a, b, preferred_element_type=F32)


def _dot_nt(a, b):
    return lax.dot_general(a, b, (((1,), (1,)), ((), ())), preferred_element_type=F32)


def _dot_tn(a, b):
    return lax.dot_general(a, b, (((0,), (0,)), ((), ())), preferred_element_type=F32)


def _dot_f32(a, b):
    return jnp.dot(a, b, preferred_element_type=F32, precision=lax.Precision.HIGHEST)


def _dot_split(a, b):
    hi = a.astype(BF16)
    lo = (a - hi.astype(F32)).astype(BF16)
    return _dot(hi, b) + _dot(lo, b)


def _rstd(x):
    return lax.rsqrt(jnp.mean(x * x, axis=-1, keepdims=True) + EPS)


def _rms_bwd(dy, n, r, g):
    dn = dy * g
    return r * (dn - n * jnp.mean(dn * n, axis=-1, keepdims=True)), dy * n


def _colsum(a):
    return jnp.sum(a, axis=0, keepdims=True)


_GELU_C = math.sqrt(2.0 / math.pi)


def _gelu(x):
    t = jnp.tanh(_GELU_C * (x + 0.044715 * (x * x * x)))
    return x * (0.5 * (1.0 + t)), t


def _gelu_grad(x, t):
    return 0.5 * (1.0 + t) + 0.5 * x * (1.0 - t * t) * (_GELU_C * (1.0 + 3 * 0.044715 * x * x))


def _pair_swap(a):
    w = a.shape[-1]
    lane = lax.broadcasted_iota(jnp.int32, a.shape, a.ndim - 1)
    return jnp.where(lane % 2 == 0, pltpu.roll(a, w - 1, a.ndim - 1), pltpu.roll(a, 1, a.ndim - 1))


def _tile_lanes(a, reps):
    return jnp.concatenate([a] * reps, axis=-1) if reps > 1 else a


def _loop_pairs(n, step, carry, per_body=2):
    assert n % per_body == 0 and per_body % 2 == 0, (n, per_body)

    def body(jj, c):
        for u in range(per_body):
            c = step(per_body * jj + u, u % 2, c)
        return c

    return lax.fori_loop(0, n // per_body, body, carry)


def _full(shape):
    nd = len(shape)
    return pl.BlockSpec(shape, lambda *_: (0,) * nd)


def _mesh_pos():
    return lax.axis_index("x"), lax.axis_index("y"), lax.axis_index("c")


def _peer(pos, d):
    x, y, c = pos
    px = 1 - x if d & 4 else x
    py = 1 - y if d & 2 else y
    pc = 1 - c if d & 1 else c
    return (px, py, pc), 4 * px + 2 * py + pc


class _Exchange:
    def __init__(self, operands, out_shape, n_local, plan):
        self.operands = list(operands)
        self.out_shape = list(out_shape)
        self.sem_shapes = [pltpu.SemaphoreType.DMA((N_DEV - 1,)), pltpu.SemaphoreType.DMA((N_DEV - 1,)),
                           pltpu.SemaphoreType.DMA((n_local,))]
        self._plan = plan

    def _copies(self, in_refs, out_refs):
        pos = _mesh_pos()
        return pos, self._plan(4 * pos[0] + 2 * pos[1] + pos[2], in_refs, out_refs)

    def start(self, in_refs, out_refs, sems):
        send_sems, recv_sems, local_sems = sems
        pos, (local, remote, _) = self._copies(in_refs, out_refs)
        for k, (src, dst) in enumerate(local):
            pltpu.make_async_copy(src, dst, local_sems.at[k]).start()
        for d in range(1, N_DEV):
            peer, peer_lin = _peer(pos, d)
            for src, dst in remote(peer_lin):
                pltpu.make_async_remote_copy(src_ref=src, dst_ref=dst, send_sem=send_sems.at[d - 1],
                                             recv_sem=recv_sems.at[d - 1], device_id=peer,
                                             device_id_type=MESH_IDS).start()

    def wait(self, in_refs, out_refs, sems):
        send_sems, recv_sems, local_sems = sems
        pos, (local, _, whole) = self._copies(in_refs, out_refs)
        for d in range(1, N_DEV):
            peer, peer_lin = _peer(pos, d)
            ref = whole(peer_lin)
            everything = pltpu.make_async_remote_copy(src_ref=ref, dst_ref=ref, send_sem=send_sems.at[d - 1],
                                                      recv_sem=recv_sems.at[d - 1], device_id=peer,
                                                      device_id_type=MESH_IDS)
            everything.wait_send()
            everything.wait_recv()
        for k, (src, dst) in enumerate(local):
            pltpu.make_async_copy(src, dst, local_sems.at[k]).wait()


def _offsets(rows):
    offs, o = [], 0
    for r in rows:
        offs.append(o)
        o += r
    return offs


def gather_exchange(src, rows):
    offs = _offsets(rows)

    def plan(me, in_refs, out_refs):
        pieces = [(in_refs[0].at[pl.ds(o, r)], out.at[me]) for o, r, out in zip(offs, rows, out_refs)]
        return pieces, (lambda peer_lin: pieces), (lambda peer_lin: in_refs[0])

    return _Exchange([src], [jax.ShapeDtypeStruct((N_DEV, r) + src.shape[1:], src.dtype) for r in rows],
                     len(rows), plan)


def scatter_exchange(grads):
    rows = [g.shape[0] // N_DEV for g in grads]
    offs = _offsets(rows)

    def plan(me, in_refs, out_refs):
        parts = out_refs[0]

        def slabs(owner):
            return [(g.at[pl.ds(pl.multiple_of(owner * r, 16), r)], parts.at[me, pl.ds(o, r)])
                    for g, o, r in zip(in_refs, offs, rows)]

        return slabs(me), slabs, (lambda peer_lin: parts.at[peer_lin])

    shape = jax.ShapeDtypeStruct((N_DEV, sum(rows)) + grads[0].shape[1:], grads[0].dtype)
    return _Exchange(grads, [shape], len(rows), plan)


def gather_two_level(src, rows, name):
    offs = _offsets(rows)
    n_p = len(rows)

    def body(src_ref, *refs):
        outs, (send_sems, recv_sems, local_sems) = refs[:n_p], refs[n_p:]
        x, y, c = _mesh_pos()
        me, sibling = (x, y, c), (x, y, 1 - c)
        chips = [(1 - x, y), (x, 1 - y), (1 - x, 1 - y)]

        def slab(w, dev):
            return outs[w].at[4 * dev[0] + 2 * dev[1] + dev[2]]

        def copy(w, k, block, to, from_src=False):
            return pltpu.make_async_remote_copy(
                src_ref=src_ref.at[pl.ds(offs[w], rows[w])] if from_src else slab(w, block), dst_ref=slab(w, block),
                send_sem=send_sems.at[w * 7 + k], recv_sem=recv_sems.at[w * 7 + k],
                device_id=to, device_id_type=MESH_IDS)

        mine = [pltpu.make_async_copy(src_ref.at[pl.ds(offs[w], rows[w])], slab(w, me), local_sems.at[w])
                for w in range(n_p)]
        for cp in mine:
            cp.start()
        first = []
        for w in range(n_p):
            first.append(copy(w, 0, me, sibling, True))
            first += [copy(w, 1 + j, me, (*chip, c), True) for j, chip in enumerate(chips)]
        for cp in first:
            cp.start()
        passed = []
        for j, chip in enumerate(chips):
            for w in range(n_p):
                copy(w, 1 + j, (*chip, c), me).wait_recv()
                cp = copy(w, 4 + j, (*chip, c), sibling)
                cp.start()
                passed.append(cp)
        for w in range(n_p):
            copy(w, 0, sibling, me).wait_recv()
            for j, chip in enumerate(chips):
                copy(w, 4 + j, (*chip, 1 - c), me).wait_recv()
        for cp in first + passed:
            cp.wait_send()
        for cp in mine:
            cp.wait()

    any_spec = pl.BlockSpec(memory_space=pl.ANY)
    return pl.pallas_call(
        functools.partial(body), name=name,
        out_shape=[jax.ShapeDtypeStruct((N_DEV, r) + src.shape[1:], src.dtype) for r in rows],
        in_specs=[any_spec], out_specs=[any_spec] * n_p,
        scratch_shapes=[pltpu.SemaphoreType.DMA((7 * n_p,)), pltpu.SemaphoreType.DMA((7 * n_p,)),
                        pltpu.SemaphoreType.DMA((n_p,))],
        compiler_params=pltpu.CompilerParams(has_side_effects=True),
    )(src)


def run_exchange(ex, name):
    n_in, n_out = len(ex.operands), len(ex.out_shape)

    def body(*refs):
        parts = refs[:n_in], refs[n_in:n_in + n_out], refs[n_in + n_out:]
        ex.start(*parts)
        ex.wait(*parts)

    any_spec = pl.BlockSpec(memory_space=pl.ANY)
    return pl.pallas_call(
        functools.partial(body), name=name, out_shape=ex.out_shape,
        in_specs=[any_spec] * n_in, out_specs=[any_spec] * n_out, scratch_shapes=ex.sem_shapes,
        compiler_params=pltpu.CompilerParams(has_side_effects=True),
    )(*ex.operands)


def _pallas(comm, body, *, name, grid, in_specs, out_specs, out_shape, args, scratch_shapes=()):
    params = _params(len(grid))
    if comm is None:
        res = pl.pallas_call(functools.partial(body), name=name, grid=grid, in_specs=list(in_specs),
                             out_specs=list(out_specs), out_shape=list(out_shape),
                             scratch_shapes=list(scratch_shapes), compiler_params=params)(*args)
        return list(res), []
    n_in, n_out, n_scr = len(in_specs), len(out_specs), len(scratch_shapes)
    c_in, c_out = len(comm.operands), len(comm.out_shape)

    def edge(last):
        conds = [pl.program_id(a) == (g - 1 if last else 0) for a, g in enumerate(grid)]
        return functools.reduce(jnp.logical_and, conds)

    def wrapped(*refs):
        refs = list(refs)
        ins, refs = refs[:n_in], refs[n_in:]
        cins, refs = refs[:c_in], refs[c_in:]
        outs, refs = refs[:n_out], refs[n_out:]
        couts, refs = refs[:c_out], refs[c_out:]
        scr, sems = refs[:n_scr], refs[n_scr:]

        @pl.when(edge(False))
        def _():
            comm.start(cins, couts, sems)

        body(*ins, *outs, *scr)

        @pl.when(edge(True))
        def _():
            comm.wait(cins, couts, sems)

    any_spec = pl.BlockSpec(memory_space=pl.ANY)
    res = pl.pallas_call(
        wrapped, name=name, grid=grid,
        in_specs=list(in_specs) + [any_spec] * c_in, out_specs=list(out_specs) + [any_spec] * c_out,
        out_shape=list(out_shape) + comm.out_shape, scratch_shapes=list(scratch_shapes) + comm.sem_shapes,
        compiler_params=pltpu.CompilerParams(dimension_semantics=("arbitrary",) * len(grid),
                                             vmem_limit_bytes=VMEM_LIMIT, has_side_effects=True),
    )(*args, *comm.operands)
    return res[:n_out], res[n_out:]


def sum_parts(parts, block_rows):
    n, rows, cols = parts.shape

    def body(p_ref, o_ref):
        acc = p_ref[0].astype(F32)
        for s in range(1, n):
            acc = acc + p_ref[s].astype(F32)
        o_ref[...] = acc

    return pl.pallas_call(
        functools.partial(body), name="sum_parts",
        grid=(rows // block_rows,),
        in_specs=[pl.BlockSpec((n, block_rows, cols), lambda i: (0, i, 0))],
        out_specs=pl.BlockSpec((block_rows, cols), lambda i: (i, 0)),
        out_shape=jax.ShapeDtypeStruct((rows, cols), F32),
        compiler_params=_params(1),
    )(parts)


def adamw(w, g, m, v):
    def body(w_ref, g_ref, m_ref, v_ref, d_ref, m_out, v_out):
        gg = g_ref[...]
        m2 = ADAM_B1 * m_ref[...] + (1.0 - ADAM_B1) * gg
        v2 = ADAM_B2 * v_ref[...] + (1.0 - ADAM_B2) * (gg * gg)
        m_hat = m2 / (1.0 - ADAM_B1 ** ADAM_STEP)
        v_hat = v2 / (1.0 - ADAM_B2 ** ADAM_STEP)
        d_ref[...] = -ADAM_LR * (m_hat / (jnp.sqrt(v_hat) + ADAM_EPS) + ADAM_WD * w_ref[...])
        m_out[...] = m2
        v_out[...] = v2

    spec = _full(w.shape)
    shape = jax.ShapeDtypeStruct(w.shape, F32)
    return pl.pallas_call(
        functools.partial(body), name="adamw",
        in_specs=[spec] * 4, out_specs=[spec] * 3, out_shape=[shape] * 3,
        compiler_params=pltpu.CompilerParams(vmem_limit_bytes=VMEM_LIMIT),
    )(w, g, m, v)


def ffn_up(x, g, wg_t, wu_t, tm, tn, comm=None):
    t = x.shape[0]

    def body(x_ref, g_ref, wg_ref, wu_ref, h_ref, silu_ref, dgate_ref, act_ref):
        xx = x_ref[...]
        h = ((xx * _rstd(xx)) * g_ref[...]).astype(BF16)
        h_ref[...] = h
        for c in range(D_FF // tn):
            cols = slice(c * tn, (c + 1) * tn)
            a = _dot_nt(h, wg_ref[cols, :])
            b = _dot_nt(h, wu_ref[cols, :])
            sig = 0.5 * jnp.tanh(0.5 * a) + 0.5
            silu = a * sig
            silu_ref[:, cols] = silu.astype(BF16)
            dgate_ref[:, cols] = (b * (sig + silu * (1.0 - sig))).astype(BF16)
            act_ref[:, cols] = (silu * b).astype(BF16)

    wide = jax.ShapeDtypeStruct((t, D_FF), BF16)
    row = lambda n: pl.BlockSpec((tm, n), lambda i: (i, 0))
    return _pallas(
        comm, body, name="ffn_up",
        grid=(t // tm,),
        in_specs=[row(D_MODEL), _full((1, D_MODEL)), _full((D_FF, D_MODEL)), _full((D_FF, D_MODEL))],
        out_specs=[row(D_MODEL), row(D_FF), row(D_FF), row(D_FF)],
        out_shape=[jax.ShapeDtypeStruct((t, D_MODEL), BF16), wide, wide, wide],
        args=(x, g, wg_t, wu_t))


def ffn_down(act, wd, x, tm):
    t = x.shape[0]

    def body(act_ref, wd_ref, x_ref, o_ref):
        o_ref[...] = x_ref[...] + 0.5 * _dot(act_ref[...], wd_ref[...])

    return pl.pallas_call(
        functools.partial(body), name="ffn_down",
        grid=(t // tm,),
        in_specs=[pl.BlockSpec((tm, D_FF), lambda i: (i, 0)), _full((D_FF, D_MODEL)),
                  pl.BlockSpec((tm, D_MODEL), lambda i: (i, 0))],
        out_specs=pl.BlockSpec((tm, D_MODEL), lambda i: (i, 0)),
        out_shape=jax.ShapeDtypeStruct((t, D_MODEL), F32),
        compiler_params=_params(1),
    )(act, wd, x)


def ffn_bwd_act(dx, wd, silu, dgate, tm, tn, comm=None):
    t = dx.shape[0]

    def body(dx_ref, wd_ref, silu_ref, dgate_ref, da_ref, db_ref):
        dxb = (0.5 * dx_ref[...]).astype(BF16)
        for c in range(D_FF // tn):
            cols = slice(c * tn, (c + 1) * tn)
            dact = _dot_nt(dxb, wd_ref[cols, :])
            da_ref[:, cols] = (dact * dgate_ref[:, cols].astype(F32)).astype(BF16)
            db_ref[:, cols] = (dact * silu_ref[:, cols].astype(F32)).astype(BF16)

    wide = jax.ShapeDtypeStruct((t, D_FF), BF16)
    row = lambda n: pl.BlockSpec((tm, n), lambda i: (i, 0))
    return _pallas(
        comm, body, name="ffn_bwd_act",
        grid=(t // tm,),
        in_specs=[row(D_MODEL), _full((D_FF, D_MODEL)), row(D_FF), row(D_FF)],
        out_specs=[row(D_FF), row(D_FF)],
        out_shape=[wide, wide],
        args=(dx, wd, silu, dgate))


def norm_bwd_matmul(a1, w1, a2, w2, x, g, dx_in, tm, comm=None):
    t = x.shape[0]
    k1, k2 = a1.shape[1], a2.shape[1]

    def body(a1_ref, w1_ref, a2_ref, w2_ref, x_ref, g_ref, dxin_ref, dx_ref, dg_ref):
        dh = _dot(a1_ref[...], w1_ref[...]) + _dot(a2_ref[...], w2_ref[...])
        xx = x_ref[...]
        r = _rstd(xx)
        dx, dg_rows = _rms_bwd(dh, xx * r, r, g_ref[...])
        dx_ref[...] = dxin_ref[...] + dx

        @pl.when(pl.program_id(0) == 0)
        def _():
            dg_ref[...] = jnp.zeros_like(dg_ref)

        dg_ref[...] += _colsum(dg_rows)

    row = pl.BlockSpec((tm, D_MODEL), lambda i: (i, 0))
    return _pallas(
        comm, body, name="norm_bwd_matmul",
        grid=(t // tm,),
        in_specs=[pl.BlockSpec((tm, k1), lambda i: (i, 0)), _full((k1, D_MODEL)),
                  pl.BlockSpec((tm, k2), lambda i: (i, 0)), _full((k2, D_MODEL)),
                  row, _full((1, D_MODEL)), row],
        out_specs=[row, _full((1, D_MODEL))],
        out_shape=[jax.ShapeDtypeStruct((t, D_MODEL), F32), jax.ShapeDtypeStruct((1, D_MODEL), F32)],
        args=(a1, w1, a2, w2, x, g, dx_in))


def matmul_tn(a, b, scale, tmm, tk, comm=None):
    t, m = a.shape
    n = b.shape[1]
    nk = t // tk

    def body(a_ref, b_ref, o_ref, acc_ref):
        k = pl.program_id(1)

        @pl.when(k == 0)
        def _():
            acc_ref[...] = jnp.zeros_like(acc_ref)

        acc_ref[...] += _dot_tn(a_ref[...].astype(BF16), b_ref[...].astype(BF16))

        @pl.when(k == nk - 1)
        def _():
            o_ref[...] = (scale * acc_ref[...]).astype(BF16)

    (out,), comm_outs = _pallas(
        comm, body, name="matmul_tn",
        grid=(m // tmm, nk),
        in_specs=[pl.BlockSpec((tk, tmm), lambda i, k: (k, i)), pl.BlockSpec((tk, n), lambda i, k: (k, 0))],
        out_specs=[pl.BlockSpec((tmm, n), lambda i, k: (i, 0))],
        out_shape=[jax.ShapeDtypeStruct((m, n), BF16)],
        scratch_shapes=[pltpu.VMEM((tmm, n), F32)],
        args=(a, b))
    return out if comm is None else (out, comm_outs)


def input_projection(x, g, w_qkv_t, w_z_t, tm):
    t = x.shape[0]

    def body(x_ref, g_ref, wq_ref, wz_ref, qkv_ref, z_ref, h_ref):
        xx = x_ref[...]
        h = ((xx * _rstd(xx)) * g_ref[...]).astype(BF16)
        h_ref[...] = h
        qkv_ref[...] = _dot_nt(h, wq_ref[...])
        z_ref[...] = _dot_nt(h, wz_ref[...])

    row = lambda n: pl.BlockSpec((tm, n), lambda i: (i, 0))
    return pl.pallas_call(
        functools.partial(body), name="input_projection", grid=(t // tm,),
        in_specs=[row(D_MODEL), _full((1, D_MODEL)), _full((D_QKV, D_MODEL)), _full((2 * D_SGU, D_MODEL))],
        out_specs=[row(D_QKV), row(2 * D_SGU), row(D_MODEL)],
        out_shape=[jax.ShapeDtypeStruct((t, D_QKV), F32), jax.ShapeDtypeStruct((t, 2 * D_SGU), F32),
                   jax.ShapeDtypeStruct((t, D_MODEL), BF16)],
        compiler_params=_params(1))(x, g, w_qkv_t, w_z_t)


def _shift_rows(shape, first, second):
    row = lax.broadcasted_iota(jnp.int32, shape, len(shape) - 2)
    return jnp.where(row == 0, first, jnp.where(row == 1, second, 0.0))


def _hi_lo(a):
    hi = a.astype(BF16).astype(F32)
    return hi, a - hi


def _head_tile_spec(tm, rows):
    return pl.BlockSpec((N_HEADS, None, rows, tm), lambda i: (0, i, 0, 0))


def _to_head_tiles(a):
    return a.T.reshape(N_HEADS, HEAD_DIM, a.shape[0])


def _from_head_tiles(a):
    return a.reshape(D_ATTN, a.shape[-1]).T


def _head_mean_matrix(width):
    head = jnp.arange(width) // HEAD_DIM
    return (head[:, None] == head[None, :]).astype(F32) / HEAD_DIM


def _kv_tile_spec(n_sub, rows, cols):
    return pl.BlockSpec((N_KV_HEADS, n_sub, rows, cols), lambda i: (0, i, 0, 0))


def qk_prep(qkv, gq_w, gk_w, cos_w, sin_w, mean_q, mean_k, tm, tk, tk_v):
    t = qkv.shape[0]
    n_sub, n_sub_v = tm // tk, tm // tk_v

    def body(p_ref, gq_ref, gk_ref, cos_ref, sin_ref, mq_ref, mk_ref, q_ref, k_ref, kt_ref, vt_ref, vtb_ref,
             qmax_ref, kmax_ref):
        @pl.when(pl.program_id(0) == 0)
        def _():
            qmax_ref[...] = jnp.zeros_like(qmax_ref)
            kmax_ref[...] = jnp.zeros_like(kmax_ref)

        cos2, sin2 = cos_ref[...], sin_ref[...]
        q = p_ref[:, :D_ATTN]
        k = p_ref[:, D_ATTN:D_ATTN + D_KV]
        qn = q * lax.rsqrt(_dot_split(q * q, mq_ref[...]) + EPS) * gq_ref[...]
        kn = k * lax.rsqrt(_dot_split(k * k, mk_ref[...]) + EPS) * gk_ref[...]
        cos8, sin8 = _tile_lanes(cos2, D_ATTN // LANES), _tile_lanes(sin2, D_ATTN // LANES)
        q_rot = (qn * cos8 + _pair_swap(qn) * sin8) * Q_SCALE
        q_ref[...] = _to_head_tiles(q_rot).astype(BF16)
        k_rot = kn * cos2 + _pair_swap(kn) * sin2
        q_sq = HEAD_DIM * _dot_split(q_rot * q_rot, mq_ref[...])
        k_sq = HEAD_DIM * _dot_split(k_rot * k_rot, mk_ref[...])
        qmax_ref[...] = jnp.maximum(qmax_ref[...], jnp.max(q_sq, axis=0, keepdims=True))
        kmax_ref[...] = jnp.maximum(kmax_ref[...], jnp.max(k_sq, axis=0, keepdims=True))
        vv = p_ref[:, D_ATTN + D_KV:]
        second = pltpu.roll(k_rot, HEAD_DIM, 1)
        for c in range(n_sub):
            rows = slice(c * tk, (c + 1) * tk)
            k_ref[0, c] = k_rot[rows, :HEAD_DIM].astype(BF16)
            k_ref[1, c] = second[rows, :HEAD_DIM].astype(BF16)
        for a, feat_ref, width, n in ((k_rot, kt_ref, tk, n_sub), (vv, vtb_ref, tk, n_sub), (vv, vt_ref, tk_v, n_sub_v)):
            for c in range(n):
                tile = a[c * width:(c + 1) * width].T.reshape(N_KV_HEADS, HEAD_DIM, width)
                feat_ref[:, c, :HEAD_DIM, :] = tile.astype(BF16)
        vt_ref[:, :, HEAD_DIM:, :] = jnp.ones((N_KV_HEADS, n_sub_v, ONES_ROWS, tk_v), BF16)
        minus = _shift_rows((N_KV_HEADS, n_sub, HEAD_DIM, tk), -1.0, -1.0).astype(BF16)
        kt_ref[:, :, HEAD_DIM:, :] = minus
        vtb_ref[:, :, HEAD_DIM:, :] = minus

    kv = lambda rows, cols: jax.ShapeDtypeStruct((N_KV_HEADS, t // tk, rows, cols), BF16)
    return pl.pallas_call(
        functools.partial(body), name="qk_prep", grid=(t // tm,),
        in_specs=[pl.BlockSpec((tm, D_QKV), lambda i: (i, 0)), _full((1, D_ATTN)), _full((1, D_KV)),
                  pl.BlockSpec((tm, LANES), lambda i: (i, 0)), pl.BlockSpec((tm, LANES), lambda i: (i, 0)),
                  _full((D_ATTN, D_ATTN)), _full((D_KV, D_KV))],
        out_specs=[_head_tile_spec(tm, HEAD_DIM), _kv_tile_spec(n_sub, tk, HEAD_DIM),
                   _kv_tile_spec(n_sub, 2 * HEAD_DIM, tk),
                   _kv_tile_spec(n_sub_v, HEAD_DIM + ONES_ROWS, tk_v), _kv_tile_spec(n_sub, 2 * HEAD_DIM, tk),
                   _full((1, D_ATTN)), _full((1, D_KV))],
        out_shape=[jax.ShapeDtypeStruct((N_HEADS, t // tm, HEAD_DIM, tm), BF16), kv(tk, HEAD_DIM), kv(2 * HEAD_DIM, tk),
                   jax.ShapeDtypeStruct((N_KV_HEADS, t // tk_v, HEAD_DIM + ONES_ROWS, tk_v), BF16),
                   kv(2 * HEAD_DIM, tk),
                   jax.ShapeDtypeStruct((1, D_ATTN), F32), jax.ShapeDtypeStruct((1, D_KV), F32)],
        compiler_params=_params(1),
    )(qkv, gq_w, gk_w, cos_w, sin_w, mean_q, mean_k)


def qk_bwd(dq_rot, dk_rot, dv, qkv, gq_w, gk_w, cos_w, sin_w, mean_q, mean_k, tm):
    t = qkv.shape[0]
    tk = dk_rot.shape[-1]
    n_sub = tm // tk

    def token_major(ref):
        return jnp.concatenate([ref[:, c].reshape(D_KV, tk).T for c in range(n_sub)], axis=0)

    def branch(raw, d_rot, gain, mean_mat, cos, sin, scale):
        r = lax.rsqrt(_dot_split(raw * raw, mean_mat) + EPS)
        n = raw * r
        dy = (d_rot * cos - _pair_swap(d_rot) * sin) * scale
        dn = dy * gain
        return r * (dn - n * _dot_split(dn * n, mean_mat)), dy * n

    def body(dq_ref, dk_ref, dv_ref, p_ref, gq_ref, gk_ref, cos_ref, sin_ref, mq_ref, mk_ref,
             dp_ref, dgq_ref, dgk_ref):
        cos2, sin2 = cos_ref[...], sin_ref[...]
        cos8, sin8 = _tile_lanes(cos2, D_ATTN // LANES), _tile_lanes(sin2, D_ATTN // LANES)
        dq, dgq = branch(p_ref[:, :D_ATTN], _from_head_tiles(dq_ref[...]), gq_ref[...], mq_ref[...], cos8, sin8,
                         HEAD_DIM ** -0.5)
        dk, dgk = branch(p_ref[:, D_ATTN:D_ATTN + D_KV], token_major(dk_ref), gk_ref[...], mk_ref[...], cos2, sin2, 1.0)
        dp_ref[...] = jnp.concatenate([dq, dk, token_major(dv_ref)], axis=-1).astype(BF16)

        @pl.when(pl.program_id(0) == 0)
        def _():
            dgq_ref[...] = jnp.zeros_like(dgq_ref)
            dgk_ref[...] = jnp.zeros_like(dgk_ref)

        dgq_ref[...] += _colsum(dgq)
        dgk_ref[...] += _colsum(dgk)

    return pl.pallas_call(
        functools.partial(body), name="qk_bwd", grid=(t // tm,),
        in_specs=[_head_tile_spec(tm, HEAD_DIM), _kv_tile_spec(n_sub, HEAD_DIM, tk),
                  _kv_tile_spec(n_sub, HEAD_DIM, tk), pl.BlockSpec((tm, D_QKV), lambda i: (i, 0)),
                  _full((1, D_ATTN)), _full((1, D_KV)),
                  pl.BlockSpec((tm, LANES), lambda i: (i, 0)), pl.BlockSpec((tm, LANES), lambda i: (i, 0)),
                  _full((D_ATTN, D_ATTN)), _full((D_KV, D_KV))],
        out_specs=[pl.BlockSpec((tm, D_QKV), lambda i: (i, 0)), _full((1, D_ATTN)), _full((1, D_KV))],
        out_shape=[jax.ShapeDtypeStruct((t, D_QKV), BF16), jax.ShapeDtypeStruct((1, D_ATTN), F32),
                   jax.ShapeDtypeStruct((1, D_KV), F32)],
        compiler_params=_params(1),
    )(dq_rot, dk_rot, dv, qkv, gq_w, gk_w, cos_w, sin_w, mean_q, mean_k)


def attention_fwd(bound, q_t, k, v_t, comm=None):
    _, nq, _, tq = q_t.shape
    _, nk, tk, _ = k.shape

    def body(bound_ref, q_ref, k_ref, v_ref, o_ref, qtok_ref, s_scr, p_scr):
        head_bound = bound_ref[pl.program_id(0)]
        safe = head_bound <= SAFE_SCORE_BOUND
        q = q_ref[...]
        s_scr[0] = _dot(k_ref[0], q)
        p_scr[1] = jnp.zeros((tk, tq), BF16)
        zero = jnp.zeros((HEAD_DIM + ONES_ROWS, tq), F32)

        def matmuls(j, slot):
            pv = _dot(v_ref[jnp.maximum(j - 1, 0)], p_scr[1 - slot])
            s_scr[1 - slot] = _dot(k_ref[jnp.minimum(j + 1, nk - 1)], q)
            return pv

        def finish(m, acc):
            acc = acc + _dot(v_ref[nk - 1], p_scr[(nk - 1) % 2])
            l = acc[HEAD_DIM:HEAD_DIM + 1]
            o_ref[...] = acc[:HEAD_DIM] / l
            lse_rows = _shift_rows((HEAD_DIM, tq), *_hi_lo(m + jnp.log2(l)))
            qtok_ref[...] = jnp.concatenate([q.astype(F32), lse_rows], axis=0).T.astype(BF16)

        @pl.when(safe)
        def _():
            m = jnp.full((1, tq), head_bound, F32)

            def step(j, slot, acc):
                s = s_scr[slot]
                pv = matmuls(j, slot)
                p_scr[slot] = jnp.exp2(s - m).astype(BF16)
                return acc + pv

            finish(m, _loop_pairs(nk, step, zero, 16 if nk % 16 == 0 else 2))

        @pl.when(jnp.logical_not(safe))
        def _():
            def step(j, slot, carry):
                m, acc = carry
                s = s_scr[slot]
                pv = matmuls(j, slot)
                m_new = jnp.maximum(m, jnp.max(s, axis=0, keepdims=True))
                p_scr[slot] = jnp.exp2(s - m_new).astype(BF16)
                return m_new, jnp.exp2(m - m_new) * (acc + pv)

            finish(*_loop_pairs(nk, step, (jnp.full((1, tq), -1e30, F32), zero)))

    return _pallas(
        comm, body, name="attention_fwd", grid=(N_HEADS, nq),
        in_specs=[pl.BlockSpec(memory_space=pltpu.SMEM),
                  pl.BlockSpec((None, None, HEAD_DIM, tq), lambda h, i: (h, i, 0, 0)),
                  pl.BlockSpec((None, nk, tk, HEAD_DIM), lambda h, i: (h // KV_GROUP, 0, 0, 0)),
                  pl.BlockSpec((None, nk, HEAD_DIM + ONES_ROWS, tk), lambda h, i: (h // KV_GROUP, 0, 0, 0))],
        out_specs=[pl.BlockSpec((None, None, HEAD_DIM, tq), lambda h, i: (h, i, 0, 0)),
                   pl.BlockSpec((None, None, tq, 2 * HEAD_DIM), lambda h, i: (h, i, 0, 0))],
        out_shape=[jax.ShapeDtypeStruct((N_HEADS, nq, HEAD_DIM, tq), F32),
                   jax.ShapeDtypeStruct((N_HEADS, nq, tq, 2 * HEAD_DIM), BF16)],
        scratch_shapes=[pltpu.VMEM((2, tk, tq), F32), pltpu.VMEM((2, tk, tq), BF16)],
        args=(bound, q_t, k, v_t))


def attention_bwd(q_tok, do_tok, q_t, do_t, k_t, v_t, comm=None):
    _, nq, _, tq = q_t.shape
    _, nk, _, tk = k_t.shape

    def body(qtok_ref, dotok_ref, q_ref, do_ref, kt_ref, vt_ref, dq_ref, dk_ref, dv_ref,
             s_scr, dp_scr, p_scr, ds_scr):
        @pl.when(pl.program_id(1) == 0)
        def _():
            dq_ref[...] = jnp.zeros_like(dq_ref)

        kt_aug, vt_aug = kt_ref[...], vt_ref[...]
        kt = kt_aug[:HEAD_DIM]
        n = KV_GROUP * nq
        s_scr[0] = _dot(qtok_ref[0, 0], kt_aug)
        dp_scr[0] = _dot(dotok_ref[0, 0], vt_aug)
        p_scr[1] = jnp.zeros((tq, tk), BF16)
        ds_scr[1] = jnp.zeros((tq, tk), BF16)

        def products(t, slot, dk, dv):
            h, i = t // nq, t % nq
            ds = ds_scr[slot]
            dq_ref[h, i] += _dot_nt(kt, ds)
            return dk + _dot(q_ref[h, i], ds), dv + _dot(do_ref[h, i], p_scr[slot])

        def step(t, slot, carry):
            s, dp = s_scr[slot], dp_scr[slot]
            dk, dv = products(jnp.maximum(t - 1, 0), 1 - slot, *carry)
            nxt = jnp.minimum(t + 1, n - 1)
            s_scr[1 - slot] = _dot(qtok_ref[nxt // nq, nxt % nq], kt_aug)
            dp_scr[1 - slot] = _dot(dotok_ref[nxt // nq, nxt % nq], vt_aug)
            p = jnp.exp2(s)
            p_scr[slot] = p.astype(BF16)
            ds_scr[slot] = (p * dp).astype(BF16)
            return dk, dv

        zero = jnp.zeros((HEAD_DIM, tk), F32)
        dk, dv = products(n - 1, (n - 1) % 2, *_loop_pairs(n, step, (zero, zero), 16 if n % 16 == 0 else 2))
        dk_ref[...] = dk * (1.0 / LOG2_E)
        dv_ref[...] = dv

    group = lambda g, j: (g, 0, 0, 0)
    tile = lambda g, j: (g, j, 0, 0)
    once = pl.Buffered(1)
    return _pallas(
        comm, body, name="attention_bwd", grid=(N_KV_HEADS, nk),
        in_specs=[pl.BlockSpec((KV_GROUP, nq, tq, 2 * HEAD_DIM), group, pipeline_mode=once),
                  pl.BlockSpec((KV_GROUP, nq, tq, 2 * HEAD_DIM), group, pipeline_mode=once),
                  pl.BlockSpec((KV_GROUP, nq, HEAD_DIM, tq), group, pipeline_mode=once),
                  pl.BlockSpec((KV_GROUP, nq, HEAD_DIM, tq), group, pipeline_mode=once),
                  pl.BlockSpec((None, None, 2 * HEAD_DIM, tk), tile),
                  pl.BlockSpec((None, None, 2 * HEAD_DIM, tk), tile)],
        out_specs=[pl.BlockSpec((KV_GROUP, nq, HEAD_DIM, tq), group),
                   pl.BlockSpec((None, None, HEAD_DIM, tk), tile),
                   pl.BlockSpec((None, None, HEAD_DIM, tk), tile)],
        out_shape=[jax.ShapeDtypeStruct((N_HEADS, nq, HEAD_DIM, tq), F32),
                   jax.ShapeDtypeStruct((N_KV_HEADS, nk, HEAD_DIM, tk), F32),
                   jax.ShapeDtypeStruct((N_KV_HEADS, nk, HEAD_DIM, tk), F32)],
        scratch_shapes=[pltpu.VMEM((2, tq, tk), F32), pltpu.VMEM((2, tq, tk), F32),
                        pltpu.VMEM((2, tq, tk), BF16), pltpu.VMEM((2, tq, tk), BF16)],
        args=(q_tok, do_tok, q_t, do_t, k_t, v_t))


def _group_matmul(a_t, w_ref):
    return jnp.concatenate([_dot(a_t[g * SGU_GROUP_DIM:(g + 1) * SGU_GROUP_DIM], w_ref[g])
                            for g in range(N_SGU_GROUPS)], axis=0)


def _gate_forward(z, g_sgu, wst_ref, bias):
    gz, th = _gelu(z)
    u, vv = gz[:, :D_SGU], gz[:, D_SGU:]
    rv = _rstd(vv)
    nv = vv * rv
    vn = nv * g_sgu
    v_chunks, fs = [], []
    for c in range(z.shape[0] // CHUNK):
        vt = vn[c * CHUNK:(c + 1) * CHUNK].T.astype(BF16)
        v_chunks.append(vt)
        fs.append(_group_matmul(vt, wst_ref).T + bias)
    f = jnp.concatenate(fs, axis=0) if len(fs) > 1 else fs[0]
    return th, u, rv, nv, v_chunks, f


def mix_out(z, o, x, g_sgu, g_ao, g_so, ws_t, bias, w_out, tm):
    t = x.shape[0]

    def body(z_ref, o_ref, x_ref, gs_ref, gao_ref, gso_ref, ws_ref, bias_ref, wout_ref, x2_ref, mixed_ref):
        _, u, _, _, _, f = _gate_forward(z_ref[...], gs_ref[...], ws_ref, bias_ref[...])
        sgu = u * f
        oo = _from_head_tiles(o_ref[...])
        mixed = jnp.concatenate([oo * _rstd(oo) * gao_ref[...], sgu * _rstd(sgu) * gso_ref[...]], axis=-1).astype(BF16)
        mixed_ref[...] = mixed
        x2_ref[...] = x_ref[...] + _dot(mixed, wout_ref[...])

    row = lambda n: pl.BlockSpec((tm, n), lambda i: (i, 0))
    return pl.pallas_call(
        functools.partial(body), name="mix_out", grid=(t // tm,),
        in_specs=[row(2 * D_SGU), _head_tile_spec(tm, HEAD_DIM), row(D_MODEL), _full((1, D_SGU)), _full((1, D_ATTN)),
                  _full((1, D_SGU)),
                  _full((N_SGU_GROUPS, CHUNK, CHUNK)), _full((CHUNK, D_SGU)), _full((D_MODEL, D_MODEL))],
        out_specs=[row(D_MODEL), row(D_MODEL)],
        out_shape=[jax.ShapeDtypeStruct((t, D_MODEL), F32), jax.ShapeDtypeStruct((t, D_MODEL), BF16)],
        compiler_params=_params(1),
    )(z, o, x, g_sgu, g_ao, g_so, ws_t, bias, w_out)


def mix_bwd(dx2, z, o, g_sgu, g_ao, g_so, ws, ws_t, bias, w_out, group_ind, tm):
    t = dx2.shape[0]
    n_tiles = t // tm

    def body(dx_ref, z_ref, o_ref, gs_ref, gao_ref, gso_ref, ws_ref, wst_ref, bias_ref, wout_ref, ind_ref,
             do_ref, dotok_ref, dz_ref, dg_ref, dws_ref, dbs_ref, df_sum):
        step = pl.program_id(0)

        @pl.when(step == 0)
        def _():
            dg_ref[...] = jnp.zeros_like(dg_ref)
            dws_ref[...] = jnp.zeros_like(dws_ref)
            df_sum[...] = jnp.zeros_like(df_sum)

        z = z_ref[...]
        th, u, rv, nv, v_chunks, f = _gate_forward(z, gs_ref[...], wst_ref, bias_ref[...])
        dmixed = _dot_nt(dx_ref[...].astype(BF16), wout_ref[...])
        o_tiles = o_ref[...]
        oo = _from_head_tiles(o_tiles)
        ro = _rstd(oo)
        d_o, dgao = _rms_bwd(dmixed[:, :D_ATTN], oo * ro, ro, gao_ref[...])
        do_tiles = _to_head_tiles(d_o)
        do_ref[...] = do_tiles.astype(BF16)
        delta_hi, delta_lo = _hi_lo(jnp.sum(do_tiles * o_tiles, axis=1, keepdims=True))
        for h in range(N_HEADS):
            delta_rows = _shift_rows((HEAD_DIM, tm), delta_hi[h], delta_lo[h])
            dotok_ref[h] = jnp.concatenate([do_tiles[h], delta_rows], axis=0).T.astype(BF16)
        sgu = u * f
        rs = _rstd(sgu)
        dsgu, dgso = _rms_bwd(dmixed[:, D_ATTN:], sgu * rs, rs, gso_ref[...])
        du = dsgu * f
        df = dsgu * u
        dvns = []
        df_acc = jnp.zeros((CHUNK, D_SGU), F32)
        for c in range(tm // CHUNK):
            dfc32 = df[c * CHUNK:(c + 1) * CHUNK]
            dft = dfc32.T.astype(BF16)
            dvns.append(_group_matmul(dft, ws_ref).T)
            for g in range(N_SGU_GROUPS):
                rows = slice(g * SGU_GROUP_DIM, (g + 1) * SGU_GROUP_DIM)
                dws_ref[g] += _dot_tn(dft[rows], v_chunks[c][rows])
            df_acc = df_acc + dfc32
        df_sum[...] += df_acc
        dvn = jnp.concatenate(dvns, axis=0) if len(dvns) > 1 else dvns[0]
        dvv, dgs = _rms_bwd(dvn, nv, rv, gs_ref[...])
        dz_ref[...] = (jnp.concatenate([du, dvv], axis=-1) * _gelu_grad(z, th)).astype(BF16)
        dg_ref[0:1, :] += _colsum(dgao)
        dg_ref[1:2, :] += _colsum(dgso)
        dg_ref[2:3, :] += _colsum(dgs)

        @pl.when(step == n_tiles - 1)
        def _():
            dbs_ref[...] = _dot_f32(df_sum[...], ind_ref[...])

    row = lambda n: pl.BlockSpec((tm, n), lambda i: (i, 0))
    return pl.pallas_call(
        functools.partial(body), name="mix_bwd", grid=(n_tiles,),
        in_specs=[row(D_MODEL), row(2 * D_SGU), _head_tile_spec(tm, HEAD_DIM), _full((1, D_SGU)), _full((1, D_ATTN)),
                  _full((1, D_SGU)),
                  _full((N_SGU_GROUPS, CHUNK, CHUNK)), _full((N_SGU_GROUPS, CHUNK, CHUNK)), _full((CHUNK, D_SGU)),
                  _full((D_MODEL, D_MODEL)), _full((D_SGU, LANES))],
        out_specs=[_head_tile_spec(tm, HEAD_DIM), pl.BlockSpec((N_HEADS, None, tm, 2 * HEAD_DIM), lambda i: (0, i, 0, 0)),
                   row(2 * D_SGU), _full((8, D_SGU)),
                   _full((N_SGU_GROUPS, CHUNK, CHUNK)), _full((CHUNK, LANES))],
        out_shape=[jax.ShapeDtypeStruct((N_HEADS, n_tiles, HEAD_DIM, tm), BF16),
                   jax.ShapeDtypeStruct((N_HEADS, n_tiles, tm, 2 * HEAD_DIM), BF16),
                   jax.ShapeDtypeStruct((t, 2 * D_SGU), BF16),
                   jax.ShapeDtypeStruct((8, D_SGU), F32),
                   jax.ShapeDtypeStruct((N_SGU_GROUPS, CHUNK, CHUNK), F32),
                   jax.ShapeDtypeStruct((CHUNK, LANES), F32)],
        scratch_shapes=[pltpu.VMEM((CHUNK, D_SGU), F32)],
        compiler_params=_params(1),
    )(dx2, z, o, g_sgu, g_ao, g_so, ws, ws_t, bias, w_out, group_ind)


def ffn_down_loss(act, wd, x, g, target, tm):
    t = x.shape[0]

    def body(act_ref, wd_ref, x_ref, g_ref, t_ref, loss_ref, dx_ref, dg_ref):
        @pl.when(pl.program_id(0) == 0)
        def _():
            loss_ref[...] = jnp.zeros_like(loss_ref)
            dg_ref[...] = jnp.zeros_like(dg_ref)

        xx = x_ref[...] + 0.5 * _dot(act_ref[...], wd_ref[...])
        r = _rstd(xx)
        n = xx * r
        err = n * g_ref[...] - t_ref[...]
        per_token = jnp.mean(err * err, axis=-1, keepdims=True)
        loss_ref[...] += 0.5 * jnp.sum(per_token, axis=0, keepdims=True)
        dx, dg_rows = _rms_bwd(err * (1.0 / D_MODEL), n, r, g_ref[...])
        dx_ref[...] = dx
        dg_ref[...] += _colsum(dg_rows)

    row = pl.BlockSpec((tm, D_MODEL), lambda i: (i, 0))
    return pl.pallas_call(
        functools.partial(body), name="ffn_down_loss", grid=(t // tm,),
        in_specs=[pl.BlockSpec((tm, D_FF), lambda i: (i, 0)), _full((D_FF, D_MODEL)), row, _full((1, D_MODEL)), row],
        out_specs=[_full((1, LANES)), row, _full((1, D_MODEL))],
        out_shape=[jax.ShapeDtypeStruct((1, LANES), F32), jax.ShapeDtypeStruct((t, D_MODEL), F32),
                   jax.ShapeDtypeStruct((1, D_MODEL), F32)],
        compiler_params=_params(1),
    )(act, wd, x, g, target)


def _rope_tables(t):
    rows = t // GRID_W
    row_idx = jnp.repeat(jnp.arange(rows, dtype=F32), GRID_W)
    col_idx = jnp.tile(jnp.arange(GRID_W, dtype=F32), rows)
    axis_dim = HEAD_DIM // 2
    inv = 1.0 / (ROPE_THETA ** (jnp.arange(0, axis_dim, 2, dtype=F32) / axis_dim))
    ang = jnp.concatenate([row_idx[:, None] * inv, col_idx[:, None] * inv], axis=-1)
    cos = jnp.repeat(jnp.cos(ang), 2, axis=-1)
    sin = jnp.repeat(jnp.sin(ang), 2, axis=-1) * jnp.tile(jnp.array([-1.0, 1.0], F32), HEAD_DIM // 2)
    return jnp.tile(cos, (1, LANES // HEAD_DIM)), jnp.tile(sin, (1, LANES // HEAD_DIM))


def kernel(x, g_ffn1, w1_gate, w1_up, w1_down, g_mix, w_in, g_q, g_k, g_sgu, w_s, b_s, g_attn_out, g_sgu_out, w_out, g_ffn2, w2_gate, w2_up, w2_down, g_final, loss_target, m_g_ffn1, m_w1_gate, m_w1_up, m_w1_down, m_g_mix, m_w_in, m_g_q, m_g_k, m_g_sgu, m_w_s, m_b_s, m_g_attn_out, m_g_sgu_out, m_w_out, m_g_ffn2, m_w2_gate, m_w2_up, m_w2_down, m_g_final, v_g_ffn1, v_w1_gate, v_w1_up, v_w1_down, v_g_mix, v_w_in, v_g_q, v_g_k, v_g_sgu, v_w_s, v_b_s, v_g_attn_out, v_g_sgu_out, v_w_out, v_g_ffn2, v_w2_gate, v_w2_up, v_w2_down, v_g_final):
    weights = dict(g_ffn1=g_ffn1, w1_gate=w1_gate, w1_up=w1_up, w1_down=w1_down, g_mix=g_mix, w_in=w_in, g_q=g_q,
                   g_k=g_k, g_sgu=g_sgu, w_s=w_s, b_s=b_s, g_attn_out=g_attn_out, g_sgu_out=g_sgu_out, w_out=w_out,
                   g_ffn2=g_ffn2, w2_gate=w2_gate, w2_up=w2_up, w2_down=w2_down, g_final=g_final)
    m_in = dict(g_ffn1=m_g_ffn1, w1_gate=m_w1_gate, w1_up=m_w1_up, w1_down=m_w1_down, g_mix=m_g_mix, w_in=m_w_in,
                g_q=m_g_q, g_k=m_g_k, g_sgu=m_g_sgu, w_s=m_w_s, b_s=m_b_s, g_attn_out=m_g_attn_out,
                g_sgu_out=m_g_sgu_out, w_out=m_w_out, g_ffn2=m_g_ffn2, w2_gate=m_w2_gate, w2_up=m_w2_up,
                w2_down=m_w2_down, g_final=m_g_final)
    v_in = dict(g_ffn1=v_g_ffn1, w1_gate=v_w1_gate, w1_up=v_w1_up, w1_down=v_w1_down, g_mix=v_g_mix, w_in=v_w_in,
                g_q=v_g_q, g_k=v_g_k, g_sgu=v_g_sgu, w_s=v_w_s, b_s=v_b_s, g_attn_out=v_g_attn_out,
                g_sgu_out=v_g_sgu_out, w_out=v_w_out, g_ffn2=v_g_ffn2, w2_gate=v_w2_gate, w2_up=v_w2_up,
                w2_down=v_w2_down, g_final=v_g_final)
    names = list(weights)

    t = x.shape[1]
    x0 = x[0]
    target = loss_target[0]
    tm = min(512, t)
    tm_ff = min(256, t)
    tn_ff = 256
    tq = min(512, t)
    tk = min(256, t)
    tk_fwd = min(512, t)
    tk_w = min(2048, t)

    def shard_rows(name):
        w = weights[name][0]
        return (w.T if name in TRANSPOSED else w).astype(BF16)

    rows_of = dict(SHARD_ROWS)
    full = {}

    def packed(group):
        return jnp.concatenate([shard_rows(n) for n in group], axis=0), [rows_of[n] for n in group]

    def gather_of(group):
        return gather_exchange(*packed(group))

    def take(group, gathered):
        for n, g in zip(group, gathered):
            full[n] = g.reshape(N_DEV * rows_of[n], D_MODEL)

    first, second, third = ("w1_gate", "w1_up"), ("w1_down", "w_in", "w_out"), ("w2_gate", "w2_up", "w2_down")
    take(first, gather_two_level(*packed(first), "gather_first"))

    (h1, a1, b1, act1), gathered = ffn_up(x0, g_ffn1, full["w1_gate"], full["w1_up"], tm_ff, tn_ff, gather_of(second))
    take(second, gathered)
    w_in_t = full["w_in"]
    w_qkv_t, w_z_t = w_in_t[:D_QKV], w_in_t[D_QKV:]
    x1 = ffn_down(act1, full["w1_down"], x0, tm)

    qkv, z, h2 = input_projection(x1, g_mix, w_qkv_t, w_z_t, tm)
    cos_w, sin_w = _rope_tables(t)
    gq_w = jnp.tile(g_q, (1, N_HEADS))
    gk_w = jnp.tile(g_k, (1, N_KV_HEADS))
    mean_q, mean_k = _head_mean_matrix(D_ATTN).astype(BF16), _head_mean_matrix(D_KV).astype(BF16)
    q_t, k_tiles, kt_tiles, vt_tiles, vt_tiles_bwd, q_sq_max, k_sq_max = qk_prep(
        qkv, gq_w, gk_w, cos_w, sin_w, mean_q, mean_k, tq, tk, tk_fwd)
    score_bound = 1.02 * jnp.sqrt(q_sq_max.reshape(N_HEADS, HEAD_DIM)[:, 0]
                                  * jnp.repeat(k_sq_max.reshape(N_KV_HEADS, HEAD_DIM)[:, 0], KV_GROUP))
    k_tiles_fwd = k_tiles.reshape(N_KV_HEADS, t // tk_fwd, tk_fwd, HEAD_DIM)
    (o_t, q_tok), gathered = attention_fwd(score_bound, q_t, k_tiles_fwd, vt_tiles, gather_of(third))
    take(third, gathered)

    ws_b = w_s[0].astype(BF16)
    ws_tb = jnp.swapaxes(w_s[0], 1, 2).astype(BF16)
    bias = jnp.repeat(b_s[0].T, SGU_GROUP_DIM, axis=1)
    x2, mixed = mix_out(z, o_t, x1, g_sgu, g_attn_out, g_sgu_out, ws_tb, bias, full["w_out"], tq)

    (h3, a2, b2, act2), _ = ffn_up(x2, g_ffn2, full["w2_gate"], full["w2_up"], tm_ff, tn_ff)

    loss_part, dx3, dg_final = ffn_down_loss(act2, full["w2_down"], x2, g_final, target, tm)

    tmm = D_FF // 2
    (da2, db2), _ = ffn_bwd_act(dx3, full["w2_down"], a2, b2, tm_ff, tn_ff)
    (dx2, dg_ffn2), _ = norm_bwd_matmul(da2, full["w2_gate"], db2, full["w2_up"], x2, g_ffn2, dx3, tm)
    dwg2, dwu2 = matmul_tn(da2, h3, 1.0, tmm, tk_w), matmul_tn(db2, h3, 1.0, tmm, tk_w)
    dwd2 = matmul_tn(act2, dx3, 0.5, tmm, tk_w)

    group_ind = (jnp.arange(D_SGU)[:, None] // SGU_GROUP_DIM == jnp.arange(LANES)[None, :]).astype(F32)
    do_t, do_tok, dz, dg_mixrow, dws, dbs = mix_bwd(dx2, z, o_t, g_sgu, g_attn_out, g_sgu_out, ws_b, ws_tb, bias,
                                                    full["w_out"], group_ind, tq)
    dw_out = matmul_tn(mixed, dx2, 1.0, D_MODEL // 2, tk_w)

    group_a = ("w2_gate", "w2_up", "w2_down", "w_out")
    (dq_t, dk_t, dv_t), (parts_a,) = attention_bwd(q_tok, do_tok, q_t, do_t, kt_tiles, vt_tiles_bwd,
                                                   scatter_exchange([dwg2, dwu2, dwd2, dw_out]))
    dqkv, dgq_w, dgk_w = qk_bwd(dq_t, dk_t, dv_t, qkv, gq_w, gk_w, cos_w, sin_w, mean_q, mean_k, tq)

    def pack_small(arrays):
        pieces = []
        for a in arrays:
            flat = a.reshape(-1)
            pieces.append(jnp.pad(flat, (0, (-flat.shape[0]) % (8 * LANES))).reshape(-1, LANES))
        return jnp.concatenate(pieces, axis=0), [p.shape[0] for p in pieces]

    early = dict(g_ffn2=dg_ffn2, g_final=dg_final, g_q=dgq_w.reshape(N_HEADS, HEAD_DIM).sum(0),
                 g_k=dgk_w.reshape(N_KV_HEADS, HEAD_DIM).sum(0), g_attn_out=dg_mixrow[0], g_sgu_out=dg_mixrow[1],
                 g_sgu=dg_mixrow[2], w_s=dws, b_s=dbs[:, :N_SGU_GROUPS].T)
    early_pack, early_rows = pack_small(list(early.values()))
    (dx1, dg_mix), (early_parts,) = norm_bwd_matmul(dqkv, w_qkv_t, dz, w_z_t, x1, g_mix, dx2, tm,
                                                    gather_exchange(early_pack, [early_pack.shape[0]]))
    dw_in = jnp.concatenate([matmul_tn(dqkv, h2, 1.0, D_QKV // 2, tk_w), matmul_tn(dz, h2, 1.0, D_SGU, tk_w)], axis=0)

    dwd1, (parts_in,) = matmul_tn(act1, dx1, 0.5, tmm, tk_w, scatter_exchange([dw_in]))
    (da1, db1), (parts_d1,) = ffn_bwd_act(dx1, full["w1_down"], a1, b1, tm_ff, tn_ff, scatter_exchange([dwd1]))
    dwg1 = matmul_tn(da1, h1, 1.0, tmm, tk_w)
    dwu1, (parts_g1,) = matmul_tn(db1, h1, 1.0, tmm, tk_w, scatter_exchange([dwg1]))
    (dx0, dg_ffn1), (parts_u1,) = norm_bwd_matmul(da1, full["w1_gate"], db1, full["w1_up"], x0, g_ffn1, dx1, tm,
                                                  scatter_exchange([dwu1]))
    scattered = ((group_a, parts_a), (("w_in",), parts_in), (("w1_down",), parts_d1), (("w1_gate",), parts_g1),
                 (("w1_up",), parts_u1))

    late = dict(g_mix=dg_mix, g_ffn1=dg_ffn1, loss=loss_part)
    late_pack, late_rows = pack_small(list(late.values()))
    (late_parts,) = run_exchange(gather_exchange(late_pack, [late_pack.shape[0]]), "gather_late_small_grads")
    small_sums = {}
    for entries, rows, parts in ((early, early_rows, early_parts), (late, late_rows, late_parts)):
        summed = sum_parts(parts, parts.shape[1])
        off = 0
        for n, r in zip(entries, rows):
            small_sums[n] = summed[off:off + r]
            off += r
    loss = small_sums.pop("loss")[0, 0]

    grads, row_grads = {}, {}
    for group, parts in scattered:
        rows = parts.shape[1]
        summed = sum_parts(parts, rows if rows <= 2 * rows_of["w1_gate"] else rows // 2)
        off = 0
        for n in group:
            row_grads[n] = summed[off:off + rows_of[n]]
            grads[n] = (row_grads[n].T if n in TRANSPOSED else row_grads[n])[None]
            off += rows_of[n]
    for n, summed in small_sums.items():
        grads[n] = summed.reshape(-1)[:weights[n].size].reshape(weights[n].shape)

    delta_w, new_m, new_v = {}, {}, {}
    for n in names:
        shape = weights[n].shape
        if n in TRANSPOSED:
            view, unview, g = (lambda a: a[0].T), (lambda a: a.T[None]), row_grads[n]
        else:
            view, unview = (lambda a: a.reshape(-1, shape[-1])), (lambda a: a.reshape(shape))
            g = view(grads[n])
        d, m2, v2 = adamw(view(weights[n]), g, view(m_in[n]), view(v_in[n]))
        delta_w[n], new_m[n], new_v[n] = unview(d), unview(m2), unview(v2)

    return (loss, dx0[None], *[grads[n] for n in names], *[delta_w[n] for n in names],
            *[new_m[n] for n in names], *[new_v[n] for n in names])
```

```python
import functools
import math

import jax
import jax.numpy as jnp
from jax import lax
from jax.experimental import pallas as pl
from jax.experimental.pallas import tpu as pltpu

F32 = jnp.float32
BF16 = jnp.bfloat16

D_MODEL = 1024
D_FF = 2816
N_HEADS = 8
HEAD_DIM = 64
N_KV_HEADS = 2
KV_GROUP = N_HEADS // N_KV_HEADS
D_ATTN = N_HEADS * HEAD_DIM
D_KV = N_KV_HEADS * HEAD_DIM
D_QKV = D_ATTN + 2 * D_KV
N_SGU_GROUPS = 8
SGU_GROUP_DIM = 64
D_SGU = N_SGU_GROUPS * SGU_GROUP_DIM
CHUNK = 128
GRID_W = 64
ROPE_THETA = 10000.0
EPS = 1e-6
N_DEV = 8
LANES = 128

ONES_ROWS = 16
SAFE_SCORE_BOUND = 60.0
LOG2_E = math.log2(math.e)
Q_SCALE = HEAD_DIM ** -0.5 * LOG2_E

ADAM_LR = 0.001
ADAM_B1 = 0.9
ADAM_B2 = 0.999
ADAM_EPS = 1e-08
ADAM_WD = 0.01
ADAM_STEP = 10

MESH_IDS = pl.DeviceIdType.MESH

VMEM_LIMIT = 56 * 1024 * 1024

SHARD_ROWS = (("w1_gate", D_FF // N_DEV), ("w1_up", D_FF // N_DEV), ("w1_down", D_FF // N_DEV),
              ("w_in", (D_QKV + 2 * D_SGU) // N_DEV), ("w_out", D_MODEL // N_DEV),
              ("w2_gate", D_FF // N_DEV), ("w2_up", D_FF // N_DEV), ("w2_down", D_FF // N_DEV))
TRANSPOSED = ("w1_gate", "w1_up", "w_in", "w2_gate", "w2_up")


def _params(n_grid):
    return pltpu.CompilerParams(dimension_semantics=("arbitrary",) * n_grid, vmem_limit_bytes=VMEM_LIMIT)


def _dot(a, b):
    return jnp.dot(a, b, preferred_element_type=F32)


def _dot_nt(a, b):
    return lax.dot_general(a, b, (((1,), (1,)), ((), ())), preferred_element_type=F32)


def _dot_tn(a, b):
    return lax.dot_general(a, b, (((0,), (0,)), ((), ())), preferred_element_type=F32)


def _dot_f32(a, b):
    return jnp.dot(a, b, preferred_element_type=F32, precision=lax.Precision.HIGHEST)


def _dot_split(a, b):
    hi = a.astype(BF16)
    lo = (a - hi.astype(F32)).astype(BF16)
    return _dot(hi, b) + _dot(lo, b)


def _rstd(x):
    return lax.rsqrt(jnp.mean(x * x, axis=-1, keepdims=True) + EPS)


def _rms_bwd(dy, n, r, g):
    dn = dy * g
    return r * (dn - n * jnp.mean(dn * n, axis=-1, keepdims=True)), dy * n


def _colsum(a):
    return jnp.sum(a, axis=0, keepdims=True)


_GELU_C = math.sqrt(2.0 / math.pi)


def _gelu(x):
    t = jnp.tanh(_GELU_C * (x + 0.044715 * (x * x * x)))
    return x * (0.5 * (1.0 + t)), t


def _gelu_grad(x, t):
    return 0.5 * (1.0 + t) + 0.5 * x * (1.0 - t * t) * (_GELU_C * (1.0 + 3 * 0.044715 * x * x))


def _pair_swap(a):
    w = a.shape[-1]
    lane = lax.broadcasted_iota(jnp.int32, a.shape, a.ndim - 1)
    return jnp.where(lane % 2 == 0, pltpu.roll(a, w - 1, a.ndim - 1), pltpu.roll(a, 1, a.ndim - 1))


def _tile_lanes(a, reps):
    return jnp.concatenate([a] * reps, axis=-1) if reps > 1 else a


def _loop_pairs(n, step, carry, per_body=2):
    assert n % per_body == 0 and per_body % 2 == 0, (n, per_body)

    def body(jj, c):
        for u in range(per_body):
            c = step(per_body * jj + u, u % 2, c)
        return c

    return lax.fori_loop(0, n // per_body, body, carry)


def _full(shape):
    nd = len(shape)
    return pl.BlockSpec(shape, lambda *_: (0,) * nd)


def _mesh_pos():
    return lax.axis_index("x"), lax.axis_index("y"), lax.axis_index("c")


def _peer(pos, d):
    x, y, c = pos
    px = 1 - x if d & 4 else x
    py = 1 - y if d & 2 else y
    pc = 1 - c if d & 1 else c
    return (px, py, pc), 4 * px + 2 * py + pc


class _Exchange:
    def __init__(self, operands, out_shape, n_local, plan):
        self.operands = list(operands)
        self.out_shape = list(out_shape)
        self.sem_shapes = [pltpu.SemaphoreType.DMA((N_DEV - 1,)), pltpu.SemaphoreType.DMA((N_DEV - 1,)),
                           pltpu.SemaphoreType.DMA((n_local,))]
        self._plan = plan

    def _copies(self, in_refs, out_refs):
        pos = _mesh_pos()
        return pos, self._plan(4 * pos[0] + 2 * pos[1] + pos[2], in_refs, out_refs)

    def start(self, in_refs, out_refs, sems):
        send_sems, recv_sems, local_sems = sems
        pos, (local, remote, _) = self._copies(in_refs, out_refs)
        for k, (src, dst) in enumerate(local):
            pltpu.make_async_copy(src, dst, local_sems.at[k]).start()
        for d in range(1, N_DEV):
            peer, peer_lin = _peer(pos, d)
            for src, dst in remote(peer_lin):
                pltpu.make_async_remote_copy(src_ref=src, dst_ref=dst, send_sem=send_sems.at[d - 1],
                                             recv_sem=recv_sems.at[d - 1], device_id=peer,
                                             device_id_type=MESH_IDS).start()

    def wait(self, in_refs, out_refs, sems):
        send_sems, recv_sems, local_sems = sems
        pos, (local, _, whole) = self._copies(in_refs, out_refs)
        for d in range(1, N_DEV):
            peer, peer_lin = _peer(pos, d)
            ref = whole(peer_lin)
            everything = pltpu.make_async_remote_copy(src_ref=ref, dst_ref=ref, send_sem=send_sems.at[d - 1],
                                                      recv_sem=recv_sems.at[d - 1], device_id=peer,
                                                      device_id_type=MESH_IDS)
            everything.wait_send()
            everything.wait_recv()
        for k, (src, dst) in enumerate(local):
            pltpu.make_async_copy(src, dst, local_sems.at[k]).wait()


def _offsets(rows):
    offs, o = [], 0
    for r in rows:
        offs.append(o)
        o += r
    return offs


def gather_exchange(src, rows):
    offs = _offsets(rows)

    def plan(me, in_refs, out_refs):
        pieces = [(in_refs[0].at[pl.ds(o, r)], out.at[me]) for o, r, out in zip(offs, rows, out_refs)]
        return pieces, (lambda peer_lin: pieces), (lambda peer_lin: in_refs[0])

    return _Exchange([src], [jax.ShapeDtypeStruct((N_DEV, r) + src.shape[1:], src.dtype) for r in rows],
                     len(rows), plan)


def scatter_exchange(grads):
    rows = [g.shape[0] // N_DEV for g in grads]
    offs = _offsets(rows)

    def plan(me, in_refs, out_refs):
        parts = out_refs[0]

        def slabs(owner):
            return [(g.at[pl.ds(pl.multiple_of(owner * r, 16), r)], parts.at[me, pl.ds(o, r)])
                    for g, o, r in zip(in_refs, offs, rows)]

        return slabs(me), slabs, (lambda peer_lin: parts.at[peer_lin])

    shape = jax.ShapeDtypeStruct((N_DEV, sum(rows)) + grads[0].shape[1:], grads[0].dtype)
    return _Exchange(grads, [shape], len(rows), plan)


def gather_two_level(src, rows, name):
    offs = _offsets(rows)
    n_p = len(rows)

    def body(src_ref, *refs):
        outs, (send_sems, recv_sems, local_sems) = refs[:n_p], refs[n_p:]
        x, y, c = _mesh_pos()
        me, sibling = (x, y, c), (x, y, 1 - c)
        chips = [(1 - x, y), (x, 1 - y), (1 - x, 1 - y)]

        def slab(w, dev):
            return outs[w].at[4 * dev[0] + 2 * dev[1] + dev[2]]

        def copy(w, k, block, to, from_src=False):
            return pltpu.make_async_remote_copy(
                src_ref=src_ref.at[pl.ds(offs[w], rows[w])] if from_src else slab(w, block), dst_ref=slab(w, block),
                send_sem=send_sems.at[w * 7 + k], recv_sem=recv_sems.at[w * 7 + k],
                device_id=to, device_id_type=MESH_IDS)

        mine = [pltpu.make_async_copy(src_ref.at[pl.ds(offs[w], rows[w])], slab(w, me), local_sems.at[w])
                for w in range(n_p)]
        for cp in mine:
            cp.start()
        first = []
        for w in range(n_p):
            first.append(copy(w, 0, me, sibling, True))
            first += [copy(w, 1 + j, me, (*chip, c), True) for j, chip in enumerate(chips)]
        for cp in first:
            cp.start()
        passed = []
        for j, chip in enumerate(chips):
            for w in range(n_p):
                copy(w, 1 + j, (*chip, c), me).wait_recv()
                cp = copy(w, 4 + j, (*chip, c), sibling)
                cp.start()
                passed.append(cp)
        for w in range(n_p):
            copy(w, 0, sibling, me).wait_recv()
            for j, chip in enumerate(chips):
                copy(w, 4 + j, (*chip, 1 - c), me).wait_recv()
        for cp in first + passed:
            cp.wait_send()
        for cp in mine:
            cp.wait()

    any_spec = pl.BlockSpec(memory_space=pl.ANY)
    return pl.pallas_call(
        functools.partial(body), name=name,
        out_shape=[jax.ShapeDtypeStruct((N_DEV, r) + src.shape[1:], src.dtype) for r in rows],
        in_specs=[any_spec], out_specs=[any_spec] * n_p,
        scratch_shapes=[pltpu.SemaphoreType.DMA((7 * n_p,)), pltpu.SemaphoreType.DMA((7 * n_p,)),
                        pltpu.SemaphoreType.DMA((n_p,))],
        compiler_params=pltpu.CompilerParams(has_side_effects=True),
    )(src)


def run_exchange(ex, name):
    n_in, n_out = len(ex.operands), len(ex.out_shape)

    def body(*refs):
        parts = refs[:n_in], refs[n_in:n_in + n_out], refs[n_in + n_out:]
        ex.start(*parts)
        ex.wait(*parts)

    any_spec = pl.BlockSpec(memory_space=pl.ANY)
    return pl.pallas_call(
        functools.partial(body), name=name, out_shape=ex.out_shape,
        in_specs=[any_spec] * n_in, out_specs=[any_spec] * n_out, scratch_shapes=ex.sem_shapes,
        compiler_params=pltpu.CompilerParams(has_side_effects=True),
    )(*ex.operands)


def _pallas(comm, body, *, name, grid, in_specs, out_specs, out_shape, args, scratch_shapes=()):
    params = _params(len(grid))
    if comm is None:
        res = pl.pallas_call(functools.partial(body), name=name, grid=grid, in_specs=list(in_specs),
                             out_specs=list(out_specs), out_shape=list(out_shape),
                             scratch_shapes=list(scratch_shapes), compiler_params=params)(*args)
        return list(res), []
    n_in, n_out, n_scr = len(in_specs), len(out_specs), len(scratch_shapes)
    c_in, c_out = len(comm.operands), len(comm.out_shape)

    def edge(last):
        conds = [pl.program_id(a) == (g - 1 if last else 0) for a, g in enumerate(grid)]
        return functools.reduce(jnp.logical_and, conds)

    def wrapped(*refs):
        refs = list(refs)
        ins, refs = refs[:n_in], refs[n_in:]
        cins, refs = refs[:c_in], refs[c_in:]
        outs, refs = refs[:n_out], refs[n_out:]
        couts, refs = refs[:c_out], refs[c_out:]
        scr, sems = refs[:n_scr], refs[n_scr:]

        @pl.when(edge(False))
        def _():
            comm.start(cins, couts, sems)

        body(*ins, *outs, *scr)

        @pl.when(edge(True))
        def _():
            comm.wait(cins, couts, sems)

    any_spec = pl.BlockSpec(memory_space=pl.ANY)
    res = pl.pallas_call(
        wrapped, name=name, grid=grid,
        in_specs=list(in_specs) + [any_spec] * c_in, out_specs=list(out_specs) + [any_spec] * c_out,
        out_shape=list(out_shape) + comm.out_shape, scratch_shapes=list(scratch_shapes) + comm.sem_shapes,
        compiler_params=pltpu.CompilerParams(dimension_semantics=("arbitrary",) * len(grid),
                                             vmem_limit_bytes=VMEM_LIMIT, has_side_effects=True),
    )(*args, *comm.operands)
    return res[:n_out], res[n_out:]


def sum_parts(parts, block_rows):
    n, rows, cols = parts.shape

    def body(p_ref, o_ref):
        acc = p_ref[0].astype(F32)
        for s in range(1, n):
            acc = acc + p_ref[s].astype(F32)
        o_ref[...] = acc

    return pl.pallas_call(
        functools.partial(body), name="sum_parts",
        grid=(rows // block_rows,),
        in_specs=[pl.BlockSpec((n, block_rows, cols), lambda i: (0, i, 0))],
        out_specs=pl.BlockSpec((block_rows, cols), lambda i: (i, 0)),
        out_shape=jax.ShapeDtypeStruct((rows, cols), F32),
        compiler_params=_params(1),
    )(parts)


def adamw(w, g, m, v):
    def body(w_ref, g_ref, m_ref, v_ref, d_ref, m_out, v_out):
        gg = g_ref[...]
        m2 = ADAM_B1 * m_ref[...] + (1.0 - ADAM_B1) * gg
        v2 = ADAM_B2 * v_ref[...] + (1.0 - ADAM_B2) * (gg * gg)
        m_hat = m2 / (1.0 - ADAM_B1 ** ADAM_STEP)
        v_hat = v2 / (1.0 - ADAM_B2 ** ADAM_STEP)
        d_ref[...] = -ADAM_LR * (m_hat / (jnp.sqrt(v_hat) + ADAM_EPS) + ADAM_WD * w_ref[...])
        m_out[...] = m2
        v_out[...] = v2

    spec = _full(w.shape)
    shape = jax.ShapeDtypeStruct(w.shape, F32)
    return pl.pallas_call(
        functools.partial(body), name="adamw",
        in_specs=[spec] * 4, out_specs=[spec] * 3, out_shape=[shape] * 3,
        compiler_params=pltpu.CompilerParams(vmem_limit_bytes=VMEM_LIMIT),
    )(w, g, m, v)


def ffn_up(x, g, wg_t, wu_t, tm, tn, comm=None):
    t = x.shape[0]

    def body(x_ref, g_ref, wg_ref, wu_ref, h_ref, silu_ref, dgate_ref, act_ref):
        xx = x_ref[...]
        h = ((xx * _rstd(xx)) * g_ref[...]).astype(BF16)
        h_ref[...] = h
        for c in range(D_FF // tn):
            cols = slice(c * tn, (c + 1) * tn)
            a = _dot_nt(h, wg_ref[cols, :])
            b = _dot_nt(h, wu_ref[cols, :])
            sig = 0.5 * jnp.tanh(0.5 * a) + 0.5
            silu = a * sig
            silu_ref[:, cols] = silu.astype(BF16)
            dgate_ref[:, cols] = (b * (sig + silu * (1.0 - sig))).astype(BF16)
            act_ref[:, cols] = (silu * b).astype(BF16)

    wide = jax.ShapeDtypeStruct((t, D_FF), BF16)
    row = lambda n: pl.BlockSpec((tm, n), lambda i: (i, 0))
    return _pallas(
        comm, body, name="ffn_up",
        grid=(t // tm,),
        in_specs=[row(D_MODEL), _full((1, D_MODEL)), _full((D_FF, D_MODEL)), _full((D_FF, D_MODEL))],
        out_specs=[row(D_MODEL), row(D_FF), row(D_FF), row(D_FF)],
        out_shape=[jax.ShapeDtypeStruct((t, D_MODEL), BF16), wide, wide, wide],
        args=(x, g, wg_t, wu_t))


def ffn_down(act, wd, x, tm):
    t = x.shape[0]

    def body(act_ref, wd_ref, x_ref, o_ref):
        o_ref[...] = x_ref[...] + 0.5 * _dot(act_ref[...], wd_ref[...])

    return pl.pallas_call(
        functools.partial(body), name="ffn_down",
        grid=(t // tm,),
        in_specs=[pl.BlockSpec((tm, D_FF), lambda i: (i, 0)), _full((D_FF, D_MODEL)),
                  pl.BlockSpec((tm, D_MODEL), lambda i: (i, 0))],
        out_specs=pl.BlockSpec((tm, D_MODEL), lambda i: (i, 0)),
        out_shape=jax.ShapeDtypeStruct((t, D_MODEL), F32),
        compiler_params=_params(1),
    )(act, wd, x)


def ffn_bwd_act(dx, wd, silu, dgate, tm, tn, comm=None):
    t = dx.shape[0]

    def body(dx_ref, wd_ref, silu_ref, dgate_ref, da_ref, db_ref):
        dxb = (0.5 * dx_ref[...]).astype(BF16)
        for c in range(D_FF // tn):
            cols = slice(c * tn, (c + 1) * tn)
            dact = _dot_nt(dxb, wd_ref[cols, :])
            da_ref[:, cols] = (dact * dgate_ref[:, cols].astype(F32)).astype(BF16)
            db_ref[:, cols] = (dact * silu_ref[:, cols].astype(F32)).astype(BF16)

    wide = jax.ShapeDtypeStruct((t, D_FF), BF16)
    row = lambda n: pl.BlockSpec((tm, n), lambda i: (i, 0))
    return _pallas(
        comm, body, name="ffn_bwd_act",
        grid=(t // tm,),
        in_specs=[row(D_MODEL), _full((D_FF, D_MODEL)), row(D_FF), row(D_FF)],
        out_specs=[row(D_FF), row(D_FF)],
        out_shape=[wide, wide],
        args=(dx, wd, silu, dgate))


def norm_bwd_matmul(a1, w1, a2, w2, x, g, dx_in, tm, comm=None):
    t = x.shape[0]
    k1, k2 = a1.shape[1], a2.shape[1]

    def body(a1_ref, w1_ref, a2_ref, w2_ref, x_ref, g_ref, dxin_ref, dx_ref, dg_ref):
        dh = _dot(a1_ref[...], w1_ref[...]) + _dot(a2_ref[...], w2_ref[...])
        xx = x_ref[...]
        r = _rstd(xx)
        dx, dg_rows = _rms_bwd(dh, xx * r, r, g_ref[...])
        dx_ref[...] = dxin_ref[...] + dx

        @pl.when(pl.program_id(0) == 0)
        def _():
            dg_ref[...] = jnp.zeros_like(dg_ref)

        dg_ref[...] += _colsum(dg_rows)

    row = pl.BlockSpec((tm, D_MODEL), lambda i: (i, 0))
    return _pallas(
        comm, body, name="norm_bwd_matmul",
        grid=(t // tm,),
        in_specs=[pl.BlockSpec((tm, k1), lambda i: (i, 0)), _full((k1, D_MODEL)),
                  pl.BlockSpec((tm, k2), lambda i: (i, 0)), _full((k2, D_MODEL)),
                  row, _full((1, D_MODEL)), row],
        out_specs=[row, _full((1, D_MODEL))],
        out_shape=[jax.ShapeDtypeStruct((t, D_MODEL), F32), jax.ShapeDtypeStruct((1, D_MODEL), F32)],
        args=(a1, w1, a2, w2, x, g, dx_in))


def matmul_tn(a, b, scale, tmm, tk, comm=None):
    t, m = a.shape
    n = b.shape[1]
    nk = t // tk

    def body(a_ref, b_ref, o_ref, acc_ref):
        k = pl.program_id(1)

        @pl.when(k == 0)
        def _():
            acc_ref[...] = jnp.zeros_like(acc_ref)

        acc_ref[...] += _dot_tn(a_ref[...].astype(BF16), b_ref[...].astype(BF16))

        @pl.when(k == nk - 1)
        def _():
            o_ref[...] = (scale * acc_ref[...]).astype(BF16)

    (out,), comm_outs = _pallas(
        comm, body, name="matmul_tn",
        grid=(m // tmm, nk),
        in_specs=[pl.BlockSpec((tk, tmm), lambda i, k: (k, i)), pl.BlockSpec((tk, n), lambda i, k: (k, 0))],
        out_specs=[pl.BlockSpec((tmm, n), lambda i, k: (i, 0))],
        out_shape=[jax.ShapeDtypeStruct((m, n), BF16)],
        scratch_shapes=[pltpu.VMEM((tmm, n), F32)],
        args=(a, b))
    return out if comm is None else (out, comm_outs)


def input_projection(x, g, w_qkv_t, w_z_t, tm):
    t = x.shape[0]

    def body(x_ref, g_ref, wq_ref, wz_ref, qkv_ref, z_ref, h_ref):
        xx = x_ref[...]
        h = ((xx * _rstd(xx)) * g_ref[...]).astype(BF16)
        h_ref[...] = h
        qkv_ref[...] = _dot_nt(h, wq_ref[...])
        z_ref[...] = _dot_nt(h, wz_ref[...])

    row = lambda n: pl.BlockSpec((tm, n), lambda i: (i, 0))
    return pl.pallas_call(
        functools.partial(body), name="input_projection", grid=(t // tm,),
        in_specs=[row(D_MODEL), _full((1, D_MODEL)), _full((D_QKV, D_MODEL)), _full((2 * D_SGU, D_MODEL))],
        out_specs=[row(D_QKV), row(2 * D_SGU), row(D_MODEL)],
        out_shape=[jax.ShapeDtypeStruct((t, D_QKV), F32), jax.ShapeDtypeStruct((t, 2 * D_SGU), F32),
                   jax.ShapeDtypeStruct((t, D_MODEL), BF16)],
        compiler_params=_params(1))(x, g, w_qkv_t, w_z_t)


def _shift_rows(shape, first, second):
    row = lax.broadcasted_iota(jnp.int32, shape, len(shape) - 2)
    return jnp.where(row == 0, first, jnp.where(row == 1, second, 0.0))


def _hi_lo(a):
    hi = a.astype(BF16).astype(F32)
    return hi, a - hi


def _head_tile_spec(tm, rows):
    return pl.BlockSpec((N_HEADS, None, rows, tm), lambda i: (0, i, 0, 0))


def _to_head_tiles(a):
    return a.T.reshape(N_HEADS, HEAD_DIM, a.shape[0])


def _from_head_tiles(a):
    return a.reshape(D_ATTN, a.shape[-1]).T


def _head_mean_matrix(width):
    head = jnp.arange(width) // HEAD_DIM
    return (head[:, None] == head[None, :]).astype(F32) / HEAD_DIM


def _kv_tile_spec(n_sub, rows, cols):
    return pl.BlockSpec((N_KV_HEADS, n_sub, rows, cols), lambda i: (0, i, 0, 0))


def qk_prep(qkv, gq_w, gk_w, cos_w, sin_w, mean_q, mean_k, tm, tk, tk_v):
    t = qkv.shape[0]
    n_sub, n_sub_v = tm // tk, tm // tk_v

    def body(p_ref, gq_ref, gk_ref, cos_ref, sin_ref, mq_ref, mk_ref, q_ref, k_ref, kt_ref, vt_ref, vtb_ref,
             qmax_ref, kmax_ref):
        @pl.when(pl.program_id(0) == 0)
        def _():
            qmax_ref[...] = jnp.zeros_like(qmax_ref)
            kmax_ref[...] = jnp.zeros_like(kmax_ref)

        cos2, sin2 = cos_ref[...], sin_ref[...]
        q = p_ref[:, :D_ATTN]
        k = p_ref[:, D_ATTN:D_ATTN + D_KV]
        qn = q * lax.rsqrt(_dot_split(q * q, mq_ref[...]) + EPS) * gq_ref[...]
        kn = k * lax.rsqrt(_dot_split(k * k, mk_ref[...]) + EPS) * gk_ref[...]
        cos8, sin8 = _tile_lanes(cos2, D_ATTN // LANES), _tile_lanes(sin2, D_ATTN // LANES)
        q_rot = (qn * cos8 + _pair_swap(qn) * sin8) * Q_SCALE
        q_ref[...] = _to_head_tiles(q_rot).astype(BF16)
        k_rot = kn * cos2 + _pair_swap(kn) * sin2
        q_sq = HEAD_DIM * _dot_split(q_rot * q_rot, mq_ref[...])
        k_sq = HEAD_DIM * _dot_split(k_rot * k_rot, mk_ref[...])
        qmax_ref[...] = jnp.maximum(qmax_ref[...], jnp.max(q_sq, axis=0, keepdims=True))
        kmax_ref[...] = jnp.maximum(kmax_ref[...], jnp.max(k_sq, axis=0, keepdims=True))
        vv = p_ref[:, D_ATTN + D_KV:]
        second = pltpu.roll(k_rot, HEAD_DIM, 1)
        for c in range(n_sub):
            rows = slice(c * tk, (c + 1) * tk)
            k_ref[0, c] = k_rot[rows, :HEAD_DIM].astype(BF16)
            k_ref[1, c] = second[rows, :HEAD_DIM].astype(BF16)
        for a, feat_ref, width, n in ((k_rot, kt_ref, tk, n_sub), (vv, vtb_ref, tk, n_sub), (vv, vt_ref, tk_v, n_sub_v)):
            for c in range(n):
                tile = a[c * width:(c + 1) * width].T.reshape(N_KV_HEADS, HEAD_DIM, width)
                feat_ref[:, c, :HEAD_DIM, :] = tile.astype(BF16)
        vt_ref[:, :, HEAD_DIM:, :] = jnp.ones((N_KV_HEADS, n_sub_v, ONES_ROWS, tk_v), BF16)
        minus = _shift_rows((N_KV_HEADS, n_sub, HEAD_DIM, tk), -1.0, -1.0).astype(BF16)
        kt_ref[:, :, HEAD_DIM:, :] = minus
        vtb_ref[:, :, HEAD_DIM:, :] = minus

    kv = lambda rows, cols: jax.ShapeDtypeStruct((N_KV_HEADS, t // tk, rows, cols), BF16)
    return pl.pallas_call(
        functools.partial(body), name="qk_prep", grid=(t // tm,),
        in_specs=[pl.BlockSpec((tm, D_QKV), lambda i: (i, 0)), _full((1, D_ATTN)), _full((1, D_KV)),
                  pl.BlockSpec((tm, LANES), lambda i: (i, 0)), pl.BlockSpec((tm, LANES), lambda i: (i, 0)),
                  _full((D_ATTN, D_ATTN)), _full((D_KV, D_KV))],
        out_specs=[_head_tile_spec(tm, HEAD_DIM), _kv_tile_spec(n_sub, tk, HEAD_DIM),
                   _kv_tile_spec(n_sub, 2 * HEAD_DIM, tk),
                   _kv_tile_spec(n_sub_v, HEAD_DIM + ONES_ROWS, tk_v), _kv_tile_spec(n_sub, 2 * HEAD_DIM, tk),
                   _full((1, D_ATTN)), _full((1, D_KV))],
        out_shape=[jax.ShapeDtypeStruct((N_HEADS, t // tm, HEAD_DIM, tm), BF16), kv(tk, HEAD_DIM), kv(2 * HEAD_DIM, tk),
                   jax.ShapeDtypeStruct((N_KV_HEADS, t // tk_v, HEAD_DIM + ONES_ROWS, tk_v), BF16),
                   kv(2 * HEAD_DIM, tk),
                   jax.ShapeDtypeStruct((1, D_ATTN), F32), jax.ShapeDtypeStruct((1, D_KV), F32)],
        compiler_params=_params(1),
    )(qkv, gq_w, gk_w, cos_w, sin_w, mean_q, mean_k)


def qk_bwd(dq_rot, dk_rot, dv, qkv, gq_w, gk_w, cos_w, sin_w, mean_q, mean_k, tm):
    t = qkv.shape[0]
    tk = dk_rot.shape[-1]
    n_sub = tm // tk

    def token_major(ref):
        return jnp.concatenate([ref[:, c].reshape(D_KV, tk).T for c in range(n_sub)], axis=0)

    def branch(raw, d_rot, gain, mean_mat, cos, sin, scale):
        r = lax.rsqrt(_dot_split(raw * raw, mean_mat) + EPS)
        n = raw * r
        dy = (d_rot * cos - _pair_swap(d_rot) * sin) * scale
        dn = dy * gain
        return r * (dn - n * _dot_split(dn * n, mean_mat)), dy * n

    def body(dq_ref, dk_ref, dv_ref, p_ref, gq_ref, gk_ref, cos_ref, sin_ref, mq_ref, mk_ref,
             dp_ref, dgq_ref, dgk_ref):
        cos2, sin2 = cos_ref[...], sin_ref[...]
        cos8, sin8 = _tile_lanes(cos2, D_ATTN // LANES), _tile_lanes(sin2, D_ATTN // LANES)
        dq, dgq = branch(p_ref[:, :D_ATTN], _from_head_tiles(dq_ref[...]), gq_ref[...], mq_ref[...], cos8, sin8,
                         HEAD_DIM ** -0.5)
        dk, dgk = branch(p_ref[:, D_ATTN:D_ATTN + D_KV], token_major(dk_ref), gk_ref[...], mk_ref[...], cos2, sin2, 1.0)
        dp_ref[...] = jnp.concatenate([dq, dk, token_major(dv_ref)], axis=-1).astype(BF16)

        @pl.when(pl.program_id(0) == 0)
        def _():
            dgq_ref[...] = jnp.zeros_like(dgq_ref)
            dgk_ref[...] = jnp.zeros_like(dgk_ref)

        dgq_ref[...] += _colsum(dgq)
        dgk_ref[...] += _colsum(dgk)

    return pl.pallas_call(
        functools.partial(body), name="qk_bwd", grid=(t // tm,),
        in_specs=[_head_tile_spec(tm, HEAD_DIM), _kv_tile_spec(n_sub, HEAD_DIM, tk),
                  _kv_tile_spec(n_sub, HEAD_DIM, tk), pl.BlockSpec((tm, D_QKV), lambda i: (i, 0)),
                  _full((1, D_ATTN)), _full((1, D_KV)),
                  pl.BlockSpec((tm, LANES), lambda i: (i, 0)), pl.BlockSpec((tm, LANES), lambda i: (i, 0)),
                  _full((D_ATTN, D_ATTN)), _full((D_KV, D_KV))],
        out_specs=[pl.BlockSpec((tm, D_QKV), lambda i: (i, 0)), _full((1, D_ATTN)), _full((1, D_KV))],
        out_shape=[jax.ShapeDtypeStruct((t, D_QKV), BF16), jax.ShapeDtypeStruct((1, D_ATTN), F32),
                   jax.ShapeDtypeStruct((1, D_KV), F32)],
        compiler_params=_params(1),
    )(dq_rot, dk_rot, dv, qkv, gq_w, gk_w, cos_w, sin_w, mean_q, mean_k)


def attention_fwd(bound, q_t, k, v_t, comm=None):
    _, nq, _, tq = q_t.shape
    _, nk, tk, _ = k.shape

    def body(bound_ref, q_ref, k_ref, v_ref, o_ref, qtok_ref, s_scr, p_scr):
        head_bound = bound_ref[pl.program_id(0)]
        safe = head_bound <= SAFE_SCORE_BOUND
        q = q_ref[...]
        s_scr[0] = _dot(k_ref[0], q)
        p_scr[1] = jnp.zeros((tk, tq), BF16)
        zero = jnp.zeros((HEAD_DIM + ONES_ROWS, tq), F32)

        def matmuls(j, slot):
            pv = _dot(v_ref[jnp.maximum(j - 1, 0)], p_scr[1 - slot])
            s_scr[1 - slot] = _dot(k_ref[jnp.minimum(j + 1, nk - 1)], q)
            return pv

        def finish(m, acc):
            acc = acc + _dot(v_ref[nk - 1], p_scr[(nk - 1) % 2])
            l = acc[HEAD_DIM:HEAD_DIM + 1]
            o_ref[...] = acc[:HEAD_DIM] / l
            lse_rows = _shift_rows((HEAD_DIM, tq), *_hi_lo(m + jnp.log2(l)))
            qtok_ref[...] = jnp.concatenate([q.astype(F32), lse_rows], axis=0).T.astype(BF16)

        @pl.when(safe)
        def _():
            m = jnp.full((1, tq), head_bound, F32)

            def step(j, slot, acc):
                s = s_scr[slot]
                pv = matmuls(j, slot)
                p_scr[slot] = jnp.exp2(s - m).astype(BF16)
                return acc + pv

            finish(m, _loop_pairs(nk, step, zero, 16 if nk % 16 == 0 else 2))

        @pl.when(jnp.logical_not(safe))
        def _():
            def step(j, slot, carry):
                m, acc = carry
                s = s_scr[slot]
                pv = matmuls(j, slot)
                m_new = jnp.maximum(m, jnp.max(s, axis=0, keepdims=True))
                p_scr[slot] = jnp.exp2(s - m_new).astype(BF16)
                return m_new, jnp.exp2(m - m_new) * (acc + pv)

            finish(*_loop_pairs(nk, step, (jnp.full((1, tq), -1e30, F32), zero)))

    return _pallas(
        comm, body, name="attention_fwd", grid=(N_HEADS, nq),
        in_specs=[pl.BlockSpec(memory_space=pltpu.SMEM),
                  pl.BlockSpec((None, None, HEAD_DIM, tq), lambda h, i: (h, i, 0, 0)),
                  pl.BlockSpec((None, nk, tk, HEAD_DIM), lambda h, i: (h // KV_GROUP, 0, 0, 0)),
                  pl.BlockSpec((None, nk, HEAD_DIM + ONES_ROWS, tk), lambda h, i: (h // KV_GROUP, 0, 0, 0))],
        out_specs=[pl.BlockSpec((None, None, HEAD_DIM, tq), lambda h, i: (h, i, 0, 0)),
                   pl.BlockSpec((None, None, tq, 2 * HEAD_DIM), lambda h, i: (h, i, 0, 0))],
        out_shape=[jax.ShapeDtypeStruct((N_HEADS, nq, HEAD_DIM, tq), F32),
                   jax.ShapeDtypeStruct((N_HEADS, nq, tq, 2 * HEAD_DIM), BF16)],
        scratch_shapes=[pltpu.VMEM((2, tk, tq), F32), pltpu.VMEM((2, tk, tq), BF16)],
        args=(bound, q_t, k, v_t))


def attention_bwd(q_tok, do_tok, q_t, do_t, k_t, v_t, comm=None):
    _, nq, _, tq = q_t.shape
    _, nk, _, tk = k_t.shape

    def body(qtok_ref, dotok_ref, q_ref, do_ref, kt_ref, vt_ref, dq_ref, dk_ref, dv_ref,
             s_scr, dp_scr, p_scr, ds_scr):
        @pl.when(pl.program_id(1) == 0)
        def _():
            dq_ref[...] = jnp.zeros_like(dq_ref)

        kt_aug, vt_aug = kt_ref[...], vt_ref[...]
        kt = kt_aug[:HEAD_DIM]
        n = KV_GROUP * nq
        s_scr[0] = _dot(qtok_ref[0, 0], kt_aug)
        dp_scr[0] = _dot(dotok_ref[0, 0], vt_aug)
        p_scr[1] = jnp.zeros((tq, tk), BF16)
        ds_scr[1] = jnp.zeros((tq, tk), BF16)

        def products(t, slot, dk, dv):
            h, i = t // nq, t % nq
            ds = ds_scr[slot]
            dq_ref[h, i] += _dot_nt(kt, ds)
            return dk + _dot(q_ref[h, i], ds), dv + _dot(do_ref[h, i], p_scr[slot])

        def step(t, slot, carry):
            s, dp = s_scr[slot], dp_scr[slot]
            dk, dv = products(jnp.maximum(t - 1, 0), 1 - slot, *carry)
            nxt = jnp.minimum(t + 1, n - 1)
            s_scr[1 - slot] = _dot(qtok_ref[nxt // nq, nxt % nq], kt_aug)
            dp_scr[1 - slot] = _dot(dotok_ref[nxt // nq, nxt % nq], vt_aug)
            p = jnp.exp2(s)
            p_scr[slot] = p.astype(BF16)
            ds_scr[slot] = (p * dp).astype(BF16)
            return dk, dv

        zero = jnp.zeros((HEAD_DIM, tk), F32)
        dk, dv = products(n - 1, (n - 1) % 2, *_loop_pairs(n, step, (zero, zero), 64 if n % 64 == 0 else 2))
        dk_ref[...] = dk * (1.0 / LOG2_E)
        dv_ref[...] = dv

    group = lambda g, j: (g, 0, 0, 0)
    tile = lambda g, j: (g, j, 0, 0)
    once = pl.Buffered(1)
    return _pallas(
        comm, body, name="attention_bwd", grid=(N_KV_HEADS, nk),
        in_specs=[pl.BlockSpec((KV_GROUP, nq, tq, 2 * HEAD_DIM), group, pipeline_mode=once),
                  pl.BlockSpec((KV_GROUP, nq, tq, 2 * HEAD_DIM), group, pipeline_mode=once),
                  pl.BlockSpec((KV_GROUP, nq, HEAD_DIM, tq), group, pipeline_mode=once),
                  pl.BlockSpec((KV_GROUP, nq, HEAD_DIM, tq), group, pipeline_mode=once),
                  pl.BlockSpec((None, None, 2 * HEAD_DIM, tk), tile),
                  pl.BlockSpec((None, None, 2 * HEAD_DIM, tk), tile)],
        out_specs=[pl.BlockSpec((KV_GROUP, nq, HEAD_DIM, tq), group),
                   pl.BlockSpec((None, None, HEAD_DIM, tk), tile),
                   pl.BlockSpec((None, None, HEAD_DIM, tk), tile)],
        out_shape=[jax.ShapeDtypeStruct((N_HEADS, nq, HEAD_DIM, tq), F32),
                   jax.ShapeDtypeStruct((N_KV_HEADS, nk, HEAD_DIM, tk), F32),
                   jax.ShapeDtypeStruct((N_KV_HEADS, nk, HEAD_DIM, tk), F32)],
        scratch_shapes=[pltpu.VMEM((2, tq, tk), F32), pltpu.VMEM((2, tq, tk), F32),
                        pltpu.VMEM((2, tq, tk), BF16), pltpu.VMEM((2, tq, tk), BF16)],
        args=(q_tok, do_tok, q_t, do_t, k_t, v_t))


def _group_matmul(a_t, w_ref):
    return jnp.concatenate([_dot(a_t[g * SGU_GROUP_DIM:(g + 1) * SGU_GROUP_DIM], w_ref[g])
                            for g in range(N_SGU_GROUPS)], axis=0)


def _gate_forward(z, g_sgu, wst_ref, bias):
    gz, th = _gelu(z)
    u, vv = gz[:, :D_SGU], gz[:, D_SGU:]
    rv = _rstd(vv)
    nv = vv * rv
    vn = nv * g_sgu
    v_chunks, fs = [], []
    for c in range(z.shape[0] // CHUNK):
        vt = vn[c * CHUNK:(c + 1) * CHUNK].T.astype(BF16)
        v_chunks.append(vt)
        fs.append(_group_matmul(vt, wst_ref).T + bias)
    f = jnp.concatenate(fs, axis=0) if len(fs) > 1 else fs[0]
    return th, u, rv, nv, v_chunks, f


def mix_out(z, o, x, g_sgu, g_ao, g_so, ws_t, bias, w_out, tm):
    t = x.shape[0]

    def body(z_ref, o_ref, x_ref, gs_ref, gao_ref, gso_ref, ws_ref, bias_ref, wout_ref, x2_ref, mixed_ref):
        _, u, _, _, _, f = _gate_forward(z_ref[...], gs_ref[...], ws_ref, bias_ref[...])
        sgu = u * f
        oo = _from_head_tiles(o_ref[...])
        mixed = jnp.concatenate([oo * _rstd(oo) * gao_ref[...], sgu * _rstd(sgu) * gso_ref[...]], axis=-1).astype(BF16)
        mixed_ref[...] = mixed
        x2_ref[...] = x_ref[...] + _dot(mixed, wout_ref[...])

    row = lambda n: pl.BlockSpec((tm, n), lambda i: (i, 0))
    return pl.pallas_call(
        functools.partial(body), name="mix_out", grid=(t // tm,),
        in_specs=[row(2 * D_SGU), _head_tile_spec(tm, HEAD_DIM), row(D_MODEL), _full((1, D_SGU)), _full((1, D_ATTN)),
                  _full((1, D_SGU)),
                  _full((N_SGU_GROUPS, CHUNK, CHUNK)), _full((CHUNK, D_SGU)), _full((D_MODEL, D_MODEL))],
        out_specs=[row(D_MODEL), row(D_MODEL)],
        out_shape=[jax.ShapeDtypeStruct((t, D_MODEL), F32), jax.ShapeDtypeStruct((t, D_MODEL), BF16)],
        compiler_params=_params(1),
    )(z, o, x, g_sgu, g_ao, g_so, ws_t, bias, w_out)


def mix_bwd(dx2, z, o, g_sgu, g_ao, g_so, ws, ws_t, bias, w_out, group_ind, tm):
    t = dx2.shape[0]
    n_tiles = t // tm

    def body(dx_ref, z_ref, o_ref, gs_ref, gao_ref, gso_ref, ws_ref, wst_ref, bias_ref, wout_ref, ind_ref,
             do_ref, dotok_ref, dz_ref, dg_ref, dws_ref, dbs_ref, df_sum):
        step = pl.program_id(0)

        @pl.when(step == 0)
        def _():
            dg_ref[...] = jnp.zeros_like(dg_ref)
            dws_ref[...] = jnp.zeros_like(dws_ref)
            df_sum[...] = jnp.zeros_like(df_sum)

        z = z_ref[...]
        th, u, rv, nv, v_chunks, f = _gate_forward(z, gs_ref[...], wst_ref, bias_ref[...])
        dmixed = _dot_nt(dx_ref[...].astype(BF16), wout_ref[...])
        o_tiles = o_ref[...]
        oo = _from_head_tiles(o_tiles)
        ro = _rstd(oo)
        d_o, dgao = _rms_bwd(dmixed[:, :D_ATTN], oo * ro, ro, gao_ref[...])
        do_tiles = _to_head_tiles(d_o)
        do_ref[...] = do_tiles.astype(BF16)
        delta_hi, delta_lo = _hi_lo(jnp.sum(do_tiles * o_tiles, axis=1, keepdims=True))
        for h in range(N_HEADS):
            delta_rows = _shift_rows((HEAD_DIM, tm), delta_hi[h], delta_lo[h])
            dotok_ref[h] = jnp.concatenate([do_tiles[h], delta_rows], axis=0).T.astype(BF16)
        sgu = u * f
        rs = _rstd(sgu)
        dsgu, dgso = _rms_bwd(dmixed[:, D_ATTN:], sgu * rs, rs, gso_ref[...])
        du = dsgu * f
        df = dsgu * u
        dvns = []
        df_acc = jnp.zeros((CHUNK, D_SGU), F32)
        for c in range(tm // CHUNK):
            dfc32 = df[c * CHUNK:(c + 1) * CHUNK]
            dft = dfc32.T.astype(BF16)
            dvns.append(_group_matmul(dft, ws_ref).T)
            for g in range(N_SGU_GROUPS):
                rows = slice(g * SGU_GROUP_DIM, (g + 1) * SGU_GROUP_DIM)
                dws_ref[g] += _dot_tn(dft[rows], v_chunks[c][rows])
            df_acc = df_acc + dfc32
        df_sum[...] += df_acc
        dvn = jnp.concatenate(dvns, axis=0) if len(dvns) > 1 else dvns[0]
        dvv, dgs = _rms_bwd(dvn, nv, rv, gs_ref[...])
        dz_ref[...] = (jnp.concatenate([du, dvv], axis=-1) * _gelu_grad(z, th)).astype(BF16)
        dg_ref[0:1, :] += _colsum(dgao)
        dg_ref[1:2, :] += _colsum(dgso)
        dg_ref[2:3, :] += _colsum(dgs)

        @pl.when(step == n_tiles - 1)
        def _():
            dbs_ref[...] = _dot_f32(df_sum[...], ind_ref[...])

    row = lambda n: pl.BlockSpec((tm, n), lambda i: (i, 0))
    return pl.pallas_call(
        functools.partial(body), name="mix_bwd", grid=(n_tiles,),
        in_specs=[row(D_MODEL), row(2 * D_SGU), _head_tile_spec(tm, HEAD_DIM), _full((1, D_SGU)), _full((1, D_ATTN)),
                  _full((1, D_SGU)),
                  _full((N_SGU_GROUPS, CHUNK, CHUNK)), _full((N_SGU_GROUPS, CHUNK, CHUNK)), _full((CHUNK, D_SGU)),
                  _full((D_MODEL, D_MODEL)), _full((D_SGU, LANES))],
        out_specs=[_head_tile_spec(tm, HEAD_DIM), pl.BlockSpec((N_HEADS, None, tm, 2 * HEAD_DIM), lambda i: (0, i, 0, 0)),
                   row(2 * D_SGU), _full((8, D_SGU)),
                   _full((N_SGU_GROUPS, CHUNK, CHUNK)), _full((CHUNK, LANES))],
        out_shape=[jax.ShapeDtypeStruct((N_HEADS, n_tiles, HEAD_DIM, tm), BF16),
                   jax.ShapeDtypeStruct((N_HEADS, n_tiles, tm, 2 * HEAD_DIM), BF16),
                   jax.ShapeDtypeStruct((t, 2 * D_SGU), BF16),
                   jax.ShapeDtypeStruct((8, D_SGU), F32),
                   jax.ShapeDtypeStruct((N_SGU_GROUPS, CHUNK, CHUNK), F32),
                   jax.ShapeDtypeStruct((CHUNK, LANES), F32)],
        scratch_shapes=[pltpu.VMEM((CHUNK, D_SGU), F32)],
        compiler_params=_params(1),
    )(dx2, z, o, g_sgu, g_ao, g_so, ws, ws_t, bias, w_out, group_ind)


def ffn_down_loss(act, wd, x, g, target, tm):
    t = x.shape[0]

    def body(act_ref, wd_ref, x_ref, g_ref, t_ref, loss_ref, dx_ref, dg_ref):
        @pl.when(pl.program_id(0) == 0)
        def _():
            loss_ref[...] = jnp.zeros_like(loss_ref)
            dg_ref[...] = jnp.zeros_like(dg_ref)

        xx = x_ref[...] + 0.5 * _dot(act_ref[...], wd_ref[...])
        r = _rstd(xx)
        n = xx * r
        err = n * g_ref[...] - t_ref[...]
        per_token = jnp.mean(err * err, axis=-1, keepdims=True)
        loss_ref[...] += 0.5 * jnp.sum(per_token, axis=0, keepdims=True)
        dx, dg_rows = _rms_bwd(err * (1.0 / D_MODEL), n, r, g_ref[...])
        dx_ref[...] = dx
        dg_ref[...] += _colsum(dg_rows)

    row = pl.BlockSpec((tm, D_MODEL), lambda i: (i, 0))
    return pl.pallas_call(
        functools.partial(body), name="ffn_down_loss", grid=(t // tm,),
        in_specs=[pl.BlockSpec((tm, D_FF), lambda i: (i, 0)), _full((D_FF, D_MODEL)), row, _full((1, D_MODEL)), row],
        out_specs=[_full((1, LANES)), row, _full((1, D_MODEL))],
        out_shape=[jax.ShapeDtypeStruct((1, LANES), F32), jax.ShapeDtypeStruct((t, D_MODEL), F32),
                   jax.ShapeDtypeStruct((1, D_MODEL), F32)],
        compiler_params=_params(1),
    )(act, wd, x, g, target)


def _rope_tables(t):
    rows = t // GRID_W
    row_idx = jnp.repeat(jnp.arange(rows, dtype=F32), GRID_W)
    col_idx = jnp.tile(jnp.arange(GRID_W, dtype=F32), rows)
    axis_dim = HEAD_DIM // 2
    inv = 1.0 / (ROPE_THETA ** (jnp.arange(0, axis_dim, 2, dtype=F32) / axis_dim))
    ang = jnp.concatenate([row_idx[:, None] * inv, col_idx[:, None] * inv], axis=-1)
    cos = jnp.repeat(jnp.cos(ang), 2, axis=-1)
    sin = jnp.repeat(jnp.sin(ang), 2, axis=-1) * jnp.tile(jnp.array([-1.0, 1.0], F32), HEAD_DIM // 2)
    return jnp.tile(cos, (1, LANES // HEAD_DIM)), jnp.tile(sin, (1, LANES // HEAD_DIM))


def kernel(x, g_ffn1, w1_gate, w1_up, w1_down, g_mix, w_in, g_q, g_k, g_sgu, w_s, b_s, g_attn_out, g_sgu_out, w_out, g_ffn2, w2_gate, w2_up, w2_down, g_final, loss_target, m_g_ffn1, m_w1_gate, m_w1_up, m_w1_down, m_g_mix, m_w_in, m_g_q, m_g_k, m_g_sgu, m_w_s, m_b_s, m_g_attn_out, m_g_sgu_out, m_w_out, m_g_ffn2, m_w2_gate, m_w2_up, m_w2_down, m_g_final, v_g_ffn1, v_w1_gate, v_w1_up, v_w1_down, v_g_mix, v_w_in, v_g_q, v_g_k, v_g_sgu, v_w_s, v_b_s, v_g_attn_out, v_g_sgu_out, v_w_out, v_g_ffn2, v_w2_gate, v_w2_up, v_w2_down, v_g_final):
    weights = dict(g_ffn1=g_ffn1, w1_gate=w1_gate, w1_up=w1_up, w1_down=w1_down, g_mix=g_mix, w_in=w_in, g_q=g_q,
                   g_k=g_k, g_sgu=g_sgu, w_s=w_s, b_s=b_s, g_attn_out=g_attn_out, g_sgu_out=g_sgu_out, w_out=w_out,
                   g_ffn2=g_ffn2, w2_gate=w2_gate, w2_up=w2_up, w2_down=w2_down, g_final=g_final)
    m_in = dict(g_ffn1=m_g_ffn1, w1_gate=m_w1_gate, w1_up=m_w1_up, w1_down=m_w1_down, g_mix=m_g_mix, w_in=m_w_in,
                g_q=m_g_q, g_k=m_g_k, g_sgu=m_g_sgu, w_s=m_w_s, b_s=m_b_s, g_attn_out=m_g_attn_out,
                g_sgu_out=m_g_sgu_out, w_out=m_w_out, g_ffn2=m_g_ffn2, w2_gate=m_w2_gate, w2_up=m_w2_up,
                w2_down=m_w2_down, g_final=m_g_final)
    v_in = dict(g_ffn1=v_g_ffn1, w1_gate=v_w1_gate, w1_up=v_w1_up, w1_down=v_w1_down, g_mix=v_g_mix, w_in=v_w_in,
                g_q=v_g_q, g_k=v_g_k, g_sgu=v_g_sgu, w_s=v_w_s, b_s=v_b_s, g_attn_out=v_g_attn_out,
                g_sgu_out=v_g_sgu_out, w_out=v_w_out, g_ffn2=v_g_ffn2, w2_gate=v_w2_gate, w2_up=v_w2_up,
                w2_down=v_w2_down, g_final=v_g_final)
    names = list(weights)

    t = x.shape[1]
    x0 = x[0]
    target = loss_target[0]
    tm = min(512, t)
    tm_ff = min(256, t)
    tn_ff = 256
    tq = min(512, t)
    tk = min(256, t)
    tk_fwd = min(512, t)
    tk_w = min(2048, t)

    def shard_rows(name):
        w = weights[name][0]
        return (w.T if name in TRANSPOSED else w).astype(BF16)

    rows_of = dict(SHARD_ROWS)
    full = {}

    def packed(group):
        return jnp.concatenate([shard_rows(n) for n in group], axis=0), [rows_of[n] for n in group]

    def gather_of(group):
        return gather_exchange(*packed(group))

    def take(group, gathered):
        for n, g in zip(group, gathered):
            full[n] = g.reshape(N_DEV * rows_of[n], D_MODEL)

    first, second, third = ("w1_gate", "w1_up"), ("w1_down", "w_in", "w_out"), ("w2_gate", "w2_up", "w2_down")
    take(first, gather_two_level(*packed(first), "gather_first"))

    (h1, a1, b1, act1), gathered = ffn_up(x0, g_ffn1, full["w1_gate"], full["w1_up"], tm_ff, tn_ff, gather_of(second))
    take(second, gathered)
    w_in_t = full["w_in"]
    w_qkv_t, w_z_t = w_in_t[:D_QKV], w_in_t[D_QKV:]
    x1 = ffn_down(act1, full["w1_down"], x0, tm)

    qkv, z, h2 = input_projection(x1, g_mix, w_qkv_t, w_z_t, tm)
    cos_w, sin_w = _rope_tables(t)
    gq_w = jnp.tile(g_q, (1, N_HEADS))
    gk_w = jnp.tile(g_k, (1, N_KV_HEADS))
    mean_q, mean_k = _head_mean_matrix(D_ATTN).astype(BF16), _head_mean_matrix(D_KV).astype(BF16)
    q_t, k_tiles, kt_tiles, vt_tiles, vt_tiles_bwd, q_sq_max, k_sq_max = qk_prep(
        qkv, gq_w, gk_w, cos_w, sin_w, mean_q, mean_k, tq, tk, tk_fwd)
    score_bound = 1.02 * jnp.sqrt(q_sq_max.reshape(N_HEADS, HEAD_DIM)[:, 0]
                                  * jnp.repeat(k_sq_max.reshape(N_KV_HEADS, HEAD_DIM)[:, 0], KV_GROUP))
    k_tiles_fwd = k_tiles.reshape(N_KV_HEADS, t // tk_fwd, tk_fwd, HEAD_DIM)
    (o_t, q_tok), gathered = attention_fwd(score_bound, q_t, k_tiles_fwd, vt_tiles, gather_of(third))
    take(third, gathered)

    ws_b = w_s[0].astype(BF16)
    ws_tb = jnp.swapaxes(w_s[0], 1, 2).astype(BF16)
    bias = jnp.repeat(b_s[0].T, SGU_GROUP_DIM, axis=1)
    x2, mixed = mix_out(z, o_t, x1, g_sgu, g_attn_out, g_sgu_out, ws_tb, bias, full["w_out"], tq)

    (h3, a2, b2, act2), _ = ffn_up(x2, g_ffn2, full["w2_gate"], full["w2_up"], tm_ff, tn_ff)

    loss_part, dx3, dg_final = ffn_down_loss(act2, full["w2_down"], x2, g_final, target, tm)

    tmm = D_FF // 2
    (da2, db2), _ = ffn_bwd_act(dx3, full["w2_down"], a2, b2, tm_ff, tn_ff)
    (dx2, dg_ffn2), _ = norm_bwd_matmul(da2, full["w2_gate"], db2, full["w2_up"], x2, g_ffn2, dx3, tm)
    dwg2, dwu2 = matmul_tn(da2, h3, 1.0, tmm, tk_w), matmul_tn(db2, h3, 1.0, tmm, tk_w)
    dwd2 = matmul_tn(act2, dx3, 0.5, tmm, tk_w)

    group_ind = (jnp.arange(D_SGU)[:, None] // SGU_GROUP_DIM == jnp.arange(LANES)[None, :]).astype(F32)
    do_t, do_tok, dz, dg_mixrow, dws, dbs = mix_bwd(dx2, z, o_t, g_sgu, g_attn_out, g_sgu_out, ws_b, ws_tb, bias,
                                                    full["w_out"], group_ind, tq)
    dw_out = matmul_tn(mixed, dx2, 1.0, D_MODEL // 2, tk_w)

    group_a = ("w2_gate", "w2_up", "w2_down", "w_out")
    (dq_t, dk_t, dv_t), (parts_a,) = attention_bwd(q_tok, do_tok, q_t, do_t, kt_tiles, vt_tiles_bwd,
                                                   scatter_exchange([dwg2, dwu2, dwd2, dw_out]))
    dqkv, dgq_w, dgk_w = qk_bwd(dq_t, dk_t, dv_t, qkv, gq_w, gk_w, cos_w, sin_w, mean_q, mean_k, tq)

    def pack_small(arrays):
        pieces = []
        for a in arrays:
            flat = a.reshape(-1)
            pieces.append(jnp.pad(flat, (0, (-flat.shape[0]) % (8 * LANES))).reshape(-1, LANES))
        return jnp.concatenate(pieces, axis=0), [p.shape[0] for p in pieces]

    early = dict(g_ffn2=dg_ffn2, g_final=dg_final, g_q=dgq_w.reshape(N_HEADS, HEAD_DIM).sum(0),
                 g_k=dgk_w.reshape(N_KV_HEADS, HEAD_DIM).sum(0), g_attn_out=dg_mixrow[0], g_sgu_out=dg_mixrow[1],
                 g_sgu=dg_mixrow[2], w_s=dws, b_s=dbs[:, :N_SGU_GROUPS].T)
    early_pack, early_rows = pack_small(list(early.values()))
    (dx1, dg_mix), (early_parts,) = norm_bwd_matmul(dqkv, w_qkv_t, dz, w_z_t, x1, g_mix, dx2, tm,
                                                    gather_exchange(early_pack, [early_pack.shape[0]]))
    dw_in = jnp.concatenate([matmul_tn(dqkv, h2, 1.0, D_QKV // 2, tk_w), matmul_tn(dz, h2, 1.0, D_SGU, tk_w)], axis=0)

    dwd1, (parts_in,) = matmul_tn(act1, dx1, 0.5, tmm, tk_w, scatter_exchange([dw_in]))
    (da1, db1), (parts_d1,) = ffn_bwd_act(dx1, full["w1_down"], a1, b1, tm_ff, tn_ff, scatter_exchange([dwd1]))
    dwg1 = matmul_tn(da1, h1, 1.0, tmm, tk_w)
    dwu1, (parts_g1,) = matmul_tn(db1, h1, 1.0, tmm, tk_w, scatter_exchange([dwg1]))
    (dx0, dg_ffn1), (parts_u1,) = norm_bwd_matmul(da1, full["w1_gate"], db1, full["w1_up"], x0, g_ffn1, dx1, tm,
                                                  scatter_exchange([dwu1]))
    scattered = ((group_a, parts_a), (("w_in",), parts_in), (("w1_down",), parts_d1), (("w1_gate",), parts_g1),
                 (("w1_up",), parts_u1))

    late = dict(g_mix=dg_mix, g_ffn1=dg_ffn1, loss=loss_part)
    late_pack, late_rows = pack_small(list(late.values()))
    (late_parts,) = run_exchange(gather_exchange(late_pack, [late_pack.shape[0]]), "gather_late_small_grads")
    small_sums = {}
    for entries, rows, parts in ((early, early_rows, early_parts), (late, late_rows, late_parts)):
        summed = sum_parts(parts, parts.shape[1])
        off = 0
        for n, r in zip(entries, rows):
            small_sums[n] = summed[off:off + r]
            off += r
    loss = small_sums.pop("loss")[0, 0]

    grads, row_grads = {}, {}
    for group, parts in scattered:
        rows = parts.shape[1]
        summed = sum_parts(parts, rows if rows <= 2 * rows_of["w1_gate"] else rows // 2)
        off = 0
        for n in group:
            row_grads[n] = summed[off:off + rows_of[n]]
            grads[n] = (row_grads[n].T if n in TRANSPOSED else row_grads[n])[None]
            off += rows_of[n]
    for n, summed in small_sums.items():
        grads[n] = summed.reshape(-1)[:weights[n].size].reshape(weights[n].shape)

    delta_w, new_m, new_v = {}, {}, {}
    for n in names:
        shape = weights[n].shape
        if n in TRANSPOSED:
            view, unview, g = (lambda a: a[0].T), (lambda a: a.T[None]), row_grads[n]
        else:
            view, unview = (lambda a: a.reshape(-1, shape[-1])), (lambda a: a.reshape(shape))
            g = view(grads[n])
        d, m2, v2 = adamw(view(weights[n]), g, view(m_in[n]), view(v_in[n]))
        delta_w[n], new_m[n], new_v[n] = unview(d), unview(m2), unview(v2)

    return (loss, dx0[None], *[grads[n] for n in names], *[delta_w[n] for n in names],
            *[new_m[n] for n in names], *[new_v[n] for n in names])
```

```python
import functools
import math

import jax
import jax.numpy as jnp
from jax import lax
from jax.experimental import pallas as pl
from jax.experimental.pallas import tpu as pltpu

F32 = jnp.float32
BF16 = jnp.bfloat16

D_MODEL = 1024
D_FF = 2816
N_HEADS = 8
HEAD_DIM = 64
N_KV_HEADS = 2
KV_GROUP = N_HEADS // N_KV_HEADS
D_ATTN = N_HEADS * HEAD_DIM
D_KV = N_KV_HEADS * HEAD_DIM
D_QKV = D_ATTN + 2 * D_KV
N_SGU_GROUPS = 8
SGU_GROUP_DIM = 64
D_SGU = N_SGU_GROUPS * SGU_GROUP_DIM
CHUNK = 128
GRID_W = 64
ROPE_THETA = 10000.0
EPS = 1e-6
N_DEV = 8
LANES = 128

ONES_ROWS = 16
SAFE_SCORE_BOUND = 60.0
LOG2_E = math.log2(math.e)
Q_SCALE = HEAD_DIM ** -0.5 * LOG2_E

ADAM_LR = 0.001
ADAM_B1 = 0.9
ADAM_B2 = 0.999
ADAM_EPS = 1e-08
ADAM_WD = 0.01
ADAM_STEP = 10

MESH_IDS = pl.DeviceIdType.MESH

VMEM_LIMIT = 56 * 1024 * 1024

SHARD_ROWS = (("w1_gate", D_FF // N_DEV), ("w1_up", D_FF // N_DEV), ("w1_down", D_FF // N_DEV),
              ("w_in", (D_QKV + 2 * D_SGU) // N_DEV), ("w_out", D_MODEL // N_DEV),
              ("w2_gate", D_FF // N_DEV), ("w2_up", D_FF // N_DEV), ("w2_down", D_FF // N_DEV))
TRANSPOSED = ("w1_gate", "w1_up", "w_in", "w2_gate", "w2_up")


def _params(n_grid):
    return pltpu.CompilerParams(dimension_semantics=("arbitrary",) * n_grid, vmem_limit_bytes=VMEM_LIMIT)


def _dot(a, b):
    return jnp.dot(a, b, preferred_element_type=F32)


def _dot_nt(a, b):
    return lax.dot_general(a, b, (((1,), (1,)), ((), ())), preferred_element_type=F32)


def _dot_tn(a, b):
    return lax.dot_general(a, b, (((0,), (0,)), ((), ())), preferred_element_type=F32)


def _dot_f32(a, b):
    return jnp.dot(a, b, preferred_element_type=F32, precision=lax.Precision.HIGHEST)


def _dot_split(a, b):
    hi = a.astype(BF16)
    lo = (a - hi.astype(F32)).astype(BF16)
    return _dot(hi, b) + _dot(lo, b)


def _rstd(x):
    return lax.rsqrt(jnp.mean(x * x, axis=-1, keepdims=True) + EPS)


def _rms_bwd(dy, n, r, g):
    dn = dy * g
    return r * (dn - n * jnp.mean(dn * n, axis=-1, keepdims=True)), dy * n


def _colsum(a):
    return jnp.sum(a, axis=0, keepdims=True)


_GELU_C = math.sqrt(2.0 / math.pi)


def _gelu(x):
    t = jnp.tanh(_GELU_C * (x + 0.044715 * (x * x * x)))
    return x * (0.5 * (1.0 + t)), t


def _gelu_grad(x, t):
    return 0.5 * (1.0 + t) + 0.5 * x * (1.0 - t * t) * (_GELU_C * (1.0 + 3 * 0.044715 * x * x))


def _pair_swap(a):
    w = a.shape[-1]
    lane = lax.broadcasted_iota(jnp.int32, a.shape, a.ndim - 1)
    return jnp.where(lane % 2 == 0, pltpu.roll(a, w - 1, a.ndim - 1), pltpu.roll(a, 1, a.ndim - 1))


def _tile_lanes(a, reps):
    return jnp.concatenate([a] * reps, axis=-1) if reps > 1 else a


def _loop_pairs(n, step, carry, per_body=2):
    assert n % per_body == 0 and per_body % 2 == 0, (n, per_body)

    def body(jj, c):
        for u in range(per_body):
            c = step(per_body * jj + u, u % 2, c)
        return c

    return lax.fori_loop(0, n // per_body, body, carry)


def _full(shape):
    nd = len(shape)
    return pl.BlockSpec(shape, lambda *_: (0,) * nd)


def _mesh_pos():
    return lax.axis_index("x"), lax.axis_index("y"), lax.axis_index("c")


def _peer(pos, d):
    x, y, c = pos
    px = 1 - x if d & 4 else x
    py = 1 - y if d & 2 else y
    pc = 1 - c if d & 1 else c
    return (px, py, pc), 4 * px + 2 * py + pc


class _Exchange:
    def __init__(self, operands, out_shape, n_local, plan):
        self.operands = list(operands)
        self.out_shape = list(out_shape)
        self.sem_shapes = [pltpu.SemaphoreType.DMA((N_DEV - 1,)), pltpu.SemaphoreType.DMA((N_DEV - 1,)),
                           pltpu.SemaphoreType.DMA((n_local,))]
        self._plan = plan

    def _copies(self, in_refs, out_refs):
        pos = _mesh_pos()
        return pos, self._plan(4 * pos[0] + 2 * pos[1] + pos[2], in_refs, out_refs)

    def start(self, in_refs, out_refs, sems):
        send_sems, recv_sems, local_sems = sems
        pos, (local, remote, _) = self._copies(in_refs, out_refs)
        for k, (src, dst) in enumerate(local):
            pltpu.make_async_copy(src, dst, local_sems.at[k]).start()
        for d in range(1, N_DEV):
            peer, peer_lin = _peer(pos, d)
            for src, dst in remote(peer_lin):
                pltpu.make_async_remote_copy(src_ref=src, dst_ref=dst, send_sem=send_sems.at[d - 1],
                                             recv_sem=recv_sems.at[d - 1], device_id=peer,
                                             device_id_type=MESH_IDS).start()

    def wait(self, in_refs, out_refs, sems):
        send_sems, recv_sems, local_sems = sems
        pos, (local, _, whole) = self._copies(in_refs, out_refs)
        for d in range(1, N_DEV):
            peer, peer_lin = _peer(pos, d)
            ref = whole(peer_lin)
            everything = pltpu.make_async_remote_copy(src_ref=ref, dst_ref=ref, send_sem=send_sems.at[d - 1],
                                                      recv_sem=recv_sems.at[d - 1], device_id=peer,
                                                      device_id_type=MESH_IDS)
            everything.wait_send()
            everything.wait_recv()
        for k, (src, dst) in enumerate(local):
            pltpu.make_async_copy(src, dst, local_sems.at[k]).wait()


def _offsets(rows):
    offs, o = [], 0
    for r in rows:
        offs.append(o)
        o += r
    return offs


def gather_exchange(src, rows):
    offs = _offsets(rows)

    def plan(me, in_refs, out_refs):
        pieces = [(in_refs[0].at[pl.ds(o, r)], out.at[me]) for o, r, out in zip(offs, rows, out_refs)]
        return pieces, (lambda peer_lin: pieces), (lambda peer_lin: in_refs[0])

    return _Exchange([src], [jax.ShapeDtypeStruct((N_DEV, r) + src.shape[1:], src.dtype) for r in rows],
                     len(rows), plan)


def scatter_exchange(grads):
    rows = [g.shape[0] // N_DEV for g in grads]
    offs = _offsets(rows)

    def plan(me, in_refs, out_refs):
        parts = out_refs[0]

        def slabs(owner):
            return [(g.at[pl.ds(pl.multiple_of(owner * r, 16), r)], parts.at[me, pl.ds(o, r)])
                    for g, o, r in zip(in_refs, offs, rows)]

        return slabs(me), slabs, (lambda peer_lin: parts.at[peer_lin])

    shape = jax.ShapeDtypeStruct((N_DEV, sum(rows)) + grads[0].shape[1:], grads[0].dtype)
    return _Exchange(grads, [shape], len(rows), plan)


def gather_two_level(src, rows, name):
    offs = _offsets(rows)
    n_p = len(rows)

    def body(src_ref, *refs):
        outs, (send_sems, recv_sems, local_sems) = refs[:n_p], refs[n_p:]
        x, y, c = _mesh_pos()
        me, sibling = (x, y, c), (x, y, 1 - c)
        chips = [(1 - x, y), (x, 1 - y), (1 - x, 1 - y)]

        def slab(w, dev):
            return outs[w].at[4 * dev[0] + 2 * dev[1] + dev[2]]

        def copy(w, k, block, to, from_src=False):
            return pltpu.make_async_remote_copy(
                src_ref=src_ref.at[pl.ds(offs[w], rows[w])] if from_src else slab(w, block), dst_ref=slab(w, block),
                send_sem=send_sems.at[w * 7 + k], recv_sem=recv_sems.at[w * 7 + k],
                device_id=to, device_id_type=MESH_IDS)

        mine = [pltpu.make_async_copy(src_ref.at[pl.ds(offs[w], rows[w])], slab(w, me), local_sems.at[w])
                for w in range(n_p)]
        for cp in mine:
            cp.start()
        first = []
        for w in range(n_p):
            first.append(copy(w, 0, me, sibling, True))
            first += [copy(w, 1 + j, me, (*chip, c), True) for j, chip in enumerate(chips)]
        for cp in first:
            cp.start()
        passed = []
        for j, chip in enumerate(chips):
            for w in range(n_p):
                copy(w, 1 + j, (*chip, c), me).wait_recv()
                cp = copy(w, 4 + j, (*chip, c), sibling)
                cp.start()
                passed.append(cp)
        for w in range(n_p):
            copy(w, 0, sibling, me).wait_recv()
            for j, chip in enumerate(chips):
                copy(w, 4 + j, (*chip, 1 - c), me).wait_recv()
        for cp in first + passed:
            cp.wait_send()
        for cp in mine:
            cp.wait()

    any_spec = pl.BlockSpec(memory_space=pl.ANY)
    return pl.pallas_call(
        functools.partial(body), name=name,
        out_shape=[jax.ShapeDtypeStruct((N_DEV, r) + src.shape[1:], src.dtype) for r in rows],
        in_specs=[any_spec], out_specs=[any_spec] * n_p,
        scratch_shapes=[pltpu.SemaphoreType.DMA((7 * n_p,)), pltpu.SemaphoreType.DMA((7 * n_p,)),
                        pltpu.SemaphoreType.DMA((n_p,))],
        compiler_params=pltpu.CompilerParams(has_side_effects=True),
    )(src)


def run_exchange(ex, name):
    n_in, n_out = len(ex.operands), len(ex.out_shape)

    def body(*refs):
        parts = refs[:n_in], refs[n_in:n_in + n_out], refs[n_in + n_out:]
        ex.start(*parts)
        ex.wait(*parts)

    any_spec = pl.BlockSpec(memory_space=pl.ANY)
    return pl.pallas_call(
        functools.partial(body), name=name, out_shape=ex.out_shape,
        in_specs=[any_spec] * n_in, out_specs=[any_spec] * n_out, scratch_shapes=ex.sem_shapes,
        compiler_params=pltpu.CompilerParams(has_side_effects=True),
    )(*ex.operands)


def _pallas(comm, body, *, name, grid, in_specs, out_specs, out_shape, args, scratch_shapes=()):
    params = _params(len(grid))
    if comm is None:
        res = pl.pallas_call(functools.partial(body), name=name, grid=grid, in_specs=list(in_specs),
                             out_specs=list(out_specs), out_shape=list(out_shape),
                             scratch_shapes=list(scratch_shapes), compiler_params=params)(*args)
        return list(res), []
    n_in, n_out, n_scr = len(in_specs), len(out_specs), len(scratch_shapes)
    c_in, c_out = len(comm.operands), len(comm.out_shape)

    def edge(last):
        conds = [pl.program_id(a) == (g - 1 if last else 0) for a, g in enumerate(grid)]
        return functools.reduce(jnp.logical_and, conds)

    def wrapped(*refs):
        refs = list(refs)
        ins, refs = refs[:n_in], refs[n_in:]
        cins, refs = refs[:c_in], refs[c_in:]
        outs, refs = refs[:n_out], refs[n_out:]
        couts, refs = refs[:c_out], refs[c_out:]
        scr, sems = refs[:n_scr], refs[n_scr:]

        @pl.when(edge(False))
        def _():
            comm.start(cins, couts, sems)

        body(*ins, *outs, *scr)

        @pl.when(edge(True))
        def _():
            comm.wait(cins, couts, sems)

    any_spec = pl.BlockSpec(memory_space=pl.ANY)
    res = pl.pallas_call(
        wrapped, name=name, grid=grid,
        in_specs=list(in_specs) + [any_spec] * c_in, out_specs=list(out_specs) + [any_spec] * c_out,
        out_shape=list(out_shape) + comm.out_shape, scratch_shapes=list(scratch_shapes) + comm.sem_shapes,
        compiler_params=pltpu.CompilerParams(dimension_semantics=("arbitrary",) * len(grid),
                                             vmem_limit_bytes=VMEM_LIMIT, has_side_effects=True),
    )(*args, *comm.operands)
    return res[:n_out], res[n_out:]


def sum_parts(parts, block_rows):
    n, rows, cols = parts.shape

    def body(p_ref, o_ref):
        acc = p_ref[0].astype(F32)
        for s in range(1, n):
            acc = acc + p_ref[s].astype(F32)
        o_ref[...] = acc

    return pl.pallas_call(
        functools.partial(body), name="sum_parts",
        grid=(rows // block_rows,),
        in_specs=[pl.BlockSpec((n, block_rows, cols), lambda i: (0, i, 0))],
        out_specs=pl.BlockSpec((block_rows, cols), lambda i: (i, 0)),
        out_shape=jax.ShapeDtypeStruct((rows, cols), F32),
        compiler_params=_params(1),
    )(parts)


def adamw(w, g, m, v):
    def body(w_ref, g_ref, m_ref, v_ref, d_ref, m_out, v_out):
        gg = g_ref[...]
        m2 = ADAM_B1 * m_ref[...] + (1.0 - ADAM_B1) * gg
        v2 = ADAM_B2 * v_ref[...] + (1.0 - ADAM_B2) * (gg * gg)
        m_hat = m2 / (1.0 - ADAM_B1 ** ADAM_STEP)
        v_hat = v2 / (1.0 - ADAM_B2 ** ADAM_STEP)
        d_ref[...] = -ADAM_LR * (m_hat / (jnp.sqrt(v_hat) + ADAM_EPS) + ADAM_WD * w_ref[...])
        m_out[...] = m2
        v_out[...] = v2

    spec = _full(w.shape)
    shape = jax.ShapeDtypeStruct(w.shape, F32)
    return pl.pallas_call(
        functools.partial(body), name="adamw",
        in_specs=[spec] * 4, out_specs=[spec] * 3, out_shape=[shape] * 3,
        compiler_params=pltpu.CompilerParams(vmem_limit_bytes=VMEM_LIMIT),
    )(w, g, m, v)


def ffn_up(x, g, wg_t, wu_t, tm, tn, comm=None):
    t = x.shape[0]

    def body(x_ref, g_ref, wg_ref, wu_ref, h_ref, silu_ref, dgate_ref, act_ref):
        xx = x_ref[...]
        h = ((xx * _rstd(xx)) * g_ref[...]).astype(BF16)
        h_ref[...] = h
        for c in range(D_FF // tn):
            cols = slice(c * tn, (c + 1) * tn)
            a = _dot_nt(h, wg_ref[cols, :])
            b = _dot_nt(h, wu_ref[cols, :])
            sig = 0.5 * jnp.tanh(0.5 * a) + 0.5
            silu = a * sig
            silu_ref[:, cols] = silu.astype(BF16)
            dgate_ref[:, cols] = (b * (sig + silu * (1.0 - sig))).astype(BF16)
            act_ref[:, cols] = (silu * b).astype(BF16)

    wide = jax.ShapeDtypeStruct((t, D_FF), BF16)
    row = lambda n: pl.BlockSpec((tm, n), lambda i: (i, 0))
    return _pallas(
        comm, body, name="ffn_up",
        grid=(t // tm,),
        in_specs=[row(D_MODEL), _full((1, D_MODEL)), _full((D_FF, D_MODEL)), _full((D_FF, D_MODEL))],
        out_specs=[row(D_MODEL), row(D_FF), row(D_FF), row(D_FF)],
        out_shape=[jax.ShapeDtypeStruct((t, D_MODEL), BF16), wide, wide, wide],
        args=(x, g, wg_t, wu_t))


def ffn_down(act, wd, x, tm):
    t = x.shape[0]

    def body(act_ref, wd_ref, x_ref, o_ref):
        o_ref[...] = x_ref[...] + 0.5 * _dot(act_ref[...], wd_ref[...])

    return pl.pallas_call(
        functools.partial(body), name="ffn_down",
        grid=(t // tm,),
        in_specs=[pl.BlockSpec((tm, D_FF), lambda i: (i, 0)), _full((D_FF, D_MODEL)),
                  pl.BlockSpec((tm, D_MODEL), lambda i: (i, 0))],
        out_specs=pl.BlockSpec((tm, D_MODEL), lambda i: (i, 0)),
        out_shape=jax.ShapeDtypeStruct((t, D_MODEL), F32),
        compiler_params=_params(1),
    )(act, wd, x)


def ffn_bwd_act(dx, wd, silu, dgate, tm, tn, comm=None):
    t = dx.shape[0]

    def body(dx_ref, wd_ref, silu_ref, dgate_ref, da_ref, db_ref):
        dxb = (0.5 * dx_ref[...]).astype(BF16)
        for c in range(D_FF // tn):
            cols = slice(c * tn, (c + 1) * tn)
            dact = _dot_nt(dxb, wd_ref[cols, :])
            da_ref[:, cols] = (dact * dgate_ref[:, cols].astype(F32)).astype(BF16)
            db_ref[:, cols] = (dact * silu_ref[:, cols].astype(F32)).astype(BF16)

    wide = jax.ShapeDtypeStruct((t, D_FF), BF16)
    row = lambda n: pl.BlockSpec((tm, n), lambda i: (i, 0))
    return _pallas(
        comm, body, name="ffn_bwd_act",
        grid=(t // tm,),
        in_specs=[row(D_MODEL), _full((D_FF, D_MODEL)), row(D_FF), row(D_FF)],
        out_specs=[row(D_FF), row(D_FF)],
        out_shape=[wide, wide],
        args=(dx, wd, silu, dgate))


def norm_bwd_matmul(a1, w1, a2, w2, x, g, dx_in, tm, comm=None):
    t = x.shape[0]
    k1, k2 = a1.shape[1], a2.shape[1]

    def body(a1_ref, w1_ref, a2_ref, w2_ref, x_ref, g_ref, dxin_ref, dx_ref, dg_ref):
        dh = _dot(a1_ref[...], w1_ref[...]) + _dot(a2_ref[...], w2_ref[...])
        xx = x_ref[...]
        r = _rstd(xx)
        dx, dg_rows = _rms_bwd(dh, xx * r, r, g_ref[...])
        dx_ref[...] = dxin_ref[...] + dx

        @pl.when(pl.program_id(0) == 0)
        def _():
            dg_ref[...] = jnp.zeros_like(dg_ref)

        dg_ref[...] += _colsum(dg_rows)

    row = pl.BlockSpec((tm, D_MODEL), lambda i: (i, 0))
    return _pallas(
        comm, body, name="norm_bwd_matmul",
        grid=(t // tm,),
        in_specs=[pl.BlockSpec((tm, k1), lambda i: (i, 0)), _full((k1, D_MODEL)),
                  pl.BlockSpec((tm, k2), lambda i: (i, 0)), _full((k2, D_MODEL)),
                  row, _full((1, D_MODEL)), row],
        out_specs=[row, _full((1, D_MODEL))],
        out_shape=[jax.ShapeDtypeStruct((t, D_MODEL), F32), jax.ShapeDtypeStruct((1, D_MODEL), F32)],
        args=(a1, w1, a2, w2, x, g, dx_in))


def matmul_tn(a, b, scale, tmm, tk, comm=None):
    t, m = a.shape
    n = b.shape[1]
    nk = t // tk

    def body(a_ref, b_ref, o_ref, acc_ref):
        k = pl.program_id(1)

        @pl.when(k == 0)
        def _():
            acc_ref[...] = jnp.zeros_like(acc_ref)

        acc_ref[...] += _dot_tn(a_ref[...].astype(BF16), b_ref[...].astype(BF16))

        @pl.when(k == nk - 1)
        def _():
            o_ref[...] = (scale * acc_ref[...]).astype(BF16)

    (out,), comm_outs = _pallas(
        comm, body, name="matmul_tn",
        grid=(m // tmm, nk),
        in_specs=[pl.BlockSpec((tk, tmm), lambda i, k: (k, i)), pl.BlockSpec((tk, n), lambda i, k: (k, 0))],
        out_specs=[pl.BlockSpec((tmm, n), lambda i, k: (i, 0))],
        out_shape=[jax.ShapeDtypeStruct((m, n), BF16)],
        scratch_shapes=[pltpu.VMEM((tmm, n), F32)],
        args=(a, b))
    return out if comm is None else (out, comm_outs)


def input_projection(x, g, w_qkv_t, w_z_t, tm):
    t = x.shape[0]

    def body(x_ref, g_ref, wq_ref, wz_ref, qkv_ref, z_ref, h_ref):
        xx = x_ref[...]
        h = ((xx * _rstd(xx)) * g_ref[...]).astype(BF16)
        h_ref[...] = h
        qkv_ref[...] = _dot_nt(h, wq_ref[...])
        z_ref[...] = _dot_nt(h, wz_ref[...])

    row = lambda n: pl.BlockSpec((tm, n), lambda i: (i, 0))
    return pl.pallas_call(
        functools.partial(body), name="input_projection", grid=(t // tm,),
        in_specs=[row(D_MODEL), _full((1, D_MODEL)), _full((D_QKV, D_MODEL)), _full((2 * D_SGU, D_MODEL))],
        out_specs=[row(D_QKV), row(2 * D_SGU), row(D_MODEL)],
        out_shape=[jax.ShapeDtypeStruct((t, D_QKV), F32), jax.ShapeDtypeStruct((t, 2 * D_SGU), F32),
                   jax.ShapeDtypeStruct((t, D_MODEL), BF16)],
        compiler_params=_params(1))(x, g, w_qkv_t, w_z_t)


def _shift_rows(shape, first, second):
    row = lax.broadcasted_iota(jnp.int32, shape, len(shape) - 2)
    return jnp.where(row == 0, first, jnp.where(row == 1, second, 0.0))


def _hi_lo(a):
    hi = a.astype(BF16).astype(F32)
    return hi, a - hi


def _head_tile_spec(tm, rows):
    return pl.BlockSpec((N_HEADS, None, rows, tm), lambda i: (0, i, 0, 0))


def _to_head_tiles(a):
    return a.T.reshape(N_HEADS, HEAD_DIM, a.shape[0])


def _from_head_tiles(a):
    return a.reshape(D_ATTN, a.shape[-1]).T


def _head_mean_matrix(width):
    head = jnp.arange(width) // HEAD_DIM
    return (head[:, None] == head[None, :]).astype(F32) / HEAD_DIM


def _kv_tile_spec(n_sub, rows, cols):
    return pl.BlockSpec((N_KV_HEADS, n_sub, rows, cols), lambda i: (0, i, 0, 0))


def qk_prep(qkv, gq_w, gk_w, cos_w, sin_w, mean_q, mean_k, tm, tk, tk_v):
    t = qkv.shape[0]
    n_sub, n_sub_v = tm // tk, tm // tk_v

    def body(p_ref, gq_ref, gk_ref, cos_ref, sin_ref, mq_ref, mk_ref, q_ref, k_ref, kt_ref, vt_ref, vtb_ref,
             qmax_ref, kmax_ref):
        @pl.when(pl.program_id(0) == 0)
        def _():
            qmax_ref[...] = jnp.zeros_like(qmax_ref)
            kmax_ref[...] = jnp.zeros_like(kmax_ref)

        cos2, sin2 = cos_ref[...], sin_ref[...]
        q = p_ref[:, :D_ATTN]
        k = p_ref[:, D_ATTN:D_ATTN + D_KV]
        qn = q * lax.rsqrt(_dot_split(q * q, mq_ref[...]) + EPS) * gq_ref[...]
        kn = k * lax.rsqrt(_dot_split(k * k, mk_ref[...]) + EPS) * gk_ref[...]
        cos8, sin8 = _tile_lanes(cos2, D_ATTN // LANES), _tile_lanes(sin2, D_ATTN // LANES)
        q_rot = (qn * cos8 + _pair_swap(qn) * sin8) * Q_SCALE
        q_ref[...] = _to_head_tiles(q_rot).astype(BF16)
        k_rot = kn * cos2 + _pair_swap(kn) * sin2
        q_sq = HEAD_DIM * _dot_split(q_rot * q_rot, mq_ref[...])
        k_sq = HEAD_DIM * _dot_split(k_rot * k_rot, mk_ref[...])
        qmax_ref[...] = jnp.maximum(qmax_ref[...], jnp.max(q_sq, axis=0, keepdims=True))
        kmax_ref[...] = jnp.maximum(kmax_ref[...], jnp.max(k_sq, axis=0, keepdims=True))
        vv = p_ref[:, D_ATTN + D_KV:]
        second = pltpu.roll(k_rot, HEAD_DIM, 1)
        for c in range(n_sub):
            rows = slice(c * tk, (c + 1) * tk)
            k_ref[0, c] = k_rot[rows, :HEAD_DIM].astype(BF16)
            k_ref[1, c] = second[rows, :HEAD_DIM].astype(BF16)
        for a, feat_ref, width, n in ((k_rot, kt_ref, tk, n_sub), (vv, vtb_ref, tk, n_sub), (vv, vt_ref, tk_v, n_sub_v)):
            for c in range(n):
                tile = a[c * width:(c + 1) * width].T.reshape(N_KV_HEADS, HEAD_DIM, width)
                feat_ref[:, c, :HEAD_DIM, :] = tile.astype(BF16)
        vt_ref[:, :, HEAD_DIM:, :] = jnp.ones((N_KV_HEADS, n_sub_v, ONES_ROWS, tk_v), BF16)
        minus = _shift_rows((N_KV_HEADS, n_sub, HEAD_DIM, tk), -1.0, -1.0).astype(BF16)
        kt_ref[:, :, HEAD_DIM:, :] = minus
        vtb_ref[:, :, HEAD_DIM:, :] = minus

    kv = lambda rows, cols: jax.ShapeDtypeStruct((N_KV_HEADS, t // tk, rows, cols), BF16)
    return pl.pallas_call(
        functools.partial(body), name="qk_prep", grid=(t // tm,),
        in_specs=[pl.BlockSpec((tm, D_QKV), lambda i: (i, 0)), _full((1, D_ATTN)), _full((1, D_KV)),
                  pl.BlockSpec((tm, LANES), lambda i: (i, 0)), pl.BlockSpec((tm, LANES), lambda i: (i, 0)),
                  _full((D_ATTN, D_ATTN)), _full((D_KV, D_KV))],
        out_specs=[_head_tile_spec(tm, HEAD_DIM), _kv_tile_spec(n_sub, tk, HEAD_DIM),
                   _kv_tile_spec(n_sub, 2 * HEAD_DIM, tk),
                   _kv_tile_spec(n_sub_v, HEAD_DIM + ONES_ROWS, tk_v), _kv_tile_spec(n_sub, 2 * HEAD_DIM, tk),
                   _full((1, D_ATTN)), _full((1, D_KV))],
        out_shape=[jax.ShapeDtypeStruct((N_HEADS, t // tm, HEAD_DIM, tm), BF16), kv(tk, HEAD_DIM), kv(2 * HEAD_DIM, tk),
                   jax.ShapeDtypeStruct((N_KV_HEADS, t // tk_v, HEAD_DIM + ONES_ROWS, tk_v), BF16),
                   kv(2 * HEAD_DIM, tk),
                   jax.ShapeDtypeStruct((1, D_ATTN), F32), jax.ShapeDtypeStruct((1, D_KV), F32)],
        compiler_params=_params(1),
    )(qkv, gq_w, gk_w, cos_w, sin_w, mean_q, mean_k)


def qk_bwd(dq_rot, dk_rot, dv, qkv, gq_w, gk_w, cos_w, sin_w, mean_q, mean_k, tm):
    t = qkv.shape[0]
    tk = dk_rot.shape[-1]
    n_sub = tm // tk

    def token_major(ref):
        return jnp.concatenate([ref[:, c].reshape(D_KV, tk).T for c in range(n_sub)], axis=0)

    def branch(raw, d_rot, gain, mean_mat, cos, sin, scale):
        r = lax.rsqrt(_dot_split(raw * raw, mean_mat) + EPS)
        n = raw * r
        dy = (d_rot * cos - _pair_swap(d_rot) * sin) * scale
        dn = dy * gain
        return r * (dn - n * _dot_split(dn * n, mean_mat)), dy * n

    def body(dq_ref, dk_ref, dv_ref, p_ref, gq_ref, gk_ref, cos_ref, sin_ref, mq_ref, mk_ref,
             dp_ref, dgq_ref, dgk_ref):
        cos2, sin2 = cos_ref[...], sin_ref[...]
        cos8, sin8 = _tile_lanes(cos2, D_ATTN // LANES), _tile_lanes(sin2, D_ATTN // LANES)
        dq, dgq = branch(p_ref[:, :D_ATTN], _from_head_tiles(dq_ref[...]), gq_ref[...], mq_ref[...], cos8, sin8,
                         HEAD_DIM ** -0.5)
        dk, dgk = branch(p_ref[:, D_ATTN:D_ATTN + D_KV], token_major(dk_ref), gk_ref[...], mk_ref[...], cos2, sin2, 1.0)
        dp_ref[...] = jnp.concatenate([dq, dk, token_major(dv_ref)], axis=-1).astype(BF16)

        @pl.when(pl.program_id(0) == 0)
        def _():
            dgq_ref[...] = jnp.zeros_like(dgq_ref)
            dgk_ref[...] = jnp.zeros_like(dgk_ref)

        dgq_ref[...] += _colsum(dgq)
        dgk_ref[...] += _colsum(dgk)

    return pl.pallas_call(
        functools.partial(body), name="qk_bwd", grid=(t // tm,),
        in_specs=[_head_tile_spec(tm, HEAD_DIM), _kv_tile_spec(n_sub, HEAD_DIM, tk),
                  _kv_tile_spec(n_sub, HEAD_DIM, tk), pl.BlockSpec((tm, D_QKV), lambda i: (i, 0)),
                  _full((1, D_ATTN)), _full((1, D_KV)),
                  pl.BlockSpec((tm, LANES), lambda i: (i, 0)), pl.BlockSpec((tm, LANES), lambda i: (i, 0)),
                  _full((D_ATTN, D_ATTN)), _full((D_KV, D_KV))],
        out_specs=[pl.BlockSpec((tm, D_QKV), lambda i: (i, 0)), _full((1, D_ATTN)), _full((1, D_KV))],
        out_shape=[jax.ShapeDtypeStruct((t, D_QKV), BF16), jax.ShapeDtypeStruct((1, D_ATTN), F32),
                   jax.ShapeDtypeStruct((1, D_KV), F32)],
        compiler_params=_params(1),
    )(dq_rot, dk_rot, dv, qkv, gq_w, gk_w, cos_w, sin_w, mean_q, mean_k)


def attention_fwd(bound, q_t, k, v_t, comm=None):
    _, nq, _, tq = q_t.shape
    _, nk, tk, _ = k.shape

    def body(bound_ref, q_ref, k_ref, v_ref, o_ref, qtok_ref, s_scr, p_scr):
        head_bound = bound_ref[pl.program_id(0)]
        safe = head_bound <= SAFE_SCORE_BOUND
        q = q_ref[...]
        s_scr[0] = _dot(k_ref[0], q)
        p_scr[1] = jnp.zeros((tk, tq), BF16)
        zero = jnp.zeros((HEAD_DIM + ONES_ROWS, tq), F32)

        def matmuls(j, slot):
            pv = _dot(v_ref[jnp.maximum(j - 1, 0)], p_scr[1 - slot])
            s_scr[1 - slot] = _dot(k_ref[jnp.minimum(j + 1, nk - 1)], q)
            return pv

        def finish(m, acc):
            acc = acc + _dot(v_ref[nk - 1], p_scr[(nk - 1) % 2])
            l = acc[HEAD_DIM:HEAD_DIM + 1]
            o_ref[...] = acc[:HEAD_DIM] / l
            lse_rows = _shift_rows((HEAD_DIM, tq), *_hi_lo(m + jnp.log2(l)))
            qtok_ref[...] = jnp.concatenate([q.astype(F32), lse_rows], axis=0).T.astype(BF16)

        @pl.when(safe)
        def _():
            m = jnp.full((1, tq), head_bound, F32)

            def step(j, slot, acc):
                s = s_scr[slot]
                pv = matmuls(j, slot)
                p_scr[slot] = jnp.exp2(s - m).astype(BF16)
                return acc + pv

            finish(m, _loop_pairs(nk, step, zero, 16 if nk % 16 == 0 else 2))

        @pl.when(jnp.logical_not(safe))
        def _():
            def step(j, slot, carry):
                m, acc = carry
                s = s_scr[slot]
                pv = matmuls(j, slot)
                m_new = jnp.maximum(m, jnp.max(s, axis=0, keepdims=True))
                p_scr[slot] = jnp.exp2(s - m_new).astype(BF16)
                return m_new, jnp.exp2(m - m_new) * (acc + pv)

            finish(*_loop_pairs(nk, step, (jnp.full((1, tq), -1e30, F32), zero)))

    return _pallas(
        comm, body, name="attention_fwd", grid=(N_HEADS, nq),
        in_specs=[pl.BlockSpec(memory_space=pltpu.SMEM),
                  pl.BlockSpec((None, None, HEAD_DIM, tq), lambda h, i: (h, i, 0, 0)),
                  pl.BlockSpec((None, nk, tk, HEAD_DIM), lambda h, i: (h // KV_GROUP, 0, 0, 0)),
                  pl.BlockSpec((None, nk, HEAD_DIM + ONES_ROWS, tk), lambda h, i: (h // KV_GROUP, 0, 0, 0))],
        out_specs=[pl.BlockSpec((None, None, HEAD_DIM, tq), lambda h, i: (h, i, 0, 0)),
                   pl.BlockSpec((None, None, tq, 2 * HEAD_DIM), lambda h, i: (h, i, 0, 0))],
        out_shape=[jax.ShapeDtypeStruct((N_HEADS, nq, HEAD_DIM, tq), F32),
                   jax.ShapeDtypeStruct((N_HEADS, nq, tq, 2 * HEAD_DIM), BF16)],
        scratch_shapes=[pltpu.VMEM((2, tk, tq), F32), pltpu.VMEM((2, tk, tq), BF16)],
        args=(bound, q_t, k, v_t))


def attention_bwd(q_tok, do_tok, q_t, do_t, k_t, v_t, comm=None):
    _, nq, _, tq = q_t.shape
    _, nk, _, tk = k_t.shape

    def body(qtok_ref, dotok_ref, q_ref, do_ref, kt_ref, vt_ref, dq_ref, dk_ref, dv_ref,
             s_scr, dp_scr, p_scr, ds_scr):
        @pl.when(pl.program_id(1) == 0)
        def _():
            dq_ref[...] = jnp.zeros_like(dq_ref)

        kt_aug, vt_aug = kt_ref[...], vt_ref[...]
        kt = kt_aug[:HEAD_DIM]
        n = KV_GROUP * nq
        s_scr[0] = _dot(qtok_ref[0, 0], kt_aug)
        dp_scr[0] = _dot(dotok_ref[0, 0], vt_aug)
        p_scr[1] = jnp.zeros((tq, tk), BF16)
        ds_scr[1] = jnp.zeros((tq, tk), BF16)

        def products(t, slot, dk, dv):
            h, i = t // nq, t % nq
            ds = ds_scr[slot]
            dq_ref[h, i] += _dot_nt(kt, ds)
            return dk + _dot(q_ref[h, i], ds), dv + _dot(do_ref[h, i], p_scr[slot])

        def step(t, slot, carry):
            s, dp = s_scr[slot], dp_scr[slot]
            dk, dv = products(jnp.maximum(t - 1, 0), 1 - slot, *carry)
            nxt = jnp.minimum(t + 1, n - 1)
            s_scr[1 - slot] = _dot(qtok_ref[nxt // nq, nxt % nq], kt_aug)
            dp_scr[1 - slot] = _dot(dotok_ref[nxt // nq, nxt % nq], vt_aug)
            p = jnp.exp2(s)
            p_scr[slot] = p.astype(BF16)
            ds_scr[slot] = (p * dp).astype(BF16)
            return dk, dv

        zero = jnp.zeros((HEAD_DIM, tk), F32)
        dk, dv = products(n - 1, (n - 1) % 2, *_loop_pairs(n, step, (zero, zero), 32 if n % 32 == 0 else 2))
        dk_ref[...] = dk * (1.0 / LOG2_E)
        dv_ref[...] = dv

    group = lambda g, j: (g, 0, 0, 0)
    tile = lambda g, j: (g, j, 0, 0)
    once = pl.Buffered(1)
    return _pallas(
        comm, body, name="attention_bwd", grid=(N_KV_HEADS, nk),
        in_specs=[pl.BlockSpec((KV_GROUP, nq, tq, 2 * HEAD_DIM), group, pipeline_mode=once),
                  pl.BlockSpec((KV_GROUP, nq, tq, 2 * HEAD_DIM), group, pipeline_mode=once),
                  pl.BlockSpec((KV_GROUP, nq, HEAD_DIM, tq), group, pipeline_mode=once),
                  pl.BlockSpec((KV_GROUP, nq, HEAD_DIM, tq), group, pipeline_mode=once),
                  pl.BlockSpec((None, None, 2 * HEAD_DIM, tk), tile),
                  pl.BlockSpec((None, None, 2 * HEAD_DIM, tk), tile)],
        out_specs=[pl.BlockSpec((KV_GROUP, nq, HEAD_DIM, tq), group),
                   pl.BlockSpec((None, None, HEAD_DIM, tk), tile),
                   pl.BlockSpec((None, None, HEAD_DIM, tk), tile)],
        out_shape=[jax.ShapeDtypeStruct((N_HEADS, nq, HEAD_DIM, tq), F32),
                   jax.ShapeDtypeStruct((N_KV_HEADS, nk, HEAD_DIM, tk), F32),
                   jax.ShapeDtypeStruct((N_KV_HEADS, nk, HEAD_DIM, tk), F32)],
        scratch_shapes=[pltpu.VMEM((2, tq, tk), F32), pltpu.VMEM((2, tq, tk), F32),
                        pltpu.VMEM((2, tq, tk), BF16), pltpu.VMEM((2, tq, tk), BF16)],
        args=(q_tok, do_tok, q_t, do_t, k_t, v_t))


def _group_matmul(a_t, w_ref):
    return jnp.concatenate([_dot(a_t[g * SGU_GROUP_DIM:(g + 1) * SGU_GROUP_DIM], w_ref[g])
                            for g in range(N_SGU_GROUPS)], axis=0)


def _gate_forward(z, g_sgu, wst_ref, bias):
    gz, th = _gelu(z)
    u, vv = gz[:, :D_SGU], gz[:, D_SGU:]
    rv = _rstd(vv)
    nv = vv * rv
    vn = nv * g_sgu
    v_chunks, fs = [], []
    for c in range(z.shape[0] // CHUNK):
        vt = vn[c * CHUNK:(c + 1) * CHUNK].T.astype(BF16)
        v_chunks.append(vt)
        fs.append(_group_matmul(vt, wst_ref).T + bias)
    f = jnp.concatenate(fs, axis=0) if len(fs) > 1 else fs[0]
    return th, u, rv, nv, v_chunks, f


def mix_out(z, o, x, g_sgu, g_ao, g_so, ws_t, bias, w_out, tm):
    t = x.shape[0]

    def body(z_ref, o_ref, x_ref, gs_ref, gao_ref, gso_ref, ws_ref, bias_ref, wout_ref, x2_ref, mixed_ref):
        _, u, _, _, _, f = _gate_forward(z_ref[...], gs_ref[...], ws_ref, bias_ref[...])
        sgu = u * f
        oo = _from_head_tiles(o_ref[...])
        mixed = jnp.concatenate([oo * _rstd(oo) * gao_ref[...], sgu * _rstd(sgu) * gso_ref[...]], axis=-1).astype(BF16)
        mixed_ref[...] = mixed
        x2_ref[...] = x_ref[...] + _dot(mixed, wout_ref[...])

    row = lambda n: pl.BlockSpec((tm, n), lambda i: (i, 0))
    return pl.pallas_call(
        functools.partial(body), name="mix_out", grid=(t // tm,),
        in_specs=[row(2 * D_SGU), _head_tile_spec(tm, HEAD_DIM), row(D_MODEL), _full((1, D_SGU)), _full((1, D_ATTN)),
                  _full((1, D_SGU)),
                  _full((N_SGU_GROUPS, CHUNK, CHUNK)), _full((CHUNK, D_SGU)), _full((D_MODEL, D_MODEL))],
        out_specs=[row(D_MODEL), row(D_MODEL)],
        out_shape=[jax.ShapeDtypeStruct((t, D_MODEL), F32), jax.ShapeDtypeStruct((t, D_MODEL), BF16)],
        compiler_params=_params(1),
    )(z, o, x, g_sgu, g_ao, g_so, ws_t, bias, w_out)


def mix_bwd(dx2, z, o, g_sgu, g_ao, g_so, ws, ws_t, bias, w_out, group_ind, tm):
    t = dx2.shape[0]
    n_tiles = t // tm

    def body(dx_ref, z_ref, o_ref, gs_ref, gao_ref, gso_ref, ws_ref, wst_ref, bias_ref, wout_ref, ind_ref,
             do_ref, dotok_ref, dz_ref, dg_ref, dws_ref, dbs_ref, df_sum):
        step = pl.program_id(0)

        @pl.when(step == 0)
        def _():
            dg_ref[...] = jnp.zeros_like(dg_ref)
            dws_ref[...] = jnp.zeros_like(dws_ref)
            df_sum[...] = jnp.zeros_like(df_sum)

        z = z_ref[...]
        th, u, rv, nv, v_chunks, f = _gate_forward(z, gs_ref[...], wst_ref, bias_ref[...])
        dmixed = _dot_nt(dx_ref[...].astype(BF16), wout_ref[...])
        o_tiles = o_ref[...]
        oo = _from_head_tiles(o_tiles)
        ro = _rstd(oo)
        d_o, dgao = _rms_bwd(dmixed[:, :D_ATTN], oo * ro, ro, gao_ref[...])
        do_tiles = _to_head_tiles(d_o)
        do_ref[...] = do_tiles.astype(BF16)
        delta_hi, delta_lo = _hi_lo(jnp.sum(do_tiles * o_tiles, axis=1, keepdims=True))
        for h in range(N_HEADS):
            delta_rows = _shift_rows((HEAD_DIM, tm), delta_hi[h], delta_lo[h])
            dotok_ref[h] = jnp.concatenate([do_tiles[h], delta_rows], axis=0).T.astype(BF16)
        sgu = u * f
        rs = _rstd(sgu)
        dsgu, dgso = _rms_bwd(dmixed[:, D_ATTN:], sgu * rs, rs, gso_ref[...])
        du = dsgu * f
        df = dsgu * u
        dvns = []
        df_acc = jnp.zeros((CHUNK, D_SGU), F32)
        for c in range(tm // CHUNK):
            dfc32 = df[c * CHUNK:(c + 1) * CHUNK]
            dft = dfc32.T.astype(BF16)
            dvns.append(_group_matmul(dft, ws_ref).T)
            for g in range(N_SGU_GROUPS):
                rows = slice(g * SGU_GROUP_DIM, (g + 1) * SGU_GROUP_DIM)
                dws_ref[g] += _dot_tn(dft[rows], v_chunks[c][rows])
            df_acc = df_acc + dfc32
        df_sum[...] += df_acc
        dvn = jnp.concatenate(dvns, axis=0) if len(dvns) > 1 else dvns[0]
        dvv, dgs = _rms_bwd(dvn, nv, rv, gs_ref[...])
        dz_ref[...] = (jnp.concatenate([du, dvv], axis=-1) * _gelu_grad(z, th)).astype(BF16)
        dg_ref[0:1, :] += _colsum(dgao)
        dg_ref[1:2, :] += _colsum(dgso)
        dg_ref[2:3, :] += _colsum(dgs)

        @pl.when(step == n_tiles - 1)
        def _():
            dbs_ref[...] = _dot_f32(df_sum[...], ind_ref[...])

    row = lambda n: pl.BlockSpec((tm, n), lambda i: (i, 0))
    return pl.pallas_call(
        functools.partial(body), name="mix_bwd", grid=(n_tiles,),
        in_specs=[row(D_MODEL), row(2 * D_SGU), _head_tile_spec(tm, HEAD_DIM), _full((1, D_SGU)), _full((1, D_ATTN)),
                  _full((1, D_SGU)),
                  _full((N_SGU_GROUPS, CHUNK, CHUNK)), _full((N_SGU_GROUPS, CHUNK, CHUNK)), _full((CHUNK, D_SGU)),
                  _full((D_MODEL, D_MODEL)), _full((D_SGU, LANES))],
        out_specs=[_head_tile_spec(tm, HEAD_DIM), pl.BlockSpec((N_HEADS, None, tm, 2 * HEAD_DIM), lambda i: (0, i, 0, 0)),
                   row(2 * D_SGU), _full((8, D_SGU)),
                   _full((N_SGU_GROUPS, CHUNK, CHUNK)), _full((CHUNK, LANES))],
        out_shape=[jax.ShapeDtypeStruct((N_HEADS, n_tiles, HEAD_DIM, tm), BF16),
                   jax.ShapeDtypeStruct((N_HEADS, n_tiles, tm, 2 * HEAD_DIM), BF16),
                   jax.ShapeDtypeStruct((t, 2 * D_SGU), BF16),
                   jax.ShapeDtypeStruct((8, D_SGU), F32),
                   jax.ShapeDtypeStruct((N_SGU_GROUPS, CHUNK, CHUNK), F32),
                   jax.ShapeDtypeStruct((CHUNK, LANES), F32)],
        scratch_shapes=[pltpu.VMEM((CHUNK, D_SGU), F32)],
        compiler_params=_params(1),
    )(dx2, z, o, g_sgu, g_ao, g_so, ws, ws_t, bias, w_out, group_ind)


def ffn_down_loss(act, wd, x, g, target, tm):
    t = x.shape[0]

    def body(act_ref, wd_ref, x_ref, g_ref, t_ref, loss_ref, dx_ref, dg_ref):
        @pl.when(pl.program_id(0) == 0)
        def _():
            loss_ref[...] = jnp.zeros_like(loss_ref)
            dg_ref[...] = jnp.zeros_like(dg_ref)

        xx = x_ref[...] + 0.5 * _dot(act_ref[...], wd_ref[...])
        r = _rstd(xx)
        n = xx * r
        err = n * g_ref[...] - t_ref[...]
        per_token = jnp.mean(err * err, axis=-1, keepdims=True)
        loss_ref[...] += 0.5 * jnp.sum(per_token, axis=0, keepdims=True)
        dx, dg_rows = _rms_bwd(err * (1.0 / D_MODEL), n, r, g_ref[...])
        dx_ref[...] = dx
        dg_ref[...] += _colsum(dg_rows)

    row = pl.BlockSpec((tm, D_MODEL), lambda i: (i, 0))
    return pl.pallas_call(
        functools.partial(body), name="ffn_down_loss", grid=(t // tm,),
        in_specs=[pl.BlockSpec((tm, D_FF), lambda i: (i, 0)), _full((D_FF, D_MODEL)), row, _full((1, D_MODEL)), row],
        out_specs=[_full((1, LANES)), row, _full((1, D_MODEL))],
        out_shape=[jax.ShapeDtypeStruct((1, LANES), F32), jax.ShapeDtypeStruct((t, D_MODEL), F32),
                   jax.ShapeDtypeStruct((1, D_MODEL), F32)],
        compiler_params=_params(1),
    )(act, wd, x, g, target)


def _rope_tables(t):
    rows = t // GRID_W
    row_idx = jnp.repeat(jnp.arange(rows, dtype=F32), GRID_W)
    col_idx = jnp.tile(jnp.arange(GRID_W, dtype=F32), rows)
    axis_dim = HEAD_DIM // 2
    inv = 1.0 / (ROPE_THETA ** (jnp.arange(0, axis_dim, 2, dtype=F32) / axis_dim))
    ang = jnp.concatenate([row_idx[:, None] * inv, col_idx[:, None] * inv], axis=-1)
    cos = jnp.repeat(jnp.cos(ang), 2, axis=-1)
    sin = jnp.repeat(jnp.sin(ang), 2, axis=-1) * jnp.tile(jnp.array([-1.0, 1.0], F32), HEAD_DIM // 2)
    return jnp.tile(cos, (1, LANES // HEAD_DIM)), jnp.tile(sin, (1, LANES // HEAD_DIM))


def kernel(x, g_ffn1, w1_gate, w1_up, w1_down, g_mix, w_in, g_q, g_k, g_sgu, w_s, b_s, g_attn_out, g_sgu_out, w_out, g_ffn2, w2_gate, w2_up, w2_down, g_final, loss_target, m_g_ffn1, m_w1_gate, m_w1_up, m_w1_down, m_g_mix, m_w_in, m_g_q, m_g_k, m_g_sgu, m_w_s, m_b_s, m_g_attn_out, m_g_sgu_out, m_w_out, m_g_ffn2, m_w2_gate, m_w2_up, m_w2_down, m_g_final, v_g_ffn1, v_w1_gate, v_w1_up, v_w1_down, v_g_mix, v_w_in, v_g_q, v_g_k, v_g_sgu, v_w_s, v_b_s, v_g_attn_out, v_g_sgu_out, v_w_out, v_g_ffn2, v_w2_gate, v_w2_up, v_w2_down, v_g_final):
    weights = dict(g_ffn1=g_ffn1, w1_gate=w1_gate, w1_up=w1_up, w1_down=w1_down, g_mix=g_mix, w_in=w_in, g_q=g_q,
                   g_k=g_k, g_sgu=g_sgu, w_s=w_s, b_s=b_s, g_attn_out=g_attn_out, g_sgu_out=g_sgu_out, w_out=w_out,
                   g_ffn2=g_ffn2, w2_gate=w2_gate, w2_up=w2_up, w2_down=w2_down, g_final=g_final)
    m_in = dict(g_ffn1=m_g_ffn1, w1_gate=m_w1_gate, w1_up=m_w1_up, w1_down=m_w1_down, g_mix=m_g_mix, w_in=m_w_in,
                g_q=m_g_q, g_k=m_g_k, g_sgu=m_g_sgu, w_s=m_w_s, b_s=m_b_s, g_attn_out=m_g_attn_out,
                g_sgu_out=m_g_sgu_out, w_out=m_w_out, g_ffn2=m_g_ffn2, w2_gate=m_w2_gate, w2_up=m_w2_up,
                w2_down=m_w2_down, g_final=m_g_final)
    v_in = dict(g_ffn1=v_g_ffn1, w1_gate=v_w1_gate, w1_up=v_w1_up, w1_down=v_w1_down, g_mix=v_g_mix, w_in=v_w_in,
                g_q=v_g_q, g_k=v_g_k, g_sgu=v_g_sgu, w_s=v_w_s, b_s=v_b_s, g_attn_out=v_g_attn_out,
                g_sgu_out=v_g_sgu_out, w_out=v_w_out, g_ffn2=v_g_ffn2, w2_gate=v_w2_gate, w2_up=v_w2_up,
                w2_down=v_w2_down, g_final=v_g_final)
    names = list(weights)

    t = x.shape[1]
    x0 = x[0]
    target = loss_target[0]
    tm = min(512, t)
    tm_ff = min(256, t)
    tn_ff = 256
    tq = min(512, t)
    tk = min(256, t)
    tk_fwd = min(512, t)
    tk_w = min(2048, t)

    def shard_rows(name):
        w = weights[name][0]
        return (w.T if name in TRANSPOSED else w).astype(BF16)

    rows_of = dict(SHARD_ROWS)
    full = {}

    def packed(group):
        return jnp.concatenate([shard_rows(n) for n in group], axis=0), [rows_of[n] for n in group]

    def gather_of(group):
        return gather_exchange(*packed(group))

    def take(group, gathered):
        for n, g in zip(group, gathered):
            full[n] = g.reshape(N_DEV * rows_of[n], D_MODEL)

    first, second, third = ("w1_gate", "w1_up"), ("w1_down", "w_in", "w_out"), ("w2_gate", "w2_up", "w2_down")
    take(first, gather_two_level(*packed(first), "gather_first"))

    (h1, a1, b1, act1), gathered = ffn_up(x0, g_ffn1, full["w1_gate"], full["w1_up"], tm_ff, tn_ff, gather_of(second))
    take(second, gathered)
    w_in_t = full["w_in"]
    w_qkv_t, w_z_t = w_in_t[:D_QKV], w_in_t[D_QKV:]
    x1 = ffn_down(act1, full["w1_down"], x0, tm)

    qkv, z, h2 = input_projection(x1, g_mix, w_qkv_t, w_z_t, tm)
    cos_w, sin_w = _rope_tables(t)
    gq_w = jnp.tile(g_q, (1, N_HEADS))
    gk_w = jnp.tile(g_k, (1, N_KV_HEADS))
    mean_q, mean_k = _head_mean_matrix(D_ATTN).astype(BF16), _head_mean_matrix(D_KV).astype(BF16)
    q_t, k_tiles, kt_tiles, vt_tiles, vt_tiles_bwd, q_sq_max, k_sq_max = qk_prep(
        qkv, gq_w, gk_w, cos_w, sin_w, mean_q, mean_k, tq, tk, tk_fwd)
    score_bound = 1.02 * jnp.sqrt(q_sq_max.reshape(N_HEADS, HEAD_DIM)[:, 0]
                                  * jnp.repeat(k_sq_max.reshape(N_KV_HEADS, HEAD_DIM)[:, 0], KV_GROUP))
    k_tiles_fwd = k_tiles.reshape(N_KV_HEADS, t // tk_fwd, tk_fwd, HEAD_DIM)
    (o_t, q_tok), gathered = attention_fwd(score_bound, q_t, k_tiles_fwd, vt_tiles, gather_of(third))
    take(third, gathered)

    ws_b = w_s[0].astype(BF16)
    ws_tb = jnp.swapaxes(w_s[0], 1, 2).astype(BF16)
    bias = jnp.repeat(b_s[0].T, SGU_GROUP_DIM, axis=1)
    x2, mixed = mix_out(z, o_t, x1, g_sgu, g_attn_out, g_sgu_out, ws_tb, bias, full["w_out"], tq)

    (h3, a2, b2, act2), _ = ffn_up(x2, g_ffn2, full["w2_gate"], full["w2_up"], tm_ff, tn_ff)

    loss_part, dx3, dg_final = ffn_down_loss(act2, full["w2_down"], x2, g_final, target, tm)

    tmm = D_FF // 2
    (da2, db2), _ = ffn_bwd_act(dx3, full["w2_down"], a2, b2, tm_ff, tn_ff)
    (dx2, dg_ffn2), _ = norm_bwd_matmul(da2, full["w2_gate"], db2, full["w2_up"], x2, g_ffn2, dx3, tm)
    dwg2, dwu2 = matmul_tn(da2, h3, 1.0, tmm, tk_w), matmul_tn(db2, h3, 1.0, tmm, tk_w)
    dwd2 = matmul_tn(act2, dx3, 0.5, tmm, tk_w)

    group_ind = (jnp.arange(D_SGU)[:, None] // SGU_GROUP_DIM == jnp.arange(LANES)[None, :]).astype(F32)
    do_t, do_tok, dz, dg_mixrow, dws, dbs = mix_bwd(dx2, z, o_t, g_sgu, g_attn_out, g_sgu_out, ws_b, ws_tb, bias,
                                                    full["w_out"], group_ind, tq)
    dw_out = matmul_tn(mixed, dx2, 1.0, D_MODEL // 2, tk_w)

    group_a = ("w2_gate", "w2_up", "w2_down", "w_out")
    (dq_t, dk_t, dv_t), (parts_a,) = attention_bwd(q_tok, do_tok, q_t, do_t, kt_tiles, vt_tiles_bwd,
                                                   scatter_exchange([dwg2, dwu2, dwd2, dw_out]))
    dqkv, dgq_w, dgk_w = qk_bwd(dq_t, dk_t, dv_t, qkv, gq_w, gk_w, cos_w, sin_w, mean_q, mean_k, tq)

    def pack_small(arrays):
        pieces = []
        for a in arrays:
            flat = a.reshape(-1)
            pieces.append(jnp.pad(flat, (0, (-flat.shape[0]) % (8 * LANES))).reshape(-1, LANES))
        return jnp.concatenate(pieces, axis=0), [p.shape[0] for p in pieces]

    early = dict(g_ffn2=dg_ffn2, g_final=dg_final, g_q=dgq_w.reshape(N_HEADS, HEAD_DIM).sum(0),
                 g_k=dgk_w.reshape(N_KV_HEADS, HEAD_DIM).sum(0), g_attn_out=dg_mixrow[0], g_sgu_out=dg_mixrow[1],
                 g_sgu=dg_mixrow[2], w_s=dws, b_s=dbs[:, :N_SGU_GROUPS].T)
    early_pack, early_rows = pack_small(list(early.values()))
    (dx1, dg_mix), (early_parts,) = norm_bwd_matmul(dqkv, w_qkv_t, dz, w_z_t, x1, g_mix, dx2, tm,
                                                    gather_exchange(early_pack, [early_pack.shape[0]]))
    dw_in = jnp.concatenate([matmul_tn(dqkv, h2, 1.0, D_QKV // 2, tk_w), matmul_tn(dz, h2, 1.0, D_SGU, tk_w)], axis=0)

    dwd1, (parts_in,) = matmul_tn(act1, dx1, 0.5, tmm, tk_w, scatter_exchange([dw_in]))
    (da1, db1), (parts_d1,) = ffn_bwd_act(dx1, full["w1_down"], a1, b1, tm_ff, tn_ff, scatter_exchange([dwd1]))
    dwg1 = matmul_tn(da1, h1, 1.0, tmm, tk_w)
    dwu1, (parts_g1,) = matmul_tn(db1, h1, 1.0, tmm, tk_w, scatter_exchange([dwg1]))
    (dx0, dg_ffn1), (parts_u1,) = norm_bwd_matmul(da1, full["w1_gate"], db1, full["w1_up"], x0, g_ffn1, dx1, tm,
                                                  scatter_exchange([dwu1]))
    scattered = ((group_a, parts_a), (("w_in",), parts_in), (("w1_down",), parts_d1), (("w1_gate",), parts_g1),
                 (("w1_up",), parts_u1))

    late = dict(g_mix=dg_mix, g_ffn1=dg_ffn1, loss=loss_part)
    late_pack, late_rows = pack_small(list(late.values()))
    (late_parts,) = run_exchange(gather_exchange(late_pack, [late_pack.shape[0]]), "gather_late_small_grads")
    small_sums = {}
    for entries, rows, parts in ((early, early_rows, early_parts), (late, late_rows, late_parts)):
        summed = sum_parts(parts, parts.shape[1])
        off = 0
        for n, r in zip(entries, rows):
            small_sums[n] = summed[off:off + r]
            off += r
    loss = small_sums.pop("loss")[0, 0]

    grads, row_grads = {}, {}
    for group, parts in scattered:
        rows = parts.shape[1]
        summed = sum_parts(parts, rows if rows <= 2 * rows_of["w1_gate"] else rows // 2)
        off = 0
        for n in group:
            row_grads[n] = summed[off:off + rows_of[n]]
            grads[n] = (row_grads[n].T if n in TRANSPOSED else row_grads[n])[None]
            off += rows_of[n]
    for n, summed in small_sums.items():
        grads[n] = summed.reshape(-1)[:weights[n].size].reshape(weights[n].shape)

    delta_w, new_m, new_v = {}, {}, {}
    for n in names:
        shape = weights[n].shape
        if n in TRANSPOSED:
            view, unview, g = (lambda a: a[0].T), (lambda a: a.T[None]), row_grads[n]
        else:
            view, unview = (lambda a: a.reshape(-1, shape[-1])), (lambda a: a.reshape(shape))
            g = view(grads[n])
        d, m2, v2 = adamw(view(weights[n]), g, view(m_in[n]), view(v_in[n]))
        delta_w[n], new_m[n], new_v[n] = unview(d), unview(m2), unview(v2)

    return (loss, dx0[None], *[grads[n] for n in names], *[delta_w[n] for n in names],
            *[new_m[n] for n in names], *[new_v[n] for n in names])
```

```python
import functools
import math

import jax
import jax.numpy as jnp
from jax import lax
from jax.experimental import pallas as pl
from jax.experimental.pallas import tpu as pltpu

F32 = jnp.float32
BF16 = jnp.bfloat16

D_MODEL = 1024
D_FF = 2816
N_HEADS = 8
HEAD_DIM = 64
N_KV_HEADS = 2
KV_GROUP = N_HEADS // N_KV_HEADS
D_ATTN = N_HEADS * HEAD_DIM
D_KV = N_KV_HEADS * HEAD_DIM
D_QKV = D_ATTN + 2 * D_KV
N_SGU_GROUPS = 8
SGU_GROUP_DIM = 64
D_SGU = N_SGU_GROUPS * SGU_GROUP_DIM
CHUNK = 128
GRID_W = 64
ROPE_THETA = 10000.0
EPS = 1e-6
N_DEV = 8
LANES = 128

ONES_ROWS = 16
SAFE_SCORE_BOUND = 60.0
LOG2_E = math.log2(math.e)
Q_SCALE = HEAD_DIM ** -0.5 * LOG2_E

ADAM_LR = 0.001
ADAM_B1 = 0.9
ADAM_B2 = 0.999
ADAM_EPS = 1e-08
ADAM_WD = 0.01
ADAM_STEP = 10

MESH_IDS = pl.DeviceIdType.MESH

VMEM_LIMIT = 56 * 1024 * 1024

SHARD_ROWS = (("w1_gate", D_FF // N_DEV), ("w1_up", D_FF // N_DEV), ("w1_down", D_FF // N_DEV),
              ("w_in", (D_QKV + 2 * D_SGU) // N_DEV), ("w_out", D_MODEL // N_DEV),
              ("w2_gate", D_FF // N_DEV), ("w2_up", D_FF // N_DEV), ("w2_down", D_FF // N_DEV))
TRANSPOSED = ("w1_gate", "w1_up", "w_in", "w2_gate", "w2_up")


def _params(n_grid):
    return pltpu.CompilerParams(dimension_semantics=("arbitrary",) * n_grid, vmem_limit_bytes=VMEM_LIMIT)


def _dot(a, b):
    return jnp.dot(a, b, preferred_element_type=F32)


def _dot_nt(a, b):
    return lax.dot_general(a, b, (((1,), (1,)), ((), ())), preferred_element_type=F32)


def _dot_tn(a, b):
    return lax.dot_general(a, b, (((0,), (0,)), ((), ())), preferred_element_type=F32)


def _dot_f32(a, b):
    return jnp.dot(a, b, preferred_element_type=F32, precision=lax.Precision.HIGHEST)


def _dot_split(a, b):
    hi = a.astype(BF16)
    lo = (a - hi.astype(F32)).astype(BF16)
    return _dot(hi, b) + _dot(lo, b)


def _rstd(x):
    return lax.rsqrt(jnp.mean(x * x, axis=-1, keepdims=True) + EPS)


def _rms_bwd(dy, n, r, g):
    dn = dy * g
    return r * (dn - n * jnp.mean(dn * n, axis=-1, keepdims=True)), dy * n


def _colsum(a):
    return jnp.sum(a, axis=0, keepdims=True)


_GELU_C = math.sqrt(2.0 / math.pi)


def _gelu(x):
    t = jnp.tanh(_GELU_C * (x + 0.044715 * (x * x * x)))
    return x * (0.5 * (1.0 + t)), t


def _gelu_grad(x, t):
    return 0.5 * (1.0 + t) + 0.5 * x * (1.0 - t * t) * (_GELU_C * (1.0 + 3 * 0.044715 * x * x))


def _pair_swap(a):
    w = a.shape[-1]
    lane = lax.broadcasted_iota(jnp.int32, a.shape, a.ndim - 1)
    return jnp.where(lane % 2 == 0, pltpu.roll(a, w - 1, a.ndim - 1), pltpu.roll(a, 1, a.ndim - 1))


def _tile_lanes(a, reps):
    return jnp.concatenate([a] * reps, axis=-1) if reps > 1 else a


def _loop_pairs(n, step, carry, per_body=2):
    assert n % per_body == 0 and per_body % 2 == 0, (n, per_body)

    def body(jj, c):
        for u in range(per_body):
            c = step(per_body * jj + u, u % 2, c)
        return c

    return lax.fori_loop(0, n // per_body, body, carry)


def _full(shape):
    nd = len(shape)
    return pl.BlockSpec(shape, lambda *_: (0,) * nd)


def _mesh_pos():
    return lax.axis_index("x"), lax.axis_index("y"), lax.axis_index("c")


def _peer(pos, d):
    x, y, c = pos
    px = 1 - x if d & 4 else x
    py = 1 - y if d & 2 else y
    pc = 1 - c if d & 1 else c
    return (px, py, pc), 4 * px + 2 * py + pc


class _Exchange:
    def __init__(self, operands, out_shape, n_local, plan):
        self.operands = list(operands)
        self.out_shape = list(out_shape)
        self.sem_shapes = [pltpu.SemaphoreType.DMA((N_DEV - 1,)), pltpu.SemaphoreType.DMA((N_DEV - 1,)),
                           pltpu.SemaphoreType.DMA((n_local,))]
        self._plan = plan

    def _copies(self, in_refs, out_refs):
        pos = _mesh_pos()
        return pos, self._plan(4 * pos[0] + 2 * pos[1] + pos[2], in_refs, out_refs)

    def start(self, in_refs, out_refs, sems):
        send_sems, recv_sems, local_sems = sems
        pos, (local, remote, _) = self._copies(in_refs, out_refs)
        for k, (src, dst) in enumerate(local):
            pltpu.make_async_copy(src, dst, local_sems.at[k]).start()
        for d in range(1, N_DEV):
            peer, peer_lin = _peer(pos, d)
            for src, dst in remote(peer_lin):
                pltpu.make_async_remote_copy(src_ref=src, dst_ref=dst, send_sem=send_sems.at[d - 1],
                                             recv_sem=recv_sems.at[d - 1], device_id=peer,
                                             device_id_type=MESH_IDS).start()

    def wait(self, in_refs, out_refs, sems):
        send_sems, recv_sems, local_sems = sems
        pos, (local, _, whole) = self._copies(in_refs, out_refs)
        for d in range(1, N_DEV):
            peer, peer_lin = _peer(pos, d)
            ref = whole(peer_lin)
            everything = pltpu.make_async_remote_copy(src_ref=ref, dst_ref=ref, send_sem=send_sems.at[d - 1],
                                                      recv_sem=recv_sems.at[d - 1], device_id=peer,
                                                      device_id_type=MESH_IDS)
            everything.wait_send()
            everything.wait_recv()
        for k, (src, dst) in enumerate(local):
            pltpu.make_async_copy(src, dst, local_sems.at[k]).wait()


def _offsets(rows):
    offs, o = [], 0
    for r in rows:
        offs.append(o)
        o += r
    return offs


def gather_exchange(src, rows):
    offs = _offsets(rows)

    def plan(me, in_refs, out_refs):
        pieces = [(in_refs[0].at[pl.ds(o, r)], out.at[me]) for o, r, out in zip(offs, rows, out_refs)]
        return pieces, (lambda peer_lin: pieces), (lambda peer_lin: in_refs[0])

    return _Exchange([src], [jax.ShapeDtypeStruct((N_DEV, r) + src.shape[1:], src.dtype) for r in rows],
                     len(rows), plan)


def scatter_exchange(grads):
    rows = [g.shape[0] // N_DEV for g in grads]
    offs = _offsets(rows)

    def plan(me, in_refs, out_refs):
        parts = out_refs[0]

        def slabs(owner):
            return [(g.at[pl.ds(pl.multiple_of(owner * r, 16), r)], parts.at[me, pl.ds(o, r)])
                    for g, o, r in zip(in_refs, offs, rows)]

        return slabs(me), slabs, (lambda peer_lin: parts.at[peer_lin])

    shape = jax.ShapeDtypeStruct((N_DEV, sum(rows)) + grads[0].shape[1:], grads[0].dtype)
    return _Exchange(grads, [shape], len(rows), plan)


def gather_two_level(src, rows, name):
    offs = _offsets(rows)
    n_p = len(rows)

    def body(src_ref, *refs):
        outs, (send_sems, recv_sems, local_sems) = refs[:n_p], refs[n_p:]
        x, y, c = _mesh_pos()
        me, sibling = (x, y, c), (x, y, 1 - c)
        chips = [(1 - x, y), (x, 1 - y), (1 - x, 1 - y)]

        def slab(w, dev):
            return outs[w].at[4 * dev[0] + 2 * dev[1] + dev[2]]

        def copy(w, k, block, to, from_src=False):
            return pltpu.make_async_remote_copy(
                src_ref=src_ref.at[pl.ds(offs[w], rows[w])] if from_src else slab(w, block), dst_ref=slab(w, block),
                send_sem=send_sems.at[w * 7 + k], recv_sem=recv_sems.at[w * 7 + k],
                device_id=to, device_id_type=MESH_IDS)

        mine = [pltpu.make_async_copy(src_ref.at[pl.ds(offs[w], rows[w])], slab(w, me), local_sems.at[w])
                for w in range(n_p)]
        for cp in mine:
            cp.start()
        first = []
        for w in range(n_p):
            first.append(copy(w, 0, me, sibling, True))
            first += [copy(w, 1 + j, me, (*chip, c), True) for j, chip in enumerate(chips)]
        for cp in first:
            cp.start()
        passed = []
        for j, chip in enumerate(chips):
            for w in range(n_p):
                copy(w, 1 + j, (*chip, c), me).wait_recv()
                cp = copy(w, 4 + j, (*chip, c), sibling)
                cp.start()
                passed.append(cp)
        for w in range(n_p):
            copy(w, 0, sibling, me).wait_recv()
            for j, chip in enumerate(chips):
                copy(w, 4 + j, (*chip, 1 - c), me).wait_recv()
        for cp in first + passed:
            cp.wait_send()
        for cp in mine:
            cp.wait()

    any_spec = pl.BlockSpec(memory_space=pl.ANY)
    return pl.pallas_call(
        functools.partial(body), name=name,
        out_shape=[jax.ShapeDtypeStruct((N_DEV, r) + src.shape[1:], src.dtype) for r in rows],
        in_specs=[any_spec], out_specs=[any_spec] * n_p,
        scratch_shapes=[pltpu.SemaphoreType.DMA((7 * n_p,)), pltpu.SemaphoreType.DMA((7 * n_p,)),
                        pltpu.SemaphoreType.DMA((n_p,))],
        compiler_params=pltpu.CompilerParams(has_side_effects=True),
    )(src)


def run_exchange(ex, name):
    n_in, n_out = len(ex.operands), len(ex.out_shape)

    def body(*refs):
        parts = refs[:n_in], refs[n_in:n_in + n_out], refs[n_in + n_out:]
        ex.start(*parts)
        ex.wait(*parts)

    any_spec = pl.BlockSpec(memory_space=pl.ANY)
    return pl.pallas_call(
        functools.partial(body), name=name, out_shape=ex.out_shape,
        in_specs=[any_spec] * n_in, out_specs=[any_spec] * n_out, scratch_shapes=ex.sem_shapes,
        compiler_params=pltpu.CompilerParams(has_side_effects=True),
    )(*ex.operands)


def _pallas(comm, body, *, name, grid, in_specs, out_specs, out_shape, args, scratch_shapes=()):
    params = _params(len(grid))
    if comm is None:
        res = pl.pallas_call(functools.partial(body), name=name, grid=grid, in_specs=list(in_specs),
                             out_specs=list(out_specs), out_shape=list(out_shape),
                             scratch_shapes=list(scratch_shapes), compiler_params=params)(*args)
        return list(res), []
    n_in, n_out, n_scr = len(in_specs), len(out_specs), len(scratch_shapes)
    c_in, c_out = len(comm.operands), len(comm.out_shape)

    def edge(last):
        conds = [pl.program_id(a) == (g - 1 if last else 0) for a, g in enumerate(grid)]
        return functools.reduce(jnp.logical_and, conds)

    def wrapped(*refs):
        refs = list(refs)
        ins, refs = refs[:n_in], refs[n_in:]
        cins, refs = refs[:c_in], refs[c_in:]
        outs, refs = refs[:n_out], refs[n_out:]
        couts, refs = refs[:c_out], refs[c_out:]
        scr, sems = refs[:n_scr], refs[n_scr:]

        @pl.when(edge(False))
        def _():
            comm.start(cins, couts, sems)

        body(*ins, *outs, *scr)

        @pl.when(edge(True))
        def _():
            comm.wait(cins, couts, sems)

    any_spec = pl.BlockSpec(memory_space=pl.ANY)
    res = pl.pallas_call(
        wrapped, name=name, grid=grid,
        in_specs=list(in_specs) + [any_spec] * c_in, out_specs=list(out_specs) + [any_spec] * c_out,
        out_shape=list(out_shape) + comm.out_shape, scratch_shapes=list(scratch_shapes) + comm.sem_shapes,
        compiler_params=pltpu.CompilerParams(dimension_semantics=("arbitrary",) * len(grid),
                                             vmem_limit_bytes=VMEM_LIMIT, has_side_effects=True),
    )(*args, *comm.operands)
    return res[:n_out], res[n_out:]


def sum_parts(parts, block_rows):
    n, rows, cols = parts.shape

    def body(p_ref, o_ref):
        acc = p_ref[0].astype(F32)
        for s in range(1, n):
            acc = acc + p_ref[s].astype(F32)
        o_ref[...] = acc

    return pl.pallas_call(
        functools.partial(body), name="sum_parts",
        grid=(rows // block_rows,),
        in_specs=[pl.BlockSpec((n, block_rows, cols), lambda i: (0, i, 0))],
        out_specs=pl.BlockSpec((block_rows, cols), lambda i: (i, 0)),
        out_shape=jax.ShapeDtypeStruct((rows, cols), F32),
        compiler_params=_params(1),
    )(parts)


def adamw(w, g, m, v):
    def body(w_ref, g_ref, m_ref, v_ref, d_ref, m_out, v_out):
        gg = g_ref[...]
        m2 = ADAM_B1 * m_ref[...] + (1.0 - ADAM_B1) * gg
        v2 = ADAM_B2 * v_ref[...] + (1.0 - ADAM_B2) * (gg * gg)
        m_hat = m2 / (1.0 - ADAM_B1 ** ADAM_STEP)
        v_hat = v2 / (1.0 - ADAM_B2 ** ADAM_STEP)
        d_ref[...] = -ADAM_LR * (m_hat / (jnp.sqrt(v_hat) + ADAM_EPS) + ADAM_WD * w_ref[...])
        m_out[...] = m2
        v_out[...] = v2

    spec = _full(w.shape)
    shape = jax.ShapeDtypeStruct(w.shape, F32)
    return pl.pallas_call(
        functools.partial(body), name="adamw",
        in_specs=[spec] * 4, out_specs=[spec] * 3, out_shape=[shape] * 3,
        compiler_params=pltpu.CompilerParams(vmem_limit_bytes=VMEM_LIMIT),
    )(w, g, m, v)


def ffn_up(x, g, wg_t, wu_t, tm, tn, comm=None):
    t = x.shape[0]

    def body(x_ref, g_ref, wg_ref, wu_ref, h_ref, silu_ref, dgate_ref, act_ref):
        xx = x_ref[...]
        h = ((xx * _rstd(xx)) * g_ref[...]).astype(BF16)
        h_ref[...] = h
        for c in range(D_FF // tn):
            cols = slice(c * tn, (c + 1) * tn)
            a = _dot_nt(h, wg_ref[cols, :])
            b = _dot_nt(h, wu_ref[cols, :])
            sig = 0.5 * jnp.tanh(0.5 * a) + 0.5
            silu = a * sig
            silu_ref[:, cols] = silu.astype(BF16)
            dgate_ref[:, cols] = (b * (sig + silu * (1.0 - sig))).astype(BF16)
            act_ref[:, cols] = (silu * b).astype(BF16)

    wide = jax.ShapeDtypeStruct((t, D_FF), BF16)
    row = lambda n: pl.BlockSpec((tm, n), lambda i: (i, 0))
    return _pallas(
        comm, body, name="ffn_up",
        grid=(t // tm,),
        in_specs=[row(D_MODEL), _full((1, D_MODEL)), _full((D_FF, D_MODEL)), _full((D_FF, D_MODEL))],
        out_specs=[row(D_MODEL), row(D_FF), row(D_FF), row(D_FF)],
        out_shape=[jax.ShapeDtypeStruct((t, D_MODEL), BF16), wide, wide, wide],
        args=(x, g, wg_t, wu_t))


def ffn_down(act, wd, x, tm):
    t = x.shape[0]

    def body(act_ref, wd_ref, x_ref, o_ref):
        o_ref[...] = x_ref[...] + 0.5 * _dot(act_ref[...], wd_ref[...])

    return pl.pallas_call(
        functools.partial(body), name="ffn_down",
        grid=(t // tm,),
        in_specs=[pl.BlockSpec((tm, D_FF), lambda i: (i, 0)), _full((D_FF, D_MODEL)),
                  pl.BlockSpec((tm, D_MODEL), lambda i: (i, 0))],
        out_specs=pl.BlockSpec((tm, D_MODEL), lambda i: (i, 0)),
        out_shape=jax.ShapeDtypeStruct((t, D_MODEL), F32),
        compiler_params=_params(1),
    )(act, wd, x)


def ffn_bwd_act(dx, wd, silu, dgate, tm, tn, comm=None):
    t = dx.shape[0]

    def body(dx_ref, wd_ref, silu_ref, dgate_ref, da_ref, db_ref):
        dxb = (0.5 * dx_ref[...]).astype(BF16)
        for c in range(D_FF // tn):
            cols = slice(c * tn, (c + 1) * tn)
            dact = _dot_nt(dxb, wd_ref[cols, :])
            da_ref[:, cols] = (dact * dgate_ref[:, cols].astype(F32)).astype(BF16)
            db_ref[:, cols] = (dact * silu_ref[:, cols].astype(F32)).astype(BF16)

    wide = jax.ShapeDtypeStruct((t, D_FF), BF16)
    row = lambda n: pl.BlockSpec((tm, n), lambda i: (i, 0))
    return _pallas(
        comm, body, name="ffn_bwd_act",
        grid=(t // tm,),
        in_specs=[row(D_MODEL), _full((D_FF, D_MODEL)), row(D_FF), row(D_FF)],
        out_specs=[row(D_FF), row(D_FF)],
        out_shape=[wide, wide],
        args=(dx, wd, silu, dgate))


def norm_bwd_matmul(a1, w1, a2, w2, x, g, dx_in, tm, comm=None):
    t = x.shape[0]
    k1, k2 = a1.shape[1], a2.shape[1]

    def body(a1_ref, w1_ref, a2_ref, w2_ref, x_ref, g_ref, dxin_ref, dx_ref, dg_ref):
        dh = _dot(a1_ref[...], w1_ref[...]) + _dot(a2_ref[...], w2_ref[...])
        xx = x_ref[...]
        r = _rstd(xx)
        dx, dg_rows = _rms_bwd(dh, xx * r, r, g_ref[...])
        dx_ref[...] = dxin_ref[...] + dx

        @pl.when(pl.program_id(0) == 0)
        def _():
            dg_ref[...] = jnp.zeros_like(dg_ref)

        dg_ref[...] += _colsum(dg_rows)

    row = pl.BlockSpec((tm, D_MODEL), lambda i: (i, 0))
    return _pallas(
        comm, body, name="norm_bwd_matmul",
        grid=(t // tm,),
        in_specs=[pl.BlockSpec((tm, k1), lambda i: (i, 0)), _full((k1, D_MODEL)),
                  pl.BlockSpec((tm, k2), lambda i: (i, 0)), _full((k2, D_MODEL)),
                  row, _full((1, D_MODEL)), row],
        out_specs=[row, _full((1, D_MODEL))],
        out_shape=[jax.ShapeDtypeStruct((t, D_MODEL), F32), jax.ShapeDtypeStruct((1, D_MODEL), F32)],
        args=(a1, w1, a2, w2, x, g, dx_in))


def matmul_tn(a, b, scale, tmm, tk, comm=None):
    t, m = a.shape
    n = b.shape[1]
    nk = t // tk

    def body(a_ref, b_ref, o_ref, acc_ref):
        k = pl.program_id(1)

        @pl.when(k == 0)
        def _():
            acc_ref[...] = jnp.zeros_like(acc_ref)

        acc_ref[...] += _dot_tn(a_ref[...].astype(BF16), b_ref[...].astype(BF16))

        @pl.when(k == nk - 1)
        def _():
            o_ref[...] = (scale * acc_ref[...]).astype(BF16)

    (out,), comm_outs = _pallas(
        comm, body, name="matmul_tn",
        grid=(m // tmm, nk),
        in_specs=[pl.BlockSpec((tk, tmm), lambda i, k: (k, i)), pl.BlockSpec((tk, n), lambda i, k: (k, 0))],
        out_specs=[pl.BlockSpec((tmm, n), lambda i, k: (i, 0))],
        out_shape=[jax.ShapeDtypeStruct((m, n), BF16)],
        scratch_shapes=[pltpu.VMEM((tmm, n), F32)],
        args=(a, b))
    return out if comm is None else (out, comm_outs)


def input_projection(x, g, w_qkv_t, w_z_t, tm):
    t = x.shape[0]

    def body(x_ref, g_ref, wq_ref, wz_ref, qkv_ref, z_ref, h_ref):
        xx = x_ref[...]
        h = ((xx * _rstd(xx)) * g_ref[...]).astype(BF16)
        h_ref[...] = h
        qkv_ref[...] = _dot_nt(h, wq_ref[...])
        z_ref[...] = _dot_nt(h, wz_ref[...])

    row = lambda n: pl.BlockSpec((tm, n), lambda i: (i, 0))
    return pl.pallas_call(
        functools.partial(body), name="input_projection", grid=(t // tm,),
        in_specs=[row(D_MODEL), _full((1, D_MODEL)), _full((D_QKV, D_MODEL)), _full((2 * D_SGU, D_MODEL))],
        out_specs=[row(D_QKV), row(2 * D_SGU), row(D_MODEL)],
        out_shape=[jax.ShapeDtypeStruct((t, D_QKV), F32), jax.ShapeDtypeStruct((t, 2 * D_SGU), F32),
                   jax.ShapeDtypeStruct((t, D_MODEL), BF16)],
        compiler_params=_params(1))(x, g, w_qkv_t, w_z_t)


def _shift_rows(shape, first, second):
    row = lax.broadcasted_iota(jnp.int32, shape, len(shape) - 2)
    return jnp.where(row == 0, first, jnp.where(row == 1, second, 0.0))


def _hi_lo(a):
    hi = a.astype(BF16).astype(F32)
    return hi, a - hi


def _head_tile_spec(tm, rows):
    return pl.BlockSpec((N_HEADS, None, rows, tm), lambda i: (0, i, 0, 0))


def _to_head_tiles(a):
    return a.T.reshape(N_HEADS, HEAD_DIM, a.shape[0])


def _from_head_tiles(a):
    return a.reshape(D_ATTN, a.shape[-1]).T


def _head_mean_matrix(width):
    head = jnp.arange(width) // HEAD_DIM
    return (head[:, None] == head[None, :]).astype(F32) / HEAD_DIM


def _kv_tile_spec(n_sub, rows, cols):
    return pl.BlockSpec((N_KV_HEADS, n_sub, rows, cols), lambda i: (0, i, 0, 0))


def qk_prep(qkv, gq_w, gk_w, cos_w, sin_w, mean_q, mean_k, tm, tk, tk_v):
    t = qkv.shape[0]
    n_sub, n_sub_v = tm // tk, tm // tk_v

    def body(p_ref, gq_ref, gk_ref, cos_ref, sin_ref, mq_ref, mk_ref, q_ref, k_ref, kt_ref, vt_ref, vtb_ref,
             qmax_ref, kmax_ref):
        @pl.when(pl.program_id(0) == 0)
        def _():
            qmax_ref[...] = jnp.zeros_like(qmax_ref)
            kmax_ref[...] = jnp.zeros_like(kmax_ref)

        cos2, sin2 = cos_ref[...], sin_ref[...]
        q = p_ref[:, :D_ATTN]
        k = p_ref[:, D_ATTN:D_ATTN + D_KV]
        qn = q * lax.rsqrt(_dot_split(q * q, mq_ref[...]) + EPS) * gq_ref[...]
        kn = k * lax.rsqrt(_dot_split(k * k, mk_ref[...]) + EPS) * gk_ref[...]
        cos8, sin8 = _tile_lanes(cos2, D_ATTN // LANES), _tile_lanes(sin2, D_ATTN // LANES)
        q_rot = (qn * cos8 + _pair_swap(qn) * sin8) * Q_SCALE
        q_ref[...] = _to_head_tiles(q_rot).astype(BF16)
        k_rot = kn * cos2 + _pair_swap(kn) * sin2
        q_sq = HEAD_DIM * _dot_split(q_rot * q_rot, mq_ref[...])
        k_sq = HEAD_DIM * _dot_split(k_rot * k_rot, mk_ref[...])
        qmax_ref[...] = jnp.maximum(qmax_ref[...], jnp.max(q_sq, axis=0, keepdims=True))
        kmax_ref[...] = jnp.maximum(kmax_ref[...], jnp.max(k_sq, axis=0, keepdims=True))
        vv = p_ref[:, D_ATTN + D_KV:]
        second = pltpu.roll(k_rot, HEAD_DIM, 1)
        for c in range(n_sub):
            rows = slice(c * tk, (c + 1) * tk)
            k_ref[0, c] = k_rot[rows, :HEAD_DIM].astype(BF16)
            k_ref[1, c] = second[rows, :HEAD_DIM].astype(BF16)
        for a, feat_ref, width, n in ((k_rot, kt_ref, tk, n_sub), (vv, vtb_ref, tk, n_sub), (vv, vt_ref, tk_v, n_sub_v)):
            for c in range(n):
                tile = a[c * width:(c + 1) * width].T.reshape(N_KV_HEADS, HEAD_DIM, width)
                feat_ref[:, c, :HEAD_DIM, :] = tile.astype(BF16)
        vt_ref[:, :, HEAD_DIM:, :] = jnp.ones((N_KV_HEADS, n_sub_v, ONES_ROWS, tk_v), BF16)
        minus = _shift_rows((N_KV_HEADS, n_sub, HEAD_DIM, tk), -1.0, -1.0).astype(BF16)
        kt_ref[:, :, HEAD_DIM:, :] = minus
        vtb_ref[:, :, HEAD_DIM:, :] = minus

    kv = lambda rows, cols: jax.ShapeDtypeStruct((N_KV_HEADS, t // tk, rows, cols), BF16)
    return pl.pallas_call(
        functools.partial(body), name="qk_prep", grid=(t // tm,),
        in_specs=[pl.BlockSpec((tm, D_QKV), lambda i: (i, 0)), _full((1, D_ATTN)), _full((1, D_KV)),
                  pl.BlockSpec((tm, LANES), lambda i: (i, 0)), pl.BlockSpec((tm, LANES), lambda i: (i, 0)),
                  _full((D_ATTN, D_ATTN)), _full((D_KV, D_KV))],
        out_specs=[_head_tile_spec(tm, HEAD_DIM), _kv_tile_spec(n_sub, tk, HEAD_DIM),
                   _kv_tile_spec(n_sub, 2 * HEAD_DIM, tk),
                   _kv_tile_spec(n_sub_v, HEAD_DIM + ONES_ROWS, tk_v), _kv_tile_spec(n_sub, 2 * HEAD_DIM, tk),
                   _full((1, D_ATTN)), _full((1, D_KV))],
        out_shape=[jax.ShapeDtypeStruct((N_HEADS, t // tm, HEAD_DIM, tm), BF16), kv(tk, HEAD_DIM), kv(2 * HEAD_DIM, tk),
                   jax.ShapeDtypeStruct((N_KV_HEADS, t // tk_v, HEAD_DIM + ONES_ROWS, tk_v), BF16),
                   kv(2 * HEAD_DIM, tk),
                   jax.ShapeDtypeStruct((1, D_ATTN), F32), jax.ShapeDtypeStruct((1, D_KV), F32)],
        compiler_params=_params(1),
    )(qkv, gq_w, gk_w, cos_w, sin_w, mean_q, mean_k)


def qk_bwd(dq_rot, dk_rot, dv, qkv, gq_w, gk_w, cos_w, sin_w, mean_q, mean_k, tm):
    t = qkv.shape[0]
    tk = dk_rot.shape[-1]
    n_sub = tm // tk

    def token_major(ref):
        return jnp.concatenate([ref[:, c].reshape(D_KV, tk).T for c in range(n_sub)], axis=0)

    def branch(raw, d_rot, gain, mean_mat, cos, sin, scale):
        r = lax.rsqrt(_dot_split(raw * raw, mean_mat) + EPS)
        n = raw * r
        dy = (d_rot * cos - _pair_swap(d_rot) * sin) * scale
        dn = dy * gain
        return r * (dn - n * _dot_split(dn * n, mean_mat)), dy * n

    def body(dq_ref, dk_ref, dv_ref, p_ref, gq_ref, gk_ref, cos_ref, sin_ref, mq_ref, mk_ref,
             dp_ref, dgq_ref, dgk_ref):
        cos2, sin2 = cos_ref[...], sin_ref[...]
        cos8, sin8 = _tile_lanes(cos2, D_ATTN // LANES), _tile_lanes(sin2, D_ATTN // LANES)
        dq, dgq = branch(p_ref[:, :D_ATTN], _from_head_tiles(dq_ref[...]), gq_ref[...], mq_ref[...], cos8, sin8,
                         HEAD_DIM ** -0.5)
        dk, dgk = branch(p_ref[:, D_ATTN:D_ATTN + D_KV], token_major(dk_ref), gk_ref[...], mk_ref[...], cos2, sin2, 1.0)
        dp_ref[...] = jnp.concatenate([dq, dk, token_major(dv_ref)], axis=-1).astype(BF16)

        @pl.when(pl.program_id(0) == 0)
        def _():
            dgq_ref[...] = jnp.zeros_like(dgq_ref)
            dgk_ref[...] = jnp.zeros_like(dgk_ref)

        dgq_ref[...] += _colsum(dgq)
        dgk_ref[...] += _colsum(dgk)

    return pl.pallas_call(
        functools.partial(body), name="qk_bwd", grid=(t // tm,),
        in_specs=[_head_tile_spec(tm, HEAD_DIM), _kv_tile_spec(n_sub, HEAD_DIM, tk),
                  _kv_tile_spec(n_sub, HEAD_DIM, tk), pl.BlockSpec((tm, D_QKV), lambda i: (i, 0)),
                  _full((1, D_ATTN)), _full((1, D_KV)),
                  pl.BlockSpec((tm, LANES), lambda i: (i, 0)), pl.BlockSpec((tm, LANES), lambda i: (i, 0)),
                  _full((D_ATTN, D_ATTN)), _full((D_KV, D_KV))],
        out_specs=[pl.BlockSpec((tm, D_QKV), lambda i: (i, 0)), _full((1, D_ATTN)), _full((1, D_KV))],
        out_shape=[jax.ShapeDtypeStruct((t, D_QKV), BF16), jax.ShapeDtypeStruct((1, D_ATTN), F32),
                   jax.ShapeDtypeStruct((1, D_KV), F32)],
        compiler_params=_params(1),
    )(dq_rot, dk_rot, dv, qkv, gq_w, gk_w, cos_w, sin_w, mean_q, mean_k)


def attention_fwd(q_t, k, v_t, comm=None):
    _, nq, _, tq = q_t.shape
    _, nk, tk, _ = k.shape

    def body(q_ref, k_ref, v_ref, o_ref, qtok_ref, s_scr, p_scr):
        q = q_ref[...]
        s_scr[0] = _dot(k_ref[0], q)
        p_scr[1] = jnp.zeros((tk, tq), BF16)
        zero = jnp.zeros((HEAD_DIM + ONES_ROWS, tq), F32)

        def matmuls(j, slot):
            pv = _dot(v_ref[jnp.maximum(j - 1, 0)], p_scr[1 - slot])
            s_scr[1 - slot] = _dot(k_ref[jnp.minimum(j + 1, nk - 1)], q)
            return pv

        def finish(m, acc):
            acc = acc + _dot(v_ref[nk - 1], p_scr[(nk - 1) % 2])
            l = acc[HEAD_DIM:HEAD_DIM + 1]
            o_ref[...] = acc[:HEAD_DIM] / l
            lse_rows = _shift_rows((HEAD_DIM, tq), *_hi_lo(m + jnp.log2(l)))
            qtok_ref[...] = jnp.concatenate([q.astype(F32), lse_rows], axis=0).T.astype(BF16)

        def step(j, slot, carry):
            m, acc = carry
            s = s_scr[slot]
            pv = matmuls(j, slot)
            m_new = jnp.maximum(m, jnp.max(s, axis=0, keepdims=True))
            p_scr[slot] = jnp.exp2(s - m_new).astype(BF16)
            return m_new, jnp.exp2(m - m_new) * (acc + pv)

        finish(*_loop_pairs(nk, step, (jnp.full((1, tq), -1e30, F32), zero), 16 if nk % 16 == 0 else 2))

    return _pallas(
        comm, body, name="attention_fwd", grid=(N_HEADS, nq),
        in_specs=[pl.BlockSpec((None, None, HEAD_DIM, tq), lambda h, i: (h, i, 0, 0)),
                  pl.BlockSpec((None, nk, tk, HEAD_DIM), lambda h, i: (h // KV_GROUP, 0, 0, 0)),
                  pl.BlockSpec((None, nk, HEAD_DIM + ONES_ROWS, tk), lambda h, i: (h // KV_GROUP, 0, 0, 0))],
        out_specs=[pl.BlockSpec((None, None, HEAD_DIM, tq), lambda h, i: (h, i, 0, 0)),
                   pl.BlockSpec((None, None, tq, 2 * HEAD_DIM), lambda h, i: (h, i, 0, 0))],
        out_shape=[jax.ShapeDtypeStruct((N_HEADS, nq, HEAD_DIM, tq), F32),
                   jax.ShapeDtypeStruct((N_HEADS, nq, tq, 2 * HEAD_DIM), BF16)],
        scratch_shapes=[pltpu.VMEM((2, tk, tq), F32), pltpu.VMEM((2, tk, tq), BF16)],
        args=(q_t, k, v_t))


def attention_bwd(q_tok, do_tok, q_t, do_t, k_t, v_t, comm=None):
    _, nq, _, tq = q_t.shape
    _, nk, _, tk = k_t.shape

    def body(qtok_ref, dotok_ref, q_ref, do_ref, kt_ref, vt_ref, dq_ref, dk_ref, dv_ref,
             s_scr, dp_scr, p_scr, ds_scr):
        @pl.when(pl.program_id(1) == 0)
        def _():
            dq_ref[...] = jnp.zeros_like(dq_ref)

        kt_aug, vt_aug = kt_ref[...], vt_ref[...]
        kt = kt_aug[:HEAD_DIM]
        n = KV_GROUP * nq
        s_scr[0] = _dot(qtok_ref[0, 0], kt_aug)
        dp_scr[0] = _dot(dotok_ref[0, 0], vt_aug)
        p_scr[1] = jnp.zeros((tq, tk), BF16)
        ds_scr[1] = jnp.zeros((tq, tk), BF16)

        def products(t, slot, dk, dv):
            h, i = t // nq, t % nq
            ds = ds_scr[slot]
            dq_ref[h, i] += _dot_nt(kt, ds)
            return dk + _dot(q_ref[h, i], ds), dv + _dot(do_ref[h, i], p_scr[slot])

        def step(t, slot, carry):
            s, dp = s_scr[slot], dp_scr[slot]
            dk, dv = products(jnp.maximum(t - 1, 0), 1 - slot, *carry)
            nxt = jnp.minimum(t + 1, n - 1)
            s_scr[1 - slot] = _dot(qtok_ref[nxt // nq, nxt % nq], kt_aug)
            dp_scr[1 - slot] = _dot(dotok_ref[nxt // nq, nxt % nq], vt_aug)
            p = jnp.exp2(s)
            p_scr[slot] = p.astype(BF16)
            ds_scr[slot] = (p * dp).astype(BF16)
            return dk, dv

        zero = jnp.zeros((HEAD_DIM, tk), F32)
        dk, dv = products(n - 1, (n - 1) % 2, *_loop_pairs(n, step, (zero, zero), 16 if n % 16 == 0 else 2))
        dk_ref[...] = dk * (1.0 / LOG2_E)
        dv_ref[...] = dv

    group = lambda g, j: (g, 0, 0, 0)
    tile = lambda g, j: (g, j, 0, 0)
    once = pl.Buffered(1)
    return _pallas(
        comm, body, name="attention_bwd", grid=(N_KV_HEADS, nk),
        in_specs=[pl.BlockSpec((KV_GROUP, nq, tq, 2 * HEAD_DIM), group, pipeline_mode=once),
                  pl.BlockSpec((KV_GROUP, nq, tq, 2 * HEAD_DIM), group, pipeline_mode=once),
                  pl.BlockSpec((KV_GROUP, nq, HEAD_DIM, tq), group, pipeline_mode=once),
                  pl.BlockSpec((KV_GROUP, nq, HEAD_DIM, tq), group, pipeline_mode=once),
                  pl.BlockSpec((None, None, 2 * HEAD_DIM, tk), tile),
                  pl.BlockSpec((None, None, 2 * HEAD_DIM, tk), tile)],
        out_specs=[pl.BlockSpec((KV_GROUP, nq, HEAD_DIM, tq), group),
                   pl.BlockSpec((None, None, HEAD_DIM, tk), tile),
                   pl.BlockSpec((None, None, HEAD_DIM, tk), tile)],
        out_shape=[jax.ShapeDtypeStruct((N_HEADS, nq, HEAD_DIM, tq), F32),
                   jax.ShapeDtypeStruct((N_KV_HEADS, nk, HEAD_DIM, tk), F32),
                   jax.ShapeDtypeStruct((N_KV_HEADS, nk, HEAD_DIM, tk), F32)],
        scratch_shapes=[pltpu.VMEM((2, tq, tk), F32), pltpu.VMEM((2, tq, tk), F32),
                        pltpu.VMEM((2, tq, tk), BF16), pltpu.VMEM((2, tq, tk), BF16)],
        args=(q_tok, do_tok, q_t, do_t, k_t, v_t))


def _group_matmul(a_t, w_ref):
    return jnp.concatenate([_dot(a_t[g * SGU_GROUP_DIM:(g + 1) * SGU_GROUP_DIM], w_ref[g])
                            for g in range(N_SGU_GROUPS)], axis=0)


def _gate_forward(z, g_sgu, wst_ref, bias):
    gz, th = _gelu(z)
    u, vv = gz[:, :D_SGU], gz[:, D_SGU:]
    rv = _rstd(vv)
    nv = vv * rv
    vn = nv * g_sgu
    v_chunks, fs = [], []
    for c in range(z.shape[0] // CHUNK):
        vt = vn[c * CHUNK:(c + 1) * CHUNK].T.astype(BF16)
        v_chunks.append(vt)
        fs.append(_group_matmul(vt, wst_ref).T + bias)
    f = jnp.concatenate(fs, axis=0) if len(fs) > 1 else fs[0]
    return th, u, rv, nv, v_chunks, f


def mix_out(z, o, x, g_sgu, g_ao, g_so, ws_t, bias, w_out, tm):
    t = x.shape[0]

    def body(z_ref, o_ref, x_ref, gs_ref, gao_ref, gso_ref, ws_ref, bias_ref, wout_ref, x2_ref, mixed_ref):
        _, u, _, _, _, f = _gate_forward(z_ref[...], gs_ref[...], ws_ref, bias_ref[...])
        sgu = u * f
        oo = _from_head_tiles(o_ref[...])
        mixed = jnp.concatenate([oo * _rstd(oo) * gao_ref[...], sgu * _rstd(sgu) * gso_ref[...]], axis=-1).astype(BF16)
        mixed_ref[...] = mixed
        x2_ref[...] = x_ref[...] + _dot(mixed, wout_ref[...])

    row = lambda n: pl.BlockSpec((tm, n), lambda i: (i, 0))
    return pl.pallas_call(
        functools.partial(body), name="mix_out", grid=(t // tm,),
        in_specs=[row(2 * D_SGU), _head_tile_spec(tm, HEAD_DIM), row(D_MODEL), _full((1, D_SGU)), _full((1, D_ATTN)),
                  _full((1, D_SGU)),
                  _full((N_SGU_GROUPS, CHUNK, CHUNK)), _full((CHUNK, D_SGU)), _full((D_MODEL, D_MODEL))],
        out_specs=[row(D_MODEL), row(D_MODEL)],
        out_shape=[jax.ShapeDtypeStruct((t, D_MODEL), F32), jax.ShapeDtypeStruct((t, D_MODEL), BF16)],
        compiler_params=_params(1),
    )(z, o, x, g_sgu, g_ao, g_so, ws_t, bias, w_out)


def mix_bwd(dx2, z, o, g_sgu, g_ao, g_so, ws, ws_t, bias, w_out, group_ind, tm):
    t = dx2.shape[0]
    n_tiles = t // tm

    def body(dx_ref, z_ref, o_ref, gs_ref, gao_ref, gso_ref, ws_ref, wst_ref, bias_ref, wout_ref, ind_ref,
             do_ref, dotok_ref, dz_ref, dg_ref, dws_ref, dbs_ref, df_sum):
        step = pl.program_id(0)

        @pl.when(step == 0)
        def _():
            dg_ref[...] = jnp.zeros_like(dg_ref)
            dws_ref[...] = jnp.zeros_like(dws_ref)
            df_sum[...] = jnp.zeros_like(df_sum)

        z = z_ref[...]
        th, u, rv, nv, v_chunks, f = _gate_forward(z, gs_ref[...], wst_ref, bias_ref[...])
        dmixed = _dot_nt(dx_ref[...].astype(BF16), wout_ref[...])
        o_tiles = o_ref[...]
        oo = _from_head_tiles(o_tiles)
        ro = _rstd(oo)
        d_o, dgao = _rms_bwd(dmixed[:, :D_ATTN], oo * ro, ro, gao_ref[...])
        do_tiles = _to_head_tiles(d_o)
        do_ref[...] = do_tiles.astype(BF16)
        delta_hi, delta_lo = _hi_lo(jnp.sum(do_tiles * o_tiles, axis=1, keepdims=True))
        for h in range(N_HEADS):
            delta_rows = _shift_rows((HEAD_DIM, tm), delta_hi[h], delta_lo[h])
            dotok_ref[h] = jnp.concatenate([do_tiles[h], delta_rows], axis=0).T.astype(BF16)
        sgu = u * f
        rs = _rstd(sgu)
        dsgu, dgso = _rms_bwd(dmixed[:, D_ATTN:], sgu * rs, rs, gso_ref[...])
        du = dsgu * f
        df = dsgu * u
        dvns = []
        df_acc = jnp.zeros((CHUNK, D_SGU), F32)
        for c in range(tm // CHUNK):
            dfc32 = df[c * CHUNK:(c + 1) * CHUNK]
            dft = dfc32.T.astype(BF16)
            dvns.append(_group_matmul(dft, ws_ref).T)
            for g in range(N_SGU_GROUPS):
                rows = slice(g * SGU_GROUP_DIM, (g + 1) * SGU_GROUP_DIM)
                dws_ref[g] += _dot_tn(dft[rows], v_chunks[c][rows])
            df_acc = df_acc + dfc32
        df_sum[...] += df_acc
        dvn = jnp.concatenate(dvns, axis=0) if len(dvns) > 1 else dvns[0]
        dvv, dgs = _rms_bwd(dvn, nv, rv, gs_ref[...])
        dz_ref[...] = (jnp.concatenate([du, dvv], axis=-1) * _gelu_grad(z, th)).astype(BF16)
        dg_ref[0:1, :] += _colsum(dgao)
        dg_ref[1:2, :] += _colsum(dgso)
        dg_ref[2:3, :] += _colsum(dgs)

        @pl.when(step == n_tiles - 1)
        def _():
            dbs_ref[...] = _dot_f32(df_sum[...], ind_ref[...])

    row = lambda n: pl.BlockSpec((tm, n), lambda i: (i, 0))
    return pl.pallas_call(
        functools.partial(body), name="mix_bwd", grid=(n_tiles,),
        in_specs=[row(D_MODEL), row(2 * D_SGU), _head_tile_spec(tm, HEAD_DIM), _full((1, D_SGU)), _full((1, D_ATTN)),
                  _full((1, D_SGU)),
                  _full((N_SGU_GROUPS, CHUNK, CHUNK)), _full((N_SGU_GROUPS, CHUNK, CHUNK)), _full((CHUNK, D_SGU)),
                  _full((D_MODEL, D_MODEL)), _full((D_SGU, LANES))],
        out_specs=[_head_tile_spec(tm, HEAD_DIM), pl.BlockSpec((N_HEADS, None, tm, 2 * HEAD_DIM), lambda i: (0, i, 0, 0)),
                   row(2 * D_SGU), _full((8, D_SGU)),
                   _full((N_SGU_GROUPS, CHUNK, CHUNK)), _full((CHUNK, LANES))],
        out_shape=[jax.ShapeDtypeStruct((N_HEADS, n_tiles, HEAD_DIM, tm), BF16),
                   jax.ShapeDtypeStruct((N_HEADS, n_tiles, tm, 2 * HEAD_DIM), BF16),
                   jax.ShapeDtypeStruct((t, 2 * D_SGU), BF16),
                   jax.ShapeDtypeStruct((8, D_SGU), F32),
                   jax.ShapeDtypeStruct((N_SGU_GROUPS, CHUNK, CHUNK), F32),
                   jax.ShapeDtypeStruct((CHUNK, LANES), F32)],
        scratch_shapes=[pltpu.VMEM((CHUNK, D_SGU), F32)],
        compiler_params=_params(1),
    )(dx2, z, o, g_sgu, g_ao, g_so, ws, ws_t, bias, w_out, group_ind)


def ffn_down_loss(act, wd, x, g, target, tm):
    t = x.shape[0]

    def body(act_ref, wd_ref, x_ref, g_ref, t_ref, loss_ref, dx_ref, dg_ref):
        @pl.when(pl.program_id(0) == 0)
        def _():
            loss_ref[...] = jnp.zeros_like(loss_ref)
            dg_ref[...] = jnp.zeros_like(dg_ref)

        xx = x_ref[...] + 0.5 * _dot(act_ref[...], wd_ref[...])
        r = _rstd(xx)
        n = xx * r
        err = n * g_ref[...] - t_ref[...]
        per_token = jnp.mean(err * err, axis=-1, keepdims=True)
        loss_ref[...] += 0.5 * jnp.sum(per_token, axis=0, keepdims=True)
        dx, dg_rows = _rms_bwd(err * (1.0 / D_MODEL), n, r, g_ref[...])
        dx_ref[...] = dx
        dg_ref[...] += _colsum(dg_rows)

    row = pl.BlockSpec((tm, D_MODEL), lambda i: (i, 0))
    return pl.pallas_call(
        functools.partial(body), name="ffn_down_loss", grid=(t // tm,),
        in_specs=[pl.BlockSpec((tm, D_FF), lambda i: (i, 0)), _full((D_FF, D_MODEL)), row, _full((1, D_MODEL)), row],
        out_specs=[_full((1, LANES)), row, _full((1, D_MODEL))],
        out_shape=[jax.ShapeDtypeStruct((1, LANES), F32), jax.ShapeDtypeStruct((t, D_MODEL), F32),
                   jax.ShapeDtypeStruct((1, D_MODEL), F32)],
        compiler_params=_params(1),
    )(act, wd, x, g, target)


def _rope_tables(t):
    rows = t // GRID_W
    row_idx = jnp.repeat(jnp.arange(rows, dtype=F32), GRID_W)
    col_idx = jnp.tile(jnp.arange(GRID_W, dtype=F32), rows)
    axis_dim = HEAD_DIM // 2
    inv = 1.0 / (ROPE_THETA ** (jnp.arange(0, axis_dim, 2, dtype=F32) / axis_dim))
    ang = jnp.concatenate([row_idx[:, None] * inv, col_idx[:, None] * inv], axis=-1)
    cos = jnp.repeat(jnp.cos(ang), 2, axis=-1)
    sin = jnp.repeat(jnp.sin(ang), 2, axis=-1) * jnp.tile(jnp.array([-1.0, 1.0], F32), HEAD_DIM // 2)
    return jnp.tile(cos, (1, LANES // HEAD_DIM)), jnp.tile(sin, (1, LANES // HEAD_DIM))


def kernel(x, g_ffn1, w1_gate, w1_up, w1_down, g_mix, w_in, g_q, g_k, g_sgu, w_s, b_s, g_attn_out, g_sgu_out, w_out, g_ffn2, w2_gate, w2_up, w2_down, g_final, loss_target, m_g_ffn1, m_w1_gate, m_w1_up, m_w1_down, m_g_mix, m_w_in, m_g_q, m_g_k, m_g_sgu, m_w_s, m_b_s, m_g_attn_out, m_g_sgu_out, m_w_out, m_g_ffn2, m_w2_gate, m_w2_up, m_w2_down, m_g_final, v_g_ffn1, v_w1_gate, v_w1_up, v_w1_down, v_g_mix, v_w_in, v_g_q, v_g_k, v_g_sgu, v_w_s, v_b_s, v_g_attn_out, v_g_sgu_out, v_w_out, v_g_ffn2, v_w2_gate, v_w2_up, v_w2_down, v_g_final):
    weights = dict(g_ffn1=g_ffn1, w1_gate=w1_gate, w1_up=w1_up, w1_down=w1_down, g_mix=g_mix, w_in=w_in, g_q=g_q,
                   g_k=g_k, g_sgu=g_sgu, w_s=w_s, b_s=b_s, g_attn_out=g_attn_out, g_sgu_out=g_sgu_out, w_out=w_out,
                   g_ffn2=g_ffn2, w2_gate=w2_gate, w2_up=w2_up, w2_down=w2_down, g_final=g_final)
    m_in = dict(g_ffn1=m_g_ffn1, w1_gate=m_w1_gate, w1_up=m_w1_up, w1_down=m_w1_down, g_mix=m_g_mix, w_in=m_w_in,
                g_q=m_g_q, g_k=m_g_k, g_sgu=m_g_sgu, w_s=m_w_s, b_s=m_b_s, g_attn_out=m_g_attn_out,
                g_sgu_out=m_g_sgu_out, w_out=m_w_out, g_ffn2=m_g_ffn2, w2_gate=m_w2_gate, w2_up=m_w2_up,
                w2_down=m_w2_down, g_final=m_g_final)
    v_in = dict(g_ffn1=v_g_ffn1, w1_gate=v_w1_gate, w1_up=v_w1_up, w1_down=v_w1_down, g_mix=v_g_mix, w_in=v_w_in,
                g_q=v_g_q, g_k=v_g_k, g_sgu=v_g_sgu, w_s=v_w_s, b_s=v_b_s, g_attn_out=v_g_attn_out,
                g_sgu_out=v_g_sgu_out, w_out=v_w_out, g_ffn2=v_g_ffn2, w2_gate=v_w2_gate, w2_up=v_w2_up,
                w2_down=v_w2_down, g_final=v_g_final)
    names = list(weights)

    t = x.shape[1]
    x0 = x[0]
    target = loss_target[0]
    tm = min(512, t)
    tm_ff = min(256, t)
    tn_ff = 256
    tq = min(512, t)
    tk = min(256, t)
    tk_fwd = min(512, t)
    tk_w = min(2048, t)

    def shard_rows(name):
        w = weights[name][0]
        return (w.T if name in TRANSPOSED else w).astype(BF16)

    rows_of = dict(SHARD_ROWS)
    full = {}

    def packed(group):
        return jnp.concatenate([shard_rows(n) for n in group], axis=0), [rows_of[n] for n in group]

    def gather_of(group):
        return gather_exchange(*packed(group))

    def take(group, gathered):
        for n, g in zip(group, gathered):
            full[n] = g.reshape(N_DEV * rows_of[n], D_MODEL)

    first, second, third = ("w1_gate", "w1_up"), ("w1_down", "w_in", "w_out"), ("w2_gate", "w2_up", "w2_down")
    take(first, gather_two_level(*packed(first), "gather_first"))

    (h1, a1, b1, act1), gathered = ffn_up(x0, g_ffn1, full["w1_gate"], full["w1_up"], tm_ff, tn_ff, gather_of(second))
    take(second, gathered)
    w_in_t = full["w_in"]
    w_qkv_t, w_z_t = w_in_t[:D_QKV], w_in_t[D_QKV:]
    x1 = ffn_down(act1, full["w1_down"], x0, tm)

    qkv, z, h2 = input_projection(x1, g_mix, w_qkv_t, w_z_t, tm)
    cos_w, sin_w = _rope_tables(t)
    gq_w = jnp.tile(g_q, (1, N_HEADS))
    gk_w = jnp.tile(g_k, (1, N_KV_HEADS))
    mean_q, mean_k = _head_mean_matrix(D_ATTN).astype(BF16), _head_mean_matrix(D_KV).astype(BF16)
    q_t, k_tiles, kt_tiles, vt_tiles, vt_tiles_bwd, _, _ = qk_prep(
        qkv, gq_w, gk_w, cos_w, sin_w, mean_q, mean_k, tq, tk, tk_fwd)
    k_tiles_fwd = k_tiles.reshape(N_KV_HEADS, t // tk_fwd, tk_fwd, HEAD_DIM)
    (o_t, q_tok), gathered = attention_fwd(q_t, k_tiles_fwd, vt_tiles, gather_of(third))
    take(third, gathered)

    ws_b = w_s[0].astype(BF16)
    ws_tb = jnp.swapaxes(w_s[0], 1, 2).astype(BF16)
    bias = jnp.repeat(b_s[0].T, SGU_GROUP_DIM, axis=1)
    x2, mixed = mix_out(z, o_t, x1, g_sgu, g_attn_out, g_sgu_out, ws_tb, bias, full["w_out"], tq)

    (h3, a2, b2, act2), _ = ffn_up(x2, g_ffn2, full["w2_gate"], full["w2_up"], tm_ff, tn_ff)

    loss_part, dx3, dg_final = ffn_down_loss(act2, full["w2_down"], x2, g_final, target, tm)

    tmm = D_FF // 2
    (da2, db2), _ = ffn_bwd_act(dx3, full["w2_down"], a2, b2, tm_ff, tn_ff)
    (dx2, dg_ffn2), _ = norm_bwd_matmul(da2, full["w2_gate"], db2, full["w2_up"], x2, g_ffn2, dx3, tm)
    dwg2, dwu2 = matmul_tn(da2, h3, 1.0, tmm, tk_w), matmul_tn(db2, h3, 1.0, tmm, tk_w)
    dwd2 = matmul_tn(act2, dx3, 0.5, tmm, tk_w)

    group_ind = (jnp.arange(D_SGU)[:, None] // SGU_GROUP_DIM == jnp.arange(LANES)[None, :]).astype(F32)
    do_t, do_tok, dz, dg_mixrow, dws, dbs = mix_bwd(dx2, z, o_t, g_sgu, g_attn_out, g_sgu_out, ws_b, ws_tb, bias,
                                                    full["w_out"], group_ind, tq)
    dw_out = matmul_tn(mixed, dx2, 1.0, D_MODEL // 2, tk_w)

    group_a = ("w2_gate", "w2_up", "w2_down", "w_out")
    (dq_t, dk_t, dv_t), (parts_a,) = attention_bwd(q_tok, do_tok, q_t, do_t, kt_tiles, vt_tiles_bwd,
                                                   scatter_exchange([dwg2, dwu2, dwd2, dw_out]))
    dqkv, dgq_w, dgk_w = qk_bwd(dq_t, dk_t, dv_t, qkv, gq_w, gk_w, cos_w, sin_w, mean_q, mean_k, tq)

    def pack_small(arrays):
        pieces = []
        for a in arrays:
            flat = a.reshape(-1)
            pieces.append(jnp.pad(flat, (0, (-flat.shape[0]) % (8 * LANES))).reshape(-1, LANES))
        return jnp.concatenate(pieces, axis=0), [p.shape[0] for p in pieces]

    early = dict(g_ffn2=dg_ffn2, g_final=dg_final, g_q=dgq_w.reshape(N_HEADS, HEAD_DIM).sum(0),
                 g_k=dgk_w.reshape(N_KV_HEADS, HEAD_DIM).sum(0), g_attn_out=dg_mixrow[0], g_sgu_out=dg_mixrow[1],
                 g_sgu=dg_mixrow[2], w_s=dws, b_s=dbs[:, :N_SGU_GROUPS].T)
    early_pack, early_rows = pack_small(list(early.values()))
    (dx1, dg_mix), (early_parts,) = norm_bwd_matmul(dqkv, w_qkv_t, dz, w_z_t, x1, g_mix, dx2, tm,
                                                    gather_exchange(early_pack, [early_pack.shape[0]]))
    dw_in = jnp.concatenate([matmul_tn(dqkv, h2, 1.0, D_QKV // 2, tk_w), matmul_tn(dz, h2, 1.0, D_SGU, tk_w)], axis=0)

    dwd1, (parts_in,) = matmul_tn(act1, dx1, 0.5, tmm, tk_w, scatter_exchange([dw_in]))
    (da1, db1), (parts_d1,) = ffn_bwd_act(dx1, full["w1_down"], a1, b1, tm_ff, tn_ff, scatter_exchange([dwd1]))
    dwg1 = matmul_tn(da1, h1, 1.0, tmm, tk_w)
    dwu1, (parts_g1,) = matmul_tn(db1, h1, 1.0, tmm, tk_w, scatter_exchange([dwg1]))
    (dx0, dg_ffn1), (parts_u1,) = norm_bwd_matmul(da1, full["w1_gate"], db1, full["w1_up"], x0, g_ffn1, dx1, tm,
                                                  scatter_exchange([dwu1]))
    scattered = ((group_a, parts_a), (("w_in",), parts_in), (("w1_down",), parts_d1), (("w1_gate",), parts_g1),
                 (("w1_up",), parts_u1))

    late = dict(g_mix=dg_mix, g_ffn1=dg_ffn1, loss=loss_part)
    late_pack, late_rows = pack_small(list(late.values()))
    (late_parts,) = run_exchange(gather_exchange(late_pack, [late_pack.shape[0]]), "gather_late_small_grads")
    small_sums = {}
    for entries, rows, parts in ((early, early_rows, early_parts), (late, late_rows, late_parts)):
        summed = sum_parts(parts, parts.shape[1])
        off = 0
        for n, r in zip(entries, rows):
            small_sums[n] = summed[off:off + r]
            off += r
    loss = small_sums.pop("loss")[0, 0]

    grads, row_grads = {}, {}
    for group, parts in scattered:
        rows = parts.shape[1]
        summed = sum_parts(parts, rows if rows <= 2 * rows_of["w1_gate"] else rows // 2)
        off = 0
        for n in group:
            row_grads[n] = summed[off:off + rows_of[n]]
            grads[n] = (row_grads[n].T if n in TRANSPOSED else row_grads[n])[None]
            off += rows_of[n]
    for n, summed in small_sums.items():
        grads[n] = summed.reshape(-1)[:weights[n].size].reshape(weights[n].shape)

    delta_w, new_m, new_v = {}, {}, {}
    for n in names:
        shape = weights[n].shape
        if n in TRANSPOSED:
            view, unview, g = (lambda a: a[0].T), (lambda a: a.T[None]), row_grads[n]
        else:
            view, unview = (lambda a: a.reshape(-1, shape[-1])), (lambda a: a.reshape(shape))
            g = view(grads[n])
        d, m2, v2 = adamw(view(weights[n]), g, view(m_in[n]), view(v_in[n]))
        delta_w[n], new_m[n], new_v[n] = unview(d), unview(m2), unview(v2)

    return (loss, dx0[None], *[grads[n] for n in names], *[delta_w[n] for n in names],
            *[new_m[n] for n in names], *[new_v[n] for n in names])
```

```python
import functools
import math

import jax
import jax.numpy as jnp
from jax import lax
from jax.experimental import pallas as pl
from jax.experimental.pallas import tpu as pltpu

F32 = jnp.float32
BF16 = jnp.bfloat16

D_MODEL = 1024
D_FF = 2816
N_HEADS = 8
HEAD_DIM = 64
N_KV_HEADS = 2
KV_GROUP = N_HEADS // N_KV_HEADS
D_ATTN = N_HEADS * HEAD_DIM
D_KV = N_KV_HEADS * HEAD_DIM
D_QKV = D_ATTN + 2 * D_KV
N_SGU_GROUPS = 8
SGU_GROUP_DIM = 64
D_SGU = N_SGU_GROUPS * SGU_GROUP_DIM
CHUNK = 128
GRID_W = 64
ROPE_THETA = 10000.0
EPS = 1e-6
N_DEV = 8
LANES = 128

ONES_ROWS = 16
SAFE_SCORE_BOUND = 60.0
LOG2_E = math.log2(math.e)
Q_SCALE = HEAD_DIM ** -0.5 * LOG2_E

ADAM_LR = 0.001
ADAM_B1 = 0.9
ADAM_B2 = 0.999
ADAM_EPS = 1e-08
ADAM_WD = 0.01
ADAM_STEP = 10

MESH_IDS = pl.DeviceIdType.MESH

VMEM_LIMIT = 56 * 1024 * 1024

SHARD_ROWS = (("w1_gate", D_FF // N_DEV), ("w1_up", D_FF // N_DEV), ("w1_down", D_FF // N_DEV),
              ("w_in", (D_QKV + 2 * D_SGU) // N_DEV), ("w_out", D_MODEL // N_DEV),
              ("w2_gate", D_FF // N_DEV), ("w2_up", D_FF // N_DEV), ("w2_down", D_FF // N_DEV))
TRANSPOSED = ("w1_gate", "w1_up", "w_in", "w2_gate", "w2_up")


def _params(n_grid):
    return pltpu.CompilerParams(dimension_semantics=("arbitrary",) * n_grid, vmem_limit_bytes=VMEM_LIMIT)


def _dot(a, b):
    return jnp.dot(a, b, preferred_element_type=F32)


def _dot_nt(a, b):
    return lax.dot_general(a, b, (((1,), (1,)), ((), ())), preferred_element_type=F32)


def _dot_tn(a, b):
    return lax.dot_general(a, b, (((0,), (0,)), ((), ())), preferred_element_type=F32)


def _dot_f32(a, b):
    return jnp.dot(a, b, preferred_element_type=F32, precision=lax.Precision.HIGHEST)


def _dot_split(a, b):
    hi = a.astype(BF16)
    lo = (a - hi.astype(F32)).astype(BF16)
    return _dot(hi, b) + _dot(lo, b)


def _rstd(x):
    return lax.rsqrt(jnp.mean(x * x, axis=-1, keepdims=True) + EPS)


def _rms_bwd(dy, n, r, g):
    dn = dy * g
    return r * (dn - n * jnp.mean(dn * n, axis=-1, keepdims=True)), dy * n


def _colsum(a):
    return jnp.sum(a, axis=0, keepdims=True)


_GELU_C = math.sqrt(2.0 / math.pi)


def _gelu(x):
    t = jnp.tanh(_GELU_C * (x + 0.044715 * (x * x * x)))
    return x * (0.5 * (1.0 + t)), t


def _gelu_grad(x, t):
    return 0.5 * (1.0 + t) + 0.5 * x * (1.0 - t * t) * (_GELU_C * (1.0 + 3 * 0.044715 * x * x))


def _pair_swap(a):
    w = a.shape[-1]
    lane = lax.broadcasted_iota(jnp.int32, a.shape, a.ndim - 1)
    return jnp.where(lane % 2 == 0, pltpu.roll(a, w - 1, a.ndim - 1), pltpu.roll(a, 1, a.ndim - 1))


def _tile_lanes(a, reps):
    return jnp.concatenate([a] * reps, axis=-1) if reps > 1 else a


def _loop_pairs(n, step, carry, per_body=2):
    assert n % per_body == 0 and per_body % 2 == 0, (n, per_body)

    def body(jj, c):
        for u in range(per_body):
            c = step(per_body * jj + u, u % 2, c)
        return c

    return lax.fori_loop(0, n // per_body, body, carry)


def _full(shape):
    nd = len(shape)
    return pl.BlockSpec(shape, lambda *_: (0,) * nd)


def _mesh_pos():
    return lax.axis_index("x"), lax.axis_index("y"), lax.axis_index("c")


def _peer(pos, d):
    x, y, c = pos
    px = 1 - x if d & 4 else x
    py = 1 - y if d & 2 else y
    pc = 1 - c if d & 1 else c
    return (px, py, pc), 4 * px + 2 * py + pc


class _Exchange:
    def __init__(self, operands, out_shape, n_local, plan):
        self.operands = list(operands)
        self.out_shape = list(out_shape)
        self.sem_shapes = [pltpu.SemaphoreType.DMA((N_DEV - 1,)), pltpu.SemaphoreType.DMA((N_DEV - 1,)),
                           pltpu.SemaphoreType.DMA((n_local,))]
        self._plan = plan

    def _copies(self, in_refs, out_refs):
        pos = _mesh_pos()
        return pos, self._plan(4 * pos[0] + 2 * pos[1] + pos[2], in_refs, out_refs)

    def start(self, in_refs, out_refs, sems):
        send_sems, recv_sems, local_sems = sems
        pos, (local, remote, _) = self._copies(in_refs, out_refs)
        for k, (src, dst) in enumerate(local):
            pltpu.make_async_copy(src, dst, local_sems.at[k]).start()
        for d in range(1, N_DEV):
            peer, peer_lin = _peer(pos, d)
            for src, dst in remote(peer_lin):
                pltpu.make_async_remote_copy(src_ref=src, dst_ref=dst, send_sem=send_sems.at[d - 1],
                                             recv_sem=recv_sems.at[d - 1], device_id=peer,
                                             device_id_type=MESH_IDS).start()

    def wait(self, in_refs, out_refs, sems):
        send_sems, recv_sems, local_sems = sems
        pos, (local, _, whole) = self._copies(in_refs, out_refs)
        for d in range(1, N_DEV):
            peer, peer_lin = _peer(pos, d)
            ref = whole(peer_lin)
            everything = pltpu.make_async_remote_copy(src_ref=ref, dst_ref=ref, send_sem=send_sems.at[d - 1],
                                                      recv_sem=recv_sems.at[d - 1], device_id=peer,
                                                      device_id_type=MESH_IDS)
            everything.wait_send()
            everything.wait_recv()
        for k, (src, dst) in enumerate(local):
            pltpu.make_async_copy(src, dst, local_sems.at[k]).wait()


def _offsets(rows):
    offs, o = [], 0
    for r in rows:
        offs.append(o)
        o += r
    return offs


def gather_exchange(src, rows):
    offs = _offsets(rows)

    def plan(me, in_refs, out_refs):
        pieces = [(in_refs[0].at[pl.ds(o, r)], out.at[me]) for o, r, out in zip(offs, rows, out_refs)]
        return pieces, (lambda peer_lin: pieces), (lambda peer_lin: in_refs[0])

    return _Exchange([src], [jax.ShapeDtypeStruct((N_DEV, r) + src.shape[1:], src.dtype) for r in rows],
                     len(rows), plan)


def scatter_exchange(grads):
    rows = [g.shape[0] // N_DEV for g in grads]
    offs = _offsets(rows)

    def plan(me, in_refs, out_refs):
        parts = out_refs[0]

        def slabs(owner):
            return [(g.at[pl.ds(pl.multiple_of(owner * r, 16), r)], parts.at[me, pl.ds(o, r)])
                    for g, o, r in zip(in_refs, offs, rows)]

        return slabs(me), slabs, (lambda peer_lin: parts.at[peer_lin])

    shape = jax.ShapeDtypeStruct((N_DEV, sum(rows)) + grads[0].shape[1:], grads[0].dtype)
    return _Exchange(grads, [shape], len(rows), plan)


def gather_two_level(src, rows, name):
    offs = _offsets(rows)
    n_p = len(rows)

    def body(src_ref, *refs):
        outs, (send_sems, recv_sems, local_sems) = refs[:n_p], refs[n_p:]
        x, y, c = _mesh_pos()
        me, sibling = (x, y, c), (x, y, 1 - c)
        chips = [(1 - x, y), (x, 1 - y), (1 - x, 1 - y)]

        def slab(w, dev):
            return outs[w].at[4 * dev[0] + 2 * dev[1] + dev[2]]

        def copy(w, k, block, to, from_src=False):
            return pltpu.make_async_remote_copy(
                src_ref=src_ref.at[pl.ds(offs[w], rows[w])] if from_src else slab(w, block), dst_ref=slab(w, block),
                send_sem=send_sems.at[w * 7 + k], recv_sem=recv_sems.at[w * 7 + k],
                device_id=to, device_id_type=MESH_IDS)

        mine = [pltpu.make_async_copy(src_ref.at[pl.ds(offs[w], rows[w])], slab(w, me), local_sems.at[w])
                for w in range(n_p)]
        for cp in mine:
            cp.start()
        first = []
        for w in range(n_p):
            first.append(copy(w, 0, me, sibling, True))
            first += [copy(w, 1 + j, me, (*chip, c), True) for j, chip in enumerate(chips)]
        for cp in first:
            cp.start()
        passed = []
        for j, chip in enumerate(chips):
            for w in range(n_p):
                copy(w, 1 + j, (*chip, c), me).wait_recv()
                cp = copy(w, 4 + j, (*chip, c), sibling)
                cp.start()
                passed.append(cp)
        for w in range(n_p):
            copy(w, 0, sibling, me).wait_recv()
            for j, chip in enumerate(chips):
                copy(w, 4 + j, (*chip, 1 - c), me).wait_recv()
        for cp in first + passed:
            cp.wait_send()
        for cp in mine:
            cp.wait()

    any_spec = pl.BlockSpec(memory_space=pl.ANY)
    return pl.pallas_call(
        functools.partial(body), name=name,
        out_shape=[jax.ShapeDtypeStruct((N_DEV, r) + src.shape[1:], src.dtype) for r in rows],
        in_specs=[any_spec], out_specs=[any_spec] * n_p,
        scratch_shapes=[pltpu.SemaphoreType.DMA((7 * n_p,)), pltpu.SemaphoreType.DMA((7 * n_p,)),
                        pltpu.SemaphoreType.DMA((n_p,))],
        compiler_params=pltpu.CompilerParams(has_side_effects=True),
    )(src)


def run_exchange(ex, name):
    n_in, n_out = len(ex.operands), len(ex.out_shape)

    def body(*refs):
        parts = refs[:n_in], refs[n_in:n_in + n_out], refs[n_in + n_out:]
        ex.start(*parts)
        ex.wait(*parts)

    any_spec = pl.BlockSpec(memory_space=pl.ANY)
    return pl.pallas_call(
        functools.partial(body), name=name, out_shape=ex.out_shape,
        in_specs=[any_spec] * n_in, out_specs=[any_spec] * n_out, scratch_shapes=ex.sem_shapes,
        compiler_params=pltpu.CompilerParams(has_side_effects=True),
    )(*ex.operands)


def _pallas(comm, body, *, name, grid, in_specs, out_specs, out_shape, args, scratch_shapes=()):
    params = _params(len(grid))
    if comm is None:
        res = pl.pallas_call(functools.partial(body), name=name, grid=grid, in_specs=list(in_specs),
                             out_specs=list(out_specs), out_shape=list(out_shape),
                             scratch_shapes=list(scratch_shapes), compiler_params=params)(*args)
        return list(res), []
    n_in, n_out, n_scr = len(in_specs), len(out_specs), len(scratch_shapes)
    c_in, c_out = len(comm.operands), len(comm.out_shape)

    def edge(last):
        conds = [pl.program_id(a) == (g - 1 if last else 0) for a, g in enumerate(grid)]
        return functools.reduce(jnp.logical_and, conds)

    def wrapped(*refs):
        refs = list(refs)
        ins, refs = refs[:n_in], refs[n_in:]
        cins, refs = refs[:c_in], refs[c_in:]
        outs, refs = refs[:n_out], refs[n_out:]
        couts, refs = refs[:c_out], refs[c_out:]
        scr, sems = refs[:n_scr], refs[n_scr:]

        @pl.when(edge(False))
        def _():
            comm.start(cins, couts, sems)

        body(*ins, *outs, *scr)

        @pl.when(edge(True))
        def _():
            comm.wait(cins, couts, sems)

    any_spec = pl.BlockSpec(memory_space=pl.ANY)
    res = pl.pallas_call(
        wrapped, name=name, grid=grid,
        in_specs=list(in_specs) + [any_spec] * c_in, out_specs=list(out_specs) + [any_spec] * c_out,
        out_shape=list(out_shape) + comm.out_shape, scratch_shapes=list(scratch_shapes) + comm.sem_shapes,
        compiler_params=pltpu.CompilerParams(dimension_semantics=("arbitrary",) * len(grid),
                                             vmem_limit_bytes=VMEM_LIMIT, has_side_effects=True),
    )(*args, *comm.operands)
    return res[:n_out], res[n_out:]


def sum_parts(parts, block_rows):
    n, rows, cols = parts.shape

    def body(p_ref, o_ref):
        acc = p_ref[0].astype(F32)
        for s in range(1, n):
            acc = acc + p_ref[s].astype(F32)
        o_ref[...] = acc

    return pl.pallas_call(
        functools.partial(body), name="sum_parts",
        grid=(rows // block_rows,),
        in_specs=[pl.BlockSpec((n, block_rows, cols), lambda i: (0, i, 0))],
        out_specs=pl.BlockSpec((block_rows, cols), lambda i: (i, 0)),
        out_shape=jax.ShapeDtypeStruct((rows, cols), F32),
        compiler_params=_params(1),
    )(parts)


def adamw(w, g, m, v):
    def body(w_ref, g_ref, m_ref, v_ref, d_ref, m_out, v_out):
        gg = g_ref[...]
        m2 = ADAM_B1 * m_ref[...] + (1.0 - ADAM_B1) * gg
        v2 = ADAM_B2 * v_ref[...] + (1.0 - ADAM_B2) * (gg * gg)
        m_hat = m2 / (1.0 - ADAM_B1 ** ADAM_STEP)
        v_hat = v2 / (1.0 - ADAM_B2 ** ADAM_STEP)
        d_ref[...] = -ADAM_LR * (m_hat / (jnp.sqrt(v_hat) + ADAM_EPS) + ADAM_WD * w_ref[...])
        m_out[...] = m2
        v_out[...] = v2

    spec = _full(w.shape)
    shape = jax.ShapeDtypeStruct(w.shape, F32)
    return pl.pallas_call(
        functools.partial(body), name="adamw",
        in_specs=[spec] * 4, out_specs=[spec] * 3, out_shape=[shape] * 3,
        compiler_params=pltpu.CompilerParams(vmem_limit_bytes=VMEM_LIMIT),
    )(w, g, m, v)


def ffn_up(x, g, wg_t, wu_t, tm, tn, comm=None):
    t = x.shape[0]

    def body(x_ref, g_ref, wg_ref, wu_ref, h_ref, silu_ref, dgate_ref, act_ref):
        xx = x_ref[...]
        h = ((xx * _rstd(xx)) * g_ref[...]).astype(BF16)
        h_ref[...] = h
        for c in range(D_FF // tn):
            cols = slice(c * tn, (c + 1) * tn)
            a = _dot_nt(h, wg_ref[cols, :])
            b = _dot_nt(h, wu_ref[cols, :])
            sig = 0.5 * jnp.tanh(0.5 * a) + 0.5
            silu = a * sig
            silu_ref[:, cols] = silu.astype(BF16)
            dgate_ref[:, cols] = (b * (sig + silu * (1.0 - sig))).astype(BF16)
            act_ref[:, cols] = (silu * b).astype(BF16)

    wide = jax.ShapeDtypeStruct((t, D_FF), BF16)
    row = lambda n: pl.BlockSpec((tm, n), lambda i: (i, 0))
    return _pallas(
        comm, body, name="ffn_up",
        grid=(t // tm,),
        in_specs=[row(D_MODEL), _full((1, D_MODEL)), _full((D_FF, D_MODEL)), _full((D_FF, D_MODEL))],
        out_specs=[row(D_MODEL), row(D_FF), row(D_FF), row(D_FF)],
        out_shape=[jax.ShapeDtypeStruct((t, D_MODEL), BF16), wide, wide, wide],
        args=(x, g, wg_t, wu_t))


def ffn_down(act, wd, x, tm):
    t = x.shape[0]

    def body(act_ref, wd_ref, x_ref, o_ref):
        o_ref[...] = x_ref[...] + 0.5 * _dot(act_ref[...], wd_ref[...])

    return pl.pallas_call(
        functools.partial(body), name="ffn_down",
        grid=(t // tm,),
        in_specs=[pl.BlockSpec((tm, D_FF), lambda i: (i, 0)), _full((D_FF, D_MODEL)),
                  pl.BlockSpec((tm, D_MODEL), lambda i: (i, 0))],
        out_specs=pl.BlockSpec((tm, D_MODEL), lambda i: (i, 0)),
        out_shape=jax.ShapeDtypeStruct((t, D_MODEL), F32),
        compiler_params=_params(1),
    )(act, wd, x)


def ffn_bwd_act(dx, wd, silu, dgate, tm, tn, comm=None):
    t = dx.shape[0]

    def body(dx_ref, wd_ref, silu_ref, dgate_ref, da_ref, db_ref):
        dxb = (0.5 * dx_ref[...]).astype(BF16)
        for c in range(D_FF // tn):
            cols = slice(c * tn, (c + 1) * tn)
            dact = _dot_nt(dxb, wd_ref[cols, :])
            da_ref[:, cols] = (dact * dgate_ref[:, cols].astype(F32)).astype(BF16)
            db_ref[:, cols] = (dact * silu_ref[:, cols].astype(F32)).astype(BF16)

    wide = jax.ShapeDtypeStruct((t, D_FF), BF16)
    row = lambda n: pl.BlockSpec((tm, n), lambda i: (i, 0))
    return _pallas(
        comm, body, name="ffn_bwd_act",
        grid=(t // tm,),
        in_specs=[row(D_MODEL), _full((D_FF, D_MODEL)), row(D_FF), row(D_FF)],
        out_specs=[row(D_FF), row(D_FF)],
        out_shape=[wide, wide],
        args=(dx, wd, silu, dgate))


def norm_bwd_matmul(a1, w1, a2, w2, x, g, dx_in, tm, comm=None):
    t = x.shape[0]
    k1, k2 = a1.shape[1], a2.shape[1]

    def body(a1_ref, w1_ref, a2_ref, w2_ref, x_ref, g_ref, dxin_ref, dx_ref, dg_ref):
        dh = _dot(a1_ref[...], w1_ref[...]) + _dot(a2_ref[...], w2_ref[...])
        xx = x_ref[...]
        r = _rstd(xx)
        dx, dg_rows = _rms_bwd(dh, xx * r, r, g_ref[...])
        dx_ref[...] = dxin_ref[...] + dx

        @pl.when(pl.program_id(0) == 0)
        def _():
            dg_ref[...] = jnp.zeros_like(dg_ref)

        dg_ref[...] += _colsum(dg_rows)

    row = pl.BlockSpec((tm, D_MODEL), lambda i: (i, 0))
    return _pallas(
        comm, body, name="norm_bwd_matmul",
        grid=(t // tm,),
        in_specs=[pl.BlockSpec((tm, k1), lambda i: (i, 0)), _full((k1, D_MODEL)),
                  pl.BlockSpec((tm, k2), lambda i: (i, 0)), _full((k2, D_MODEL)),
                  row, _full((1, D_MODEL)), row],
        out_specs=[row, _full((1, D_MODEL))],
        out_shape=[jax.ShapeDtypeStruct((t, D_MODEL), F32), jax.ShapeDtypeStruct((1, D_MODEL), F32)],
        args=(a1, w1, a2, w2, x, g, dx_in))


def matmul_tn(a, b, scale, tmm, tk, comm=None):
    t, m = a.shape
    n = b.shape[1]
    nk = t // tk

    def body(a_ref, b_ref, o_ref, acc_ref):
        k = pl.program_id(1)

        @pl.when(k == 0)
        def _():
            acc_ref[...] = jnp.zeros_like(acc_ref)

        acc_ref[...] += _dot_tn(a_ref[...].astype(BF16), b_ref[...].astype(BF16))

        @pl.when(k == nk - 1)
        def _():
            o_ref[...] = (scale * acc_ref[...]).astype(BF16)

    (out,), comm_outs = _pallas(
        comm, body, name="matmul_tn",
        grid=(m // tmm, nk),
        in_specs=[pl.BlockSpec((tk, tmm), lambda i, k: (k, i)), pl.BlockSpec((tk, n), lambda i, k: (k, 0))],
        out_specs=[pl.BlockSpec((tmm, n), lambda i, k: (i, 0))],
        out_shape=[jax.ShapeDtypeStruct((m, n), BF16)],
        scratch_shapes=[pltpu.VMEM((tmm, n), F32)],
        args=(a, b))
    return out if comm is None else (out, comm_outs)


def input_projection(x, g, w_qkv_t, w_z_t, tm):
    t = x.shape[0]

    def body(x_ref, g_ref, wq_ref, wz_ref, qkv_ref, z_ref, h_ref):
        xx = x_ref[...]
        h = ((xx * _rstd(xx)) * g_ref[...]).astype(BF16)
        h_ref[...] = h
        qkv_ref[...] = _dot_nt(h, wq_ref[...])
        z_ref[...] = _dot_nt(h, wz_ref[...])

    row = lambda n: pl.BlockSpec((tm, n), lambda i: (i, 0))
    return pl.pallas_call(
        functools.partial(body), name="input_projection", grid=(t // tm,),
        in_specs=[row(D_MODEL), _full((1, D_MODEL)), _full((D_QKV, D_MODEL)), _full((2 * D_SGU, D_MODEL))],
        out_specs=[row(D_QKV), row(2 * D_SGU), row(D_MODEL)],
        out_shape=[jax.ShapeDtypeStruct((t, D_QKV), F32), jax.ShapeDtypeStruct((t, 2 * D_SGU), F32),
                   jax.ShapeDtypeStruct((t, D_MODEL), BF16)],
        compiler_params=_params(1))(x, g, w_qkv_t, w_z_t)


def _shift_rows(shape, first, second):
    row = lax.broadcasted_iota(jnp.int32, shape, len(shape) - 2)
    return jnp.where(row == 0, first, jnp.where(row == 1, second, 0.0))


def _hi_lo(a):
    hi = a.astype(BF16).astype(F32)
    return hi, a - hi


def _head_tile_spec(tm, rows):
    return pl.BlockSpec((N_HEADS, None, rows, tm), lambda i: (0, i, 0, 0))


def _to_head_tiles(a):
    return a.T.reshape(N_HEADS, HEAD_DIM, a.shape[0])


def _from_head_tiles(a):
    return a.reshape(D_ATTN, a.shape[-1]).T


def _head_mean_matrix(width):
    head = jnp.arange(width) // HEAD_DIM
    return (head[:, None] == head[None, :]).astype(F32) / HEAD_DIM


def _kv_tile_spec(n_sub, rows, cols):
    return pl.BlockSpec((N_KV_HEADS, n_sub, rows, cols), lambda i: (0, i, 0, 0))


def qk_prep(qkv, gq_w, gk_w, cos_w, sin_w, mean_q, mean_k, tm, tk, tk_v):
    t = qkv.shape[0]
    n_sub, n_sub_v = tm // tk, tm // tk_v

    def body(p_ref, gq_ref, gk_ref, cos_ref, sin_ref, mq_ref, mk_ref, q_ref, k_ref, kt_ref, vt_ref, vtb_ref,
             qmax_ref, kmax_ref):
        @pl.when(pl.program_id(0) == 0)
        def _():
            qmax_ref[...] = jnp.zeros_like(qmax_ref)
            kmax_ref[...] = jnp.zeros_like(kmax_ref)

        cos2, sin2 = cos_ref[...], sin_ref[...]
        q = p_ref[:, :D_ATTN]
        k = p_ref[:, D_ATTN:D_ATTN + D_KV]
        qn = q * lax.rsqrt(_dot_split(q * q, mq_ref[...]) + EPS) * gq_ref[...]
        kn = k * lax.rsqrt(_dot_split(k * k, mk_ref[...]) + EPS) * gk_ref[...]
        cos8, sin8 = _tile_lanes(cos2, D_ATTN // LANES), _tile_lanes(sin2, D_ATTN // LANES)
        q_rot = (qn * cos8 + _pair_swap(qn) * sin8) * Q_SCALE
        q_ref[...] = _to_head_tiles(q_rot).astype(BF16)
        k_rot = kn * cos2 + _pair_swap(kn) * sin2
        q_sq = HEAD_DIM * _dot_split(q_rot * q_rot, mq_ref[...])
        k_sq = HEAD_DIM * _dot_split(k_rot * k_rot, mk_ref[...])
        qmax_ref[...] = jnp.maximum(qmax_ref[...], jnp.max(q_sq, axis=0, keepdims=True))
        kmax_ref[...] = jnp.maximum(kmax_ref[...], jnp.max(k_sq, axis=0, keepdims=True))
        vv = p_ref[:, D_ATTN + D_KV:]
        second = pltpu.roll(k_rot, HEAD_DIM, 1)
        for c in range(n_sub):
            rows = slice(c * tk, (c + 1) * tk)
            k_ref[0, c] = k_rot[rows, :HEAD_DIM].astype(BF16)
            k_ref[1, c] = second[rows, :HEAD_DIM].astype(BF16)
        for a, feat_ref, width, n in ((k_rot, kt_ref, tk, n_sub), (vv, vtb_ref, tk, n_sub), (vv, vt_ref, tk_v, n_sub_v)):
            for c in range(n):
                tile = a[c * width:(c + 1) * width].T.reshape(N_KV_HEADS, HEAD_DIM, width)
                feat_ref[:, c, :HEAD_DIM, :] = tile.astype(BF16)
        vt_ref[:, :, HEAD_DIM:, :] = jnp.ones((N_KV_HEADS, n_sub_v, ONES_ROWS, tk_v), BF16)
        minus = _shift_rows((N_KV_HEADS, n_sub, HEAD_DIM, tk), -1.0, -1.0).astype(BF16)
        kt_ref[:, :, HEAD_DIM:, :] = minus
        vtb_ref[:, :, HEAD_DIM:, :] = minus

    kv = lambda rows, cols: jax.ShapeDtypeStruct((N_KV_HEADS, t // tk, rows, cols), BF16)
    return pl.pallas_call(
        functools.partial(body), name="qk_prep", grid=(t // tm,),
        in_specs=[pl.BlockSpec((tm, D_QKV), lambda i: (i, 0)), _full((1, D_ATTN)), _full((1, D_KV)),
                  pl.BlockSpec((tm, LANES), lambda i: (i, 0)), pl.BlockSpec((tm, LANES), lambda i: (i, 0)),
                  _full((D_ATTN, D_ATTN)), _full((D_KV, D_KV))],
        out_specs=[_head_tile_spec(tm, HEAD_DIM), _kv_tile_spec(n_sub, tk, HEAD_DIM),
                   _kv_tile_spec(n_sub, 2 * HEAD_DIM, tk),
                   _kv_tile_spec(n_sub_v, HEAD_DIM + ONES_ROWS, tk_v), _kv_tile_spec(n_sub, 2 * HEAD_DIM, tk),
                   _full((1, D_ATTN)), _full((1, D_KV))],
        out_shape=[jax.ShapeDtypeStruct((N_HEADS, t // tm, HEAD_DIM, tm), BF16), kv(tk, HEAD_DIM), kv(2 * HEAD_DIM, tk),
                   jax.ShapeDtypeStruct((N_KV_HEADS, t // tk_v, HEAD_DIM + ONES_ROWS, tk_v), BF16),
                   kv(2 * HEAD_DIM, tk),
                   jax.ShapeDtypeStruct((1, D_ATTN), F32), jax.ShapeDtypeStruct((1, D_KV), F32)],
        compiler_params=_params(1),
    )(qkv, gq_w, gk_w, cos_w, sin_w, mean_q, mean_k)


def qk_bwd(dq_rot, dk_rot, dv, qkv, gq_w, gk_w, cos_w, sin_w, mean_q, mean_k, tm):
    t = qkv.shape[0]
    tk = dk_rot.shape[-1]
    n_sub = tm // tk

    def token_major(ref):
        return jnp.concatenate([ref[:, c].reshape(D_KV, tk).T for c in range(n_sub)], axis=0)

    def branch(raw, d_rot, gain, mean_mat, cos, sin, scale):
        r = lax.rsqrt(_dot_split(raw * raw, mean_mat) + EPS)
        n = raw * r
        dy = (d_rot * cos - _pair_swap(d_rot) * sin) * scale
        dn = dy * gain
        return r * (dn - n * _dot_split(dn * n, mean_mat)), dy * n

    def body(dq_ref, dk_ref, dv_ref, p_ref, gq_ref, gk_ref, cos_ref, sin_ref, mq_ref, mk_ref,
             dp_ref, dgq_ref, dgk_ref):
        cos2, sin2 = cos_ref[...], sin_ref[...]
        cos8, sin8 = _tile_lanes(cos2, D_ATTN // LANES), _tile_lanes(sin2, D_ATTN // LANES)
        dq, dgq = branch(p_ref[:, :D_ATTN], _from_head_tiles(dq_ref[...]), gq_ref[...], mq_ref[...], cos8, sin8,
                         HEAD_DIM ** -0.5)
        dk, dgk = branch(p_ref[:, D_ATTN:D_ATTN + D_KV], token_major(dk_ref), gk_ref[...], mk_ref[...], cos2, sin2, 1.0)
        dp_ref[...] = jnp.concatenate([dq, dk, token_major(dv_ref)], axis=-1).astype(BF16)

        @pl.when(pl.program_id(0) == 0)
        def _():
            dgq_ref[...] = jnp.zeros_like(dgq_ref)
            dgk_ref[...] = jnp.zeros_like(dgk_ref)

        dgq_ref[...] += _colsum(dgq)
        dgk_ref[...] += _colsum(dgk)

    return pl.pallas_call(
        functools.partial(body), name="qk_bwd", grid=(t // tm,),
        in_specs=[_head_tile_spec(tm, HEAD_DIM), _kv_tile_spec(n_sub, HEAD_DIM, tk),
                  _kv_tile_spec(n_sub, HEAD_DIM, tk), pl.BlockSpec((tm, D_QKV), lambda i: (i, 0)),
                  _full((1, D_ATTN)), _full((1, D_KV)),
                  pl.BlockSpec((tm, LANES), lambda i: (i, 0)), pl.BlockSpec((tm, LANES), lambda i: (i, 0)),
                  _full((D_ATTN, D_ATTN)), _full((D_KV, D_KV))],
        out_specs=[pl.BlockSpec((tm, D_QKV), lambda i: (i, 0)), _full((1, D_ATTN)), _full((1, D_KV))],
        out_shape=[jax.ShapeDtypeStruct((t, D_QKV), BF16), jax.ShapeDtypeStruct((1, D_ATTN), F32),
                   jax.ShapeDtypeStruct((1, D_KV), F32)],
        compiler_params=_params(1),
    )(dq_rot, dk_rot, dv, qkv, gq_w, gk_w, cos_w, sin_w, mean_q, mean_k)


def attention_fwd(bound, q_t, k, v_t, comm=None):
    _, nq, _, tq = q_t.shape
    _, nk, tk, _ = k.shape

    def body(bound_ref, q_ref, k_ref, v_ref, o_ref, qtok_ref, s_scr, p_scr):
        head_bound = bound_ref[pl.program_id(0)]
        safe = head_bound <= SAFE_SCORE_BOUND
        q = q_ref[...]
        s_scr[0] = _dot(k_ref[0], q)
        p_scr[1] = jnp.zeros((tk, tq), BF16)
        zero = jnp.zeros((HEAD_DIM + ONES_ROWS, tq), F32)

        def matmuls(j, slot):
            pv = _dot(v_ref[jnp.maximum(j - 1, 0)], p_scr[1 - slot])
            s_scr[1 - slot] = _dot(k_ref[jnp.minimum(j + 1, nk - 1)], q)
            return pv

        def finish(m, acc):
            acc = acc + _dot(v_ref[nk - 1], p_scr[(nk - 1) % 2])
            l = acc[HEAD_DIM:HEAD_DIM + 1]
            o_ref[...] = acc[:HEAD_DIM] / l
            lse_rows = _shift_rows((HEAD_DIM, tq), *_hi_lo(m + jnp.log2(l)))
            qtok_ref[...] = jnp.concatenate([q.astype(F32), lse_rows], axis=0).T.astype(BF16)

        @pl.when(safe)
        def _():
            m = jnp.full((1, tq), head_bound, F32)

            def step(j, slot, acc):
                s = s_scr[slot]
                pv = matmuls(j, slot)
                p_scr[slot] = jnp.exp2(s - m).astype(BF16)
                return acc + pv

            finish(m, _loop_pairs(nk, step, zero, 16 if nk % 16 == 0 else 2))

        @pl.when(jnp.logical_not(safe))
        def _():
            def step(j, slot, carry):
                m, acc = carry
                s = s_scr[slot]
                pv = matmuls(j, slot)
                m_new = jnp.maximum(m, jnp.max(s, axis=0, keepdims=True))
                p_scr[slot] = jnp.exp2(s - m_new).astype(BF16)
                return m_new, jnp.exp2(m - m_new) * (acc + pv)

            finish(*_loop_pairs(nk, step, (jnp.full((1, tq), -1e30, F32), zero)))

    return _pallas(
        comm, body, name="attention_fwd", grid=(N_HEADS, nq),
        in_specs=[pl.BlockSpec(memory_space=pltpu.SMEM),
                  pl.BlockSpec((None, None, HEAD_DIM, tq), lambda h, i: (h, i, 0, 0)),
                  pl.BlockSpec((None, nk, tk, HEAD_DIM), lambda h, i: (h // KV_GROUP, 0, 0, 0)),
                  pl.BlockSpec((None, nk, HEAD_DIM + ONES_ROWS, tk), lambda h, i: (h // KV_GROUP, 0, 0, 0))],
        out_specs=[pl.BlockSpec((None, None, HEAD_DIM, tq), lambda h, i: (h, i, 0, 0)),
                   pl.BlockSpec((None, None, tq, 2 * HEAD_DIM), lambda h, i: (h, i, 0, 0))],
        out_shape=[jax.ShapeDtypeStruct((N_HEADS, nq, HEAD_DIM, tq), F32),
                   jax.ShapeDtypeStruct((N_HEADS, nq, tq, 2 * HEAD_DIM), BF16)],
        scratch_shapes=[pltpu.VMEM((2, tk, tq), F32), pltpu.VMEM((2, tk, tq), BF16)],
        args=(bound, q_t, k, v_t))


def attention_bwd(q_tok, do_tok, q_t, do_t, k_t, v_t, comm=None):
    _, nq, _, tq = q_t.shape
    _, nk, _, tk = k_t.shape

    def body(qtok_ref, dotok_ref, q_ref, do_ref, kt_ref, vt_ref, dq_ref, dk_ref, dv_ref,
             s_scr, dp_scr, p_scr, ds_scr):
        @pl.when(pl.program_id(1) == 0)
        def _():
            dq_ref[...] = jnp.zeros_like(dq_ref)

        kt_aug, vt_aug = kt_ref[...], vt_ref[...]
        kt = kt_aug[:HEAD_DIM]
        n = KV_GROUP * nq
        s_scr[0] = _dot(qtok_ref[0, 0], kt_aug)
        dp_scr[0] = _dot(dotok_ref[0, 0], vt_aug)
        p_scr[1] = jnp.zeros((tq, tk), BF16)
        ds_scr[1] = jnp.zeros((tq, tk), BF16)

        def products(t, slot, dk, dv):
            h, i = t // nq, t % nq
            ds = ds_scr[slot]
            dq_ref[h, i] += _dot_nt(kt, ds)
            return dk + _dot(q_ref[h, i], ds), dv + _dot(do_ref[h, i], p_scr[slot])

        def step(t, slot, carry):
            s, dp = s_scr[slot], dp_scr[slot]
            dk, dv = products(jnp.maximum(t - 1, 0), 1 - slot, *carry)
            nxt = jnp.minimum(t + 1, n - 1)
            s_scr[1 - slot] = _dot(qtok_ref[nxt // nq, nxt % nq], kt_aug)
            dp_scr[1 - slot] = _dot(dotok_ref[nxt // nq, nxt % nq], vt_aug)
            p = jnp.exp2(s)
            p_scr[slot] = p.astype(BF16)
            ds_scr[slot] = (p * dp).astype(BF16)
            return dk, dv

        zero = jnp.zeros((HEAD_DIM, tk), F32)
        dk, dv = products(n - 1, (n - 1) % 2, *_loop_pairs(n, step, (zero, zero), 8 if n % 8 == 0 else 2))
        dk_ref[...] = dk * (1.0 / LOG2_E)
        dv_ref[...] = dv

    group = lambda g, j: (g, 0, 0, 0)
    tile = lambda g, j: (g, j, 0, 0)
    once = pl.Buffered(1)
    return _pallas(
        comm, body, name="attention_bwd", grid=(N_KV_HEADS, nk),
        in_specs=[pl.BlockSpec((KV_GROUP, nq, tq, 2 * HEAD_DIM), group, pipeline_mode=once),
                  pl.BlockSpec((KV_GROUP, nq, tq, 2 * HEAD_DIM), group, pipeline_mode=once),
                  pl.BlockSpec((KV_GROUP, nq, HEAD_DIM, tq), group, pipeline_mode=once),
                  pl.BlockSpec((KV_GROUP, nq, HEAD_DIM, tq), group, pipeline_mode=once),
                  pl.BlockSpec((None, None, 2 * HEAD_DIM, tk), tile),
                  pl.BlockSpec((None, None, 2 * HEAD_DIM, tk), tile)],
        out_specs=[pl.BlockSpec((KV_GROUP, nq, HEAD_DIM, tq), group),
                   pl.BlockSpec((None, None, HEAD_DIM, tk), tile),
                   pl.BlockSpec((None, None, HEAD_DIM, tk), tile)],
        out_shape=[jax.ShapeDtypeStruct((N_HEADS, nq, HEAD_DIM, tq), F32),
                   jax.ShapeDtypeStruct((N_KV_HEADS, nk, HEAD_DIM, tk), F32),
                   jax.ShapeDtypeStruct((N_KV_HEADS, nk, HEAD_DIM, tk), F32)],
        scratch_shapes=[pltpu.VMEM((2, tq, tk), F32), pltpu.VMEM((2, tq, tk), F32),
                        pltpu.VMEM((2, tq, tk), BF16), pltpu.VMEM((2, tq, tk), BF16)],
        args=(q_tok, do_tok, q_t, do_t, k_t, v_t))


def _group_matmul(a_t, w_ref):
    return jnp.concatenate([_dot(a_t[g * SGU_GROUP_DIM:(g + 1) * SGU_GROUP_DIM], w_ref[g])
                            for g in range(N_SGU_GROUPS)], axis=0)


def _gate_forward(z, g_sgu, wst_ref, bias):
    gz, th = _gelu(z)
    u, vv = gz[:, :D_SGU], gz[:, D_SGU:]
    rv = _rstd(vv)
    nv = vv * rv
    vn = nv * g_sgu
    v_chunks, fs = [], []
    for c in range(z.shape[0] // CHUNK):
        vt = vn[c * CHUNK:(c + 1) * CHUNK].T.astype(BF16)
        v_chunks.append(vt)
        fs.append(_group_matmul(vt, wst_ref).T + bias)
    f = jnp.concatenate(fs, axis=0) if len(fs) > 1 else fs[0]
    return th, u, rv, nv, v_chunks, f


def mix_out(z, o, x, g_sgu, g_ao, g_so, ws_t, bias, w_out, tm):
    t = x.shape[0]

    def body(z_ref, o_ref, x_ref, gs_ref, gao_ref, gso_ref, ws_ref, bias_ref, wout_ref, x2_ref, mixed_ref):
        _, u, _, _, _, f = _gate_forward(z_ref[...], gs_ref[...], ws_ref, bias_ref[...])
        sgu = u * f
        oo = _from_head_tiles(o_ref[...])
        mixed = jnp.concatenate([oo * _rstd(oo) * gao_ref[...], sgu * _rstd(sgu) * gso_ref[...]], axis=-1).astype(BF16)
        mixed_ref[...] = mixed
        x2_ref[...] = x_ref[...] + _dot(mixed, wout_ref[...])

    row = lambda n: pl.BlockSpec((tm, n), lambda i: (i, 0))
    return pl.pallas_call(
        functools.partial(body), name="mix_out", grid=(t // tm,),
        in_specs=[row(2 * D_SGU), _head_tile_spec(tm, HEAD_DIM), row(D_MODEL), _full((1, D_SGU)), _full((1, D_ATTN)),
                  _full((1, D_SGU)),
                  _full((N_SGU_GROUPS, CHUNK, CHUNK)), _full((CHUNK, D_SGU)), _full((D_MODEL, D_MODEL))],
        out_specs=[row(D_MODEL), row(D_MODEL)],
        out_shape=[jax.ShapeDtypeStruct((t, D_MODEL), F32), jax.ShapeDtypeStruct((t, D_MODEL), BF16)],
        compiler_params=_params(1),
    )(z, o, x, g_sgu, g_ao, g_so, ws_t, bias, w_out)


def mix_bwd(dx2, z, o, g_sgu, g_ao, g_so, ws, ws_t, bias, w_out, group_ind, tm):
    t = dx2.shape[0]
    n_tiles = t // tm

    def body(dx_ref, z_ref, o_ref, gs_ref, gao_ref, gso_ref, ws_ref, wst_ref, bias_ref, wout_ref, ind_ref,
             do_ref, dotok_ref, dz_ref, dg_ref, dws_ref, dbs_ref, df_sum):
        step = pl.program_id(0)

        @pl.when(step == 0)
        def _():
            dg_ref[...] = jnp.zeros_like(dg_ref)
            dws_ref[...] = jnp.zeros_like(dws_ref)
            df_sum[...] = jnp.zeros_like(df_sum)

        z = z_ref[...]
        th, u, rv, nv, v_chunks, f = _gate_forward(z, gs_ref[...], wst_ref, bias_ref[...])
        dmixed = _dot_nt(dx_ref[...].astype(BF16), wout_ref[...])
        o_tiles = o_ref[...]
        oo = _from_head_tiles(o_tiles)
        ro = _rstd(oo)
        d_o, dgao = _rms_bwd(dmixed[:, :D_ATTN], oo * ro, ro, gao_ref[...])
        do_tiles = _to_head_tiles(d_o)
        do_ref[...] = do_tiles.astype(BF16)
        delta_hi, delta_lo = _hi_lo(jnp.sum(do_tiles * o_tiles, axis=1, keepdims=True))
        for h in range(N_HEADS):
            delta_rows = _shift_rows((HEAD_DIM, tm), delta_hi[h], delta_lo[h])
            dotok_ref[h] = jnp.concatenate([do_tiles[h], delta_rows], axis=0).T.astype(BF16)
        sgu = u * f
        rs = _rstd(sgu)
        dsgu, dgso = _rms_bwd(dmixed[:, D_ATTN:], sgu * rs, rs, gso_ref[...])
        du = dsgu * f
        df = dsgu * u
        dvns = []
        df_acc = jnp.zeros((CHUNK, D_SGU), F32)
        for c in range(tm // CHUNK):
            dfc32 = df[c * CHUNK:(c + 1) * CHUNK]
            dft = dfc32.T.astype(BF16)
            dvns.append(_group_matmul(dft, ws_ref).T)
            for g in range(N_SGU_GROUPS):
                rows = slice(g * SGU_GROUP_DIM, (g + 1) * SGU_GROUP_DIM)
                dws_ref[g] += _dot_tn(dft[rows], v_chunks[c][rows])
            df_acc = df_acc + dfc32
        df_sum[...] += df_acc
        dvn = jnp.concatenate(dvns, axis=0) if len(dvns) > 1 else dvns[0]
        dvv, dgs = _rms_bwd(dvn, nv, rv, gs_ref[...])
        dz_ref[...] = (jnp.concatenate([du, dvv], axis=-1) * _gelu_grad(z, th)).astype(BF16)
        dg_ref[0:1, :] += _colsum(dgao)
        dg_ref[1:2, :] += _colsum(dgso)
        dg_ref[2:3, :] += _colsum(dgs)

        @pl.when(step == n_tiles - 1)
        def _():
            dbs_ref[...] = _dot_f32(df_sum[...], ind_ref[...])

    row = lambda n: pl.BlockSpec((tm, n), lambda i: (i, 0))
    return pl.pallas_call(
        functools.partial(body), name="mix_bwd", grid=(n_tiles,),
        in_specs=[row(D_MODEL), row(2 * D_SGU), _head_tile_spec(tm, HEAD_DIM), _full((1, D_SGU)), _full((1, D_ATTN)),
                  _full((1, D_SGU)),
                  _full((N_SGU_GROUPS, CHUNK, CHUNK)), _full((N_SGU_GROUPS, CHUNK, CHUNK)), _full((CHUNK, D_SGU)),
                  _full((D_MODEL, D_MODEL)), _full((D_SGU, LANES))],
        out_specs=[_head_tile_spec(tm, HEAD_DIM), pl.BlockSpec((N_HEADS, None, tm, 2 * HEAD_DIM), lambda i: (0, i, 0, 0)),
                   row(2 * D_SGU), _full((8, D_SGU)),
                   _full((N_SGU_GROUPS, CHUNK, CHUNK)), _full((CHUNK, LANES))],
        out_shape=[jax.ShapeDtypeStruct((N_HEADS, n_tiles, HEAD_DIM, tm), BF16),
                   jax.ShapeDtypeStruct((N_HEADS, n_tiles, tm, 2 * HEAD_DIM), BF16),
                   jax.ShapeDtypeStruct((t, 2 * D_SGU), BF16),
                   jax.ShapeDtypeStruct((8, D_SGU), F32),
                   jax.ShapeDtypeStruct((N_SGU_GROUPS, CHUNK, CHUNK), F32),
                   jax.ShapeDtypeStruct((CHUNK, LANES), F32)],
        scratch_shapes=[pltpu.VMEM((CHUNK, D_SGU), F32)],
        compiler_params=_params(1),
    )(dx2, z, o, g_sgu, g_ao, g_so, ws, ws_t, bias, w_out, group_ind)


def ffn_down_loss(act, wd, x, g, target, tm):
    t = x.shape[0]

    def body(act_ref, wd_ref, x_ref, g_ref, t_ref, loss_ref, dx_ref, dg_ref):
        @pl.when(pl.program_id(0) == 0)
        def _():
            loss_ref[...] = jnp.zeros_like(loss_ref)
            dg_ref[...] = jnp.zeros_like(dg_ref)

        xx = x_ref[...] + 0.5 * _dot(act_ref[...], wd_ref[...])
        r = _rstd(xx)
        n = xx * r
        err = n * g_ref[...] - t_ref[...]
        per_token = jnp.mean(err * err, axis=-1, keepdims=True)
        loss_ref[...] += 0.5 * jnp.sum(per_token, axis=0, keepdims=True)
        dx, dg_rows = _rms_bwd(err * (1.0 / D_MODEL), n, r, g_ref[...])
        dx_ref[...] = dx
        dg_ref[...] += _colsum(dg_rows)

    row = pl.BlockSpec((tm, D_MODEL), lambda i: (i, 0))
    return pl.pallas_call(
        functools.partial(body), name="ffn_down_loss", grid=(t // tm,),
        in_specs=[pl.BlockSpec((tm, D_FF), lambda i: (i, 0)), _full((D_FF, D_MODEL)), row, _full((1, D_MODEL)), row],
        out_specs=[_full((1, LANES)), row, _full((1, D_MODEL))],
        out_shape=[jax.ShapeDtypeStruct((1, LANES), F32), jax.ShapeDtypeStruct((t, D_MODEL), F32),
                   jax.ShapeDtypeStruct((1, D_MODEL), F32)],
        compiler_params=_params(1),
    )(act, wd, x, g, target)


def _rope_tables(t):
    rows = t // GRID_W
    row_idx = jnp.repeat(jnp.arange(rows, dtype=F32), GRID_W)
    col_idx = jnp.tile(jnp.arange(GRID_W, dtype=F32), rows)
    axis_dim = HEAD_DIM // 2
    inv = 1.0 / (ROPE_THETA ** (jnp.arange(0, axis_dim, 2, dtype=F32) / axis_dim))
    ang = jnp.concatenate([row_idx[:, None] * inv, col_idx[:, None] * inv], axis=-1)
    cos = jnp.repeat(jnp.cos(ang), 2, axis=-1)
    sin = jnp.repeat(jnp.sin(ang), 2, axis=-1) * jnp.tile(jnp.array([-1.0, 1.0], F32), HEAD_DIM // 2)
    return jnp.tile(cos, (1, LANES // HEAD_DIM)), jnp.tile(sin, (1, LANES // HEAD_DIM))


def kernel(x, g_ffn1, w1_gate, w1_up, w1_down, g_mix, w_in, g_q, g_k, g_sgu, w_s, b_s, g_attn_out, g_sgu_out, w_out, g_ffn2, w2_gate, w2_up, w2_down, g_final, loss_target, m_g_ffn1, m_w1_gate, m_w1_up, m_w1_down, m_g_mix, m_w_in, m_g_q, m_g_k, m_g_sgu, m_w_s, m_b_s, m_g_attn_out, m_g_sgu_out, m_w_out, m_g_ffn2, m_w2_gate, m_w2_up, m_w2_down, m_g_final, v_g_ffn1, v_w1_gate, v_w1_up, v_w1_down, v_g_mix, v_w_in, v_g_q, v_g_k, v_g_sgu, v_w_s, v_b_s, v_g_attn_out, v_g_sgu_out, v_w_out, v_g_ffn2, v_w2_gate, v_w2_up, v_w2_down, v_g_final):
    weights = dict(g_ffn1=g_ffn1, w1_gate=w1_gate, w1_up=w1_up, w1_down=w1_down, g_mix=g_mix, w_in=w_in, g_q=g_q,
                   g_k=g_k, g_sgu=g_sgu, w_s=w_s, b_s=b_s, g_attn_out=g_attn_out, g_sgu_out=g_sgu_out, w_out=w_out,
                   g_ffn2=g_ffn2, w2_gate=w2_gate, w2_up=w2_up, w2_down=w2_down, g_final=g_final)
    m_in = dict(g_ffn1=m_g_ffn1, w1_gate=m_w1_gate, w1_up=m_w1_up, w1_down=m_w1_down, g_mix=m_g_mix, w_in=m_w_in,
                g_q=m_g_q, g_k=m_g_k, g_sgu=m_g_sgu, w_s=m_w_s, b_s=m_b_s, g_attn_out=m_g_attn_out,
                g_sgu_out=m_g_sgu_out, w_out=m_w_out, g_ffn2=m_g_ffn2, w2_gate=m_w2_gate, w2_up=m_w2_up,
                w2_down=m_w2_down, g_final=m_g_final)
    v_in = dict(g_ffn1=v_g_ffn1, w1_gate=v_w1_gate, w1_up=v_w1_up, w1_down=v_w1_down, g_mix=v_g_mix, w_in=v_w_in,
                g_q=v_g_q, g_k=v_g_k, g_sgu=v_g_sgu, w_s=v_w_s, b_s=v_b_s, g_attn_out=v_g_attn_out,
                g_sgu_out=v_g_sgu_out, w_out=v_w_out, g_ffn2=v_g_ffn2, w2_gate=v_w2_gate, w2_up=v_w2_up,
                w2_down=v_w2_down, g_final=v_g_final)
    names = list(weights)

    t = x.shape[1]
    x0 = x[0]
    target = loss_target[0]
    tm = min(512, t)
    tm_ff = min(256, t)
    tn_ff = 256
    tq = min(512, t)
    tk = min(256, t)
    tk_fwd = min(512, t)
    tk_w = min(2048, t)

    def shard_rows(name):
        w = weights[name][0]
        return (w.T if name in TRANSPOSED else w).astype(BF16)

    rows_of = dict(SHARD_ROWS)
    full = {}

    def packed(group):
        return jnp.concatenate([shard_rows(n) for n in group], axis=0), [rows_of[n] for n in group]

    def gather_of(group):
        return gather_exchange(*packed(group))

    def take(group, gathered):
        for n, g in zip(group, gathered):
            full[n] = g.reshape(N_DEV * rows_of[n], D_MODEL)

    first, second, third = ("w1_gate", "w1_up"), ("w1_down", "w_in", "w_out"), ("w2_gate", "w2_up", "w2_down")
    take(first, gather_two_level(*packed(first), "gather_first"))

    (h1, a1, b1, act1), gathered = ffn_up(x0, g_ffn1, full["w1_gate"], full["w1_up"], tm_ff, tn_ff, gather_of(second))
    take(second, gathered)
    w_in_t = full["w_in"]
    w_qkv_t, w_z_t = w_in_t[:D_QKV], w_in_t[D_QKV:]
    x1 = ffn_down(act1, full["w1_down"], x0, tm)

    qkv, z, h2 = input_projection(x1, g_mix, w_qkv_t, w_z_t, tm)
    cos_w, sin_w = _rope_tables(t)
    gq_w = jnp.tile(g_q, (1, N_HEADS))
    gk_w = jnp.tile(g_k, (1, N_KV_HEADS))
    mean_q, mean_k = _head_mean_matrix(D_ATTN).astype(BF16), _head_mean_matrix(D_KV).astype(BF16)
    q_t, k_tiles, kt_tiles, vt_tiles, vt_tiles_bwd, q_sq_max, k_sq_max = qk_prep(
        qkv, gq_w, gk_w, cos_w, sin_w, mean_q, mean_k, tq, tk, tk_fwd)
    score_bound = 1.02 * jnp.sqrt(q_sq_max.reshape(N_HEADS, HEAD_DIM)[:, 0]
                                  * jnp.repeat(k_sq_max.reshape(N_KV_HEADS, HEAD_DIM)[:, 0], KV_GROUP))
    k_tiles_fwd = k_tiles.reshape(N_KV_HEADS, t // tk_fwd, tk_fwd, HEAD_DIM)
    (o_t, q_tok), gathered = attention_fwd(score_bound, q_t, k_tiles_fwd, vt_tiles, gather_of(third))
    take(third, gathered)

    ws_b = w_s[0].astype(BF16)
    ws_tb = jnp.swapaxes(w_s[0], 1, 2).astype(BF16)
    bias = jnp.repeat(b_s[0].T, SGU_GROUP_DIM, axis=1)
    x2, mixed = mix_out(z, o_t, x1, g_sgu, g_attn_out, g_sgu_out, ws_tb, bias, full["w_out"], tq)

    (h3, a2, b2, act2), _ = ffn_up(x2, g_ffn2, full["w2_gate"], full["w2_up"], tm_ff, tn_ff)

    loss_part, dx3, dg_final = ffn_down_loss(act2, full["w2_down"], x2, g_final, target, tm)

    tmm = D_FF // 2
    (da2, db2), _ = ffn_bwd_act(dx3, full["w2_down"], a2, b2, tm_ff, tn_ff)
    (dx2, dg_ffn2), _ = norm_bwd_matmul(da2, full["w2_gate"], db2, full["w2_up"], x2, g_ffn2, dx3, tm)
    dwg2, dwu2 = matmul_tn(da2, h3, 1.0, tmm, tk_w), matmul_tn(db2, h3, 1.0, tmm, tk_w)
    dwd2 = matmul_tn(act2, dx3, 0.5, tmm, tk_w)

    group_ind = (jnp.arange(D_SGU)[:, None] // SGU_GROUP_DIM == jnp.arange(LANES)[None, :]).astype(F32)
    do_t, do_tok, dz, dg_mixrow, dws, dbs = mix_bwd(dx2, z, o_t, g_sgu, g_attn_out, g_sgu_out, ws_b, ws_tb, bias,
                                                    full["w_out"], group_ind, tq)
    dw_out = matmul_tn(mixed, dx2, 1.0, D_MODEL // 2, tk_w)

    group_a = ("w2_gate", "w2_up", "w2_down", "w_out")
    (dq_t, dk_t, dv_t), (parts_a,) = attention_bwd(q_tok, do_tok, q_t, do_t, kt_tiles, vt_tiles_bwd,
                                                   scatter_exchange([dwg2, dwu2, dwd2, dw_out]))
    dqkv, dgq_w, dgk_w = qk_bwd(dq_t, dk_t, dv_t, qkv, gq_w, gk_w, cos_w, sin_w, mean_q, mean_k, tq)

    def pack_small(arrays):
        pieces = []
        for a in arrays:
            flat = a.reshape(-1)
            pieces.append(jnp.pad(flat, (0, (-flat.shape[0]) % (8 * LANES))).reshape(-1, LANES))
        return jnp.concatenate(pieces, axis=0), [p.shape[0] for p in pieces]

    early = dict(g_ffn2=dg_ffn2, g_final=dg_final, g_q=dgq_w.reshape(N_HEADS, HEAD_DIM).sum(0),
                 g_k=dgk_w.reshape(N_KV_HEADS, HEAD_DIM).sum(0), g_attn_out=dg_mixrow[0], g_sgu_out=dg_mixrow[1],
                 g_sgu=dg_mixrow[2], w_s=dws, b_s=dbs[:, :N_SGU_GROUPS].T)
    early_pack, early_rows = pack_small(list(early.values()))
    (dx1, dg_mix), (early_parts,) = norm_bwd_matmul(dqkv, w_qkv_t, dz, w_z_t, x1, g_mix, dx2, tm,
                                                    gather_exchange(early_pack, [early_pack.shape[0]]))
    dw_in = jnp.concatenate([matmul_tn(dqkv, h2, 1.0, D_QKV // 2, tk_w), matmul_tn(dz, h2, 1.0, D_SGU, tk_w)], axis=0)

    dwd1, (parts_in,) = matmul_tn(act1, dx1, 0.5, tmm, tk_w, scatter_exchange([dw_in]))
    (da1, db1), (parts_d1,) = ffn_bwd_act(dx1, full["w1_down"], a1, b1, tm_ff, tn_ff, scatter_exchange([dwd1]))
    dwg1 = matmul_tn(da1, h1, 1.0, tmm, tk_w)
    dwu1, (parts_g1,) = matmul_tn(db1, h1, 1.0, tmm, tk_w, scatter_exchange([dwg1]))
    (dx0, dg_ffn1), (parts_u1,) = norm_bwd_matmul(da1, full["w1_gate"], db1, full["w1_up"], x0, g_ffn1, dx1, tm,
                                                  scatter_exchange([dwu1]))
    scattered = ((group_a, parts_a), (("w_in",), parts_in), (("w1_down",), parts_d1), (("w1_gate",), parts_g1),
                 (("w1_up",), parts_u1))

    late = dict(g_mix=dg_mix, g_ffn1=dg_ffn1, loss=loss_part)
    late_pack, late_rows = pack_small(list(late.values()))
    (late_parts,) = run_exchange(gather_exchange(late_pack, [late_pack.shape[0]]), "gather_late_small_grads")
    small_sums = {}
    for entries, rows, parts in ((early, early_rows, early_parts), (late, late_rows, late_parts)):
        summed = sum_parts(parts, parts.shape[1])
        off = 0
        for n, r in zip(entries, rows):
            small_sums[n] = summed[off:off + r]
            off += r
    loss = small_sums.pop("loss")[0, 0]

    grads, row_grads = {}, {}
    for group, parts in scattered:
        rows = parts.shape[1]
        summed = sum_parts(parts, rows if rows <= 2 * rows_of["w1_gate"] else rows // 2)
        off = 0
        for n in group:
            row_grads[n] = summed[off:off + rows_of[n]]
            grads[n] = (row_grads[n].T if n in TRANSPOSED else row_grads[n])[None]
            off += rows_of[n]
    for n, summed in small_sums.items():
        grads[n] = summed.reshape(-1)[:weights[n].size].reshape(weights[n].shape)

    delta_w, new_m, new_v = {}, {}, {}
    for n in names:
        shape = weights[n].shape
        if n in TRANSPOSED:
            view, unview, g = (lambda a: a[0].T), (lambda a: a.T[None]), row_grads[n]
        else:
            view, unview = (lambda a: a.reshape(-1, shape[-1])), (lambda a: a.reshape(shape))
            g = view(grads[n])
        d, m2, v2 = adamw(view(weights[n]), g, view(m_in[n]), view(v_in[n]))
        delta_w[n], new_m[n], new_v[n] = unview(d), unview(m2), unview(v2)

    return (loss, dx0[None], *[grads[n] for n in names], *[delta_w[n] for n in names],
            *[new_m[n] for n in names], *[new_v[n] for n in names])
```
